```python
import math
import jax
import jax.numpy as jnp
from jax import lax
import numpy as np

D_MODEL = 1024
BATCH = 4
SEQ = 4096
DEPTH = 2
DEC_BATCH = 16
DEC_SEQ = 16
PAST_LEN = 2048

CHUNK = 64
N_EVEN = (DEPTH + 1) // 2
N_ODD = DEPTH // 2
ML_HEADS = 8
ML_DK = 64
ML_DV = 64
SW_HEADS = 8
SW_KV_HEADS = 2
SW_HD = 64
SW_GROUP = SW_HEADS // SW_KV_HEADS
WINDOW = 128
WIN_CHUNKS = WINDOW // CHUNK
ROPE_THETA = 10000.0
GD_QK_HEADS = 8
GD_V_HEADS = 16
GD_HD = 128
GD_CONV = 4
GD_QK_W = GD_QK_HEADS * GD_HD
GD_V_W = GD_V_HEADS * GD_HD
GD_CONV_CH = 2 * GD_QK_W + GD_V_W
N_EXPERTS = 16
N_GROUPS = 4
EXP_PER_GROUP = N_EXPERTS // N_GROUPS
TOP_K = 2
D_EXPERT = 512
DN_ALPHA = (2 * DEPTH) ** 0.25
DN_BETA = (8 * DEPTH) ** -0.25
LN_EPS = 1e-5
RMS_EPS = 1e-6
AB_SIZES = (ML_HEADS * ML_DK, ML_HEADS * ML_DK, ML_HEADS * ML_DV, ML_HEADS * ML_DV, ML_HEADS, ML_HEADS, SW_HEADS * SW_HD, SW_KV_HEADS * SW_HD, SW_KV_HEADS * SW_HD)
AB_VALUE = (False, False, True, False, False, False, False, False, True)
AB_IN = sum(AB_SIZES)
AB_OUT = ML_HEADS * ML_DV + SW_HEADS * SW_HD
C_SIZES = (GD_QK_W, GD_QK_W, GD_V_W, GD_V_W, GD_V_HEADS, GD_V_HEADS)
C_VALUE = (False, False, True, False, False, False)
C_IN = sum(C_SIZES)
C_OUT = GD_V_W

kernel_name = 'hybrid_stream_mlstm_swa_gdn_moe_step'


def split_cols(z, sizes):
    out, start = [], 0
    for s in sizes:
        out.append(z[..., start:start + s])
        start += s
    return out


def layer_norm(x, g, b):
    xf = x.astype(jnp.float32)
    mu = xf.mean(-1, keepdims=True)
    var = jnp.square(xf - mu).mean(-1, keepdims=True)
    return ((xf - mu) * lax.rsqrt(var + LN_EPS) * g + b).astype(x.dtype)


def rms_norm(x, g):
    xf = x.astype(jnp.float32)
    return (xf * lax.rsqrt(jnp.mean(xf * xf, -1, keepdims=True) + RMS_EPS) * g).astype(x.dtype)


def l2norm(x):
    xf = x.astype(jnp.float32)
    return xf * lax.rsqrt(jnp.sum(xf * xf, -1, keepdims=True) + 1e-6)


def rope(x, pos):
    half = x.shape[-1] // 2
    inv = ROPE_THETA ** (-jnp.arange(half, dtype=jnp.float32) / half)
    ang = pos.astype(jnp.float32)[:, None] * inv[None, :]
    cos = jnp.cos(ang)[None, :, None, :]
    sin = jnp.sin(ang)[None, :, None, :]
    xf = x.astype(jnp.float32)
    x1, x2 = xf[..., :half], xf[..., half:]
    return jnp.concatenate([x1 * cos - x2 * sin, x2 * cos + x1 * sin], -1).astype(x.dtype)


def to_chunks(a, L):
    nc = a.shape[2] // L
    return jnp.moveaxis(a.reshape(a.shape[:2] + (nc, L) + a.shape[3:]), 2, 0)


def from_chunks(a):
    a = jnp.moveaxis(a, 0, 2)
    return a.reshape(a.shape[:2] + (a.shape[2] * a.shape[3],) + a.shape[4:])


def mlstm_chunked(q, k, v, ig, lf, C0, n0, m0, L):
    causal = jnp.tril(jnp.ones((L, L), bool))

    def step(carry, inp):
        C, n, m = carry
        qc, kc, vc, igc, lfc = inp
        b = jnp.cumsum(lfc, -1)
        D = jnp.where(causal, b[..., :, None] - b[..., None, :] + igc[..., None, :], -jnp.inf)
        inter = b + m[..., None]
        mt = jnp.maximum(inter, D.max(-1))
        a = jnp.exp(inter - mt)
        s = jnp.einsum('bhtd,bhsd->bhts', qc, kc) * jnp.exp(D - mt[..., None])
        num = a[..., None] * jnp.einsum('bhtd,bhde->bhte', qc, C) + jnp.einsum('bhts,bhse->bhte', s, vc)
        den = a * jnp.einsum('bhtd,bhd->bht', qc, n) + s.sum(-1)
        h = num / jnp.maximum(jnp.abs(den), jnp.exp(-mt))[..., None]
        m_new = mt[..., -1]
        wk = jnp.exp(b[..., -1:] - b + igc - m_new[..., None])
        dec = jnp.exp(b[..., -1] + m - m_new)
        C_new = dec[..., None, None] * C + jnp.einsum('bhs,bhsd,bhse->bhde', wk, kc, vc)
        n_new = dec[..., None] * n + jnp.einsum('bhs,bhsd->bhd', wk, kc)
        return (C_new, n_new, m_new), h

    xs = tuple(to_chunks(a, L) for a in (q, k, v, ig, lf))
    (C, n, m), h = lax.scan(step, (C0, n0, m0), xs)
    return from_chunks(h), C, n, m


def gdn_chunked(q, k, v, g, beta, S0, L):
    incl = jnp.tril(jnp.ones((L, L), bool))
    strict = jnp.tril(jnp.ones((L, L), bool), -1)
    eye = jnp.eye(L, dtype=jnp.float32)
    dv = v.shape[-1]

    def step(S, inp):
        qc, kc, vc, gc, bc = inp
        G = jnp.cumsum(gc, -1)
        decay = jnp.exp(jnp.where(incl, G[..., :, None] - G[..., None, :], -jnp.inf))
        A = jnp.where(strict, bc[..., :, None] * jnp.einsum('bhtd,bhsd->bhts', kc, kc) * decay, 0.0)
        rhs = jnp.concatenate([bc[..., None] * vc, (bc * jnp.exp(G))[..., None] * kc], -1)
        sol = lax.linalg.triangular_solve(A + eye, rhs, left_side=True, lower=True, unit_diagonal=True)
        wn = sol[..., :dv] - jnp.einsum('bhtd,bhde->bhte', sol[..., dv:], S)
        o = (jnp.exp(G)[..., None] * jnp.einsum('bhtd,bhde->bhte', qc, S)
             + jnp.einsum('bhts,bhse->bhte', jnp.einsum('bhtd,bhsd->bhts', qc, kc) * decay, wn))
        GL = G[..., -1:]
        S_new = jnp.exp(GL)[..., None] * S + jnp.einsum('bhs,bhsd,bhse->bhde', jnp.exp(GL - G), kc, wn)
        return S_new, o

    xs = tuple(to_chunks(a, L) for a in (q, k, v, g, beta))
    S, o = lax.scan(step, S0, xs)
    return from_chunks(o), S


def sink_softmax(s, sinks):
    col = jnp.broadcast_to(sinks, s.shape[:-1] + (1,))
    return jax.nn.softmax(jnp.concatenate([s, col], -1), axis=-1)[..., :-1]


def swa_prompt(q, k, v, sinks):
    B, T = q.shape[:2]
    nc = T // CHUNK
    qc = q.reshape(B, nc, CHUNK, SW_KV_HEADS, SW_GROUP, SW_HD)

    def band(a):
        ac = a.reshape(B, nc, CHUNK, SW_KV_HEADS, SW_HD)
        ap = jnp.concatenate([jnp.zeros((B, WIN_CHUNKS) + ac.shape[2:], a.dtype), ac], 1)
        return jnp.concatenate([ap[:, j:j + nc] for j in range(WIN_CHUNKS + 1)], 2)

    kb, vb = band(k), band(v)
    s = jnp.einsum('bnqhgd,bnkhd->bnhgqk', qc, kb).astype(jnp.float32) * SW_HD ** -0.5
    key_chunk = jnp.arange(nc)[:, None] + jnp.arange((WIN_CHUNKS + 1) * CHUNK)[None, :] // CHUNK - WIN_CHUNKS
    s = jnp.where((key_chunk >= 0)[None, :, None, None, None, :], s, -jnp.inf)
    p = sink_softmax(s, sinks.astype(jnp.float32).reshape(SW_KV_HEADS, SW_GROUP, 1, 1))
    o = jnp.einsum('bnhgqk,bnkhd->bnqhgd', p.astype(v.dtype), vb)
    return o.reshape(B, T, SW_HEADS * SW_HD)


def swa_sample(q, k, v, k_cache, v_cache, sinks):
    B, T = q.shape[:2]
    kk = jnp.concatenate([k_cache.astype(k.dtype), k], 1)
    vv = jnp.concatenate([v_cache.astype(v.dtype), v], 1)
    qg = q.reshape(B, T, SW_KV_HEADS, SW_GROUP, SW_HD)
    s = jnp.einsum('bqhgd,bkhd->bhgqk', qg, kk).astype(jnp.float32) * SW_HD ** -0.5
    p = sink_softmax(s, sinks.astype(jnp.float32).reshape(SW_KV_HEADS, SW_GROUP, 1, 1))
    o = jnp.einsum('bhgqk,bkhd->bqhgd', p.astype(v.dtype), vv)
    return o.reshape(B, T, SW_HEADS * SW_HD)


def mixer_ab(x, pos, L, kv_cache, ml_state, w_in, b_i, b_f, ml_norm, sinks, w_out):
    B, T, _ = x.shape
    f32 = jnp.float32
    mq, mk, mv, mo, mi, mf, sq, sk, sv = split_cols(x @ w_in, AB_SIZES)

    def heads(a, h):
        return a.reshape(B, T, h, -1).transpose(0, 2, 1, 3).astype(f32)

    q = heads(mq, ML_HEADS)
    k = heads(mk, ML_HEADS) * ML_DK ** -0.5
    v = heads(mv, ML_HEADS)
    ig = (mi + b_i).astype(f32).transpose(0, 2, 1)
    lf = jax.nn.log_sigmoid((mf + b_f).astype(f32)).transpose(0, 2, 1)
    C0, n0, m0 = ml_state
    h, C, n, m = mlstm_chunked(q, k, v, ig, lf, C0.astype(f32), n0.astype(f32), m0.astype(f32), L)
    h = rms_norm(h, ml_norm.reshape(ML_HEADS, 1, ML_DV))
    h = h.transpose(0, 2, 1, 3).reshape(B, T, ML_HEADS * ML_DV).astype(x.dtype) * jax.nn.sigmoid(mo)
    q_s = rope(sq.reshape(B, T, SW_HEADS, SW_HD), pos)
    k_s = rope(sk.reshape(B, T, SW_KV_HEADS, SW_HD), pos)
    v_s = sv.reshape(B, T, SW_KV_HEADS, SW_HD)
    if kv_cache is None:
        a = swa_prompt(q_s, k_s, v_s, sinks)
        new_k, new_v = k_s[:, -WINDOW:], v_s[:, -WINDOW:]
    else:
        a = swa_sample(q_s, k_s, v_s, kv_cache[0], kv_cache[1], sinks)
        new_k, new_v = k_s, v_s
    y = jnp.concatenate([h, a.astype(x.dtype)], -1) @ w_out
    return y, (new_k, new_v, C, n, m)


def mixer_c(x, L, conv_buf, S0, w_in, conv_w, a_log, dt_bias, norm_w, w_out):
    B, T, _ = x.shape
    f32 = jnp.float32
    z = x @ w_in
    qkv = z[..., :GD_CONV_CH]
    _, _, _, zg, bt, at = split_cols(z, C_SIZES)
    if conv_buf is None:
        conv_buf = jnp.zeros((B, GD_CONV - 1, GD_CONV_CH), qkv.dtype)
    xp = jnp.concatenate([conv_buf.astype(qkv.dtype), qkv], 1)
    c = xp[:, 0:T] * conv_w[0]
    for j in range(1, GD_CONV):
        c = c + xp[:, j:j + T] * conv_w[j]
    c = jax.nn.silu(c)
    new_buf = xp[:, -(GD_CONV - 1):]
    q, k, v = split_cols(c, (GD_QK_W, GD_QK_W, GD_V_W))
    rep = GD_V_HEADS // GD_QK_HEADS
    q = jnp.repeat(l2norm(q.reshape(B, T, GD_QK_HEADS, GD_HD)) * GD_HD ** -0.5, rep, axis=2).transpose(0, 2, 1, 3)
    k = jnp.repeat(l2norm(k.reshape(B, T, GD_QK_HEADS, GD_HD)), rep, axis=2).transpose(0, 2, 1, 3)
    v = v.reshape(B, T, GD_V_HEADS, GD_HD).transpose(0, 2, 1, 3).astype(f32)
    beta = jax.nn.sigmoid(bt.astype(f32)).transpose(0, 2, 1)
    g = (-jnp.exp(a_log.astype(f32)) * jax.nn.softplus(at.astype(f32) + dt_bias.astype(f32))).transpose(0, 2, 1)
    o, S = gdn_chunked(q, k, v, g, beta, S0.astype(f32), L)
    o = rms_norm(o.transpose(0, 2, 1, 3), norm_w)
    o = (o * jax.nn.silu(zg.astype(f32).reshape(B, T, GD_V_HEADS, GD_HD))).reshape(B, T, C_OUT).astype(x.dtype)
    return o @ w_out, (new_buf, S)


def moe(x, router_w, router_b, w_gate, w_up, w_down):
    xf = x.reshape(-1, D_MODEL)
    N = xf.shape[0]
    aff = jax.nn.sigmoid((xf @ router_w).astype(jnp.float32))
    grp = (aff + router_b.astype(jnp.float32)).reshape(N, N_GROUPS, EXP_PER_GROUP)
    g_idx = jnp.argmax(lax.top_k(grp, TOP_K)[0].sum(-1), -1)
    in_grp = jnp.take_along_axis(grp, g_idx[:, None, None], 1)[:, 0]
    e_idx = g_idx[:, None] * EXP_PER_GROUP + lax.top_k(in_grp, TOP_K)[1]
    w = jnp.take_along_axis(aff, e_idx, 1)
    w = w / w.sum(-1, keepdims=True)
    gates = jnp.sum(jax.nn.one_hot(e_idx, N_EXPERTS, dtype=jnp.float32) * w[..., None], 1)
    y = jnp.zeros((N, D_MODEL), jnp.float32)
    for e in range(N_EXPERTS):
        h = jax.nn.silu(xf @ w_gate[e]) * (xf @ w_up[e])
        y = y + gates[:, e:e + 1] * (h @ w_down[e]).astype(jnp.float32)
    return y.astype(x.dtype).reshape(x.shape)


def run_trunk(x, pos, L, sw_k, sw_v, ml_C, ml_n, ml_m, gd_S, gd_conv, weights):
    (ab_w_in, ab_b_i, ab_b_f, ab_norm, ab_sinks, ab_w_out, c_w_in, c_conv_w, c_a_log, c_dt_bias,
     c_norm, c_w_out, ln_g, ln_b, router_w, router_b, ex_gate, ex_up, ex_down) = weights
    B = x.shape[0]
    new_k, new_v, new_C, new_n, new_m, new_S, new_conv = [], [], [], [], [], [], []
    for i in range(DEPTH):
        j = i // 2
        if i % 2 == 0:
            kv = None if sw_k is None else (sw_k[j], sw_v[j])
            if ml_C is None:
                st = (jnp.zeros((B, ML_HEADS, ML_DK, ML_DV), jnp.float32),
                      jnp.zeros((B, ML_HEADS, ML_DK), jnp.float32), jnp.zeros((B, ML_HEADS), jnp.float32))
            else:
                st = (ml_C[j], ml_n[j], ml_m[j])
            y, (k_, v_, C_, n_, m_) = mixer_ab(x, pos, L, kv, st, ab_w_in[j], ab_b_i[j], ab_b_f[j],
                                               ab_norm[j], ab_sinks[j], ab_w_out[j])
            new_k.append(k_); new_v.append(v_); new_C.append(C_); new_n.append(n_); new_m.append(m_)
        else:
            buf = None if gd_conv is None else gd_conv[j]
            S0 = jnp.zeros((B, GD_V_HEADS, GD_HD, GD_HD), jnp.float32) if gd_S is None else gd_S[j]
            y, (b_, S_) = mixer_c(x, L, buf, S0, c_w_in[j], c_conv_w[j], c_a_log[j], c_dt_bias[j],
                                  c_norm[j], c_w_out[j])
            new_conv.append(b_); new_S.append(S_)
        x = layer_norm(DN_ALPHA * x + y, ln_g[i, 0], ln_b[i, 0])
        x = layer_norm(DN_ALPHA * x + moe(x, router_w, router_b, ex_gate[i], ex_up[i], ex_down[i]), ln_g[i, 1], ln_b[i, 1])
    st = [jnp.stack(a, 0) for a in (new_k, new_v, new_C, new_n, new_m, new_S, new_conv)]
    return x, st


def setup_inputs(seed: int = 0) -> dict:
    key = jax.random.key(seed)
    ks = iter(jax.random.split(key, 48))
    f32 = jnp.float32

    def nrm(shape, scale):
        return scale * jax.random.normal(next(ks), shape, f32)

    def col_scale(sizes, flags):
        return jnp.concatenate([jnp.full((s,), DN_BETA if fl else 1.0, f32) for s, fl in zip(sizes, flags)])

    dt = jnp.exp(jax.random.uniform(next(ks), (N_ODD, GD_V_HEADS), f32, math.log(1e-3), math.log(1e-1)))
    return {
        'x_prompt': nrm((BATCH, SEQ, D_MODEL), 1.0),
        'x_sample': nrm((DEC_BATCH, DEC_SEQ, D_MODEL), 1.0),
        'cache_swa_k': nrm((N_EVEN, DEC_BATCH, WINDOW, SW_KV_HEADS, SW_HD), 1.0),
        'cache_swa_v': nrm((N_EVEN, DEC_BATCH, WINDOW, SW_KV_HEADS, SW_HD), 0.5),
        'state_mlstm_C': nrm((N_EVEN, DEC_BATCH, ML_HEADS, ML_DK, ML_DV), 0.3),
        'state_mlstm_n': nrm((N_EVEN, DEC_BATCH, ML_HEADS, ML_DK), 0.3),
        'state_mlstm_m': nrm((N_EVEN, DEC_BATCH, ML_HEADS), 1.0),
        'state_gdn_S': nrm((N_ODD, DEC_BATCH, GD_V_HEADS, GD_HD, GD_HD), 0.1),
        'state_gdn_conv': nrm((N_ODD, DEC_BATCH, GD_CONV - 1, GD_CONV_CH), 1.0),
        'ab_w_in': nrm((N_EVEN, D_MODEL, AB_IN), D_MODEL ** -0.5) * col_scale(AB_SIZES, AB_VALUE),
        'ab_b_i': nrm((N_EVEN, ML_HEADS), 0.1),
        'ab_b_f': 3.0 + nrm((N_EVEN, ML_HEADS), 0.5),
        'ab_norm': 1.0 + nrm((N_EVEN, ML_HEADS * ML_DV), 0.02),
        'ab_sinks': nrm((N_EVEN, SW_HEADS), 0.5),
        'ab_w_out': nrm((N_EVEN, AB_OUT, D_MODEL), AB_OUT ** -0.5 * DN_BETA),
        'c_w_in': nrm((N_ODD, D_MODEL, C_IN), D_MODEL ** -0.5) * col_scale(C_SIZES, C_VALUE),
        'c_conv_w': nrm((N_ODD, GD_CONV, GD_CONV_CH), GD_CONV ** -0.5),
        'c_a_log': jnp.log(jax.random.uniform(next(ks), (N_ODD, GD_V_HEADS), f32, 1.0, 16.0)),
        'c_dt_bias': dt + jnp.log(-jnp.expm1(-dt)),
        'c_norm': 1.0 + nrm((N_ODD, GD_HD), 0.02),
        'c_w_out': nrm((N_ODD, C_OUT, D_MODEL), C_OUT ** -0.5 * DN_BETA),
        'ln_g': 1.0 + nrm((DEPTH, 2, D_MODEL), 0.02),
        'ln_b': nrm((DEPTH, 2, D_MODEL), 0.02),
        'router_w': nrm((D_MODEL, N_EXPERTS), D_MODEL ** -0.5),
        'router_b': nrm((N_EXPERTS,), 0.01),
        'ex_gate': nrm((DEPTH, N_EXPERTS, D_MODEL, D_EXPERT), D_MODEL ** -0.5),
        'ex_up': nrm((DEPTH, N_EXPERTS, D_MODEL, D_EXPERT), D_MODEL ** -0.5 * DN_BETA),
        'ex_down': nrm((DEPTH, N_EXPERTS, D_EXPERT, D_MODEL), D_EXPERT ** -0.5 * DN_BETA),
    }


def reference(x_prompt, x_sample, cache_swa_k, cache_swa_v, state_mlstm_C, state_mlstm_n, state_mlstm_m,
              state_gdn_S, state_gdn_conv, ab_w_in, ab_b_i, ab_b_f, ab_norm, ab_sinks, ab_w_out,
              c_w_in, c_conv_w, c_a_log, c_dt_bias, c_norm, c_w_out, ln_g, ln_b, router_w, router_b,
              ex_gate, ex_up, ex_down):
    weights = (ab_w_in, ab_b_i, ab_b_f, ab_norm, ab_sinks, ab_w_out, c_w_in, c_conv_w, c_a_log, c_dt_bias,
               c_norm, c_w_out, ln_g, ln_b, router_w, router_b, ex_gate, ex_up, ex_down)
    pos_p = jnp.arange(x_prompt.shape[1], dtype=jnp.int32)
    y_prompt, (p_k, p_v, p_C, p_n, p_m, p_S, p_conv) = run_trunk(
        x_prompt, pos_p, CHUNK, None, None, None, None, None, None, None, weights)
    t_s = x_sample.shape[1]
    pos_s = PAST_LEN + jnp.arange(t_s, dtype=jnp.int32)
    y_sample, (s_k, s_v, s_C, s_n, s_m, s_S, s_conv) = run_trunk(
        x_sample, pos_s, t_s, cache_swa_k, cache_swa_v, state_mlstm_C, state_mlstm_n, state_mlstm_m,
        state_gdn_S, state_gdn_conv, weights)
    return (y_prompt, y_sample, p_k, p_v, p_C, p_n, p_m, p_S, p_conv, s_k, s_v, s_C, s_n, s_m, s_S, s_conv)
```

```python
import functools
import math

import jax
import jax.numpy as jnp
from jax import lax
from jax.experimental import pallas as pl
from jax.experimental.pallas import tpu as pltpu

f32 = jnp.float32
bf16 = jnp.bfloat16
HIGHEST = lax.Precision.HIGHEST

D_MODEL = 1024
DEPTH = 2
CHUNK = 64
PAST_LEN = 2048
ML_HEADS = 8
ML_DK = 64
ML_DV = 64
SW_HEADS = 8
SW_KV_HEADS = 2
SW_HD = 64
SW_GROUP = SW_HEADS // SW_KV_HEADS
WINDOW = 128
ROPE_THETA = 10000.0
GD_QK_HEADS = 8
GD_V_HEADS = 16
GD_HD = 128
GD_CONV = 4
GD_QK_W = GD_QK_HEADS * GD_HD
GD_V_W = GD_V_HEADS * GD_HD
GD_CONV_CH = 2 * GD_QK_W + GD_V_W
N_EXPERTS = 16
N_GROUPS = 4
EXP_PER_GROUP = 4
D_EXPERT = 512
DN_ALPHA = (2 * DEPTH) ** 0.25
LN_EPS = 1e-5
RMS_EPS = 1e-6

LANES = 128
ML_W = 4 * ML_HEADS * ML_DK
SW_W = SW_HEADS * SW_HD + 2 * SW_KV_HEADS * SW_HD
VMEM_LIMIT = 56 * 1024 * 1024

SDS = jax.ShapeDtypeStruct


def _cparams(*sem):
    return pltpu.CompilerParams(dimension_semantics=sem, vmem_limit_bytes=VMEM_LIMIT)


def _dot(a, b):
    return jnp.dot(a, b, preferred_element_type=f32, precision=HIGHEST)


def _dot_nt(a, b):
    return lax.dot_general(a, b, (((1,), (1,)), ((), ())), preferred_element_type=f32, precision=HIGHEST)


def _dot_tn(a, b):
    return lax.dot_general(a, b, (((0,), (0,)), ((), ())), preferred_element_type=f32, precision=HIGHEST)


def _bdot(a, b):
    return jnp.dot(a.astype(bf16), b.astype(bf16), preferred_element_type=f32)


def _sigmoid(x):
    return 1.0 / (1.0 + jnp.exp(-x))


def _silu(x):
    return x * _sigmoid(x)


def _softplus(x):
    return jnp.maximum(x, 0.0) + jnp.log(1.0 + jnp.exp(-jnp.abs(x)))


def _layer_norm(v, g, b):
    mu = jnp.mean(v, axis=-1, keepdims=True)
    d = v - mu
    var = jnp.mean(d * d, axis=-1, keepdims=True)
    return d * lax.rsqrt(var + LN_EPS) * g + b


def _proj_kernel(x_ref, w_ref, *o_refs, splits, col_chunk):
    xb = x_ref[...].astype(bf16)
    for o_ref, (start, width) in zip(o_refs, splits):
        for c in range(0, width, col_chunk):
            cw = min(col_chunk, width - c)
            o_ref[:, c:c + cw] = jnp.dot(xb, w_ref[:, start + c:start + c + cw], preferred_element_type=f32)


def _proj(x2, w, splits, tm):
    n, k = x2.shape
    return pl.pallas_call(
        functools.partial(_proj_kernel, splits=splits, col_chunk=512),
        grid=(n // tm,),
        in_specs=[pl.BlockSpec((tm, k), lambda i: (i, 0)),
                  pl.BlockSpec(w.shape, lambda i: (0, 0), pipeline_mode=pl.Buffered(1))],
        out_specs=[pl.BlockSpec((tm, wd), lambda i: (i, 0)) for _, wd in splits],
        out_shape=[SDS((n, wd), f32) for _, wd in splits],
        compiler_params=_cparams("parallel"),
        name="in_proj",
    )(x2, w)


def _mlstm_kernel(z_ref, g_ref, bias_ref, nw_ref, cn0_ref, m0_ref, h_ref, cn_ref, m_ref, *, L):
    @pl.when(pl.program_id(1) == 0)
    def _():
        cn_ref[...] = cn0_ref[...]
        m_ref[...] = m0_ref[...]

    g = g_ref[...] + bias_ref[...]
    lf = jnp.minimum(g, 0.0) - jnp.log(1.0 + jnp.exp(-jnp.abs(g)))
    row = lax.broadcasted_iota(jnp.int32, (L, L), 0)
    col = lax.broadcasted_iota(jnp.int32, (L, L), 1)
    causal = row >= col
    bcum = _dot(causal.astype(f32), lf)
    b_t = bcum.T
    g_t = g.T
    lane = lax.broadcasted_iota(jnp.int32, (1, LANES), 1)
    lane_l = lax.broadcasted_iota(jnp.int32, (L, ML_DV), 1)
    one_hot0 = (lane_l == 0).astype(f32)
    m_row = m_ref[0]
    new_m = m_row
    for h in range(ML_HEADS):
        b_col = bcum[:, ML_HEADS + h:ML_HEADS + h + 1]
        b_row = b_t[ML_HEADS + h:ML_HEADS + h + 1, :]
        ig_row = g_t[h:h + 1, :]
        ig_col = g[:, h:h + 1]
        m_h = m_row[:, h:h + 1]
        dmat = jnp.where(causal, b_col - b_row + ig_row, -jnp.inf)
        inter = b_col + m_h
        mt = jnp.maximum(inter, jnp.max(dmat, axis=-1, keepdims=True))
        a = jnp.exp(inter - mt)
        q = z_ref[:, h * ML_DK:(h + 1) * ML_DK]
        k = z_ref[:, ML_HEADS * ML_DK + h * ML_DK:ML_HEADS * ML_DK + (h + 1) * ML_DK] * (ML_DK ** -0.5)
        v = z_ref[:, 2 * ML_HEADS * ML_DK + h * ML_DV:2 * ML_HEADS * ML_DK + (h + 1) * ML_DV]
        og = z_ref[:, 3 * ML_HEADS * ML_DK + h * ML_DV:3 * ML_HEADS * ML_DK + (h + 1) * ML_DV]
        s = _dot_nt(q, k) * jnp.exp(dmat - mt)
        vext = jnp.concatenate([v, one_hot0], axis=-1)
        cn = cn_ref[0, h]
        tot = a * _dot(q, cn) + _dot(s, vext)
        num = tot[:, :ML_DV]
        den = tot[:, ML_DV:ML_DV + 1]
        hh = num / jnp.maximum(jnp.abs(den), jnp.exp(-mt))
        hh = hh * lax.rsqrt(jnp.mean(hh * hh, axis=-1, keepdims=True) + RMS_EPS) * nw_ref[:, h * ML_DV:(h + 1) * ML_DV]
        h_ref[:, h * ML_DV:(h + 1) * ML_DV] = hh * _sigmoid(og)
        m_new = mt[L - 1:L, :]
        b_last = b_col[L - 1:L, :]
        wk = jnp.exp(b_last - b_col + ig_col - m_new)
        dec = jnp.exp(b_last + m_h - m_new)
        cn_ref[0, h] = dec * cn + _dot_tn(k, wk * vext)
        new_m = jnp.where(lane == h, m_new, new_m)
    m_ref[0] = new_m


def _mlstm(z_ml, z_g, bias_row, norm_row, cn0, m0, B, T, L):
    nc = T // L
    n = B * T
    tok = lambda b, c: (b * nc + c, 0)
    return pl.pallas_call(
        functools.partial(_mlstm_kernel, L=L),
        grid=(B, nc),
        in_specs=[pl.BlockSpec((L, ML_W), tok),
                  pl.BlockSpec((L, LANES), tok),
                  pl.BlockSpec((1, LANES), lambda b, c: (0, 0)),
                  pl.BlockSpec((1, ML_HEADS * ML_DV), lambda b, c: (0, 0)),
                  pl.BlockSpec((1, ML_HEADS, ML_DK, LANES), lambda b, c: (b, 0, 0, 0)),
                  pl.BlockSpec((1, 1, LANES), lambda b, c: (b, 0, 0))],
        out_specs=[pl.BlockSpec((L, ML_HEADS * ML_DV), tok),
                   pl.BlockSpec((1, ML_HEADS, ML_DK, LANES), lambda b, c: (b, 0, 0, 0)),
                   pl.BlockSpec((1, 1, LANES), lambda b, c: (b, 0, 0))],
        out_shape=[SDS((n, ML_HEADS * ML_DV), f32),
                   SDS((B, ML_HEADS, ML_DK, LANES), f32),
                   SDS((B, 1, LANES), f32)],
        compiler_params=_cparams("parallel", "arbitrary"),
        name="mlstm",
    )(z_ml, z_g, bias_row, norm_row, cn0, m0)


def _rope(x, cos, sin_signed):
    w = x.shape[-1]
    lane = lax.broadcasted_iota(jnp.int32, x.shape, 1)
    swapped = jnp.where((lane % SW_HD) < SW_HD // 2, pltpu.roll(x, w - SW_HD // 2, 1), pltpu.roll(x, SW_HD // 2, 1))
    return x * cos + swapped * sin_signed


def _swa_attend(qr, keys, vals, first_valid, sinks_ref, o_ref, L):
    for g in range(SW_KV_HEADS):
        kg = jnp.concatenate([k[:, g * SW_HD:(g + 1) * SW_HD] for k in keys], axis=0)
        vg = jnp.concatenate([v[:, g * SW_HD:(g + 1) * SW_HD] for v in vals], axis=0)
        q4 = jnp.concatenate([qr[:, (g * SW_GROUP + i) * SW_HD:(g * SW_GROUP + i + 1) * SW_HD]
                              for i in range(SW_GROUP)], axis=0)
        s = _dot_nt(q4, kg) * (SW_HD ** -0.5)
        if first_valid is not None:
            kcol = lax.broadcasted_iota(jnp.int32, (1, kg.shape[0]), 1)
            s = jnp.where(kcol >= first_valid, s, -jnp.inf)
        sink = jnp.concatenate(
            [jnp.broadcast_to(sinks_ref[:, g * SW_GROUP + i:g * SW_GROUP + i + 1], (L, 1)) for i in range(SW_GROUP)],
            axis=0)
        mx = jnp.maximum(jnp.max(s, axis=-1, keepdims=True), sink)
        e = jnp.exp(s - mx)
        p = e / (jnp.sum(e, axis=-1, keepdims=True) + jnp.exp(sink - mx))
        o = _dot(p, vg)
        for i in range(SW_GROUP):
            hd = g * SW_GROUP + i
            o_ref[:, hd * SW_HD:(hd + 1) * SW_HD] = o[i * L:(i + 1) * L, :]


def _swa_prompt_kernel(q_ref, k0_ref, k1_ref, k2_ref, v0_ref, v1_ref, v2_ref,
                       c0_ref, c1_ref, c2_ref, s0_ref, s1_ref, s2_ref, sinks_ref, o_ref, kr_ref, *, L):
    c = pl.program_id(1)
    cos_q = jnp.concatenate([c2_ref[...]] * (SW_HEADS // SW_KV_HEADS), axis=-1)
    sin_q = jnp.concatenate([s2_ref[...]] * (SW_HEADS // SW_KV_HEADS), axis=-1)
    qr = _rope(q_ref[...], cos_q, sin_q)
    keys = [_rope(k0_ref[...], c0_ref[...], s0_ref[...]),
            _rope(k1_ref[...], c1_ref[...], s1_ref[...]),
            _rope(k2_ref[...], c2_ref[...], s2_ref[...])]
    kr_ref[...] = keys[2]
    vals = [v0_ref[...], v1_ref[...], v2_ref[...]]
    _swa_attend(qr, keys, vals, (2 - c) * L, sinks_ref, o_ref, L)


def _swa_prompt(z_sw, cos_t, sin_t, sinks_row, B, T, L):
    nc = T // L
    n = B * T
    kcol = SW_HEADS * SW_HD // LANES
    vcol = kcol + 1

    def tok(j):
        return lambda b, c: (b * nc + jnp.maximum(c - j, 0), 0)

    def tokc(j, col):
        return lambda b, c: (b * nc + jnp.maximum(c - j, 0), col)

    def tab(j):
        return lambda b, c: (jnp.maximum(c - j, 0), 0)

    return pl.pallas_call(
        functools.partial(_swa_prompt_kernel, L=L),
        grid=(B, nc),
        in_specs=[pl.BlockSpec((L, SW_HEADS * SW_HD), tok(0))]
        + [pl.BlockSpec((L, LANES), tokc(j, kcol)) for j in (2, 1, 0)]
        + [pl.BlockSpec((L, LANES), tokc(j, vcol)) for j in (2, 1, 0)]
        + [pl.BlockSpec((L, LANES), tab(j)) for j in (2, 1, 0)]
        + [pl.BlockSpec((L, LANES), tab(j)) for j in (2, 1, 0)]
        + [pl.BlockSpec((1, LANES), lambda b, c: (0, 0))],
        out_specs=[pl.BlockSpec((L, SW_HEADS * SW_HD), tok(0)),
                   pl.BlockSpec((L, LANES), tok(0))],
        out_shape=[SDS((n, SW_HEADS * SW_HD), f32), SDS((n, LANES), f32)],
        compiler_params=_cparams("parallel", "parallel"),
        name="swa_prompt",
    )(z_sw, z_sw, z_sw, z_sw, z_sw, z_sw, z_sw, cos_t, cos_t, cos_t, sin_t, sin_t, sin_t, sinks_row)


def _swa_sample_kernel(q_ref, k_ref, v_ref, ck_ref, cv_ref, cos_ref, sin_ref, sinks_ref, o_ref, kr_ref, *, L):
    cos_q = jnp.concatenate([cos_ref[...]] * (SW_HEADS // SW_KV_HEADS), axis=-1)
    sin_q = jnp.concatenate([sin_ref[...]] * (SW_HEADS // SW_KV_HEADS), axis=-1)
    qr = _rope(q_ref[...], cos_q, sin_q)
    kr = _rope(k_ref[...], cos_ref[...], sin_ref[...])
    kr_ref[...] = kr
    _swa_attend(qr, [ck_ref[0], kr], [cv_ref[0], v_ref[...]], None, sinks_ref, o_ref, L)


def _swa_sample(z_sw, cache_k, cache_v, cos_t, sin_t, sinks_row, B, T):
    n = B * T
    kcol = SW_HEADS * SW_HD // LANES
    return pl.pallas_call(
        functools.partial(_swa_sample_kernel, L=T),
        grid=(B,),
        in_specs=[pl.BlockSpec((T, SW_HEADS * SW_HD), lambda b: (b, 0)),
                  pl.BlockSpec((T, LANES), lambda b: (b, kcol)),
                  pl.BlockSpec((T, LANES), lambda b: (b, kcol + 1)),
                  pl.BlockSpec((1, WINDOW, LANES), lambda b: (b, 0, 0)),
                  pl.BlockSpec((1, WINDOW, LANES), lambda b: (b, 0, 0)),
                  pl.BlockSpec((T, LANES), lambda b: (0, 0)),
                  pl.BlockSpec((T, LANES), lambda b: (0, 0)),
                  pl.BlockSpec((1, LANES), lambda b: (0, 0))],
        out_specs=[pl.BlockSpec((T, SW_HEADS * SW_HD), lambda b: (b, 0)),
                   pl.BlockSpec((T, LANES), lambda b: (b, 0))],
        out_shape=[SDS((n, SW_HEADS * SW_HD), f32), SDS((n, LANES), f32)],
        compiler_params=_cparams("parallel"),
        name="swa_sample",
    )(z_sw, z_sw, z_sw, cache_k, cache_v, cos_t, sin_t, sinks_row)


def _out_ln_kernel(*refs, n_in):
    x_ref = refs[0]
    a_refs = refs[1:1 + n_in]
    w_refs = refs[1 + n_in:1 + 2 * n_in]
    g_ref, b_ref, o_ref = refs[1 + 2 * n_in:]
    y = _bdot(a_refs[0][...], w_refs[0][...])
    for a_ref, w_ref in zip(a_refs[1:], w_refs[1:]):
        y = y + _bdot(a_ref[...], w_ref[...])
    o_ref[...] = _layer_norm(DN_ALPHA * x_ref[...] + y, g_ref[...], b_ref[...])


def _out_ln(x2, acts, ws, g_row, b_row, tm):
    n = x2.shape[0]
    row = lambda i: (i, 0)
    const = lambda i: (0, 0)
    return pl.pallas_call(
        functools.partial(_out_ln_kernel, n_in=len(acts)),
        grid=(n // tm,),
        in_specs=[pl.BlockSpec((tm, D_MODEL), row)]
        + [pl.BlockSpec((tm, a.shape[1]), row) for a in acts]
        + [pl.BlockSpec(w.shape, const) for w in ws]
        + [pl.BlockSpec((1, D_MODEL), const), pl.BlockSpec((1, D_MODEL), const)],
        out_specs=pl.BlockSpec((tm, D_MODEL), row),
        out_shape=SDS((n, D_MODEL), f32),
        compiler_params=_cparams("parallel"),
        name="out_proj_ln",
    )(x2, *acts, *ws, g_row, b_row)


def _router_kernel(x_ref, rw_ref, rb_ref, o_ref):
    logits = _dot_nt(rw_ref[...], x_ref[...])
    aff = _sigmoid(logits)
    sc = aff + rb_ref[...]
    s = [sc[e:e + 1, :] for e in range(N_EXPERTS)]
    a = [aff[e:e + 1, :] for e in range(N_EXPERTS)]
    scores = []
    for gi in range(N_GROUPS):
        w, x, y, z = s[4 * gi:4 * gi + 4]
        p, q = jnp.maximum(w, x), jnp.minimum(w, x)
        r, t = jnp.maximum(y, z), jnp.minimum(y, z)
        scores.append(jnp.maximum(p, r) + jnp.maximum(jnp.minimum(p, r), jnp.maximum(q, t)))
    best = scores[0]
    gsel = jnp.zeros_like(best, dtype=jnp.int32)
    for gi in range(1, N_GROUPS):
        better = scores[gi] > best
        best = jnp.where(better, scores[gi], best)
        gsel = jnp.where(better, gi, gsel)
    sel = []
    for e in range(N_EXPERTS):
        gi, i = divmod(e, EXP_PER_GROUP)
        beaten = jnp.zeros_like(gsel)
        for j in range(EXP_PER_GROUP):
            if j == i:
                continue
            o = s[4 * gi + j]
            wins = (o >= s[e]) if j < i else (o > s[e])
            beaten = beaten + wins.astype(jnp.int32)
        sel.append((gsel == gi) & (beaten < 2))
    den = jnp.zeros_like(best)
    for e in range(N_EXPERTS):
        den = den + jnp.where(sel[e], a[e], 0.0)
    o_ref[...] = jnp.concatenate([jnp.where(sel[e], a[e] / den, 0.0) for e in range(N_EXPERTS)], axis=0)


def _router(x2, rw_t, rb_col, tm):
    n = x2.shape[0]
    return pl.pallas_call(
        _router_kernel,
        grid=(n // tm,),
        in_specs=[pl.BlockSpec((tm, D_MODEL), lambda i: (i, 0)),
                  pl.BlockSpec((N_EXPERTS, D_MODEL), lambda i: (0, 0)),
                  pl.BlockSpec((N_EXPERTS, 1), lambda i: (0, 0))],
        out_specs=pl.BlockSpec((N_EXPERTS, tm), lambda i: (0, i)),
        out_shape=SDS((N_EXPERTS, n), f32),
        compiler_params=_cparams("parallel"),
        name="router",
    )(x2, rw_t, rb_col)


def _moe_kernel(x_ref, gates_ref, wg_ref, wu_ref, wd_ref, g_ref, b_ref, o_ref, xb_ref, acc_ref):
    e = pl.program_id(1)

    @pl.when(e == 0)
    def _():
        xb_ref[...] = x_ref[...].astype(bf16)
        acc_ref[...] = jnp.zeros_like(acc_ref)

    xb = xb_ref[...]
    lane = lax.broadcasted_iota(jnp.int32, gates_ref.shape, 1)
    gcol = jnp.sum(jnp.where(lane == e, gates_ref[...], 0.0), axis=-1, keepdims=True)
    h = _silu(jnp.dot(xb, wg_ref[0], preferred_element_type=f32)) * jnp.dot(xb, wu_ref[0], preferred_element_type=f32)
    acc_ref[...] += jnp.dot((gcol * h).astype(bf16), wd_ref[0], preferred_element_type=f32)

    @pl.when(e == N_EXPERTS - 1)
    def _():
        o_ref[...] = _layer_norm(DN_ALPHA * x_ref[...] + acc_ref[...], g_ref[...], b_ref[...])


def _moe_ln(x2, gates, wg, wu, wd, g_row, b_row, tm):
    n = x2.shape[0]
    row = lambda i, e: (i, 0)
    const = lambda i, e: (0, 0)
    return pl.pallas_call(
        _moe_kernel,
        grid=(n // tm, N_EXPERTS),
        in_specs=[pl.BlockSpec((tm, D_MODEL), row),
                  pl.BlockSpec((tm, N_EXPERTS), row),
                  pl.BlockSpec((1, D_MODEL, D_EXPERT), lambda i, e: (e, 0, 0)),
                  pl.BlockSpec((1, D_MODEL, D_EXPERT), lambda i, e: (e, 0, 0)),
                  pl.BlockSpec((1, D_EXPERT, D_MODEL), lambda i, e: (e, 0, 0)),
                  pl.BlockSpec((1, D_MODEL), const), pl.BlockSpec((1, D_MODEL), const)],
        out_specs=pl.BlockSpec((tm, D_MODEL), row),
        out_shape=SDS((n, D_MODEL), f32),
        scratch_shapes=[pltpu.VMEM((tm, D_MODEL), bf16), pltpu.VMEM((tm, D_MODEL), f32)],
        compiler_params=_cparams("parallel", "arbitrary"),
        name="moe_ln",
    )(x2, gates, wg, wu, wd, g_row, b_row)


def _unit_lower_inverse(a, L):
    row = lax.broadcasted_iota(jnp.int32, (L, L), 0)
    col = lax.broadcasted_iota(jnp.int32, (L, L), 1)
    p = (row == col).astype(f32) - a
    pw = a
    span = 2
    while span < L:
        pw = _dot(pw, pw)
        p = p + _dot(pw, p)
        span *= 2
    return p


def _gdn_kernel(x_ref, zg_ref, ba_ref, cw_ref, alog_ref, dt_ref, nw_ref, s0_ref, cb_ref,
                o_ref, s_ref, prev_ref, *, L):
    @pl.when(pl.program_id(1) == 0)
    def _():
        s_ref[...] = s0_ref[...]
        prev_ref[...] = cb_ref[0]

    def conv_silu(lo, width):
        cur = x_ref[:, lo:lo + width]
        cat = jnp.concatenate([prev_ref[:, lo:lo + width], cur], axis=0)
        acc = cat[5:5 + L] * cw_ref[0:1, lo:lo + width]
        acc = acc + cat[6:6 + L] * cw_ref[1:2, lo:lo + width]
        acc = acc + cat[7:7 + L] * cw_ref[2:3, lo:lo + width]
        acc = acc + cur * cw_ref[3:4, lo:lo + width]
        return _silu(acc)

    def l2n(v):
        return v * lax.rsqrt(jnp.sum(v * v, axis=-1, keepdims=True) + 1e-6)

    ba = ba_ref[...]
    beta = _sigmoid(ba)
    gl = -jnp.exp(alog_ref[...]) * _softplus(ba + dt_ref[...])
    row = lax.broadcasted_iota(jnp.int32, (L, L), 0)
    col = lax.broadcasted_iota(jnp.int32, (L, L), 1)
    incl = row >= col
    strict = row > col
    gcum = _dot(incl.astype(f32), gl)
    gcum_t = gcum.T
    for j in range(GD_QK_HEADS):
        q = l2n(conv_silu(j * GD_HD, GD_HD)) * (GD_HD ** -0.5)
        k = l2n(conv_silu(GD_QK_W + j * GD_HD, GD_HD))
        kk = _dot_nt(k, k)
        qk = _dot_nt(q, k)
        for r in range(GD_V_HEADS // GD_QK_HEADS):
            hv = j * (GD_V_HEADS // GD_QK_HEADS) + r
            v = conv_silu(2 * GD_QK_W + hv * GD_HD, GD_HD)
            g_col = gcum[:, GD_V_HEADS + hv:GD_V_HEADS + hv + 1]
            g_row = gcum_t[GD_V_HEADS + hv:GD_V_HEADS + hv + 1, :]
            b_col = beta[:, hv:hv + 1]
            decay = jnp.exp(jnp.where(incl, g_col - g_row, -jnp.inf))
            amat = jnp.where(strict, b_col * kk * decay, 0.0)
            tinv = _unit_lower_inverse(amat, L)
            eg = jnp.exp(g_col)
            rhs = jnp.concatenate([b_col * v, (b_col * eg) * k], axis=-1)
            sol = _dot(tinv, rhs)
            st = s_ref[0, hv]
            wn = sol[:, :GD_HD] - _dot(sol[:, GD_HD:], st)
            o = eg * _dot(q, st) + _dot(qk * decay, wn)
            g_last = g_col[L - 1:L, :]
            s_ref[0, hv] = jnp.exp(g_last) * st + _dot_tn(jnp.exp(g_last - g_col) * k, wn)
            o = o * lax.rsqrt(jnp.mean(o * o, axis=-1, keepdims=True) + RMS_EPS) * nw_ref[...]
            o_ref[:, hv * GD_HD:(hv + 1) * GD_HD] = o * _silu(zg_ref[:, hv * GD_HD:(hv + 1) * GD_HD])
    prev_ref[...] = x_ref[L - 8:L, :]


def _gdn(qkv, zg, ba, conv_w, alog_row, dt_row, norm_row, s0, conv8, B, T, L):
    nc = T // L
    n = B * T
    tok = lambda b, c: (b * nc + c, 0)
    const = lambda b, c: (0, 0)
    return pl.pallas_call(
        functools.partial(_gdn_kernel, L=L),
        grid=(B, nc),
        in_specs=[pl.BlockSpec((L, GD_CONV_CH), tok),
                  pl.BlockSpec((L, GD_V_W), tok),
                  pl.BlockSpec((L, LANES), tok),
                  pl.BlockSpec((GD_CONV, GD_CONV_CH), const),
                  pl.BlockSpec((1, LANES), const),
                  pl.BlockSpec((1, LANES), const),
                  pl.BlockSpec((1, GD_HD), const),
                  pl.BlockSpec((1, GD_V_HEADS, GD_HD, GD_HD), lambda b, c: (b, 0, 0, 0)),
                  pl.BlockSpec((1, 8, GD_CONV_CH), lambda b, c: (b, 0, 0))],
        out_specs=[pl.BlockSpec((L, GD_V_W), tok),
                   pl.BlockSpec((1, GD_V_HEADS, GD_HD, GD_HD), lambda b, c: (b, 0, 0, 0))],
        out_shape=[SDS((n, GD_V_W), f32), SDS((B, GD_V_HEADS, GD_HD, GD_HD), f32)],
        scratch_shapes=[pltpu.VMEM((8, GD_CONV_CH), f32)],
        compiler_params=_cparams("parallel", "arbitrary"),
        name="gdn",
    )(qkv, zg, ba, conv_w, alog_row, dt_row, norm_row, s0, conv8)


def _pad_lanes(row, offset=0):
    return jnp.zeros((1, LANES), f32).at[0, offset:offset + row.shape[0]].set(row.astype(f32))


def _rope_tables(pos):
    half = SW_HD // 2
    inv = ROPE_THETA ** (-jnp.arange(half, dtype=f32) / half)
    ang = pos.astype(f32)[:, None] * inv[None, :]
    cos, sin = jnp.cos(ang), jnp.sin(ang)
    cos_t = jnp.concatenate([cos, cos] * SW_KV_HEADS, axis=-1)
    sin_t = jnp.concatenate([-sin, sin] * SW_KV_HEADS, axis=-1)
    return cos_t, sin_t


def _tile(n, pref):
    return pref if n % pref == 0 else n


def _trunk(x, pos, L, state, p):
    B, T, _ = x.shape
    n = B * T
    x2 = x.reshape(n, D_MODEL)
    tm = _tile(n, 512)

    z_ml, z_sw, z_g = _proj(x2, p["ab_w"], ((0, ML_W), (ML_W, SW_W), (ML_W + SW_W, LANES)), tm)
    if state is None:
        cn0 = jnp.zeros((B, ML_HEADS, ML_DK, LANES), f32)
        m0 = jnp.zeros((B, 1, LANES), f32)
    else:
        cn0 = jnp.concatenate([state["ml_C"], state["ml_n"][..., None],
                               jnp.zeros((B, ML_HEADS, ML_DK, LANES - ML_DV - 1), f32)], axis=-1)
        m0 = jnp.zeros((B, 1, LANES), f32).at[:, 0, :ML_HEADS].set(state["ml_m"])
    h_ml, cn, m_out = _mlstm(z_ml, z_g, p["ab_bias"], p["ab_norm"], cn0, m0, B, T, L)
    cos_t, sin_t = _rope_tables(pos)
    if state is None:
        a_sw, k_rot = _swa_prompt(z_sw, cos_t, sin_t, p["ab_sinks"], B, T, L)
    else:
        a_sw, k_rot = _swa_sample(z_sw, state["sw_k"].reshape(B, WINDOW, LANES),
                                  state["sw_v"].reshape(B, WINDOW, LANES), cos_t, sin_t, p["ab_sinks"], B, T)
    keep = min(T, WINDOW)
    new_k = k_rot.reshape(B, T, SW_KV_HEADS, SW_HD)[:, T - keep:]
    new_v = z_sw[:, SW_HEADS * SW_HD + LANES:].reshape(B, T, SW_KV_HEADS, SW_HD)[:, T - keep:]
    x2 = _out_ln(x2, [h_ml, a_sw], [p["ab_wo_h"], p["ab_wo_a"]], p["ln_g"][0][0], p["ln_b"][0][0], tm)
    x2 = _moe_block(x2, p, 0)

    tm1 = _tile(n, 256)
    qkv, zg, ba = _proj(x2, p["c_w"], ((0, GD_CONV_CH), (GD_CONV_CH, GD_V_W), (GD_CONV_CH + GD_V_W, LANES)), tm1)
    if state is None:
        s0 = jnp.zeros((B, GD_V_HEADS, GD_HD, GD_HD), f32)
        conv8 = jnp.zeros((B, 8, GD_CONV_CH), f32)
    else:
        s0 = state["gd_S"]
        conv8 = jnp.concatenate([jnp.zeros((B, 8 - (GD_CONV - 1), GD_CONV_CH), f32), state["gd_conv"]], axis=1)
    o_gd, s_out = _gdn(qkv, zg, ba, p["c_conv_w"], p["c_alog"], p["c_dt"], p["c_norm"], s0, conv8, B, T, L)
    new_conv = qkv.reshape(B, T, GD_CONV_CH)[:, T - (GD_CONV - 1):]
    x2 = _out_ln(x2, [o_gd], [p["c_wo"]], p["ln_g"][1][0], p["ln_b"][1][0], tm)
    x2 = _moe_block(x2, p, 1)

    outs = (new_k[None], new_v[None], cn[None, ..., :ML_DV], cn[None, ..., ML_DV], m_out[None, :, 0, :ML_HEADS],
            s_out[None], new_conv[None])
    return x2.reshape(B, T, D_MODEL), outs


def _moe_block(x2, p, layer):
    n = x2.shape[0]
    gates_t = _router(x2, p["router_wt"], p["router_b"], _tile(n, 512))
    return _moe_ln(x2, gates_t.T, p["ex_gate"][layer], p["ex_up"][layer], p["ex_down"][layer],
                   p["ln_g"][layer][1], p["ln_b"][layer][1], _tile(n, 1024))


def kernel(x_prompt, x_sample, cache_swa_k, cache_swa_v, state_mlstm_C, state_mlstm_n, state_mlstm_m, state_gdn_S, state_gdn_conv, ab_w_in, ab_b_i, ab_b_f, ab_norm, ab_sinks, ab_w_out, c_w_in, c_conv_w, c_a_log, c_dt_bias, c_norm, c_w_out, ln_g, ln_b, router_w, router_b, ex_gate, ex_up, ex_down):
    gate_lo = ML_W
    sw_lo = ML_W + 2 * ML_HEADS
    w0 = ab_w_in[0]
    ab_w = jnp.concatenate([w0[:, :gate_lo], w0[:, sw_lo:], w0[:, gate_lo:sw_lo],
                            jnp.zeros((D_MODEL, LANES - 2 * ML_HEADS), f32)], axis=1).astype(bf16)
    w1 = c_w_in[0]
    c_w = jnp.concatenate([w1, jnp.zeros((D_MODEL, LANES - 2 * GD_V_HEADS), f32)], axis=1).astype(bf16)
    wo = ab_w_out[0].astype(bf16)
    p = {
        "ab_w": ab_w,
        "ab_bias": _pad_lanes(jnp.concatenate([ab_b_i[0], ab_b_f[0]])),
        "ab_norm": ab_norm[0].reshape(1, ML_HEADS * ML_DV),
        "ab_sinks": _pad_lanes(ab_sinks[0]),
        "ab_wo_h": wo[:ML_HEADS * ML_DV],
        "ab_wo_a": wo[ML_HEADS * ML_DV:],
        "c_w": c_w,
        "c_conv_w": c_conv_w[0],
        "c_alog": _pad_lanes(c_a_log[0], GD_V_HEADS),
        "c_dt": _pad_lanes(c_dt_bias[0], GD_V_HEADS),
        "c_norm": c_norm[0].reshape(1, GD_HD),
        "c_wo": c_w_out[0].astype(bf16),
        "ln_g": [[ln_g[i, j].reshape(1, D_MODEL) for j in range(2)] for i in range(DEPTH)],
        "ln_b": [[ln_b[i, j].reshape(1, D_MODEL) for j in range(2)] for i in range(DEPTH)],
        "router_wt": router_w.T,
        "router_b": router_b.reshape(N_EXPERTS, 1),
        "ex_gate": ex_gate.astype(bf16),
        "ex_up": ex_up.astype(bf16),
        "ex_down": ex_down.astype(bf16),
    }
    t_p = x_prompt.shape[1]
    y_p, st_p = _trunk(x_prompt, jnp.arange(t_p, dtype=jnp.int32), CHUNK, None, p)
    t_s = x_sample.shape[1]
    state = {"sw_k": cache_swa_k[0], "sw_v": cache_swa_v[0], "ml_C": state_mlstm_C[0], "ml_n": state_mlstm_n[0],
             "ml_m": state_mlstm_m[0], "gd_S": state_gdn_S[0], "gd_conv": state_gdn_conv[0]}
    y_s, st_s = _trunk(x_sample, PAST_LEN + jnp.arange(t_s, dtype=jnp.int32), t_s, state, p)
    return (y_p, y_s) + st_p + st_s
```

```python
import functools
import math

import jax
import jax.numpy as jnp
from jax import lax
from jax.experimental import pallas as pl
from jax.experimental.pallas import tpu as pltpu

f32 = jnp.float32
bf16 = jnp.bfloat16
HIGHEST = lax.Precision.HIGHEST

D_MODEL = 1024
DEPTH = 2
CHUNK = 64
PAST_LEN = 2048
ML_HEADS = 8
ML_DK = 64
ML_DV = 64
SW_HEADS = 8
SW_KV_HEADS = 2
SW_HD = 64
SW_GROUP = SW_HEADS // SW_KV_HEADS
WINDOW = 128
ROPE_THETA = 10000.0
GD_QK_HEADS = 8
GD_V_HEADS = 16
GD_HD = 128
GD_CONV = 4
GD_QK_W = GD_QK_HEADS * GD_HD
GD_V_W = GD_V_HEADS * GD_HD
GD_CONV_CH = 2 * GD_QK_W + GD_V_W
N_EXPERTS = 16
N_GROUPS = 4
EXP_PER_GROUP = 4
D_EXPERT = 512
DN_ALPHA = (2 * DEPTH) ** 0.25
LN_EPS = 1e-5
RMS_EPS = 1e-6

LANES = 128
ML_W = 4 * ML_HEADS * ML_DK
SW_W = SW_HEADS * SW_HD + 2 * SW_KV_HEADS * SW_HD
VMEM_LIMIT = 56 * 1024 * 1024

SDS = jax.ShapeDtypeStruct


def _cparams(*sem):
    return pltpu.CompilerParams(dimension_semantics=sem, vmem_limit_bytes=VMEM_LIMIT)


def _dot(a, b):
    return jnp.dot(a, b, preferred_element_type=f32, precision=HIGHEST)


def _dot_nt(a, b):
    return lax.dot_general(a, b, (((1,), (1,)), ((), ())), preferred_element_type=f32, precision=HIGHEST)


def _bdot(a, b):
    return jnp.dot(a.astype(bf16), b.astype(bf16), preferred_element_type=f32)


def _bdot_nt(a, b):
    return lax.dot_general(a.astype(bf16), b.astype(bf16), (((1,), (1,)), ((), ())), preferred_element_type=f32)


def _bdot_tn(a, b):
    return jnp.dot(a.T.astype(bf16), b.astype(bf16), preferred_element_type=f32)


def _sigmoid(x):
    return 1.0 / (1.0 + jnp.exp(-x))


def _silu(x):
    return x * _sigmoid(x)


def _softplus(x):
    return jnp.maximum(x, 0.0) + jnp.log(1.0 + jnp.exp(-jnp.abs(x)))


def _layer_norm(v, g, b):
    mu = jnp.mean(v, axis=-1, keepdims=True)
    d = v - mu
    var = jnp.mean(d * d, axis=-1, keepdims=True)
    return d * lax.rsqrt(var + LN_EPS) * g + b


def _proj_kernel(x_ref, w_ref, *o_refs, splits, col_chunk):
    xb = x_ref[...].astype(bf16)
    for o_ref, (start, width) in zip(o_refs, splits):
        for c in range(0, width, col_chunk):
            cw = min(col_chunk, width - c)
            o_ref[:, c:c + cw] = jnp.dot(xb, w_ref[:, start + c:start + c + cw], preferred_element_type=f32)


def _proj(x2, w, splits, tm):
    n, k = x2.shape
    return pl.pallas_call(
        functools.partial(_proj_kernel, splits=splits, col_chunk=512),
        grid=(n // tm,),
        in_specs=[pl.BlockSpec((tm, k), lambda i: (i, 0)),
                  pl.BlockSpec(w.shape, lambda i: (0, 0), pipeline_mode=pl.Buffered(1))],
        out_specs=[pl.BlockSpec((tm, wd), lambda i: (i, 0)) for _, wd in splits],
        out_shape=[SDS((n, wd), f32) for _, wd in splits],
        compiler_params=_cparams("parallel"),
        name="in_proj",
    )(x2, w)


def _mlstm_kernel(z_ref, g_ref, bias_ref, nw_ref, cn0_ref, m0_ref, h_ref, cn_ref, m_ref, *, L, BB):
    @pl.when(pl.program_id(1) == 0)
    def _():
        cn_ref[...] = cn0_ref[...]
        m_ref[...] = m0_ref[...]

    row = lax.broadcasted_iota(jnp.int32, (L, L), 0)
    col = lax.broadcasted_iota(jnp.int32, (L, L), 1)
    causal = row >= col
    tri = causal.astype(f32)
    lane = lax.broadcasted_iota(jnp.int32, (1, LANES), 1)
    lane_l = lax.broadcasted_iota(jnp.int32, (L, ML_DV), 1)
    one_hot0 = (lane_l == 0).astype(f32)
    for bi in range(BB):
        g = g_ref[bi] + bias_ref[...]
        lf = jnp.minimum(g, 0.0) - jnp.log(1.0 + jnp.exp(-jnp.abs(g)))
        bcum = _dot(tri, lf)
        b_t = bcum.T
        g_t = g.T
        m_row = m_ref[bi]
        new_m = m_row
        outs = []
        for h in range(ML_HEADS):
            b_col = bcum[:, ML_HEADS + h:ML_HEADS + h + 1]
            b_row = b_t[ML_HEADS + h:ML_HEADS + h + 1, :]
            ig_row = g_t[h:h + 1, :]
            ig_col = g[:, h:h + 1]
            m_h = m_row[:, h:h + 1]
            dmat = jnp.where(causal, b_col - b_row + ig_row, -jnp.inf)
            inter = b_col + m_h
            mt = jnp.maximum(inter, jnp.max(dmat, axis=-1, keepdims=True))
            a = jnp.exp(inter - mt)
            q = z_ref[bi, :, h * ML_DK:(h + 1) * ML_DK]
            k = z_ref[bi, :, ML_HEADS * ML_DK + h * ML_DK:ML_HEADS * ML_DK + (h + 1) * ML_DK] * (ML_DK ** -0.5)
            v = z_ref[bi, :, 2 * ML_HEADS * ML_DK + h * ML_DV:2 * ML_HEADS * ML_DK + (h + 1) * ML_DV]
            og = z_ref[bi, :, 3 * ML_HEADS * ML_DK + h * ML_DV:3 * ML_HEADS * ML_DK + (h + 1) * ML_DV]
            s = _bdot_nt(q, k) * jnp.exp(dmat - mt)
            vext = jnp.concatenate([v, one_hot0], axis=-1)
            cn = cn_ref[bi, h]
            tot = a * _bdot(q, cn) + _bdot(s, vext)
            num = tot[:, :ML_DV]
            den = tot[:, ML_DV:ML_DV + 1]
            hh = num / jnp.maximum(jnp.abs(den), jnp.exp(-mt))
            hh = hh * lax.rsqrt(jnp.mean(hh * hh, axis=-1, keepdims=True) + RMS_EPS) * nw_ref[:, h * ML_DV:(h + 1) * ML_DV]
            outs.append(hh * _sigmoid(og))
            m_new = mt[L - 1:L, :]
            b_last = b_col[L - 1:L, :]
            wk = jnp.exp(b_last - b_col + ig_col - m_new)
            dec = jnp.exp(b_last + m_h - m_new)
            cn_ref[bi, h] = dec * cn + _bdot_tn(k, wk * vext)
            new_m = jnp.where(lane == h, m_new, new_m)
        m_ref[bi] = new_m
        h_ref[bi] = jnp.concatenate(outs, axis=-1)


def _mlstm(z_ml, z_g, bias_row, norm_row, cn0, m0, B, T, L):
    nc = T // L
    bb = min(B, 4)
    tok = lambda b, c: (b, c, 0)
    st4 = lambda b, c: (b, 0, 0, 0)
    st3 = lambda b, c: (b, 0, 0)
    return pl.pallas_call(
        functools.partial(_mlstm_kernel, L=L, BB=bb),
        grid=(B // bb, nc),
        in_specs=[pl.BlockSpec((bb, L, ML_W), tok),
                  pl.BlockSpec((bb, L, LANES), tok),
                  pl.BlockSpec((1, LANES), lambda b, c: (0, 0)),
                  pl.BlockSpec((1, ML_HEADS * ML_DV), lambda b, c: (0, 0)),
                  pl.BlockSpec((bb, ML_HEADS, ML_DK, LANES), st4),
                  pl.BlockSpec((bb, 1, LANES), st3)],
        out_specs=[pl.BlockSpec((bb, L, ML_HEADS * ML_DV), tok),
                   pl.BlockSpec((bb, ML_HEADS, ML_DK, LANES), st4),
                   pl.BlockSpec((bb, 1, LANES), st3)],
        out_shape=[SDS((B, T, ML_HEADS * ML_DV), f32),
                   SDS((B, ML_HEADS, ML_DK, LANES), f32),
                   SDS((B, 1, LANES), f32)],
        compiler_params=_cparams("parallel", "arbitrary"),
        name="mlstm",
    )(z_ml.reshape(B, T, ML_W), z_g.reshape(B, T, LANES), bias_row, norm_row, cn0, m0)


def _rope(x, cos, sin_signed):
    w = x.shape[-1]
    lane = lax.broadcasted_iota(jnp.int32, x.shape, 1)
    swapped = jnp.where((lane % SW_HD) < SW_HD // 2, pltpu.roll(x, w - SW_HD // 2, 1), pltpu.roll(x, SW_HD // 2, 1))
    return x * cos + swapped * sin_signed


def _swa_attend(jobs, sinks_ref, L):
    units = [(j, g) for j in range(len(jobs)) for g in range(SW_KV_HEADS)]
    sinks = [jnp.concatenate(
        [jnp.broadcast_to(sinks_ref[:, g * SW_GROUP + i:g * SW_GROUP + i + 1], (L, 1)) for i in range(SW_GROUP)],
        axis=0) for g in range(SW_KV_HEADS)]
    s, p = {}, {}
    for j, g in units:
        qr, keys, _, _ = jobs[j]
        q4 = jnp.concatenate([qr[:, (g * SW_GROUP + i) * SW_HD:(g * SW_GROUP + i + 1) * SW_HD]
                              for i in range(SW_GROUP)], axis=0)
        s[j, g] = _bdot_nt(q4, keys[:, g * SW_HD:(g + 1) * SW_HD]) * (SW_HD ** -0.5)
    for j, g in units:
        first_valid = jobs[j][3]
        sc = s[j, g]
        if first_valid is not None:
            kcol = lax.broadcasted_iota(jnp.int32, (1, sc.shape[1]), 1)
            sc = jnp.where(kcol >= first_valid, sc, -jnp.inf)
        mx = jnp.maximum(jnp.max(sc, axis=-1, keepdims=True), sinks[g])
        e = jnp.exp(sc - mx)
        p[j, g] = e / (jnp.sum(e, axis=-1, keepdims=True) + jnp.exp(sinks[g] - mx))
    o = {u: _bdot(p[u], jobs[u[0]][2][:, u[1] * SW_HD:(u[1] + 1) * SW_HD]) for u in units}
    return [jnp.concatenate([o[j, g][i * L:(i + 1) * L, :] for g in range(SW_KV_HEADS) for i in range(SW_GROUP)],
                            axis=-1) for j in range(len(jobs))]


def _swa_prompt_kernel(q_ref, kp_ref, kc_ref, vp_ref, vc_ref, cp_ref, cc_ref, sp_ref, sc_ref, sinks_ref,
                       o_ref, kr_ref, *, L, CB):
    i = pl.program_id(1)
    rows = CB * L
    back = 2 * L
    cos_q = jnp.concatenate([cc_ref[...]] * (SW_HEADS // SW_KV_HEADS), axis=-1)
    sin_q = jnp.concatenate([sc_ref[...]] * (SW_HEADS // SW_KV_HEADS), axis=-1)
    qr = _rope(q_ref[...], cos_q, sin_q)
    k_cur = _rope(kc_ref[...], cc_ref[...], sc_ref[...])
    kr_ref[...] = k_cur
    k_prev = _rope(kp_ref[rows - back:rows, :], cp_ref[rows - back:rows, :], sp_ref[rows - back:rows, :])
    keys = jnp.concatenate([k_prev, k_cur], axis=0)
    vals = jnp.concatenate([vp_ref[rows - back:rows, :], vc_ref[...]], axis=0)
    jobs = []
    for u in range(CB):
        first_valid = jnp.where(i == 0, back - u * L, 0) if u * L < back else None
        jobs.append((qr[u * L:(u + 1) * L], keys[u * L:(u + 3) * L], vals[u * L:(u + 3) * L], first_valid))
    o_ref[...] = jnp.concatenate(_swa_attend(jobs, sinks_ref, L), axis=0)


def _swa_prompt(z_sw, cos_t, sin_t, sinks_row, B, T, L):
    cb = 4
    rows = cb * L
    nb = T // rows
    n = B * T
    kcol = SW_HEADS * SW_HD // LANES
    vcol = kcol + 1
    cur = lambda b, i: (b * nb + i, 0)
    prev = lambda col: (lambda b, i: (b * nb + jnp.maximum(i - 1, 0), col))
    curc = lambda col: (lambda b, i: (b * nb + i, col))
    tab_cur = lambda b, i: (i, 0)
    tab_prev = lambda b, i: (jnp.maximum(i - 1, 0), 0)
    return pl.pallas_call(
        functools.partial(_swa_prompt_kernel, L=L, CB=cb),
        grid=(B, nb),
        in_specs=[pl.BlockSpec((rows, SW_HEADS * SW_HD), cur),
                  pl.BlockSpec((rows, LANES), prev(kcol)), pl.BlockSpec((rows, LANES), curc(kcol)),
                  pl.BlockSpec((rows, LANES), prev(vcol)), pl.BlockSpec((rows, LANES), curc(vcol)),
                  pl.BlockSpec((rows, LANES), tab_prev), pl.BlockSpec((rows, LANES), tab_cur),
                  pl.BlockSpec((rows, LANES), tab_prev), pl.BlockSpec((rows, LANES), tab_cur),
                  pl.BlockSpec((1, LANES), lambda b, i: (0, 0))],
        out_specs=[pl.BlockSpec((rows, SW_HEADS * SW_HD), cur),
                   pl.BlockSpec((rows, LANES), cur)],
        out_shape=[SDS((n, SW_HEADS * SW_HD), f32), SDS((n, LANES), f32)],
        compiler_params=_cparams("parallel", "parallel"),
        name="swa_prompt",
    )(z_sw, z_sw, z_sw, z_sw, z_sw, cos_t, cos_t, sin_t, sin_t, sinks_row)


def _swa_sample_kernel(q_ref, k_ref, v_ref, ck_ref, cv_ref, cos_ref, sin_ref, sinks_ref, o_ref, kr_ref, *, L):
    cos_q = jnp.concatenate([cos_ref[...]] * (SW_HEADS // SW_KV_HEADS), axis=-1)
    sin_q = jnp.concatenate([sin_ref[...]] * (SW_HEADS // SW_KV_HEADS), axis=-1)
    qr = _rope(q_ref[...], cos_q, sin_q)
    kr = _rope(k_ref[...], cos_ref[...], sin_ref[...])
    kr_ref[...] = kr
    keys = jnp.concatenate([ck_ref[0], kr], axis=0)
    vals = jnp.concatenate([cv_ref[0], v_ref[...]], axis=0)
    o_ref[...] = _swa_attend([(qr, keys, vals, None)], sinks_ref, L)[0]


def _swa_sample(z_sw, cache_k, cache_v, cos_t, sin_t, sinks_row, B, T):
    n = B * T
    kcol = SW_HEADS * SW_HD // LANES
    return pl.pallas_call(
        functools.partial(_swa_sample_kernel, L=T),
        grid=(B,),
        in_specs=[pl.BlockSpec((T, SW_HEADS * SW_HD), lambda b: (b, 0)),
                  pl.BlockSpec((T, LANES), lambda b: (b, kcol)),
                  pl.BlockSpec((T, LANES), lambda b: (b, kcol + 1)),
                  pl.BlockSpec((1, WINDOW, LANES), lambda b: (b, 0, 0)),
                  pl.BlockSpec((1, WINDOW, LANES), lambda b: (b, 0, 0)),
                  pl.BlockSpec((T, LANES), lambda b: (0, 0)),
                  pl.BlockSpec((T, LANES), lambda b: (0, 0)),
                  pl.BlockSpec((1, LANES), lambda b: (0, 0))],
        out_specs=[pl.BlockSpec((T, SW_HEADS * SW_HD), lambda b: (b, 0)),
                   pl.BlockSpec((T, LANES), lambda b: (b, 0))],
        out_shape=[SDS((n, SW_HEADS * SW_HD), f32), SDS((n, LANES), f32)],
        compiler_params=_cparams("parallel"),
        name="swa_sample",
    )(z_sw, z_sw, z_sw, cache_k, cache_v, cos_t, sin_t, sinks_row)


def _out_ln_kernel(*refs, n_in):
    x_ref = refs[0]
    a_refs = refs[1:1 + n_in]
    w_refs = refs[1 + n_in:1 + 2 * n_in]
    g_ref, b_ref, o_ref = refs[1 + 2 * n_in:]
    y = _bdot(a_refs[0][...], w_refs[0][...])
    for a_ref, w_ref in zip(a_refs[1:], w_refs[1:]):
        y = y + _bdot(a_ref[...], w_ref[...])
    o_ref[...] = _layer_norm(DN_ALPHA * x_ref[...] + y, g_ref[...], b_ref[...])


def _out_ln(x2, acts, ws, g_row, b_row, tm):
    n = x2.shape[0]
    row = lambda i: (i, 0)
    const = lambda i: (0, 0)
    return pl.pallas_call(
        functools.partial(_out_ln_kernel, n_in=len(acts)),
        grid=(n // tm,),
        in_specs=[pl.BlockSpec((tm, D_MODEL), row)]
        + [pl.BlockSpec((tm, a.shape[1]), row) for a in acts]
        + [pl.BlockSpec(w.shape, const) for w in ws]
        + [pl.BlockSpec((1, D_MODEL), const), pl.BlockSpec((1, D_MODEL), const)],
        out_specs=pl.BlockSpec((tm, D_MODEL), row),
        out_shape=SDS((n, D_MODEL), f32),
        compiler_params=_cparams("parallel"),
        name="out_proj_ln",
    )(x2, *acts, *ws, g_row, b_row)


def _router_kernel(x_ref, rw_ref, rb_ref, o_ref):
    logits = _dot_nt(rw_ref[...], x_ref[...])
    aff = _sigmoid(logits)
    sc = aff + rb_ref[...]
    s = [sc[e:e + 1, :] for e in range(N_EXPERTS)]
    a = [aff[e:e + 1, :] for e in range(N_EXPERTS)]
    scores = []
    for gi in range(N_GROUPS):
        w, x, y, z = s[4 * gi:4 * gi + 4]
        p, q = jnp.maximum(w, x), jnp.minimum(w, x)
        r, t = jnp.maximum(y, z), jnp.minimum(y, z)
        scores.append(jnp.maximum(p, r) + jnp.maximum(jnp.minimum(p, r), jnp.maximum(q, t)))
    best = scores[0]
    gsel = jnp.zeros_like(best, dtype=jnp.int32)
    for gi in range(1, N_GROUPS):
        better = scores[gi] > best
        best = jnp.where(better, scores[gi], best)
        gsel = jnp.where(better, gi, gsel)
    sel = []
    for e in range(N_EXPERTS):
        gi, i = divmod(e, EXP_PER_GROUP)
        beaten = jnp.zeros_like(gsel)
        for j in range(EXP_PER_GROUP):
            if j == i:
                continue
            o = s[4 * gi + j]
            wins = (o >= s[e]) if j < i else (o > s[e])
            beaten = beaten + wins.astype(jnp.int32)
        sel.append((gsel == gi) & (beaten < 2))
    den = jnp.zeros_like(best)
    for e in range(N_EXPERTS):
        den = den + jnp.where(sel[e], a[e], 0.0)
    o_ref[...] = jnp.concatenate([jnp.where(sel[e], a[e] / den, 0.0) for e in range(N_EXPERTS)], axis=0)


def _router(x2, rw_t, rb_col, tm):
    n = x2.shape[0]
    return pl.pallas_call(
        _router_kernel,
        grid=(n // tm,),
        in_specs=[pl.BlockSpec((tm, D_MODEL), lambda i: (i, 0)),
                  pl.BlockSpec((N_EXPERTS, D_MODEL), lambda i: (0, 0)),
                  pl.BlockSpec((N_EXPERTS, 1), lambda i: (0, 0))],
        out_specs=pl.BlockSpec((N_EXPERTS, tm), lambda i: (0, i)),
        out_shape=SDS((N_EXPERTS, n), f32),
        compiler_params=_cparams("parallel"),
        name="router",
    )(x2, rw_t, rb_col)


def _moe_kernel(x_ref, gates_ref, wg_ref, wu_ref, wd_ref, g_ref, b_ref, o_ref, xb_ref, acc_ref):
    e = pl.program_id(1)

    @pl.when(e == 0)
    def _():
        xb_ref[...] = x_ref[...].astype(bf16)
        acc_ref[...] = jnp.zeros_like(acc_ref)

    xb = xb_ref[...]
    lane = lax.broadcasted_iota(jnp.int32, gates_ref.shape, 1)
    gcol = jnp.sum(jnp.where(lane == e, gates_ref[...], 0.0), axis=-1, keepdims=True)
    h = _silu(jnp.dot(xb, wg_ref[0], preferred_element_type=f32)) * jnp.dot(xb, wu_ref[0], preferred_element_type=f32)
    acc_ref[...] += jnp.dot((gcol * h).astype(bf16), wd_ref[0], preferred_element_type=f32)

    @pl.when(e == N_EXPERTS - 1)
    def _():
        o_ref[...] = _layer_norm(DN_ALPHA * x_ref[...] + acc_ref[...], g_ref[...], b_ref[...])


def _moe_ln(x2, gates, wg, wu, wd, g_row, b_row, tm):
    n = x2.shape[0]
    row = lambda i, e: (i, 0)
    const = lambda i, e: (0, 0)
    return pl.pallas_call(
        _moe_kernel,
        grid=(n // tm, N_EXPERTS),
        in_specs=[pl.BlockSpec((tm, D_MODEL), row),
                  pl.BlockSpec((tm, N_EXPERTS), row),
                  pl.BlockSpec((1, D_MODEL, D_EXPERT), lambda i, e: (e, 0, 0)),
                  pl.BlockSpec((1, D_MODEL, D_EXPERT), lambda i, e: (e, 0, 0)),
                  pl.BlockSpec((1, D_EXPERT, D_MODEL), lambda i, e: (e, 0, 0)),
                  pl.BlockSpec((1, D_MODEL), const), pl.BlockSpec((1, D_MODEL), const)],
        out_specs=pl.BlockSpec((tm, D_MODEL), row),
        out_shape=SDS((n, D_MODEL), f32),
        scratch_shapes=[pltpu.VMEM((tm, D_MODEL), bf16), pltpu.VMEM((tm, D_MODEL), f32)],
        compiler_params=_cparams("parallel", "arbitrary"),
        name="moe_ln",
    )(x2, gates, wg, wu, wd, g_row, b_row)


def _split(a):
    hi = a.astype(bf16)
    return hi, (a - hi.astype(f32)).astype(bf16)


def _dot3(a, b):
    (ah, al), (bh, bl) = a, b
    mm = lambda x, y: jnp.dot(x, y, preferred_element_type=f32)
    return mm(ah, bh) + (mm(ah, bl) + mm(al, bh))


def _unit_lower_inverses(mats, L):
    row = lax.broadcasted_iota(jnp.int32, (L, L), 0)
    col = lax.broadcasted_iota(jnp.int32, (L, L), 1)
    eye = (row == col).astype(f32)
    ps = [eye - a for a in mats]
    pws = [_split(a) for a in mats]
    span = 2
    while span < L:
        pws = [_split(_dot3(pw, pw)) for pw in pws]
        ps = [p + _dot3(pw, _split(p)) for p, pw in zip(ps, pws)]
        span *= 2
    return ps


def _gdn_kernel(x_ref, zg_ref, ba_ref, cw_ref, alog_ref, dt_ref, nw_ref, s0_ref, cb_ref,
                o_ref, s_ref, prev_ref, *, L):
    @pl.when(pl.program_id(1) == 0)
    def _():
        s_ref[...] = s0_ref[...]
        prev_ref[...] = cb_ref[0]

    def conv_silu(lo, width):
        cur = x_ref[:, lo:lo + width]
        cat = jnp.concatenate([prev_ref[:, lo:lo + width], cur], axis=0)
        acc = cat[5:5 + L] * cw_ref[0:1, lo:lo + width]
        acc = acc + cat[6:6 + L] * cw_ref[1:2, lo:lo + width]
        acc = acc + cat[7:7 + L] * cw_ref[2:3, lo:lo + width]
        acc = acc + cur * cw_ref[3:4, lo:lo + width]
        return _silu(acc)

    def l2n(v):
        return v * lax.rsqrt(jnp.sum(v * v, axis=-1, keepdims=True) + 1e-6)

    ba = ba_ref[...]
    beta = _sigmoid(ba)
    gl = -jnp.exp(alog_ref[...]) * _softplus(ba + dt_ref[...])
    row = lax.broadcasted_iota(jnp.int32, (L, L), 0)
    col = lax.broadcasted_iota(jnp.int32, (L, L), 1)
    incl = row >= col
    strict = row > col
    gcum = _dot(incl.astype(f32), gl)
    gcum_t = gcum.T
    rep = GD_V_HEADS // GD_QK_HEADS
    qs, ks, amats, qkds, egs, g_cols, rhss = [], [], [], [], [], [], []
    for j in range(GD_QK_HEADS):
        q = l2n(conv_silu(j * GD_HD, GD_HD)) * (GD_HD ** -0.5)
        k = l2n(conv_silu(GD_QK_W + j * GD_HD, GD_HD))
        qb, kb = q.astype(bf16), k.astype(bf16)
        kk = lax.dot_general(kb, kb, (((1,), (1,)), ((), ())), preferred_element_type=f32)
        qk = lax.dot_general(qb, kb, (((1,), (1,)), ((), ())), preferred_element_type=f32)
        qs.append(qb)
        ks.append(k)
        for r in range(rep):
            hv = j * rep + r
            v = conv_silu(2 * GD_QK_W + hv * GD_HD, GD_HD)
            g_col = gcum[:, GD_V_HEADS + hv:GD_V_HEADS + hv + 1]
            g_row = gcum_t[GD_V_HEADS + hv:GD_V_HEADS + hv + 1, :]
            b_col = beta[:, hv:hv + 1]
            decay = jnp.exp(jnp.where(incl, g_col - g_row, -jnp.inf))
            eg = jnp.exp(g_col)
            amats.append(jnp.where(strict, b_col * kk * decay, 0.0))
            qkds.append((qk * decay).astype(bf16))
            egs.append(eg)
            g_cols.append(g_col)
            rhss.append(jnp.concatenate([b_col * v, (b_col * eg) * k], axis=-1))
    tinvs = _unit_lower_inverses(amats, L)
    sols = [_dot3(_split(t), _split(r)) for t, r in zip(tinvs, rhss)]
    for hv in range(GD_V_HEADS):
        j = hv // rep
        st = s_ref[0, hv]
        stb = st.astype(bf16)
        sol = sols[hv]
        wn = sol[:, :GD_HD] - jnp.dot(sol[:, GD_HD:].astype(bf16), stb, preferred_element_type=f32)
        wnb = wn.astype(bf16)
        o = egs[hv] * jnp.dot(qs[j], stb, preferred_element_type=f32) + jnp.dot(qkds[hv], wnb, preferred_element_type=f32)
        g_col = g_cols[hv]
        g_last = g_col[L - 1:L, :]
        kt = (jnp.exp(g_last - g_col) * ks[j]).T.astype(bf16)
        s_ref[0, hv] = jnp.exp(g_last) * st + jnp.dot(kt, wnb, preferred_element_type=f32)
        o = o * lax.rsqrt(jnp.mean(o * o, axis=-1, keepdims=True) + RMS_EPS) * nw_ref[...]
        o_ref[:, hv * GD_HD:(hv + 1) * GD_HD] = o * _silu(zg_ref[:, hv * GD_HD:(hv + 1) * GD_HD])
    prev_ref[...] = x_ref[L - 8:L, :]


def _gdn(qkv, zg, ba, conv_w, alog_row, dt_row, norm_row, s0, conv8, B, T, L):
    nc = T // L
    n = B * T
    tok = lambda b, c: (b * nc + c, 0)
    const = lambda b, c: (0, 0)
    return pl.pallas_call(
        functools.partial(_gdn_kernel, L=L),
        grid=(B, nc),
        in_specs=[pl.BlockSpec((L, GD_CONV_CH), tok),
                  pl.BlockSpec((L, GD_V_W), tok),
                  pl.BlockSpec((L, LANES), tok),
                  pl.BlockSpec((GD_CONV, GD_CONV_CH), const),
                  pl.BlockSpec((1, LANES), const),
                  pl.BlockSpec((1, LANES), const),
                  pl.BlockSpec((1, GD_HD), const),
                  pl.BlockSpec((1, GD_V_HEADS, GD_HD, GD_HD), lambda b, c: (b, 0, 0, 0)),
                  pl.BlockSpec((1, 8, GD_CONV_CH), lambda b, c: (b, 0, 0))],
        out_specs=[pl.BlockSpec((L, GD_V_W), tok),
                   pl.BlockSpec((1, GD_V_HEADS, GD_HD, GD_HD), lambda b, c: (b, 0, 0, 0))],
        out_shape=[SDS((n, GD_V_W), f32), SDS((B, GD_V_HEADS, GD_HD, GD_HD), f32)],
        scratch_shapes=[pltpu.VMEM((8, GD_CONV_CH), f32)],
        compiler_params=_cparams("parallel", "arbitrary"),
        name="gdn",
    )(qkv, zg, ba, conv_w, alog_row, dt_row, norm_row, s0, conv8)


def _pad_lanes(row, offset=0):
    return jnp.zeros((1, LANES), f32).at[0, offset:offset + row.shape[0]].set(row.astype(f32))


def _rope_tables(pos):
    half = SW_HD // 2
    inv = ROPE_THETA ** (-jnp.arange(half, dtype=f32) / half)
    ang = pos.astype(f32)[:, None] * inv[None, :]
    cos, sin = jnp.cos(ang), jnp.sin(ang)
    cos_t = jnp.concatenate([cos, cos] * SW_KV_HEADS, axis=-1)
    sin_t = jnp.concatenate([-sin, sin] * SW_KV_HEADS, axis=-1)
    return cos_t, sin_t


def _tile(n, pref):
    return pref if n % pref == 0 else n


def _trunk(x, pos, L, state, p):
    B, T, _ = x.shape
    n = B * T
    x2 = x.reshape(n, D_MODEL)
    tm = _tile(n, 512)

    z_ml, z_sw, z_g = _proj(x2, p["ab_w"], ((0, ML_W), (ML_W, SW_W), (ML_W + SW_W, LANES)), tm)
    if state is None:
        cn0 = jnp.zeros((B, ML_HEADS, ML_DK, LANES), f32)
        m0 = jnp.zeros((B, 1, LANES), f32)
    else:
        cn0 = jnp.concatenate([state["ml_C"], state["ml_n"][..., None],
                               jnp.zeros((B, ML_HEADS, ML_DK, LANES - ML_DV - 1), f32)], axis=-1)
        m0 = jnp.zeros((B, 1, LANES), f32).at[:, 0, :ML_HEADS].set(state["ml_m"])
    h_ml, cn, m_out = _mlstm(z_ml, z_g, p["ab_bias"], p["ab_norm"], cn0, m0, B, T, L)
    h_ml = h_ml.reshape(n, ML_HEADS * ML_DV)
    cos_t, sin_t = _rope_tables(pos)
    if state is None:
        a_sw, k_rot = _swa_prompt(z_sw, cos_t, sin_t, p["ab_sinks"], B, T, L)
    else:
        a_sw, k_rot = _swa_sample(z_sw, state["sw_k"].reshape(B, WINDOW, LANES),
                                  state["sw_v"].reshape(B, WINDOW, LANES), cos_t, sin_t, p["ab_sinks"], B, T)
    keep = min(T, WINDOW)
    new_k = k_rot.reshape(B, T, SW_KV_HEADS, SW_HD)[:, T - keep:]
    new_v = z_sw[:, SW_HEADS * SW_HD + LANES:].reshape(B, T, SW_KV_HEADS, SW_HD)[:, T - keep:]
    x2 = _out_ln(x2, [h_ml, a_sw], [p["ab_wo_h"], p["ab_wo_a"]], p["ln_g"][0][0], p["ln_b"][0][0], tm)
    x2 = _moe_block(x2, p, 0)

    tm1 = _tile(n, 256)
    qkv, zg, ba = _proj(x2, p["c_w"], ((0, GD_CONV_CH), (GD_CONV_CH, GD_V_W), (GD_CONV_CH + GD_V_W, LANES)), tm1)
    if state is None:
        s0 = jnp.zeros((B, GD_V_HEADS, GD_HD, GD_HD), f32)
        conv8 = jnp.zeros((B, 8, GD_CONV_CH), f32)
    else:
        s0 = state["gd_S"]
        conv8 = jnp.concatenate([jnp.zeros((B, 8 - (GD_CONV - 1), GD_CONV_CH), f32), state["gd_conv"]], axis=1)
    o_gd, s_out = _gdn(qkv, zg, ba, p["c_conv_w"], p["c_alog"], p["c_dt"], p["c_norm"], s0, conv8, B, T, L)
    new_conv = qkv.reshape(B, T, GD_CONV_CH)[:, T - (GD_CONV - 1):]
    x2 = _out_ln(x2, [o_gd], [p["c_wo"]], p["ln_g"][1][0], p["ln_b"][1][0], tm)
    x2 = _moe_block(x2, p, 1)

    outs = (new_k[None], new_v[None], cn[None, ..., :ML_DV], cn[None, ..., ML_DV], m_out[None, :, 0, :ML_HEADS],
            s_out[None], new_conv[None])
    return x2.reshape(B, T, D_MODEL), outs


def _moe_block(x2, p, layer):
    n = x2.shape[0]
    gates_t = _router(x2, p["router_wt"], p["router_b"], _tile(n, 512))
    return _moe_ln(x2, gates_t.T, p["ex_gate"][layer], p["ex_up"][layer], p["ex_down"][layer],
                   p["ln_g"][layer][1], p["ln_b"][layer][1], _tile(n, 1024))


def kernel(x_prompt, x_sample, cache_swa_k, cache_swa_v, state_mlstm_C, state_mlstm_n, state_mlstm_m, state_gdn_S, state_gdn_conv, ab_w_in, ab_b_i, ab_b_f, ab_norm, ab_sinks, ab_w_out, c_w_in, c_conv_w, c_a_log, c_dt_bias, c_norm, c_w_out, ln_g, ln_b, router_w, router_b, ex_gate, ex_up, ex_down):
    gate_lo = ML_W
    sw_lo = ML_W + 2 * ML_HEADS
    w0 = ab_w_in[0]
    ab_w = jnp.concatenate([w0[:, :gate_lo], w0[:, sw_lo:], w0[:, gate_lo:sw_lo],
                            jnp.zeros((D_MODEL, LANES - 2 * ML_HEADS), f32)], axis=1).astype(bf16)
    w1 = c_w_in[0]
    c_w = jnp.concatenate([w1, jnp.zeros((D_MODEL, LANES - 2 * GD_V_HEADS), f32)], axis=1).astype(bf16)
    wo = ab_w_out[0].astype(bf16)
    p = {
        "ab_w": ab_w,
        "ab_bias": _pad_lanes(jnp.concatenate([ab_b_i[0], ab_b_f[0]])),
        "ab_norm": ab_norm[0].reshape(1, ML_HEADS * ML_DV),
        "ab_sinks": _pad_lanes(ab_sinks[0]),
        "ab_wo_h": wo[:ML_HEADS * ML_DV],
        "ab_wo_a": wo[ML_HEADS * ML_DV:],
        "c_w": c_w,
        "c_conv_w": c_conv_w[0],
        "c_alog": _pad_lanes(c_a_log[0], GD_V_HEADS),
        "c_dt": _pad_lanes(c_dt_bias[0], GD_V_HEADS),
        "c_norm": c_norm[0].reshape(1, GD_HD),
        "c_wo": c_w_out[0].astype(bf16),
        "ln_g": [[ln_g[i, j].reshape(1, D_MODEL) for j in range(2)] for i in range(DEPTH)],
        "ln_b": [[ln_b[i, j].reshape(1, D_MODEL) for j in range(2)] for i in range(DEPTH)],
        "router_wt": router_w.T,
        "router_b": router_b.reshape(N_EXPERTS, 1),
        "ex_gate": ex_gate.astype(bf16),
        "ex_up": ex_up.astype(bf16),
        "ex_down": ex_down.astype(bf16),
    }
    t_p = x_prompt.shape[1]
    y_p, st_p = _trunk(x_prompt, jnp.arange(t_p, dtype=jnp.int32), CHUNK, None, p)
    t_s = x_sample.shape[1]
    state = {"sw_k": cache_swa_k[0], "sw_v": cache_swa_v[0], "ml_C": state_mlstm_C[0], "ml_n": state_mlstm_n[0],
             "ml_m": state_mlstm_m[0], "gd_S": state_gdn_S[0], "gd_conv": state_gdn_conv[0]}
    y_s, st_s = _trunk(x_sample, PAST_LEN + jnp.arange(t_s, dtype=jnp.int32), t_s, state, p)
    return (y_p, y_s) + st_p + st_s
```

```python
import functools
import math

import jax
import jax.numpy as jnp
from jax import lax
from jax.experimental import pallas as pl
from jax.experimental.pallas import tpu as pltpu

f32 = jnp.float32
bf16 = jnp.bfloat16
HIGHEST = lax.Precision.HIGHEST

D_MODEL = 1024
DEPTH = 2
CHUNK = 64
PAST_LEN = 2048
ML_HEADS = 8
ML_DK = 64
ML_DV = 64
SW_HEADS = 8
SW_KV_HEADS = 2
SW_HD = 64
SW_GROUP = SW_HEADS // SW_KV_HEADS
WINDOW = 128
ROPE_THETA = 10000.0
GD_QK_HEADS = 8
GD_V_HEADS = 16
GD_HD = 128
GD_CONV = 4
GD_QK_W = GD_QK_HEADS * GD_HD
GD_V_W = GD_V_HEADS * GD_HD
GD_CONV_CH = 2 * GD_QK_W + GD_V_W
N_EXPERTS = 16
N_GROUPS = 4
EXP_PER_GROUP = 4
D_EXPERT = 512
DN_ALPHA = (2 * DEPTH) ** 0.25
LN_EPS = 1e-5
RMS_EPS = 1e-6

LANES = 128
ML_W = 4 * ML_HEADS * ML_DK
SW_W = SW_HEADS * SW_HD + 2 * SW_KV_HEADS * SW_HD
VMEM_LIMIT = 56 * 1024 * 1024

SDS = jax.ShapeDtypeStruct


def _cparams(*sem):
    return pltpu.CompilerParams(dimension_semantics=sem, vmem_limit_bytes=VMEM_LIMIT)


def _dot(a, b):
    return jnp.dot(a, b, preferred_element_type=f32, precision=HIGHEST)


def _dot_nt(a, b):
    return lax.dot_general(a, b, (((1,), (1,)), ((), ())), preferred_element_type=f32, precision=HIGHEST)


def _bdot(a, b):
    return jnp.dot(a.astype(bf16), b.astype(bf16), preferred_element_type=f32)


def _bdot_nt(a, b):
    return lax.dot_general(a.astype(bf16), b.astype(bf16), (((1,), (1,)), ((), ())), preferred_element_type=f32)


def _bdot_tn(a, b):
    return jnp.dot(a.T.astype(bf16), b.astype(bf16), preferred_element_type=f32)


def _sigmoid(x):
    return 1.0 / (1.0 + jnp.exp(-x))


def _silu(x):
    return x * _sigmoid(x)


def _softplus(x):
    return jnp.maximum(x, 0.0) + jnp.log(1.0 + jnp.exp(-jnp.abs(x)))


def _layer_norm(v, g, b):
    mu = jnp.mean(v, axis=-1, keepdims=True)
    d = v - mu
    var = jnp.mean(d * d, axis=-1, keepdims=True)
    return d * lax.rsqrt(var + LN_EPS) * g + b


def _proj_kernel(x_ref, w_ref, *o_refs, splits, col_chunk):
    xb = x_ref[...].astype(bf16)
    for o_ref, (start, width) in zip(o_refs, splits):
        for c in range(0, width, col_chunk):
            cw = min(col_chunk, width - c)
            o_ref[:, c:c + cw] = jnp.dot(xb, w_ref[:, start + c:start + c + cw], preferred_element_type=f32)


def _proj(x2, w, splits, tm):
    n, k = x2.shape
    return pl.pallas_call(
        functools.partial(_proj_kernel, splits=splits, col_chunk=512),
        grid=(n // tm,),
        in_specs=[pl.BlockSpec((tm, k), lambda i: (i, 0)),
                  pl.BlockSpec(w.shape, lambda i: (0, 0), pipeline_mode=pl.Buffered(1))],
        out_specs=[pl.BlockSpec((tm, wd), lambda i: (i, 0)) for _, wd in splits],
        out_shape=[SDS((n, wd), f32) for _, wd in splits],
        compiler_params=_cparams("parallel"),
        name="in_proj",
    )(x2, w)


def _mlstm_kernel(z_ref, g_ref, bias_ref, nw_ref, cn0_ref, m0_ref, h_ref, cn_ref, m_ref, *, L, BB):
    @pl.when(pl.program_id(1) == 0)
    def _():
        cn_ref[...] = cn0_ref[...]
        m_ref[...] = m0_ref[...]

    row = lax.broadcasted_iota(jnp.int32, (L, L), 0)
    col = lax.broadcasted_iota(jnp.int32, (L, L), 1)
    causal = row >= col
    tri = causal.astype(f32)
    lane = lax.broadcasted_iota(jnp.int32, (1, LANES), 1)
    lane_l = lax.broadcasted_iota(jnp.int32, (L, ML_DV), 1)
    one_hot0 = (lane_l == 0).astype(f32)
    for bi in range(BB):
        g = g_ref[bi] + bias_ref[...]
        lf = jnp.minimum(g, 0.0) - jnp.log(1.0 + jnp.exp(-jnp.abs(g)))
        bcum = _dot(tri, lf)
        b_t = bcum.T
        g_t = g.T
        m_row = m_ref[bi]
        new_m = m_row
        outs = []
        for h in range(ML_HEADS):
            b_col = bcum[:, ML_HEADS + h:ML_HEADS + h + 1]
            b_row = b_t[ML_HEADS + h:ML_HEADS + h + 1, :]
            ig_row = g_t[h:h + 1, :]
            ig_col = g[:, h:h + 1]
            m_h = m_row[:, h:h + 1]
            dmat = jnp.where(causal, b_col - b_row + ig_row, -jnp.inf)
            inter = b_col + m_h
            mt = jnp.maximum(inter, jnp.max(dmat, axis=-1, keepdims=True))
            a = jnp.exp(inter - mt)
            q = z_ref[bi, :, h * ML_DK:(h + 1) * ML_DK]
            k = z_ref[bi, :, ML_HEADS * ML_DK + h * ML_DK:ML_HEADS * ML_DK + (h + 1) * ML_DK] * (ML_DK ** -0.5)
            v = z_ref[bi, :, 2 * ML_HEADS * ML_DK + h * ML_DV:2 * ML_HEADS * ML_DK + (h + 1) * ML_DV]
            og = z_ref[bi, :, 3 * ML_HEADS * ML_DK + h * ML_DV:3 * ML_HEADS * ML_DK + (h + 1) * ML_DV]
            s = _bdot_nt(q, k) * jnp.exp(dmat - mt)
            vext = jnp.concatenate([v, one_hot0], axis=-1)
            cn = cn_ref[bi, h]
            tot = a * _bdot(q, cn) + _bdot(s, vext)
            num = tot[:, :ML_DV]
            den = tot[:, ML_DV:ML_DV + 1]
            hh = num / jnp.maximum(jnp.abs(den), jnp.exp(-mt))
            hh = hh * lax.rsqrt(jnp.mean(hh * hh, axis=-1, keepdims=True) + RMS_EPS) * nw_ref[:, h * ML_DV:(h + 1) * ML_DV]
            outs.append(hh * _sigmoid(og))
            m_new = mt[L - 1:L, :]
            b_last = b_col[L - 1:L, :]
            wk = jnp.exp(b_last - b_col + ig_col - m_new)
            dec = jnp.exp(b_last + m_h - m_new)
            cn_ref[bi, h] = dec * cn + _bdot_tn(k, wk * vext)
            new_m = jnp.where(lane == h, m_new, new_m)
        m_ref[bi] = new_m
        h_ref[bi] = jnp.concatenate(outs, axis=-1)


def _mlstm(z_ml, z_g, bias_row, norm_row, cn0, m0, B, T, L):
    nc = T // L
    bb = min(B, 4)
    tok = lambda b, c: (b, c, 0)
    st4 = lambda b, c: (b, 0, 0, 0)
    st3 = lambda b, c: (b, 0, 0)
    return pl.pallas_call(
        functools.partial(_mlstm_kernel, L=L, BB=bb),
        grid=(B // bb, nc),
        in_specs=[pl.BlockSpec((bb, L, ML_W), tok),
                  pl.BlockSpec((bb, L, LANES), tok),
                  pl.BlockSpec((1, LANES), lambda b, c: (0, 0)),
                  pl.BlockSpec((1, ML_HEADS * ML_DV), lambda b, c: (0, 0)),
                  pl.BlockSpec((bb, ML_HEADS, ML_DK, LANES), st4),
                  pl.BlockSpec((bb, 1, LANES), st3)],
        out_specs=[pl.BlockSpec((bb, L, ML_HEADS * ML_DV), tok),
                   pl.BlockSpec((bb, ML_HEADS, ML_DK, LANES), st4),
                   pl.BlockSpec((bb, 1, LANES), st3)],
        out_shape=[SDS((B, T, ML_HEADS * ML_DV), f32),
                   SDS((B, ML_HEADS, ML_DK, LANES), f32),
                   SDS((B, 1, LANES), f32)],
        compiler_params=_cparams("parallel", "arbitrary"),
        name="mlstm",
    )(z_ml.reshape(B, T, ML_W), z_g.reshape(B, T, LANES), bias_row, norm_row, cn0, m0)


def _rope(x, cos, sin_signed):
    w = x.shape[-1]
    lane = lax.broadcasted_iota(jnp.int32, x.shape, 1)
    swapped = jnp.where((lane % SW_HD) < SW_HD // 2, pltpu.roll(x, w - SW_HD // 2, 1), pltpu.roll(x, SW_HD // 2, 1))
    return x * cos + swapped * sin_signed


def _swa_attend(jobs, sinks_ref, L):
    units = [(j, g) for j in range(len(jobs)) for g in range(SW_KV_HEADS)]
    sinks = [jnp.concatenate(
        [jnp.broadcast_to(sinks_ref[:, g * SW_GROUP + i:g * SW_GROUP + i + 1], (L, 1)) for i in range(SW_GROUP)],
        axis=0) for g in range(SW_KV_HEADS)]
    s, p = {}, {}
    for j, g in units:
        qr, keys, _, _ = jobs[j]
        q4 = jnp.concatenate([qr[:, (g * SW_GROUP + i) * SW_HD:(g * SW_GROUP + i + 1) * SW_HD]
                              for i in range(SW_GROUP)], axis=0)
        s[j, g] = _bdot_nt(q4, keys[:, g * SW_HD:(g + 1) * SW_HD]) * (SW_HD ** -0.5)
    for j, g in units:
        first_valid = jobs[j][3]
        sc = s[j, g]
        if first_valid is not None:
            kcol = lax.broadcasted_iota(jnp.int32, (1, sc.shape[1]), 1)
            sc = jnp.where(kcol >= first_valid, sc, -jnp.inf)
        mx = jnp.maximum(jnp.max(sc, axis=-1, keepdims=True), sinks[g])
        e = jnp.exp(sc - mx)
        p[j, g] = e / (jnp.sum(e, axis=-1, keepdims=True) + jnp.exp(sinks[g] - mx))
    o = {u: _bdot(p[u], jobs[u[0]][2][:, u[1] * SW_HD:(u[1] + 1) * SW_HD]) for u in units}
    return [jnp.concatenate([o[j, g][i * L:(i + 1) * L, :] for g in range(SW_KV_HEADS) for i in range(SW_GROUP)],
                            axis=-1) for j in range(len(jobs))]


def _swa_prompt_kernel(q_ref, kp_ref, kc_ref, vp_ref, vc_ref, cp_ref, cc_ref, sp_ref, sc_ref, sinks_ref,
                       o_ref, kr_ref, *, L, CB):
    i = pl.program_id(1)
    rows = CB * L
    back = 2 * L
    cos_q = jnp.concatenate([cc_ref[...]] * (SW_HEADS // SW_KV_HEADS), axis=-1)
    sin_q = jnp.concatenate([sc_ref[...]] * (SW_HEADS // SW_KV_HEADS), axis=-1)
    qr = _rope(q_ref[...], cos_q, sin_q)
    k_cur = _rope(kc_ref[...], cc_ref[...], sc_ref[...])
    kr_ref[...] = k_cur
    k_prev = _rope(kp_ref[rows - back:rows, :], cp_ref[rows - back:rows, :], sp_ref[rows - back:rows, :])
    keys = jnp.concatenate([k_prev, k_cur], axis=0)
    vals = jnp.concatenate([vp_ref[rows - back:rows, :], vc_ref[...]], axis=0)
    jobs = []
    for u in range(CB):
        first_valid = jnp.where(i == 0, back - u * L, 0) if u * L < back else None
        jobs.append((qr[u * L:(u + 1) * L], keys[u * L:(u + 3) * L], vals[u * L:(u + 3) * L], first_valid))
    o_ref[...] = jnp.concatenate(_swa_attend(jobs, sinks_ref, L), axis=0)


def _swa_prompt(z_sw, cos_t, sin_t, sinks_row, B, T, L):
    cb = 4
    rows = cb * L
    nb = T // rows
    n = B * T
    kcol = SW_HEADS * SW_HD // LANES
    vcol = kcol + 1
    cur = lambda b, i: (b * nb + i, 0)
    prev = lambda col: (lambda b, i: (b * nb + jnp.maximum(i - 1, 0), col))
    curc = lambda col: (lambda b, i: (b * nb + i, col))
    tab_cur = lambda b, i: (i, 0)
    tab_prev = lambda b, i: (jnp.maximum(i - 1, 0), 0)
    return pl.pallas_call(
        functools.partial(_swa_prompt_kernel, L=L, CB=cb),
        grid=(B, nb),
        in_specs=[pl.BlockSpec((rows, SW_HEADS * SW_HD), cur),
                  pl.BlockSpec((rows, LANES), prev(kcol)), pl.BlockSpec((rows, LANES), curc(kcol)),
                  pl.BlockSpec((rows, LANES), prev(vcol)), pl.BlockSpec((rows, LANES), curc(vcol)),
                  pl.BlockSpec((rows, LANES), tab_prev), pl.BlockSpec((rows, LANES), tab_cur),
                  pl.BlockSpec((rows, LANES), tab_prev), pl.BlockSpec((rows, LANES), tab_cur),
                  pl.BlockSpec((1, LANES), lambda b, i: (0, 0))],
        out_specs=[pl.BlockSpec((rows, SW_HEADS * SW_HD), cur),
                   pl.BlockSpec((rows, LANES), cur)],
        out_shape=[SDS((n, SW_HEADS * SW_HD), f32), SDS((n, LANES), f32)],
        compiler_params=_cparams("parallel", "parallel"),
        name="swa_prompt",
    )(z_sw, z_sw, z_sw, z_sw, z_sw, cos_t, cos_t, sin_t, sin_t, sinks_row)


def _swa_sample_kernel(q_ref, k_ref, v_ref, ck_ref, cv_ref, cos_ref, sin_ref, sinks_ref, o_ref, kr_ref, *, L):
    cos_q = jnp.concatenate([cos_ref[...]] * (SW_HEADS // SW_KV_HEADS), axis=-1)
    sin_q = jnp.concatenate([sin_ref[...]] * (SW_HEADS // SW_KV_HEADS), axis=-1)
    qr = _rope(q_ref[...], cos_q, sin_q)
    kr = _rope(k_ref[...], cos_ref[...], sin_ref[...])
    kr_ref[...] = kr
    keys = jnp.concatenate([ck_ref[0], kr], axis=0)
    vals = jnp.concatenate([cv_ref[0], v_ref[...]], axis=0)
    o_ref[...] = _swa_attend([(qr, keys, vals, None)], sinks_ref, L)[0]


def _swa_sample(z_sw, cache_k, cache_v, cos_t, sin_t, sinks_row, B, T):
    n = B * T
    kcol = SW_HEADS * SW_HD // LANES
    return pl.pallas_call(
        functools.partial(_swa_sample_kernel, L=T),
        grid=(B,),
        in_specs=[pl.BlockSpec((T, SW_HEADS * SW_HD), lambda b: (b, 0)),
                  pl.BlockSpec((T, LANES), lambda b: (b, kcol)),
                  pl.BlockSpec((T, LANES), lambda b: (b, kcol + 1)),
                  pl.BlockSpec((1, WINDOW, LANES), lambda b: (b, 0, 0)),
                  pl.BlockSpec((1, WINDOW, LANES), lambda b: (b, 0, 0)),
                  pl.BlockSpec((T, LANES), lambda b: (0, 0)),
                  pl.BlockSpec((T, LANES), lambda b: (0, 0)),
                  pl.BlockSpec((1, LANES), lambda b: (0, 0))],
        out_specs=[pl.BlockSpec((T, SW_HEADS * SW_HD), lambda b: (b, 0)),
                   pl.BlockSpec((T, LANES), lambda b: (b, 0))],
        out_shape=[SDS((n, SW_HEADS * SW_HD), f32), SDS((n, LANES), f32)],
        compiler_params=_cparams("parallel"),
        name="swa_sample",
    )(z_sw, z_sw, z_sw, cache_k, cache_v, cos_t, sin_t, sinks_row)


def _out_ln_kernel(*refs, n_in):
    x_ref = refs[0]
    a_refs = refs[1:1 + n_in]
    w_refs = refs[1 + n_in:1 + 2 * n_in]
    g_ref, b_ref, o_ref = refs[1 + 2 * n_in:]
    y = _bdot(a_refs[0][...], w_refs[0][...])
    for a_ref, w_ref in zip(a_refs[1:], w_refs[1:]):
        y = y + _bdot(a_ref[...], w_ref[...])
    o_ref[...] = _layer_norm(DN_ALPHA * x_ref[...] + y, g_ref[...], b_ref[...])


def _out_ln(x2, acts, ws, g_row, b_row, tm):
    n = x2.shape[0]
    row = lambda i: (i, 0)
    const = lambda i: (0, 0)
    return pl.pallas_call(
        functools.partial(_out_ln_kernel, n_in=len(acts)),
        grid=(n // tm,),
        in_specs=[pl.BlockSpec((tm, D_MODEL), row)]
        + [pl.BlockSpec((tm, a.shape[1]), row) for a in acts]
        + [pl.BlockSpec(w.shape, const) for w in ws]
        + [pl.BlockSpec((1, D_MODEL), const), pl.BlockSpec((1, D_MODEL), const)],
        out_specs=pl.BlockSpec((tm, D_MODEL), row),
        out_shape=SDS((n, D_MODEL), f32),
        compiler_params=_cparams("parallel"),
        name="out_proj_ln",
    )(x2, *acts, *ws, g_row, b_row)


def _router_kernel(x_ref, rw_ref, rb_ref, o_ref, route_ref):
    logits = _dot_nt(rw_ref[...], x_ref[...])
    aff = _sigmoid(logits)
    sc = aff + rb_ref[...]
    s = [sc[e:e + 1, :] for e in range(N_EXPERTS)]
    a = [aff[e:e + 1, :] for e in range(N_EXPERTS)]
    scores = []
    for gi in range(N_GROUPS):
        w, x, y, z = s[4 * gi:4 * gi + 4]
        p, q = jnp.maximum(w, x), jnp.minimum(w, x)
        r, t = jnp.maximum(y, z), jnp.minimum(y, z)
        scores.append(jnp.maximum(p, r) + jnp.maximum(jnp.minimum(p, r), jnp.maximum(q, t)))
    best = scores[0]
    gsel = jnp.zeros_like(best, dtype=jnp.int32)
    for gi in range(1, N_GROUPS):
        better = scores[gi] > best
        best = jnp.where(better, scores[gi], best)
        gsel = jnp.where(better, gi, gsel)
    sel = []
    for e in range(N_EXPERTS):
        gi, i = divmod(e, EXP_PER_GROUP)
        beaten = jnp.zeros_like(gsel)
        for j in range(EXP_PER_GROUP):
            if j == i:
                continue
            o = s[4 * gi + j]
            wins = (o >= s[e]) if j < i else (o > s[e])
            beaten = beaten + wins.astype(jnp.int32)
        sel.append((gsel == gi) & (beaten < 2))
    den = jnp.zeros_like(best)
    for e in range(N_EXPERTS):
        den = den + jnp.where(sel[e], a[e], 0.0)
    gate = [jnp.where(sel[e], a[e] / den, 0.0) for e in range(N_EXPERTS)]
    o_ref[...] = jnp.concatenate(gate, axis=0)
    taken = jnp.zeros_like(gsel)
    ea = eb = wa = wb = jnp.zeros_like(best)
    for e in range(N_EXPERTS):
        first = sel[e] & (taken == 0)
        second = sel[e] & (taken == 1)
        ea = jnp.where(first, float(e), ea)
        wa = jnp.where(first, gate[e], wa)
        eb = jnp.where(second, float(e), eb)
        wb = jnp.where(second, gate[e], wb)
        taken = taken + sel[e].astype(jnp.int32)
    route_ref[...] = jnp.concatenate([ea, eb, wa, wb, jnp.zeros((4, ea.shape[1]), f32)], axis=0)


def _router(x2, rw_t, rb_col, tm):
    n = x2.shape[0]
    return pl.pallas_call(
        _router_kernel,
        grid=(n // tm,),
        in_specs=[pl.BlockSpec((tm, D_MODEL), lambda i: (i, 0)),
                  pl.BlockSpec((N_EXPERTS, D_MODEL), lambda i: (0, 0)),
                  pl.BlockSpec((N_EXPERTS, 1), lambda i: (0, 0))],
        out_specs=[pl.BlockSpec((N_EXPERTS, tm), lambda i: (0, i)),
                   pl.BlockSpec((8, tm), lambda i: (0, i))],
        out_shape=[SDS((N_EXPERTS, n), f32), SDS((8, n), f32)],
        compiler_params=_cparams("parallel"),
        name="router",
    )(x2, rw_t, rb_col)


def _moe_kernel(x_ref, gates_ref, wg_ref, wu_ref, wd_ref, g_ref, b_ref, o_ref, xb_ref, acc_ref):
    e = pl.program_id(1)

    @pl.when(e == 0)
    def _():
        xb_ref[...] = x_ref[...].astype(bf16)
        acc_ref[...] = jnp.zeros_like(acc_ref)

    xb = xb_ref[...]
    lane = lax.broadcasted_iota(jnp.int32, gates_ref.shape, 1)
    gcol = jnp.sum(jnp.where(lane == e, gates_ref[...], 0.0), axis=-1, keepdims=True)
    h = _silu(jnp.dot(xb, wg_ref[0], preferred_element_type=f32)) * jnp.dot(xb, wu_ref[0], preferred_element_type=f32)
    acc_ref[...] += jnp.dot((gcol * h).astype(bf16), wd_ref[0], preferred_element_type=f32)

    @pl.when(e == N_EXPERTS - 1)
    def _():
        o_ref[...] = _layer_norm(DN_ALPHA * x_ref[...] + acc_ref[...], g_ref[...], b_ref[...])


def _moe_ln(x2, gates, wg, wu, wd, g_row, b_row, tm):
    n = x2.shape[0]
    row = lambda i, e: (i, 0)
    const = lambda i, e: (0, 0)
    return pl.pallas_call(
        _moe_kernel,
        grid=(n // tm, N_EXPERTS),
        in_specs=[pl.BlockSpec((tm, D_MODEL), row),
                  pl.BlockSpec((tm, N_EXPERTS), row),
                  pl.BlockSpec((1, D_MODEL, D_EXPERT), lambda i, e: (e, 0, 0)),
                  pl.BlockSpec((1, D_MODEL, D_EXPERT), lambda i, e: (e, 0, 0)),
                  pl.BlockSpec((1, D_EXPERT, D_MODEL), lambda i, e: (e, 0, 0)),
                  pl.BlockSpec((1, D_MODEL), const), pl.BlockSpec((1, D_MODEL), const)],
        out_specs=pl.BlockSpec((tm, D_MODEL), row),
        out_shape=SDS((n, D_MODEL), f32),
        scratch_shapes=[pltpu.VMEM((tm, D_MODEL), bf16), pltpu.VMEM((tm, D_MODEL), f32)],
        compiler_params=_cparams("parallel", "arbitrary"),
        name="moe_ln",
    )(x2, gates, wg, wu, wd, g_row, b_row)


N_PAIRS = N_GROUPS * (EXP_PER_GROUP * (EXP_PER_GROUP - 1) // 2)
MOE_TM = 256
_PAIR_A = [g * EXP_PER_GROUP + a for g in range(N_GROUPS) for a in range(EXP_PER_GROUP) for b in range(a + 1, EXP_PER_GROUP)]
_PAIR_B = [g * EXP_PER_GROUP + b for g in range(N_GROUPS) for a in range(EXP_PER_GROUP) for b in range(a + 1, EXP_PER_GROUP)]


def _gather_rows_kernel(idx_ref, src_ref, o_ref, sem, *, rows):
    base = pl.program_id(0) * rows

    def row_copy(j):
        return pltpu.make_async_copy(src_ref.at[pl.ds(idx_ref[base + j], 1)], o_ref.at[pl.ds(j, 1)], sem)

    def issue(j, carry):
        row_copy(j).start()
        return carry

    def drain(j, carry):
        row_copy(j).wait()
        return carry

    lax.fori_loop(0, rows, issue, 0, unroll=8)
    lax.fori_loop(0, rows, drain, 0, unroll=8)


def _gather_rows(src, idx, rows):
    n_out = idx.shape[0]
    d = src.shape[1]
    return pl.pallas_call(
        functools.partial(_gather_rows_kernel, rows=rows),
        grid_spec=pltpu.PrefetchScalarGridSpec(
            num_scalar_prefetch=1,
            grid=(n_out // rows,),
            in_specs=[pl.BlockSpec(memory_space=pl.ANY)],
            out_specs=pl.BlockSpec((rows, d), lambda i, idx_ref: (i, 0)),
            scratch_shapes=[pltpu.SemaphoreType.DMA(())]),
        out_shape=SDS((n_out, d), f32),
        compiler_params=_cparams("arbitrary"),
        name="gather_rows",
    )(idx, src)


def _pair_plan(route, n, tm):
    ea = route[0].astype(jnp.int32)
    eb = route[1].astype(jnp.int32)
    a = ea % EXP_PER_GROUP
    b = eb % EXP_PER_GROUP
    pidx = jnp.where(a == 0, b - 1, jnp.where(a == 1, b + 1, 5))
    pair = (ea // EXP_PER_GROUP) * (N_PAIRS // N_GROUPS) + pidx
    onehot = (pair[:, None] == jnp.arange(N_PAIRS, dtype=jnp.int32)[None, :]).astype(jnp.int32)
    csum = jnp.cumsum(onehot, axis=0)
    counts = csum[-1]
    rank = jnp.take_along_axis(csum, pair[:, None], axis=1)[:, 0] - 1
    ntiles = (counts + tm - 1) // tm
    tile_end = jnp.cumsum(ntiles)
    tile_start = tile_end - ntiles
    nt = n // tm + N_PAIRS
    tile_id = jnp.arange(nt, dtype=jnp.int32)
    tile_valid = tile_id < tile_end[-1]
    last_pair = jnp.searchsorted(tile_end, tile_end[-1] - 1, side="right").astype(jnp.int32)
    tile_pair = jnp.where(tile_valid, jnp.searchsorted(tile_end, tile_id, side="right").astype(jnp.int32), last_pair)
    tile_pair = jnp.minimum(tile_pair, N_PAIRS - 1)
    order = jnp.argsort(pair, stable=True).astype(jnp.int32)
    list_start = jnp.cumsum(counts) - counts
    row = jnp.arange(nt * tm, dtype=jnp.int32)
    rp = tile_pair[row // tm]
    j = row - tile_start[rp] * tm
    row_valid = tile_valid[row // tm] & (j < counts[rp])
    token_of_row = jnp.where(row_valid, order[jnp.clip(list_start[rp] + j, 0, n - 1)], 0)
    row_of_token = tile_start[pair] * tm + rank
    w_rows = jnp.stack([route[2], route[3]], axis=1)[token_of_row]
    tile_a = jnp.asarray(_PAIR_A, jnp.int32)[tile_pair]
    tile_b = jnp.asarray(_PAIR_B, jnp.int32)[tile_pair]
    return token_of_row, row_of_token, w_rows, tile_a, tile_b, tile_valid.astype(jnp.int32)


def _pair_expert_kernel(ta_ref, tb_ref, tv_ref, x_ref, w_ref, wga_ref, wua_ref, wda_ref, wgb_ref, wub_ref, wdb_ref,
                        g_ref, b_ref, o_ref):
    valid = tv_ref[pl.program_id(0)] == 1

    @pl.when(valid)
    def _():
        x = x_ref[...]
        xb = x.astype(bf16)
        w = w_ref[...]
        acc = None
        for col, (wg, wu, wd) in enumerate(((wga_ref, wua_ref, wda_ref), (wgb_ref, wub_ref, wdb_ref))):
            h = _silu(jnp.dot(xb, wg[0], preferred_element_type=f32)) * jnp.dot(xb, wu[0], preferred_element_type=f32)
            y = jnp.dot((w[:, col:col + 1] * h).astype(bf16), wd[0], preferred_element_type=f32)
            acc = y if acc is None else acc + y
        o_ref[...] = _layer_norm(DN_ALPHA * x + acc, g_ref[...], b_ref[...])

    @pl.when(jnp.logical_not(valid))
    def _():
        o_ref[...] = jnp.zeros_like(o_ref)


def _pair_experts(xs, w_rows, tile_a, tile_b, tile_valid, wg, wu, wd, g_row, b_row, tm):
    rows = xs.shape[0]
    row = lambda i, ta, tb, tv: (i, 0)
    const = lambda i, ta, tb, tv: (0, 0)
    ex_a = lambda i, ta, tb, tv: (ta[i], 0, 0)
    ex_b = lambda i, ta, tb, tv: (tb[i], 0, 0)
    up = pl.BlockSpec((1, D_MODEL, D_EXPERT), ex_a), pl.BlockSpec((1, D_MODEL, D_EXPERT), ex_b)
    down = pl.BlockSpec((1, D_EXPERT, D_MODEL), ex_a), pl.BlockSpec((1, D_EXPERT, D_MODEL), ex_b)
    return pl.pallas_call(
        _pair_expert_kernel,
        grid_spec=pltpu.PrefetchScalarGridSpec(
            num_scalar_prefetch=3,
            grid=(rows // tm,),
            in_specs=[pl.BlockSpec((tm, D_MODEL), row), pl.BlockSpec((tm, 2), row),
                      up[0], up[0], down[0], up[1], up[1], down[1],
                      pl.BlockSpec((1, D_MODEL), const), pl.BlockSpec((1, D_MODEL), const)],
            out_specs=pl.BlockSpec((tm, D_MODEL), row)),
        out_shape=SDS((rows, D_MODEL), f32),
        compiler_params=_cparams("arbitrary"),
        name="pair_experts",
    )(tile_a, tile_b, tile_valid, xs, w_rows, wg, wu, wd, wg, wu, wd, g_row, b_row)


def _split(a):
    hi = a.astype(bf16)
    return hi, (a - hi.astype(f32)).astype(bf16)


def _dot3(a, b):
    (ah, al), (bh, bl) = a, b
    mm = lambda x, y: jnp.dot(x, y, preferred_element_type=f32)
    return mm(ah, bh) + (mm(ah, bl) + mm(al, bh))


def _unit_lower_inverses(mats, L):
    row = lax.broadcasted_iota(jnp.int32, (L, L), 0)
    col = lax.broadcasted_iota(jnp.int32, (L, L), 1)
    eye = (row == col).astype(f32)
    ps = [eye - a for a in mats]
    pws = [_split(a) for a in mats]
    span = 2
    while span < L:
        pws = [_split(_dot3(pw, pw)) for pw in pws]
        ps = [p + _dot3(pw, _split(p)) for p, pw in zip(ps, pws)]
        span *= 2
    return ps


def _gdn_kernel(x_ref, zg_ref, ba_ref, cw_ref, alog_ref, dt_ref, nw_ref, s0_ref, cb_ref,
                o_ref, s_ref, prev_ref, *, L):
    @pl.when(pl.program_id(1) == 0)
    def _():
        s_ref[...] = s0_ref[...]
        prev_ref[...] = cb_ref[0]

    def conv_silu(lo, width):
        cur = x_ref[:, lo:lo + width]
        cat = jnp.concatenate([prev_ref[:, lo:lo + width], cur], axis=0)
        acc = cat[5:5 + L] * cw_ref[0:1, lo:lo + width]
        acc = acc + cat[6:6 + L] * cw_ref[1:2, lo:lo + width]
        acc = acc + cat[7:7 + L] * cw_ref[2:3, lo:lo + width]
        acc = acc + cur * cw_ref[3:4, lo:lo + width]
        return _silu(acc)

    def l2n(v):
        return v * lax.rsqrt(jnp.sum(v * v, axis=-1, keepdims=True) + 1e-6)

    ba = ba_ref[...]
    beta = _sigmoid(ba)
    gl = -jnp.exp(alog_ref[...]) * _softplus(ba + dt_ref[...])
    row = lax.broadcasted_iota(jnp.int32, (L, L), 0)
    col = lax.broadcasted_iota(jnp.int32, (L, L), 1)
    incl = row >= col
    strict = row > col
    gcum = _dot(incl.astype(f32), gl)
    gcum_t = gcum.T
    rep = GD_V_HEADS // GD_QK_HEADS
    qs, ks, amats, qkds, egs, g_cols, rhss = [], [], [], [], [], [], []
    for j in range(GD_QK_HEADS):
        q = l2n(conv_silu(j * GD_HD, GD_HD)) * (GD_HD ** -0.5)
        k = l2n(conv_silu(GD_QK_W + j * GD_HD, GD_HD))
        qb, kb = q.astype(bf16), k.astype(bf16)
        kk = lax.dot_general(kb, kb, (((1,), (1,)), ((), ())), preferred_element_type=f32)
        qk = lax.dot_general(qb, kb, (((1,), (1,)), ((), ())), preferred_element_type=f32)
        qs.append(qb)
        ks.append(k)
        for r in range(rep):
            hv = j * rep + r
            v = conv_silu(2 * GD_QK_W + hv * GD_HD, GD_HD)
            g_col = gcum[:, GD_V_HEADS + hv:GD_V_HEADS + hv + 1]
            g_row = gcum_t[GD_V_HEADS + hv:GD_V_HEADS + hv + 1, :]
            b_col = beta[:, hv:hv + 1]
            decay = jnp.exp(jnp.where(incl, g_col - g_row, -jnp.inf))
            eg = jnp.exp(g_col)
            amats.append(jnp.where(strict, b_col * kk * decay, 0.0))
            qkds.append((qk * decay).astype(bf16))
            egs.append(eg)
            g_cols.append(g_col)
            rhss.append(jnp.concatenate([b_col * v, (b_col * eg) * k], axis=-1))
    tinvs = _unit_lower_inverses(amats, L)
    sols = [_dot3(_split(t), _split(r)) for t, r in zip(tinvs, rhss)]
    for hv in range(GD_V_HEADS):
        j = hv // rep
        st = s_ref[0, hv]
        stb = st.astype(bf16)
        sol = sols[hv]
        wn = sol[:, :GD_HD] - jnp.dot(sol[:, GD_HD:].astype(bf16), stb, preferred_element_type=f32)
        wnb = wn.astype(bf16)
        o = egs[hv] * jnp.dot(qs[j], stb, preferred_element_type=f32) + jnp.dot(qkds[hv], wnb, preferred_element_type=f32)
        g_col = g_cols[hv]
        g_last = g_col[L - 1:L, :]
        kt = (jnp.exp(g_last - g_col) * ks[j]).T.astype(bf16)
        s_ref[0, hv] = jnp.exp(g_last) * st + jnp.dot(kt, wnb, preferred_element_type=f32)
        o = o * lax.rsqrt(jnp.mean(o * o, axis=-1, keepdims=True) + RMS_EPS) * nw_ref[...]
        o_ref[:, hv * GD_HD:(hv + 1) * GD_HD] = o * _silu(zg_ref[:, hv * GD_HD:(hv + 1) * GD_HD])
    prev_ref[...] = x_ref[L - 8:L, :]


def _gdn(qkv, zg, ba, conv_w, alog_row, dt_row, norm_row, s0, conv8, B, T, L):
    nc = T // L
    n = B * T
    tok = lambda b, c: (b * nc + c, 0)
    const = lambda b, c: (0, 0)
    return pl.pallas_call(
        functools.partial(_gdn_kernel, L=L),
        grid=(B, nc),
        in_specs=[pl.BlockSpec((L, GD_CONV_CH), tok),
                  pl.BlockSpec((L, GD_V_W), tok),
                  pl.BlockSpec((L, LANES), tok),
                  pl.BlockSpec((GD_CONV, GD_CONV_CH), const),
                  pl.BlockSpec((1, LANES), const),
                  pl.BlockSpec((1, LANES), const),
                  pl.BlockSpec((1, GD_HD), const),
                  pl.BlockSpec((1, GD_V_HEADS, GD_HD, GD_HD), lambda b, c: (b, 0, 0, 0)),
                  pl.BlockSpec((1, 8, GD_CONV_CH), lambda b, c: (b, 0, 0))],
        out_specs=[pl.BlockSpec((L, GD_V_W), tok),
                   pl.BlockSpec((1, GD_V_HEADS, GD_HD, GD_HD), lambda b, c: (b, 0, 0, 0))],
        out_shape=[SDS((n, GD_V_W), f32), SDS((B, GD_V_HEADS, GD_HD, GD_HD), f32)],
        scratch_shapes=[pltpu.VMEM((8, GD_CONV_CH), f32)],
        compiler_params=_cparams("parallel", "arbitrary"),
        name="gdn",
    )(qkv, zg, ba, conv_w, alog_row, dt_row, norm_row, s0, conv8)


def _pad_lanes(row, offset=0):
    return jnp.zeros((1, LANES), f32).at[0, offset:offset + row.shape[0]].set(row.astype(f32))


def _rope_tables(pos):
    half = SW_HD // 2
    inv = ROPE_THETA ** (-jnp.arange(half, dtype=f32) / half)
    ang = pos.astype(f32)[:, None] * inv[None, :]
    cos, sin = jnp.cos(ang), jnp.sin(ang)
    cos_t = jnp.concatenate([cos, cos] * SW_KV_HEADS, axis=-1)
    sin_t = jnp.concatenate([-sin, sin] * SW_KV_HEADS, axis=-1)
    return cos_t, sin_t


def _tile(n, pref):
    return pref if n % pref == 0 else n


def _trunk(x, pos, L, state, p):
    B, T, _ = x.shape
    n = B * T
    x2 = x.reshape(n, D_MODEL)
    tm = _tile(n, 512)

    z_ml, z_sw, z_g = _proj(x2, p["ab_w"], ((0, ML_W), (ML_W, SW_W), (ML_W + SW_W, LANES)), tm)
    if state is None:
        cn0 = jnp.zeros((B, ML_HEADS, ML_DK, LANES), f32)
        m0 = jnp.zeros((B, 1, LANES), f32)
    else:
        cn0 = jnp.concatenate([state["ml_C"], state["ml_n"][..., None],
                               jnp.zeros((B, ML_HEADS, ML_DK, LANES - ML_DV - 1), f32)], axis=-1)
        m0 = jnp.zeros((B, 1, LANES), f32).at[:, 0, :ML_HEADS].set(state["ml_m"])
    h_ml, cn, m_out = _mlstm(z_ml, z_g, p["ab_bias"], p["ab_norm"], cn0, m0, B, T, L)
    h_ml = h_ml.reshape(n, ML_HEADS * ML_DV)
    cos_t, sin_t = _rope_tables(pos)
    if state is None:
        a_sw, k_rot = _swa_prompt(z_sw, cos_t, sin_t, p["ab_sinks"], B, T, L)
    else:
        a_sw, k_rot = _swa_sample(z_sw, state["sw_k"].reshape(B, WINDOW, LANES),
                                  state["sw_v"].reshape(B, WINDOW, LANES), cos_t, sin_t, p["ab_sinks"], B, T)
    keep = min(T, WINDOW)
    new_k = k_rot.reshape(B, T, SW_KV_HEADS, SW_HD)[:, T - keep:]
    new_v = z_sw[:, SW_HEADS * SW_HD + LANES:].reshape(B, T, SW_KV_HEADS, SW_HD)[:, T - keep:]
    x2 = _out_ln(x2, [h_ml, a_sw], [p["ab_wo_h"], p["ab_wo_a"]], p["ln_g"][0][0], p["ln_b"][0][0], tm)
    x2 = _moe_block(x2, p, 0)

    tm1 = _tile(n, 256)
    qkv, zg, ba = _proj(x2, p["c_w"], ((0, GD_CONV_CH), (GD_CONV_CH, GD_V_W), (GD_CONV_CH + GD_V_W, LANES)), tm1)
    if state is None:
        s0 = jnp.zeros((B, GD_V_HEADS, GD_HD, GD_HD), f32)
        conv8 = jnp.zeros((B, 8, GD_CONV_CH), f32)
    else:
        s0 = state["gd_S"]
        conv8 = jnp.concatenate([jnp.zeros((B, 8 - (GD_CONV - 1), GD_CONV_CH), f32), state["gd_conv"]], axis=1)
    o_gd, s_out = _gdn(qkv, zg, ba, p["c_conv_w"], p["c_alog"], p["c_dt"], p["c_norm"], s0, conv8, B, T, L)
    new_conv = qkv.reshape(B, T, GD_CONV_CH)[:, T - (GD_CONV - 1):]
    x2 = _out_ln(x2, [o_gd], [p["c_wo"]], p["ln_g"][1][0], p["ln_b"][1][0], tm)
    x2 = _moe_block(x2, p, 1)

    outs = (new_k[None], new_v[None], cn[None, ..., :ML_DV], cn[None, ..., ML_DV], m_out[None, :, 0, :ML_HEADS],
            s_out[None], new_conv[None])
    return x2.reshape(B, T, D_MODEL), outs


def _moe_block(x2, p, layer):
    n = x2.shape[0]
    gates_t, route = _router(x2, p["router_wt"], p["router_b"], _tile(n, 512))
    wg, wu, wd = p["ex_gate"][layer], p["ex_up"][layer], p["ex_down"][layer]
    g_row, b_row = p["ln_g"][layer][1], p["ln_b"][layer][1]
    if n < N_PAIRS * MOE_TM:
        return _moe_ln(x2, gates_t.T, wg, wu, wd, g_row, b_row, _tile(n, 1024))
    token_of_row, row_of_token, w_rows, tile_a, tile_b, tile_valid = _pair_plan(route, n, MOE_TM)
    xs = _gather_rows(x2, token_of_row, MOE_TM)
    ys = _pair_experts(xs, w_rows, tile_a, tile_b, tile_valid, wg, wu, wd, g_row, b_row, MOE_TM)
    return _gather_rows(ys, row_of_token, MOE_TM)


def kernel(x_prompt, x_sample, cache_swa_k, cache_swa_v, state_mlstm_C, state_mlstm_n, state_mlstm_m, state_gdn_S, state_gdn_conv, ab_w_in, ab_b_i, ab_b_f, ab_norm, ab_sinks, ab_w_out, c_w_in, c_conv_w, c_a_log, c_dt_bias, c_norm, c_w_out, ln_g, ln_b, router_w, router_b, ex_gate, ex_up, ex_down):
    gate_lo = ML_W
    sw_lo = ML_W + 2 * ML_HEADS
    w0 = ab_w_in[0]
    ab_w = jnp.concatenate([w0[:, :gate_lo], w0[:, sw_lo:], w0[:, gate_lo:sw_lo],
                            jnp.zeros((D_MODEL, LANES - 2 * ML_HEADS), f32)], axis=1).astype(bf16)
    w1 = c_w_in[0]
    c_w = jnp.concatenate([w1, jnp.zeros((D_MODEL, LANES - 2 * GD_V_HEADS), f32)], axis=1).astype(bf16)
    wo = ab_w_out[0].astype(bf16)
    p = {
        "ab_w": ab_w,
        "ab_bias": _pad_lanes(jnp.concatenate([ab_b_i[0], ab_b_f[0]])),
        "ab_norm": ab_norm[0].reshape(1, ML_HEADS * ML_DV),
        "ab_sinks": _pad_lanes(ab_sinks[0]),
        "ab_wo_h": wo[:ML_HEADS * ML_DV],
        "ab_wo_a": wo[ML_HEADS * ML_DV:],
        "c_w": c_w,
        "c_conv_w": c_conv_w[0],
        "c_alog": _pad_lanes(c_a_log[0], GD_V_HEADS),
        "c_dt": _pad_lanes(c_dt_bias[0], GD_V_HEADS),
        "c_norm": c_norm[0].reshape(1, GD_HD),
        "c_wo": c_w_out[0].astype(bf16),
        "ln_g": [[ln_g[i, j].reshape(1, D_MODEL) for j in range(2)] for i in range(DEPTH)],
        "ln_b": [[ln_b[i, j].reshape(1, D_MODEL) for j in range(2)] for i in range(DEPTH)],
        "router_wt": router_w.T,
        "router_b": router_b.reshape(N_EXPERTS, 1),
        "ex_gate": ex_gate.astype(bf16),
        "ex_up": ex_up.astype(bf16),
        "ex_down": ex_down.astype(bf16),
    }
    t_p = x_prompt.shape[1]
    y_p, st_p = _trunk(x_prompt, jnp.arange(t_p, dtype=jnp.int32), CHUNK, None, p)
    t_s = x_sample.shape[1]
    state = {"sw_k": cache_swa_k[0], "sw_v": cache_swa_v[0], "ml_C": state_mlstm_C[0], "ml_n": state_mlstm_n[0],
             "ml_m": state_mlstm_m[0], "gd_S": state_gdn_S[0], "gd_conv": state_gdn_conv[0]}
    y_s, st_s = _trunk(x_sample, PAST_LEN + jnp.arange(t_s, dtype=jnp.int32), t_s, state, p)
    return (y_p, y_s) + st_p + st_s
```

```python
import functools
import math

import jax
import jax.numpy as jnp
from jax import lax
from jax.experimental import pallas as pl
from jax.experimental.pallas import tpu as pltpu

f32 = jnp.float32
bf16 = jnp.bfloat16
HIGHEST = lax.Precision.HIGHEST

D_MODEL = 1024
DEPTH = 2
CHUNK = 64
PAST_LEN = 2048
ML_HEADS = 8
ML_DK = 64
ML_DV = 64
SW_HEADS = 8
SW_KV_HEADS = 2
SW_HD = 64
SW_GROUP = SW_HEADS // SW_KV_HEADS
WINDOW = 128
ROPE_THETA = 10000.0
GD_QK_HEADS = 8
GD_V_HEADS = 16
GD_HD = 128
GD_CONV = 4
GD_QK_W = GD_QK_HEADS * GD_HD
GD_V_W = GD_V_HEADS * GD_HD
GD_CONV_CH = 2 * GD_QK_W + GD_V_W
N_EXPERTS = 16
N_GROUPS = 4
EXP_PER_GROUP = 4
D_EXPERT = 512
DN_ALPHA = (2 * DEPTH) ** 0.25
LN_EPS = 1e-5
RMS_EPS = 1e-6

LANES = 128
ML_W = 4 * ML_HEADS * ML_DK
SW_W = SW_HEADS * SW_HD + 2 * SW_KV_HEADS * SW_HD
VMEM_LIMIT = 56 * 1024 * 1024

SDS = jax.ShapeDtypeStruct


def _cparams(*sem):
    return pltpu.CompilerParams(dimension_semantics=sem, vmem_limit_bytes=VMEM_LIMIT)


def _dot(a, b):
    return jnp.dot(a, b, preferred_element_type=f32, precision=HIGHEST)


def _dot_nt(a, b):
    return lax.dot_general(a, b, (((1,), (1,)), ((), ())), preferred_element_type=f32, precision=HIGHEST)


def _bdot(a, b):
    return jnp.dot(a.astype(bf16), b.astype(bf16), preferred_element_type=f32)


def _bdot_nt(a, b):
    return lax.dot_general(a.astype(bf16), b.astype(bf16), (((1,), (1,)), ((), ())), preferred_element_type=f32)


def _bdot_tn(a, b):
    return jnp.dot(a.T.astype(bf16), b.astype(bf16), preferred_element_type=f32)


def _sigmoid(x):
    return 1.0 / (1.0 + jnp.exp(-x))


def _silu(x):
    return x * _sigmoid(x)


def _softplus(x):
    return jnp.maximum(x, 0.0) + jnp.log(1.0 + jnp.exp(-jnp.abs(x)))


def _layer_norm(v, g, b):
    mu = jnp.mean(v, axis=-1, keepdims=True)
    d = v - mu
    var = jnp.mean(d * d, axis=-1, keepdims=True)
    return d * lax.rsqrt(var + LN_EPS) * g + b


def _proj_kernel(x_ref, w_ref, *o_refs, splits, col_chunk):
    xb = x_ref[...].astype(bf16)
    for o_ref, (start, width) in zip(o_refs, splits):
        for c in range(0, width, col_chunk):
            cw = min(col_chunk, width - c)
            o_ref[:, c:c + cw] = jnp.dot(xb, w_ref[:, start + c:start + c + cw], preferred_element_type=f32)


def _proj(x2, w, splits, tm):
    n, k = x2.shape
    return pl.pallas_call(
        functools.partial(_proj_kernel, splits=splits, col_chunk=512),
        grid=(n // tm,),
        in_specs=[pl.BlockSpec((tm, k), lambda i: (i, 0)),
                  pl.BlockSpec(w.shape, lambda i: (0, 0), pipeline_mode=pl.Buffered(1))],
        out_specs=[pl.BlockSpec((tm, wd), lambda i: (i, 0)) for _, wd in splits],
        out_shape=[SDS((n, wd), f32) for _, wd in splits],
        compiler_params=_cparams("parallel"),
        name="in_proj",
    )(x2, w)


def _mlstm_kernel(z_ref, g_ref, bias_ref, nw_ref, cn0_ref, m0_ref, h_ref, cn_ref, m_ref, *, L, BB):
    @pl.when(pl.program_id(1) == 0)
    def _():
        cn_ref[...] = cn0_ref[...]
        m_ref[...] = m0_ref[...]

    row = lax.broadcasted_iota(jnp.int32, (L, L), 0)
    col = lax.broadcasted_iota(jnp.int32, (L, L), 1)
    causal = row >= col
    tri = causal.astype(f32)
    lane = lax.broadcasted_iota(jnp.int32, (1, LANES), 1)
    lane_l = lax.broadcasted_iota(jnp.int32, (L, ML_DV), 1)
    one_hot0 = (lane_l == 0).astype(f32)
    for bi in range(BB):
        g = g_ref[bi] + bias_ref[...]
        lf = jnp.minimum(g, 0.0) - jnp.log(1.0 + jnp.exp(-jnp.abs(g)))
        bcum = _dot(tri, lf)
        b_t = bcum.T
        g_t = g.T
        m_row = m_ref[bi]
        new_m = m_row
        outs = []
        for h in range(ML_HEADS):
            b_col = bcum[:, ML_HEADS + h:ML_HEADS + h + 1]
            b_row = b_t[ML_HEADS + h:ML_HEADS + h + 1, :]
            ig_row = g_t[h:h + 1, :]
            ig_col = g[:, h:h + 1]
            m_h = m_row[:, h:h + 1]
            dmat = jnp.where(causal, b_col - b_row + ig_row, -jnp.inf)
            inter = b_col + m_h
            mt = jnp.maximum(inter, jnp.max(dmat, axis=-1, keepdims=True))
            a = jnp.exp(inter - mt)
            q = z_ref[bi, :, h * ML_DK:(h + 1) * ML_DK]
            k = z_ref[bi, :, ML_HEADS * ML_DK + h * ML_DK:ML_HEADS * ML_DK + (h + 1) * ML_DK] * (ML_DK ** -0.5)
            v = z_ref[bi, :, 2 * ML_HEADS * ML_DK + h * ML_DV:2 * ML_HEADS * ML_DK + (h + 1) * ML_DV]
            og = z_ref[bi, :, 3 * ML_HEADS * ML_DK + h * ML_DV:3 * ML_HEADS * ML_DK + (h + 1) * ML_DV]
            s = _bdot_nt(q, k) * jnp.exp(dmat - mt)
            vext = jnp.concatenate([v, one_hot0], axis=-1)
            cn = cn_ref[bi, h]
            tot = a * _bdot(q, cn) + _bdot(s, vext)
            num = tot[:, :ML_DV]
            den = tot[:, ML_DV:ML_DV + 1]
            hh = num / jnp.maximum(jnp.abs(den), jnp.exp(-mt))
            hh = hh * lax.rsqrt(jnp.mean(hh * hh, axis=-1, keepdims=True) + RMS_EPS) * nw_ref[:, h * ML_DV:(h + 1) * ML_DV]
            outs.append(hh * _sigmoid(og))
            m_new = mt[L - 1:L, :]
            b_last = b_col[L - 1:L, :]
            wk = jnp.exp(b_last - b_col + ig_col - m_new)
            dec = jnp.exp(b_last + m_h - m_new)
            cn_ref[bi, h] = dec * cn + _bdot_tn(k, wk * vext)
            new_m = jnp.where(lane == h, m_new, new_m)
        m_ref[bi] = new_m
        h_ref[bi] = jnp.concatenate(outs, axis=-1)


def _mlstm(z_ml, z_g, bias_row, norm_row, cn0, m0, B, T, L):
    nc = T // L
    bb = min(B, 4)
    tok = lambda b, c: (b, c, 0)
    st4 = lambda b, c: (b, 0, 0, 0)
    st3 = lambda b, c: (b, 0, 0)
    return pl.pallas_call(
        functools.partial(_mlstm_kernel, L=L, BB=bb),
        grid=(B // bb, nc),
        in_specs=[pl.BlockSpec((bb, L, ML_W), tok),
                  pl.BlockSpec((bb, L, LANES), tok),
                  pl.BlockSpec((1, LANES), lambda b, c: (0, 0)),
                  pl.BlockSpec((1, ML_HEADS * ML_DV), lambda b, c: (0, 0)),
                  pl.BlockSpec((bb, ML_HEADS, ML_DK, LANES), st4),
                  pl.BlockSpec((bb, 1, LANES), st3)],
        out_specs=[pl.BlockSpec((bb, L, ML_HEADS * ML_DV), tok),
                   pl.BlockSpec((bb, ML_HEADS, ML_DK, LANES), st4),
                   pl.BlockSpec((bb, 1, LANES), st3)],
        out_shape=[SDS((B, T, ML_HEADS * ML_DV), f32),
                   SDS((B, ML_HEADS, ML_DK, LANES), f32),
                   SDS((B, 1, LANES), f32)],
        compiler_params=_cparams("parallel", "arbitrary"),
        name="mlstm",
    )(z_ml.reshape(B, T, ML_W), z_g.reshape(B, T, LANES), bias_row, norm_row, cn0, m0)


def _rope(x, cos, sin_signed):
    w = x.shape[-1]
    lane = lax.broadcasted_iota(jnp.int32, x.shape, 1)
    swapped = jnp.where((lane % SW_HD) < SW_HD // 2, pltpu.roll(x, w - SW_HD // 2, 1), pltpu.roll(x, SW_HD // 2, 1))
    return x * cos + swapped * sin_signed


def _swa_attend(jobs, sinks_ref, L):
    units = [(j, g) for j in range(len(jobs)) for g in range(SW_KV_HEADS)]
    sinks = [jnp.concatenate(
        [jnp.broadcast_to(sinks_ref[:, g * SW_GROUP + i:g * SW_GROUP + i + 1], (L, 1)) for i in range(SW_GROUP)],
        axis=0) for g in range(SW_KV_HEADS)]
    s, p = {}, {}
    for j, g in units:
        qr, keys, _, _ = jobs[j]
        q4 = jnp.concatenate([qr[:, (g * SW_GROUP + i) * SW_HD:(g * SW_GROUP + i + 1) * SW_HD]
                              for i in range(SW_GROUP)], axis=0)
        s[j, g] = _bdot_nt(q4, keys[:, g * SW_HD:(g + 1) * SW_HD]) * (SW_HD ** -0.5)
    for j, g in units:
        first_valid = jobs[j][3]
        sc = s[j, g]
        if first_valid is not None:
            kcol = lax.broadcasted_iota(jnp.int32, (1, sc.shape[1]), 1)
            sc = jnp.where(kcol >= first_valid, sc, -jnp.inf)
        mx = jnp.maximum(jnp.max(sc, axis=-1, keepdims=True), sinks[g])
        e = jnp.exp(sc - mx)
        p[j, g] = e / (jnp.sum(e, axis=-1, keepdims=True) + jnp.exp(sinks[g] - mx))
    o = {u: _bdot(p[u], jobs[u[0]][2][:, u[1] * SW_HD:(u[1] + 1) * SW_HD]) for u in units}
    return [jnp.concatenate([o[j, g][i * L:(i + 1) * L, :] for g in range(SW_KV_HEADS) for i in range(SW_GROUP)],
                            axis=-1) for j in range(len(jobs))]


def _swa_prompt_kernel(q_ref, kp_ref, kc_ref, vp_ref, vc_ref, cp_ref, cc_ref, sp_ref, sc_ref, sinks_ref,
                       o_ref, kr_ref, *, L, CB):
    i = pl.program_id(1)
    rows = CB * L
    back = 2 * L
    cos_q = jnp.concatenate([cc_ref[...]] * (SW_HEADS // SW_KV_HEADS), axis=-1)
    sin_q = jnp.concatenate([sc_ref[...]] * (SW_HEADS // SW_KV_HEADS), axis=-1)
    qr = _rope(q_ref[...], cos_q, sin_q)
    k_cur = _rope(kc_ref[...], cc_ref[...], sc_ref[...])
    kr_ref[...] = k_cur
    k_prev = _rope(kp_ref[rows - back:rows, :], cp_ref[rows - back:rows, :], sp_ref[rows - back:rows, :])
    keys = jnp.concatenate([k_prev, k_cur], axis=0)
    vals = jnp.concatenate([vp_ref[rows - back:rows, :], vc_ref[...]], axis=0)
    jobs = []
    for u in range(CB):
        first_valid = jnp.where(i == 0, back - u * L, 0) if u * L < back else None
        jobs.append((qr[u * L:(u + 1) * L], keys[u * L:(u + 3) * L], vals[u * L:(u + 3) * L], first_valid))
    o_ref[...] = jnp.concatenate(_swa_attend(jobs, sinks_ref, L), axis=0)


def _swa_prompt(z_sw, cos_t, sin_t, sinks_row, B, T, L):
    cb = 4
    rows = cb * L
    nb = T // rows
    n = B * T
    kcol = SW_HEADS * SW_HD // LANES
    vcol = kcol + 1
    cur = lambda b, i: (b * nb + i, 0)
    prev = lambda col: (lambda b, i: (b * nb + jnp.maximum(i - 1, 0), col))
    curc = lambda col: (lambda b, i: (b * nb + i, col))
    tab_cur = lambda b, i: (i, 0)
    tab_prev = lambda b, i: (jnp.maximum(i - 1, 0), 0)
    return pl.pallas_call(
        functools.partial(_swa_prompt_kernel, L=L, CB=cb),
        grid=(B, nb),
        in_specs=[pl.BlockSpec((rows, SW_HEADS * SW_HD), cur),
                  pl.BlockSpec((rows, LANES), prev(kcol)), pl.BlockSpec((rows, LANES), curc(kcol)),
                  pl.BlockSpec((rows, LANES), prev(vcol)), pl.BlockSpec((rows, LANES), curc(vcol)),
                  pl.BlockSpec((rows, LANES), tab_prev), pl.BlockSpec((rows, LANES), tab_cur),
                  pl.BlockSpec((rows, LANES), tab_prev), pl.BlockSpec((rows, LANES), tab_cur),
                  pl.BlockSpec((1, LANES), lambda b, i: (0, 0))],
        out_specs=[pl.BlockSpec((rows, SW_HEADS * SW_HD), cur),
                   pl.BlockSpec((rows, LANES), cur)],
        out_shape=[SDS((n, SW_HEADS * SW_HD), f32), SDS((n, LANES), f32)],
        compiler_params=_cparams("parallel", "parallel"),
        name="swa_prompt",
    )(z_sw, z_sw, z_sw, z_sw, z_sw, cos_t, cos_t, sin_t, sin_t, sinks_row)


def _swa_sample_kernel(q_ref, k_ref, v_ref, ck_ref, cv_ref, cos_ref, sin_ref, sinks_ref, o_ref, kr_ref, *, L):
    cos_q = jnp.concatenate([cos_ref[...]] * (SW_HEADS // SW_KV_HEADS), axis=-1)
    sin_q = jnp.concatenate([sin_ref[...]] * (SW_HEADS // SW_KV_HEADS), axis=-1)
    qr = _rope(q_ref[...], cos_q, sin_q)
    kr = _rope(k_ref[...], cos_ref[...], sin_ref[...])
    kr_ref[...] = kr
    keys = jnp.concatenate([ck_ref[0], kr], axis=0)
    vals = jnp.concatenate([cv_ref[0], v_ref[...]], axis=0)
    o_ref[...] = _swa_attend([(qr, keys, vals, None)], sinks_ref, L)[0]


def _swa_sample(z_sw, cache_k, cache_v, cos_t, sin_t, sinks_row, B, T):
    n = B * T
    kcol = SW_HEADS * SW_HD // LANES
    return pl.pallas_call(
        functools.partial(_swa_sample_kernel, L=T),
        grid=(B,),
        in_specs=[pl.BlockSpec((T, SW_HEADS * SW_HD), lambda b: (b, 0)),
                  pl.BlockSpec((T, LANES), lambda b: (b, kcol)),
                  pl.BlockSpec((T, LANES), lambda b: (b, kcol + 1)),
                  pl.BlockSpec((1, WINDOW, LANES), lambda b: (b, 0, 0)),
                  pl.BlockSpec((1, WINDOW, LANES), lambda b: (b, 0, 0)),
                  pl.BlockSpec((T, LANES), lambda b: (0, 0)),
                  pl.BlockSpec((T, LANES), lambda b: (0, 0)),
                  pl.BlockSpec((1, LANES), lambda b: (0, 0))],
        out_specs=[pl.BlockSpec((T, SW_HEADS * SW_HD), lambda b: (b, 0)),
                   pl.BlockSpec((T, LANES), lambda b: (b, 0))],
        out_shape=[SDS((n, SW_HEADS * SW_HD), f32), SDS((n, LANES), f32)],
        compiler_params=_cparams("parallel"),
        name="swa_sample",
    )(z_sw, z_sw, z_sw, cache_k, cache_v, cos_t, sin_t, sinks_row)


def _out_ln_kernel(*refs, n_in):
    x_ref = refs[0]
    a_refs = refs[1:1 + n_in]
    w_refs = refs[1 + n_in:1 + 2 * n_in]
    g_ref, b_ref, o_ref = refs[1 + 2 * n_in:]
    y = _bdot(a_refs[0][...], w_refs[0][...])
    for a_ref, w_ref in zip(a_refs[1:], w_refs[1:]):
        y = y + _bdot(a_ref[...], w_ref[...])
    o_ref[...] = _layer_norm(DN_ALPHA * x_ref[...] + y, g_ref[...], b_ref[...])


def _out_ln(x2, acts, ws, g_row, b_row, tm):
    n = x2.shape[0]
    row = lambda i: (i, 0)
    const = lambda i: (0, 0)
    return pl.pallas_call(
        functools.partial(_out_ln_kernel, n_in=len(acts)),
        grid=(n // tm,),
        in_specs=[pl.BlockSpec((tm, D_MODEL), row)]
        + [pl.BlockSpec((tm, a.shape[1]), row) for a in acts]
        + [pl.BlockSpec(w.shape, const) for w in ws]
        + [pl.BlockSpec((1, D_MODEL), const), pl.BlockSpec((1, D_MODEL), const)],
        out_specs=pl.BlockSpec((tm, D_MODEL), row),
        out_shape=SDS((n, D_MODEL), f32),
        compiler_params=_cparams("parallel"),
        name="out_proj_ln",
    )(x2, *acts, *ws, g_row, b_row)


def _router_kernel(x_ref, rw_ref, rb_ref, o_ref, route_ref):
    logits = _dot_nt(rw_ref[...], x_ref[...])
    aff = _sigmoid(logits)
    sc = aff + rb_ref[...]
    s = [sc[e:e + 1, :] for e in range(N_EXPERTS)]
    a = [aff[e:e + 1, :] for e in range(N_EXPERTS)]
    scores = []
    for gi in range(N_GROUPS):
        w, x, y, z = s[4 * gi:4 * gi + 4]
        p, q = jnp.maximum(w, x), jnp.minimum(w, x)
        r, t = jnp.maximum(y, z), jnp.minimum(y, z)
        scores.append(jnp.maximum(p, r) + jnp.maximum(jnp.minimum(p, r), jnp.maximum(q, t)))
    best = scores[0]
    gsel = jnp.zeros_like(best, dtype=jnp.int32)
    for gi in range(1, N_GROUPS):
        better = scores[gi] > best
        best = jnp.where(better, scores[gi], best)
        gsel = jnp.where(better, gi, gsel)
    sel = []
    for e in range(N_EXPERTS):
        gi, i = divmod(e, EXP_PER_GROUP)
        beaten = jnp.zeros_like(gsel)
        for j in range(EXP_PER_GROUP):
            if j == i:
                continue
            o = s[4 * gi + j]
            wins = (o >= s[e]) if j < i else (o > s[e])
            beaten = beaten + wins.astype(jnp.int32)
        sel.append((gsel == gi) & (beaten < 2))
    den = jnp.zeros_like(best)
    for e in range(N_EXPERTS):
        den = den + jnp.where(sel[e], a[e], 0.0)
    gate = [jnp.where(sel[e], a[e] / den, 0.0) for e in range(N_EXPERTS)]
    o_ref[...] = jnp.concatenate(gate, axis=0)
    taken = jnp.zeros_like(gsel)
    ea = eb = wa = wb = jnp.zeros_like(best)
    for e in range(N_EXPERTS):
        first = sel[e] & (taken == 0)
        second = sel[e] & (taken == 1)
        ea = jnp.where(first, float(e), ea)
        wa = jnp.where(first, gate[e], wa)
        eb = jnp.where(second, float(e), eb)
        wb = jnp.where(second, gate[e], wb)
        taken = taken + sel[e].astype(jnp.int32)
    route_ref[...] = jnp.concatenate([ea, eb, wa, wb, jnp.zeros((4, ea.shape[1]), f32)], axis=0)


def _router(x2, rw_t, rb_col, tm):
    n = x2.shape[0]
    return pl.pallas_call(
        _router_kernel,
        grid=(n // tm,),
        in_specs=[pl.BlockSpec((tm, D_MODEL), lambda i: (i, 0)),
                  pl.BlockSpec((N_EXPERTS, D_MODEL), lambda i: (0, 0)),
                  pl.BlockSpec((N_EXPERTS, 1), lambda i: (0, 0))],
        out_specs=[pl.BlockSpec((N_EXPERTS, tm), lambda i: (0, i)),
                   pl.BlockSpec((8, tm), lambda i: (0, i))],
        out_shape=[SDS((N_EXPERTS, n), f32), SDS((8, n), f32)],
        compiler_params=_cparams("parallel"),
        name="router",
    )(x2, rw_t, rb_col)


def _moe_kernel(x_ref, gates_ref, wg_ref, wu_ref, wd_ref, g_ref, b_ref, o_ref, xb_ref, acc_ref):
    e = pl.program_id(1)

    @pl.when(e == 0)
    def _():
        xb_ref[...] = x_ref[...].astype(bf16)
        acc_ref[...] = jnp.zeros_like(acc_ref)

    xb = xb_ref[...]
    lane = lax.broadcasted_iota(jnp.int32, gates_ref.shape, 1)
    gcol = jnp.sum(jnp.where(lane == e, gates_ref[...], 0.0), axis=-1, keepdims=True)
    h = _silu(jnp.dot(xb, wg_ref[0], preferred_element_type=f32)) * jnp.dot(xb, wu_ref[0], preferred_element_type=f32)
    acc_ref[...] += jnp.dot((gcol * h).astype(bf16), wd_ref[0], preferred_element_type=f32)

    @pl.when(e == N_EXPERTS - 1)
    def _():
        o_ref[...] = _layer_norm(DN_ALPHA * x_ref[...] + acc_ref[...], g_ref[...], b_ref[...])


def _moe_ln(x2, gates, wg, wu, wd, g_row, b_row, tm):
    n = x2.shape[0]
    row = lambda i, e: (i, 0)
    const = lambda i, e: (0, 0)
    return pl.pallas_call(
        _moe_kernel,
        grid=(n // tm, N_EXPERTS),
        in_specs=[pl.BlockSpec((tm, D_MODEL), row),
                  pl.BlockSpec((tm, N_EXPERTS), row),
                  pl.BlockSpec((1, D_MODEL, D_EXPERT), lambda i, e: (e, 0, 0)),
                  pl.BlockSpec((1, D_MODEL, D_EXPERT), lambda i, e: (e, 0, 0)),
                  pl.BlockSpec((1, D_EXPERT, D_MODEL), lambda i, e: (e, 0, 0)),
                  pl.BlockSpec((1, D_MODEL), const), pl.BlockSpec((1, D_MODEL), const)],
        out_specs=pl.BlockSpec((tm, D_MODEL), row),
        out_shape=SDS((n, D_MODEL), f32),
        scratch_shapes=[pltpu.VMEM((tm, D_MODEL), bf16), pltpu.VMEM((tm, D_MODEL), f32)],
        compiler_params=_cparams("parallel", "arbitrary"),
        name="moe_ln",
    )(x2, gates, wg, wu, wd, g_row, b_row)


N_PAIRS = N_GROUPS * (EXP_PER_GROUP * (EXP_PER_GROUP - 1) // 2)
MOE_TM = 256
_PAIR_A = [g * EXP_PER_GROUP + a for g in range(N_GROUPS) for a in range(EXP_PER_GROUP) for b in range(a + 1, EXP_PER_GROUP)]
_PAIR_B = [g * EXP_PER_GROUP + b for g in range(N_GROUPS) for a in range(EXP_PER_GROUP) for b in range(a + 1, EXP_PER_GROUP)]


def _gather_rows_kernel(idx_ref, src_ref, o_ref, sem, *, rows):
    base = pl.program_id(0) * rows

    def row_copy(j):
        return pltpu.make_async_copy(src_ref.at[pl.ds(idx_ref[base + j], 1)], o_ref.at[pl.ds(j, 1)], sem)

    def issue(j, carry):
        row_copy(j).start()
        return carry

    def drain(j, carry):
        row_copy(j).wait()
        return carry

    lax.fori_loop(0, rows, issue, 0, unroll=8)
    lax.fori_loop(0, rows, drain, 0, unroll=8)


def _gather_rows(src, idx, rows):
    n_out = idx.shape[0]
    d = src.shape[1]
    return pl.pallas_call(
        functools.partial(_gather_rows_kernel, rows=rows),
        grid_spec=pltpu.PrefetchScalarGridSpec(
            num_scalar_prefetch=1,
            grid=(n_out // rows,),
            in_specs=[pl.BlockSpec(memory_space=pl.ANY)],
            out_specs=pl.BlockSpec((rows, d), lambda i, idx_ref: (i, 0)),
            scratch_shapes=[pltpu.SemaphoreType.DMA(())]),
        out_shape=SDS((n_out, d), f32),
        compiler_params=_cparams("arbitrary"),
        name="gather_rows",
    )(idx, src)


def _scatter_rows_kernel(idx_ref, src_ref, init_ref, o_ref, sem, *, rows):
    del init_ref
    base = pl.program_id(0) * rows

    def row_copy(j):
        return pltpu.make_async_copy(src_ref.at[pl.ds(base + j, 1)], o_ref.at[pl.ds(idx_ref[base + j], 1)], sem)

    def issue(j, carry):
        row_copy(j).start()
        return carry

    def drain(j, carry):
        row_copy(j).wait()
        return carry

    lax.fori_loop(0, rows, issue, 0, unroll=8)
    lax.fori_loop(0, rows, drain, 0, unroll=8)


def _scatter_rows(src, idx, n_out, rows):
    n_src, d = src.shape
    return pl.pallas_call(
        functools.partial(_scatter_rows_kernel, rows=rows),
        grid_spec=pltpu.PrefetchScalarGridSpec(
            num_scalar_prefetch=1,
            grid=(n_src // rows,),
            in_specs=[pl.BlockSpec(memory_space=pl.ANY), pl.BlockSpec(memory_space=pl.ANY)],
            out_specs=pl.BlockSpec(memory_space=pl.ANY),
            scratch_shapes=[pltpu.SemaphoreType.DMA(())]),
        out_shape=SDS((n_out, d), f32),
        input_output_aliases={2: 0},
        compiler_params=_cparams("arbitrary"),
        name="scatter_rows",
    )(idx, src, jnp.zeros((n_out, d), f32))


def _pair_plan(route, n, tm):
    ea = route[0].astype(jnp.int32)
    eb = route[1].astype(jnp.int32)
    a = ea % EXP_PER_GROUP
    b = eb % EXP_PER_GROUP
    pidx = jnp.where(a == 0, b - 1, jnp.where(a == 1, b + 1, 5))
    pair = (ea // EXP_PER_GROUP) * (N_PAIRS // N_GROUPS) + pidx
    onehot = (pair[:, None] == jnp.arange(N_PAIRS, dtype=jnp.int32)[None, :]).astype(jnp.int32)
    csum = jnp.cumsum(onehot, axis=0)
    counts = csum[-1]
    ntiles = (counts + tm - 1) // tm
    tile_end = jnp.cumsum(ntiles)
    tile_start = tile_end - ntiles
    row_of_token = jnp.sum(onehot * (csum - 1 + (tile_start * tm)[None, :]), axis=1)
    nt = n // tm + N_PAIRS
    tile_id = jnp.arange(nt, dtype=jnp.int32)
    tile_valid = tile_id < tile_end[-1]
    tile_pair = jnp.sum((tile_end[None, :] <= jnp.minimum(tile_id, tile_end[-1] - 1)[:, None]).astype(jnp.int32), axis=1)
    tile_pair = jnp.minimum(tile_pair, N_PAIRS - 1)
    pick = (tile_pair[:, None] == jnp.arange(N_PAIRS, dtype=jnp.int32)[None, :]).astype(jnp.int32)
    tile_a = jnp.sum(pick * jnp.asarray(_PAIR_A, jnp.int32)[None, :], axis=1)
    tile_b = jnp.sum(pick * jnp.asarray(_PAIR_B, jnp.int32)[None, :], axis=1)
    return row_of_token, tile_a, tile_b, tile_valid.astype(jnp.int32)


def _pair_expert_kernel(ta_ref, tb_ref, tv_ref, x_ref, rwt_ref, wga_ref, wua_ref, wda_ref, wgb_ref, wub_ref, wdb_ref,
                        g_ref, b_ref, o_ref):
    i = pl.program_id(0)
    valid = tv_ref[i] == 1

    @pl.when(valid)
    def _():
        x = x_ref[...]
        xb = x.astype(bf16)
        aff_a = _sigmoid(jnp.sum(x * rwt_ref[pl.ds(ta_ref[i], 1), :], axis=-1, keepdims=True))
        aff_b = _sigmoid(jnp.sum(x * rwt_ref[pl.ds(tb_ref[i], 1), :], axis=-1, keepdims=True))
        den = aff_a + aff_b
        acc = None
        for w, (wg, wu, wd) in ((aff_a / den, (wga_ref, wua_ref, wda_ref)), (aff_b / den, (wgb_ref, wub_ref, wdb_ref))):
            h = _silu(jnp.dot(xb, wg[0], preferred_element_type=f32)) * jnp.dot(xb, wu[0], preferred_element_type=f32)
            y = jnp.dot((w * h).astype(bf16), wd[0], preferred_element_type=f32)
            acc = y if acc is None else acc + y
        o_ref[...] = _layer_norm(DN_ALPHA * x + acc, g_ref[...], b_ref[...])

    @pl.when(jnp.logical_not(valid))
    def _():
        o_ref[...] = jnp.zeros_like(o_ref)


def _pair_experts(xs, rw_t, tile_a, tile_b, tile_valid, wg, wu, wd, g_row, b_row, tm):
    rows = xs.shape[0]
    row = lambda i, ta, tb, tv: (i, 0)
    const = lambda i, ta, tb, tv: (0, 0)
    ex_a = lambda i, ta, tb, tv: (ta[i], 0, 0)
    ex_b = lambda i, ta, tb, tv: (tb[i], 0, 0)
    up = pl.BlockSpec((1, D_MODEL, D_EXPERT), ex_a), pl.BlockSpec((1, D_MODEL, D_EXPERT), ex_b)
    down = pl.BlockSpec((1, D_EXPERT, D_MODEL), ex_a), pl.BlockSpec((1, D_EXPERT, D_MODEL), ex_b)
    return pl.pallas_call(
        _pair_expert_kernel,
        grid_spec=pltpu.PrefetchScalarGridSpec(
            num_scalar_prefetch=3,
            grid=(rows // tm,),
            in_specs=[pl.BlockSpec((tm, D_MODEL), row), pl.BlockSpec((N_EXPERTS, D_MODEL), const),
                      up[0], up[0], down[0], up[1], up[1], down[1],
                      pl.BlockSpec((1, D_MODEL), const), pl.BlockSpec((1, D_MODEL), const)],
            out_specs=pl.BlockSpec((tm, D_MODEL), row)),
        out_shape=SDS((rows, D_MODEL), f32),
        compiler_params=_cparams("arbitrary"),
        name="pair_experts",
    )(tile_a, tile_b, tile_valid, xs, rw_t, wg, wu, wd, wg, wu, wd, g_row, b_row)


def _split(a):
    hi = a.astype(bf16)
    return hi, (a - hi.astype(f32)).astype(bf16)


def _dot3(a, b):
    (ah, al), (bh, bl) = a, b
    mm = lambda x, y: jnp.dot(x, y, preferred_element_type=f32)
    return mm(ah, bh) + (mm(ah, bl) + mm(al, bh))


def _unit_lower_inverses(mats, L):
    row = lax.broadcasted_iota(jnp.int32, (L, L), 0)
    col = lax.broadcasted_iota(jnp.int32, (L, L), 1)
    eye = (row == col).astype(f32)
    ps = [eye - a for a in mats]
    pws = [_split(a) for a in mats]
    span = 2
    while span < L:
        pws = [_split(_dot3(pw, pw)) for pw in pws]
        ps = [p + _dot3(pw, _split(p)) for p, pw in zip(ps, pws)]
        span *= 2
    return ps


def _gdn_kernel(x_ref, zg_ref, ba_ref, cw_ref, alog_ref, dt_ref, nw_ref, s0_ref, cb_ref,
                o_ref, s_ref, prev_ref, *, L):
    @pl.when(pl.program_id(1) == 0)
    def _():
        s_ref[...] = s0_ref[...]
        prev_ref[...] = cb_ref[0]

    def conv_silu(lo, width):
        cur = x_ref[:, lo:lo + width]
        cat = jnp.concatenate([prev_ref[:, lo:lo + width], cur], axis=0)
        acc = cat[5:5 + L] * cw_ref[0:1, lo:lo + width]
        acc = acc + cat[6:6 + L] * cw_ref[1:2, lo:lo + width]
        acc = acc + cat[7:7 + L] * cw_ref[2:3, lo:lo + width]
        acc = acc + cur * cw_ref[3:4, lo:lo + width]
        return _silu(acc)

    def l2n(v):
        return v * lax.rsqrt(jnp.sum(v * v, axis=-1, keepdims=True) + 1e-6)

    ba = ba_ref[...]
    beta = _sigmoid(ba)
    gl = -jnp.exp(alog_ref[...]) * _softplus(ba + dt_ref[...])
    row = lax.broadcasted_iota(jnp.int32, (L, L), 0)
    col = lax.broadcasted_iota(jnp.int32, (L, L), 1)
    incl = row >= col
    strict = row > col
    gcum = _dot(incl.astype(f32), gl)
    gcum_t = gcum.T
    rep = GD_V_HEADS // GD_QK_HEADS
    qs, ks, amats, qkds, egs, g_cols, rhss = [], [], [], [], [], [], []
    for j in range(GD_QK_HEADS):
        q = l2n(conv_silu(j * GD_HD, GD_HD)) * (GD_HD ** -0.5)
        k = l2n(conv_silu(GD_QK_W + j * GD_HD, GD_HD))
        qb, kb = q.astype(bf16), k.astype(bf16)
        kk = lax.dot_general(kb, kb, (((1,), (1,)), ((), ())), preferred_element_type=f32)
        qk = lax.dot_general(qb, kb, (((1,), (1,)), ((), ())), preferred_element_type=f32)
        qs.append(qb)
        ks.append(k)
        for r in range(rep):
            hv = j * rep + r
            v = conv_silu(2 * GD_QK_W + hv * GD_HD, GD_HD)
            g_col = gcum[:, GD_V_HEADS + hv:GD_V_HEADS + hv + 1]
            g_row = gcum_t[GD_V_HEADS + hv:GD_V_HEADS + hv + 1, :]
            b_col = beta[:, hv:hv + 1]
            decay = jnp.exp(jnp.where(incl, g_col - g_row, -jnp.inf))
            eg = jnp.exp(g_col)
            amats.append(jnp.where(strict, b_col * kk * decay, 0.0))
            qkds.append((qk * decay).astype(bf16))
            egs.append(eg)
            g_cols.append(g_col)
            rhss.append(jnp.concatenate([b_col * v, (b_col * eg) * k], axis=-1))
    tinvs = _unit_lower_inverses(amats, L)
    sols = [_dot3(_split(t), _split(r)) for t, r in zip(tinvs, rhss)]
    for hv in range(GD_V_HEADS):
        j = hv // rep
        st = s_ref[0, hv]
        stb = st.astype(bf16)
        sol = sols[hv]
        wn = sol[:, :GD_HD] - jnp.dot(sol[:, GD_HD:].astype(bf16), stb, preferred_element_type=f32)
        wnb = wn.astype(bf16)
        o = egs[hv] * jnp.dot(qs[j], stb, preferred_element_type=f32) + jnp.dot(qkds[hv], wnb, preferred_element_type=f32)
        g_col = g_cols[hv]
        g_last = g_col[L - 1:L, :]
        kt = (jnp.exp(g_last - g_col) * ks[j]).T.astype(bf16)
        s_ref[0, hv] = jnp.exp(g_last) * st + jnp.dot(kt, wnb, preferred_element_type=f32)
        o = o * lax.rsqrt(jnp.mean(o * o, axis=-1, keepdims=True) + RMS_EPS) * nw_ref[...]
        o_ref[:, hv * GD_HD:(hv + 1) * GD_HD] = o * _silu(zg_ref[:, hv * GD_HD:(hv + 1) * GD_HD])
    prev_ref[...] = x_ref[L - 8:L, :]


def _gdn(qkv, zg, ba, conv_w, alog_row, dt_row, norm_row, s0, conv8, B, T, L):
    nc = T // L
    n = B * T
    tok = lambda b, c: (b * nc + c, 0)
    const = lambda b, c: (0, 0)
    return pl.pallas_call(
        functools.partial(_gdn_kernel, L=L),
        grid=(B, nc),
        in_specs=[pl.BlockSpec((L, GD_CONV_CH), tok),
                  pl.BlockSpec((L, GD_V_W), tok),
                  pl.BlockSpec((L, LANES), tok),
                  pl.BlockSpec((GD_CONV, GD_CONV_CH), const),
                  pl.BlockSpec((1, LANES), const),
                  pl.BlockSpec((1, LANES), const),
                  pl.BlockSpec((1, GD_HD), const),
                  pl.BlockSpec((1, GD_V_HEADS, GD_HD, GD_HD), lambda b, c: (b, 0, 0, 0)),
                  pl.BlockSpec((1, 8, GD_CONV_CH), lambda b, c: (b, 0, 0))],
        out_specs=[pl.BlockSpec((L, GD_V_W), tok),
                   pl.BlockSpec((1, GD_V_HEADS, GD_HD, GD_HD), lambda b, c: (b, 0, 0, 0))],
        out_shape=[SDS((n, GD_V_W), f32), SDS((B, GD_V_HEADS, GD_HD, GD_HD), f32)],
        scratch_shapes=[pltpu.VMEM((8, GD_CONV_CH), f32)],
        compiler_params=_cparams("parallel", "arbitrary"),
        name="gdn",
    )(qkv, zg, ba, conv_w, alog_row, dt_row, norm_row, s0, conv8)


def _pad_lanes(row, offset=0):
    return jnp.zeros((1, LANES), f32).at[0, offset:offset + row.shape[0]].set(row.astype(f32))


def _rope_tables(pos):
    half = SW_HD // 2
    inv = ROPE_THETA ** (-jnp.arange(half, dtype=f32) / half)
    ang = pos.astype(f32)[:, None] * inv[None, :]
    cos, sin = jnp.cos(ang), jnp.sin(ang)
    cos_t = jnp.concatenate([cos, cos] * SW_KV_HEADS, axis=-1)
    sin_t = jnp.concatenate([-sin, sin] * SW_KV_HEADS, axis=-1)
    return cos_t, sin_t


def _tile(n, pref):
    return pref if n % pref == 0 else n


def _trunk(x, pos, L, state, p):
    B, T, _ = x.shape
    n = B * T
    x2 = x.reshape(n, D_MODEL)
    tm = _tile(n, 512)

    z_ml, z_sw, z_g = _proj(x2, p["ab_w"], ((0, ML_W), (ML_W, SW_W), (ML_W + SW_W, LANES)), tm)
    if state is None:
        cn0 = jnp.zeros((B, ML_HEADS, ML_DK, LANES), f32)
        m0 = jnp.zeros((B, 1, LANES), f32)
    else:
        cn0 = jnp.concatenate([state["ml_C"], state["ml_n"][..., None],
                               jnp.zeros((B, ML_HEADS, ML_DK, LANES - ML_DV - 1), f32)], axis=-1)
        m0 = jnp.zeros((B, 1, LANES), f32).at[:, 0, :ML_HEADS].set(state["ml_m"])
    h_ml, cn, m_out = _mlstm(z_ml, z_g, p["ab_bias"], p["ab_norm"], cn0, m0, B, T, L)
    h_ml = h_ml.reshape(n, ML_HEADS * ML_DV)
    cos_t, sin_t = _rope_tables(pos)
    if state is None:
        a_sw, k_rot = _swa_prompt(z_sw, cos_t, sin_t, p["ab_sinks"], B, T, L)
    else:
        a_sw, k_rot = _swa_sample(z_sw, state["sw_k"].reshape(B, WINDOW, LANES),
                                  state["sw_v"].reshape(B, WINDOW, LANES), cos_t, sin_t, p["ab_sinks"], B, T)
    keep = min(T, WINDOW)
    new_k = k_rot.reshape(B, T, SW_KV_HEADS, SW_HD)[:, T - keep:]
    new_v = z_sw[:, SW_HEADS * SW_HD + LANES:].reshape(B, T, SW_KV_HEADS, SW_HD)[:, T - keep:]
    x2 = _out_ln(x2, [h_ml, a_sw], [p["ab_wo_h"], p["ab_wo_a"]], p["ln_g"][0][0], p["ln_b"][0][0], tm)
    x2 = _moe_block(x2, p, 0)

    tm1 = _tile(n, 256)
    qkv, zg, ba = _proj(x2, p["c_w"], ((0, GD_CONV_CH), (GD_CONV_CH, GD_V_W), (GD_CONV_CH + GD_V_W, LANES)), tm1)
    if state is None:
        s0 = jnp.zeros((B, GD_V_HEADS, GD_HD, GD_HD), f32)
        conv8 = jnp.zeros((B, 8, GD_CONV_CH), f32)
    else:
        s0 = state["gd_S"]
        conv8 = jnp.concatenate([jnp.zeros((B, 8 - (GD_CONV - 1), GD_CONV_CH), f32), state["gd_conv"]], axis=1)
    o_gd, s_out = _gdn(qkv, zg, ba, p["c_conv_w"], p["c_alog"], p["c_dt"], p["c_norm"], s0, conv8, B, T, L)
    new_conv = qkv.reshape(B, T, GD_CONV_CH)[:, T - (GD_CONV - 1):]
    x2 = _out_ln(x2, [o_gd], [p["c_wo"]], p["ln_g"][1][0], p["ln_b"][1][0], tm)
    x2 = _moe_block(x2, p, 1)

    outs = (new_k[None], new_v[None], cn[None, ..., :ML_DV], cn[None, ..., ML_DV], m_out[None, :, 0, :ML_HEADS],
            s_out[None], new_conv[None])
    return x2.reshape(B, T, D_MODEL), outs


def _moe_block(x2, p, layer):
    n = x2.shape[0]
    gates_t, route = _router(x2, p["router_wt"], p["router_b"], _tile(n, 512))
    wg, wu, wd = p["ex_gate"][layer], p["ex_up"][layer], p["ex_down"][layer]
    g_row, b_row = p["ln_g"][layer][1], p["ln_b"][layer][1]
    if n < N_PAIRS * MOE_TM:
        return _moe_ln(x2, gates_t.T, wg, wu, wd, g_row, b_row, _tile(n, 1024))
    row_of_token, tile_a, tile_b, tile_valid = _pair_plan(route, n, MOE_TM)
    xs = _scatter_rows(x2, row_of_token, n + N_PAIRS * MOE_TM, MOE_TM)
    ys = _pair_experts(xs, p["router_wt"], tile_a, tile_b, tile_valid, wg, wu, wd, g_row, b_row, MOE_TM)
    return _gather_rows(ys, row_of_token, MOE_TM)


def kernel(x_prompt, x_sample, cache_swa_k, cache_swa_v, state_mlstm_C, state_mlstm_n, state_mlstm_m, state_gdn_S, state_gdn_conv, ab_w_in, ab_b_i, ab_b_f, ab_norm, ab_sinks, ab_w_out, c_w_in, c_conv_w, c_a_log, c_dt_bias, c_norm, c_w_out, ln_g, ln_b, router_w, router_b, ex_gate, ex_up, ex_down):
    gate_lo = ML_W
    sw_lo = ML_W + 2 * ML_HEADS
    w0 = ab_w_in[0]
    ab_w = jnp.concatenate([w0[:, :gate_lo], w0[:, sw_lo:], w0[:, gate_lo:sw_lo],
                            jnp.zeros((D_MODEL, LANES - 2 * ML_HEADS), f32)], axis=1).astype(bf16)
    w1 = c_w_in[0]
    c_w = jnp.concatenate([w1, jnp.zeros((D_MODEL, LANES - 2 * GD_V_HEADS), f32)], axis=1).astype(bf16)
    wo = ab_w_out[0].astype(bf16)
    p = {
        "ab_w": ab_w,
        "ab_bias": _pad_lanes(jnp.concatenate([ab_b_i[0], ab_b_f[0]])),
        "ab_norm": ab_norm[0].reshape(1, ML_HEADS * ML_DV),
        "ab_sinks": _pad_lanes(ab_sinks[0]),
        "ab_wo_h": wo[:ML_HEADS * ML_DV],
        "ab_wo_a": wo[ML_HEADS * ML_DV:],
        "c_w": c_w,
        "c_conv_w": c_conv_w[0],
        "c_alog": _pad_lanes(c_a_log[0], GD_V_HEADS),
        "c_dt": _pad_lanes(c_dt_bias[0], GD_V_HEADS),
        "c_norm": c_norm[0].reshape(1, GD_HD),
        "c_wo": c_w_out[0].astype(bf16),
        "ln_g": [[ln_g[i, j].reshape(1, D_MODEL) for j in range(2)] for i in range(DEPTH)],
        "ln_b": [[ln_b[i, j].reshape(1, D_MODEL) for j in range(2)] for i in range(DEPTH)],
        "router_wt": router_w.T,
        "router_b": router_b.reshape(N_EXPERTS, 1),
        "ex_gate": ex_gate.astype(bf16),
        "ex_up": ex_up.astype(bf16),
        "ex_down": ex_down.astype(bf16),
    }
    t_p = x_prompt.shape[1]
    y_p, st_p = _trunk(x_prompt, jnp.arange(t_p, dtype=jnp.int32), CHUNK, None, p)
    t_s = x_sample.shape[1]
    state = {"sw_k": cache_swa_k[0], "sw_v": cache_swa_v[0], "ml_C": state_mlstm_C[0], "ml_n": state_mlstm_n[0],
             "ml_m": state_mlstm_m[0], "gd_S": state_gdn_S[0], "gd_conv": state_gdn_conv[0]}
    y_s, st_s = _trunk(x_sample, PAST_LEN + jnp.arange(t_s, dtype=jnp.int32), t_s, state, p)
    return (y_p, y_s) + st_p + st_s
```

```python
import functools
import math

import jax
import jax.numpy as jnp
from jax import lax
from jax.experimental import pallas as pl
from jax.experimental.pallas import tpu as pltpu

f32 = jnp.float32
bf16 = jnp.bfloat16
HIGHEST = lax.Precision.HIGHEST

D_MODEL = 1024
DEPTH = 2
CHUNK = 64
PAST_LEN = 2048
ML_HEADS = 8
ML_DK = 64
ML_DV = 64
SW_HEADS = 8
SW_KV_HEADS = 2
SW_HD = 64
SW_GROUP = SW_HEADS // SW_KV_HEADS
WINDOW = 128
ROPE_THETA = 10000.0
GD_QK_HEADS = 8
GD_V_HEADS = 16
GD_HD = 128
GD_CONV = 4
GD_QK_W = GD_QK_HEADS * GD_HD
GD_V_W = GD_V_HEADS * GD_HD
GD_CONV_CH = 2 * GD_QK_W + GD_V_W
N_EXPERTS = 16
N_GROUPS = 4
EXP_PER_GROUP = 4
D_EXPERT = 512
DN_ALPHA = (2 * DEPTH) ** 0.25
LN_EPS = 1e-5
RMS_EPS = 1e-6

LANES = 128
ML_W = 4 * ML_HEADS * ML_DK
SW_W = SW_HEADS * SW_HD + 2 * SW_KV_HEADS * SW_HD
VMEM_LIMIT = 56 * 1024 * 1024

SDS = jax.ShapeDtypeStruct


def _cparams(*sem):
    return pltpu.CompilerParams(dimension_semantics=sem, vmem_limit_bytes=VMEM_LIMIT)


def _dot(a, b):
    return jnp.dot(a, b, preferred_element_type=f32, precision=HIGHEST)


def _dot_nt(a, b):
    return lax.dot_general(a, b, (((1,), (1,)), ((), ())), preferred_element_type=f32, precision=HIGHEST)


def _bdot(a, b):
    return jnp.dot(a.astype(bf16), b.astype(bf16), preferred_element_type=f32)


def _bdot_nt(a, b):
    return lax.dot_general(a.astype(bf16), b.astype(bf16), (((1,), (1,)), ((), ())), preferred_element_type=f32)


def _bdot_tn(a, b):
    return jnp.dot(a.T.astype(bf16), b.astype(bf16), preferred_element_type=f32)


def _sigmoid(x):
    return 1.0 / (1.0 + jnp.exp(-x))


def _silu(x):
    return x * _sigmoid(x)


def _softplus(x):
    return jnp.maximum(x, 0.0) + jnp.log(1.0 + jnp.exp(-jnp.abs(x)))


def _layer_norm(v, g, b):
    mu = jnp.mean(v, axis=-1, keepdims=True)
    d = v - mu
    var = jnp.mean(d * d, axis=-1, keepdims=True)
    return d * lax.rsqrt(var + LN_EPS) * g + b


def _proj_kernel(x_ref, w_ref, *o_refs, splits, col_chunk):
    xb = x_ref[...].astype(bf16)
    for o_ref, (start, width) in zip(o_refs, splits):
        for c in range(0, width, col_chunk):
            cw = min(col_chunk, width - c)
            o_ref[:, c:c + cw] = jnp.dot(xb, w_ref[:, start + c:start + c + cw], preferred_element_type=f32)


def _proj(x2, w, splits, tm):
    n, k = x2.shape
    return pl.pallas_call(
        functools.partial(_proj_kernel, splits=splits, col_chunk=512),
        grid=(n // tm,),
        in_specs=[pl.BlockSpec((tm, k), lambda i: (i, 0)),
                  pl.BlockSpec(w.shape, lambda i: (0, 0), pipeline_mode=pl.Buffered(1))],
        out_specs=[pl.BlockSpec((tm, wd), lambda i: (i, 0)) for _, wd in splits],
        out_shape=[SDS((n, wd), f32) for _, wd in splits],
        compiler_params=_cparams("parallel"),
        name="in_proj",
    )(x2, w)


def _mlstm_kernel(z_ref, g_ref, bias_ref, nw_ref, cn0_ref, m0_ref, h_ref, cn_ref, m_ref, *, L, BB):
    @pl.when(pl.program_id(1) == 0)
    def _():
        cn_ref[...] = cn0_ref[...]
        m_ref[...] = m0_ref[...]

    row = lax.broadcasted_iota(jnp.int32, (L, L), 0)
    col = lax.broadcasted_iota(jnp.int32, (L, L), 1)
    causal = row >= col
    tri = causal.astype(f32)
    lane = lax.broadcasted_iota(jnp.int32, (1, LANES), 1)
    lane_l = lax.broadcasted_iota(jnp.int32, (L, ML_DV), 1)
    one_hot0 = (lane_l == 0).astype(f32)
    for bi in range(BB):
        g = g_ref[bi] + bias_ref[...]
        lf = jnp.minimum(g, 0.0) - jnp.log(1.0 + jnp.exp(-jnp.abs(g)))
        bcum = _dot(tri, lf)
        b_t = bcum.T
        g_t = g.T
        m_row = m_ref[bi]
        new_m = m_row
        outs = []
        for h in range(ML_HEADS):
            b_col = bcum[:, ML_HEADS + h:ML_HEADS + h + 1]
            b_row = b_t[ML_HEADS + h:ML_HEADS + h + 1, :]
            ig_row = g_t[h:h + 1, :]
            ig_col = g[:, h:h + 1]
            m_h = m_row[:, h:h + 1]
            dmat = jnp.where(causal, b_col - b_row + ig_row, -jnp.inf)
            inter = b_col + m_h
            mt = jnp.maximum(inter, jnp.max(dmat, axis=-1, keepdims=True))
            a = jnp.exp(inter - mt)
            q = z_ref[bi, :, h * ML_DK:(h + 1) * ML_DK]
            k = z_ref[bi, :, ML_HEADS * ML_DK + h * ML_DK:ML_HEADS * ML_DK + (h + 1) * ML_DK] * (ML_DK ** -0.5)
            v = z_ref[bi, :, 2 * ML_HEADS * ML_DK + h * ML_DV:2 * ML_HEADS * ML_DK + (h + 1) * ML_DV]
            og = z_ref[bi, :, 3 * ML_HEADS * ML_DK + h * ML_DV:3 * ML_HEADS * ML_DK + (h + 1) * ML_DV]
            s = _bdot_nt(q, k) * jnp.exp(dmat - mt)
            vext = jnp.concatenate([v, one_hot0], axis=-1)
            cn = cn_ref[bi, h]
            tot = a * _bdot(q, cn) + _bdot(s, vext)
            num = tot[:, :ML_DV]
            den = tot[:, ML_DV:ML_DV + 1]
            hh = num / jnp.maximum(jnp.abs(den), jnp.exp(-mt))
            hh = hh * lax.rsqrt(jnp.mean(hh * hh, axis=-1, keepdims=True) + RMS_EPS) * nw_ref[:, h * ML_DV:(h + 1) * ML_DV]
            outs.append(hh * _sigmoid(og))
            m_new = mt[L - 1:L, :]
            b_last = b_col[L - 1:L, :]
            wk = jnp.exp(b_last - b_col + ig_col - m_new)
            dec = jnp.exp(b_last + m_h - m_new)
            cn_ref[bi, h] = dec * cn + _bdot_tn(k, wk * vext)
            new_m = jnp.where(lane == h, m_new, new_m)
        m_ref[bi] = new_m
        h_ref[bi] = jnp.concatenate(outs, axis=-1)


def _mlstm(z_ml, z_g, bias_row, norm_row, cn0, m0, B, T, L):
    nc = T // L
    bb = min(B, 4)
    tok = lambda b, c: (b, c, 0)
    st4 = lambda b, c: (b, 0, 0, 0)
    st3 = lambda b, c: (b, 0, 0)
    return pl.pallas_call(
        functools.partial(_mlstm_kernel, L=L, BB=bb),
        grid=(B // bb, nc),
        in_specs=[pl.BlockSpec((bb, L, ML_W), tok),
                  pl.BlockSpec((bb, L, LANES), tok),
                  pl.BlockSpec((1, LANES), lambda b, c: (0, 0)),
                  pl.BlockSpec((1, ML_HEADS * ML_DV), lambda b, c: (0, 0)),
                  pl.BlockSpec((bb, ML_HEADS, ML_DK, LANES), st4),
                  pl.BlockSpec((bb, 1, LANES), st3)],
        out_specs=[pl.BlockSpec((bb, L, ML_HEADS * ML_DV), tok),
                   pl.BlockSpec((bb, ML_HEADS, ML_DK, LANES), st4),
                   pl.BlockSpec((bb, 1, LANES), st3)],
        out_shape=[SDS((B, T, ML_HEADS * ML_DV), f32),
                   SDS((B, ML_HEADS, ML_DK, LANES), f32),
                   SDS((B, 1, LANES), f32)],
        compiler_params=_cparams("parallel", "arbitrary"),
        name="mlstm",
    )(z_ml.reshape(B, T, ML_W), z_g.reshape(B, T, LANES), bias_row, norm_row, cn0, m0)


def _rope(x, cos, sin_signed):
    w = x.shape[-1]
    lane = lax.broadcasted_iota(jnp.int32, x.shape, 1)
    swapped = jnp.where((lane % SW_HD) < SW_HD // 2, pltpu.roll(x, w - SW_HD // 2, 1), pltpu.roll(x, SW_HD // 2, 1))
    return x * cos + swapped * sin_signed


def _swa_attend(jobs, sinks_ref, L):
    units = [(j, g) for j in range(len(jobs)) for g in range(SW_KV_HEADS)]
    sinks = [jnp.concatenate(
        [jnp.broadcast_to(sinks_ref[:, g * SW_GROUP + i:g * SW_GROUP + i + 1], (L, 1)) for i in range(SW_GROUP)],
        axis=0) for g in range(SW_KV_HEADS)]
    s, p = {}, {}
    for j, g in units:
        qr, keys, _, _ = jobs[j]
        q4 = jnp.concatenate([qr[:, (g * SW_GROUP + i) * SW_HD:(g * SW_GROUP + i + 1) * SW_HD]
                              for i in range(SW_GROUP)], axis=0)
        s[j, g] = _bdot_nt(q4, keys[:, g * SW_HD:(g + 1) * SW_HD]) * (SW_HD ** -0.5)
    for j, g in units:
        first_valid = jobs[j][3]
        sc = s[j, g]
        if first_valid is not None:
            kcol = lax.broadcasted_iota(jnp.int32, (1, sc.shape[1]), 1)
            sc = jnp.where(kcol >= first_valid, sc, -jnp.inf)
        mx = jnp.maximum(jnp.max(sc, axis=-1, keepdims=True), sinks[g])
        e = jnp.exp(sc - mx)
        p[j, g] = e / (jnp.sum(e, axis=-1, keepdims=True) + jnp.exp(sinks[g] - mx))
    o = {u: _bdot(p[u], jobs[u[0]][2][:, u[1] * SW_HD:(u[1] + 1) * SW_HD]) for u in units}
    return [jnp.concatenate([o[j, g][i * L:(i + 1) * L, :] for g in range(SW_KV_HEADS) for i in range(SW_GROUP)],
                            axis=-1) for j in range(len(jobs))]


def _swa_prompt_kernel(q_ref, kp_ref, kc_ref, vp_ref, vc_ref, cp_ref, cc_ref, sp_ref, sc_ref, sinks_ref,
                       o_ref, kr_ref, *, L, CB):
    i = pl.program_id(1)
    rows = CB * L
    back = 2 * L
    cos_q = jnp.concatenate([cc_ref[...]] * (SW_HEADS // SW_KV_HEADS), axis=-1)
    sin_q = jnp.concatenate([sc_ref[...]] * (SW_HEADS // SW_KV_HEADS), axis=-1)
    qr = _rope(q_ref[...], cos_q, sin_q)
    k_cur = _rope(kc_ref[...], cc_ref[...], sc_ref[...])
    kr_ref[...] = k_cur
    k_prev = _rope(kp_ref[rows - back:rows, :], cp_ref[rows - back:rows, :], sp_ref[rows - back:rows, :])
    keys = jnp.concatenate([k_prev, k_cur], axis=0)
    vals = jnp.concatenate([vp_ref[rows - back:rows, :], vc_ref[...]], axis=0)
    jobs = []
    for u in range(CB):
        first_valid = jnp.where(i == 0, back - u * L, 0) if u * L < back else None
        jobs.append((qr[u * L:(u + 1) * L], keys[u * L:(u + 3) * L], vals[u * L:(u + 3) * L], first_valid))
    o_ref[...] = jnp.concatenate(_swa_attend(jobs, sinks_ref, L), axis=0)


def _swa_prompt(z_sw, cos_t, sin_t, sinks_row, B, T, L):
    cb = 4
    rows = cb * L
    nb = T // rows
    n = B * T
    kcol = SW_HEADS * SW_HD // LANES
    vcol = kcol + 1
    cur = lambda b, i: (b * nb + i, 0)
    prev = lambda col: (lambda b, i: (b * nb + jnp.maximum(i - 1, 0), col))
    curc = lambda col: (lambda b, i: (b * nb + i, col))
    tab_cur = lambda b, i: (i, 0)
    tab_prev = lambda b, i: (jnp.maximum(i - 1, 0), 0)
    return pl.pallas_call(
        functools.partial(_swa_prompt_kernel, L=L, CB=cb),
        grid=(B, nb),
        in_specs=[pl.BlockSpec((rows, SW_HEADS * SW_HD), cur),
                  pl.BlockSpec((rows, LANES), prev(kcol)), pl.BlockSpec((rows, LANES), curc(kcol)),
                  pl.BlockSpec((rows, LANES), prev(vcol)), pl.BlockSpec((rows, LANES), curc(vcol)),
                  pl.BlockSpec((rows, LANES), tab_prev), pl.BlockSpec((rows, LANES), tab_cur),
                  pl.BlockSpec((rows, LANES), tab_prev), pl.BlockSpec((rows, LANES), tab_cur),
                  pl.BlockSpec((1, LANES), lambda b, i: (0, 0))],
        out_specs=[pl.BlockSpec((rows, SW_HEADS * SW_HD), cur),
                   pl.BlockSpec((rows, LANES), cur)],
        out_shape=[SDS((n, SW_HEADS * SW_HD), f32), SDS((n, LANES), f32)],
        compiler_params=_cparams("parallel", "parallel"),
        name="swa_prompt",
    )(z_sw, z_sw, z_sw, z_sw, z_sw, cos_t, cos_t, sin_t, sin_t, sinks_row)


def _swa_sample_kernel(q_ref, k_ref, v_ref, ck_ref, cv_ref, cos_ref, sin_ref, sinks_ref, o_ref, kr_ref, *, L):
    cos_q = jnp.concatenate([cos_ref[...]] * (SW_HEADS // SW_KV_HEADS), axis=-1)
    sin_q = jnp.concatenate([sin_ref[...]] * (SW_HEADS // SW_KV_HEADS), axis=-1)
    qr = _rope(q_ref[...], cos_q, sin_q)
    kr = _rope(k_ref[...], cos_ref[...], sin_ref[...])
    kr_ref[...] = kr
    keys = jnp.concatenate([ck_ref[0], kr], axis=0)
    vals = jnp.concatenate([cv_ref[0], v_ref[...]], axis=0)
    o_ref[...] = _swa_attend([(qr, keys, vals, None)], sinks_ref, L)[0]


def _swa_sample(z_sw, cache_k, cache_v, cos_t, sin_t, sinks_row, B, T):
    n = B * T
    kcol = SW_HEADS * SW_HD // LANES
    return pl.pallas_call(
        functools.partial(_swa_sample_kernel, L=T),
        grid=(B,),
        in_specs=[pl.BlockSpec((T, SW_HEADS * SW_HD), lambda b: (b, 0)),
                  pl.BlockSpec((T, LANES), lambda b: (b, kcol)),
                  pl.BlockSpec((T, LANES), lambda b: (b, kcol + 1)),
                  pl.BlockSpec((1, WINDOW, LANES), lambda b: (b, 0, 0)),
                  pl.BlockSpec((1, WINDOW, LANES), lambda b: (b, 0, 0)),
                  pl.BlockSpec((T, LANES), lambda b: (0, 0)),
                  pl.BlockSpec((T, LANES), lambda b: (0, 0)),
                  pl.BlockSpec((1, LANES), lambda b: (0, 0))],
        out_specs=[pl.BlockSpec((T, SW_HEADS * SW_HD), lambda b: (b, 0)),
                   pl.BlockSpec((T, LANES), lambda b: (b, 0))],
        out_shape=[SDS((n, SW_HEADS * SW_HD), f32), SDS((n, LANES), f32)],
        compiler_params=_cparams("parallel"),
        name="swa_sample",
    )(z_sw, z_sw, z_sw, cache_k, cache_v, cos_t, sin_t, sinks_row)


def _out_ln_kernel(*refs, n_in):
    x_ref = refs[0]
    a_refs = refs[1:1 + n_in]
    w_refs = refs[1 + n_in:1 + 2 * n_in]
    g_ref, b_ref, o_ref = refs[1 + 2 * n_in:]
    y = _bdot(a_refs[0][...], w_refs[0][...])
    for a_ref, w_ref in zip(a_refs[1:], w_refs[1:]):
        y = y + _bdot(a_ref[...], w_ref[...])
    o_ref[...] = _layer_norm(DN_ALPHA * x_ref[...] + y, g_ref[...], b_ref[...])


def _out_ln(x2, acts, ws, g_row, b_row, tm):
    n = x2.shape[0]
    row = lambda i: (i, 0)
    const = lambda i: (0, 0)
    return pl.pallas_call(
        functools.partial(_out_ln_kernel, n_in=len(acts)),
        grid=(n // tm,),
        in_specs=[pl.BlockSpec((tm, D_MODEL), row)]
        + [pl.BlockSpec((tm, a.shape[1]), row) for a in acts]
        + [pl.BlockSpec(w.shape, const) for w in ws]
        + [pl.BlockSpec((1, D_MODEL), const), pl.BlockSpec((1, D_MODEL), const)],
        out_specs=pl.BlockSpec((tm, D_MODEL), row),
        out_shape=SDS((n, D_MODEL), f32),
        compiler_params=_cparams("parallel"),
        name="out_proj_ln",
    )(x2, *acts, *ws, g_row, b_row)


def _router_kernel(x_ref, rw_ref, rb_ref, o_ref, route_ref):
    logits = _dot_nt(rw_ref[...], x_ref[...])
    aff = _sigmoid(logits)
    sc = aff + rb_ref[...]
    s = [sc[e:e + 1, :] for e in range(N_EXPERTS)]
    a = [aff[e:e + 1, :] for e in range(N_EXPERTS)]
    scores = []
    for gi in range(N_GROUPS):
        w, x, y, z = s[4 * gi:4 * gi + 4]
        p, q = jnp.maximum(w, x), jnp.minimum(w, x)
        r, t = jnp.maximum(y, z), jnp.minimum(y, z)
        scores.append(jnp.maximum(p, r) + jnp.maximum(jnp.minimum(p, r), jnp.maximum(q, t)))
    best = scores[0]
    gsel = jnp.zeros_like(best, dtype=jnp.int32)
    for gi in range(1, N_GROUPS):
        better = scores[gi] > best
        best = jnp.where(better, scores[gi], best)
        gsel = jnp.where(better, gi, gsel)
    sel = []
    for e in range(N_EXPERTS):
        gi, i = divmod(e, EXP_PER_GROUP)
        beaten = jnp.zeros_like(gsel)
        for j in range(EXP_PER_GROUP):
            if j == i:
                continue
            o = s[4 * gi + j]
            wins = (o >= s[e]) if j < i else (o > s[e])
            beaten = beaten + wins.astype(jnp.int32)
        sel.append((gsel == gi) & (beaten < 2))
    den = jnp.zeros_like(best)
    for e in range(N_EXPERTS):
        den = den + jnp.where(sel[e], a[e], 0.0)
    gate = [jnp.where(sel[e], a[e] / den, 0.0) for e in range(N_EXPERTS)]
    o_ref[...] = jnp.concatenate(gate, axis=0)
    taken = jnp.zeros_like(gsel)
    ea = eb = wa = wb = jnp.zeros_like(best)
    for e in range(N_EXPERTS):
        first = sel[e] & (taken == 0)
        second = sel[e] & (taken == 1)
        ea = jnp.where(first, float(e), ea)
        wa = jnp.where(first, gate[e], wa)
        eb = jnp.where(second, float(e), eb)
        wb = jnp.where(second, gate[e], wb)
        taken = taken + sel[e].astype(jnp.int32)
    route_ref[...] = jnp.concatenate([ea, eb, wa, wb, jnp.zeros((4, ea.shape[1]), f32)], axis=0)


def _router(x2, rw_t, rb_col, tm):
    n = x2.shape[0]
    return pl.pallas_call(
        _router_kernel,
        grid=(n // tm,),
        in_specs=[pl.BlockSpec((tm, D_MODEL), lambda i: (i, 0)),
                  pl.BlockSpec((N_EXPERTS, D_MODEL), lambda i: (0, 0)),
                  pl.BlockSpec((N_EXPERTS, 1), lambda i: (0, 0))],
        out_specs=[pl.BlockSpec((N_EXPERTS, tm), lambda i: (0, i)),
                   pl.BlockSpec((8, tm), lambda i: (0, i))],
        out_shape=[SDS((N_EXPERTS, n), f32), SDS((8, n), f32)],
        compiler_params=_cparams("parallel"),
        name="router",
    )(x2, rw_t, rb_col)


def _moe_kernel(x_ref, gates_ref, wg_ref, wu_ref, wd_ref, g_ref, b_ref, o_ref, xb_ref, acc_ref):
    e = pl.program_id(1)

    @pl.when(e == 0)
    def _():
        xb_ref[...] = x_ref[...].astype(bf16)
        acc_ref[...] = jnp.zeros_like(acc_ref)

    xb = xb_ref[...]
    lane = lax.broadcasted_iota(jnp.int32, gates_ref.shape, 1)
    gcol = jnp.sum(jnp.where(lane == e, gates_ref[...], 0.0), axis=-1, keepdims=True)
    h = _silu(jnp.dot(xb, wg_ref[0], preferred_element_type=f32)) * jnp.dot(xb, wu_ref[0], preferred_element_type=f32)
    acc_ref[...] += jnp.dot((gcol * h).astype(bf16), wd_ref[0], preferred_element_type=f32)

    @pl.when(e == N_EXPERTS - 1)
    def _():
        o_ref[...] = _layer_norm(DN_ALPHA * x_ref[...] + acc_ref[...], g_ref[...], b_ref[...])


def _moe_ln(x2, gates, wg, wu, wd, g_row, b_row, tm):
    n = x2.shape[0]
    row = lambda i, e: (i, 0)
    const = lambda i, e: (0, 0)
    return pl.pallas_call(
        _moe_kernel,
        grid=(n // tm, N_EXPERTS),
        in_specs=[pl.BlockSpec((tm, D_MODEL), row),
                  pl.BlockSpec((tm, N_EXPERTS), row),
                  pl.BlockSpec((1, D_MODEL, D_EXPERT), lambda i, e: (e, 0, 0)),
                  pl.BlockSpec((1, D_MODEL, D_EXPERT), lambda i, e: (e, 0, 0)),
                  pl.BlockSpec((1, D_EXPERT, D_MODEL), lambda i, e: (e, 0, 0)),
                  pl.BlockSpec((1, D_MODEL), const), pl.BlockSpec((1, D_MODEL), const)],
        out_specs=pl.BlockSpec((tm, D_MODEL), row),
        out_shape=SDS((n, D_MODEL), f32),
        scratch_shapes=[pltpu.VMEM((tm, D_MODEL), bf16), pltpu.VMEM((tm, D_MODEL), f32)],
        compiler_params=_cparams("parallel", "arbitrary"),
        name="moe_ln",
    )(x2, gates, wg, wu, wd, g_row, b_row)


N_PAIRS = N_GROUPS * (EXP_PER_GROUP * (EXP_PER_GROUP - 1) // 2)
MOE_TM = 256
_PAIR_A = [g * EXP_PER_GROUP + a for g in range(N_GROUPS) for a in range(EXP_PER_GROUP) for b in range(a + 1, EXP_PER_GROUP)]
_PAIR_B = [g * EXP_PER_GROUP + b for g in range(N_GROUPS) for a in range(EXP_PER_GROUP) for b in range(a + 1, EXP_PER_GROUP)]


def _gather_rows_kernel(idx_ref, src_ref, o_ref, sem, *, rows):
    base = pl.program_id(0) * rows

    def row_copy(j):
        return pltpu.make_async_copy(src_ref.at[pl.ds(idx_ref[base + j], 1)], o_ref.at[pl.ds(j, 1)], sem)

    def issue(j, carry):
        row_copy(j).start()
        return carry

    def drain(j, carry):
        row_copy(j).wait()
        return carry

    lax.fori_loop(0, rows, issue, 0, unroll=8)
    lax.fori_loop(0, rows, drain, 0, unroll=8)


def _gather_rows(src, idx, rows):
    n_out = idx.shape[0]
    d = src.shape[1]
    return pl.pallas_call(
        functools.partial(_gather_rows_kernel, rows=rows),
        grid_spec=pltpu.PrefetchScalarGridSpec(
            num_scalar_prefetch=1,
            grid=(n_out // rows,),
            in_specs=[pl.BlockSpec(memory_space=pl.ANY)],
            out_specs=pl.BlockSpec((rows, d), lambda i, idx_ref: (i, 0)),
            scratch_shapes=[pltpu.SemaphoreType.DMA(())]),
        out_shape=SDS((n_out, d), f32),
        compiler_params=_cparams("arbitrary"),
        name="gather_rows",
    )(idx, src)


def _scatter_rows_kernel(idx_ref, src_ref, init_ref, o_ref, sem, *, rows):
    del init_ref
    base = pl.program_id(0) * rows

    def row_copy(j):
        return pltpu.make_async_copy(src_ref.at[pl.ds(j, 1)], o_ref.at[pl.ds(idx_ref[base + j], 1)], sem)

    def issue(j, carry):
        row_copy(j).start()
        return carry

    def drain(j, carry):
        row_copy(j).wait()
        return carry

    lax.fori_loop(0, rows, issue, 0, unroll=8)
    lax.fori_loop(0, rows, drain, 0, unroll=8)


def _scatter_rows(src, idx, n_out, rows):
    n_src, d = src.shape
    return pl.pallas_call(
        functools.partial(_scatter_rows_kernel, rows=rows),
        grid_spec=pltpu.PrefetchScalarGridSpec(
            num_scalar_prefetch=1,
            grid=(n_src // rows,),
            in_specs=[pl.BlockSpec((rows, d), lambda i, idx_ref: (i, 0)), pl.BlockSpec(memory_space=pl.ANY)],
            out_specs=pl.BlockSpec(memory_space=pl.ANY),
            scratch_shapes=[pltpu.SemaphoreType.DMA(())]),
        out_shape=SDS((n_out, d), f32),
        input_output_aliases={2: 0},
        compiler_params=_cparams("arbitrary"),
        name="scatter_rows",
    )(idx, src, jnp.zeros((n_out, d), f32))


def _pair_plan(route, n, tm):
    ea = route[0].astype(jnp.int32)
    eb = route[1].astype(jnp.int32)
    a = ea % EXP_PER_GROUP
    b = eb % EXP_PER_GROUP
    pidx = jnp.where(a == 0, b - 1, jnp.where(a == 1, b + 1, 5))
    pair = (ea // EXP_PER_GROUP) * (N_PAIRS // N_GROUPS) + pidx
    onehot = (pair[:, None] == jnp.arange(N_PAIRS, dtype=jnp.int32)[None, :]).astype(jnp.int32)
    csum = jnp.cumsum(onehot, axis=0)
    counts = csum[-1]
    ntiles = (counts + tm - 1) // tm
    tile_end = jnp.cumsum(ntiles)
    tile_start = tile_end - ntiles
    row_of_token = jnp.sum(onehot * (csum - 1 + (tile_start * tm)[None, :]), axis=1)
    nt = n // tm + N_PAIRS
    tile_id = jnp.arange(nt, dtype=jnp.int32)
    tile_valid = tile_id < tile_end[-1]
    tile_pair = jnp.sum((tile_end[None, :] <= jnp.minimum(tile_id, tile_end[-1] - 1)[:, None]).astype(jnp.int32), axis=1)
    tile_pair = jnp.minimum(tile_pair, N_PAIRS - 1)
    pick = (tile_pair[:, None] == jnp.arange(N_PAIRS, dtype=jnp.int32)[None, :]).astype(jnp.int32)
    tile_a = jnp.sum(pick * jnp.asarray(_PAIR_A, jnp.int32)[None, :], axis=1)
    tile_b = jnp.sum(pick * jnp.asarray(_PAIR_B, jnp.int32)[None, :], axis=1)
    return row_of_token, tile_a, tile_b, tile_valid.astype(jnp.int32)


def _pair_expert_kernel(ta_ref, tb_ref, tv_ref, x_ref, rwt_ref, wga_ref, wua_ref, wda_ref, wgb_ref, wub_ref, wdb_ref,
                        g_ref, b_ref, o_ref):
    i = pl.program_id(0)
    valid = tv_ref[i] == 1

    @pl.when(valid)
    def _():
        x = x_ref[...]
        xb = x.astype(bf16)
        aff_a = _sigmoid(jnp.sum(x * rwt_ref[pl.ds(ta_ref[i], 1), :], axis=-1, keepdims=True))
        aff_b = _sigmoid(jnp.sum(x * rwt_ref[pl.ds(tb_ref[i], 1), :], axis=-1, keepdims=True))
        den = aff_a + aff_b
        acc = None
        for w, (wg, wu, wd) in ((aff_a / den, (wga_ref, wua_ref, wda_ref)), (aff_b / den, (wgb_ref, wub_ref, wdb_ref))):
            h = _silu(jnp.dot(xb, wg[0], preferred_element_type=f32)) * jnp.dot(xb, wu[0], preferred_element_type=f32)
            y = jnp.dot((w * h).astype(bf16), wd[0], preferred_element_type=f32)
            acc = y if acc is None else acc + y
        o_ref[...] = _layer_norm(DN_ALPHA * x + acc, g_ref[...], b_ref[...])

    @pl.when(jnp.logical_not(valid))
    def _():
        o_ref[...] = jnp.zeros_like(o_ref)


def _pair_experts(xs, rw_t, tile_a, tile_b, tile_valid, wg, wu, wd, g_row, b_row, tm):
    rows = xs.shape[0]
    row = lambda i, ta, tb, tv: (i, 0)
    const = lambda i, ta, tb, tv: (0, 0)
    ex_a = lambda i, ta, tb, tv: (ta[i], 0, 0)
    ex_b = lambda i, ta, tb, tv: (tb[i], 0, 0)
    up = pl.BlockSpec((1, D_MODEL, D_EXPERT), ex_a), pl.BlockSpec((1, D_MODEL, D_EXPERT), ex_b)
    down = pl.BlockSpec((1, D_EXPERT, D_MODEL), ex_a), pl.BlockSpec((1, D_EXPERT, D_MODEL), ex_b)
    return pl.pallas_call(
        _pair_expert_kernel,
        grid_spec=pltpu.PrefetchScalarGridSpec(
            num_scalar_prefetch=3,
            grid=(rows // tm,),
            in_specs=[pl.BlockSpec((tm, D_MODEL), row), pl.BlockSpec((N_EXPERTS, D_MODEL), const),
                      up[0], up[0], down[0], up[1], up[1], down[1],
                      pl.BlockSpec((1, D_MODEL), const), pl.BlockSpec((1, D_MODEL), const)],
            out_specs=pl.BlockSpec((tm, D_MODEL), row)),
        out_shape=SDS((rows, D_MODEL), f32),
        compiler_params=_cparams("arbitrary"),
        name="pair_experts",
    )(tile_a, tile_b, tile_valid, xs, rw_t, wg, wu, wd, wg, wu, wd, g_row, b_row)


def _split(a):
    hi = a.astype(bf16)
    return hi, (a - hi.astype(f32)).astype(bf16)


def _dot3(a, b):
    (ah, al), (bh, bl) = a, b
    mm = lambda x, y: jnp.dot(x, y, preferred_element_type=f32)
    return mm(ah, bh) + (mm(ah, bl) + mm(al, bh))


def _unit_lower_inverses(mats, L):
    row = lax.broadcasted_iota(jnp.int32, (L, L), 0)
    col = lax.broadcasted_iota(jnp.int32, (L, L), 1)
    eye = (row == col).astype(f32)
    ps = [eye - a for a in mats]
    pws = [_split(a) for a in mats]
    span = 2
    while span < L:
        pws = [_split(_dot3(pw, pw)) for pw in pws]
        ps = [p + _dot3(pw, _split(p)) for p, pw in zip(ps, pws)]
        span *= 2
    return ps


def _gdn_kernel(x_ref, zg_ref, ba_ref, cw_ref, alog_ref, dt_ref, nw_ref, s0_ref, cb_ref,
                o_ref, s_ref, prev_ref, *, L):
    @pl.when(pl.program_id(1) == 0)
    def _():
        s_ref[...] = s0_ref[...]
        prev_ref[...] = cb_ref[0]

    def conv_silu(lo, width):
        cur = x_ref[:, lo:lo + width]
        cat = jnp.concatenate([prev_ref[:, lo:lo + width], cur], axis=0)
        acc = cat[5:5 + L] * cw_ref[0:1, lo:lo + width]
        acc = acc + cat[6:6 + L] * cw_ref[1:2, lo:lo + width]
        acc = acc + cat[7:7 + L] * cw_ref[2:3, lo:lo + width]
        acc = acc + cur * cw_ref[3:4, lo:lo + width]
        return _silu(acc)

    def l2n(v):
        return v * lax.rsqrt(jnp.sum(v * v, axis=-1, keepdims=True) + 1e-6)

    ba = ba_ref[...]
    beta = _sigmoid(ba)
    gl = -jnp.exp(alog_ref[...]) * _softplus(ba + dt_ref[...])
    row = lax.broadcasted_iota(jnp.int32, (L, L), 0)
    col = lax.broadcasted_iota(jnp.int32, (L, L), 1)
    incl = row >= col
    strict = row > col
    gcum = _dot(incl.astype(f32), gl)
    gcum_t = gcum.T
    rep = GD_V_HEADS // GD_QK_HEADS
    qs, ks, amats, qkds, egs, g_cols, rhss = [], [], [], [], [], [], []
    for j in range(GD_QK_HEADS):
        q = l2n(conv_silu(j * GD_HD, GD_HD)) * (GD_HD ** -0.5)
        k = l2n(conv_silu(GD_QK_W + j * GD_HD, GD_HD))
        qb, kb = q.astype(bf16), k.astype(bf16)
        kk = lax.dot_general(kb, kb, (((1,), (1,)), ((), ())), preferred_element_type=f32)
        qk = lax.dot_general(qb, kb, (((1,), (1,)), ((), ())), preferred_element_type=f32)
        qs.append(qb)
        ks.append(k)
        for r in range(rep):
            hv = j * rep + r
            v = conv_silu(2 * GD_QK_W + hv * GD_HD, GD_HD)
            g_col = gcum[:, GD_V_HEADS + hv:GD_V_HEADS + hv + 1]
            g_row = gcum_t[GD_V_HEADS + hv:GD_V_HEADS + hv + 1, :]
            b_col = beta[:, hv:hv + 1]
            decay = jnp.exp(jnp.where(incl, g_col - g_row, -jnp.inf))
            eg = jnp.exp(g_col)
            amats.append(jnp.where(strict, b_col * kk * decay, 0.0))
            qkds.append((qk * decay).astype(bf16))
            egs.append(eg)
            g_cols.append(g_col)
            rhss.append(jnp.concatenate([b_col * v, (b_col * eg) * k], axis=-1))
    tinvs = _unit_lower_inverses(amats, L)
    sols = [_dot3(_split(t), _split(r)) for t, r in zip(tinvs, rhss)]
    for hv in range(GD_V_HEADS):
        j = hv // rep
        st = s_ref[0, hv]
        stb = st.astype(bf16)
        sol = sols[hv]
        wn = sol[:, :GD_HD] - jnp.dot(sol[:, GD_HD:].astype(bf16), stb, preferred_element_type=f32)
        wnb = wn.astype(bf16)
        o = egs[hv] * jnp.dot(qs[j], stb, preferred_element_type=f32) + jnp.dot(qkds[hv], wnb, preferred_element_type=f32)
        g_col = g_cols[hv]
        g_last = g_col[L - 1:L, :]
        kt = (jnp.exp(g_last - g_col) * ks[j]).T.astype(bf16)
        s_ref[0, hv] = jnp.exp(g_last) * st + jnp.dot(kt, wnb, preferred_element_type=f32)
        o = o * lax.rsqrt(jnp.mean(o * o, axis=-1, keepdims=True) + RMS_EPS) * nw_ref[...]
        o_ref[:, hv * GD_HD:(hv + 1) * GD_HD] = o * _silu(zg_ref[:, hv * GD_HD:(hv + 1) * GD_HD])
    prev_ref[...] = x_ref[L - 8:L, :]


def _gdn(qkv, zg, ba, conv_w, alog_row, dt_row, norm_row, s0, conv8, B, T, L):
    nc = T // L
    n = B * T
    tok = lambda b, c: (b * nc + c, 0)
    const = lambda b, c: (0, 0)
    return pl.pallas_call(
        functools.partial(_gdn_kernel, L=L),
        grid=(B, nc),
        in_specs=[pl.BlockSpec((L, GD_CONV_CH), tok),
                  pl.BlockSpec((L, GD_V_W), tok),
                  pl.BlockSpec((L, LANES), tok),
                  pl.BlockSpec((GD_CONV, GD_CONV_CH), const),
                  pl.BlockSpec((1, LANES), const),
                  pl.BlockSpec((1, LANES), const),
                  pl.BlockSpec((1, GD_HD), const),
                  pl.BlockSpec((1, GD_V_HEADS, GD_HD, GD_HD), lambda b, c: (b, 0, 0, 0)),
                  pl.BlockSpec((1, 8, GD_CONV_CH), lambda b, c: (b, 0, 0))],
        out_specs=[pl.BlockSpec((L, GD_V_W), tok),
                   pl.BlockSpec((1, GD_V_HEADS, GD_HD, GD_HD), lambda b, c: (b, 0, 0, 0))],
        out_shape=[SDS((n, GD_V_W), f32), SDS((B, GD_V_HEADS, GD_HD, GD_HD), f32)],
        scratch_shapes=[pltpu.VMEM((8, GD_CONV_CH), f32)],
        compiler_params=_cparams("parallel", "arbitrary"),
        name="gdn",
    )(qkv, zg, ba, conv_w, alog_row, dt_row, norm_row, s0, conv8)


def _pad_lanes(row, offset=0):
    return jnp.zeros((1, LANES), f32).at[0, offset:offset + row.shape[0]].set(row.astype(f32))


def _rope_tables(pos):
    half = SW_HD // 2
    inv = ROPE_THETA ** (-jnp.arange(half, dtype=f32) / half)
    ang = pos.astype(f32)[:, None] * inv[None, :]
    cos, sin = jnp.cos(ang), jnp.sin(ang)
    cos_t = jnp.concatenate([cos, cos] * SW_KV_HEADS, axis=-1)
    sin_t = jnp.concatenate([-sin, sin] * SW_KV_HEADS, axis=-1)
    return cos_t, sin_t


def _tile(n, pref):
    return pref if n % pref == 0 else n


def _trunk(x, pos, L, state, p):
    B, T, _ = x.shape
    n = B * T
    x2 = x.reshape(n, D_MODEL)
    tm = _tile(n, 512)

    z_ml, z_sw, z_g = _proj(x2, p["ab_w"], ((0, ML_W), (ML_W, SW_W), (ML_W + SW_W, LANES)), tm)
    if state is None:
        cn0 = jnp.zeros((B, ML_HEADS, ML_DK, LANES), f32)
        m0 = jnp.zeros((B, 1, LANES), f32)
    else:
        cn0 = jnp.concatenate([state["ml_C"], state["ml_n"][..., None],
                               jnp.zeros((B, ML_HEADS, ML_DK, LANES - ML_DV - 1), f32)], axis=-1)
        m0 = jnp.zeros((B, 1, LANES), f32).at[:, 0, :ML_HEADS].set(state["ml_m"])
    h_ml, cn, m_out = _mlstm(z_ml, z_g, p["ab_bias"], p["ab_norm"], cn0, m0, B, T, L)
    h_ml = h_ml.reshape(n, ML_HEADS * ML_DV)
    cos_t, sin_t = _rope_tables(pos)
    if state is None:
        a_sw, k_rot = _swa_prompt(z_sw, cos_t, sin_t, p["ab_sinks"], B, T, L)
    else:
        a_sw, k_rot = _swa_sample(z_sw, state["sw_k"].reshape(B, WINDOW, LANES),
                                  state["sw_v"].reshape(B, WINDOW, LANES), cos_t, sin_t, p["ab_sinks"], B, T)
    keep = min(T, WINDOW)
    new_k = k_rot.reshape(B, T, SW_KV_HEADS, SW_HD)[:, T - keep:]
    new_v = z_sw[:, SW_HEADS * SW_HD + LANES:].reshape(B, T, SW_KV_HEADS, SW_HD)[:, T - keep:]
    x2 = _out_ln(x2, [h_ml, a_sw], [p["ab_wo_h"], p["ab_wo_a"]], p["ln_g"][0][0], p["ln_b"][0][0], tm)
    x2 = _moe_block(x2, p, 0)

    tm1 = _tile(n, 256)
    qkv, zg, ba = _proj(x2, p["c_w"], ((0, GD_CONV_CH), (GD_CONV_CH, GD_V_W), (GD_CONV_CH + GD_V_W, LANES)), tm1)
    if state is None:
        s0 = jnp.zeros((B, GD_V_HEADS, GD_HD, GD_HD), f32)
        conv8 = jnp.zeros((B, 8, GD_CONV_CH), f32)
    else:
        s0 = state["gd_S"]
        conv8 = jnp.concatenate([jnp.zeros((B, 8 - (GD_CONV - 1), GD_CONV_CH), f32), state["gd_conv"]], axis=1)
    o_gd, s_out = _gdn(qkv, zg, ba, p["c_conv_w"], p["c_alog"], p["c_dt"], p["c_norm"], s0, conv8, B, T, L)
    new_conv = qkv.reshape(B, T, GD_CONV_CH)[:, T - (GD_CONV - 1):]
    x2 = _out_ln(x2, [o_gd], [p["c_wo"]], p["ln_g"][1][0], p["ln_b"][1][0], tm)
    x2 = _moe_block(x2, p, 1)

    outs = (new_k[None], new_v[None], cn[None, ..., :ML_DV], cn[None, ..., ML_DV], m_out[None, :, 0, :ML_HEADS],
            s_out[None], new_conv[None])
    return x2.reshape(B, T, D_MODEL), outs


def _moe_block(x2, p, layer):
    n = x2.shape[0]
    gates_t, route = _router(x2, p["router_wt"], p["router_b"], _tile(n, 512))
    wg, wu, wd = p["ex_gate"][layer], p["ex_up"][layer], p["ex_down"][layer]
    g_row, b_row = p["ln_g"][layer][1], p["ln_b"][layer][1]
    if n < N_PAIRS * MOE_TM:
        return _moe_ln(x2, gates_t.T, wg, wu, wd, g_row, b_row, _tile(n, 1024))
    row_of_token, tile_a, tile_b, tile_valid = _pair_plan(route, n, MOE_TM)
    xs = _scatter_rows(x2, row_of_token, n + N_PAIRS * MOE_TM, MOE_TM)
    ys = _pair_experts(xs, p["router_wt"], tile_a, tile_b, tile_valid, wg, wu, wd, g_row, b_row, MOE_TM)
    return _gather_rows(ys, row_of_token, MOE_TM)


def kernel(x_prompt, x_sample, cache_swa_k, cache_swa_v, state_mlstm_C, state_mlstm_n, state_mlstm_m, state_gdn_S, state_gdn_conv, ab_w_in, ab_b_i, ab_b_f, ab_norm, ab_sinks, ab_w_out, c_w_in, c_conv_w, c_a_log, c_dt_bias, c_norm, c_w_out, ln_g, ln_b, router_w, router_b, ex_gate, ex_up, ex_down):
    gate_lo = ML_W
    sw_lo = ML_W + 2 * ML_HEADS
    w0 = ab_w_in[0]
    ab_w = jnp.concatenate([w0[:, :gate_lo], w0[:, sw_lo:], w0[:, gate_lo:sw_lo],
                            jnp.zeros((D_MODEL, LANES - 2 * ML_HEADS), f32)], axis=1).astype(bf16)
    w1 = c_w_in[0]
    c_w = jnp.concatenate([w1, jnp.zeros((D_MODEL, LANES - 2 * GD_V_HEADS), f32)], axis=1).astype(bf16)
    wo = ab_w_out[0].astype(bf16)
    p = {
        "ab_w": ab_w,
        "ab_bias": _pad_lanes(jnp.concatenate([ab_b_i[0], ab_b_f[0]])),
        "ab_norm": ab_norm[0].reshape(1, ML_HEADS * ML_DV),
        "ab_sinks": _pad_lanes(ab_sinks[0]),
        "ab_wo_h": wo[:ML_HEADS * ML_DV],
        "ab_wo_a": wo[ML_HEADS * ML_DV:],
        "c_w": c_w,
        "c_conv_w": c_conv_w[0],
        "c_alog": _pad_lanes(c_a_log[0], GD_V_HEADS),
        "c_dt": _pad_lanes(c_dt_bias[0], GD_V_HEADS),
        "c_norm": c_norm[0].reshape(1, GD_HD),
        "c_wo": c_w_out[0].astype(bf16),
        "ln_g": [[ln_g[i, j].reshape(1, D_MODEL) for j in range(2)] for i in range(DEPTH)],
        "ln_b": [[ln_b[i, j].reshape(1, D_MODEL) for j in range(2)] for i in range(DEPTH)],
        "router_wt": router_w.T,
        "router_b": router_b.reshape(N_EXPERTS, 1),
        "ex_gate": ex_gate.astype(bf16),
        "ex_up": ex_up.astype(bf16),
        "ex_down": ex_down.astype(bf16),
    }
    t_p = x_prompt.shape[1]
    y_p, st_p = _trunk(x_prompt, jnp.arange(t_p, dtype=jnp.int32), CHUNK, None, p)
    t_s = x_sample.shape[1]
    state = {"sw_k": cache_swa_k[0], "sw_v": cache_swa_v[0], "ml_C": state_mlstm_C[0], "ml_n": state_mlstm_n[0],
             "ml_m": state_mlstm_m[0], "gd_S": state_gdn_S[0], "gd_conv": state_gdn_conv[0]}
    y_s, st_s = _trunk(x_sample, PAST_LEN + jnp.arange(t_s, dtype=jnp.int32), t_s, state, p)
    return (y_p, y_s) + st_p + st_s
```

```python
import functools
import math

import jax
import jax.numpy as jnp
from jax import lax
from jax.experimental import pallas as pl
from jax.experimental.pallas import tpu as pltpu

f32 = jnp.float32
bf16 = jnp.bfloat16
HIGHEST = lax.Precision.HIGHEST

D_MODEL = 1024
DEPTH = 2
CHUNK = 64
PAST_LEN = 2048
ML_HEADS = 8
ML_DK = 64
ML_DV = 64
SW_HEADS = 8
SW_KV_HEADS = 2
SW_HD = 64
SW_GROUP = SW_HEADS // SW_KV_HEADS
WINDOW = 128
ROPE_THETA = 10000.0
GD_QK_HEADS = 8
GD_V_HEADS = 16
GD_HD = 128
GD_CONV = 4
GD_QK_W = GD_QK_HEADS * GD_HD
GD_V_W = GD_V_HEADS * GD_HD
GD_CONV_CH = 2 * GD_QK_W + GD_V_W
N_EXPERTS = 16
N_GROUPS = 4
EXP_PER_GROUP = 4
D_EXPERT = 512
DN_ALPHA = (2 * DEPTH) ** 0.25
LN_EPS = 1e-5
RMS_EPS = 1e-6

LANES = 128
ML_W = 4 * ML_HEADS * ML_DK
SW_W = SW_HEADS * SW_HD + 2 * SW_KV_HEADS * SW_HD
VMEM_LIMIT = 56 * 1024 * 1024

SDS = jax.ShapeDtypeStruct


def _cparams(*sem):
    return pltpu.CompilerParams(dimension_semantics=sem, vmem_limit_bytes=VMEM_LIMIT)


def _dot(a, b):
    return jnp.dot(a, b, preferred_element_type=f32, precision=HIGHEST)


def _dot_nt(a, b):
    return lax.dot_general(a, b, (((1,), (1,)), ((), ())), preferred_element_type=f32, precision=HIGHEST)


def _bdot(a, b):
    return jnp.dot(a.astype(bf16), b.astype(bf16), preferred_element_type=f32)


def _bdot_nt(a, b):
    return lax.dot_general(a.astype(bf16), b.astype(bf16), (((1,), (1,)), ((), ())), preferred_element_type=f32)


def _bdot_tn(a, b):
    return jnp.dot(a.T.astype(bf16), b.astype(bf16), preferred_element_type=f32)


def _dots(precise):
    if precise:
        return _dot, _dot_nt, lambda a, b: _dot(a.T, b)
    return _bdot, _bdot_nt, _bdot_tn


def _sigmoid(x):
    return 0.5 + 0.5 * jnp.tanh(0.5 * x)


def _silu(x):
    hx = 0.5 * x
    return hx + hx * jnp.tanh(hx)


def _softplus(x):
    return jnp.maximum(x, 0.0) + jnp.log(1.0 + jnp.exp(-jnp.abs(x)))


def _layer_norm(v, g, b):
    mu = jnp.mean(v, axis=-1, keepdims=True)
    d = v - mu
    var = jnp.mean(d * d, axis=-1, keepdims=True)
    return d * lax.rsqrt(var + LN_EPS) * g + b


def _proj_kernel(x_ref, w_ref, *o_refs, splits, col_chunk, precise):
    mm = _dots(precise)[0]
    xb = x_ref[...] if precise else x_ref[...].astype(bf16)
    for o_ref, (start, width) in zip(o_refs, splits):
        for c in range(0, width, col_chunk):
            cw = min(col_chunk, width - c)
            o_ref[:, c:c + cw] = mm(xb, w_ref[:, start + c:start + c + cw])


def _proj(x2, w, splits, tm):
    n, k = x2.shape
    return pl.pallas_call(
        functools.partial(_proj_kernel, splits=splits, col_chunk=512, precise=w.dtype == f32),
        grid=(n // tm,),
        in_specs=[pl.BlockSpec((tm, k), lambda i: (i, 0)),
                  pl.BlockSpec(w.shape, lambda i: (0, 0), pipeline_mode=pl.Buffered(1))],
        out_specs=[pl.BlockSpec((tm, wd), lambda i: (i, 0)) for _, wd in splits],
        out_shape=[SDS((n, wd), f32) for _, wd in splits],
        compiler_params=_cparams("parallel"),
        name="in_proj",
    )(x2, w)


def _mlstm_kernel(z_ref, g_ref, bias_ref, nw_ref, cn0_ref, m0_ref, h_ref, cn_ref, m_ref, *, L, BB, precise):
    @pl.when(pl.program_id(1) == 0)
    def _():
        cn_ref[...] = cn0_ref[...]
        m_ref[...] = m0_ref[...]

    mm, mm_nt, mm_tn = _dots(precise)
    row = lax.broadcasted_iota(jnp.int32, (L, L), 0)
    col = lax.broadcasted_iota(jnp.int32, (L, L), 1)
    causal = row >= col
    tri = causal.astype(f32)
    lane = lax.broadcasted_iota(jnp.int32, (1, LANES), 1)
    lane_l = lax.broadcasted_iota(jnp.int32, (L, ML_DV), 1)
    one_hot0 = (lane_l == 0).astype(f32)
    for bi in range(BB):
        g = g_ref[bi] + bias_ref[...]
        lf = jnp.minimum(g, 0.0) - jnp.log(1.0 + jnp.exp(-jnp.abs(g)))
        bcum = _dot(tri, lf)
        b_t = bcum.T
        g_t = g.T
        m_row = m_ref[bi]
        new_m = m_row
        outs = []
        for h in range(ML_HEADS):
            b_col = bcum[:, ML_HEADS + h:ML_HEADS + h + 1]
            b_row = b_t[ML_HEADS + h:ML_HEADS + h + 1, :]
            ig_row = g_t[h:h + 1, :]
            ig_col = g[:, h:h + 1]
            m_h = m_row[:, h:h + 1]
            dmat = jnp.where(causal, b_col - b_row + ig_row, -jnp.inf)
            inter = b_col + m_h
            mt = jnp.maximum(inter, jnp.max(dmat, axis=-1, keepdims=True))
            a = jnp.exp(inter - mt)
            q = z_ref[bi, :, h * ML_DK:(h + 1) * ML_DK]
            k = z_ref[bi, :, ML_HEADS * ML_DK + h * ML_DK:ML_HEADS * ML_DK + (h + 1) * ML_DK] * (ML_DK ** -0.5)
            v = z_ref[bi, :, 2 * ML_HEADS * ML_DK + h * ML_DV:2 * ML_HEADS * ML_DK + (h + 1) * ML_DV]
            og = z_ref[bi, :, 3 * ML_HEADS * ML_DK + h * ML_DV:3 * ML_HEADS * ML_DK + (h + 1) * ML_DV]
            s = mm_nt(q, k) * jnp.exp(dmat - mt)
            vext = jnp.concatenate([v, one_hot0], axis=-1)
            cn = cn_ref[bi, h]
            tot = a * mm(q, cn) + mm(s, vext)
            num = tot[:, :ML_DV]
            den = tot[:, ML_DV:ML_DV + 1]
            hh = num / jnp.maximum(jnp.abs(den), jnp.exp(-mt))
            hh = hh * lax.rsqrt(jnp.mean(hh * hh, axis=-1, keepdims=True) + RMS_EPS) * nw_ref[:, h * ML_DV:(h + 1) * ML_DV]
            outs.append(hh * _sigmoid(og))
            m_new = mt[L - 1:L, :]
            b_last = b_col[L - 1:L, :]
            wk = jnp.exp(b_last - b_col + ig_col - m_new)
            dec = jnp.exp(b_last + m_h - m_new)
            cn_ref[bi, h] = dec * cn + mm_tn(k, wk * vext)
            new_m = jnp.where(lane == h, m_new, new_m)
        m_ref[bi] = new_m
        h_ref[bi] = jnp.concatenate(outs, axis=-1)


def _mlstm(z_ml, z_g, bias_row, norm_row, cn0, m0, B, T, L, precise):
    nc = T // L
    bb = min(B, 4)
    tok = lambda b, c: (b, c, 0)
    st4 = lambda b, c: (b, 0, 0, 0)
    st3 = lambda b, c: (b, 0, 0)
    return pl.pallas_call(
        functools.partial(_mlstm_kernel, L=L, BB=bb, precise=precise),
        grid=(B // bb, nc),
        in_specs=[pl.BlockSpec((bb, L, ML_W), tok),
                  pl.BlockSpec((bb, L, LANES), tok),
                  pl.BlockSpec((1, LANES), lambda b, c: (0, 0)),
                  pl.BlockSpec((1, ML_HEADS * ML_DV), lambda b, c: (0, 0)),
                  pl.BlockSpec((bb, ML_HEADS, ML_DK, LANES), st4),
                  pl.BlockSpec((bb, 1, LANES), st3)],
        out_specs=[pl.BlockSpec((bb, L, ML_HEADS * ML_DV), tok),
                   pl.BlockSpec((bb, ML_HEADS, ML_DK, LANES), st4),
                   pl.BlockSpec((bb, 1, LANES), st3)],
        out_shape=[SDS((B, T, ML_HEADS * ML_DV), f32),
                   SDS((B, ML_HEADS, ML_DK, LANES), f32),
                   SDS((B, 1, LANES), f32)],
        compiler_params=_cparams("parallel", "arbitrary"),
        name="mlstm",
    )(z_ml.reshape(B, T, ML_W), z_g.reshape(B, T, LANES), bias_row, norm_row, cn0, m0)


def _rope(x, cos, sin_signed):
    w = x.shape[-1]
    lane = lax.broadcasted_iota(jnp.int32, x.shape, 1)
    swapped = jnp.where((lane % SW_HD) < SW_HD // 2, pltpu.roll(x, w - SW_HD // 2, 1), pltpu.roll(x, SW_HD // 2, 1))
    return x * cos + swapped * sin_signed


def _swa_attend(jobs, sinks_ref, L, precise=False):
    mm, mm_nt, _ = _dots(precise)
    units = [(j, g) for j in range(len(jobs)) for g in range(SW_KV_HEADS)]
    sinks = [jnp.concatenate(
        [jnp.broadcast_to(sinks_ref[:, g * SW_GROUP + i:g * SW_GROUP + i + 1], (L, 1)) for i in range(SW_GROUP)],
        axis=0) for g in range(SW_KV_HEADS)]
    s, p = {}, {}
    for j, g in units:
        qr, keys, _, _ = jobs[j]
        q4 = jnp.concatenate([qr[:, (g * SW_GROUP + i) * SW_HD:(g * SW_GROUP + i + 1) * SW_HD]
                              for i in range(SW_GROUP)], axis=0)
        s[j, g] = mm_nt(q4, keys[:, g * SW_HD:(g + 1) * SW_HD]) * (SW_HD ** -0.5)
    for j, g in units:
        first_valid = jobs[j][3]
        sc = s[j, g]
        if first_valid is not None:
            kcol = lax.broadcasted_iota(jnp.int32, (1, sc.shape[1]), 1)
            sc = jnp.where(kcol >= first_valid, sc, -jnp.inf)
        mx = jnp.maximum(jnp.max(sc, axis=-1, keepdims=True), sinks[g])
        e = jnp.exp(sc - mx)
        p[j, g] = e / (jnp.sum(e, axis=-1, keepdims=True) + jnp.exp(sinks[g] - mx))
    o = {u: mm(p[u], jobs[u[0]][2][:, u[1] * SW_HD:(u[1] + 1) * SW_HD]) for u in units}
    return [jnp.concatenate([o[j, g][i * L:(i + 1) * L, :] for g in range(SW_KV_HEADS) for i in range(SW_GROUP)],
                            axis=-1) for j in range(len(jobs))]


def _swa_prompt_kernel(q_ref, kp_ref, kc_ref, vp_ref, vc_ref, cp_ref, cc_ref, sp_ref, sc_ref, sinks_ref,
                       o_ref, kr_ref, *, L, CB):
    i = pl.program_id(1)
    rows = CB * L
    back = 2 * L
    cos_q = jnp.concatenate([cc_ref[...]] * (SW_HEADS // SW_KV_HEADS), axis=-1)
    sin_q = jnp.concatenate([sc_ref[...]] * (SW_HEADS // SW_KV_HEADS), axis=-1)
    qr = _rope(q_ref[...], cos_q, sin_q)
    k_cur = _rope(kc_ref[...], cc_ref[...], sc_ref[...])
    kr_ref[...] = k_cur
    k_prev = _rope(kp_ref[rows - back:rows, :], cp_ref[rows - back:rows, :], sp_ref[rows - back:rows, :])
    keys = jnp.concatenate([k_prev, k_cur], axis=0)
    vals = jnp.concatenate([vp_ref[rows - back:rows, :], vc_ref[...]], axis=0)
    jobs = []
    for u in range(CB):
        first_valid = jnp.where(i == 0, back - u * L, 0) if u * L < back else None
        jobs.append((qr[u * L:(u + 1) * L], keys[u * L:(u + 3) * L], vals[u * L:(u + 3) * L], first_valid))
    o_ref[...] = jnp.concatenate(_swa_attend(jobs, sinks_ref, L), axis=0)


def _swa_prompt(z_sw, cos_t, sin_t, sinks_row, B, T, L):
    cb = 4
    rows = cb * L
    nb = T // rows
    n = B * T
    kcol = SW_HEADS * SW_HD // LANES
    vcol = kcol + 1
    cur = lambda b, i: (b * nb + i, 0)
    prev = lambda col: (lambda b, i: (b * nb + jnp.maximum(i - 1, 0), col))
    curc = lambda col: (lambda b, i: (b * nb + i, col))
    tab_cur = lambda b, i: (i, 0)
    tab_prev = lambda b, i: (jnp.maximum(i - 1, 0), 0)
    return pl.pallas_call(
        functools.partial(_swa_prompt_kernel, L=L, CB=cb),
        grid=(B, nb),
        in_specs=[pl.BlockSpec((rows, SW_HEADS * SW_HD), cur),
                  pl.BlockSpec((rows, LANES), prev(kcol)), pl.BlockSpec((rows, LANES), curc(kcol)),
                  pl.BlockSpec((rows, LANES), prev(vcol)), pl.BlockSpec((rows, LANES), curc(vcol)),
                  pl.BlockSpec((rows, LANES), tab_prev), pl.BlockSpec((rows, LANES), tab_cur),
                  pl.BlockSpec((rows, LANES), tab_prev), pl.BlockSpec((rows, LANES), tab_cur),
                  pl.BlockSpec((1, LANES), lambda b, i: (0, 0))],
        out_specs=[pl.BlockSpec((rows, SW_HEADS * SW_HD), cur),
                   pl.BlockSpec((rows, LANES), cur)],
        out_shape=[SDS((n, SW_HEADS * SW_HD), f32), SDS((n, LANES), f32)],
        compiler_params=_cparams("parallel", "parallel"),
        name="swa_prompt",
    )(z_sw, z_sw, z_sw, z_sw, z_sw, cos_t, cos_t, sin_t, sin_t, sinks_row)


def _swa_sample_kernel(q_ref, k_ref, v_ref, ck_ref, cv_ref, cos_ref, sin_ref, sinks_ref, o_ref, kr_ref, *, L):
    cos_q = jnp.concatenate([cos_ref[...]] * (SW_HEADS // SW_KV_HEADS), axis=-1)
    sin_q = jnp.concatenate([sin_ref[...]] * (SW_HEADS // SW_KV_HEADS), axis=-1)
    qr = _rope(q_ref[...], cos_q, sin_q)
    kr = _rope(k_ref[...], cos_ref[...], sin_ref[...])
    kr_ref[...] = kr
    keys = jnp.concatenate([ck_ref[0], kr], axis=0)
    vals = jnp.concatenate([cv_ref[0], v_ref[...]], axis=0)
    o_ref[...] = _swa_attend([(qr, keys, vals, None)], sinks_ref, L, precise=True)[0]


def _swa_sample(z_sw, cache_k, cache_v, cos_t, sin_t, sinks_row, B, T):
    n = B * T
    kcol = SW_HEADS * SW_HD // LANES
    return pl.pallas_call(
        functools.partial(_swa_sample_kernel, L=T),
        grid=(B,),
        in_specs=[pl.BlockSpec((T, SW_HEADS * SW_HD), lambda b: (b, 0)),
                  pl.BlockSpec((T, LANES), lambda b: (b, kcol)),
                  pl.BlockSpec((T, LANES), lambda b: (b, kcol + 1)),
                  pl.BlockSpec((1, WINDOW, LANES), lambda b: (b, 0, 0)),
                  pl.BlockSpec((1, WINDOW, LANES), lambda b: (b, 0, 0)),
                  pl.BlockSpec((T, LANES), lambda b: (0, 0)),
                  pl.BlockSpec((T, LANES), lambda b: (0, 0)),
                  pl.BlockSpec((1, LANES), lambda b: (0, 0))],
        out_specs=[pl.BlockSpec((T, SW_HEADS * SW_HD), lambda b: (b, 0)),
                   pl.BlockSpec((T, LANES), lambda b: (b, 0))],
        out_shape=[SDS((n, SW_HEADS * SW_HD), f32), SDS((n, LANES), f32)],
        compiler_params=_cparams("parallel"),
        name="swa_sample",
    )(z_sw, z_sw, z_sw, cache_k, cache_v, cos_t, sin_t, sinks_row)


def _out_ln_kernel(*refs, n_in):
    x_ref = refs[0]
    a_refs = refs[1:1 + n_in]
    w_refs = refs[1 + n_in:1 + 2 * n_in]
    g_ref, b_ref, rw_ref, rb_ref, o_ref, gates_ref, route_ref = refs[1 + 2 * n_in:]
    mm = _dots(w_refs[0].dtype == f32)[0]
    y = mm(a_refs[0][...], w_refs[0][...])
    for a_ref, w_ref in zip(a_refs[1:], w_refs[1:]):
        y = y + mm(a_ref[...], w_ref[...])
    x_new = _layer_norm(DN_ALPHA * x_ref[...] + y, g_ref[...], b_ref[...])
    o_ref[...] = x_new
    gates_ref[...], route_ref[...] = _route(x_new, rw_ref, rb_ref)


def _out_ln(x2, acts, ws, g_row, b_row, rw_t, rb_col, tm):
    n = x2.shape[0]
    row = lambda i: (i, 0)
    const = lambda i: (0, 0)
    col = lambda i: (0, i)
    return pl.pallas_call(
        functools.partial(_out_ln_kernel, n_in=len(acts)),
        grid=(n // tm,),
        in_specs=[pl.BlockSpec((tm, D_MODEL), row)]
        + [pl.BlockSpec((tm, a.shape[1]), row) for a in acts]
        + [pl.BlockSpec(w.shape, const) for w in ws]
        + [pl.BlockSpec((1, D_MODEL), const), pl.BlockSpec((1, D_MODEL), const),
           pl.BlockSpec((N_EXPERTS, D_MODEL), const), pl.BlockSpec((N_EXPERTS, 1), const)],
        out_specs=[pl.BlockSpec((tm, D_MODEL), row), pl.BlockSpec((N_EXPERTS, tm), col), pl.BlockSpec((8, tm), col)],
        out_shape=[SDS((n, D_MODEL), f32), SDS((N_EXPERTS, n), f32), SDS((8, n), f32)],
        compiler_params=_cparams("parallel"),
        name="out_proj_ln",
    )(x2, *acts, *ws, g_row, b_row, rw_t, rb_col)


def _logistic(x):
    return 1.0 / (1.0 + jnp.exp(-x))


def _route(x, rw_ref, rb_ref):
    logits = _dot_nt(rw_ref[...], x)
    aff = _logistic(logits)
    sc = aff + rb_ref[...]
    s = [sc[e:e + 1, :] for e in range(N_EXPERTS)]
    a = [aff[e:e + 1, :] for e in range(N_EXPERTS)]
    scores = []
    for gi in range(N_GROUPS):
        w, x, y, z = s[4 * gi:4 * gi + 4]
        p, q = jnp.maximum(w, x), jnp.minimum(w, x)
        r, t = jnp.maximum(y, z), jnp.minimum(y, z)
        scores.append(jnp.maximum(p, r) + jnp.maximum(jnp.minimum(p, r), jnp.maximum(q, t)))
    best = scores[0]
    gsel = jnp.zeros_like(best, dtype=jnp.int32)
    for gi in range(1, N_GROUPS):
        better = scores[gi] > best
        best = jnp.where(better, scores[gi], best)
        gsel = jnp.where(better, gi, gsel)
    sel = []
    for e in range(N_EXPERTS):
        gi, i = divmod(e, EXP_PER_GROUP)
        beaten = jnp.zeros_like(gsel)
        for j in range(EXP_PER_GROUP):
            if j == i:
                continue
            o = s[4 * gi + j]
            wins = (o >= s[e]) if j < i else (o > s[e])
            beaten = beaten + wins.astype(jnp.int32)
        sel.append((gsel == gi) & (beaten < 2))
    den = jnp.zeros_like(best)
    for e in range(N_EXPERTS):
        den = den + jnp.where(sel[e], a[e], 0.0)
    gate = [jnp.where(sel[e], a[e] / den, 0.0) for e in range(N_EXPERTS)]
    taken = jnp.zeros_like(gsel)
    ea = eb = wa = wb = jnp.zeros_like(best)
    for e in range(N_EXPERTS):
        first = sel[e] & (taken == 0)
        second = sel[e] & (taken == 1)
        ea = jnp.where(first, float(e), ea)
        wa = jnp.where(first, gate[e], wa)
        eb = jnp.where(second, float(e), eb)
        wb = jnp.where(second, gate[e], wb)
        taken = taken + sel[e].astype(jnp.int32)
    route = jnp.concatenate([ea, eb, wa, wb, jnp.zeros((4, ea.shape[1]), f32)], axis=0)
    return jnp.concatenate(gate, axis=0), route


def _moe_kernel(x_ref, gates_ref, wg_ref, wu_ref, wd_ref, g_ref, b_ref, o_ref, xb_ref, acc_ref):
    e = pl.program_id(1)

    @pl.when(e == 0)
    def _():
        xb_ref[...] = x_ref[...].astype(bf16)
        acc_ref[...] = jnp.zeros_like(acc_ref)

    xb = xb_ref[...]
    lane = lax.broadcasted_iota(jnp.int32, gates_ref.shape, 1)
    gcol = jnp.sum(jnp.where(lane == e, gates_ref[...], 0.0), axis=-1, keepdims=True)
    h = _silu(jnp.dot(xb, wg_ref[0], preferred_element_type=f32)) * jnp.dot(xb, wu_ref[0], preferred_element_type=f32)
    acc_ref[...] += jnp.dot((gcol * h).astype(bf16), wd_ref[0], preferred_element_type=f32)

    @pl.when(e == N_EXPERTS - 1)
    def _():
        o_ref[...] = _layer_norm(DN_ALPHA * x_ref[...] + acc_ref[...], g_ref[...], b_ref[...])


def _moe_ln(x2, gates, wg, wu, wd, g_row, b_row, tm):
    n = x2.shape[0]
    row = lambda i, e: (i, 0)
    const = lambda i, e: (0, 0)
    return pl.pallas_call(
        _moe_kernel,
        grid=(n // tm, N_EXPERTS),
        in_specs=[pl.BlockSpec((tm, D_MODEL), row),
                  pl.BlockSpec((tm, N_EXPERTS), row),
                  pl.BlockSpec((1, D_MODEL, D_EXPERT), lambda i, e: (e, 0, 0)),
                  pl.BlockSpec((1, D_MODEL, D_EXPERT), lambda i, e: (e, 0, 0)),
                  pl.BlockSpec((1, D_EXPERT, D_MODEL), lambda i, e: (e, 0, 0)),
                  pl.BlockSpec((1, D_MODEL), const), pl.BlockSpec((1, D_MODEL), const)],
        out_specs=pl.BlockSpec((tm, D_MODEL), row),
        out_shape=SDS((n, D_MODEL), f32),
        scratch_shapes=[pltpu.VMEM((tm, D_MODEL), bf16), pltpu.VMEM((tm, D_MODEL), f32)],
        compiler_params=_cparams("parallel", "arbitrary"),
        name="moe_ln",
    )(x2, gates, wg, wu, wd, g_row, b_row)


N_PAIRS = N_GROUPS * (EXP_PER_GROUP * (EXP_PER_GROUP - 1) // 2)
MOE_TM = 256
_PAIR_A = [g * EXP_PER_GROUP + a for g in range(N_GROUPS) for a in range(EXP_PER_GROUP) for b in range(a + 1, EXP_PER_GROUP)]
_PAIR_B = [g * EXP_PER_GROUP + b for g in range(N_GROUPS) for a in range(EXP_PER_GROUP) for b in range(a + 1, EXP_PER_GROUP)]


def _gather_rows_kernel(idx_ref, src_ref, o_ref, sem, *, rows):
    base = pl.program_id(0) * rows

    def row_copy(j):
        return pltpu.make_async_copy(src_ref.at[pl.ds(idx_ref[base + j], 1)], o_ref.at[pl.ds(j, 1)], sem)

    def issue(j, carry):
        row_copy(j).start()
        return carry

    def drain(j, carry):
        row_copy(j).wait()
        return carry

    lax.fori_loop(0, rows, issue, 0, unroll=8)
    lax.fori_loop(0, rows, drain, 0, unroll=8)


def _gather_rows(src, idx, rows):
    n_out = idx.shape[0]
    d = src.shape[1]
    return pl.pallas_call(
        functools.partial(_gather_rows_kernel, rows=rows),
        grid_spec=pltpu.PrefetchScalarGridSpec(
            num_scalar_prefetch=1,
            grid=(n_out // rows,),
            in_specs=[pl.BlockSpec(memory_space=pl.ANY)],
            out_specs=pl.BlockSpec((rows, d), lambda i, idx_ref: (i, 0)),
            scratch_shapes=[pltpu.SemaphoreType.DMA(())]),
        out_shape=SDS((n_out, d), f32),
        compiler_params=_cparams("arbitrary"),
        name="gather_rows",
    )(idx, src)


def _scatter_rows_kernel(idx_ref, src_ref, init_ref, o_ref, sem, *, rows):
    del init_ref
    base = pl.program_id(0) * rows

    def row_copy(j):
        return pltpu.make_async_copy(src_ref.at[pl.ds(j, 1)], o_ref.at[pl.ds(idx_ref[base + j], 1)], sem)

    def issue(j, carry):
        row_copy(j).start()
        return carry

    def drain(j, carry):
        row_copy(j).wait()
        return carry

    lax.fori_loop(0, rows, issue, 0, unroll=8)
    lax.fori_loop(0, rows, drain, 0, unroll=8)


def _scatter_rows(src, idx, n_out, rows):
    n_src, d = src.shape
    return pl.pallas_call(
        functools.partial(_scatter_rows_kernel, rows=rows),
        grid_spec=pltpu.PrefetchScalarGridSpec(
            num_scalar_prefetch=1,
            grid=(n_src // rows,),
            in_specs=[pl.BlockSpec((rows, d), lambda i, idx_ref: (i, 0)), pl.BlockSpec(memory_space=pl.ANY)],
            out_specs=pl.BlockSpec(memory_space=pl.ANY),
            scratch_shapes=[pltpu.SemaphoreType.DMA(())]),
        out_shape=SDS((n_out, d), f32),
        input_output_aliases={2: 0},
        compiler_params=_cparams("arbitrary"),
        name="scatter_rows",
    )(idx, src, jnp.zeros((n_out, d), f32))


def _pair_plan(route, n, tm):
    ea = route[0].astype(jnp.int32)
    eb = route[1].astype(jnp.int32)
    a = ea % EXP_PER_GROUP
    b = eb % EXP_PER_GROUP
    pidx = jnp.where(a == 0, b - 1, jnp.where(a == 1, b + 1, 5))
    pair = (ea // EXP_PER_GROUP) * (N_PAIRS // N_GROUPS) + pidx
    onehot = (pair[:, None] == jnp.arange(N_PAIRS, dtype=jnp.int32)[None, :]).astype(jnp.int32)
    csum = jnp.cumsum(onehot, axis=0)
    counts = csum[-1]
    ntiles = (counts + tm - 1) // tm
    tile_end = jnp.cumsum(ntiles)
    tile_start = tile_end - ntiles
    row_of_token = jnp.sum(onehot * (csum - 1 + (tile_start * tm)[None, :]), axis=1)
    nt = n // tm + N_PAIRS
    tile_id = jnp.arange(nt, dtype=jnp.int32)
    tile_valid = tile_id < tile_end[-1]
    tile_pair = jnp.sum((tile_end[None, :] <= jnp.minimum(tile_id, tile_end[-1] - 1)[:, None]).astype(jnp.int32), axis=1)
    tile_pair = jnp.minimum(tile_pair, N_PAIRS - 1)
    pick = (tile_pair[:, None] == jnp.arange(N_PAIRS, dtype=jnp.int32)[None, :]).astype(jnp.int32)
    tile_a = jnp.sum(pick * jnp.asarray(_PAIR_A, jnp.int32)[None, :], axis=1)
    tile_b = jnp.sum(pick * jnp.asarray(_PAIR_B, jnp.int32)[None, :], axis=1)
    return row_of_token, tile_a, tile_b, tile_valid.astype(jnp.int32)


def _pair_expert_kernel(ta_ref, tb_ref, tv_ref, x_ref, rwt_ref, wga_ref, wua_ref, wda_ref, wgb_ref, wub_ref, wdb_ref,
                        g_ref, b_ref, o_ref):
    i = pl.program_id(0)
    valid = tv_ref[i] == 1

    @pl.when(valid)
    def _():
        x = x_ref[...]
        xb = x.astype(bf16)
        aff_a = _logistic(jnp.sum(x * rwt_ref[pl.ds(ta_ref[i], 1), :], axis=-1, keepdims=True))
        aff_b = _logistic(jnp.sum(x * rwt_ref[pl.ds(tb_ref[i], 1), :], axis=-1, keepdims=True))
        den = aff_a + aff_b
        acc = None
        for w, (wg, wu, wd) in ((aff_a / den, (wga_ref, wua_ref, wda_ref)), (aff_b / den, (wgb_ref, wub_ref, wdb_ref))):
            h = _silu(jnp.dot(xb, wg[0], preferred_element_type=f32)) * jnp.dot(xb, wu[0], preferred_element_type=f32)
            y = jnp.dot((w * h).astype(bf16), wd[0], preferred_element_type=f32)
            acc = y if acc is None else acc + y
        o_ref[...] = _layer_norm(DN_ALPHA * x + acc, g_ref[...], b_ref[...])

    @pl.when(jnp.logical_not(valid))
    def _():
        o_ref[...] = jnp.zeros_like(o_ref)


def _pair_experts(xs, rw_t, tile_a, tile_b, tile_valid, wg, wu, wd, g_row, b_row, tm):
    rows = xs.shape[0]
    row = lambda i, ta, tb, tv: (i, 0)
    const = lambda i, ta, tb, tv: (0, 0)
    ex_a = lambda i, ta, tb, tv: (ta[i], 0, 0)
    ex_b = lambda i, ta, tb, tv: (tb[i], 0, 0)
    up = pl.BlockSpec((1, D_MODEL, D_EXPERT), ex_a), pl.BlockSpec((1, D_MODEL, D_EXPERT), ex_b)
    down = pl.BlockSpec((1, D_EXPERT, D_MODEL), ex_a), pl.BlockSpec((1, D_EXPERT, D_MODEL), ex_b)
    return pl.pallas_call(
        _pair_expert_kernel,
        grid_spec=pltpu.PrefetchScalarGridSpec(
            num_scalar_prefetch=3,
            grid=(rows // tm,),
            in_specs=[pl.BlockSpec((tm, D_MODEL), row), pl.BlockSpec((N_EXPERTS, D_MODEL), const),
                      up[0], up[0], down[0], up[1], up[1], down[1],
                      pl.BlockSpec((1, D_MODEL), const), pl.BlockSpec((1, D_MODEL), const)],
            out_specs=pl.BlockSpec((tm, D_MODEL), row)),
        out_shape=SDS((rows, D_MODEL), f32),
        compiler_params=_cparams("arbitrary"),
        name="pair_experts",
    )(tile_a, tile_b, tile_valid, xs, rw_t, wg, wu, wd, wg, wu, wd, g_row, b_row)


def _unit_lower_inverses(mats, L):
    row = lax.broadcasted_iota(jnp.int32, (L, L), 0)
    col = lax.broadcasted_iota(jnp.int32, (L, L), 1)
    eye = (row == col).astype(f32)
    ps = [eye - a for a in mats]
    pws = [a.astype(bf16) for a in mats]
    span = 2
    while span < L:
        pws = [jnp.dot(pw, pw, preferred_element_type=f32).astype(bf16) for pw in pws]
        ps = [p + jnp.dot(pw, p.astype(bf16), preferred_element_type=f32) for p, pw in zip(ps, pws)]
        span *= 2
    return ps


def _gdn_kernel(x_ref, zg_ref, ba_ref, cw_ref, alog_ref, dt_ref, nw_ref, s0_ref, cb_ref,
                o_ref, s_ref, prev_ref, *, L):
    @pl.when(pl.program_id(1) == 0)
    def _():
        s_ref[...] = s0_ref[...]
        prev_ref[...] = cb_ref[0]

    def conv_silu(lo, width):
        cur = x_ref[:, lo:lo + width]
        cat = jnp.concatenate([prev_ref[:, lo:lo + width], cur], axis=0)
        acc = cat[5:5 + L] * cw_ref[0:1, lo:lo + width]
        acc = acc + cat[6:6 + L] * cw_ref[1:2, lo:lo + width]
        acc = acc + cat[7:7 + L] * cw_ref[2:3, lo:lo + width]
        acc = acc + cur * cw_ref[3:4, lo:lo + width]
        return _silu(acc)

    def l2n(v):
        return v * lax.rsqrt(jnp.sum(v * v, axis=-1, keepdims=True) + 1e-6)

    ba = ba_ref[...]
    beta = _sigmoid(ba)
    gl = -jnp.exp(alog_ref[...]) * _softplus(ba + dt_ref[...])
    row = lax.broadcasted_iota(jnp.int32, (L, L), 0)
    col = lax.broadcasted_iota(jnp.int32, (L, L), 1)
    incl = row >= col
    strict = row > col
    gcum = _dot(incl.astype(f32), gl)
    gcum_t = gcum.T
    rep = GD_V_HEADS // GD_QK_HEADS
    qs, ks, amats, qkds, egs, g_cols, rhss = [], [], [], [], [], [], []
    for j in range(GD_QK_HEADS):
        q = l2n(conv_silu(j * GD_HD, GD_HD)) * (GD_HD ** -0.5)
        k = l2n(conv_silu(GD_QK_W + j * GD_HD, GD_HD))
        qb, kb = q.astype(bf16), k.astype(bf16)
        kk = lax.dot_general(kb, kb, (((1,), (1,)), ((), ())), preferred_element_type=f32)
        qk = lax.dot_general(qb, kb, (((1,), (1,)), ((), ())), preferred_element_type=f32)
        qs.append(qb)
        ks.append(k)
        for r in range(rep):
            hv = j * rep + r
            v = conv_silu(2 * GD_QK_W + hv * GD_HD, GD_HD)
            g_col = gcum[:, GD_V_HEADS + hv:GD_V_HEADS + hv + 1]
            g_row = gcum_t[GD_V_HEADS + hv:GD_V_HEADS + hv + 1, :]
            b_col = beta[:, hv:hv + 1]
            decay = jnp.exp(jnp.where(incl, g_col - g_row, -jnp.inf))
            eg = jnp.exp(g_col)
            amats.append(jnp.where(strict, b_col * kk * decay, 0.0))
            qkds.append((qk * decay).astype(bf16))
            egs.append(eg)
            g_cols.append(g_col)
            rhss.append(jnp.concatenate([b_col * v, (b_col * eg) * k], axis=-1))
    tinvs = _unit_lower_inverses(amats, L)
    sols = [_bdot(t, r) for t, r in zip(tinvs, rhss)]
    for hv in range(GD_V_HEADS):
        j = hv // rep
        st = s_ref[0, hv]
        stb = st.astype(bf16)
        sol = sols[hv]
        wn = sol[:, :GD_HD] - jnp.dot(sol[:, GD_HD:].astype(bf16), stb, preferred_element_type=f32)
        wnb = wn.astype(bf16)
        o = egs[hv] * jnp.dot(qs[j], stb, preferred_element_type=f32) + jnp.dot(qkds[hv], wnb, preferred_element_type=f32)
        g_col = g_cols[hv]
        g_last = g_col[L - 1:L, :]
        kt = (jnp.exp(g_last - g_col) * ks[j]).T.astype(bf16)
        s_ref[0, hv] = jnp.exp(g_last) * st + jnp.dot(kt, wnb, preferred_element_type=f32)
        o = o * lax.rsqrt(jnp.mean(o * o, axis=-1, keepdims=True) + RMS_EPS) * nw_ref[...]
        o_ref[:, hv * GD_HD:(hv + 1) * GD_HD] = o * _silu(zg_ref[:, hv * GD_HD:(hv + 1) * GD_HD])
    prev_ref[...] = x_ref[L - 8:L, :]


def _gdn(qkv, zg, ba, conv_w, alog_row, dt_row, norm_row, s0, conv8, B, T, L):
    nc = T // L
    n = B * T
    tok = lambda b, c: (b * nc + c, 0)
    const = lambda b, c: (0, 0)
    return pl.pallas_call(
        functools.partial(_gdn_kernel, L=L),
        grid=(B, nc),
        in_specs=[pl.BlockSpec((L, GD_CONV_CH), tok),
                  pl.BlockSpec((L, GD_V_W), tok),
                  pl.BlockSpec((L, LANES), tok),
                  pl.BlockSpec((GD_CONV, GD_CONV_CH), const),
                  pl.BlockSpec((1, LANES), const),
                  pl.BlockSpec((1, LANES), const),
                  pl.BlockSpec((1, GD_HD), const),
                  pl.BlockSpec((1, GD_V_HEADS, GD_HD, GD_HD), lambda b, c: (b, 0, 0, 0)),
                  pl.BlockSpec((1, 8, GD_CONV_CH), lambda b, c: (b, 0, 0))],
        out_specs=[pl.BlockSpec((L, GD_V_W), tok),
                   pl.BlockSpec((1, GD_V_HEADS, GD_HD, GD_HD), lambda b, c: (b, 0, 0, 0))],
        out_shape=[SDS((n, GD_V_W), f32), SDS((B, GD_V_HEADS, GD_HD, GD_HD), f32)],
        scratch_shapes=[pltpu.VMEM((8, GD_CONV_CH), f32)],
        compiler_params=_cparams("parallel", "arbitrary"),
        name="gdn",
    )(qkv, zg, ba, conv_w, alog_row, dt_row, norm_row, s0, conv8)


def _pad_lanes(row, offset=0):
    return jnp.zeros((1, LANES), f32).at[0, offset:offset + row.shape[0]].set(row.astype(f32))


def _rope_tables(pos):
    half = SW_HD // 2
    inv = ROPE_THETA ** (-jnp.arange(half, dtype=f32) / half)
    ang = pos.astype(f32)[:, None] * inv[None, :]
    cos, sin = jnp.cos(ang), jnp.sin(ang)
    cos_t = jnp.concatenate([cos, cos] * SW_KV_HEADS, axis=-1)
    sin_t = jnp.concatenate([-sin, sin] * SW_KV_HEADS, axis=-1)
    return cos_t, sin_t


def _tile(n, pref):
    return pref if n % pref == 0 else n


def _trunk(x, pos, L, state, p):
    B, T, _ = x.shape
    n = B * T
    x2 = x.reshape(n, D_MODEL)
    tm = _tile(n, 512)
    precise = state is not None
    ab_w, wo_h, wo_a = (p["ab_w32"], p["ab_wo_h32"], p["ab_wo_a32"]) if precise else (p["ab_w"], p["ab_wo_h"], p["ab_wo_a"])

    z_ml, z_sw, z_g = _proj(x2, ab_w, ((0, ML_W), (ML_W, SW_W), (ML_W + SW_W, LANES)), tm)
    if state is None:
        cn0 = jnp.zeros((B, ML_HEADS, ML_DK, LANES), f32)
        m0 = jnp.zeros((B, 1, LANES), f32)
    else:
        cn0 = jnp.concatenate([state["ml_C"], state["ml_n"][..., None],
                               jnp.zeros((B, ML_HEADS, ML_DK, LANES - ML_DV - 1), f32)], axis=-1)
        m0 = jnp.zeros((B, 1, LANES), f32).at[:, 0, :ML_HEADS].set(state["ml_m"])
    h_ml, cn, m_out = _mlstm(z_ml, z_g, p["ab_bias"], p["ab_norm"], cn0, m0, B, T, L, precise)
    h_ml = h_ml.reshape(n, ML_HEADS * ML_DV)
    cos_t, sin_t = _rope_tables(pos)
    if state is None:
        a_sw, k_rot = _swa_prompt(z_sw, cos_t, sin_t, p["ab_sinks"], B, T, L)
    else:
        a_sw, k_rot = _swa_sample(z_sw, state["sw_k"].reshape(B, WINDOW, LANES),
                                  state["sw_v"].reshape(B, WINDOW, LANES), cos_t, sin_t, p["ab_sinks"], B, T)
    keep = min(T, WINDOW)
    new_k = k_rot.reshape(B, T, LANES)[:, T - keep:].reshape(B, keep, SW_KV_HEADS, SW_HD)
    new_v = z_sw.reshape(B, T, SW_W)[:, T - keep:, SW_HEADS * SW_HD + LANES:].reshape(B, keep, SW_KV_HEADS, SW_HD)
    x2, gates_t, route = _out_ln(x2, [h_ml, a_sw], [wo_h, wo_a], p["ln_g"][0][0], p["ln_b"][0][0],
                                 p["router_wt"], p["router_b"], tm)
    x2 = _moe_block(x2, gates_t, route, p, 0)

    tm1 = _tile(n, 256)
    qkv, zg, ba = _proj(x2, p["c_w"], ((0, GD_CONV_CH), (GD_CONV_CH, GD_V_W), (GD_CONV_CH + GD_V_W, LANES)), tm1)
    if state is None:
        s0 = jnp.zeros((B, GD_V_HEADS, GD_HD, GD_HD), f32)
        conv8 = jnp.zeros((B, 8, GD_CONV_CH), f32)
    else:
        s0 = state["gd_S"]
        conv8 = jnp.concatenate([jnp.zeros((B, 8 - (GD_CONV - 1), GD_CONV_CH), f32), state["gd_conv"]], axis=1)
    o_gd, s_out = _gdn(qkv, zg, ba, p["c_conv_w"], p["c_alog"], p["c_dt"], p["c_norm"], s0, conv8, B, T, L)
    new_conv = qkv.reshape(B, T, GD_CONV_CH)[:, T - (GD_CONV - 1):]
    x2, gates_t, route = _out_ln(x2, [o_gd], [p["c_wo"]], p["ln_g"][1][0], p["ln_b"][1][0],
                                 p["router_wt"], p["router_b"], tm)
    x2 = _moe_block(x2, gates_t, route, p, 1)

    outs = (new_k[None], new_v[None], cn[None, ..., :ML_DV], cn[None, ..., ML_DV], m_out[None, :, 0, :ML_HEADS],
            s_out[None], new_conv[None])
    return x2.reshape(B, T, D_MODEL), outs


def _moe_block(x2, gates_t, route, p, layer):
    n = x2.shape[0]
    wg, wu, wd = p["ex_gate"][layer], p["ex_up"][layer], p["ex_down"][layer]
    g_row, b_row = p["ln_g"][layer][1], p["ln_b"][layer][1]
    if n < N_PAIRS * MOE_TM:
        return _moe_ln(x2, gates_t.T, wg, wu, wd, g_row, b_row, _tile(n, 1024))
    row_of_token, tile_a, tile_b, tile_valid = _pair_plan(route, n, MOE_TM)
    xs = _scatter_rows(x2, row_of_token, n + N_PAIRS * MOE_TM, MOE_TM)
    ys = _pair_experts(xs, p["router_wt"], tile_a, tile_b, tile_valid, wg, wu, wd, g_row, b_row, MOE_TM)
    return _gather_rows(ys, row_of_token, MOE_TM)


def kernel(x_prompt, x_sample, cache_swa_k, cache_swa_v, state_mlstm_C, state_mlstm_n, state_mlstm_m, state_gdn_S, state_gdn_conv, ab_w_in, ab_b_i, ab_b_f, ab_norm, ab_sinks, ab_w_out, c_w_in, c_conv_w, c_a_log, c_dt_bias, c_norm, c_w_out, ln_g, ln_b, router_w, router_b, ex_gate, ex_up, ex_down):
    gate_lo = ML_W
    sw_lo = ML_W + 2 * ML_HEADS
    w0 = ab_w_in[0]
    ab_w32 = jnp.concatenate([w0[:, :gate_lo], w0[:, sw_lo:], w0[:, gate_lo:sw_lo],
                              jnp.zeros((D_MODEL, LANES - 2 * ML_HEADS), f32)], axis=1)
    w1 = c_w_in[0]
    c_w = jnp.concatenate([w1, jnp.zeros((D_MODEL, LANES - 2 * GD_V_HEADS), f32)], axis=1).astype(bf16)
    wo32 = ab_w_out[0]
    wo = wo32.astype(bf16)
    p = {
        "ab_w": ab_w32.astype(bf16),
        "ab_w32": ab_w32,
        "ab_wo_h32": wo32[:ML_HEADS * ML_DV],
        "ab_wo_a32": wo32[ML_HEADS * ML_DV:],
        "ab_bias": _pad_lanes(jnp.concatenate([ab_b_i[0], ab_b_f[0]])),
        "ab_norm": ab_norm[0].reshape(1, ML_HEADS * ML_DV),
        "ab_sinks": _pad_lanes(ab_sinks[0]),
        "ab_wo_h": wo[:ML_HEADS * ML_DV],
        "ab_wo_a": wo[ML_HEADS * ML_DV:],
        "c_w": c_w,
        "c_conv_w": c_conv_w[0],
        "c_alog": _pad_lanes(c_a_log[0], GD_V_HEADS),
        "c_dt": _pad_lanes(c_dt_bias[0], GD_V_HEADS),
        "c_norm": c_norm[0].reshape(1, GD_HD),
        "c_wo": c_w_out[0].astype(bf16),
        "ln_g": [[ln_g[i, j].reshape(1, D_MODEL) for j in range(2)] for i in range(DEPTH)],
        "ln_b": [[ln_b[i, j].reshape(1, D_MODEL) for j in range(2)] for i in range(DEPTH)],
        "router_wt": router_w.T,
        "router_b": router_b.reshape(N_EXPERTS, 1),
        "ex_gate": ex_gate.astype(bf16),
        "ex_up": ex_up.astype(bf16),
        "ex_down": ex_down.astype(bf16),
    }
    t_p = x_prompt.shape[1]
    y_p, st_p = _trunk(x_prompt, jnp.arange(t_p, dtype=jnp.int32), CHUNK, None, p)
    t_s = x_sample.shape[1]
    state = {"sw_k": cache_swa_k[0], "sw_v": cache_swa_v[0], "ml_C": state_mlstm_C[0], "ml_n": state_mlstm_n[0],
             "ml_m": state_mlstm_m[0], "gd_S": state_gdn_S[0], "gd_conv": state_gdn_conv[0]}
    y_s, st_s = _trunk(x_sample, PAST_LEN + jnp.arange(t_s, dtype=jnp.int32), t_s, state, p)
    return (y_p, y_s) + st_p + st_s
```

```python
import functools
import math

import jax
import jax.numpy as jnp
import numpy as np
from jax import lax
from jax.experimental import pallas as pl
from jax.experimental.pallas import tpu as pltpu

f32 = jnp.float32
bf16 = jnp.bfloat16
HIGHEST = lax.Precision.HIGHEST

D_MODEL = 1024
DEPTH = 2
CHUNK = 64
PAST_LEN = 2048
ML_HEADS = 8
ML_DK = 64
ML_DV = 64
SW_HEADS = 8
SW_KV_HEADS = 2
SW_HD = 64
SW_GROUP = SW_HEADS // SW_KV_HEADS
WINDOW = 128
ROPE_THETA = 10000.0
GD_QK_HEADS = 8
GD_V_HEADS = 16
GD_HD = 128
GD_CONV = 4
GD_QK_W = GD_QK_HEADS * GD_HD
GD_V_W = GD_V_HEADS * GD_HD
GD_CONV_CH = 2 * GD_QK_W + GD_V_W
N_EXPERTS = 16
N_GROUPS = 4
EXP_PER_GROUP = 4
D_EXPERT = 512
DN_ALPHA = (2 * DEPTH) ** 0.25
LN_EPS = 1e-5
RMS_EPS = 1e-6

LANES = 128
ML_W = 4 * ML_HEADS * ML_DK
SW_W = SW_HEADS * SW_HD + 2 * SW_KV_HEADS * SW_HD
VMEM_LIMIT = 56 * 1024 * 1024

SDS = jax.ShapeDtypeStruct


def _cparams(*sem):
    return pltpu.CompilerParams(dimension_semantics=sem, vmem_limit_bytes=VMEM_LIMIT)


def _dot(a, b):
    return jnp.dot(a, b, preferred_element_type=f32, precision=HIGHEST)


def _dot_nt(a, b):
    return lax.dot_general(a, b, (((1,), (1,)), ((), ())), preferred_element_type=f32, precision=HIGHEST)


def _bdot(a, b):
    return jnp.dot(a.astype(bf16), b.astype(bf16), preferred_element_type=f32)


def _bdot_nt(a, b):
    return lax.dot_general(a.astype(bf16), b.astype(bf16), (((1,), (1,)), ((), ())), preferred_element_type=f32)


def _bdot_tn(a, b):
    return jnp.dot(a.T.astype(bf16), b.astype(bf16), preferred_element_type=f32)


def _dots(precise):
    if precise:
        return _dot, _dot_nt, lambda a, b: _dot(a.T, b)
    return _bdot, _bdot_nt, _bdot_tn


def _sigmoid(x):
    return 0.5 + 0.5 * jnp.tanh(0.5 * x)


def _silu(x):
    hx = 0.5 * x
    return hx + hx * jnp.tanh(hx)


def _softplus(x):
    return jnp.maximum(x, 0.0) + jnp.log(1.0 + jnp.exp(-jnp.abs(x)))


def _layer_norm(v, g, b):
    mu = jnp.mean(v, axis=-1, keepdims=True)
    d = v - mu
    var = jnp.mean(d * d, axis=-1, keepdims=True)
    return d * lax.rsqrt(var + LN_EPS) * g + b


def _proj_kernel(x_ref, w_ref, *o_refs, splits, col_chunk, precise):
    mm = _dots(precise)[0]
    xb = x_ref[...] if precise else x_ref[...].astype(bf16)
    for o_ref, (start, width) in zip(o_refs, splits):
        for c in range(0, width, col_chunk):
            cw = min(col_chunk, width - c)
            o_ref[:, c:c + cw] = mm(xb, w_ref[:, start + c:start + c + cw])


def _proj(x2, w, splits, tm):
    n, k = x2.shape
    return pl.pallas_call(
        functools.partial(_proj_kernel, splits=splits, col_chunk=512, precise=w.dtype == f32),
        grid=(n // tm,),
        in_specs=[pl.BlockSpec((tm, k), lambda i: (i, 0)),
                  pl.BlockSpec(w.shape, lambda i: (0, 0), pipeline_mode=pl.Buffered(1))],
        out_specs=[pl.BlockSpec((tm, wd), lambda i: (i, 0)) for _, wd in splits],
        out_shape=[SDS((n, wd), f32) for _, wd in splits],
        compiler_params=_cparams("parallel"),
        name="in_proj",
    )(x2, w)


def _mlstm_kernel(z_ref, g_ref, bias_ref, nw_ref, cn0_ref, m0_ref, h_ref, cn_ref, m_ref, *, L, BB, precise):
    @pl.when(pl.program_id(1) == 0)
    def _():
        cn_ref[...] = cn0_ref[...]
        m_ref[...] = m0_ref[...]

    mm, mm_nt, mm_tn = _dots(precise)
    row = lax.broadcasted_iota(jnp.int32, (L, L), 0)
    col = lax.broadcasted_iota(jnp.int32, (L, L), 1)
    causal = row >= col
    tri = causal.astype(f32)
    lane = lax.broadcasted_iota(jnp.int32, (1, LANES), 1)
    lane_l = lax.broadcasted_iota(jnp.int32, (L, ML_DV), 1)
    one_hot0 = (lane_l == 0).astype(f32)
    for bi in range(BB):
        g = g_ref[bi] + bias_ref[...]
        lf = jnp.minimum(g, 0.0) - jnp.log(1.0 + jnp.exp(-jnp.abs(g)))
        bcum = _dot(tri, lf)
        b_t = bcum.T
        g_t = g.T
        m_row = m_ref[bi]
        new_m = m_row
        outs = []
        for h in range(ML_HEADS):
            b_col = bcum[:, ML_HEADS + h:ML_HEADS + h + 1]
            b_row = b_t[ML_HEADS + h:ML_HEADS + h + 1, :]
            ig_row = g_t[h:h + 1, :]
            ig_col = g[:, h:h + 1]
            m_h = m_row[:, h:h + 1]
            dmat = jnp.where(causal, b_col - b_row + ig_row, -jnp.inf)
            inter = b_col + m_h
            mt = jnp.maximum(inter, jnp.max(dmat, axis=-1, keepdims=True))
            a = jnp.exp(inter - mt)
            q = z_ref[bi, :, h * ML_DK:(h + 1) * ML_DK]
            k = z_ref[bi, :, ML_HEADS * ML_DK + h * ML_DK:ML_HEADS * ML_DK + (h + 1) * ML_DK] * (ML_DK ** -0.5)
            v = z_ref[bi, :, 2 * ML_HEADS * ML_DK + h * ML_DV:2 * ML_HEADS * ML_DK + (h + 1) * ML_DV]
            og = z_ref[bi, :, 3 * ML_HEADS * ML_DK + h * ML_DV:3 * ML_HEADS * ML_DK + (h + 1) * ML_DV]
            s = mm_nt(q, k) * jnp.exp(dmat - mt)
            vext = jnp.concatenate([v, one_hot0], axis=-1)
            cn = cn_ref[bi, h]
            tot = a * mm(q, cn) + mm(s, vext)
            num = tot[:, :ML_DV]
            den = tot[:, ML_DV:ML_DV + 1]
            hh = num / jnp.maximum(jnp.abs(den), jnp.exp(-mt))
            hh = hh * lax.rsqrt(jnp.mean(hh * hh, axis=-1, keepdims=True) + RMS_EPS) * nw_ref[:, h * ML_DV:(h + 1) * ML_DV]
            outs.append(hh * _sigmoid(og))
            m_new = mt[L - 1:L, :]
            b_last = b_col[L - 1:L, :]
            wk = jnp.exp(b_last - b_col + ig_col - m_new)
            dec = jnp.exp(b_last + m_h - m_new)
            cn_ref[bi, h] = dec * cn + mm_tn(k, wk * vext)
            new_m = jnp.where(lane == h, m_new, new_m)
        m_ref[bi] = new_m
        h_ref[bi] = jnp.concatenate(outs, axis=-1)


def _mlstm(z_ml, z_g, bias_row, norm_row, cn0, m0, B, T, L, precise):
    nc = T // L
    bb = min(B, 4)
    tok = lambda b, c: (b, c, 0)
    st4 = lambda b, c: (b, 0, 0, 0)
    st3 = lambda b, c: (b, 0, 0)
    return pl.pallas_call(
        functools.partial(_mlstm_kernel, L=L, BB=bb, precise=precise),
        grid=(B // bb, nc),
        in_specs=[pl.BlockSpec((bb, L, ML_W), tok),
                  pl.BlockSpec((bb, L, LANES), tok),
                  pl.BlockSpec((1, LANES), lambda b, c: (0, 0)),
                  pl.BlockSpec((1, ML_HEADS * ML_DV), lambda b, c: (0, 0)),
                  pl.BlockSpec((bb, ML_HEADS, ML_DK, LANES), st4),
                  pl.BlockSpec((bb, 1, LANES), st3)],
        out_specs=[pl.BlockSpec((bb, L, ML_HEADS * ML_DV), tok),
                   pl.BlockSpec((bb, ML_HEADS, ML_DK, LANES), st4),
                   pl.BlockSpec((bb, 1, LANES), st3)],
        out_shape=[SDS((B, T, ML_HEADS * ML_DV), f32),
                   SDS((B, ML_HEADS, ML_DK, LANES), f32),
                   SDS((B, 1, LANES), f32)],
        compiler_params=_cparams("parallel", "arbitrary"),
        name="mlstm",
    )(z_ml.reshape(B, T, ML_W), z_g.reshape(B, T, LANES), bias_row, norm_row, cn0, m0)


ML_PAIRS = ML_HEADS // 2
ML_REP_QUANTS = 3


def _pair_select_matrix():
    sel = np.zeros((LANES, ML_REP_QUANTS * ML_PAIRS * LANES), np.float32)
    for qn in range(ML_REP_QUANTS):
        for pr in range(ML_PAIRS):
            for half in range(2):
                lo = (qn * ML_PAIRS + pr) * LANES + half * ML_DV
                sel[ML_HEADS * qn + 2 * pr + half, lo:lo + ML_DV] = 1.0
    return jnp.asarray(sel, bf16)


def _exact_select(x, sel):
    hi = x.astype(bf16)
    r1 = x - hi.astype(f32)
    mid = r1.astype(bf16)
    lo = (r1 - mid.astype(f32)).astype(bf16)
    mm = lambda t: jnp.dot(t, sel, preferred_element_type=f32)
    return (mm(hi) + mm(mid)) + mm(lo)


def _mlstm_pair_kernel(z_ref, g_ref, bias_ref, nw_ref, sel_ref, cbd0_ref, nbd0_ref, m0_ref,
                       h_ref, cbd_ref, nbd_ref, m_ref, *, L, BB):
    @pl.when(pl.program_id(1) == 0)
    def _():
        cbd_ref[...] = cbd0_ref[...]
        nbd_ref[...] = nbd0_ref[...]
        m_ref[...] = m0_ref[...]

    tri = (lax.broadcasted_iota(jnp.int32, (L, L), 0) >= lax.broadcasted_iota(jnp.int32, (L, L), 1)).astype(f32)
    row_t = lax.broadcasted_iota(jnp.int32, (L, LANES), 0)
    lane_t = lax.broadcasted_iota(jnp.int32, (L, LANES), 1)
    first_half = lane_t < ML_DV
    causal2 = row_t >= (lane_t % ML_DV)
    rr = lax.broadcasted_iota(jnp.int32, (LANES, LANES), 0)
    cc = lax.broadcasted_iota(jnp.int32, (LANES, LANES), 1)
    same_block = (rr < ML_DV) == (cc < ML_DV)
    ones_bd = same_block.astype(bf16)
    lane_1 = lax.broadcasted_iota(jnp.int32, (1, LANES), 1)
    sel = sel_ref[...]
    n_tiles = ML_REP_QUANTS * ML_PAIRS

    gs = [g_ref[bi] + bias_ref[...] for bi in range(BB)]
    lfs = [jnp.minimum(g, 0.0) - jnp.log(1.0 + jnp.exp(-jnp.abs(g))) for g in gs]
    bc_all = _dot(tri, jnp.concatenate(lfs, axis=1))
    g_ts = [g.T for g in gs]
    b_ts = [bc_all[:, bi * LANES:(bi + 1) * LANES].T for bi in range(BB)]
    r_rows = [g_ts[bi][0:ML_HEADS] - b_ts[bi][ML_HEADS:2 * ML_HEADS] for bi in range(BB)]
    cm = jnp.concatenate([jnp.concatenate(r_rows, axis=0), jnp.full((BB * ML_HEADS, LANES - L), -jnp.inf, f32)], axis=1)
    shift = 1
    while shift < L:
        cm = jnp.maximum(cm, pltpu.roll(cm, shift, 1))
        shift *= 2
    cols = [jnp.concatenate([g_ts[bi][0:ML_HEADS], b_ts[bi][ML_HEADS:2 * ML_HEADS],
                             cm[bi * ML_HEADS:(bi + 1) * ML_HEADS, :L],
                             jnp.zeros((LANES - 3 * ML_HEADS, L), f32)], axis=0).T for bi in range(BB)]
    rep_all = _exact_select(jnp.concatenate(cols, axis=0), sel)
    rep = [rep_all[bi * L:(bi + 1) * L] for bi in range(BB)]
    m_all = jnp.concatenate([m_ref[bi] for bi in range(BB)] + [jnp.zeros((8 - BB, LANES), f32)], axis=0)
    m_rep_all = _exact_select(m_all, sel[:, :ML_PAIRS * LANES])
    m_rep = [m_rep_all[bi:bi + 1] for bi in range(BB)]

    units = [(bi, pr) for bi in range(BB) for pr in range(ML_PAIRS)]
    st = {}
    for u in units:
        bi, pr = u
        tile = lambda qn: rep[bi][:, (qn * ML_PAIRS + pr) * LANES:(qn * ML_PAIRS + pr + 1) * LANES]
        ig_rep, b_rep, cm_rep = tile(0), tile(1), tile(2)
        m_pair = m_rep[bi][:, pr * LANES:(pr + 1) * LANES]
        inter = b_rep + m_pair
        mt = jnp.maximum(inter, b_rep + cm_rep)
        r_row = jnp.concatenate([r_rows[bi][2 * pr:2 * pr + 1], r_rows[bi][2 * pr + 1:2 * pr + 2]], axis=1)
        e = jnp.exp(jnp.where(causal2, (b_rep - mt) + r_row, -jnp.inf))
        m_new = mt[L - 1:L]
        b_last = b_rep[L - 1:L]
        lo = pr * LANES
        q = z_ref[bi, :, lo:lo + LANES].astype(bf16)
        k = z_ref[bi, :, ML_HEADS * ML_DK + lo:ML_HEADS * ML_DK + lo + LANES] * (ML_DK ** -0.5)
        v = z_ref[bi, :, 2 * ML_HEADS * ML_DK + lo:2 * ML_HEADS * ML_DK + lo + LANES]
        kbd = jnp.concatenate([jnp.where(first_half, k, 0.0), jnp.where(first_half, 0.0, k)], axis=0).astype(bf16)
        vbd = jnp.concatenate([jnp.where(first_half, v, 0.0), jnp.where(first_half, 0.0, v)], axis=0).astype(bf16)
        wk = jnp.exp(b_last - b_rep + ig_rep - m_new)
        st[u] = dict(a=jnp.exp(inter - mt), em=jnp.exp(-mt), e=e, m_new=m_new, dec=jnp.exp(b_last + m_pair - m_new),
                     q=q, kbd=kbd, vbd=vbd, k_t=k.T.astype(bf16), wkv=(wk * v).astype(bf16), wk=wk.astype(bf16),
                     cbd=cbd_ref[bi, pr], nbd=nbd_ref[bi, pr])
    for u in units:
        d = st[u]
        d["qk"] = lax.dot_general(d["q"], d["kbd"], (((1,), (1,)), ((), ())), preferred_element_type=f32)
        d["qc"] = jnp.dot(d["q"], d["cbd"].astype(bf16), preferred_element_type=f32)
        d["qn"] = jnp.dot(d["q"], d["nbd"].astype(bf16), preferred_element_type=f32)
    for u in units:
        d = st[u]
        s = (d["qk"] * d["e"]).astype(bf16)
        num = d["a"] * d["qc"] + jnp.dot(s, d["vbd"], preferred_element_type=f32)
        den = d["a"] * d["qn"] + jnp.dot(s, ones_bd, preferred_element_type=f32)
        d["hh"] = num / jnp.maximum(jnp.abs(den), d["em"])
    for u in units:
        bi, pr = u
        d = st[u]
        sq = d["hh"] * d["hh"]
        sq_hi = sq.astype(bf16)
        sq_lo = (sq - sq_hi.astype(f32)).astype(bf16)
        ms = (jnp.dot(sq_hi, ones_bd, preferred_element_type=f32)
              + jnp.dot(sq_lo, ones_bd, preferred_element_type=f32)) * (1.0 / ML_DV)
        lo = pr * LANES
        og = z_ref[bi, :, 3 * ML_HEADS * ML_DK + lo:3 * ML_HEADS * ML_DK + lo + LANES]
        h_ref[bi, :, lo:lo + LANES] = d["hh"] * lax.rsqrt(ms + RMS_EPS) * nw_ref[:, lo:lo + LANES] * _sigmoid(og)
    for u in units:
        bi, pr = u
        d = st[u]
        cbd_ref[bi, pr] = d["dec"] * d["cbd"] + jnp.where(
            same_block, jnp.dot(d["k_t"], d["wkv"], preferred_element_type=f32), 0.0)
        nbd_ref[bi, pr] = d["dec"] * d["nbd"] + jnp.where(
            same_block, jnp.dot(d["k_t"], d["wk"], preferred_element_type=f32), 0.0)
    for bi in range(BB):
        new_m = m_ref[bi]
        for pr in range(ML_PAIRS):
            m_new = st[(bi, pr)]["m_new"]
            new_m = jnp.where(lane_1 == 2 * pr, m_new[:, 0:1], new_m)
            new_m = jnp.where(lane_1 == 2 * pr + 1, m_new[:, ML_DV:ML_DV + 1], new_m)
        m_ref[bi] = new_m


def _mlstm_pairs(z_ml, z_g, bias_row, norm_row, B, T, L):
    assert 2 * L == LANES and ML_DK == ML_DV == L
    nc = T // L
    bb = min(B, 4)
    tok = lambda b, c: (b, c, 0)
    st4 = lambda b, c: (b, 0, 0, 0)
    st3 = lambda b, c: (b, 0, 0)
    const = lambda b, c: (0, 0)
    sel = _pair_select_matrix()
    zeros_bd = jnp.zeros((B, ML_PAIRS, LANES, LANES), f32)
    h, cbd, nbd, m = pl.pallas_call(
        functools.partial(_mlstm_pair_kernel, L=L, BB=bb),
        grid=(B // bb, nc),
        in_specs=[pl.BlockSpec((bb, L, ML_W), tok),
                  pl.BlockSpec((bb, L, LANES), tok),
                  pl.BlockSpec((1, LANES), const),
                  pl.BlockSpec((1, ML_HEADS * ML_DV), const),
                  pl.BlockSpec(sel.shape, const),
                  pl.BlockSpec((bb, ML_PAIRS, LANES, LANES), st4),
                  pl.BlockSpec((bb, ML_PAIRS, LANES, LANES), st4),
                  pl.BlockSpec((bb, 1, LANES), st3)],
        out_specs=[pl.BlockSpec((bb, L, ML_HEADS * ML_DV), tok),
                   pl.BlockSpec((bb, ML_PAIRS, LANES, LANES), st4),
                   pl.BlockSpec((bb, ML_PAIRS, LANES, LANES), st4),
                   pl.BlockSpec((bb, 1, LANES), st3)],
        out_shape=[SDS((B, T, ML_HEADS * ML_DV), f32),
                   SDS((B, ML_PAIRS, LANES, LANES), f32),
                   SDS((B, ML_PAIRS, LANES, LANES), f32),
                   SDS((B, 1, LANES), f32)],
        compiler_params=_cparams("parallel", "arbitrary"),
        name="mlstm_pairs",
    )(z_ml.reshape(B, T, ML_W), z_g.reshape(B, T, LANES), bias_row, norm_row, sel, zeros_bd, zeros_bd,
      jnp.zeros((B, 1, LANES), f32))
    c_out = jnp.stack([cbd[:, :, :ML_DK, :ML_DV], cbd[:, :, ML_DK:, ML_DV:]], axis=2).reshape(B, ML_HEADS, ML_DK, ML_DV)
    n_out = jnp.stack([nbd[:, :, :ML_DK, 0], nbd[:, :, ML_DK:, ML_DV]], axis=2).reshape(B, ML_HEADS, ML_DK)
    return h, c_out, n_out, m[:, 0, :ML_HEADS]


def _rope(x, cos, sin_signed):
    w = x.shape[-1]
    lane = lax.broadcasted_iota(jnp.int32, x.shape, 1)
    swapped = jnp.where((lane % SW_HD) < SW_HD // 2, pltpu.roll(x, w - SW_HD // 2, 1), pltpu.roll(x, SW_HD // 2, 1))
    return x * cos + swapped * sin_signed


def _swa_attend(jobs, sinks_ref, L, precise=False):
    mm, mm_nt, _ = _dots(precise)
    units = [(j, g) for j in range(len(jobs)) for g in range(SW_KV_HEADS)]
    sinks = [jnp.concatenate(
        [jnp.broadcast_to(sinks_ref[:, g * SW_GROUP + i:g * SW_GROUP + i + 1], (L, 1)) for i in range(SW_GROUP)],
        axis=0) for g in range(SW_KV_HEADS)]
    s, p = {}, {}
    for j, g in units:
        qr, keys, _, _ = jobs[j]
        q4 = jnp.concatenate([qr[:, (g * SW_GROUP + i) * SW_HD:(g * SW_GROUP + i + 1) * SW_HD]
                              for i in range(SW_GROUP)], axis=0)
        s[j, g] = mm_nt(q4, keys[:, g * SW_HD:(g + 1) * SW_HD]) * (SW_HD ** -0.5)
    for j, g in units:
        first_valid = jobs[j][3]
        sc = s[j, g]
        if first_valid is not None:
            kcol = lax.broadcasted_iota(jnp.int32, (1, sc.shape[1]), 1)
            sc = jnp.where(kcol >= first_valid, sc, -jnp.inf)
        mx = jnp.maximum(jnp.max(sc, axis=-1, keepdims=True), sinks[g])
        e = jnp.exp(sc - mx)
        p[j, g] = e / (jnp.sum(e, axis=-1, keepdims=True) + jnp.exp(sinks[g] - mx))
    o = {u: mm(p[u], jobs[u[0]][2][:, u[1] * SW_HD:(u[1] + 1) * SW_HD]) for u in units}
    return [jnp.concatenate([o[j, g][i * L:(i + 1) * L, :] for g in range(SW_KV_HEADS) for i in range(SW_GROUP)],
                            axis=-1) for j in range(len(jobs))]


def _swa_prompt_kernel(q_ref, kp_ref, kc_ref, vp_ref, vc_ref, cp_ref, cc_ref, sp_ref, sc_ref, sinks_ref,
                       o_ref, kr_ref, *, L, CB):
    i = pl.program_id(1)
    rows = CB * L
    back = 2 * L
    cos_q = jnp.concatenate([cc_ref[...]] * (SW_HEADS // SW_KV_HEADS), axis=-1)
    sin_q = jnp.concatenate([sc_ref[...]] * (SW_HEADS // SW_KV_HEADS), axis=-1)
    qr = _rope(q_ref[...], cos_q, sin_q)
    k_cur = _rope(kc_ref[...], cc_ref[...], sc_ref[...])
    kr_ref[...] = k_cur
    k_prev = _rope(kp_ref[rows - back:rows, :], cp_ref[rows - back:rows, :], sp_ref[rows - back:rows, :])
    keys = jnp.concatenate([k_prev, k_cur], axis=0)
    vals = jnp.concatenate([vp_ref[rows - back:rows, :], vc_ref[...]], axis=0)
    jobs = []
    for u in range(CB):
        first_valid = jnp.where(i == 0, back - u * L, 0) if u * L < back else None
        jobs.append((qr[u * L:(u + 1) * L], keys[u * L:(u + 3) * L], vals[u * L:(u + 3) * L], first_valid))
    o_ref[...] = jnp.concatenate(_swa_attend(jobs, sinks_ref, L), axis=0)


def _swa_prompt(z_sw, cos_t, sin_t, sinks_row, B, T, L):
    cb = 4
    rows = cb * L
    nb = T // rows
    n = B * T
    kcol = SW_HEADS * SW_HD // LANES
    vcol = kcol + 1
    cur = lambda b, i: (b * nb + i, 0)
    prev = lambda col: (lambda b, i: (b * nb + jnp.maximum(i - 1, 0), col))
    curc = lambda col: (lambda b, i: (b * nb + i, col))
    tab_cur = lambda b, i: (i, 0)
    tab_prev = lambda b, i: (jnp.maximum(i - 1, 0), 0)
    return pl.pallas_call(
        functools.partial(_swa_prompt_kernel, L=L, CB=cb),
        grid=(B, nb),
        in_specs=[pl.BlockSpec((rows, SW_HEADS * SW_HD), cur),
                  pl.BlockSpec((rows, LANES), prev(kcol)), pl.BlockSpec((rows, LANES), curc(kcol)),
                  pl.BlockSpec((rows, LANES), prev(vcol)), pl.BlockSpec((rows, LANES), curc(vcol)),
                  pl.BlockSpec((rows, LANES), tab_prev), pl.BlockSpec((rows, LANES), tab_cur),
                  pl.BlockSpec((rows, LANES), tab_prev), pl.BlockSpec((rows, LANES), tab_cur),
                  pl.BlockSpec((1, LANES), lambda b, i: (0, 0))],
        out_specs=[pl.BlockSpec((rows, SW_HEADS * SW_HD), cur),
                   pl.BlockSpec((rows, LANES), cur)],
        out_shape=[SDS((n, SW_HEADS * SW_HD), f32), SDS((n, LANES), f32)],
        compiler_params=_cparams("parallel", "parallel"),
        name="swa_prompt",
    )(z_sw, z_sw, z_sw, z_sw, z_sw, cos_t, cos_t, sin_t, sin_t, sinks_row)


def _swa_sample_kernel(q_ref, k_ref, v_ref, ck_ref, cv_ref, cos_ref, sin_ref, sinks_ref, o_ref, kr_ref, *, L):
    cos_q = jnp.concatenate([cos_ref[...]] * (SW_HEADS // SW_KV_HEADS), axis=-1)
    sin_q = jnp.concatenate([sin_ref[...]] * (SW_HEADS // SW_KV_HEADS), axis=-1)
    qr = _rope(q_ref[...], cos_q, sin_q)
    kr = _rope(k_ref[...], cos_ref[...], sin_ref[...])
    kr_ref[...] = kr
    keys = jnp.concatenate([ck_ref[0], kr], axis=0)
    vals = jnp.concatenate([cv_ref[0], v_ref[...]], axis=0)
    o_ref[...] = _swa_attend([(qr, keys, vals, None)], sinks_ref, L, precise=True)[0]


def _swa_sample(z_sw, cache_k, cache_v, cos_t, sin_t, sinks_row, B, T):
    n = B * T
    kcol = SW_HEADS * SW_HD // LANES
    return pl.pallas_call(
        functools.partial(_swa_sample_kernel, L=T),
        grid=(B,),
        in_specs=[pl.BlockSpec((T, SW_HEADS * SW_HD), lambda b: (b, 0)),
                  pl.BlockSpec((T, LANES), lambda b: (b, kcol)),
                  pl.BlockSpec((T, LANES), lambda b: (b, kcol + 1)),
                  pl.BlockSpec((1, WINDOW, LANES), lambda b: (b, 0, 0)),
                  pl.BlockSpec((1, WINDOW, LANES), lambda b: (b, 0, 0)),
                  pl.BlockSpec((T, LANES), lambda b: (0, 0)),
                  pl.BlockSpec((T, LANES), lambda b: (0, 0)),
                  pl.BlockSpec((1, LANES), lambda b: (0, 0))],
        out_specs=[pl.BlockSpec((T, SW_HEADS * SW_HD), lambda b: (b, 0)),
                   pl.BlockSpec((T, LANES), lambda b: (b, 0))],
        out_shape=[SDS((n, SW_HEADS * SW_HD), f32), SDS((n, LANES), f32)],
        compiler_params=_cparams("parallel"),
        name="swa_sample",
    )(z_sw, z_sw, z_sw, cache_k, cache_v, cos_t, sin_t, sinks_row)


def _out_ln_kernel(*refs, n_in):
    x_ref = refs[0]
    a_refs = refs[1:1 + n_in]
    w_refs = refs[1 + n_in:1 + 2 * n_in]
    g_ref, b_ref, rw_ref, rb_ref, o_ref, gates_ref, route_ref = refs[1 + 2 * n_in:]
    mm = _dots(w_refs[0].dtype == f32)[0]
    y = mm(a_refs[0][...], w_refs[0][...])
    for a_ref, w_ref in zip(a_refs[1:], w_refs[1:]):
        y = y + mm(a_ref[...], w_ref[...])
    x_new = _layer_norm(DN_ALPHA * x_ref[...] + y, g_ref[...], b_ref[...])
    o_ref[...] = x_new
    gates_ref[...], route_ref[...] = _route(x_new, rw_ref, rb_ref)


def _out_ln(x2, acts, ws, g_row, b_row, rw_t, rb_col, tm):
    n = x2.shape[0]
    row = lambda i: (i, 0)
    const = lambda i: (0, 0)
    col = lambda i: (0, i)
    return pl.pallas_call(
        functools.partial(_out_ln_kernel, n_in=len(acts)),
        grid=(n // tm,),
        in_specs=[pl.BlockSpec((tm, D_MODEL), row)]
        + [pl.BlockSpec((tm, a.shape[1]), row) for a in acts]
        + [pl.BlockSpec(w.shape, const) for w in ws]
        + [pl.BlockSpec((1, D_MODEL), const), pl.BlockSpec((1, D_MODEL), const),
           pl.BlockSpec((N_EXPERTS, D_MODEL), const), pl.BlockSpec((N_EXPERTS, 1), const)],
        out_specs=[pl.BlockSpec((tm, D_MODEL), row), pl.BlockSpec((N_EXPERTS, tm), col), pl.BlockSpec((8, tm), col)],
        out_shape=[SDS((n, D_MODEL), f32), SDS((N_EXPERTS, n), f32), SDS((8, n), f32)],
        compiler_params=_cparams("parallel"),
        name="out_proj_ln",
    )(x2, *acts, *ws, g_row, b_row, rw_t, rb_col)


def _logistic(x):
    return 1.0 / (1.0 + jnp.exp(-x))


def _route(x, rw_ref, rb_ref):
    logits = _dot_nt(rw_ref[...], x)
    aff = _logistic(logits)
    sc = aff + rb_ref[...]
    s = [sc[e:e + 1, :] for e in range(N_EXPERTS)]
    a = [aff[e:e + 1, :] for e in range(N_EXPERTS)]
    scores = []
    for gi in range(N_GROUPS):
        w, x, y, z = s[4 * gi:4 * gi + 4]
        p, q = jnp.maximum(w, x), jnp.minimum(w, x)
        r, t = jnp.maximum(y, z), jnp.minimum(y, z)
        scores.append(jnp.maximum(p, r) + jnp.maximum(jnp.minimum(p, r), jnp.maximum(q, t)))
    best = scores[0]
    gsel = jnp.zeros_like(best, dtype=jnp.int32)
    for gi in range(1, N_GROUPS):
        better = scores[gi] > best
        best = jnp.where(better, scores[gi], best)
        gsel = jnp.where(better, gi, gsel)
    sel = []
    for e in range(N_EXPERTS):
        gi, i = divmod(e, EXP_PER_GROUP)
        beaten = jnp.zeros_like(gsel)
        for j in range(EXP_PER_GROUP):
            if j == i:
                continue
            o = s[4 * gi + j]
            wins = (o >= s[e]) if j < i else (o > s[e])
            beaten = beaten + wins.astype(jnp.int32)
        sel.append((gsel == gi) & (beaten < 2))
    den = jnp.zeros_like(best)
    for e in range(N_EXPERTS):
        den = den + jnp.where(sel[e], a[e], 0.0)
    gate = [jnp.where(sel[e], a[e] / den, 0.0) for e in range(N_EXPERTS)]
    taken = jnp.zeros_like(gsel)
    ea = eb = wa = wb = jnp.zeros_like(best)
    for e in range(N_EXPERTS):
        first = sel[e] & (taken == 0)
        second = sel[e] & (taken == 1)
        ea = jnp.where(first, float(e), ea)
        wa = jnp.where(first, gate[e], wa)
        eb = jnp.where(second, float(e), eb)
        wb = jnp.where(second, gate[e], wb)
        taken = taken + sel[e].astype(jnp.int32)
    route = jnp.concatenate([ea, eb, wa, wb, jnp.zeros((4, ea.shape[1]), f32)], axis=0)
    return jnp.concatenate(gate, axis=0), route


def _moe_kernel(x_ref, gates_ref, wg_ref, wu_ref, wd_ref, g_ref, b_ref, o_ref, xb_ref, acc_ref):
    e = pl.program_id(1)

    @pl.when(e == 0)
    def _():
        xb_ref[...] = x_ref[...].astype(bf16)
        acc_ref[...] = jnp.zeros_like(acc_ref)

    xb = xb_ref[...]
    lane = lax.broadcasted_iota(jnp.int32, gates_ref.shape, 1)
    gcol = jnp.sum(jnp.where(lane == e, gates_ref[...], 0.0), axis=-1, keepdims=True)
    h = _silu(jnp.dot(xb, wg_ref[0], preferred_element_type=f32)) * jnp.dot(xb, wu_ref[0], preferred_element_type=f32)
    acc_ref[...] += jnp.dot((gcol * h).astype(bf16), wd_ref[0], preferred_element_type=f32)

    @pl.when(e == N_EXPERTS - 1)
    def _():
        o_ref[...] = _layer_norm(DN_ALPHA * x_ref[...] + acc_ref[...], g_ref[...], b_ref[...])


def _moe_ln(x2, gates, wg, wu, wd, g_row, b_row, tm):
    n = x2.shape[0]
    row = lambda i, e: (i, 0)
    const = lambda i, e: (0, 0)
    return pl.pallas_call(
        _moe_kernel,
        grid=(n // tm, N_EXPERTS),
        in_specs=[pl.BlockSpec((tm, D_MODEL), row),
                  pl.BlockSpec((tm, N_EXPERTS), row),
                  pl.BlockSpec((1, D_MODEL, D_EXPERT), lambda i, e: (e, 0, 0)),
                  pl.BlockSpec((1, D_MODEL, D_EXPERT), lambda i, e: (e, 0, 0)),
                  pl.BlockSpec((1, D_EXPERT, D_MODEL), lambda i, e: (e, 0, 0)),
                  pl.BlockSpec((1, D_MODEL), const), pl.BlockSpec((1, D_MODEL), const)],
        out_specs=pl.BlockSpec((tm, D_MODEL), row),
        out_shape=SDS((n, D_MODEL), f32),
        scratch_shapes=[pltpu.VMEM((tm, D_MODEL), bf16), pltpu.VMEM((tm, D_MODEL), f32)],
        compiler_params=_cparams("parallel", "arbitrary"),
        name="moe_ln",
    )(x2, gates, wg, wu, wd, g_row, b_row)


N_PAIRS = N_GROUPS * (EXP_PER_GROUP * (EXP_PER_GROUP - 1) // 2)
MOE_TM = 256
_PAIR_A = [g * EXP_PER_GROUP + a for g in range(N_GROUPS) for a in range(EXP_PER_GROUP) for b in range(a + 1, EXP_PER_GROUP)]
_PAIR_B = [g * EXP_PER_GROUP + b for g in range(N_GROUPS) for a in range(EXP_PER_GROUP) for b in range(a + 1, EXP_PER_GROUP)]


def _gather_rows_kernel(idx_ref, src_ref, o_ref, sem, *, rows):
    base = pl.program_id(0) * rows

    def row_copy(j):
        return pltpu.make_async_copy(src_ref.at[pl.ds(idx_ref[base + j], 1)], o_ref.at[pl.ds(j, 1)], sem)

    def issue(j, carry):
        row_copy(j).start()
        return carry

    def drain(j, carry):
        row_copy(j).wait()
        return carry

    lax.fori_loop(0, rows, issue, 0, unroll=8)
    lax.fori_loop(0, rows, drain, 0, unroll=8)


def _gather_rows(src, idx, rows):
    n_out = idx.shape[0]
    d = src.shape[1]
    return pl.pallas_call(
        functools.partial(_gather_rows_kernel, rows=rows),
        grid_spec=pltpu.PrefetchScalarGridSpec(
            num_scalar_prefetch=1,
            grid=(n_out // rows,),
            in_specs=[pl.BlockSpec(memory_space=pl.ANY)],
            out_specs=pl.BlockSpec((rows, d), lambda i, idx_ref: (i, 0)),
            scratch_shapes=[pltpu.SemaphoreType.DMA(())]),
        out_shape=SDS((n_out, d), f32),
        compiler_params=_cparams("arbitrary"),
        name="gather_rows",
    )(idx, src)


def _scatter_rows_kernel(idx_ref, src_ref, init_ref, o_ref, sem, *, rows):
    del init_ref
    base = pl.program_id(0) * rows

    def row_copy(j):
        return pltpu.make_async_copy(src_ref.at[pl.ds(j, 1)], o_ref.at[pl.ds(idx_ref[base + j], 1)], sem)

    def issue(j, carry):
        row_copy(j).start()
        return carry

    def drain(j, carry):
        row_copy(j).wait()
        return carry

    lax.fori_loop(0, rows, issue, 0, unroll=8)
    lax.fori_loop(0, rows, drain, 0, unroll=8)


def _scatter_rows(src, idx, n_out, rows):
    n_src, d = src.shape
    return pl.pallas_call(
        functools.partial(_scatter_rows_kernel, rows=rows),
        grid_spec=pltpu.PrefetchScalarGridSpec(
            num_scalar_prefetch=1,
            grid=(n_src // rows,),
            in_specs=[pl.BlockSpec((rows, d), lambda i, idx_ref: (i, 0)), pl.BlockSpec(memory_space=pl.ANY)],
            out_specs=pl.BlockSpec(memory_space=pl.ANY),
            scratch_shapes=[pltpu.SemaphoreType.DMA(())]),
        out_shape=SDS((n_out, d), f32),
        input_output_aliases={2: 0},
        compiler_params=_cparams("arbitrary"),
        name="scatter_rows",
    )(idx, src, jnp.zeros((n_out, d), f32))


def _pair_plan(route, n, tm):
    ea = route[0].astype(jnp.int32)
    eb = route[1].astype(jnp.int32)
    a = ea % EXP_PER_GROUP
    b = eb % EXP_PER_GROUP
    pidx = jnp.where(a == 0, b - 1, jnp.where(a == 1, b + 1, 5))
    pair = (ea // EXP_PER_GROUP) * (N_PAIRS // N_GROUPS) + pidx
    onehot = (pair[:, None] == jnp.arange(N_PAIRS, dtype=jnp.int32)[None, :]).astype(jnp.int32)
    csum = jnp.cumsum(onehot, axis=0)
    counts = csum[-1]
    ntiles = (counts + tm - 1) // tm
    tile_end = jnp.cumsum(ntiles)
    tile_start = tile_end - ntiles
    row_of_token = jnp.sum(onehot * (csum - 1 + (tile_start * tm)[None, :]), axis=1)
    nt = n // tm + N_PAIRS
    tile_id = jnp.arange(nt, dtype=jnp.int32)
    tile_valid = tile_id < tile_end[-1]
    tile_pair = jnp.sum((tile_end[None, :] <= jnp.minimum(tile_id, tile_end[-1] - 1)[:, None]).astype(jnp.int32), axis=1)
    tile_pair = jnp.minimum(tile_pair, N_PAIRS - 1)
    pick = (tile_pair[:, None] == jnp.arange(N_PAIRS, dtype=jnp.int32)[None, :]).astype(jnp.int32)
    tile_a = jnp.sum(pick * jnp.asarray(_PAIR_A, jnp.int32)[None, :], axis=1)
    tile_b = jnp.sum(pick * jnp.asarray(_PAIR_B, jnp.int32)[None, :], axis=1)
    return row_of_token, tile_a, tile_b, tile_valid.astype(jnp.int32)


def _pair_expert_kernel(ta_ref, tb_ref, tv_ref, x_ref, rwt_ref, wga_ref, wua_ref, wda_ref, wgb_ref, wub_ref, wdb_ref,
                        g_ref, b_ref, o_ref):
    i = pl.program_id(0)
    valid = tv_ref[i] == 1

    @pl.when(valid)
    def _():
        x = x_ref[...]
        xb = x.astype(bf16)
        aff_a = _logistic(jnp.sum(x * rwt_ref[pl.ds(ta_ref[i], 1), :], axis=-1, keepdims=True))
        aff_b = _logistic(jnp.sum(x * rwt_ref[pl.ds(tb_ref[i], 1), :], axis=-1, keepdims=True))
        den = aff_a + aff_b
        acc = None
        for w, (wg, wu, wd) in ((aff_a / den, (wga_ref, wua_ref, wda_ref)), (aff_b / den, (wgb_ref, wub_ref, wdb_ref))):
            h = _silu(jnp.dot(xb, wg[0], preferred_element_type=f32)) * jnp.dot(xb, wu[0], preferred_element_type=f32)
            y = jnp.dot((w * h).astype(bf16), wd[0], preferred_element_type=f32)
            acc = y if acc is None else acc + y
        o_ref[...] = _layer_norm(DN_ALPHA * x + acc, g_ref[...], b_ref[...])

    @pl.when(jnp.logical_not(valid))
    def _():
        o_ref[...] = jnp.zeros_like(o_ref)


def _pair_experts(xs, rw_t, tile_a, tile_b, tile_valid, wg, wu, wd, g_row, b_row, tm):
    rows = xs.shape[0]
    row = lambda i, ta, tb, tv: (i, 0)
    const = lambda i, ta, tb, tv: (0, 0)
    ex_a = lambda i, ta, tb, tv: (ta[i], 0, 0)
    ex_b = lambda i, ta, tb, tv: (tb[i], 0, 0)
    up = pl.BlockSpec((1, D_MODEL, D_EXPERT), ex_a), pl.BlockSpec((1, D_MODEL, D_EXPERT), ex_b)
    down = pl.BlockSpec((1, D_EXPERT, D_MODEL), ex_a), pl.BlockSpec((1, D_EXPERT, D_MODEL), ex_b)
    return pl.pallas_call(
        _pair_expert_kernel,
        grid_spec=pltpu.PrefetchScalarGridSpec(
            num_scalar_prefetch=3,
            grid=(rows // tm,),
            in_specs=[pl.BlockSpec((tm, D_MODEL), row), pl.BlockSpec((N_EXPERTS, D_MODEL), const),
                      up[0], up[0], down[0], up[1], up[1], down[1],
                      pl.BlockSpec((1, D_MODEL), const), pl.BlockSpec((1, D_MODEL), const)],
            out_specs=pl.BlockSpec((tm, D_MODEL), row)),
        out_shape=SDS((rows, D_MODEL), f32),
        compiler_params=_cparams("arbitrary"),
        name="pair_experts",
    )(tile_a, tile_b, tile_valid, xs, rw_t, wg, wu, wd, wg, wu, wd, g_row, b_row)


def _unit_lower_inverses(mats, L):
    row = lax.broadcasted_iota(jnp.int32, (L, L), 0)
    col = lax.broadcasted_iota(jnp.int32, (L, L), 1)
    eye = (row == col).astype(f32)
    ps = [eye - a for a in mats]
    pws = [a.astype(bf16) for a in mats]
    span = 2
    while span < L:
        pws = [jnp.dot(pw, pw, preferred_element_type=f32).astype(bf16) for pw in pws]
        ps = [p + jnp.dot(pw, p.astype(bf16), preferred_element_type=f32) for p, pw in zip(ps, pws)]
        span *= 2
    return ps


def _gdn_kernel(x_ref, zg_ref, ba_ref, cw_ref, alog_ref, dt_ref, nw_ref, s0_ref, cb_ref,
                o_ref, s_ref, prev_ref, *, L):
    @pl.when(pl.program_id(1) == 0)
    def _():
        s_ref[...] = s0_ref[...]
        prev_ref[...] = cb_ref[0]

    def conv_silu(lo, width):
        cur = x_ref[:, lo:lo + width]
        cat = jnp.concatenate([prev_ref[:, lo:lo + width], cur], axis=0)
        acc = cat[5:5 + L] * cw_ref[0:1, lo:lo + width]
        acc = acc + cat[6:6 + L] * cw_ref[1:2, lo:lo + width]
        acc = acc + cat[7:7 + L] * cw_ref[2:3, lo:lo + width]
        acc = acc + cur * cw_ref[3:4, lo:lo + width]
        return _silu(acc)

    def l2n(v):
        return v * lax.rsqrt(jnp.sum(v * v, axis=-1, keepdims=True) + 1e-6)

    ba = ba_ref[...]
    beta = _sigmoid(ba)
    gl = -jnp.exp(alog_ref[...]) * _softplus(ba + dt_ref[...])
    row = lax.broadcasted_iota(jnp.int32, (L, L), 0)
    col = lax.broadcasted_iota(jnp.int32, (L, L), 1)
    incl = row >= col
    strict = row > col
    gcum = _dot(incl.astype(f32), gl)
    gcum_t = gcum.T
    rep = GD_V_HEADS // GD_QK_HEADS
    qs, ks, amats, qkds, egs, g_cols, rhss = [], [], [], [], [], [], []
    for j in range(GD_QK_HEADS):
        q = l2n(conv_silu(j * GD_HD, GD_HD)) * (GD_HD ** -0.5)
        k = l2n(conv_silu(GD_QK_W + j * GD_HD, GD_HD))
        qb, kb = q.astype(bf16), k.astype(bf16)
        kk = lax.dot_general(kb, kb, (((1,), (1,)), ((), ())), preferred_element_type=f32)
        qk = lax.dot_general(qb, kb, (((1,), (1,)), ((), ())), preferred_element_type=f32)
        qs.append(qb)
        ks.append(k)
        for r in range(rep):
            hv = j * rep + r
            v = conv_silu(2 * GD_QK_W + hv * GD_HD, GD_HD)
            g_col = gcum[:, GD_V_HEADS + hv:GD_V_HEADS + hv + 1]
            g_row = gcum_t[GD_V_HEADS + hv:GD_V_HEADS + hv + 1, :]
            b_col = beta[:, hv:hv + 1]
            decay = jnp.exp(jnp.where(incl, g_col - g_row, -jnp.inf))
            eg = jnp.exp(g_col)
            amats.append(jnp.where(strict, b_col * kk * decay, 0.0))
            qkds.append((qk * decay).astype(bf16))
            egs.append(eg)
            g_cols.append(g_col)
            rhss.append(jnp.concatenate([b_col * v, (b_col * eg) * k], axis=-1))
    tinvs = _unit_lower_inverses(amats, L)
    sols = [_bdot(t, r) for t, r in zip(tinvs, rhss)]
    for hv in range(GD_V_HEADS):
        j = hv // rep
        st = s_ref[0, hv]
        stb = st.astype(bf16)
        sol = sols[hv]
        wn = sol[:, :GD_HD] - jnp.dot(sol[:, GD_HD:].astype(bf16), stb, preferred_element_type=f32)
        wnb = wn.astype(bf16)
        o = egs[hv] * jnp.dot(qs[j], stb, preferred_element_type=f32) + jnp.dot(qkds[hv], wnb, preferred_element_type=f32)
        g_col = g_cols[hv]
        g_last = g_col[L - 1:L, :]
        kt = (jnp.exp(g_last - g_col) * ks[j]).T.astype(bf16)
        s_ref[0, hv] = jnp.exp(g_last) * st + jnp.dot(kt, wnb, preferred_element_type=f32)
        o = o * lax.rsqrt(jnp.mean(o * o, axis=-1, keepdims=True) + RMS_EPS) * nw_ref[...]
        o_ref[:, hv * GD_HD:(hv + 1) * GD_HD] = o * _silu(zg_ref[:, hv * GD_HD:(hv + 1) * GD_HD])
    prev_ref[...] = x_ref[L - 8:L, :]


def _gdn(qkv, zg, ba, conv_w, alog_row, dt_row, norm_row, s0, conv8, B, T, L):
    nc = T // L
    n = B * T
    tok = lambda b, c: (b * nc + c, 0)
    const = lambda b, c: (0, 0)
    return pl.pallas_call(
        functools.partial(_gdn_kernel, L=L),
        grid=(B, nc),
        in_specs=[pl.BlockSpec((L, GD_CONV_CH), tok),
                  pl.BlockSpec((L, GD_V_W), tok),
                  pl.BlockSpec((L, LANES), tok),
                  pl.BlockSpec((GD_CONV, GD_CONV_CH), const),
                  pl.BlockSpec((1, LANES), const),
                  pl.BlockSpec((1, LANES), const),
                  pl.BlockSpec((1, GD_HD), const),
                  pl.BlockSpec((1, GD_V_HEADS, GD_HD, GD_HD), lambda b, c: (b, 0, 0, 0)),
                  pl.BlockSpec((1, 8, GD_CONV_CH), lambda b, c: (b, 0, 0))],
        out_specs=[pl.BlockSpec((L, GD_V_W), tok),
                   pl.BlockSpec((1, GD_V_HEADS, GD_HD, GD_HD), lambda b, c: (b, 0, 0, 0))],
        out_shape=[SDS((n, GD_V_W), f32), SDS((B, GD_V_HEADS, GD_HD, GD_HD), f32)],
        scratch_shapes=[pltpu.VMEM((8, GD_CONV_CH), f32)],
        compiler_params=_cparams("parallel", "arbitrary"),
        name="gdn",
    )(qkv, zg, ba, conv_w, alog_row, dt_row, norm_row, s0, conv8)


def _pad_lanes(row, offset=0):
    return jnp.zeros((1, LANES), f32).at[0, offset:offset + row.shape[0]].set(row.astype(f32))


def _rope_tables(pos):
    half = SW_HD // 2
    inv = ROPE_THETA ** (-jnp.arange(half, dtype=f32) / half)
    ang = pos.astype(f32)[:, None] * inv[None, :]
    cos, sin = jnp.cos(ang), jnp.sin(ang)
    cos_t = jnp.concatenate([cos, cos] * SW_KV_HEADS, axis=-1)
    sin_t = jnp.concatenate([-sin, sin] * SW_KV_HEADS, axis=-1)
    return cos_t, sin_t


def _tile(n, pref):
    return pref if n % pref == 0 else n


def _trunk(x, pos, L, state, p):
    B, T, _ = x.shape
    n = B * T
    x2 = x.reshape(n, D_MODEL)
    tm = _tile(n, 512)
    precise = state is not None
    ab_w, wo_h, wo_a = (p["ab_w32"], p["ab_wo_h32"], p["ab_wo_a32"]) if precise else (p["ab_w"], p["ab_wo_h"], p["ab_wo_a"])

    z_ml, z_sw, z_g = _proj(x2, ab_w, ((0, ML_W), (ML_W, SW_W), (ML_W + SW_W, LANES)), tm)
    if state is None:
        h_ml, ml_c, ml_n, ml_m = _mlstm_pairs(z_ml, z_g, p["ab_bias"], p["ab_norm"], B, T, L)
    else:
        cn0 = jnp.concatenate([state["ml_C"], state["ml_n"][..., None],
                               jnp.zeros((B, ML_HEADS, ML_DK, LANES - ML_DV - 1), f32)], axis=-1)
        m0 = jnp.zeros((B, 1, LANES), f32).at[:, 0, :ML_HEADS].set(state["ml_m"])
        h_ml, cn, m_out = _mlstm(z_ml, z_g, p["ab_bias"], p["ab_norm"], cn0, m0, B, T, L, precise)
        ml_c, ml_n, ml_m = cn[..., :ML_DV], cn[..., ML_DV], m_out[:, 0, :ML_HEADS]
    h_ml = h_ml.reshape(n, ML_HEADS * ML_DV)
    cos_t, sin_t = _rope_tables(pos)
    if state is None:
        a_sw, k_rot = _swa_prompt(z_sw, cos_t, sin_t, p["ab_sinks"], B, T, L)
    else:
        a_sw, k_rot = _swa_sample(z_sw, state["sw_k"].reshape(B, WINDOW, LANES),
                                  state["sw_v"].reshape(B, WINDOW, LANES), cos_t, sin_t, p["ab_sinks"], B, T)
    keep = min(T, WINDOW)
    new_k = k_rot.reshape(B, T, LANES)[:, T - keep:].reshape(B, keep, SW_KV_HEADS, SW_HD)
    new_v = z_sw.reshape(B, T, SW_W)[:, T - keep:, SW_HEADS * SW_HD + LANES:].reshape(B, keep, SW_KV_HEADS, SW_HD)
    x2, gates_t, route = _out_ln(x2, [h_ml, a_sw], [wo_h, wo_a], p["ln_g"][0][0], p["ln_b"][0][0],
                                 p["router_wt"], p["router_b"], tm)
    x2 = _moe_block(x2, gates_t, route, p, 0)

    tm1 = _tile(n, 256)
    qkv, zg, ba = _proj(x2, p["c_w"], ((0, GD_CONV_CH), (GD_CONV_CH, GD_V_W), (GD_CONV_CH + GD_V_W, LANES)), tm1)
    if state is None:
        s0 = jnp.zeros((B, GD_V_HEADS, GD_HD, GD_HD), f32)
        conv8 = jnp.zeros((B, 8, GD_CONV_CH), f32)
    else:
        s0 = state["gd_S"]
        conv8 = jnp.concatenate([jnp.zeros((B, 8 - (GD_CONV - 1), GD_CONV_CH), f32), state["gd_conv"]], axis=1)
    o_gd, s_out = _gdn(qkv, zg, ba, p["c_conv_w"], p["c_alog"], p["c_dt"], p["c_norm"], s0, conv8, B, T, L)
    new_conv = qkv.reshape(B, T, GD_CONV_CH)[:, T - (GD_CONV - 1):]
    x2, gates_t, route = _out_ln(x2, [o_gd], [p["c_wo"]], p["ln_g"][1][0], p["ln_b"][1][0],
                                 p["router_wt"], p["router_b"], tm)
    x2 = _moe_block(x2, gates_t, route, p, 1)

    outs = (new_k[None], new_v[None], ml_c[None], ml_n[None], ml_m[None],
            s_out[None], new_conv[None])
    return x2.reshape(B, T, D_MODEL), outs


def _moe_block(x2, gates_t, route, p, layer):
    n = x2.shape[0]
    wg, wu, wd = p["ex_gate"][layer], p["ex_up"][layer], p["ex_down"][layer]
    g_row, b_row = p["ln_g"][layer][1], p["ln_b"][layer][1]
    if n < N_PAIRS * MOE_TM:
        return _moe_ln(x2, gates_t.T, wg, wu, wd, g_row, b_row, _tile(n, 1024))
    row_of_token, tile_a, tile_b, tile_valid = _pair_plan(route, n, MOE_TM)
    xs = _scatter_rows(x2, row_of_token, n + N_PAIRS * MOE_TM, MOE_TM)
    ys = _pair_experts(xs, p["router_wt"], tile_a, tile_b, tile_valid, wg, wu, wd, g_row, b_row, MOE_TM)
    return _gather_rows(ys, row_of_token, MOE_TM)


def kernel(x_prompt, x_sample, cache_swa_k, cache_swa_v, state_mlstm_C, state_mlstm_n, state_mlstm_m, state_gdn_S, state_gdn_conv, ab_w_in, ab_b_i, ab_b_f, ab_norm, ab_sinks, ab_w_out, c_w_in, c_conv_w, c_a_log, c_dt_bias, c_norm, c_w_out, ln_g, ln_b, router_w, router_b, ex_gate, ex_up, ex_down):
    gate_lo = ML_W
    sw_lo = ML_W + 2 * ML_HEADS
    w0 = ab_w_in[0]
    ab_w32 = jnp.concatenate([w0[:, :gate_lo], w0[:, sw_lo:], w0[:, gate_lo:sw_lo],
                              jnp.zeros((D_MODEL, LANES - 2 * ML_HEADS), f32)], axis=1)
    w1 = c_w_in[0]
    c_w = jnp.concatenate([w1, jnp.zeros((D_MODEL, LANES - 2 * GD_V_HEADS), f32)], axis=1).astype(bf16)
    wo32 = ab_w_out[0]
    wo = wo32.astype(bf16)
    p = {
        "ab_w": ab_w32.astype(bf16),
        "ab_w32": ab_w32,
        "ab_wo_h32": wo32[:ML_HEADS * ML_DV],
        "ab_wo_a32": wo32[ML_HEADS * ML_DV:],
        "ab_bias": _pad_lanes(jnp.concatenate([ab_b_i[0], ab_b_f[0]])),
        "ab_norm": ab_norm[0].reshape(1, ML_HEADS * ML_DV),
        "ab_sinks": _pad_lanes(ab_sinks[0]),
        "ab_wo_h": wo[:ML_HEADS * ML_DV],
        "ab_wo_a": wo[ML_HEADS * ML_DV:],
        "c_w": c_w,
        "c_conv_w": c_conv_w[0],
        "c_alog": _pad_lanes(c_a_log[0], GD_V_HEADS),
        "c_dt": _pad_lanes(c_dt_bias[0], GD_V_HEADS),
        "c_norm": c_norm[0].reshape(1, GD_HD),
        "c_wo": c_w_out[0].astype(bf16),
        "ln_g": [[ln_g[i, j].reshape(1, D_MODEL) for j in range(2)] for i in range(DEPTH)],
        "ln_b": [[ln_b[i, j].reshape(1, D_MODEL) for j in range(2)] for i in range(DEPTH)],
        "router_wt": router_w.T,
        "router_b": router_b.reshape(N_EXPERTS, 1),
        "ex_gate": ex_gate.astype(bf16),
        "ex_up": ex_up.astype(bf16),
        "ex_down": ex_down.astype(bf16),
    }
    t_p = x_prompt.shape[1]
    y_p, st_p = _trunk(x_prompt, jnp.arange(t_p, dtype=jnp.int32), CHUNK, None, p)
    t_s = x_sample.shape[1]
    state = {"sw_k": cache_swa_k[0], "sw_v": cache_swa_v[0], "ml_C": state_mlstm_C[0], "ml_n": state_mlstm_n[0],
             "ml_m": state_mlstm_m[0], "gd_S": state_gdn_S[0], "gd_conv": state_gdn_conv[0]}
    y_s, st_s = _trunk(x_sample, PAST_LEN + jnp.arange(t_s, dtype=jnp.int32), t_s, state, p)
    return (y_p, y_s) + st_p + st_s
```

```python
import functools
import math

import jax
import jax.numpy as jnp
import numpy as np
from jax import lax
from jax.experimental import pallas as pl
from jax.experimental.pallas import tpu as pltpu

f32 = jnp.float32
bf16 = jnp.bfloat16
HIGHEST = lax.Precision.HIGHEST

D_MODEL = 1024
DEPTH = 2
CHUNK = 64
PAST_LEN = 2048
ML_HEADS = 8
ML_DK = 64
ML_DV = 64
SW_HEADS = 8
SW_KV_HEADS = 2
SW_HD = 64
SW_GROUP = SW_HEADS // SW_KV_HEADS
WINDOW = 128
ROPE_THETA = 10000.0
GD_QK_HEADS = 8
GD_V_HEADS = 16
GD_HD = 128
GD_CONV = 4
GD_QK_W = GD_QK_HEADS * GD_HD
GD_V_W = GD_V_HEADS * GD_HD
GD_CONV_CH = 2 * GD_QK_W + GD_V_W
N_EXPERTS = 16
N_GROUPS = 4
EXP_PER_GROUP = 4
D_EXPERT = 512
DN_ALPHA = (2 * DEPTH) ** 0.25
LN_EPS = 1e-5
RMS_EPS = 1e-6

LANES = 128
ML_W = 4 * ML_HEADS * ML_DK
SW_W = SW_HEADS * SW_HD + 2 * SW_KV_HEADS * SW_HD
VMEM_LIMIT = 56 * 1024 * 1024

SDS = jax.ShapeDtypeStruct


def _cparams(*sem):
    return pltpu.CompilerParams(dimension_semantics=sem, vmem_limit_bytes=VMEM_LIMIT)


def _dot(a, b):
    return jnp.dot(a, b, preferred_element_type=f32, precision=HIGHEST)


def _dot_nt(a, b):
    return lax.dot_general(a, b, (((1,), (1,)), ((), ())), preferred_element_type=f32, precision=HIGHEST)


def _bdot(a, b):
    return jnp.dot(a.astype(bf16), b.astype(bf16), preferred_element_type=f32)


def _bdot_nt(a, b):
    return lax.dot_general(a.astype(bf16), b.astype(bf16), (((1,), (1,)), ((), ())), preferred_element_type=f32)


def _bdot_tn(a, b):
    return jnp.dot(a.T.astype(bf16), b.astype(bf16), preferred_element_type=f32)


def _dots(precise):
    if precise:
        return _dot, _dot_nt, lambda a, b: _dot(a.T, b)
    return _bdot, _bdot_nt, _bdot_tn


def _sigmoid(x):
    return 0.5 + 0.5 * jnp.tanh(0.5 * x)


def _silu(x):
    hx = 0.5 * x
    return hx + hx * jnp.tanh(hx)


def _softplus(x):
    return jnp.maximum(x, 0.0) + jnp.log(1.0 + jnp.exp(-jnp.abs(x)))


def _layer_norm(v, g, b):
    mu = jnp.mean(v, axis=-1, keepdims=True)
    d = v - mu
    var = jnp.mean(d * d, axis=-1, keepdims=True)
    return d * lax.rsqrt(var + LN_EPS) * g + b


def _proj_kernel(x_ref, w_ref, *o_refs, splits, col_chunk, precise):
    mm = _dots(precise)[0]
    xb = x_ref[...] if precise else x_ref[...].astype(bf16)
    for o_ref, (start, width) in zip(o_refs, splits):
        for c in range(0, width, col_chunk):
            cw = min(col_chunk, width - c)
            o_ref[:, c:c + cw] = mm(xb, w_ref[:, start + c:start + c + cw])


def _proj(x2, w, splits, tm):
    n, k = x2.shape
    return pl.pallas_call(
        functools.partial(_proj_kernel, splits=splits, col_chunk=512, precise=w.dtype == f32),
        grid=(n // tm,),
        in_specs=[pl.BlockSpec((tm, k), lambda i: (i, 0)),
                  pl.BlockSpec(w.shape, lambda i: (0, 0), pipeline_mode=pl.Buffered(1))],
        out_specs=[pl.BlockSpec((tm, wd), lambda i: (i, 0)) for _, wd in splits],
        out_shape=[SDS((n, wd), f32) for _, wd in splits],
        compiler_params=_cparams("parallel"),
        name="in_proj",
    )(x2, w)


def _mlstm_kernel(z_ref, g_ref, bias_ref, nw_ref, cn0_ref, m0_ref, h_ref, cn_ref, m_ref, *, L, BB, precise):
    @pl.when(pl.program_id(1) == 0)
    def _():
        cn_ref[...] = cn0_ref[...]
        m_ref[...] = m0_ref[...]

    mm, mm_nt, mm_tn = _dots(precise)
    row = lax.broadcasted_iota(jnp.int32, (L, L), 0)
    col = lax.broadcasted_iota(jnp.int32, (L, L), 1)
    causal = row >= col
    tri = causal.astype(f32)
    lane = lax.broadcasted_iota(jnp.int32, (1, LANES), 1)
    lane_l = lax.broadcasted_iota(jnp.int32, (L, ML_DV), 1)
    one_hot0 = (lane_l == 0).astype(f32)
    for bi in range(BB):
        g = g_ref[bi] + bias_ref[...]
        lf = jnp.minimum(g, 0.0) - jnp.log(1.0 + jnp.exp(-jnp.abs(g)))
        bcum = _dot(tri, lf)
        b_t = bcum.T
        g_t = g.T
        m_row = m_ref[bi]
        new_m = m_row
        outs = []
        for h in range(ML_HEADS):
            b_col = bcum[:, ML_HEADS + h:ML_HEADS + h + 1]
            b_row = b_t[ML_HEADS + h:ML_HEADS + h + 1, :]
            ig_row = g_t[h:h + 1, :]
            ig_col = g[:, h:h + 1]
            m_h = m_row[:, h:h + 1]
            dmat = jnp.where(causal, b_col - b_row + ig_row, -jnp.inf)
            inter = b_col + m_h
            mt = jnp.maximum(inter, jnp.max(dmat, axis=-1, keepdims=True))
            a = jnp.exp(inter - mt)
            q = z_ref[bi, :, h * ML_DK:(h + 1) * ML_DK]
            k = z_ref[bi, :, ML_HEADS * ML_DK + h * ML_DK:ML_HEADS * ML_DK + (h + 1) * ML_DK] * (ML_DK ** -0.5)
            v = z_ref[bi, :, 2 * ML_HEADS * ML_DK + h * ML_DV:2 * ML_HEADS * ML_DK + (h + 1) * ML_DV]
            og = z_ref[bi, :, 3 * ML_HEADS * ML_DK + h * ML_DV:3 * ML_HEADS * ML_DK + (h + 1) * ML_DV]
            s = mm_nt(q, k) * jnp.exp(dmat - mt)
            vext = jnp.concatenate([v, one_hot0], axis=-1)
            cn = cn_ref[bi, h]
            tot = a * mm(q, cn) + mm(s, vext)
            num = tot[:, :ML_DV]
            den = tot[:, ML_DV:ML_DV + 1]
            hh = num / jnp.maximum(jnp.abs(den), jnp.exp(-mt))
            hh = hh * lax.rsqrt(jnp.mean(hh * hh, axis=-1, keepdims=True) + RMS_EPS) * nw_ref[:, h * ML_DV:(h + 1) * ML_DV]
            outs.append(hh * _sigmoid(og))
            m_new = mt[L - 1:L, :]
            b_last = b_col[L - 1:L, :]
            wk = jnp.exp(b_last - b_col + ig_col - m_new)
            dec = jnp.exp(b_last + m_h - m_new)
            cn_ref[bi, h] = dec * cn + mm_tn(k, wk * vext)
            new_m = jnp.where(lane == h, m_new, new_m)
        m_ref[bi] = new_m
        h_ref[bi] = jnp.concatenate(outs, axis=-1)


def _mlstm(z_ml, z_g, bias_row, norm_row, cn0, m0, B, T, L, precise):
    nc = T // L
    bb = min(B, 4)
    tok = lambda b, c: (b, c, 0)
    st4 = lambda b, c: (b, 0, 0, 0)
    st3 = lambda b, c: (b, 0, 0)
    return pl.pallas_call(
        functools.partial(_mlstm_kernel, L=L, BB=bb, precise=precise),
        grid=(B // bb, nc),
        in_specs=[pl.BlockSpec((bb, L, ML_W), tok),
                  pl.BlockSpec((bb, L, LANES), tok),
                  pl.BlockSpec((1, LANES), lambda b, c: (0, 0)),
                  pl.BlockSpec((1, ML_HEADS * ML_DV), lambda b, c: (0, 0)),
                  pl.BlockSpec((bb, ML_HEADS, ML_DK, LANES), st4),
                  pl.BlockSpec((bb, 1, LANES), st3)],
        out_specs=[pl.BlockSpec((bb, L, ML_HEADS * ML_DV), tok),
                   pl.BlockSpec((bb, ML_HEADS, ML_DK, LANES), st4),
                   pl.BlockSpec((bb, 1, LANES), st3)],
        out_shape=[SDS((B, T, ML_HEADS * ML_DV), f32),
                   SDS((B, ML_HEADS, ML_DK, LANES), f32),
                   SDS((B, 1, LANES), f32)],
        compiler_params=_cparams("parallel", "arbitrary"),
        name="mlstm",
    )(z_ml.reshape(B, T, ML_W), z_g.reshape(B, T, LANES), bias_row, norm_row, cn0, m0)


ML_PAIRS = ML_HEADS // 2
ML_REP_QUANTS = 3


def _pair_select_matrix():
    sel = np.zeros((LANES, ML_REP_QUANTS * ML_PAIRS * LANES), np.float32)
    for qn in range(ML_REP_QUANTS):
        for pr in range(ML_PAIRS):
            for half in range(2):
                lo = (qn * ML_PAIRS + pr) * LANES + half * ML_DV
                sel[ML_HEADS * qn + 2 * pr + half, lo:lo + ML_DV] = 1.0
    return jnp.asarray(sel, bf16)


def _exact_select(x, sel):
    hi = x.astype(bf16)
    r1 = x - hi.astype(f32)
    mid = r1.astype(bf16)
    lo = (r1 - mid.astype(f32)).astype(bf16)
    mm = lambda t: jnp.dot(t, sel, preferred_element_type=f32)
    return (mm(hi) + mm(mid)) + mm(lo)


def _mlstm_pair_kernel(z_ref, g_ref, bias_ref, nw_ref, sel_ref, cbd0_ref, nbd0_ref, m0_ref,
                       h_ref, cbd_ref, nbd_ref, m_ref, *, L, BB):
    @pl.when(pl.program_id(1) == 0)
    def _():
        cbd_ref[...] = cbd0_ref[...]
        nbd_ref[...] = nbd0_ref[...]
        m_ref[...] = m0_ref[...]

    tri = (lax.broadcasted_iota(jnp.int32, (L, L), 0) >= lax.broadcasted_iota(jnp.int32, (L, L), 1)).astype(f32)
    row_t = lax.broadcasted_iota(jnp.int32, (L, LANES), 0)
    lane_t = lax.broadcasted_iota(jnp.int32, (L, LANES), 1)
    first_half = lane_t < ML_DV
    causal2 = row_t >= (lane_t % ML_DV)
    rr = lax.broadcasted_iota(jnp.int32, (LANES, LANES), 0)
    cc = lax.broadcasted_iota(jnp.int32, (LANES, LANES), 1)
    same_block = (rr < ML_DV) == (cc < ML_DV)
    ones_bd = same_block.astype(bf16)
    lane_1 = lax.broadcasted_iota(jnp.int32, (1, LANES), 1)
    sel = sel_ref[...]
    n_tiles = ML_REP_QUANTS * ML_PAIRS

    gs = [g_ref[bi] + bias_ref[...] for bi in range(BB)]
    lfs = [jnp.minimum(g, 0.0) - jnp.log(1.0 + jnp.exp(-jnp.abs(g))) for g in gs]
    bc_all = _dot(tri, jnp.concatenate(lfs, axis=1))
    g_ts = [g.T for g in gs]
    b_ts = [bc_all[:, bi * LANES:(bi + 1) * LANES].T for bi in range(BB)]
    r_rows = [g_ts[bi][0:ML_HEADS] - b_ts[bi][ML_HEADS:2 * ML_HEADS] for bi in range(BB)]
    cm = jnp.concatenate([jnp.concatenate(r_rows, axis=0), jnp.full((BB * ML_HEADS, LANES - L), -jnp.inf, f32)], axis=1)
    shift = 1
    while shift < L:
        cm = jnp.maximum(cm, pltpu.roll(cm, shift, 1))
        shift *= 2
    cols = [jnp.concatenate([g_ts[bi][0:ML_HEADS], b_ts[bi][ML_HEADS:2 * ML_HEADS],
                             cm[bi * ML_HEADS:(bi + 1) * ML_HEADS, :L],
                             jnp.zeros((LANES - 3 * ML_HEADS, L), f32)], axis=0).T for bi in range(BB)]
    rep_all = _exact_select(jnp.concatenate(cols, axis=0), sel)
    rep = [rep_all[bi * L:(bi + 1) * L] for bi in range(BB)]
    m_all = jnp.concatenate([m_ref[bi] for bi in range(BB)] + [jnp.zeros((8 - BB, LANES), f32)], axis=0)
    m_rep_all = _exact_select(m_all, sel[:, :ML_PAIRS * LANES])
    m_rep = [m_rep_all[bi:bi + 1] for bi in range(BB)]

    units = [(bi, pr) for bi in range(BB) for pr in range(ML_PAIRS)]
    st = {}
    for u in units:
        bi, pr = u
        tile = lambda qn: rep[bi][:, (qn * ML_PAIRS + pr) * LANES:(qn * ML_PAIRS + pr + 1) * LANES]
        ig_rep, b_rep, cm_rep = tile(0), tile(1), tile(2)
        m_pair = m_rep[bi][:, pr * LANES:(pr + 1) * LANES]
        inter = b_rep + m_pair
        mt = jnp.maximum(inter, b_rep + cm_rep)
        r_row = jnp.concatenate([r_rows[bi][2 * pr:2 * pr + 1], r_rows[bi][2 * pr + 1:2 * pr + 2]], axis=1)
        e = jnp.exp(jnp.where(causal2, (b_rep - mt) + r_row, -jnp.inf))
        m_new = mt[L - 1:L]
        b_last = b_rep[L - 1:L]
        lo = pr * LANES
        q = z_ref[bi, :, lo:lo + LANES].astype(bf16)
        k = z_ref[bi, :, ML_HEADS * ML_DK + lo:ML_HEADS * ML_DK + lo + LANES] * (ML_DK ** -0.5)
        v = z_ref[bi, :, 2 * ML_HEADS * ML_DK + lo:2 * ML_HEADS * ML_DK + lo + LANES]
        kbd = jnp.concatenate([jnp.where(first_half, k, 0.0), jnp.where(first_half, 0.0, k)], axis=0).astype(bf16)
        vbd = jnp.concatenate([jnp.where(first_half, v, 0.0), jnp.where(first_half, 0.0, v)], axis=0).astype(bf16)
        wk = jnp.exp(b_last - b_rep + ig_rep - m_new)
        st[u] = dict(a=jnp.exp(inter - mt), em=jnp.exp(-mt), e=e, m_new=m_new, dec=jnp.exp(b_last + m_pair - m_new),
                     q=q, kbd=kbd, vbd=vbd, k_t=k.T.astype(bf16), wkv=(wk * v).astype(bf16), wk=wk.astype(bf16),
                     cbd=cbd_ref[bi, pr], nbd=nbd_ref[bi, pr])
    for u in units:
        d = st[u]
        d["qk"] = lax.dot_general(d["q"], d["kbd"], (((1,), (1,)), ((), ())), preferred_element_type=f32)
        d["qc"] = jnp.dot(d["q"], d["cbd"].astype(bf16), preferred_element_type=f32)
        d["qn"] = jnp.dot(d["q"], d["nbd"].astype(bf16), preferred_element_type=f32)
    for u in units:
        d = st[u]
        s = (d["qk"] * d["e"]).astype(bf16)
        num = d["a"] * d["qc"] + jnp.dot(s, d["vbd"], preferred_element_type=f32)
        den = d["a"] * d["qn"] + jnp.dot(s, ones_bd, preferred_element_type=f32)
        d["hh"] = num / jnp.maximum(jnp.abs(den), d["em"])
    for u in units:
        bi, pr = u
        d = st[u]
        sq = d["hh"] * d["hh"]
        sq_hi = sq.astype(bf16)
        sq_lo = (sq - sq_hi.astype(f32)).astype(bf16)
        ms = (jnp.dot(sq_hi, ones_bd, preferred_element_type=f32)
              + jnp.dot(sq_lo, ones_bd, preferred_element_type=f32)) * (1.0 / ML_DV)
        lo = pr * LANES
        og = z_ref[bi, :, 3 * ML_HEADS * ML_DK + lo:3 * ML_HEADS * ML_DK + lo + LANES]
        h_ref[bi, :, lo:lo + LANES] = d["hh"] * lax.rsqrt(ms + RMS_EPS) * nw_ref[:, lo:lo + LANES] * _sigmoid(og)
    for u in units:
        bi, pr = u
        d = st[u]
        cbd_ref[bi, pr] = d["dec"] * d["cbd"] + jnp.where(
            same_block, jnp.dot(d["k_t"], d["wkv"], preferred_element_type=f32), 0.0)
        nbd_ref[bi, pr] = d["dec"] * d["nbd"] + jnp.where(
            same_block, jnp.dot(d["k_t"], d["wk"], preferred_element_type=f32), 0.0)
    for bi in range(BB):
        new_m = m_ref[bi]
        for pr in range(ML_PAIRS):
            m_new = st[(bi, pr)]["m_new"]
            new_m = jnp.where(lane_1 == 2 * pr, m_new[:, 0:1], new_m)
            new_m = jnp.where(lane_1 == 2 * pr + 1, m_new[:, ML_DV:ML_DV + 1], new_m)
        m_ref[bi] = new_m


def _mlstm_pairs(z_ml, z_g, bias_row, norm_row, B, T, L):
    assert 2 * L == LANES and ML_DK == ML_DV == L
    nc = T // L
    bb = min(B, 4)
    tok = lambda b, c: (b, c, 0)
    st4 = lambda b, c: (b, 0, 0, 0)
    st3 = lambda b, c: (b, 0, 0)
    const = lambda b, c: (0, 0)
    sel = _pair_select_matrix()
    zeros_bd = jnp.zeros((B, ML_PAIRS, LANES, LANES), f32)
    h, cbd, nbd, m = pl.pallas_call(
        functools.partial(_mlstm_pair_kernel, L=L, BB=bb),
        grid=(B // bb, nc),
        in_specs=[pl.BlockSpec((bb, L, ML_W), tok),
                  pl.BlockSpec((bb, L, LANES), tok),
                  pl.BlockSpec((1, LANES), const),
                  pl.BlockSpec((1, ML_HEADS * ML_DV), const),
                  pl.BlockSpec(sel.shape, const),
                  pl.BlockSpec((bb, ML_PAIRS, LANES, LANES), st4),
                  pl.BlockSpec((bb, ML_PAIRS, LANES, LANES), st4),
                  pl.BlockSpec((bb, 1, LANES), st3)],
        out_specs=[pl.BlockSpec((bb, L, ML_HEADS * ML_DV), tok),
                   pl.BlockSpec((bb, ML_PAIRS, LANES, LANES), st4),
                   pl.BlockSpec((bb, ML_PAIRS, LANES, LANES), st4),
                   pl.BlockSpec((bb, 1, LANES), st3)],
        out_shape=[SDS((B, T, ML_HEADS * ML_DV), f32),
                   SDS((B, ML_PAIRS, LANES, LANES), f32),
                   SDS((B, ML_PAIRS, LANES, LANES), f32),
                   SDS((B, 1, LANES), f32)],
        compiler_params=_cparams("parallel", "arbitrary"),
        name="mlstm_pairs",
    )(z_ml.reshape(B, T, ML_W), z_g.reshape(B, T, LANES), bias_row, norm_row, sel, zeros_bd, zeros_bd,
      jnp.zeros((B, 1, LANES), f32))
    c_out = jnp.stack([cbd[:, :, :ML_DK, :ML_DV], cbd[:, :, ML_DK:, ML_DV:]], axis=2).reshape(B, ML_HEADS, ML_DK, ML_DV)
    n_out = jnp.stack([nbd[:, :, :ML_DK, 0], nbd[:, :, ML_DK:, ML_DV]], axis=2).reshape(B, ML_HEADS, ML_DK)
    return h, c_out, n_out, m[:, 0, :ML_HEADS]


def _rope(x, cos, sin_signed):
    w = x.shape[-1]
    lane = lax.broadcasted_iota(jnp.int32, x.shape, 1)
    swapped = jnp.where((lane % SW_HD) < SW_HD // 2, pltpu.roll(x, w - SW_HD // 2, 1), pltpu.roll(x, SW_HD // 2, 1))
    return x * cos + swapped * sin_signed


def _swa_attend(jobs, sinks_ref, L, precise=False):
    mm, mm_nt, _ = _dots(precise)
    units = [(j, g) for j in range(len(jobs)) for g in range(SW_KV_HEADS)]
    sinks = [jnp.concatenate(
        [jnp.broadcast_to(sinks_ref[:, g * SW_GROUP + i:g * SW_GROUP + i + 1], (L, 1)) for i in range(SW_GROUP)],
        axis=0) for g in range(SW_KV_HEADS)]
    s, p = {}, {}
    for j, g in units:
        qr, keys, _, _ = jobs[j]
        q4 = jnp.concatenate([qr[:, (g * SW_GROUP + i) * SW_HD:(g * SW_GROUP + i + 1) * SW_HD]
                              for i in range(SW_GROUP)], axis=0)
        s[j, g] = mm_nt(q4, keys[:, g * SW_HD:(g + 1) * SW_HD]) * (SW_HD ** -0.5)
    for j, g in units:
        first_valid = jobs[j][3]
        sc = s[j, g]
        if first_valid is not None:
            kcol = lax.broadcasted_iota(jnp.int32, (1, sc.shape[1]), 1)
            sc = jnp.where(kcol >= first_valid, sc, -jnp.inf)
        mx = jnp.maximum(jnp.max(sc, axis=-1, keepdims=True), sinks[g])
        e = jnp.exp(sc - mx)
        p[j, g] = e / (jnp.sum(e, axis=-1, keepdims=True) + jnp.exp(sinks[g] - mx))
    o = {u: mm(p[u], jobs[u[0]][2][:, u[1] * SW_HD:(u[1] + 1) * SW_HD]) for u in units}
    return [jnp.concatenate([o[j, g][i * L:(i + 1) * L, :] for g in range(SW_KV_HEADS) for i in range(SW_GROUP)],
                            axis=-1) for j in range(len(jobs))]


def _swa_prompt_kernel(q_ref, kp_ref, kc_ref, vp_ref, vc_ref, cp_ref, cc_ref, sp_ref, sc_ref, sinks_ref,
                       o_ref, kr_ref, *, L, CB):
    i = pl.program_id(1)
    rows = CB * L
    back = 2 * L
    cos_q = jnp.concatenate([cc_ref[...]] * (SW_HEADS // SW_KV_HEADS), axis=-1)
    sin_q = jnp.concatenate([sc_ref[...]] * (SW_HEADS // SW_KV_HEADS), axis=-1)
    qr = _rope(q_ref[...], cos_q, sin_q)
    k_cur = _rope(kc_ref[...], cc_ref[...], sc_ref[...])
    kr_ref[...] = k_cur
    k_prev = _rope(kp_ref[rows - back:rows, :], cp_ref[rows - back:rows, :], sp_ref[rows - back:rows, :])
    keys = jnp.concatenate([k_prev, k_cur], axis=0)
    vals = jnp.concatenate([vp_ref[rows - back:rows, :], vc_ref[...]], axis=0)
    jobs = []
    for u in range(CB):
        first_valid = jnp.where(i == 0, back - u * L, 0) if u * L < back else None
        jobs.append((qr[u * L:(u + 1) * L], keys[u * L:(u + 3) * L], vals[u * L:(u + 3) * L], first_valid))
    o_ref[...] = jnp.concatenate(_swa_attend(jobs, sinks_ref, L), axis=0)


def _swa_prompt(z_sw, cos_t, sin_t, sinks_row, B, T, L):
    cb = 4
    rows = cb * L
    nb = T // rows
    n = B * T
    kcol = SW_HEADS * SW_HD // LANES
    vcol = kcol + 1
    cur = lambda b, i: (b * nb + i, 0)
    prev = lambda col: (lambda b, i: (b * nb + jnp.maximum(i - 1, 0), col))
    curc = lambda col: (lambda b, i: (b * nb + i, col))
    tab_cur = lambda b, i: (i, 0)
    tab_prev = lambda b, i: (jnp.maximum(i - 1, 0), 0)
    return pl.pallas_call(
        functools.partial(_swa_prompt_kernel, L=L, CB=cb),
        grid=(B, nb),
        in_specs=[pl.BlockSpec((rows, SW_HEADS * SW_HD), cur),
                  pl.BlockSpec((rows, LANES), prev(kcol)), pl.BlockSpec((rows, LANES), curc(kcol)),
                  pl.BlockSpec((rows, LANES), prev(vcol)), pl.BlockSpec((rows, LANES), curc(vcol)),
                  pl.BlockSpec((rows, LANES), tab_prev), pl.BlockSpec((rows, LANES), tab_cur),
                  pl.BlockSpec((rows, LANES), tab_prev), pl.BlockSpec((rows, LANES), tab_cur),
                  pl.BlockSpec((1, LANES), lambda b, i: (0, 0))],
        out_specs=[pl.BlockSpec((rows, SW_HEADS * SW_HD), cur),
                   pl.BlockSpec((rows, LANES), cur)],
        out_shape=[SDS((n, SW_HEADS * SW_HD), f32), SDS((n, LANES), f32)],
        compiler_params=_cparams("parallel", "parallel"),
        name="swa_prompt",
    )(z_sw, z_sw, z_sw, z_sw, z_sw, cos_t, cos_t, sin_t, sin_t, sinks_row)


def _swa_sample_kernel(q_ref, k_ref, v_ref, ck_ref, cv_ref, cos_ref, sin_ref, sinks_ref, o_ref, kr_ref, *, L):
    cos_q = jnp.concatenate([cos_ref[...]] * (SW_HEADS // SW_KV_HEADS), axis=-1)
    sin_q = jnp.concatenate([sin_ref[...]] * (SW_HEADS // SW_KV_HEADS), axis=-1)
    qr = _rope(q_ref[...], cos_q, sin_q)
    kr = _rope(k_ref[...], cos_ref[...], sin_ref[...])
    kr_ref[...] = kr
    keys = jnp.concatenate([ck_ref[0], kr], axis=0)
    vals = jnp.concatenate([cv_ref[0], v_ref[...]], axis=0)
    o_ref[...] = _swa_attend([(qr, keys, vals, None)], sinks_ref, L, precise=True)[0]


def _swa_sample(z_sw, cache_k, cache_v, cos_t, sin_t, sinks_row, B, T):
    n = B * T
    kcol = SW_HEADS * SW_HD // LANES
    return pl.pallas_call(
        functools.partial(_swa_sample_kernel, L=T),
        grid=(B,),
        in_specs=[pl.BlockSpec((T, SW_HEADS * SW_HD), lambda b: (b, 0)),
                  pl.BlockSpec((T, LANES), lambda b: (b, kcol)),
                  pl.BlockSpec((T, LANES), lambda b: (b, kcol + 1)),
                  pl.BlockSpec((1, WINDOW, LANES), lambda b: (b, 0, 0)),
                  pl.BlockSpec((1, WINDOW, LANES), lambda b: (b, 0, 0)),
                  pl.BlockSpec((T, LANES), lambda b: (0, 0)),
                  pl.BlockSpec((T, LANES), lambda b: (0, 0)),
                  pl.BlockSpec((1, LANES), lambda b: (0, 0))],
        out_specs=[pl.BlockSpec((T, SW_HEADS * SW_HD), lambda b: (b, 0)),
                   pl.BlockSpec((T, LANES), lambda b: (b, 0))],
        out_shape=[SDS((n, SW_HEADS * SW_HD), f32), SDS((n, LANES), f32)],
        compiler_params=_cparams("parallel"),
        name="swa_sample",
    )(z_sw, z_sw, z_sw, cache_k, cache_v, cos_t, sin_t, sinks_row)


def _out_ln_kernel(*refs, n_in):
    x_ref = refs[0]
    a_refs = refs[1:1 + n_in]
    w_refs = refs[1 + n_in:1 + 2 * n_in]
    g_ref, b_ref, rw_ref, rb_ref, o_ref, gates_ref, route_ref = refs[1 + 2 * n_in:]
    mm = _dots(w_refs[0].dtype == f32)[0]
    y = mm(a_refs[0][...], w_refs[0][...])
    for a_ref, w_ref in zip(a_refs[1:], w_refs[1:]):
        y = y + mm(a_ref[...], w_ref[...])
    x_new = _layer_norm(DN_ALPHA * x_ref[...] + y, g_ref[...], b_ref[...])
    o_ref[...] = x_new
    gates_ref[...], route_ref[...] = _route(x_new, rw_ref, rb_ref)


def _out_ln(x2, acts, ws, g_row, b_row, rw_t, rb_col, tm):
    n = x2.shape[0]
    row = lambda i: (i, 0)
    const = lambda i: (0, 0)
    col = lambda i: (0, i)
    return pl.pallas_call(
        functools.partial(_out_ln_kernel, n_in=len(acts)),
        grid=(n // tm,),
        in_specs=[pl.BlockSpec((tm, D_MODEL), row)]
        + [pl.BlockSpec((tm, a.shape[1]), row) for a in acts]
        + [pl.BlockSpec(w.shape, const) for w in ws]
        + [pl.BlockSpec((1, D_MODEL), const), pl.BlockSpec((1, D_MODEL), const),
           pl.BlockSpec((N_EXPERTS, D_MODEL), const), pl.BlockSpec((N_EXPERTS, 1), const)],
        out_specs=[pl.BlockSpec((tm, D_MODEL), row), pl.BlockSpec((N_EXPERTS, tm), col), pl.BlockSpec((8, tm), col)],
        out_shape=[SDS((n, D_MODEL), f32), SDS((N_EXPERTS, n), f32), SDS((8, n), f32)],
        compiler_params=_cparams("parallel"),
        name="out_proj_ln",
    )(x2, *acts, *ws, g_row, b_row, rw_t, rb_col)


def _logistic(x):
    return 1.0 / (1.0 + jnp.exp(-x))


def _route(x, rw_ref, rb_ref):
    logits = _dot_nt(rw_ref[...], x)
    aff = _logistic(logits)
    sc = aff + rb_ref[...]
    s = [sc[e:e + 1, :] for e in range(N_EXPERTS)]
    a = [aff[e:e + 1, :] for e in range(N_EXPERTS)]
    scores = []
    for gi in range(N_GROUPS):
        w, x, y, z = s[4 * gi:4 * gi + 4]
        p, q = jnp.maximum(w, x), jnp.minimum(w, x)
        r, t = jnp.maximum(y, z), jnp.minimum(y, z)
        scores.append(jnp.maximum(p, r) + jnp.maximum(jnp.minimum(p, r), jnp.maximum(q, t)))
    best = scores[0]
    gsel = jnp.zeros_like(best, dtype=jnp.int32)
    for gi in range(1, N_GROUPS):
        better = scores[gi] > best
        best = jnp.where(better, scores[gi], best)
        gsel = jnp.where(better, gi, gsel)
    sel = []
    for e in range(N_EXPERTS):
        gi, i = divmod(e, EXP_PER_GROUP)
        beaten = jnp.zeros_like(gsel)
        for j in range(EXP_PER_GROUP):
            if j == i:
                continue
            o = s[4 * gi + j]
            wins = (o >= s[e]) if j < i else (o > s[e])
            beaten = beaten + wins.astype(jnp.int32)
        sel.append((gsel == gi) & (beaten < 2))
    den = jnp.zeros_like(best)
    for e in range(N_EXPERTS):
        den = den + jnp.where(sel[e], a[e], 0.0)
    gate = [jnp.where(sel[e], a[e] / den, 0.0) for e in range(N_EXPERTS)]
    taken = jnp.zeros_like(gsel)
    ea = eb = wa = wb = jnp.zeros_like(best)
    for e in range(N_EXPERTS):
        first = sel[e] & (taken == 0)
        second = sel[e] & (taken == 1)
        ea = jnp.where(first, float(e), ea)
        wa = jnp.where(first, gate[e], wa)
        eb = jnp.where(second, float(e), eb)
        wb = jnp.where(second, gate[e], wb)
        taken = taken + sel[e].astype(jnp.int32)
    route = jnp.concatenate([ea, eb, wa, wb, jnp.zeros((4, ea.shape[1]), f32)], axis=0)
    return jnp.concatenate(gate, axis=0), route


def _moe_kernel(x_ref, gates_ref, wg_ref, wu_ref, wd_ref, g_ref, b_ref, o_ref, xb_ref, acc_ref):
    e = pl.program_id(1)

    @pl.when(e == 0)
    def _():
        xb_ref[...] = x_ref[...].astype(bf16)
        acc_ref[...] = jnp.zeros_like(acc_ref)

    xb = xb_ref[...]
    lane = lax.broadcasted_iota(jnp.int32, gates_ref.shape, 1)
    gcol = jnp.sum(jnp.where(lane == e, gates_ref[...], 0.0), axis=-1, keepdims=True)
    h = _silu(jnp.dot(xb, wg_ref[0], preferred_element_type=f32)) * jnp.dot(xb, wu_ref[0], preferred_element_type=f32)
    acc_ref[...] += jnp.dot((gcol * h).astype(bf16), wd_ref[0], preferred_element_type=f32)

    @pl.when(e == N_EXPERTS - 1)
    def _():
        o_ref[...] = _layer_norm(DN_ALPHA * x_ref[...] + acc_ref[...], g_ref[...], b_ref[...])


def _moe_ln(x2, gates, wg, wu, wd, g_row, b_row, tm):
    n = x2.shape[0]
    row = lambda i, e: (i, 0)
    const = lambda i, e: (0, 0)
    return pl.pallas_call(
        _moe_kernel,
        grid=(n // tm, N_EXPERTS),
        in_specs=[pl.BlockSpec((tm, D_MODEL), row),
                  pl.BlockSpec((tm, N_EXPERTS), row),
                  pl.BlockSpec((1, D_MODEL, D_EXPERT), lambda i, e: (e, 0, 0)),
                  pl.BlockSpec((1, D_MODEL, D_EXPERT), lambda i, e: (e, 0, 0)),
                  pl.BlockSpec((1, D_EXPERT, D_MODEL), lambda i, e: (e, 0, 0)),
                  pl.BlockSpec((1, D_MODEL), const), pl.BlockSpec((1, D_MODEL), const)],
        out_specs=pl.BlockSpec((tm, D_MODEL), row),
        out_shape=SDS((n, D_MODEL), f32),
        scratch_shapes=[pltpu.VMEM((tm, D_MODEL), bf16), pltpu.VMEM((tm, D_MODEL), f32)],
        compiler_params=_cparams("parallel", "arbitrary"),
        name="moe_ln",
    )(x2, gates, wg, wu, wd, g_row, b_row)


N_PAIRS = N_GROUPS * (EXP_PER_GROUP * (EXP_PER_GROUP - 1) // 2)
MOE_TM = 512
_PAIR_A = [g * EXP_PER_GROUP + a for g in range(N_GROUPS) for a in range(EXP_PER_GROUP) for b in range(a + 1, EXP_PER_GROUP)]
_PAIR_B = [g * EXP_PER_GROUP + b for g in range(N_GROUPS) for a in range(EXP_PER_GROUP) for b in range(a + 1, EXP_PER_GROUP)]


def _gather_rows_kernel(idx_ref, src_ref, o_ref, sem, *, rows):
    base = pl.program_id(0) * rows

    def row_copy(j):
        return pltpu.make_async_copy(src_ref.at[pl.ds(idx_ref[base + j], 1)], o_ref.at[pl.ds(j, 1)], sem)

    def issue(j, carry):
        row_copy(j).start()
        return carry

    def drain(j, carry):
        row_copy(j).wait()
        return carry

    lax.fori_loop(0, rows, issue, 0, unroll=8)
    lax.fori_loop(0, rows, drain, 0, unroll=8)


def _gather_rows(src, idx, rows):
    n_out = idx.shape[0]
    d = src.shape[1]
    return pl.pallas_call(
        functools.partial(_gather_rows_kernel, rows=rows),
        grid_spec=pltpu.PrefetchScalarGridSpec(
            num_scalar_prefetch=1,
            grid=(n_out // rows,),
            in_specs=[pl.BlockSpec(memory_space=pl.ANY)],
            out_specs=pl.BlockSpec((rows, d), lambda i, idx_ref: (i, 0)),
            scratch_shapes=[pltpu.SemaphoreType.DMA(())]),
        out_shape=SDS((n_out, d), f32),
        compiler_params=_cparams("arbitrary"),
        name="gather_rows",
    )(idx, src)


def _scatter_rows_kernel(idx_ref, src_ref, init_ref, o_ref, sem, *, rows):
    del init_ref
    base = pl.program_id(0) * rows

    def row_copy(j):
        return pltpu.make_async_copy(src_ref.at[pl.ds(j, 1)], o_ref.at[pl.ds(idx_ref[base + j], 1)], sem)

    def issue(j, carry):
        row_copy(j).start()
        return carry

    def drain(j, carry):
        row_copy(j).wait()
        return carry

    lax.fori_loop(0, rows, issue, 0, unroll=8)
    lax.fori_loop(0, rows, drain, 0, unroll=8)


def _scatter_rows(src, idx, n_out, rows):
    n_src, d = src.shape
    return pl.pallas_call(
        functools.partial(_scatter_rows_kernel, rows=rows),
        grid_spec=pltpu.PrefetchScalarGridSpec(
            num_scalar_prefetch=1,
            grid=(n_src // rows,),
            in_specs=[pl.BlockSpec((rows, d), lambda i, idx_ref: (i, 0)), pl.BlockSpec(memory_space=pl.ANY)],
            out_specs=pl.BlockSpec(memory_space=pl.ANY),
            scratch_shapes=[pltpu.SemaphoreType.DMA(())]),
        out_shape=SDS((n_out, d), f32),
        input_output_aliases={2: 0},
        compiler_params=_cparams("arbitrary"),
        name="scatter_rows",
    )(idx, src, jnp.zeros((n_out, d), f32))


def _pair_plan(route, n, tm):
    ea = route[0].astype(jnp.int32)
    eb = route[1].astype(jnp.int32)
    a = ea % EXP_PER_GROUP
    b = eb % EXP_PER_GROUP
    pidx = jnp.where(a == 0, b - 1, jnp.where(a == 1, b + 1, 5))
    pair = (ea // EXP_PER_GROUP) * (N_PAIRS // N_GROUPS) + pidx
    onehot = (pair[:, None] == jnp.arange(N_PAIRS, dtype=jnp.int32)[None, :]).astype(jnp.int32)
    csum = jnp.cumsum(onehot, axis=0)
    counts = csum[-1]
    ntiles = (counts + tm - 1) // tm
    tile_end = jnp.cumsum(ntiles)
    tile_start = tile_end - ntiles
    row_of_token = jnp.sum(onehot * (csum - 1 + (tile_start * tm)[None, :]), axis=1)
    nt = n // tm + N_PAIRS
    tile_id = jnp.arange(nt, dtype=jnp.int32)
    tile_valid = tile_id < tile_end[-1]
    tile_pair = jnp.sum((tile_end[None, :] <= jnp.minimum(tile_id, tile_end[-1] - 1)[:, None]).astype(jnp.int32), axis=1)
    tile_pair = jnp.minimum(tile_pair, N_PAIRS - 1)
    pick = (tile_pair[:, None] == jnp.arange(N_PAIRS, dtype=jnp.int32)[None, :]).astype(jnp.int32)
    tile_a = jnp.sum(pick * jnp.asarray(_PAIR_A, jnp.int32)[None, :], axis=1)
    tile_b = jnp.sum(pick * jnp.asarray(_PAIR_B, jnp.int32)[None, :], axis=1)
    return row_of_token, tile_a, tile_b, tile_valid.astype(jnp.int32)


def _pair_expert_kernel(ta_ref, tb_ref, tv_ref, x_ref, rwt_ref, wga_ref, wua_ref, wda_ref, wgb_ref, wub_ref, wdb_ref,
                        g_ref, b_ref, o_ref):
    i = pl.program_id(0)
    valid = tv_ref[i] == 1

    @pl.when(valid)
    def _():
        x = x_ref[...]
        xb = x.astype(bf16)
        aff_a = _logistic(jnp.sum(x * rwt_ref[pl.ds(ta_ref[i], 1), :], axis=-1, keepdims=True))
        aff_b = _logistic(jnp.sum(x * rwt_ref[pl.ds(tb_ref[i], 1), :], axis=-1, keepdims=True))
        den = aff_a + aff_b
        acc = None
        for w, (wg, wu, wd) in ((aff_a / den, (wga_ref, wua_ref, wda_ref)), (aff_b / den, (wgb_ref, wub_ref, wdb_ref))):
            h = _silu(jnp.dot(xb, wg[0], preferred_element_type=f32)) * jnp.dot(xb, wu[0], preferred_element_type=f32)
            y = jnp.dot((w * h).astype(bf16), wd[0], preferred_element_type=f32)
            acc = y if acc is None else acc + y
        o_ref[...] = _layer_norm(DN_ALPHA * x + acc, g_ref[...], b_ref[...])

    @pl.when(jnp.logical_not(valid))
    def _():
        o_ref[...] = jnp.zeros_like(o_ref)


def _pair_experts(xs, rw_t, tile_a, tile_b, tile_valid, wg, wu, wd, g_row, b_row, tm):
    rows = xs.shape[0]
    row = lambda i, ta, tb, tv: (i, 0)
    const = lambda i, ta, tb, tv: (0, 0)
    ex_a = lambda i, ta, tb, tv: (ta[i], 0, 0)
    ex_b = lambda i, ta, tb, tv: (tb[i], 0, 0)
    up = pl.BlockSpec((1, D_MODEL, D_EXPERT), ex_a), pl.BlockSpec((1, D_MODEL, D_EXPERT), ex_b)
    down = pl.BlockSpec((1, D_EXPERT, D_MODEL), ex_a), pl.BlockSpec((1, D_EXPERT, D_MODEL), ex_b)
    return pl.pallas_call(
        _pair_expert_kernel,
        grid_spec=pltpu.PrefetchScalarGridSpec(
            num_scalar_prefetch=3,
            grid=(rows // tm,),
            in_specs=[pl.BlockSpec((tm, D_MODEL), row), pl.BlockSpec((N_EXPERTS, D_MODEL), const),
                      up[0], up[0], down[0], up[1], up[1], down[1],
                      pl.BlockSpec((1, D_MODEL), const), pl.BlockSpec((1, D_MODEL), const)],
            out_specs=pl.BlockSpec((tm, D_MODEL), row)),
        out_shape=SDS((rows, D_MODEL), f32),
        compiler_params=_cparams("arbitrary"),
        name="pair_experts",
    )(tile_a, tile_b, tile_valid, xs, rw_t, wg, wu, wd, wg, wu, wd, g_row, b_row)


def _unit_lower_inverses(mats, L):
    row = lax.broadcasted_iota(jnp.int32, (L, L), 0)
    col = lax.broadcasted_iota(jnp.int32, (L, L), 1)
    eye = (row == col).astype(f32)
    ps = [eye - a for a in mats]
    pws = [a.astype(bf16) for a in mats]
    span = 2
    while span < L:
        pws = [jnp.dot(pw, pw, preferred_element_type=f32).astype(bf16) for pw in pws]
        ps = [p + jnp.dot(pw, p.astype(bf16), preferred_element_type=f32) for p, pw in zip(ps, pws)]
        span *= 2
    return ps


def _gdn_kernel(x_ref, zg_ref, ba_ref, cw_ref, alog_ref, dt_ref, nw_ref, s0_ref, cb_ref,
                o_ref, s_ref, prev_ref, *, L, BB):
    @pl.when(pl.program_id(1) == 0)
    def _():
        s_ref[...] = s0_ref[...]
        prev_ref[...] = cb_ref[...]

    def conv_silu(bi, lo, width):
        cur = x_ref[bi, :, lo:lo + width]
        cat =jnp.concatenate([prev_ref[bi, :, lo:lo + width], cur], axis=0)
        acc = cat[5:5 + L] * cw_ref[0:1, lo:lo + width]
        acc = acc + cat[6:6 + L] * cw_ref[1:2, lo:lo + width]
        acc = acc + cat[7:7 + L] * cw_ref[2:3, lo:lo + width]
        acc = acc + cur * cw_ref[3:4, lo:lo + width]
        return _silu(acc)

    def l2n(v, scale):
        return v * lax.rsqrt(jnp.sum(v * v, axis=-1, keepdims=True) + 1e-6) * scale

    row = lax.broadcasted_iota(jnp.int32, (L, L), 0)
    col = lax.broadcasted_iota(jnp.int32, (L, L), 1)
    incl = row >= col
    strict = row > col
    rep = GD_V_HEADS // GD_QK_HEADS
    bas = [ba_ref[bi] for bi in range(BB)]
    betas = [_sigmoid(ba) for ba in bas]
    gls = [-jnp.exp(alog_ref[...]) * _softplus(ba + dt_ref[...]) for ba in bas]
    gcum_all = _dot(incl.astype(f32), jnp.concatenate(gls, axis=1))
    gcums = [gcum_all[:, bi * LANES:(bi + 1) * LANES] for bi in range(BB)]
    gcum_ts = [g.T for g in gcums]
    units = [(bi, hv) for bi in range(BB) for hv in range(GD_V_HEADS)]
    qs, ks, amats, qkds, egs, g_cols, rhss = {}, {}, [], {}, {}, {}, []
    for bi in range(BB):
        for j in range(GD_QK_HEADS):
            q = l2n(conv_silu(bi, j * GD_HD, GD_HD), GD_HD ** -0.5)
            k = l2n(conv_silu(bi, GD_QK_W + j * GD_HD, GD_HD), 1.0)
            qb, kb = q.astype(bf16), k.astype(bf16)
            kk = lax.dot_general(kb, kb, (((1,), (1,)), ((), ())), preferred_element_type=f32)
            qk = lax.dot_general(qb, kb, (((1,), (1,)), ((), ())), preferred_element_type=f32)
            qs[bi, j] = qb
            ks[bi, j] = k
            for r in range(rep):
                hv = j * rep + r
                v = conv_silu(bi, 2 * GD_QK_W + hv * GD_HD, GD_HD)
                g_col = gcums[bi][:, GD_V_HEADS + hv:GD_V_HEADS + hv + 1]
                g_row = gcum_ts[bi][GD_V_HEADS + hv:GD_V_HEADS + hv + 1, :]
                b_col = betas[bi][:, hv:hv + 1]
                decay = jnp.exp(jnp.where(incl, g_col - g_row, -jnp.inf))
                eg = jnp.exp(g_col)
                amats.append(jnp.where(strict, b_col * kk * decay, 0.0))
                qkds[bi, hv] = (qk * decay).astype(bf16)
                egs[bi, hv] = eg
                g_cols[bi, hv] = g_col
                rhss.append(jnp.concatenate([b_col * v, (b_col * eg) * k], axis=-1))
    tinvs = _unit_lower_inverses(amats, L)
    sols = dict(zip(units, [_bdot(t, r) for t, r in zip(tinvs, rhss)]))
    for u in units:
        bi, hv = u
        j = hv // rep
        st = s_ref[bi, hv]
        stb = st.astype(bf16)
        sol = sols[u]
        wn = sol[:, :GD_HD] - jnp.dot(sol[:, GD_HD:].astype(bf16), stb, preferred_element_type=f32)
        wnb = wn.astype(bf16)
        o = egs[u] * jnp.dot(qs[bi, j], stb, preferred_element_type=f32) + jnp.dot(qkds[u], wnb, preferred_element_type=f32)
        g_col = g_cols[u]
        g_last = g_col[L - 1:L, :]
        kt = (jnp.exp(g_last - g_col) * ks[bi, j]).T.astype(bf16)
        s_ref[bi, hv] = jnp.exp(g_last) * st + jnp.dot(kt, wnb, preferred_element_type=f32)
        o = o * lax.rsqrt(jnp.mean(o * o, axis=-1, keepdims=True) + RMS_EPS) * nw_ref[...]
        o_ref[bi, :, hv * GD_HD:(hv + 1) * GD_HD] = o * _silu(zg_ref[bi, :, hv * GD_HD:(hv + 1) * GD_HD])
    for bi in range(BB):
        prev_ref[bi] = x_ref[bi, L - 8:L, :]


def _gdn(qkv, zg, ba, conv_w, alog_row, dt_row, norm_row, s0, conv8, B, T, L):
    nc = T // L
    n = B * T
    bb = 2
    tok = lambda b, c: (b, c, 0)
    const = lambda b, c: (0, 0)
    st4 = lambda b, c: (b, 0, 0, 0)
    o, s_out = pl.pallas_call(
        functools.partial(_gdn_kernel, L=L, BB=bb),
        grid=(B // bb, nc),
        in_specs=[pl.BlockSpec((bb, L, GD_CONV_CH), tok),
                  pl.BlockSpec((bb, L, GD_V_W), tok),
                  pl.BlockSpec((bb, L, LANES), tok),
                  pl.BlockSpec((GD_CONV, GD_CONV_CH), const),
                  pl.BlockSpec((1, LANES), const),
                  pl.BlockSpec((1, LANES), const),
                  pl.BlockSpec((1, GD_HD), const),
                  pl.BlockSpec((bb, GD_V_HEADS, GD_HD, GD_HD), st4),
                  pl.BlockSpec((bb, 8, GD_CONV_CH), lambda b, c: (b, 0, 0))],
        out_specs=[pl.BlockSpec((bb, L, GD_V_W), tok),
                   pl.BlockSpec((bb, GD_V_HEADS, GD_HD, GD_HD), st4)],
        out_shape=[SDS((B, T, GD_V_W), f32), SDS((B, GD_V_HEADS, GD_HD, GD_HD), f32)],
        scratch_shapes=[pltpu.VMEM((bb, 8, GD_CONV_CH), f32)],
        compiler_params=_cparams("parallel", "arbitrary"),
        name="gdn",
    )(qkv.reshape(B, T, GD_CONV_CH), zg.reshape(B, T, GD_V_W), ba.reshape(B, T, LANES), conv_w, alog_row, dt_row,
      norm_row, s0, conv8)
    return o.reshape(n, GD_V_W), s_out


def _pad_lanes(row, offset=0):
    return jnp.zeros((1, LANES), f32).at[0, offset:offset + row.shape[0]].set(row.astype(f32))


def _rope_tables(pos):
    half = SW_HD // 2
    inv = ROPE_THETA ** (-jnp.arange(half, dtype=f32) / half)
    ang = pos.astype(f32)[:, None] * inv[None, :]
    cos, sin = jnp.cos(ang), jnp.sin(ang)
    cos_t = jnp.concatenate([cos, cos] * SW_KV_HEADS, axis=-1)
    sin_t = jnp.concatenate([-sin, sin] * SW_KV_HEADS, axis=-1)
    return cos_t, sin_t


def _tile(n, pref):
    return pref if n % pref == 0 else n


def _trunk(x, pos, L, state, p):
    B, T, _ = x.shape
    n = B * T
    x2 = x.reshape(n, D_MODEL)
    tm = _tile(n, 512)
    precise = state is not None
    ab_w, wo_h, wo_a = (p["ab_w32"], p["ab_wo_h32"], p["ab_wo_a32"]) if precise else (p["ab_w"], p["ab_wo_h"], p["ab_wo_a"])

    z_ml, z_sw, z_g = _proj(x2, ab_w, ((0, ML_W), (ML_W, SW_W), (ML_W + SW_W, LANES)), tm)
    if state is None:
        h_ml, ml_c, ml_n, ml_m = _mlstm_pairs(z_ml, z_g, p["ab_bias"], p["ab_norm"], B, T, L)
    else:
        cn0 = jnp.concatenate([state["ml_C"], state["ml_n"][..., None],
                               jnp.zeros((B, ML_HEADS, ML_DK, LANES - ML_DV - 1), f32)], axis=-1)
        m0 = jnp.zeros((B, 1, LANES), f32).at[:, 0, :ML_HEADS].set(state["ml_m"])
        h_ml, cn, m_out = _mlstm(z_ml, z_g, p["ab_bias"], p["ab_norm"], cn0, m0, B, T, L, precise)
        ml_c, ml_n, ml_m = cn[..., :ML_DV], cn[..., ML_DV], m_out[:, 0, :ML_HEADS]
    h_ml = h_ml.reshape(n, ML_HEADS * ML_DV)
    cos_t, sin_t = _rope_tables(pos)
    if state is None:
        a_sw, k_rot = _swa_prompt(z_sw, cos_t, sin_t, p["ab_sinks"], B, T, L)
    else:
        a_sw, k_rot = _swa_sample(z_sw, state["sw_k"].reshape(B, WINDOW, LANES),
                                  state["sw_v"].reshape(B, WINDOW, LANES), cos_t, sin_t, p["ab_sinks"], B, T)
    keep = min(T, WINDOW)
    new_k = k_rot.reshape(B, T, LANES)[:, T - keep:].reshape(B, keep, SW_KV_HEADS, SW_HD)
    new_v = z_sw.reshape(B, T, SW_W)[:, T - keep:, SW_HEADS * SW_HD + LANES:].reshape(B, keep, SW_KV_HEADS, SW_HD)
    x2, gates_t, route = _out_ln(x2, [h_ml, a_sw], [wo_h, wo_a], p["ln_g"][0][0], p["ln_b"][0][0],
                                 p["router_wt"], p["router_b"], tm)
    x2 = _moe_block(x2, gates_t, route, p, 0)

    tm1 = _tile(n, 256)
    if state is None:
        s0 = jnp.zeros((B, GD_V_HEADS, GD_HD, GD_HD), f32)
        conv8 = jnp.zeros((B, 8, GD_CONV_CH), f32)
    else:
        s0 = state["gd_S"]
        conv8 = jnp.concatenate([jnp.zeros((B, 8 - (GD_CONV - 1), GD_CONV_CH), f32), state["gd_conv"]], axis=1)
    qkv, zg, ba = _proj(x2, p["c_w"], ((0, GD_CONV_CH), (GD_CONV_CH, GD_V_W), (GD_CONV_CH + GD_V_W, LANES)), tm1)
    new_conv = qkv.reshape(B, T, GD_CONV_CH)[:, T - (GD_CONV - 1):]
    o_gd, s_out = _gdn(qkv, zg, ba, p["c_conv_w"], p["c_alog"], p["c_dt"], p["c_norm"], s0, conv8, B, T, L)
    x2, gates_t, route = _out_ln(x2, [o_gd], [p["c_wo"]], p["ln_g"][1][0], p["ln_b"][1][0],
                                 p["router_wt"], p["router_b"], tm)
    x2 = _moe_block(x2, gates_t, route, p, 1)

    outs = (new_k[None], new_v[None], ml_c[None], ml_n[None], ml_m[None],
            s_out[None], new_conv[None])
    return x2.reshape(B, T, D_MODEL), outs


def _moe_block(x2, gates_t, route, p, layer):
    n = x2.shape[0]
    wg, wu, wd = p["ex_gate"][layer], p["ex_up"][layer], p["ex_down"][layer]
    g_row, b_row = p["ln_g"][layer][1], p["ln_b"][layer][1]
    if n < N_PAIRS * MOE_TM:
        return _moe_ln(x2, gates_t.T, wg, wu, wd, g_row, b_row, _tile(n, 1024))
    row_of_token, tile_a, tile_b, tile_valid = _pair_plan(route, n, MOE_TM)
    xs = _scatter_rows(x2, row_of_token, n + N_PAIRS * MOE_TM, MOE_TM)
    ys = _pair_experts(xs, p["router_wt"], tile_a, tile_b, tile_valid, wg, wu, wd, g_row, b_row, MOE_TM)
    return _gather_rows(ys, row_of_token, MOE_TM)


def kernel(x_prompt, x_sample, cache_swa_k, cache_swa_v, state_mlstm_C, state_mlstm_n, state_mlstm_m, state_gdn_S, state_gdn_conv, ab_w_in, ab_b_i, ab_b_f, ab_norm, ab_sinks, ab_w_out, c_w_in, c_conv_w, c_a_log, c_dt_bias, c_norm, c_w_out, ln_g, ln_b, router_w, router_b, ex_gate, ex_up, ex_down):
    gate_lo = ML_W
    sw_lo = ML_W + 2 * ML_HEADS
    w0 = ab_w_in[0]
    ab_w32 = jnp.concatenate([w0[:, :gate_lo], w0[:, sw_lo:], w0[:, gate_lo:sw_lo],
                              jnp.zeros((D_MODEL, LANES - 2 * ML_HEADS), f32)], axis=1)
    w1 = c_w_in[0]
    c_w = jnp.concatenate([w1, jnp.zeros((D_MODEL, LANES - 2 * GD_V_HEADS), f32)], axis=1).astype(bf16)
    wo32 = ab_w_out[0]
    wo = wo32.astype(bf16)
    p = {
        "ab_w": ab_w32.astype(bf16),
        "ab_w32": ab_w32,
        "ab_wo_h32": wo32[:ML_HEADS * ML_DV],
        "ab_wo_a32": wo32[ML_HEADS * ML_DV:],
        "ab_bias": _pad_lanes(jnp.concatenate([ab_b_i[0], ab_b_f[0]])),
        "ab_norm": ab_norm[0].reshape(1, ML_HEADS * ML_DV),
        "ab_sinks": _pad_lanes(ab_sinks[0]),
        "ab_wo_h": wo[:ML_HEADS * ML_DV],
        "ab_wo_a": wo[ML_HEADS * ML_DV:],
        "c_w": c_w,
        "c_conv_w": c_conv_w[0],
        "c_alog": _pad_lanes(c_a_log[0], GD_V_HEADS),
        "c_dt": _pad_lanes(c_dt_bias[0], GD_V_HEADS),
        "c_norm": c_norm[0].reshape(1, GD_HD),
        "c_wo": c_w_out[0].astype(bf16),
        "ln_g": [[ln_g[i, j].reshape(1, D_MODEL) for j in range(2)] for i in range(DEPTH)],
        "ln_b": [[ln_b[i, j].reshape(1, D_MODEL) for j in range(2)] for i in range(DEPTH)],
        "router_wt": router_w.T,
        "router_b": router_b.reshape(N_EXPERTS, 1),
        "ex_gate": [ex_gate[i].astype(bf16) for i in range(DEPTH)],
        "ex_up": [ex_up[i].astype(bf16) for i in range(DEPTH)],
        "ex_down": [ex_down[i].astype(bf16) for i in range(DEPTH)],
    }
    t_p = x_prompt.shape[1]
    y_p, st_p = _trunk(x_prompt, jnp.arange(t_p, dtype=jnp.int32), CHUNK, None, p)
    t_s = x_sample.shape[1]
    state = {"sw_k": cache_swa_k[0], "sw_v": cache_swa_v[0], "ml_C": state_mlstm_C[0], "ml_n": state_mlstm_n[0],
             "ml_m": state_mlstm_m[0], "gd_S": state_gdn_S[0], "gd_conv": state_gdn_conv[0]}
    y_s, st_s = _trunk(x_sample, PAST_LEN + jnp.arange(t_s, dtype=jnp.int32), t_s, state, p)
    return (y_p, y_s) + st_p + st_s
```

```python
import functools
import math

import jax
import jax.numpy as jnp
import numpy as np
from jax import lax
from jax.experimental import pallas as pl
from jax.experimental.pallas import tpu as pltpu

f32 = jnp.float32
bf16 = jnp.bfloat16
HIGHEST = lax.Precision.HIGHEST

D_MODEL = 1024
DEPTH = 2
CHUNK = 64
PAST_LEN = 2048
ML_HEADS = 8
ML_DK = 64
ML_DV = 64
SW_HEADS = 8
SW_KV_HEADS = 2
SW_HD = 64
SW_GROUP = SW_HEADS // SW_KV_HEADS
WINDOW = 128
ROPE_THETA = 10000.0
GD_QK_HEADS = 8
GD_V_HEADS = 16
GD_HD = 128
GD_CONV = 4
GD_QK_W = GD_QK_HEADS * GD_HD
GD_V_W = GD_V_HEADS * GD_HD
GD_CONV_CH = 2 * GD_QK_W + GD_V_W
N_EXPERTS = 16
N_GROUPS = 4
EXP_PER_GROUP = 4
D_EXPERT = 512
DN_ALPHA = (2 * DEPTH) ** 0.25
LN_EPS = 1e-5
RMS_EPS = 1e-6

LANES = 128
ML_W = 4 * ML_HEADS * ML_DK
SW_W = SW_HEADS * SW_HD + 2 * SW_KV_HEADS * SW_HD
VMEM_LIMIT = 56 * 1024 * 1024

SDS = jax.ShapeDtypeStruct


def _cparams(*sem):
    return pltpu.CompilerParams(dimension_semantics=sem, vmem_limit_bytes=VMEM_LIMIT)


def _dot(a, b):
    return jnp.dot(a, b, preferred_element_type=f32, precision=HIGHEST)


def _dot_nt(a, b):
    return lax.dot_general(a, b, (((1,), (1,)), ((), ())), preferred_element_type=f32, precision=HIGHEST)


def _bdot(a, b):
    return jnp.dot(a.astype(bf16), b.astype(bf16), preferred_element_type=f32)


def _bdot_nt(a, b):
    return lax.dot_general(a.astype(bf16), b.astype(bf16), (((1,), (1,)), ((), ())), preferred_element_type=f32)


def _bdot_tn(a, b):
    return jnp.dot(a.T.astype(bf16), b.astype(bf16), preferred_element_type=f32)


def _dots(precise):
    if precise:
        return _dot, _dot_nt, lambda a, b: _dot(a.T, b)
    return _bdot, _bdot_nt, _bdot_tn


def _sigmoid(x):
    return 0.5 + 0.5 * jnp.tanh(0.5 * x)


def _silu(x):
    hx = 0.5 * x
    return hx + hx * jnp.tanh(hx)


def _softplus(x):
    return jnp.maximum(x, 0.0) + jnp.log(1.0 + jnp.exp(-jnp.abs(x)))


def _layer_norm(v, g, b):
    mu = jnp.mean(v, axis=-1, keepdims=True)
    d = v - mu
    var = jnp.mean(d * d, axis=-1, keepdims=True)
    return d * lax.rsqrt(var + LN_EPS) * g + b


def _proj_kernel(x_ref, w_ref, *o_refs, splits, col_chunk, precise):
    mm = _dots(precise)[0]
    xb = x_ref[...] if precise else x_ref[...].astype(bf16)
    for o_ref, (start, width) in zip(o_refs, splits):
        for c in range(0, width, col_chunk):
            cw = min(col_chunk, width - c)
            o_ref[:, c:c + cw] = mm(xb, w_ref[:, start + c:start + c + cw])


def _proj(x2, w, splits, tm):
    n, k = x2.shape
    return pl.pallas_call(
        functools.partial(_proj_kernel, splits=splits, col_chunk=512, precise=w.dtype == f32),
        grid=(n // tm,),
        in_specs=[pl.BlockSpec((tm, k), lambda i: (i, 0)),
                  pl.BlockSpec(w.shape, lambda i: (0, 0), pipeline_mode=pl.Buffered(1))],
        out_specs=[pl.BlockSpec((tm, wd), lambda i: (i, 0)) for _, wd in splits],
        out_shape=[SDS((n, wd), f32) for _, wd in splits],
        compiler_params=_cparams("parallel"),
        name="in_proj",
    )(x2, w)


def _mlstm_kernel(z_ref, g_ref, bias_ref, nw_ref, cn0_ref, m0_ref, h_ref, cn_ref, m_ref, *, L, BB, precise):
    @pl.when(pl.program_id(1) == 0)
    def _():
        cn_ref[...] = cn0_ref[...]
        m_ref[...] = m0_ref[...]

    mm, mm_nt, mm_tn = _dots(precise)
    row = lax.broadcasted_iota(jnp.int32, (L, L), 0)
    col = lax.broadcasted_iota(jnp.int32, (L, L), 1)
    causal = row >= col
    tri = causal.astype(f32)
    lane = lax.broadcasted_iota(jnp.int32, (1, LANES), 1)
    lane_l = lax.broadcasted_iota(jnp.int32, (L, ML_DV), 1)
    one_hot0 = (lane_l == 0).astype(f32)
    for bi in range(BB):
        g = g_ref[bi] + bias_ref[...]
        lf = jnp.minimum(g, 0.0) - jnp.log(1.0 + jnp.exp(-jnp.abs(g)))
        bcum = _dot(tri, lf)
        b_t = bcum.T
        g_t = g.T
        m_row = m_ref[bi]
        new_m = m_row
        outs = []
        for h in range(ML_HEADS):
            b_col = bcum[:, ML_HEADS + h:ML_HEADS + h + 1]
            b_row = b_t[ML_HEADS + h:ML_HEADS + h + 1, :]
            ig_row = g_t[h:h + 1, :]
            ig_col = g[:, h:h + 1]
            m_h = m_row[:, h:h + 1]
            dmat = jnp.where(causal, b_col - b_row + ig_row, -jnp.inf)
            inter = b_col + m_h
            mt = jnp.maximum(inter, jnp.max(dmat, axis=-1, keepdims=True))
            a = jnp.exp(inter - mt)
            q = z_ref[bi, :, h * ML_DK:(h + 1) * ML_DK]
            k = z_ref[bi, :, ML_HEADS * ML_DK + h * ML_DK:ML_HEADS * ML_DK + (h + 1) * ML_DK] * (ML_DK ** -0.5)
            v = z_ref[bi, :, 2 * ML_HEADS * ML_DK + h * ML_DV:2 * ML_HEADS * ML_DK + (h + 1) * ML_DV]
            og = z_ref[bi, :, 3 * ML_HEADS * ML_DK + h * ML_DV:3 * ML_HEADS * ML_DK + (h + 1) * ML_DV]
            s = mm_nt(q, k) * jnp.exp(dmat - mt)
            vext = jnp.concatenate([v, one_hot0], axis=-1)
            cn = cn_ref[bi, h]
            tot = a * mm(q, cn) + mm(s, vext)
            num = tot[:, :ML_DV]
            den = tot[:, ML_DV:ML_DV + 1]
            hh = num / jnp.maximum(jnp.abs(den), jnp.exp(-mt))
            hh = hh * lax.rsqrt(jnp.mean(hh * hh, axis=-1, keepdims=True) + RMS_EPS) * nw_ref[:, h * ML_DV:(h + 1) * ML_DV]
            outs.append(hh * _sigmoid(og))
            m_new = mt[L - 1:L, :]
            b_last = b_col[L - 1:L, :]
            wk = jnp.exp(b_last - b_col + ig_col - m_new)
            dec = jnp.exp(b_last + m_h - m_new)
            cn_ref[bi, h] = dec * cn + mm_tn(k, wk * vext)
            new_m = jnp.where(lane == h, m_new, new_m)
        m_ref[bi] = new_m
        h_ref[bi] = jnp.concatenate(outs, axis=-1)


def _mlstm(z_ml, z_g, bias_row, norm_row, cn0, m0, B, T, L, precise):
    nc = T // L
    bb = min(B, 4)
    tok = lambda b, c: (b, c, 0)
    st4 = lambda b, c: (b, 0, 0, 0)
    st3 = lambda b, c: (b, 0, 0)
    return pl.pallas_call(
        functools.partial(_mlstm_kernel, L=L, BB=bb, precise=precise),
        grid=(B // bb, nc),
        in_specs=[pl.BlockSpec((bb, L, ML_W), tok),
                  pl.BlockSpec((bb, L, LANES), tok),
                  pl.BlockSpec((1, LANES), lambda b, c: (0, 0)),
                  pl.BlockSpec((1, ML_HEADS * ML_DV), lambda b, c: (0, 0)),
                  pl.BlockSpec((bb, ML_HEADS, ML_DK, LANES), st4),
                  pl.BlockSpec((bb, 1, LANES), st3)],
        out_specs=[pl.BlockSpec((bb, L, ML_HEADS * ML_DV), tok),
                   pl.BlockSpec((bb, ML_HEADS, ML_DK, LANES), st4),
                   pl.BlockSpec((bb, 1, LANES), st3)],
        out_shape=[SDS((B, T, ML_HEADS * ML_DV), f32),
                   SDS((B, ML_HEADS, ML_DK, LANES), f32),
                   SDS((B, 1, LANES), f32)],
        compiler_params=_cparams("parallel", "arbitrary"),
        name="mlstm",
    )(z_ml.reshape(B, T, ML_W), z_g.reshape(B, T, LANES), bias_row, norm_row, cn0, m0)


ML_PAIRS = ML_HEADS // 2
ML_REP_QUANTS = 3


def _pair_select_matrix():
    sel = np.zeros((LANES, ML_REP_QUANTS * ML_PAIRS * LANES), np.float32)
    for qn in range(ML_REP_QUANTS):
        for pr in range(ML_PAIRS):
            for half in range(2):
                lo = (qn * ML_PAIRS + pr) * LANES + half * ML_DV
                sel[ML_HEADS * qn + 2 * pr + half, lo:lo + ML_DV] = 1.0
    return jnp.asarray(sel, bf16)


def _exact_select(x, sel):
    hi = x.astype(bf16)
    r1 = x - hi.astype(f32)
    mid = r1.astype(bf16)
    lo = (r1 - mid.astype(f32)).astype(bf16)
    mm = lambda t: jnp.dot(t, sel, preferred_element_type=f32)
    return (mm(hi) + mm(mid)) + mm(lo)


def _mlstm_pair_kernel(z_ref, g_ref, bias_ref, nw_ref, sel_ref, cbd0_ref, nbd0_ref, m0_ref,
                       h_ref, cbd_ref, nbd_ref, m_ref, *, L, BB):
    @pl.when(pl.program_id(1) == 0)
    def _():
        cbd_ref[...] = cbd0_ref[...]
        nbd_ref[...] = nbd0_ref[...]
        m_ref[...] = m0_ref[...]

    tri = (lax.broadcasted_iota(jnp.int32, (L, L), 0) >= lax.broadcasted_iota(jnp.int32, (L, L), 1)).astype(f32)
    row_t = lax.broadcasted_iota(jnp.int32, (L, LANES), 0)
    lane_t = lax.broadcasted_iota(jnp.int32, (L, LANES), 1)
    first_half = lane_t < ML_DV
    causal2 = row_t >= (lane_t % ML_DV)
    rr = lax.broadcasted_iota(jnp.int32, (LANES, LANES), 0)
    cc = lax.broadcasted_iota(jnp.int32, (LANES, LANES), 1)
    same_block = (rr < ML_DV) == (cc < ML_DV)
    ones_bd = same_block.astype(bf16)
    lane_1 = lax.broadcasted_iota(jnp.int32, (1, LANES), 1)
    sel = sel_ref[...]
    n_tiles = ML_REP_QUANTS * ML_PAIRS

    gs = [g_ref[bi] + bias_ref[...] for bi in range(BB)]
    lfs = [jnp.minimum(g, 0.0) - jnp.log(1.0 + jnp.exp(-jnp.abs(g))) for g in gs]
    bc_all = _dot(tri, jnp.concatenate(lfs, axis=1))
    g_ts = [g.T for g in gs]
    b_ts = [bc_all[:, bi * LANES:(bi + 1) * LANES].T for bi in range(BB)]
    r_rows = [g_ts[bi][0:ML_HEADS] - b_ts[bi][ML_HEADS:2 * ML_HEADS] for bi in range(BB)]
    cm = jnp.concatenate([jnp.concatenate(r_rows, axis=0), jnp.full((BB * ML_HEADS, LANES - L), -jnp.inf, f32)], axis=1)
    shift = 1
    while shift < L:
        cm = jnp.maximum(cm, pltpu.roll(cm, shift, 1))
        shift *= 2
    cols = [jnp.concatenate([g_ts[bi][0:ML_HEADS], b_ts[bi][ML_HEADS:2 * ML_HEADS],
                             cm[bi * ML_HEADS:(bi + 1) * ML_HEADS, :L],
                             jnp.zeros((LANES - 3 * ML_HEADS, L), f32)], axis=0).T for bi in range(BB)]
    rep_all = _exact_select(jnp.concatenate(cols, axis=0), sel)
    rep = [rep_all[bi * L:(bi + 1) * L] for bi in range(BB)]
    m_all = jnp.concatenate([m_ref[bi] for bi in range(BB)] + [jnp.zeros((8 - BB, LANES), f32)], axis=0)
    m_rep_all = _exact_select(m_all, sel[:, :ML_PAIRS * LANES])
    m_rep = [m_rep_all[bi:bi + 1] for bi in range(BB)]

    units = [(bi, pr) for bi in range(BB) for pr in range(ML_PAIRS)]
    st = {}
    for u in units:
        bi, pr = u
        tile = lambda qn: rep[bi][:, (qn * ML_PAIRS + pr) * LANES:(qn * ML_PAIRS + pr + 1) * LANES]
        ig_rep, b_rep, cm_rep = tile(0), tile(1), tile(2)
        m_pair = m_rep[bi][:, pr * LANES:(pr + 1) * LANES]
        inter = b_rep + m_pair
        mt = jnp.maximum(inter, b_rep + cm_rep)
        r_row = jnp.concatenate([r_rows[bi][2 * pr:2 * pr + 1], r_rows[bi][2 * pr + 1:2 * pr + 2]], axis=1)
        e = jnp.exp(jnp.where(causal2, (b_rep - mt) + r_row, -jnp.inf))
        m_new = mt[L - 1:L]
        b_last = b_rep[L - 1:L]
        lo = pr * LANES
        q = z_ref[bi, :, lo:lo + LANES].astype(bf16)
        k = z_ref[bi, :, ML_HEADS * ML_DK + lo:ML_HEADS * ML_DK + lo + LANES] * (ML_DK ** -0.5)
        v = z_ref[bi, :, 2 * ML_HEADS * ML_DK + lo:2 * ML_HEADS * ML_DK + lo + LANES]
        kbd = jnp.concatenate([jnp.where(first_half, k, 0.0), jnp.where(first_half, 0.0, k)], axis=0).astype(bf16)
        vbd = jnp.concatenate([jnp.where(first_half, v, 0.0), jnp.where(first_half, 0.0, v)], axis=0).astype(bf16)
        wk = jnp.exp(b_last - b_rep + ig_rep - m_new)
        st[u] = dict(a=jnp.exp(inter - mt), em=jnp.exp(-mt), e=e, m_new=m_new, dec=jnp.exp(b_last + m_pair - m_new),
                     q=q, kbd=kbd, vbd=vbd, k_t=k.T.astype(bf16), wkv=(wk * v).astype(bf16), wk=wk.astype(bf16),
                     cbd=cbd_ref[bi, pr], nbd=nbd_ref[bi, pr])
    for u in units:
        d = st[u]
        d["qk"] = lax.dot_general(d["q"], d["kbd"], (((1,), (1,)), ((), ())), preferred_element_type=f32)
        d["qc"] = jnp.dot(d["q"], d["cbd"].astype(bf16), preferred_element_type=f32)
        d["qn"] = jnp.dot(d["q"], d["nbd"].astype(bf16), preferred_element_type=f32)
    for u in units:
        d = st[u]
        s = (d["qk"] * d["e"]).astype(bf16)
        num = d["a"] * d["qc"] + jnp.dot(s, d["vbd"], preferred_element_type=f32)
        den = d["a"] * d["qn"] + jnp.dot(s, ones_bd, preferred_element_type=f32)
        d["hh"] = num / jnp.maximum(jnp.abs(den), d["em"])
    for u in units:
        bi, pr = u
        d = st[u]
        sq = d["hh"] * d["hh"]
        sq_hi = sq.astype(bf16)
        sq_lo = (sq - sq_hi.astype(f32)).astype(bf16)
        ms = (jnp.dot(sq_hi, ones_bd, preferred_element_type=f32)
              + jnp.dot(sq_lo, ones_bd, preferred_element_type=f32)) * (1.0 / ML_DV)
        lo = pr * LANES
        og = z_ref[bi, :, 3 * ML_HEADS * ML_DK + lo:3 * ML_HEADS * ML_DK + lo + LANES]
        h_ref[bi, :, lo:lo + LANES] = d["hh"] * lax.rsqrt(ms + RMS_EPS) * nw_ref[:, lo:lo + LANES] * _sigmoid(og)
    for u in units:
        bi, pr = u
        d = st[u]
        cbd_ref[bi, pr] = d["dec"] * d["cbd"] + jnp.where(
            same_block, jnp.dot(d["k_t"], d["wkv"], preferred_element_type=f32), 0.0)
        nbd_ref[bi, pr] = d["dec"] * d["nbd"] + jnp.where(
            same_block, jnp.dot(d["k_t"], d["wk"], preferred_element_type=f32), 0.0)
    for bi in range(BB):
        new_m = m_ref[bi]
        for pr in range(ML_PAIRS):
            m_new = st[(bi, pr)]["m_new"]
            new_m = jnp.where(lane_1 == 2 * pr, m_new[:, 0:1], new_m)
            new_m = jnp.where(lane_1 == 2 * pr + 1, m_new[:, ML_DV:ML_DV + 1], new_m)
        m_ref[bi] = new_m


def _mlstm_pairs(z_ml, z_g, bias_row, norm_row, B, T, L):
    assert 2 * L == LANES and ML_DK == ML_DV == L
    nc = T // L
    bb = min(B, 4)
    tok = lambda b, c: (b, c, 0)
    st4 = lambda b, c: (b, 0, 0, 0)
    st3 = lambda b, c: (b, 0, 0)
    const = lambda b, c: (0, 0)
    sel = _pair_select_matrix()
    zeros_bd = jnp.zeros((B, ML_PAIRS, LANES, LANES), f32)
    h, cbd, nbd, m = pl.pallas_call(
        functools.partial(_mlstm_pair_kernel, L=L, BB=bb),
        grid=(B // bb, nc),
        in_specs=[pl.BlockSpec((bb, L, ML_W), tok),
                  pl.BlockSpec((bb, L, LANES), tok),
                  pl.BlockSpec((1, LANES), const),
                  pl.BlockSpec((1, ML_HEADS * ML_DV), const),
                  pl.BlockSpec(sel.shape, const),
                  pl.BlockSpec((bb, ML_PAIRS, LANES, LANES), st4),
                  pl.BlockSpec((bb, ML_PAIRS, LANES, LANES), st4),
                  pl.BlockSpec((bb, 1, LANES), st3)],
        out_specs=[pl.BlockSpec((bb, L, ML_HEADS * ML_DV), tok),
                   pl.BlockSpec((bb, ML_PAIRS, LANES, LANES), st4),
                   pl.BlockSpec((bb, ML_PAIRS, LANES, LANES), st4),
                   pl.BlockSpec((bb, 1, LANES), st3)],
        out_shape=[SDS((B, T, ML_HEADS * ML_DV), f32),
                   SDS((B, ML_PAIRS, LANES, LANES), f32),
                   SDS((B, ML_PAIRS, LANES, LANES), f32),
                   SDS((B, 1, LANES), f32)],
        compiler_params=_cparams("parallel", "arbitrary"),
        name="mlstm_pairs",
    )(z_ml.reshape(B, T, ML_W), z_g.reshape(B, T, LANES), bias_row, norm_row, sel, zeros_bd, zeros_bd,
      jnp.zeros((B, 1, LANES), f32))
    c_out = jnp.stack([cbd[:, :, :ML_DK, :ML_DV], cbd[:, :, ML_DK:, ML_DV:]], axis=2).reshape(B, ML_HEADS, ML_DK, ML_DV)
    n_out = jnp.stack([nbd[:, :, :ML_DK, 0], nbd[:, :, ML_DK:, ML_DV]], axis=2).reshape(B, ML_HEADS, ML_DK)
    return h, c_out, n_out, m[:, 0, :ML_HEADS]


def _rope(x, cos, sin_signed):
    w = x.shape[-1]
    lane = lax.broadcasted_iota(jnp.int32, x.shape, 1)
    swapped = jnp.where((lane % SW_HD) < SW_HD // 2, pltpu.roll(x, w - SW_HD // 2, 1), pltpu.roll(x, SW_HD // 2, 1))
    return x * cos + swapped * sin_signed


def _swa_attend(jobs, sinks_ref, L, precise=False):
    mm, mm_nt, _ = _dots(precise)
    units = [(j, g) for j in range(len(jobs)) for g in range(SW_KV_HEADS)]
    sinks = [jnp.concatenate(
        [jnp.broadcast_to(sinks_ref[:, g * SW_GROUP + i:g * SW_GROUP + i + 1], (L, 1)) for i in range(SW_GROUP)],
        axis=0) for g in range(SW_KV_HEADS)]
    s, p = {}, {}
    for j, g in units:
        qr, keys, _, _ = jobs[j]
        q4 = jnp.concatenate([qr[:, (g * SW_GROUP + i) * SW_HD:(g * SW_GROUP + i + 1) * SW_HD]
                              for i in range(SW_GROUP)], axis=0)
        s[j, g] = mm_nt(q4, keys[:, g * SW_HD:(g + 1) * SW_HD]) * (SW_HD ** -0.5)
    for j, g in units:
        first_valid = jobs[j][3]
        sc = s[j, g]
        if first_valid is not None:
            kcol = lax.broadcasted_iota(jnp.int32, (1, sc.shape[1]), 1)
            sc = jnp.where(kcol >= first_valid, sc, -jnp.inf)
        mx = jnp.maximum(jnp.max(sc, axis=-1, keepdims=True), sinks[g])
        e = jnp.exp(sc - mx)
        p[j, g] = e / (jnp.sum(e, axis=-1, keepdims=True) + jnp.exp(sinks[g] - mx))
    o = {u: mm(p[u], jobs[u[0]][2][:, u[1] * SW_HD:(u[1] + 1) * SW_HD]) for u in units}
    return [jnp.concatenate([o[j, g][i * L:(i + 1) * L, :] for g in range(SW_KV_HEADS) for i in range(SW_GROUP)],
                            axis=-1) for j in range(len(jobs))]


def _swa_prompt_kernel(q_ref, kp_ref, kc_ref, vp_ref, vc_ref, cp_ref, cc_ref, sp_ref, sc_ref, sinks_ref,
                       o_ref, kr_ref, *, L, CB):
    i = pl.program_id(1)
    rows = CB * L
    back = 2 * L
    cos_q = jnp.concatenate([cc_ref[...]] * (SW_HEADS // SW_KV_HEADS), axis=-1)
    sin_q = jnp.concatenate([sc_ref[...]] * (SW_HEADS // SW_KV_HEADS), axis=-1)
    qr = _rope(q_ref[...], cos_q, sin_q)
    k_cur = _rope(kc_ref[...], cc_ref[...], sc_ref[...])
    kr_ref[...] = k_cur
    k_prev = _rope(kp_ref[rows - back:rows, :], cp_ref[rows - back:rows, :], sp_ref[rows - back:rows, :])
    keys = jnp.concatenate([k_prev, k_cur], axis=0)
    vals = jnp.concatenate([vp_ref[rows - back:rows, :], vc_ref[...]], axis=0)
    jobs = []
    for u in range(CB):
        first_valid = jnp.where(i == 0, back - u * L, 0) if u * L < back else None
        jobs.append((qr[u * L:(u + 1) * L], keys[u * L:(u + 3) * L], vals[u * L:(u + 3) * L], first_valid))
    o_ref[...] = jnp.concatenate(_swa_attend(jobs, sinks_ref, L), axis=0)


def _swa_prompt(z_sw, cos_t, sin_t, sinks_row, B, T, L):
    cb = 4
    rows = cb * L
    nb = T // rows
    n = B * T
    kcol = SW_HEADS * SW_HD // LANES
    vcol = kcol + 1
    cur = lambda b, i: (b * nb + i, 0)
    prev = lambda col: (lambda b, i: (b * nb + jnp.maximum(i - 1, 0), col))
    curc = lambda col: (lambda b, i: (b * nb + i, col))
    tab_cur = lambda b, i: (i, 0)
    tab_prev = lambda b, i: (jnp.maximum(i - 1, 0), 0)
    return pl.pallas_call(
        functools.partial(_swa_prompt_kernel, L=L, CB=cb),
        grid=(B, nb),
        in_specs=[pl.BlockSpec((rows, SW_HEADS * SW_HD), cur),
                  pl.BlockSpec((rows, LANES), prev(kcol)), pl.BlockSpec((rows, LANES), curc(kcol)),
                  pl.BlockSpec((rows, LANES), prev(vcol)), pl.BlockSpec((rows, LANES), curc(vcol)),
                  pl.BlockSpec((rows, LANES), tab_prev), pl.BlockSpec((rows, LANES), tab_cur),
                  pl.BlockSpec((rows, LANES), tab_prev), pl.BlockSpec((rows, LANES), tab_cur),
                  pl.BlockSpec((1, LANES), lambda b, i: (0, 0))],
        out_specs=[pl.BlockSpec((rows, SW_HEADS * SW_HD), cur),
                   pl.BlockSpec((rows, LANES), cur)],
        out_shape=[SDS((n, SW_HEADS * SW_HD), f32), SDS((n, LANES), f32)],
        compiler_params=_cparams("parallel", "parallel"),
        name="swa_prompt",
    )(z_sw, z_sw, z_sw, z_sw, z_sw, cos_t, cos_t, sin_t, sin_t, sinks_row)


def _swa_sample_kernel(q_ref, k_ref, v_ref, ck_ref, cv_ref, cos_ref, sin_ref, sinks_ref, o_ref, kr_ref, *, L):
    cos_q = jnp.concatenate([cos_ref[...]] * (SW_HEADS // SW_KV_HEADS), axis=-1)
    sin_q = jnp.concatenate([sin_ref[...]] * (SW_HEADS // SW_KV_HEADS), axis=-1)
    qr = _rope(q_ref[...], cos_q, sin_q)
    kr = _rope(k_ref[...], cos_ref[...], sin_ref[...])
    kr_ref[...] = kr
    keys = jnp.concatenate([ck_ref[0], kr], axis=0)
    vals = jnp.concatenate([cv_ref[0], v_ref[...]], axis=0)
    o_ref[...] = _swa_attend([(qr, keys, vals, None)], sinks_ref, L, precise=True)[0]


def _swa_sample(z_sw, cache_k, cache_v, cos_t, sin_t, sinks_row, B, T):
    n = B * T
    kcol = SW_HEADS * SW_HD // LANES
    return pl.pallas_call(
        functools.partial(_swa_sample_kernel, L=T),
        grid=(B,),
        in_specs=[pl.BlockSpec((T, SW_HEADS * SW_HD), lambda b: (b, 0)),
                  pl.BlockSpec((T, LANES), lambda b: (b, kcol)),
                  pl.BlockSpec((T, LANES), lambda b: (b, kcol + 1)),
                  pl.BlockSpec((1, WINDOW, LANES), lambda b: (b, 0, 0)),
                  pl.BlockSpec((1, WINDOW, LANES), lambda b: (b, 0, 0)),
                  pl.BlockSpec((T, LANES), lambda b: (0, 0)),
                  pl.BlockSpec((T, LANES), lambda b: (0, 0)),
                  pl.BlockSpec((1, LANES), lambda b: (0, 0))],
        out_specs=[pl.BlockSpec((T, SW_HEADS * SW_HD), lambda b: (b, 0)),
                   pl.BlockSpec((T, LANES), lambda b: (b, 0))],
        out_shape=[SDS((n, SW_HEADS * SW_HD), f32), SDS((n, LANES), f32)],
        compiler_params=_cparams("parallel"),
        name="swa_sample",
    )(z_sw, z_sw, z_sw, cache_k, cache_v, cos_t, sin_t, sinks_row)


def _out_ln_kernel(*refs, n_in):
    x_ref = refs[0]
    a_refs = refs[1:1 + n_in]
    w_refs = refs[1 + n_in:1 + 2 * n_in]
    g_ref, b_ref, rw_ref, rb_ref, o_ref, gates_ref, route_ref = refs[1 + 2 * n_in:]
    mm = _dots(w_refs[0].dtype == f32)[0]
    y = mm(a_refs[0][...], w_refs[0][...])
    for a_ref, w_ref in zip(a_refs[1:], w_refs[1:]):
        y = y + mm(a_ref[...], w_ref[...])
    x_new = _layer_norm(DN_ALPHA * x_ref[...] + y, g_ref[...], b_ref[...])
    o_ref[...] = x_new
    gates_ref[...], route_ref[...] = _route(x_new, rw_ref, rb_ref)


def _out_ln(x2, acts, ws, g_row, b_row, rw_t, rb_col, tm):
    n = x2.shape[0]
    row = lambda i: (i, 0)
    const = lambda i: (0, 0)
    col = lambda i: (0, i)
    return pl.pallas_call(
        functools.partial(_out_ln_kernel, n_in=len(acts)),
        grid=(n // tm,),
        in_specs=[pl.BlockSpec((tm, D_MODEL), row)]
        + [pl.BlockSpec((tm, a.shape[1]), row) for a in acts]
        + [pl.BlockSpec(w.shape, const) for w in ws]
        + [pl.BlockSpec((1, D_MODEL), const), pl.BlockSpec((1, D_MODEL), const),
           pl.BlockSpec((N_EXPERTS, D_MODEL), const), pl.BlockSpec((N_EXPERTS, 1), const)],
        out_specs=[pl.BlockSpec((tm, D_MODEL), row), pl.BlockSpec((N_EXPERTS, tm), col), pl.BlockSpec((8, tm), col)],
        out_shape=[SDS((n, D_MODEL), f32), SDS((N_EXPERTS, n), f32), SDS((8, n), f32)],
        compiler_params=_cparams("parallel"),
        name="out_proj_ln",
    )(x2, *acts, *ws, g_row, b_row, rw_t, rb_col)


def _logistic(x):
    return 1.0 / (1.0 + jnp.exp(-x))


def _route(x, rw_ref, rb_ref):
    logits = _dot_nt(rw_ref[...], x)
    aff = _logistic(logits)
    sc = aff + rb_ref[...]
    s = [sc[e:e + 1, :] for e in range(N_EXPERTS)]
    a = [aff[e:e + 1, :] for e in range(N_EXPERTS)]
    scores = []
    for gi in range(N_GROUPS):
        w, x, y, z = s[4 * gi:4 * gi + 4]
        p, q = jnp.maximum(w, x), jnp.minimum(w, x)
        r, t = jnp.maximum(y, z), jnp.minimum(y, z)
        scores.append(jnp.maximum(p, r) + jnp.maximum(jnp.minimum(p, r), jnp.maximum(q, t)))
    best = scores[0]
    gsel = jnp.zeros_like(best, dtype=jnp.int32)
    for gi in range(1, N_GROUPS):
        better = scores[gi] > best
        best = jnp.where(better, scores[gi], best)
        gsel = jnp.where(better, gi, gsel)
    sel = []
    for e in range(N_EXPERTS):
        gi, i = divmod(e, EXP_PER_GROUP)
        beaten = jnp.zeros_like(gsel)
        for j in range(EXP_PER_GROUP):
            if j == i:
                continue
            o = s[4 * gi + j]
            wins = (o >= s[e]) if j < i else (o > s[e])
            beaten = beaten + wins.astype(jnp.int32)
        sel.append((gsel == gi) & (beaten < 2))
    den = jnp.zeros_like(best)
    for e in range(N_EXPERTS):
        den = den + jnp.where(sel[e], a[e], 0.0)
    gate = [jnp.where(sel[e], a[e] / den, 0.0) for e in range(N_EXPERTS)]
    taken = jnp.zeros_like(gsel)
    ea = eb = wa = wb = jnp.zeros_like(best)
    for e in range(N_EXPERTS):
        first = sel[e] & (taken == 0)
        second = sel[e] & (taken == 1)
        ea = jnp.where(first, float(e), ea)
        wa = jnp.where(first, gate[e], wa)
        eb = jnp.where(second, float(e), eb)
        wb = jnp.where(second, gate[e], wb)
        taken = taken + sel[e].astype(jnp.int32)
    route = jnp.concatenate([ea, eb, wa, wb, jnp.zeros((4, ea.shape[1]), f32)], axis=0)
    return jnp.concatenate(gate, axis=0), route


def _moe_kernel(x_ref, gates_ref, wg_ref, wu_ref, wd_ref, g_ref, b_ref, o_ref, xb_ref, acc_ref):
    e = pl.program_id(1)

    @pl.when(e == 0)
    def _():
        xb_ref[...] = x_ref[...].astype(bf16)
        acc_ref[...] = jnp.zeros_like(acc_ref)

    xb = xb_ref[...]
    lane = lax.broadcasted_iota(jnp.int32, gates_ref.shape, 1)
    gcol = jnp.sum(jnp.where(lane == e, gates_ref[...], 0.0), axis=-1, keepdims=True)
    h = _silu(jnp.dot(xb, wg_ref[0], preferred_element_type=f32)) * jnp.dot(xb, wu_ref[0], preferred_element_type=f32)
    acc_ref[...] += jnp.dot((gcol * h).astype(bf16), wd_ref[0], preferred_element_type=f32)

    @pl.when(e == N_EXPERTS - 1)
    def _():
        o_ref[...] = _layer_norm(DN_ALPHA * x_ref[...] + acc_ref[...], g_ref[...], b_ref[...])


def _moe_ln(x2, gates, wg, wu, wd, base, g_row, b_row, tm):
    n = x2.shape[0]
    row = lambda i, e: (i, 0)
    const = lambda i, e: (0, 0)
    expert = lambda i, e: (base + e, 0, 0)
    return pl.pallas_call(
        _moe_kernel,
        grid=(n // tm, N_EXPERTS),
        in_specs=[pl.BlockSpec((tm, D_MODEL), row),
                  pl.BlockSpec((tm, N_EXPERTS), row),
                  pl.BlockSpec((1, D_MODEL, D_EXPERT), expert),
                  pl.BlockSpec((1, D_MODEL, D_EXPERT), expert),
                  pl.BlockSpec((1, D_EXPERT, D_MODEL), expert),
                  pl.BlockSpec((1, D_MODEL), const), pl.BlockSpec((1, D_MODEL), const)],
        out_specs=pl.BlockSpec((tm, D_MODEL), row),
        out_shape=SDS((n, D_MODEL), f32),
        scratch_shapes=[pltpu.VMEM((tm, D_MODEL), bf16), pltpu.VMEM((tm, D_MODEL), f32)],
        compiler_params=_cparams("parallel", "arbitrary"),
        name="moe_ln",
    )(x2, gates, wg, wu, wd, g_row, b_row)


N_PAIRS = N_GROUPS * (EXP_PER_GROUP * (EXP_PER_GROUP - 1) // 2)
MOE_TM = 256
MOE_DMA_ROWS = 512
_PAIR_A = [g * EXP_PER_GROUP + a for g in range(N_GROUPS) for a in range(EXP_PER_GROUP) for b in range(a + 1, EXP_PER_GROUP)]
_PAIR_B = [g * EXP_PER_GROUP + b for g in range(N_GROUPS) for a in range(EXP_PER_GROUP) for b in range(a + 1, EXP_PER_GROUP)]


def _gather_rows_kernel(idx_ref, src_ref, o_ref, sem, *, rows):
    base = pl.program_id(0) * rows

    def row_copy(j):
        return pltpu.make_async_copy(src_ref.at[pl.ds(idx_ref[base + j], 1)], o_ref.at[pl.ds(j, 1)], sem)

    def issue(j, carry):
        row_copy(j).start()
        return carry

    def drain(j, carry):
        row_copy(j).wait()
        return carry

    lax.fori_loop(0, rows, issue, 0, unroll=8)
    lax.fori_loop(0, rows, drain, 0, unroll=8)


def _gather_rows(src, idx, rows):
    n_out = idx.shape[0]
    d = src.shape[1]
    return pl.pallas_call(
        functools.partial(_gather_rows_kernel, rows=rows),
        grid_spec=pltpu.PrefetchScalarGridSpec(
            num_scalar_prefetch=1,
            grid=(n_out // rows,),
            in_specs=[pl.BlockSpec(memory_space=pl.ANY)],
            out_specs=pl.BlockSpec((rows, d), lambda i, idx_ref: (i, 0)),
            scratch_shapes=[pltpu.SemaphoreType.DMA(())]),
        out_shape=SDS((n_out, d), f32),
        compiler_params=_cparams("arbitrary"),
        name="gather_rows",
    )(idx, src)


def _scatter_rows_kernel(idx_ref, src_ref, init_ref, o_ref, sem, *, rows):
    del init_ref
    base = pl.program_id(0) * rows

    def row_copy(j):
        return pltpu.make_async_copy(src_ref.at[pl.ds(j, 1)], o_ref.at[pl.ds(idx_ref[base + j], 1)], sem)

    def issue(j, carry):
        row_copy(j).start()
        return carry

    def drain(j, carry):
        row_copy(j).wait()
        return carry

    lax.fori_loop(0, rows, issue, 0, unroll=8)
    lax.fori_loop(0, rows, drain, 0, unroll=8)


def _scatter_rows(src, idx, n_out, rows):
    n_src, d = src.shape
    return pl.pallas_call(
        functools.partial(_scatter_rows_kernel, rows=rows),
        grid_spec=pltpu.PrefetchScalarGridSpec(
            num_scalar_prefetch=1,
            grid=(n_src // rows,),
            in_specs=[pl.BlockSpec((rows, d), lambda i, idx_ref: (i, 0)), pl.BlockSpec(memory_space=pl.ANY)],
            out_specs=pl.BlockSpec(memory_space=pl.ANY),
            scratch_shapes=[pltpu.SemaphoreType.DMA(())]),
        out_shape=SDS((n_out, d), f32),
        input_output_aliases={2: 0},
        compiler_params=_cparams("arbitrary"),
        name="scatter_rows",
    )(idx, src, jnp.zeros((n_out, d), f32))


def _pair_plan(route, n, tm):
    ea = route[0].astype(jnp.int32)
    eb = route[1].astype(jnp.int32)
    a = ea % EXP_PER_GROUP
    b = eb % EXP_PER_GROUP
    pidx = jnp.where(a == 0, b - 1, jnp.where(a == 1, b + 1, 5))
    pair = (ea // EXP_PER_GROUP) * (N_PAIRS // N_GROUPS) + pidx
    onehot = (pair[:, None] == jnp.arange(N_PAIRS, dtype=jnp.int32)[None, :]).astype(jnp.int32)
    csum = jnp.cumsum(onehot, axis=0)
    counts = csum[-1]
    ntiles = (counts + tm - 1) // tm
    tile_end = jnp.cumsum(ntiles)
    tile_start = tile_end - ntiles
    row_of_token = jnp.sum(onehot * (csum - 1 + (tile_start * tm)[None, :]), axis=1)
    nt = n // tm + N_PAIRS
    tile_id = jnp.arange(nt, dtype=jnp.int32)
    tile_valid = tile_id < tile_end[-1]
    tile_pair = jnp.sum((tile_end[None, :] <= jnp.minimum(tile_id, tile_end[-1] - 1)[:, None]).astype(jnp.int32), axis=1)
    tile_pair = jnp.minimum(tile_pair, N_PAIRS - 1)
    pick = (tile_pair[:, None] == jnp.arange(N_PAIRS, dtype=jnp.int32)[None, :]).astype(jnp.int32)
    tile_a = jnp.sum(pick * jnp.asarray(_PAIR_A, jnp.int32)[None, :], axis=1)
    tile_b = jnp.sum(pick * jnp.asarray(_PAIR_B, jnp.int32)[None, :], axis=1)
    return row_of_token, tile_a, tile_b, tile_valid.astype(jnp.int32)


def _pair_expert_kernel(ta_ref, tb_ref, tv_ref, x_ref, rwt_ref, wga_ref, wua_ref, wda_ref, wgb_ref, wub_ref, wdb_ref,
                        g_ref, b_ref, o_ref):
    i = pl.program_id(0)
    valid = tv_ref[i] == 1

    @pl.when(valid)
    def _():
        x = x_ref[...]
        xb = x.astype(bf16)
        aff_a = _logistic(jnp.sum(x * rwt_ref[pl.ds(ta_ref[i], 1), :], axis=-1, keepdims=True))
        aff_b = _logistic(jnp.sum(x * rwt_ref[pl.ds(tb_ref[i], 1), :], axis=-1, keepdims=True))
        den = aff_a + aff_b
        acc = None
        for w, (wg, wu, wd) in ((aff_a / den, (wga_ref, wua_ref, wda_ref)), (aff_b / den, (wgb_ref, wub_ref, wdb_ref))):
            h = _silu(jnp.dot(xb, wg[0], preferred_element_type=f32)) * jnp.dot(xb, wu[0], preferred_element_type=f32)
            y = jnp.dot((w * h).astype(bf16), wd[0], preferred_element_type=f32)
            acc = y if acc is None else acc + y
        o_ref[...] = _layer_norm(DN_ALPHA * x + acc, g_ref[...], b_ref[...])

    @pl.when(jnp.logical_not(valid))
    def _():
        o_ref[...] = jnp.zeros_like(o_ref)


def _pair_experts(xs, rw_t, tile_a, tile_b, tile_valid, wg, wu, wd, base, g_row, b_row, tm):
    rows = xs.shape[0]
    row = lambda i, ta, tb, tv: (i, 0)
    const = lambda i, ta, tb, tv: (0, 0)
    ex_a = lambda i, ta, tb, tv: (base + ta[i], 0, 0)
    ex_b = lambda i, ta, tb, tv: (base + tb[i], 0, 0)
    up = pl.BlockSpec((1, D_MODEL, D_EXPERT), ex_a), pl.BlockSpec((1, D_MODEL, D_EXPERT), ex_b)
    down = pl.BlockSpec((1, D_EXPERT, D_MODEL), ex_a), pl.BlockSpec((1, D_EXPERT, D_MODEL), ex_b)
    return pl.pallas_call(
        _pair_expert_kernel,
        grid_spec=pltpu.PrefetchScalarGridSpec(
            num_scalar_prefetch=3,
            grid=(rows // tm,),
            in_specs=[pl.BlockSpec((tm, D_MODEL), row), pl.BlockSpec((N_EXPERTS, D_MODEL), const),
                      up[0], up[0], down[0], up[1], up[1], down[1],
                      pl.BlockSpec((1, D_MODEL), const), pl.BlockSpec((1, D_MODEL), const)],
            out_specs=pl.BlockSpec((tm, D_MODEL), row)),
        out_shape=SDS((rows, D_MODEL), f32),
        compiler_params=_cparams("arbitrary"),
        name="pair_experts",
    )(tile_a, tile_b, tile_valid, xs, rw_t, wg, wu, wd, wg, wu, wd, g_row, b_row)


def _unit_lower_inverses(mats, L):
    row = lax.broadcasted_iota(jnp.int32, (L, L), 0)
    col = lax.broadcasted_iota(jnp.int32, (L, L), 1)
    eye = (row == col).astype(f32)
    ps = [eye - a for a in mats]
    pws = [a.astype(bf16) for a in mats]
    span = 2
    while span < L:
        pws = [jnp.dot(pw, pw, preferred_element_type=f32).astype(bf16) for pw in pws]
        ps = [p + jnp.dot(pw, p.astype(bf16), preferred_element_type=f32) for p, pw in zip(ps, pws)]
        span *= 2
    return ps


def _gdn_kernel(x_ref, zg_ref, ba_ref, cw_ref, alog_ref, dt_ref, nw_ref, s0_ref, cb_ref,
                o_ref, s_ref, prev_ref, *, L, BB):
    @pl.when(pl.program_id(1) == 0)
    def _():
        s_ref[...] = s0_ref[...]
        prev_ref[...] = cb_ref[...]

    def conv_silu(bi, lo, width):
        cur = x_ref[bi, :, lo:lo + width]
        cat =jnp.concatenate([prev_ref[bi, :, lo:lo + width], cur], axis=0)
        acc = cat[5:5 + L] * cw_ref[0:1, lo:lo + width]
        acc = acc + cat[6:6 + L] * cw_ref[1:2, lo:lo + width]
        acc = acc + cat[7:7 + L] * cw_ref[2:3, lo:lo + width]
        acc = acc + cur * cw_ref[3:4, lo:lo + width]
        return _silu(acc)

    def l2n(v, scale):
        return v * lax.rsqrt(jnp.sum(v * v, axis=-1, keepdims=True) + 1e-6) * scale

    row = lax.broadcasted_iota(jnp.int32, (L, L), 0)
    col = lax.broadcasted_iota(jnp.int32, (L, L), 1)
    incl = row >= col
    strict = row > col
    rep = GD_V_HEADS // GD_QK_HEADS
    bas = [ba_ref[bi] for bi in range(BB)]
    betas = [_sigmoid(ba) for ba in bas]
    gls = [-jnp.exp(alog_ref[...]) * _softplus(ba + dt_ref[...]) for ba in bas]
    gcum_all = _dot(incl.astype(f32), jnp.concatenate(gls, axis=1))
    gcums = [gcum_all[:, bi * LANES:(bi + 1) * LANES] for bi in range(BB)]
    gcum_ts = [g.T for g in gcums]
    units = [(bi, hv) for bi in range(BB) for hv in range(GD_V_HEADS)]
    qs, ks, amats, qkds, egs, g_cols, rhss = {}, {}, [], {}, {}, {}, []
    for bi in range(BB):
        for j in range(GD_QK_HEADS):
            q = l2n(conv_silu(bi, j * GD_HD, GD_HD), GD_HD ** -0.5)
            k = l2n(conv_silu(bi, GD_QK_W + j * GD_HD, GD_HD), 1.0)
            qb, kb = q.astype(bf16), k.astype(bf16)
            kk = lax.dot_general(kb, kb, (((1,), (1,)), ((), ())), preferred_element_type=f32)
            qk = lax.dot_general(qb, kb, (((1,), (1,)), ((), ())), preferred_element_type=f32)
            qs[bi, j] = qb
            ks[bi, j] = k
            for r in range(rep):
                hv = j * rep + r
                v = conv_silu(bi, 2 * GD_QK_W + hv * GD_HD, GD_HD)
                g_col = gcums[bi][:, GD_V_HEADS + hv:GD_V_HEADS + hv + 1]
                g_row = gcum_ts[bi][GD_V_HEADS + hv:GD_V_HEADS + hv + 1, :]
                b_col = betas[bi][:, hv:hv + 1]
                decay = jnp.exp(jnp.where(incl, g_col - g_row, -jnp.inf))
                eg = jnp.exp(g_col)
                amats.append(jnp.where(strict, b_col * kk * decay, 0.0))
                qkds[bi, hv] = (qk * decay).astype(bf16)
                egs[bi, hv] = eg
                g_cols[bi, hv] = g_col
                rhss.append(jnp.concatenate([b_col * v, (b_col * eg) * k], axis=-1))
    tinvs = _unit_lower_inverses(amats, L)
    sols = dict(zip(units, [_bdot(t, r) for t, r in zip(tinvs, rhss)]))
    for u in units:
        bi, hv = u
        j = hv // rep
        st = s_ref[bi, hv]
        stb = st.astype(bf16)
        sol = sols[u]
        wn = sol[:, :GD_HD] - jnp.dot(sol[:, GD_HD:].astype(bf16), stb, preferred_element_type=f32)
        wnb = wn.astype(bf16)
        o = egs[u] * jnp.dot(qs[bi, j], stb, preferred_element_type=f32) + jnp.dot(qkds[u], wnb, preferred_element_type=f32)
        g_col = g_cols[u]
        g_last = g_col[L - 1:L, :]
        kt = (jnp.exp(g_last - g_col) * ks[bi, j]).T.astype(bf16)
        s_ref[bi, hv] = jnp.exp(g_last) * st + jnp.dot(kt, wnb, preferred_element_type=f32)
        o = o * lax.rsqrt(jnp.mean(o * o, axis=-1, keepdims=True) + RMS_EPS) * nw_ref[...]
        o_ref[bi, :, hv * GD_HD:(hv + 1) * GD_HD] = o * _silu(zg_ref[bi, :, hv * GD_HD:(hv + 1) * GD_HD])
    for bi in range(BB):
        prev_ref[bi] = x_ref[bi, L - 8:L, :]


def _gdn(qkv, zg, ba, conv_w, alog_row, dt_row, norm_row, s0, conv8, B, T, L):
    nc = T // L
    n = B * T
    bb = 2
    tok = lambda b, c: (b, c, 0)
    const = lambda b, c: (0, 0)
    st4 = lambda b, c: (b, 0, 0, 0)
    o, s_out = pl.pallas_call(
        functools.partial(_gdn_kernel, L=L, BB=bb),
        grid=(B // bb, nc),
        in_specs=[pl.BlockSpec((bb, L, GD_CONV_CH), tok),
                  pl.BlockSpec((bb, L, GD_V_W), tok),
                  pl.BlockSpec((bb, L, LANES), tok),
                  pl.BlockSpec((GD_CONV, GD_CONV_CH), const),
                  pl.BlockSpec((1, LANES), const),
                  pl.BlockSpec((1, LANES), const),
                  pl.BlockSpec((1, GD_HD), const),
                  pl.BlockSpec((bb, GD_V_HEADS, GD_HD, GD_HD), st4),
                  pl.BlockSpec((bb, 8, GD_CONV_CH), lambda b, c: (b, 0, 0))],
        out_specs=[pl.BlockSpec((bb, L, GD_V_W), tok),
                   pl.BlockSpec((bb, GD_V_HEADS, GD_HD, GD_HD), st4)],
        out_shape=[SDS((B, T, GD_V_W), f32), SDS((B, GD_V_HEADS, GD_HD, GD_HD), f32)],
        scratch_shapes=[pltpu.VMEM((bb, 8, GD_CONV_CH), f32)],
        compiler_params=_cparams("parallel", "arbitrary"),
        name="gdn",
    )(qkv.reshape(B, T, GD_CONV_CH), zg.reshape(B, T, GD_V_W), ba.reshape(B, T, LANES), conv_w, alog_row, dt_row,
      norm_row, s0, conv8)
    return o.reshape(n, GD_V_W), s_out


def _pad_lanes(row, offset=0):
    return jnp.zeros((1, LANES), f32).at[0, offset:offset + row.shape[0]].set(row.astype(f32))


def _rope_tables(pos):
    half = SW_HD // 2
    inv = ROPE_THETA ** (-jnp.arange(half, dtype=f32) / half)
    ang = pos.astype(f32)[:, None] * inv[None, :]
    cos, sin = jnp.cos(ang), jnp.sin(ang)
    cos_t = jnp.concatenate([cos, cos] * SW_KV_HEADS, axis=-1)
    sin_t = jnp.concatenate([-sin, sin] * SW_KV_HEADS, axis=-1)
    return cos_t, sin_t


def _tile(n, pref):
    return pref if n % pref == 0 else n


def _trunk(x, pos, L, state, p):
    B, T, _ = x.shape
    n = B * T
    x2 = x.reshape(n, D_MODEL)
    tm = _tile(n, 512)
    precise = state is not None
    ab_w, wo_h, wo_a = (p["ab_w32"], p["ab_wo_h32"], p["ab_wo_a32"]) if precise else (p["ab_w"], p["ab_wo_h"], p["ab_wo_a"])

    z_ml, z_sw, z_g = _proj(x2, ab_w, ((0, ML_W), (ML_W, SW_W), (ML_W + SW_W, LANES)), tm)
    if state is None:
        h_ml, ml_c, ml_n, ml_m = _mlstm_pairs(z_ml, z_g, p["ab_bias"], p["ab_norm"], B, T, L)
    else:
        cn0 = jnp.concatenate([state["ml_C"], state["ml_n"][..., None],
                               jnp.zeros((B, ML_HEADS, ML_DK, LANES - ML_DV - 1), f32)], axis=-1)
        m0 = jnp.zeros((B, 1, LANES), f32).at[:, 0, :ML_HEADS].set(state["ml_m"])
        h_ml, cn, m_out = _mlstm(z_ml, z_g, p["ab_bias"], p["ab_norm"], cn0, m0, B, T, L, precise)
        ml_c, ml_n, ml_m = cn[..., :ML_DV], cn[..., ML_DV], m_out[:, 0, :ML_HEADS]
    h_ml = h_ml.reshape(n, ML_HEADS * ML_DV)
    cos_t, sin_t = _rope_tables(pos)
    if state is None:
        a_sw, k_rot = _swa_prompt(z_sw, cos_t, sin_t, p["ab_sinks"], B, T, L)
    else:
        a_sw, k_rot = _swa_sample(z_sw, state["sw_k"].reshape(B, WINDOW, LANES),
                                  state["sw_v"].reshape(B, WINDOW, LANES), cos_t, sin_t, p["ab_sinks"], B, T)
    keep = min(T, WINDOW)
    new_k = k_rot.reshape(B, T, LANES)[:, T - keep:].reshape(B, keep, SW_KV_HEADS, SW_HD)
    new_v = z_sw.reshape(B, T, SW_W)[:, T - keep:, SW_HEADS * SW_HD + LANES:].reshape(B, keep, SW_KV_HEADS, SW_HD)
    x2, gates_t, route = _out_ln(x2, [h_ml, a_sw], [wo_h, wo_a], p["ln_g"][0][0], p["ln_b"][0][0],
                                 p["router_wt"], p["router_b"], tm)
    x2 = _moe_block(x2, gates_t, route, p, 0)

    tm1 = _tile(n, 256)
    if state is None:
        s0 = jnp.zeros((B, GD_V_HEADS, GD_HD, GD_HD), f32)
        conv8 = jnp.zeros((B, 8, GD_CONV_CH), f32)
    else:
        s0 = state["gd_S"]
        conv8 = jnp.concatenate([jnp.zeros((B, 8 - (GD_CONV - 1), GD_CONV_CH), f32), state["gd_conv"]], axis=1)
    qkv, zg, ba = _proj(x2, p["c_w"], ((0, GD_CONV_CH), (GD_CONV_CH, GD_V_W), (GD_CONV_CH + GD_V_W, LANES)), tm1)
    new_conv = qkv.reshape(B, T, GD_CONV_CH)[:, T - (GD_CONV - 1):]
    o_gd, s_out = _gdn(qkv, zg, ba, p["c_conv_w"], p["c_alog"], p["c_dt"], p["c_norm"], s0, conv8, B, T, L)
    x2, gates_t, route = _out_ln(x2, [o_gd], [p["c_wo"]], p["ln_g"][1][0], p["ln_b"][1][0],
                                 p["router_wt"], p["router_b"], tm)
    x2 = _moe_block(x2, gates_t, route, p, 1)

    outs = (new_k[None], new_v[None], ml_c[None], ml_n[None], ml_m[None],
            s_out[None], new_conv[None])
    return x2.reshape(B, T, D_MODEL), outs


def _moe_block(x2, gates_t, route, p, layer):
    n = x2.shape[0]
    wg, wu, wd = p["ex_gate"], p["ex_up"], p["ex_down"]
    base = layer * N_EXPERTS
    g_row, b_row = p["ln_g"][layer][1], p["ln_b"][layer][1]
    if n < N_PAIRS * MOE_TM:
        return _moe_ln(x2, gates_t.T, wg, wu, wd, base, g_row, b_row, _tile(n, 1024))
    row_of_token, tile_a, tile_b, tile_valid = _pair_plan(route, n, MOE_TM)
    xs = _scatter_rows(x2, row_of_token, n + N_PAIRS * MOE_TM, MOE_DMA_ROWS)
    ys = _pair_experts(xs, p["router_wt"], tile_a, tile_b, tile_valid, wg, wu, wd, base, g_row, b_row, MOE_TM)
    return _gather_rows(ys, row_of_token, MOE_DMA_ROWS)


def kernel(x_prompt, x_sample, cache_swa_k, cache_swa_v, state_mlstm_C, state_mlstm_n, state_mlstm_m, state_gdn_S, state_gdn_conv, ab_w_in, ab_b_i, ab_b_f, ab_norm, ab_sinks, ab_w_out, c_w_in, c_conv_w, c_a_log, c_dt_bias, c_norm, c_w_out, ln_g, ln_b, router_w, router_b, ex_gate, ex_up, ex_down):
    gate_lo = ML_W
    sw_lo = ML_W + 2 * ML_HEADS
    w0 = ab_w_in[0]
    ab_w32 = jnp.concatenate([w0[:, :gate_lo], w0[:, sw_lo:], w0[:, gate_lo:sw_lo],
                              jnp.zeros((D_MODEL, LANES - 2 * ML_HEADS), f32)], axis=1)
    w1 = c_w_in[0]
    c_w = jnp.concatenate([w1, jnp.zeros((D_MODEL, LANES - 2 * GD_V_HEADS), f32)], axis=1).astype(bf16)
    wo32 = ab_w_out[0]
    wo = wo32.astype(bf16)
    p = {
        "ab_w": ab_w32.astype(bf16),
        "ab_w32": ab_w32,
        "ab_wo_h32": wo32[:ML_HEADS * ML_DV],
        "ab_wo_a32": wo32[ML_HEADS * ML_DV:],
        "ab_bias": _pad_lanes(jnp.concatenate([ab_b_i[0], ab_b_f[0]])),
        "ab_norm": ab_norm[0].reshape(1, ML_HEADS * ML_DV),
        "ab_sinks": _pad_lanes(ab_sinks[0]),
        "ab_wo_h": wo[:ML_HEADS * ML_DV],
        "ab_wo_a": wo[ML_HEADS * ML_DV:],
        "c_w": c_w,
        "c_conv_w": c_conv_w[0],
        "c_alog": _pad_lanes(c_a_log[0], GD_V_HEADS),
        "c_dt": _pad_lanes(c_dt_bias[0], GD_V_HEADS),
        "c_norm": c_norm[0].reshape(1, GD_HD),
        "c_wo": c_w_out[0].astype(bf16),
        "ln_g": [[ln_g[i, j].reshape(1, D_MODEL) for j in range(2)] for i in range(DEPTH)],
        "ln_b": [[ln_b[i, j].reshape(1, D_MODEL) for j in range(2)] for i in range(DEPTH)],
        "router_wt": router_w.T,
        "router_b": router_b.reshape(N_EXPERTS, 1),
        "ex_gate": ex_gate.astype(bf16).reshape(DEPTH * N_EXPERTS, D_MODEL, D_EXPERT),
        "ex_up": ex_up.astype(bf16).reshape(DEPTH * N_EXPERTS, D_MODEL, D_EXPERT),
        "ex_down": ex_down.astype(bf16).reshape(DEPTH * N_EXPERTS, D_EXPERT, D_MODEL),
    }
    t_p = x_prompt.shape[1]
    y_p, st_p = _trunk(x_prompt, jnp.arange(t_p, dtype=jnp.int32), CHUNK, None, p)
    t_s = x_sample.shape[1]
    state = {"sw_k": cache_swa_k[0], "sw_v": cache_swa_v[0], "ml_C": state_mlstm_C[0], "ml_n": state_mlstm_n[0],
             "ml_m": state_mlstm_m[0], "gd_S": state_gdn_S[0], "gd_conv": state_gdn_conv[0]}
    y_s, st_s = _trunk(x_sample, PAST_LEN + jnp.arange(t_s, dtype=jnp.int32), t_s, state, p)
    return (y_p, y_s) + st_p + st_s
```

```python
import functools
import math

import jax
import jax.numpy as jnp
import numpy as np
from jax import lax
from jax.experimental import pallas as pl
from jax.experimental.pallas import tpu as pltpu

f32 = jnp.float32
bf16 = jnp.bfloat16
HIGHEST = lax.Precision.HIGHEST

D_MODEL = 1024
DEPTH = 2
CHUNK = 64
PAST_LEN = 2048
ML_HEADS = 8
ML_DK = 64
ML_DV = 64
SW_HEADS = 8
SW_KV_HEADS = 2
SW_HD = 64
SW_GROUP = SW_HEADS // SW_KV_HEADS
WINDOW = 128
ROPE_THETA = 10000.0
GD_QK_HEADS = 8
GD_V_HEADS = 16
GD_HD = 128
GD_CONV = 4
GD_QK_W = GD_QK_HEADS * GD_HD
GD_V_W = GD_V_HEADS * GD_HD
GD_CONV_CH = 2 * GD_QK_W + GD_V_W
N_EXPERTS = 16
N_GROUPS = 4
EXP_PER_GROUP = 4
D_EXPERT = 512
DN_ALPHA = (2 * DEPTH) ** 0.25
LN_EPS = 1e-5
RMS_EPS = 1e-6

LANES = 128
ML_W = 4 * ML_HEADS * ML_DK
SW_W = SW_HEADS * SW_HD + 2 * SW_KV_HEADS * SW_HD
VMEM_LIMIT = 56 * 1024 * 1024

SDS = jax.ShapeDtypeStruct


def _cparams(*sem):
    return pltpu.CompilerParams(dimension_semantics=sem, vmem_limit_bytes=VMEM_LIMIT)


def _dot(a, b):
    return jnp.dot(a, b, preferred_element_type=f32, precision=HIGHEST)


def _dot_nt(a, b):
    return lax.dot_general(a, b, (((1,), (1,)), ((), ())), preferred_element_type=f32, precision=HIGHEST)


def _bdot(a, b):
    return jnp.dot(a.astype(bf16), b.astype(bf16), preferred_element_type=f32)


def _bdot_nt(a, b):
    return lax.dot_general(a.astype(bf16), b.astype(bf16), (((1,), (1,)), ((), ())), preferred_element_type=f32)


def _bdot_tn(a, b):
    return jnp.dot(a.T.astype(bf16), b.astype(bf16), preferred_element_type=f32)


def _dots(precise):
    if precise:
        return _dot, _dot_nt, lambda a, b: _dot(a.T, b)
    return _bdot, _bdot_nt, _bdot_tn


def _sigmoid(x):
    return 0.5 + 0.5 * jnp.tanh(0.5 * x)


def _silu(x):
    hx = 0.5 * x
    return hx + hx * jnp.tanh(hx)


def _softplus(x):
    return jnp.maximum(x, 0.0) + jnp.log(1.0 + jnp.exp(-jnp.abs(x)))


def _layer_norm(v, g, b):
    mu = jnp.mean(v, axis=-1, keepdims=True)
    d = v - mu
    var = jnp.mean(d * d, axis=-1, keepdims=True)
    return d * lax.rsqrt(var + LN_EPS) * g + b


def _proj_kernel(x_ref, w_ref, *o_refs, splits, col_chunk, precise):
    mm = _dots(precise)[0]
    xb = x_ref[...] if precise else x_ref[...].astype(bf16)
    for o_ref, (start, width) in zip(o_refs, splits):
        for c in range(0, width, col_chunk):
            cw = min(col_chunk, width - c)
            o_ref[:, c:c + cw] = mm(xb, w_ref[:, start + c:start + c + cw])


def _proj(x2, w, splits, tm):
    n, k = x2.shape
    return pl.pallas_call(
        functools.partial(_proj_kernel, splits=splits, col_chunk=512, precise=w.dtype == f32),
        grid=(n // tm,),
        in_specs=[pl.BlockSpec((tm, k), lambda i: (i, 0)),
                  pl.BlockSpec(w.shape, lambda i: (0, 0), pipeline_mode=pl.Buffered(1))],
        out_specs=[pl.BlockSpec((tm, wd), lambda i: (i, 0)) for _, wd in splits],
        out_shape=[SDS((n, wd), f32) for _, wd in splits],
        compiler_params=_cparams("parallel"),
        name="in_proj",
    )(x2, w)


def _mlstm_kernel(z_ref, g_ref, bias_ref, nw_ref, cn0_ref, m0_ref, h_ref, cn_ref, m_ref, *, L, BB, precise):
    @pl.when(pl.program_id(1) == 0)
    def _():
        cn_ref[...] = cn0_ref[...]
        m_ref[...] = m0_ref[...]

    mm, mm_nt, mm_tn = _dots(precise)
    row = lax.broadcasted_iota(jnp.int32, (L, L), 0)
    col = lax.broadcasted_iota(jnp.int32, (L, L), 1)
    causal = row >= col
    tri = causal.astype(f32)
    lane = lax.broadcasted_iota(jnp.int32, (1, LANES), 1)
    lane_l = lax.broadcasted_iota(jnp.int32, (L, ML_DV), 1)
    one_hot0 = (lane_l == 0).astype(f32)
    for bi in range(BB):
        g = g_ref[bi] + bias_ref[...]
        lf = jnp.minimum(g, 0.0) - jnp.log(1.0 + jnp.exp(-jnp.abs(g)))
        bcum = _dot(tri, lf)
        b_t = bcum.T
        g_t = g.T
        m_row = m_ref[bi]
        new_m = m_row
        outs = []
        for h in range(ML_HEADS):
            b_col = bcum[:, ML_HEADS + h:ML_HEADS + h + 1]
            b_row = b_t[ML_HEADS + h:ML_HEADS + h + 1, :]
            ig_row = g_t[h:h + 1, :]
            ig_col = g[:, h:h + 1]
            m_h = m_row[:, h:h + 1]
            dmat = jnp.where(causal, b_col - b_row + ig_row, -jnp.inf)
            inter = b_col + m_h
            mt = jnp.maximum(inter, jnp.max(dmat, axis=-1, keepdims=True))
            a = jnp.exp(inter - mt)
            q = z_ref[bi, :, h * ML_DK:(h + 1) * ML_DK]
            k = z_ref[bi, :, ML_HEADS * ML_DK + h * ML_DK:ML_HEADS * ML_DK + (h + 1) * ML_DK] * (ML_DK ** -0.5)
            v = z_ref[bi, :, 2 * ML_HEADS * ML_DK + h * ML_DV:2 * ML_HEADS * ML_DK + (h + 1) * ML_DV]
            og = z_ref[bi, :, 3 * ML_HEADS * ML_DK + h * ML_DV:3 * ML_HEADS * ML_DK + (h + 1) * ML_DV]
            s = mm_nt(q, k) * jnp.exp(dmat - mt)
            vext = jnp.concatenate([v, one_hot0], axis=-1)
            cn = cn_ref[bi, h]
            tot = a * mm(q, cn) + mm(s, vext)
            num = tot[:, :ML_DV]
            den = tot[:, ML_DV:ML_DV + 1]
            hh = num / jnp.maximum(jnp.abs(den), jnp.exp(-mt))
            hh = hh * lax.rsqrt(jnp.mean(hh * hh, axis=-1, keepdims=True) + RMS_EPS) * nw_ref[:, h * ML_DV:(h + 1) * ML_DV]
            outs.append(hh * _sigmoid(og))
            m_new = mt[L - 1:L, :]
            b_last = b_col[L - 1:L, :]
            wk = jnp.exp(b_last - b_col + ig_col - m_new)
            dec = jnp.exp(b_last + m_h - m_new)
            cn_ref[bi, h] = dec * cn + mm_tn(k, wk * vext)
            new_m = jnp.where(lane == h, m_new, new_m)
        m_ref[bi] = new_m
        h_ref[bi] = jnp.concatenate(outs, axis=-1)


def _mlstm(z_ml, z_g, bias_row, norm_row, cn0, m0, B, T, L, precise):
    nc = T // L
    bb = min(B, 4)
    tok = lambda b, c: (b, c, 0)
    st4 = lambda b, c: (b, 0, 0, 0)
    st3 = lambda b, c: (b, 0, 0)
    return pl.pallas_call(
        functools.partial(_mlstm_kernel, L=L, BB=bb, precise=precise),
        grid=(B // bb, nc),
        in_specs=[pl.BlockSpec((bb, L, ML_W), tok),
                  pl.BlockSpec((bb, L, LANES), tok),
                  pl.BlockSpec((1, LANES), lambda b, c: (0, 0)),
                  pl.BlockSpec((1, ML_HEADS * ML_DV), lambda b, c: (0, 0)),
                  pl.BlockSpec((bb, ML_HEADS, ML_DK, LANES), st4),
                  pl.BlockSpec((bb, 1, LANES), st3)],
        out_specs=[pl.BlockSpec((bb, L, ML_HEADS * ML_DV), tok),
                   pl.BlockSpec((bb, ML_HEADS, ML_DK, LANES), st4),
                   pl.BlockSpec((bb, 1, LANES), st3)],
        out_shape=[SDS((B, T, ML_HEADS * ML_DV), f32),
                   SDS((B, ML_HEADS, ML_DK, LANES), f32),
                   SDS((B, 1, LANES), f32)],
        compiler_params=_cparams("parallel", "arbitrary"),
        name="mlstm",
    )(z_ml.reshape(B, T, ML_W), z_g.reshape(B, T, LANES), bias_row, norm_row, cn0, m0)


ML_PAIRS = ML_HEADS // 2
ML_REP_QUANTS = 3


def _pair_select_matrix():
    sel = np.zeros((LANES, ML_REP_QUANTS * ML_PAIRS * LANES), np.float32)
    for qn in range(ML_REP_QUANTS):
        for pr in range(ML_PAIRS):
            for half in range(2):
                lo = (qn * ML_PAIRS + pr) * LANES + half * ML_DV
                sel[ML_HEADS * qn + 2 * pr + half, lo:lo + ML_DV] = 1.0
    return jnp.asarray(sel, bf16)


def _exact_select(x, sel):
    hi = x.astype(bf16)
    r1 = x - hi.astype(f32)
    mid = r1.astype(bf16)
    lo = (r1 - mid.astype(f32)).astype(bf16)
    mm = lambda t: jnp.dot(t, sel, preferred_element_type=f32)
    return (mm(hi) + mm(mid)) + mm(lo)


def _mlstm_pair_kernel(z_ref, g_ref, bias_ref, nw_ref, sel_ref, cbd0_ref, nbd0_ref, m0_ref,
                       h_ref, cbd_ref, nbd_ref, m_ref, *, L, BB):
    @pl.when(pl.program_id(1) == 0)
    def _():
        cbd_ref[...] = cbd0_ref[...]
        nbd_ref[...] = nbd0_ref[...]
        m_ref[...] = m0_ref[...]

    tri = (lax.broadcasted_iota(jnp.int32, (L, L), 0) >= lax.broadcasted_iota(jnp.int32, (L, L), 1)).astype(f32)
    row_t = lax.broadcasted_iota(jnp.int32, (L, LANES), 0)
    lane_t = lax.broadcasted_iota(jnp.int32, (L, LANES), 1)
    first_half = lane_t < ML_DV
    causal2 = row_t >= (lane_t % ML_DV)
    rr = lax.broadcasted_iota(jnp.int32, (LANES, LANES), 0)
    cc = lax.broadcasted_iota(jnp.int32, (LANES, LANES), 1)
    same_block = (rr < ML_DV) == (cc < ML_DV)
    ones_bd = same_block.astype(bf16)
    lane_1 = lax.broadcasted_iota(jnp.int32, (1, LANES), 1)
    sel = sel_ref[...]
    n_tiles = ML_REP_QUANTS * ML_PAIRS

    gs = [g_ref[bi] + bias_ref[...] for bi in range(BB)]
    lfs = [jnp.minimum(g, 0.0) - jnp.log(1.0 + jnp.exp(-jnp.abs(g))) for g in gs]
    bc_all = _dot(tri, jnp.concatenate(lfs, axis=1))
    g_ts = [g.T for g in gs]
    b_ts = [bc_all[:, bi * LANES:(bi + 1) * LANES].T for bi in range(BB)]
    r_rows = [g_ts[bi][0:ML_HEADS] - b_ts[bi][ML_HEADS:2 * ML_HEADS] for bi in range(BB)]
    cm = jnp.concatenate([jnp.concatenate(r_rows, axis=0), jnp.full((BB * ML_HEADS, LANES - L), -jnp.inf, f32)], axis=1)
    shift = 1
    while shift < L:
        cm = jnp.maximum(cm, pltpu.roll(cm, shift, 1))
        shift *= 2
    cols = [jnp.concatenate([g_ts[bi][0:ML_HEADS], b_ts[bi][ML_HEADS:2 * ML_HEADS],
                             cm[bi * ML_HEADS:(bi + 1) * ML_HEADS, :L],
                             jnp.zeros((LANES - 3 * ML_HEADS, L), f32)], axis=0).T for bi in range(BB)]
    rep_all = _exact_select(jnp.concatenate(cols, axis=0), sel)
    rep = [rep_all[bi * L:(bi + 1) * L] for bi in range(BB)]
    m_all = jnp.concatenate([m_ref[bi] for bi in range(BB)] + [jnp.zeros((8 - BB, LANES), f32)], axis=0)
    m_rep_all = _exact_select(m_all, sel[:, :ML_PAIRS * LANES])
    m_rep = [m_rep_all[bi:bi + 1] for bi in range(BB)]

    units = [(bi, pr) for bi in range(BB) for pr in range(ML_PAIRS)]
    st = {}
    for u in units:
        bi, pr = u
        tile = lambda qn: rep[bi][:, (qn * ML_PAIRS + pr) * LANES:(qn * ML_PAIRS + pr + 1) * LANES]
        ig_rep, b_rep, cm_rep = tile(0), tile(1), tile(2)
        m_pair = m_rep[bi][:, pr * LANES:(pr + 1) * LANES]
        inter = b_rep + m_pair
        mt = jnp.maximum(inter, b_rep + cm_rep)
        r_row = jnp.concatenate([r_rows[bi][2 * pr:2 * pr + 1], r_rows[bi][2 * pr + 1:2 * pr + 2]], axis=1)
        e = jnp.exp(jnp.where(causal2, (b_rep - mt) + r_row, -jnp.inf))
        m_new = mt[L - 1:L]
        b_last = b_rep[L - 1:L]
        lo = pr * LANES
        q = z_ref[bi, :, lo:lo + LANES].astype(bf16)
        k = z_ref[bi, :, ML_HEADS * ML_DK + lo:ML_HEADS * ML_DK + lo + LANES] * (ML_DK ** -0.5)
        v = z_ref[bi, :, 2 * ML_HEADS * ML_DK + lo:2 * ML_HEADS * ML_DK + lo + LANES]
        kbd = jnp.concatenate([jnp.where(first_half, k, 0.0), jnp.where(first_half, 0.0, k)], axis=0).astype(bf16)
        vbd = jnp.concatenate([jnp.where(first_half, v, 0.0), jnp.where(first_half, 0.0, v)], axis=0).astype(bf16)
        wk = jnp.exp(b_last - b_rep + ig_rep - m_new)
        st[u] = dict(a=jnp.exp(inter - mt), em=jnp.exp(-mt), e=e, m_new=m_new, dec=jnp.exp(b_last + m_pair - m_new),
                     q=q, kbd=kbd, vbd=vbd, k_t=k.T.astype(bf16), wkv=(wk * v).astype(bf16), wk=wk.astype(bf16),
                     cbd=cbd_ref[bi, pr], nbd=nbd_ref[bi, pr])
    for u in units:
        d = st[u]
        d["qk"] = lax.dot_general(d["q"], d["kbd"], (((1,), (1,)), ((), ())), preferred_element_type=f32)
        d["qc"] = jnp.dot(d["q"], d["cbd"].astype(bf16), preferred_element_type=f32)
        d["qn"] = jnp.dot(d["q"], d["nbd"].astype(bf16), preferred_element_type=f32)
    for u in units:
        d = st[u]
        s = (d["qk"] * d["e"]).astype(bf16)
        num = d["a"] * d["qc"] + jnp.dot(s, d["vbd"], preferred_element_type=f32)
        den = d["a"] * d["qn"] + jnp.dot(s, ones_bd, preferred_element_type=f32)
        d["hh"] = num / jnp.maximum(jnp.abs(den), d["em"])
    for u in units:
        bi, pr = u
        d = st[u]
        sq = d["hh"] * d["hh"]
        sq_hi = sq.astype(bf16)
        sq_lo = (sq - sq_hi.astype(f32)).astype(bf16)
        ms = (jnp.dot(sq_hi, ones_bd, preferred_element_type=f32)
              + jnp.dot(sq_lo, ones_bd, preferred_element_type=f32)) * (1.0 / ML_DV)
        lo = pr * LANES
        og = z_ref[bi, :, 3 * ML_HEADS * ML_DK + lo:3 * ML_HEADS * ML_DK + lo + LANES]
        h_ref[bi, :, lo:lo + LANES] = d["hh"] * lax.rsqrt(ms + RMS_EPS) * nw_ref[:, lo:lo + LANES] * _sigmoid(og)
    for u in units:
        bi, pr = u
        d = st[u]
        cbd_ref[bi, pr] = d["dec"] * d["cbd"] + jnp.where(
            same_block, jnp.dot(d["k_t"], d["wkv"], preferred_element_type=f32), 0.0)
        nbd_ref[bi, pr] = d["dec"] * d["nbd"] + jnp.where(
            same_block, jnp.dot(d["k_t"], d["wk"], preferred_element_type=f32), 0.0)
    for bi in range(BB):
        new_m = m_ref[bi]
        for pr in range(ML_PAIRS):
            m_new = st[(bi, pr)]["m_new"]
            new_m = jnp.where(lane_1 == 2 * pr, m_new[:, 0:1], new_m)
            new_m = jnp.where(lane_1 == 2 * pr + 1, m_new[:, ML_DV:ML_DV + 1], new_m)
        m_ref[bi] = new_m


def _mlstm_pairs(z_ml, z_g, bias_row, norm_row, B, T, L):
    assert 2 * L == LANES and ML_DK == ML_DV == L
    nc = T // L
    bb = min(B, 4)
    tok = lambda b, c: (b, c, 0)
    st4 = lambda b, c: (b, 0, 0, 0)
    st3 = lambda b, c: (b, 0, 0)
    const = lambda b, c: (0, 0)
    sel = _pair_select_matrix()
    zeros_bd = jnp.zeros((B, ML_PAIRS, LANES, LANES), f32)
    h, cbd, nbd, m = pl.pallas_call(
        functools.partial(_mlstm_pair_kernel, L=L, BB=bb),
        grid=(B // bb, nc),
        in_specs=[pl.BlockSpec((bb, L, ML_W), tok),
                  pl.BlockSpec((bb, L, LANES), tok),
                  pl.BlockSpec((1, LANES), const),
                  pl.BlockSpec((1, ML_HEADS * ML_DV), const),
                  pl.BlockSpec(sel.shape, const),
                  pl.BlockSpec((bb, ML_PAIRS, LANES, LANES), st4),
                  pl.BlockSpec((bb, ML_PAIRS, LANES, LANES), st4),
                  pl.BlockSpec((bb, 1, LANES), st3)],
        out_specs=[pl.BlockSpec((bb, L, ML_HEADS * ML_DV), tok),
                   pl.BlockSpec((bb, ML_PAIRS, LANES, LANES), st4),
                   pl.BlockSpec((bb, ML_PAIRS, LANES, LANES), st4),
                   pl.BlockSpec((bb, 1, LANES), st3)],
        out_shape=[SDS((B, T, ML_HEADS * ML_DV), f32),
                   SDS((B, ML_PAIRS, LANES, LANES), f32),
                   SDS((B, ML_PAIRS, LANES, LANES), f32),
                   SDS((B, 1, LANES), f32)],
        compiler_params=_cparams("parallel", "arbitrary"),
        name="mlstm_pairs",
    )(z_ml.reshape(B, T, ML_W), z_g.reshape(B, T, LANES), bias_row, norm_row, sel, zeros_bd, zeros_bd,
      jnp.zeros((B, 1, LANES), f32))
    c_out = jnp.stack([cbd[:, :, :ML_DK, :ML_DV], cbd[:, :, ML_DK:, ML_DV:]], axis=2).reshape(B, ML_HEADS, ML_DK, ML_DV)
    n_out = jnp.stack([nbd[:, :, :ML_DK, 0], nbd[:, :, ML_DK:, ML_DV]], axis=2).reshape(B, ML_HEADS, ML_DK)
    return h, c_out, n_out, m[:, 0, :ML_HEADS]


def _rope(x, cos, sin_signed):
    w = x.shape[-1]
    lane = lax.broadcasted_iota(jnp.int32, x.shape, 1)
    swapped = jnp.where((lane % SW_HD) < SW_HD // 2, pltpu.roll(x, w - SW_HD // 2, 1), pltpu.roll(x, SW_HD // 2, 1))
    return x * cos + swapped * sin_signed


def _swa_attend(jobs, sinks_ref, L, precise=False):
    mm, mm_nt, _ = _dots(precise)
    units = [(j, g) for j in range(len(jobs)) for g in range(SW_KV_HEADS)]
    sinks = [jnp.concatenate(
        [jnp.broadcast_to(sinks_ref[:, g * SW_GROUP + i:g * SW_GROUP + i + 1], (L, 1)) for i in range(SW_GROUP)],
        axis=0) for g in range(SW_KV_HEADS)]
    s, p = {}, {}
    for j, g in units:
        qr, keys, _, _ = jobs[j]
        q4 = jnp.concatenate([qr[:, (g * SW_GROUP + i) * SW_HD:(g * SW_GROUP + i + 1) * SW_HD]
                              for i in range(SW_GROUP)], axis=0)
        s[j, g] = mm_nt(q4, keys[:, g * SW_HD:(g + 1) * SW_HD]) * (SW_HD ** -0.5)
    for j, g in units:
        first_valid = jobs[j][3]
        sc = s[j, g]
        if first_valid is not None:
            kcol = lax.broadcasted_iota(jnp.int32, (1, sc.shape[1]), 1)
            sc = jnp.where(kcol >= first_valid, sc, -jnp.inf)
        mx = jnp.maximum(jnp.max(sc, axis=-1, keepdims=True), sinks[g])
        e = jnp.exp(sc - mx)
        p[j, g] = e / (jnp.sum(e, axis=-1, keepdims=True) + jnp.exp(sinks[g] - mx))
    o = {u: mm(p[u], jobs[u[0]][2][:, u[1] * SW_HD:(u[1] + 1) * SW_HD]) for u in units}
    return [jnp.concatenate([o[j, g][i * L:(i + 1) * L, :] for g in range(SW_KV_HEADS) for i in range(SW_GROUP)],
                            axis=-1) for j in range(len(jobs))]


def _swa_prompt_kernel(q_ref, kp_ref, kc_ref, vp_ref, vc_ref, cp_ref, cc_ref, sp_ref, sc_ref, sinks_ref,
                       o_ref, kr_ref, *, L, CB):
    i = pl.program_id(1)
    rows = CB * L
    back = 2 * L
    cos_q = jnp.concatenate([cc_ref[...]] * (SW_HEADS // SW_KV_HEADS), axis=-1)
    sin_q = jnp.concatenate([sc_ref[...]] * (SW_HEADS // SW_KV_HEADS), axis=-1)
    qr = _rope(q_ref[...], cos_q, sin_q)
    k_cur = _rope(kc_ref[...], cc_ref[...], sc_ref[...])
    kr_ref[...] = k_cur
    k_prev = _rope(kp_ref[rows - back:rows, :], cp_ref[rows - back:rows, :], sp_ref[rows - back:rows, :])
    keys = jnp.concatenate([k_prev, k_cur], axis=0)
    vals = jnp.concatenate([vp_ref[rows - back:rows, :], vc_ref[...]], axis=0)
    jobs = []
    for u in range(CB):
        first_valid = jnp.where(i == 0, back - u * L, 0) if u * L < back else None
        jobs.append((qr[u * L:(u + 1) * L], keys[u * L:(u + 3) * L], vals[u * L:(u + 3) * L], first_valid))
    o_ref[...] = jnp.concatenate(_swa_attend(jobs, sinks_ref, L), axis=0)


def _swa_prompt(z_sw, cos_t, sin_t, sinks_row, B, T, L):
    cb = 4
    rows = cb * L
    nb = T // rows
    n = B * T
    kcol = SW_HEADS * SW_HD // LANES
    vcol = kcol + 1
    cur = lambda b, i: (b * nb + i, 0)
    prev = lambda col: (lambda b, i: (b * nb + jnp.maximum(i - 1, 0), col))
    curc = lambda col: (lambda b, i: (b * nb + i, col))
    tab_cur = lambda b, i: (i, 0)
    tab_prev = lambda b, i: (jnp.maximum(i - 1, 0), 0)
    return pl.pallas_call(
        functools.partial(_swa_prompt_kernel, L=L, CB=cb),
        grid=(B, nb),
        in_specs=[pl.BlockSpec((rows, SW_HEADS * SW_HD), cur),
                  pl.BlockSpec((rows, LANES), prev(kcol)), pl.BlockSpec((rows, LANES), curc(kcol)),
                  pl.BlockSpec((rows, LANES), prev(vcol)), pl.BlockSpec((rows, LANES), curc(vcol)),
                  pl.BlockSpec((rows, LANES), tab_prev), pl.BlockSpec((rows, LANES), tab_cur),
                  pl.BlockSpec((rows, LANES), tab_prev), pl.BlockSpec((rows, LANES), tab_cur),
                  pl.BlockSpec((1, LANES), lambda b, i: (0, 0))],
        out_specs=[pl.BlockSpec((rows, SW_HEADS * SW_HD), cur),
                   pl.BlockSpec((rows, LANES), cur)],
        out_shape=[SDS((n, SW_HEADS * SW_HD), f32), SDS((n, LANES), f32)],
        compiler_params=_cparams("parallel", "parallel"),
        name="swa_prompt",
    )(z_sw, z_sw, z_sw, z_sw, z_sw, cos_t, cos_t, sin_t, sin_t, sinks_row)


def _swa_sample_kernel(q_ref, k_ref, v_ref, ck_ref, cv_ref, cos_ref, sin_ref, sinks_ref, o_ref, kr_ref, *, L):
    cos_q = jnp.concatenate([cos_ref[...]] * (SW_HEADS // SW_KV_HEADS), axis=-1)
    sin_q = jnp.concatenate([sin_ref[...]] * (SW_HEADS // SW_KV_HEADS), axis=-1)
    qr = _rope(q_ref[...], cos_q, sin_q)
    kr = _rope(k_ref[...], cos_ref[...], sin_ref[...])
    kr_ref[...] = kr
    keys = jnp.concatenate([ck_ref[0], kr], axis=0)
    vals = jnp.concatenate([cv_ref[0], v_ref[...]], axis=0)
    o_ref[...] = _swa_attend([(qr, keys, vals, None)], sinks_ref, L, precise=True)[0]


def _swa_sample(z_sw, cache_k, cache_v, cos_t, sin_t, sinks_row, B, T):
    n = B * T
    kcol = SW_HEADS * SW_HD // LANES
    return pl.pallas_call(
        functools.partial(_swa_sample_kernel, L=T),
        grid=(B,),
        in_specs=[pl.BlockSpec((T, SW_HEADS * SW_HD), lambda b: (b, 0)),
                  pl.BlockSpec((T, LANES), lambda b: (b, kcol)),
                  pl.BlockSpec((T, LANES), lambda b: (b, kcol + 1)),
                  pl.BlockSpec((1, WINDOW, LANES), lambda b: (b, 0, 0)),
                  pl.BlockSpec((1, WINDOW, LANES), lambda b: (b, 0, 0)),
                  pl.BlockSpec((T, LANES), lambda b: (0, 0)),
                  pl.BlockSpec((T, LANES), lambda b: (0, 0)),
                  pl.BlockSpec((1, LANES), lambda b: (0, 0))],
        out_specs=[pl.BlockSpec((T, SW_HEADS * SW_HD), lambda b: (b, 0)),
                   pl.BlockSpec((T, LANES), lambda b: (b, 0))],
        out_shape=[SDS((n, SW_HEADS * SW_HD), f32), SDS((n, LANES), f32)],
        compiler_params=_cparams("parallel"),
        name="swa_sample",
    )(z_sw, z_sw, z_sw, cache_k, cache_v, cos_t, sin_t, sinks_row)


def _out_ln_kernel(*refs, n_in):
    x_ref = refs[0]
    a_refs = refs[1:1 + n_in]
    w_refs = refs[1 + n_in:1 + 2 * n_in]
    g_ref, b_ref, rw_ref, rb_ref, o_ref, gates_ref, route_ref = refs[1 + 2 * n_in:]
    mm = _dots(w_refs[0].dtype == f32)[0]
    y = mm(a_refs[0][...], w_refs[0][...])
    for a_ref, w_ref in zip(a_refs[1:], w_refs[1:]):
        y = y + mm(a_ref[...], w_ref[...])
    x_new = _layer_norm(DN_ALPHA * x_ref[...] + y, g_ref[...], b_ref[...])
    o_ref[...] = x_new
    gates_ref[...], route_ref[...] = _route(x_new, rw_ref, rb_ref)


def _out_ln(x2, acts, ws, g_row, b_row, rw_t, rb_col, tm):
    n = x2.shape[0]
    row = lambda i: (i, 0)
    const = lambda i: (0, 0)
    col = lambda i: (0, i)
    return pl.pallas_call(
        functools.partial(_out_ln_kernel, n_in=len(acts)),
        grid=(n // tm,),
        in_specs=[pl.BlockSpec((tm, D_MODEL), row)]
        + [pl.BlockSpec((tm, a.shape[1]), row) for a in acts]
        + [pl.BlockSpec(w.shape, const) for w in ws]
        + [pl.BlockSpec((1, D_MODEL), const), pl.BlockSpec((1, D_MODEL), const),
           pl.BlockSpec((N_EXPERTS, D_MODEL), const), pl.BlockSpec((N_EXPERTS, 1), const)],
        out_specs=[pl.BlockSpec((tm, D_MODEL), row), pl.BlockSpec((N_EXPERTS, tm), col), pl.BlockSpec((8, tm), col)],
        out_shape=[SDS((n, D_MODEL), f32), SDS((N_EXPERTS, n), f32), SDS((8, n), f32)],
        compiler_params=_cparams("parallel"),
        name="out_proj_ln",
    )(x2, *acts, *ws, g_row, b_row, rw_t, rb_col)


def _logistic(x):
    return 1.0 / (1.0 + jnp.exp(-x))


def _route(x, rw_ref, rb_ref):
    logits = _dot_nt(rw_ref[...], x)
    aff = _logistic(logits)
    sc = aff + rb_ref[...]
    s = [sc[e:e + 1, :] for e in range(N_EXPERTS)]
    a = [aff[e:e + 1, :] for e in range(N_EXPERTS)]
    scores = []
    for gi in range(N_GROUPS):
        w, x, y, z = s[4 * gi:4 * gi + 4]
        p, q = jnp.maximum(w, x), jnp.minimum(w, x)
        r, t = jnp.maximum(y, z), jnp.minimum(y, z)
        scores.append(jnp.maximum(p, r) + jnp.maximum(jnp.minimum(p, r), jnp.maximum(q, t)))
    best = scores[0]
    gsel = jnp.zeros_like(best, dtype=jnp.int32)
    for gi in range(1, N_GROUPS):
        better = scores[gi] > best
        best = jnp.where(better, scores[gi], best)
        gsel = jnp.where(better, gi, gsel)
    sel = []
    for e in range(N_EXPERTS):
        gi, i = divmod(e, EXP_PER_GROUP)
        beaten = jnp.zeros_like(gsel)
        for j in range(EXP_PER_GROUP):
            if j == i:
                continue
            o = s[4 * gi + j]
            wins = (o >= s[e]) if j < i else (o > s[e])
            beaten = beaten + wins.astype(jnp.int32)
        sel.append((gsel == gi) & (beaten < 2))
    den = jnp.zeros_like(best)
    for e in range(N_EXPERTS):
        den = den + jnp.where(sel[e], a[e], 0.0)
    gate = [jnp.where(sel[e], a[e] / den, 0.0) for e in range(N_EXPERTS)]
    taken = jnp.zeros_like(gsel)
    ea = eb = wa = wb = jnp.zeros_like(best)
    for e in range(N_EXPERTS):
        first = sel[e] & (taken == 0)
        second = sel[e] & (taken == 1)
        ea = jnp.where(first, float(e), ea)
        wa = jnp.where(first, gate[e], wa)
        eb = jnp.where(second, float(e), eb)
        wb = jnp.where(second, gate[e], wb)
        taken = taken + sel[e].astype(jnp.int32)
    route = jnp.concatenate([ea, eb, wa, wb, jnp.zeros((4, ea.shape[1]), f32)], axis=0)
    return jnp.concatenate(gate, axis=0), route


def _moe_kernel(x_ref, gates_ref, wg_ref, wu_ref, wd_ref, g_ref, b_ref, o_ref, xb_ref, acc_ref):
    e = pl.program_id(1)

    @pl.when(e == 0)
    def _():
        xb_ref[...] = x_ref[...].astype(bf16)
        acc_ref[...] = jnp.zeros_like(acc_ref)

    xb = xb_ref[...]
    lane = lax.broadcasted_iota(jnp.int32, gates_ref.shape, 1)
    gcol = jnp.sum(jnp.where(lane == e, gates_ref[...], 0.0), axis=-1, keepdims=True)
    h = _silu(_bdot(xb, wg_ref[0])) * _bdot(xb, wu_ref[0])
    acc_ref[...] += _bdot(gcol * h, wd_ref[0])

    @pl.when(e == N_EXPERTS - 1)
    def _():
        o_ref[...] = _layer_norm(DN_ALPHA * x_ref[...] + acc_ref[...], g_ref[...], b_ref[...])


def _moe_ln(x2, gates, wg, wu, wd, base, g_row, b_row, tm):
    n = x2.shape[0]
    row = lambda i, e: (i, 0)
    const = lambda i, e: (0, 0)
    expert = lambda i, e: (base + e, 0, 0)
    return pl.pallas_call(
        _moe_kernel,
        grid=(n // tm, N_EXPERTS),
        in_specs=[pl.BlockSpec((tm, D_MODEL), row),
                  pl.BlockSpec((tm, N_EXPERTS), row),
                  pl.BlockSpec((1, D_MODEL, D_EXPERT), expert),
                  pl.BlockSpec((1, D_MODEL, D_EXPERT), expert),
                  pl.BlockSpec((1, D_EXPERT, D_MODEL), expert),
                  pl.BlockSpec((1, D_MODEL), const), pl.BlockSpec((1, D_MODEL), const)],
        out_specs=pl.BlockSpec((tm, D_MODEL), row),
        out_shape=SDS((n, D_MODEL), f32),
        scratch_shapes=[pltpu.VMEM((tm, D_MODEL), bf16), pltpu.VMEM((tm, D_MODEL), f32)],
        compiler_params=_cparams("parallel", "arbitrary"),
        name="moe_ln",
    )(x2, gates, wg, wu, wd, g_row, b_row)


N_PAIRS = N_GROUPS * (EXP_PER_GROUP * (EXP_PER_GROUP - 1) // 2)
MOE_TM = 256
MOE_DMA_ROWS = 512
_PAIR_A = [g * EXP_PER_GROUP + a for g in range(N_GROUPS) for a in range(EXP_PER_GROUP) for b in range(a + 1, EXP_PER_GROUP)]
_PAIR_B = [g * EXP_PER_GROUP + b for g in range(N_GROUPS) for a in range(EXP_PER_GROUP) for b in range(a + 1, EXP_PER_GROUP)]


def _gather_rows_kernel(idx_ref, src_ref, o_ref, sem, *, rows):
    base = pl.program_id(0) * rows

    def row_copy(j):
        return pltpu.make_async_copy(src_ref.at[pl.ds(idx_ref[base + j], 1)], o_ref.at[pl.ds(j, 1)], sem)

    def issue(j, carry):
        row_copy(j).start()
        return carry

    def drain(j, carry):
        row_copy(j).wait()
        return carry

    lax.fori_loop(0, rows, issue, 0, unroll=8)
    lax.fori_loop(0, rows, drain, 0, unroll=8)


def _gather_rows(src, idx, rows):
    n_out = idx.shape[0]
    d = src.shape[1]
    return pl.pallas_call(
        functools.partial(_gather_rows_kernel, rows=rows),
        grid_spec=pltpu.PrefetchScalarGridSpec(
            num_scalar_prefetch=1,
            grid=(n_out // rows,),
            in_specs=[pl.BlockSpec(memory_space=pl.ANY)],
            out_specs=pl.BlockSpec((rows, d), lambda i, idx_ref: (i, 0)),
            scratch_shapes=[pltpu.SemaphoreType.DMA(())]),
        out_shape=SDS((n_out, d), f32),
        compiler_params=_cparams("arbitrary"),
        name="gather_rows",
    )(idx, src)


def _scatter_rows_kernel(idx_ref, src_ref, init_ref, o_ref, sem, *, rows):
    del init_ref
    base = pl.program_id(0) * rows

    def row_copy(j):
        return pltpu.make_async_copy(src_ref.at[pl.ds(j, 1)], o_ref.at[pl.ds(idx_ref[base + j], 1)], sem)

    def issue(j, carry):
        row_copy(j).start()
        return carry

    def drain(j, carry):
        row_copy(j).wait()
        return carry

    lax.fori_loop(0, rows, issue, 0, unroll=8)
    lax.fori_loop(0, rows, drain, 0, unroll=8)


def _scatter_rows(src, idx, n_out, rows):
    n_src, d = src.shape
    return pl.pallas_call(
        functools.partial(_scatter_rows_kernel, rows=rows),
        grid_spec=pltpu.PrefetchScalarGridSpec(
            num_scalar_prefetch=1,
            grid=(n_src // rows,),
            in_specs=[pl.BlockSpec((rows, d), lambda i, idx_ref: (i, 0)), pl.BlockSpec(memory_space=pl.ANY)],
            out_specs=pl.BlockSpec(memory_space=pl.ANY),
            scratch_shapes=[pltpu.SemaphoreType.DMA(())]),
        out_shape=SDS((n_out, d), f32),
        input_output_aliases={2: 0},
        compiler_params=_cparams("arbitrary"),
        name="scatter_rows",
    )(idx, src, jnp.zeros((n_out, d), f32))


def _pair_plan(route, n, tm):
    ea = route[0].astype(jnp.int32)
    eb = route[1].astype(jnp.int32)
    a = ea % EXP_PER_GROUP
    b = eb % EXP_PER_GROUP
    pidx = jnp.where(a == 0, b - 1, jnp.where(a == 1, b + 1, 5))
    pair = (ea // EXP_PER_GROUP) * (N_PAIRS // N_GROUPS) + pidx
    onehot = (pair[:, None] == jnp.arange(N_PAIRS, dtype=jnp.int32)[None, :]).astype(jnp.int32)
    csum = jnp.cumsum(onehot, axis=0)
    counts = csum[-1]
    ntiles = (counts + tm - 1) // tm
    tile_end = jnp.cumsum(ntiles)
    tile_start = tile_end - ntiles
    row_of_token = jnp.sum(onehot * (csum - 1 + (tile_start * tm)[None, :]), axis=1)
    nt = n // tm + N_PAIRS
    tile_id = jnp.arange(nt, dtype=jnp.int32)
    tile_valid = tile_id < tile_end[-1]
    tile_pair = jnp.sum((tile_end[None, :] <= jnp.minimum(tile_id, tile_end[-1] - 1)[:, None]).astype(jnp.int32), axis=1)
    tile_pair = jnp.minimum(tile_pair, N_PAIRS - 1)
    pick = (tile_pair[:, None] == jnp.arange(N_PAIRS, dtype=jnp.int32)[None, :]).astype(jnp.int32)
    tile_a = jnp.sum(pick * jnp.asarray(_PAIR_A, jnp.int32)[None, :], axis=1)
    tile_b = jnp.sum(pick * jnp.asarray(_PAIR_B, jnp.int32)[None, :], axis=1)
    return row_of_token, tile_a, tile_b, tile_valid.astype(jnp.int32)


def _pair_expert_kernel(ta_ref, tb_ref, tv_ref, x_ref, rwt_ref, wga_ref, wua_ref, wda_ref, wgb_ref, wub_ref, wdb_ref,
                        g_ref, b_ref, o_ref):
    i = pl.program_id(0)
    valid = tv_ref[i] == 1

    @pl.when(valid)
    def _():
        x = x_ref[...]
        xb = x.astype(bf16)
        aff_a = _logistic(jnp.sum(x * rwt_ref[pl.ds(ta_ref[i], 1), :], axis=-1, keepdims=True))
        aff_b = _logistic(jnp.sum(x * rwt_ref[pl.ds(tb_ref[i], 1), :], axis=-1, keepdims=True))
        den = aff_a + aff_b
        acc = None
        for w, (wg, wu, wd) in ((aff_a / den, (wga_ref, wua_ref, wda_ref)), (aff_b / den, (wgb_ref, wub_ref, wdb_ref))):
            h = _silu(_bdot(xb, wg[0])) * _bdot(xb, wu[0])
            y = _bdot(w * h, wd[0])
            acc = y if acc is None else acc + y
        o_ref[...] = _layer_norm(DN_ALPHA * x + acc, g_ref[...], b_ref[...])

    @pl.when(jnp.logical_not(valid))
    def _():
        o_ref[...] = jnp.zeros_like(o_ref)


def _pair_experts(xs, rw_t, tile_a, tile_b, tile_valid, wg, wu, wd, base, g_row, b_row, tm):
    rows = xs.shape[0]
    row = lambda i, ta, tb, tv: (i, 0)
    const = lambda i, ta, tb, tv: (0, 0)
    ex_a = lambda i, ta, tb, tv: (base + ta[i], 0, 0)
    ex_b = lambda i, ta, tb, tv: (base + tb[i], 0, 0)
    up = pl.BlockSpec((1, D_MODEL, D_EXPERT), ex_a), pl.BlockSpec((1, D_MODEL, D_EXPERT), ex_b)
    down = pl.BlockSpec((1, D_EXPERT, D_MODEL), ex_a), pl.BlockSpec((1, D_EXPERT, D_MODEL), ex_b)
    return pl.pallas_call(
        _pair_expert_kernel,
        grid_spec=pltpu.PrefetchScalarGridSpec(
            num_scalar_prefetch=3,
            grid=(rows // tm,),
            in_specs=[pl.BlockSpec((tm, D_MODEL), row), pl.BlockSpec((N_EXPERTS, D_MODEL), const),
                      up[0], up[0], down[0], up[1], up[1], down[1],
                      pl.BlockSpec((1, D_MODEL), const), pl.BlockSpec((1, D_MODEL), const)],
            out_specs=pl.BlockSpec((tm, D_MODEL), row)),
        out_shape=SDS((rows, D_MODEL), f32),
        compiler_params=_cparams("arbitrary"),
        name="pair_experts",
    )(tile_a, tile_b, tile_valid, xs, rw_t, wg, wu, wd, wg, wu, wd, g_row, b_row)


def _unit_lower_inverses(mats, L):
    row = lax.broadcasted_iota(jnp.int32, (L, L), 0)
    col = lax.broadcasted_iota(jnp.int32, (L, L), 1)
    eye = (row == col).astype(f32)
    ps = [eye - a for a in mats]
    pws = [a.astype(bf16) for a in mats]
    span = 2
    while span < L:
        pws = [jnp.dot(pw, pw, preferred_element_type=f32).astype(bf16) for pw in pws]
        ps = [p + jnp.dot(pw, p.astype(bf16), preferred_element_type=f32) for p, pw in zip(ps, pws)]
        span *= 2
    return ps


def _gdn_kernel(x_ref, zg_ref, ba_ref, cw_ref, alog_ref, dt_ref, nw_ref, s0_ref, cb_ref,
                o_ref, s_ref, prev_ref, *, L, BB):
    @pl.when(pl.program_id(1) == 0)
    def _():
        s_ref[...] = s0_ref[...]
        prev_ref[...] = cb_ref[...]

    def conv_silu(bi, lo, width):
        cur = x_ref[bi, :, lo:lo + width]
        cat =jnp.concatenate([prev_ref[bi, :, lo:lo + width], cur], axis=0)
        acc = cat[5:5 + L] * cw_ref[0:1, lo:lo + width]
        acc = acc + cat[6:6 + L] * cw_ref[1:2, lo:lo + width]
        acc = acc + cat[7:7 + L] * cw_ref[2:3, lo:lo + width]
        acc = acc + cur * cw_ref[3:4, lo:lo + width]
        return _silu(acc)

    def l2n(v, scale):
        return v * lax.rsqrt(jnp.sum(v * v, axis=-1, keepdims=True) + 1e-6) * scale

    row = lax.broadcasted_iota(jnp.int32, (L, L), 0)
    col = lax.broadcasted_iota(jnp.int32, (L, L), 1)
    incl = row >= col
    strict = row > col
    rep = GD_V_HEADS // GD_QK_HEADS
    bas = [ba_ref[bi] for bi in range(BB)]
    betas = [_sigmoid(ba) for ba in bas]
    gls = [-jnp.exp(alog_ref[...]) * _softplus(ba + dt_ref[...]) for ba in bas]
    gcum_all = _dot(incl.astype(f32), jnp.concatenate(gls, axis=1))
    gcums = [gcum_all[:, bi * LANES:(bi + 1) * LANES] for bi in range(BB)]
    gcum_ts = [g.T for g in gcums]
    units = [(bi, hv) for bi in range(BB) for hv in range(GD_V_HEADS)]
    qs, ks, amats, qkds, egs, g_cols, rhss = {}, {}, [], {}, {}, {}, []
    for bi in range(BB):
        for j in range(GD_QK_HEADS):
            q = l2n(conv_silu(bi, j * GD_HD, GD_HD), GD_HD ** -0.5)
            k = l2n(conv_silu(bi, GD_QK_W + j * GD_HD, GD_HD), 1.0)
            qb, kb = q.astype(bf16), k.astype(bf16)
            kk = lax.dot_general(kb, kb, (((1,), (1,)), ((), ())), preferred_element_type=f32)
            qk = lax.dot_general(qb, kb, (((1,), (1,)), ((), ())), preferred_element_type=f32)
            qs[bi, j] = qb
            ks[bi, j] = k
            for r in range(rep):
                hv = j * rep + r
                v = conv_silu(bi, 2 * GD_QK_W + hv * GD_HD, GD_HD)
                g_col = gcums[bi][:, GD_V_HEADS + hv:GD_V_HEADS + hv + 1]
                g_row = gcum_ts[bi][GD_V_HEADS + hv:GD_V_HEADS + hv + 1, :]
                b_col = betas[bi][:, hv:hv + 1]
                decay = jnp.exp(jnp.where(incl, g_col - g_row, -jnp.inf))
                eg = jnp.exp(g_col)
                amats.append(jnp.where(strict, b_col * kk * decay, 0.0))
                qkds[bi, hv] = (qk * decay).astype(bf16)
                egs[bi, hv] = eg
                g_cols[bi, hv] = g_col
                rhss.append(jnp.concatenate([b_col * v, (b_col * eg) * k], axis=-1))
    tinvs = _unit_lower_inverses(amats, L)
    sols = dict(zip(units, [_bdot(t, r) for t, r in zip(tinvs, rhss)]))
    sts, wks, qss, kts = {}, {}, {}, {}
    for u in units:
        bi, hv = u
        j = hv // rep
        sts[u] = s_ref[bi, hv]
        stb = sts[u].astype(bf16)
        wks[u] = jnp.dot(sols[u][:, GD_HD:].astype(bf16), stb, preferred_element_type=f32)
        qss[u] = jnp.dot(qs[bi, j], stb, preferred_element_type=f32)
        g_col = g_cols[u]
        kts[u] = (jnp.exp(g_col[L - 1:L, :] - g_col) * ks[bi, j]).T.astype(bf16)
    outs = {}
    for u in units:
        bi, hv = u
        wnb = (sols[u][:, :GD_HD] - wks[u]).astype(bf16)
        outs[u] = egs[u] * qss[u] + jnp.dot(qkds[u], wnb, preferred_element_type=f32)
        s_ref[bi, hv] = jnp.exp(g_cols[u][L - 1:L, :]) * sts[u] + jnp.dot(kts[u], wnb, preferred_element_type=f32)
    for u in units:
        bi, hv = u
        o = outs[u]
        o = o * lax.rsqrt(jnp.mean(o * o, axis=-1, keepdims=True) + RMS_EPS) * nw_ref[...]
        o_ref[bi, :, hv * GD_HD:(hv + 1) * GD_HD] = o * _silu(zg_ref[bi, :, hv * GD_HD:(hv + 1) * GD_HD])
    for bi in range(BB):
        prev_ref[bi] = x_ref[bi, L - 8:L, :]


def _gdn(qkv, zg, ba, conv_w, alog_row, dt_row, norm_row, s0, conv8, B, T, L):
    nc = T // L
    n = B * T
    bb = 2
    tok = lambda b, c: (b, c, 0)
    const = lambda b, c: (0, 0)
    st4 = lambda b, c: (b, 0, 0, 0)
    o, s_out = pl.pallas_call(
        functools.partial(_gdn_kernel, L=L, BB=bb),
        grid=(B // bb, nc),
        in_specs=[pl.BlockSpec((bb, L, GD_CONV_CH), tok),
                  pl.BlockSpec((bb, L, GD_V_W), tok),
                  pl.BlockSpec((bb, L, LANES), tok),
                  pl.BlockSpec((GD_CONV, GD_CONV_CH), const),
                  pl.BlockSpec((1, LANES), const),
                  pl.BlockSpec((1, LANES), const),
                  pl.BlockSpec((1, GD_HD), const),
                  pl.BlockSpec((bb, GD_V_HEADS, GD_HD, GD_HD), st4),
                  pl.BlockSpec((bb, 8, GD_CONV_CH), lambda b, c: (b, 0, 0))],
        out_specs=[pl.BlockSpec((bb, L, GD_V_W), tok),
                   pl.BlockSpec((bb, GD_V_HEADS, GD_HD, GD_HD), st4)],
        out_shape=[SDS((B, T, GD_V_W), f32), SDS((B, GD_V_HEADS, GD_HD, GD_HD), f32)],
        scratch_shapes=[pltpu.VMEM((bb, 8, GD_CONV_CH), f32)],
        compiler_params=_cparams("parallel", "arbitrary"),
        name="gdn",
    )(qkv.reshape(B, T, GD_CONV_CH), zg.reshape(B, T, GD_V_W), ba.reshape(B, T, LANES), conv_w, alog_row, dt_row,
      norm_row, s0, conv8)
    return o.reshape(n, GD_V_W), s_out


def _pad_lanes(row, offset=0):
    return jnp.zeros((1, LANES), f32).at[0, offset:offset + row.shape[0]].set(row.astype(f32))


def _rope_tables(pos):
    half = SW_HD // 2
    inv = ROPE_THETA ** (-jnp.arange(half, dtype=f32) / half)
    ang = pos.astype(f32)[:, None] * inv[None, :]
    cos, sin = jnp.cos(ang), jnp.sin(ang)
    cos_t = jnp.concatenate([cos, cos] * SW_KV_HEADS, axis=-1)
    sin_t = jnp.concatenate([-sin, sin] * SW_KV_HEADS, axis=-1)
    return cos_t, sin_t


def _tile(n, pref):
    return pref if n % pref == 0 else n


def _trunk(x, pos, L, state, p):
    B, T, _ = x.shape
    n = B * T
    x2 = x.reshape(n, D_MODEL)
    tm = _tile(n, 512)
    precise = state is not None
    ab_w, wo_h, wo_a = (p["ab_w32"], p["ab_wo_h32"], p["ab_wo_a32"]) if precise else (p["ab_w"], p["ab_wo_h"], p["ab_wo_a"])

    z_ml, z_sw, z_g = _proj(x2, ab_w, ((0, ML_W), (ML_W, SW_W), (ML_W + SW_W, LANES)), tm)
    if state is None:
        h_ml, ml_c, ml_n, ml_m = _mlstm_pairs(z_ml, z_g, p["ab_bias"], p["ab_norm"], B, T, L)
    else:
        cn0 = jnp.concatenate([state["ml_C"], state["ml_n"][..., None],
                               jnp.zeros((B, ML_HEADS, ML_DK, LANES - ML_DV - 1), f32)], axis=-1)
        m0 = jnp.zeros((B, 1, LANES), f32).at[:, 0, :ML_HEADS].set(state["ml_m"])
        h_ml, cn, m_out = _mlstm(z_ml, z_g, p["ab_bias"], p["ab_norm"], cn0, m0, B, T, L, precise)
        ml_c, ml_n, ml_m = cn[..., :ML_DV], cn[..., ML_DV], m_out[:, 0, :ML_HEADS]
    h_ml = h_ml.reshape(n, ML_HEADS * ML_DV)
    cos_t, sin_t = _rope_tables(pos)
    if state is None:
        a_sw, k_rot = _swa_prompt(z_sw, cos_t, sin_t, p["ab_sinks"], B, T, L)
    else:
        a_sw, k_rot = _swa_sample(z_sw, state["sw_k"].reshape(B, WINDOW, LANES),
                                  state["sw_v"].reshape(B, WINDOW, LANES), cos_t, sin_t, p["ab_sinks"], B, T)
    keep = min(T, WINDOW)
    new_k = k_rot.reshape(B, T, LANES)[:, T - keep:].reshape(B, keep, SW_KV_HEADS, SW_HD)
    new_v = z_sw.reshape(B, T, SW_W)[:, T - keep:, SW_HEADS * SW_HD + LANES:].reshape(B, keep, SW_KV_HEADS, SW_HD)
    x2, gates_t, route = _out_ln(x2, [h_ml, a_sw], [wo_h, wo_a], p["ln_g"][0][0], p["ln_b"][0][0],
                                 p["router_wt"], p["router_b"], tm)
    x2 = _moe_block(x2, gates_t, route, p, 0)

    tm1 = _tile(n, 256)
    if state is None:
        s0 = jnp.zeros((B, GD_V_HEADS, GD_HD, GD_HD), f32)
        conv8 = jnp.zeros((B, 8, GD_CONV_CH), f32)
    else:
        s0 = state["gd_S"]
        conv8 = jnp.concatenate([jnp.zeros((B, 8 - (GD_CONV - 1), GD_CONV_CH), f32), state["gd_conv"]], axis=1)
    qkv, zg, ba = _proj(x2, p["c_w"], ((0, GD_CONV_CH), (GD_CONV_CH, GD_V_W), (GD_CONV_CH + GD_V_W, LANES)), tm1)
    new_conv = qkv.reshape(B, T, GD_CONV_CH)[:, T - (GD_CONV - 1):]
    o_gd, s_out = _gdn(qkv, zg, ba, p["c_conv_w"], p["c_alog"], p["c_dt"], p["c_norm"], s0, conv8, B, T, L)
    x2, gates_t, route = _out_ln(x2, [o_gd], [p["c_wo"]], p["ln_g"][1][0], p["ln_b"][1][0],
                                 p["router_wt"], p["router_b"], tm)
    x2 = _moe_block(x2, gates_t, route, p, 1)

    outs = (new_k[None], new_v[None], ml_c[None], ml_n[None], ml_m[None],
            s_out[None], new_conv[None])
    return x2.reshape(B, T, D_MODEL), outs


def _moe_block(x2, gates_t, route, p, layer):
    n = x2.shape[0]
    wg, wu, wd = p["ex_gate"], p["ex_up"], p["ex_down"]
    base = layer * N_EXPERTS
    g_row, b_row = p["ln_g"][layer][1], p["ln_b"][layer][1]
    if n < N_PAIRS * MOE_TM:
        return _moe_ln(x2, gates_t.T, wg, wu, wd, base, g_row, b_row, _tile(n, 1024))
    row_of_token, tile_a, tile_b, tile_valid = _pair_plan(route, n, MOE_TM)
    xs = _scatter_rows(x2, row_of_token, n + N_PAIRS * MOE_TM, MOE_DMA_ROWS)
    ys = _pair_experts(xs, p["router_wt"], tile_a, tile_b, tile_valid, wg, wu, wd, base, g_row, b_row, MOE_TM)
    return _gather_rows(ys, row_of_token, MOE_DMA_ROWS)


def kernel(x_prompt, x_sample, cache_swa_k, cache_swa_v, state_mlstm_C, state_mlstm_n, state_mlstm_m, state_gdn_S, state_gdn_conv, ab_w_in, ab_b_i, ab_b_f, ab_norm, ab_sinks, ab_w_out, c_w_in, c_conv_w, c_a_log, c_dt_bias, c_norm, c_w_out, ln_g, ln_b, router_w, router_b, ex_gate, ex_up, ex_down):
    gate_lo = ML_W
    sw_lo = ML_W + 2 * ML_HEADS
    w0 = ab_w_in[0]
    ab_w32 = jnp.concatenate([w0[:, :gate_lo], w0[:, sw_lo:], w0[:, gate_lo:sw_lo],
                              jnp.zeros((D_MODEL, LANES - 2 * ML_HEADS), f32)], axis=1)
    w1 = c_w_in[0]
    c_w = jnp.concatenate([w1, jnp.zeros((D_MODEL, LANES - 2 * GD_V_HEADS), f32)], axis=1).astype(bf16)
    wo32 = ab_w_out[0]
    wo = wo32.astype(bf16)
    p = {
        "ab_w": ab_w32.astype(bf16),
        "ab_w32": ab_w32,
        "ab_wo_h32": wo32[:ML_HEADS * ML_DV],
        "ab_wo_a32": wo32[ML_HEADS * ML_DV:],
        "ab_bias": _pad_lanes(jnp.concatenate([ab_b_i[0], ab_b_f[0]])),
        "ab_norm": ab_norm[0].reshape(1, ML_HEADS * ML_DV),
        "ab_sinks": _pad_lanes(ab_sinks[0]),
        "ab_wo_h": wo[:ML_HEADS * ML_DV],
        "ab_wo_a": wo[ML_HEADS * ML_DV:],
        "c_w": c_w,
        "c_conv_w": c_conv_w[0],
        "c_alog": _pad_lanes(c_a_log[0], GD_V_HEADS),
        "c_dt": _pad_lanes(c_dt_bias[0], GD_V_HEADS),
        "c_norm": c_norm[0].reshape(1, GD_HD),
        "c_wo": c_w_out[0].astype(bf16),
        "ln_g": [[ln_g[i, j].reshape(1, D_MODEL) for j in range(2)] for i in range(DEPTH)],
        "ln_b": [[ln_b[i, j].reshape(1, D_MODEL) for j in range(2)] for i in range(DEPTH)],
        "router_wt": router_w.T,
        "router_b": router_b.reshape(N_EXPERTS, 1),
        "ex_gate": ex_gate.reshape(DEPTH * N_EXPERTS, D_MODEL, D_EXPERT),
        "ex_up": ex_up.reshape(DEPTH * N_EXPERTS, D_MODEL, D_EXPERT),
        "ex_down": ex_down.reshape(DEPTH * N_EXPERTS, D_EXPERT, D_MODEL),
    }
    t_p = x_prompt.shape[1]
    y_p, st_p = _trunk(x_prompt, jnp.arange(t_p, dtype=jnp.int32), CHUNK, None, p)
    t_s = x_sample.shape[1]
    state = {"sw_k": cache_swa_k[0], "sw_v": cache_swa_v[0], "ml_C": state_mlstm_C[0], "ml_n": state_mlstm_n[0],
             "ml_m": state_mlstm_m[0], "gd_S": state_gdn_S[0], "gd_conv": state_gdn_conv[0]}
    y_s, st_s = _trunk(x_sample, PAST_LEN + jnp.arange(t_s, dtype=jnp.int32), t_s, state, p)
    return (y_p, y_s) + st_p + st_s
```

```python
import functools
import math

import jax
import jax.numpy as jnp
import numpy as np
from jax import lax
from jax.experimental import pallas as pl
from jax.experimental.pallas import tpu as pltpu

f32 = jnp.float32
bf16 = jnp.bfloat16
HIGHEST = lax.Precision.HIGHEST

D_MODEL = 1024
DEPTH = 2
CHUNK = 64
PAST_LEN = 2048
ML_HEADS = 8
ML_DK = 64
ML_DV = 64
SW_HEADS = 8
SW_KV_HEADS = 2
SW_HD = 64
SW_GROUP = SW_HEADS // SW_KV_HEADS
WINDOW = 128
ROPE_THETA = 10000.0
GD_QK_HEADS = 8
GD_V_HEADS = 16
GD_HD = 128
GD_CONV = 4
GD_QK_W = GD_QK_HEADS * GD_HD
GD_V_W = GD_V_HEADS * GD_HD
GD_CONV_CH = 2 * GD_QK_W + GD_V_W
N_EXPERTS = 16
N_GROUPS = 4
EXP_PER_GROUP = 4
D_EXPERT = 512
DN_ALPHA = (2 * DEPTH) ** 0.25
LN_EPS = 1e-5
RMS_EPS = 1e-6

LANES = 128
ML_W = 4 * ML_HEADS * ML_DK
SW_W = SW_HEADS * SW_HD + 2 * SW_KV_HEADS * SW_HD
VMEM_LIMIT = 56 * 1024 * 1024

SDS = jax.ShapeDtypeStruct


def _cparams(*sem):
    return pltpu.CompilerParams(dimension_semantics=sem, vmem_limit_bytes=VMEM_LIMIT)


def _dot(a, b):
    return jnp.dot(a, b, preferred_element_type=f32, precision=HIGHEST)


def _dot_nt(a, b):
    return lax.dot_general(a, b, (((1,), (1,)), ((), ())), preferred_element_type=f32, precision=HIGHEST)


def _bdot(a, b):
    return jnp.dot(a.astype(bf16), b.astype(bf16), preferred_element_type=f32)


def _bdot_nt(a, b):
    return lax.dot_general(a.astype(bf16), b.astype(bf16), (((1,), (1,)), ((), ())), preferred_element_type=f32)


def _bdot_tn(a, b):
    return jnp.dot(a.T.astype(bf16), b.astype(bf16), preferred_element_type=f32)


def _dots(precise):
    if precise:
        return _dot, _dot_nt, lambda a, b: _dot(a.T, b)
    return _bdot, _bdot_nt, _bdot_tn


def _sigmoid(x):
    return 0.5 + 0.5 * jnp.tanh(0.5 * x)


def _silu(x):
    hx = 0.5 * x
    return hx + hx * jnp.tanh(hx)


def _softplus(x):
    return jnp.maximum(x, 0.0) + jnp.log(1.0 + jnp.exp(-jnp.abs(x)))


def _layer_norm(v, g, b):
    mu = jnp.mean(v, axis=-1, keepdims=True)
    d = v - mu
    var = jnp.mean(d * d, axis=-1, keepdims=True)
    return d * lax.rsqrt(var + LN_EPS) * g + b


def _proj_kernel(x_ref, w_ref, *o_refs, splits, col_chunk, precise):
    mm = _dots(precise)[0]
    xb = x_ref[...] if precise else x_ref[...].astype(bf16)
    for o_ref, (start, width) in zip(o_refs, splits):
        for c in range(0, width, col_chunk):
            cw = min(col_chunk, width - c)
            o_ref[:, c:c + cw] = mm(xb, w_ref[:, start + c:start + c + cw])


def _proj(x2, w, splits, tm):
    n, k = x2.shape
    return pl.pallas_call(
        functools.partial(_proj_kernel, splits=splits, col_chunk=512, precise=w.dtype == f32),
        grid=(n // tm,),
        in_specs=[pl.BlockSpec((tm, k), lambda i: (i, 0)),
                  pl.BlockSpec(w.shape, lambda i: (0, 0), pipeline_mode=pl.Buffered(1))],
        out_specs=[pl.BlockSpec((tm, wd), lambda i: (i, 0)) for _, wd in splits],
        out_shape=[SDS((n, wd), f32) for _, wd in splits],
        compiler_params=_cparams("parallel"),
        name="in_proj",
    )(x2, w)


def _mlstm_kernel(z_ref, g_ref, bias_ref, nw_ref, cn0_ref, m0_ref, h_ref, cn_ref, m_ref, *, L, BB, precise):
    @pl.when(pl.program_id(1) == 0)
    def _():
        cn_ref[...] = cn0_ref[...]
        m_ref[...] = m0_ref[...]

    mm, mm_nt, mm_tn = _dots(precise)
    row = lax.broadcasted_iota(jnp.int32, (L, L), 0)
    col = lax.broadcasted_iota(jnp.int32, (L, L), 1)
    causal = row >= col
    tri = causal.astype(f32)
    lane = lax.broadcasted_iota(jnp.int32, (1, LANES), 1)
    lane_l = lax.broadcasted_iota(jnp.int32, (L, ML_DV), 1)
    one_hot0 = (lane_l == 0).astype(f32)
    for bi in range(BB):
        g = g_ref[bi] + bias_ref[...]
        lf = jnp.minimum(g, 0.0) - jnp.log(1.0 + jnp.exp(-jnp.abs(g)))
        bcum = _dot(tri, lf)
        b_t = bcum.T
        g_t = g.T
        m_row = m_ref[bi]
        new_m = m_row
        outs = []
        for h in range(ML_HEADS):
            b_col = bcum[:, ML_HEADS + h:ML_HEADS + h + 1]
            b_row = b_t[ML_HEADS + h:ML_HEADS + h + 1, :]
            ig_row = g_t[h:h + 1, :]
            ig_col = g[:, h:h + 1]
            m_h = m_row[:, h:h + 1]
            dmat = jnp.where(causal, b_col - b_row + ig_row, -jnp.inf)
            inter = b_col + m_h
            mt = jnp.maximum(inter, jnp.max(dmat, axis=-1, keepdims=True))
            a = jnp.exp(inter - mt)
            q = z_ref[bi, :, h * ML_DK:(h + 1) * ML_DK]
            k = z_ref[bi, :, ML_HEADS * ML_DK + h * ML_DK:ML_HEADS * ML_DK + (h + 1) * ML_DK] * (ML_DK ** -0.5)
            v = z_ref[bi, :, 2 * ML_HEADS * ML_DK + h * ML_DV:2 * ML_HEADS * ML_DK + (h + 1) * ML_DV]
            og = z_ref[bi, :, 3 * ML_HEADS * ML_DK + h * ML_DV:3 * ML_HEADS * ML_DK + (h + 1) * ML_DV]
            s = mm_nt(q, k) * jnp.exp(dmat - mt)
            vext = jnp.concatenate([v, one_hot0], axis=-1)
            cn = cn_ref[bi, h]
            tot = a * mm(q, cn) + mm(s, vext)
            num = tot[:, :ML_DV]
            den = tot[:, ML_DV:ML_DV + 1]
            hh = num / jnp.maximum(jnp.abs(den), jnp.exp(-mt))
            hh = hh * lax.rsqrt(jnp.mean(hh * hh, axis=-1, keepdims=True) + RMS_EPS) * nw_ref[:, h * ML_DV:(h + 1) * ML_DV]
            outs.append(hh * _sigmoid(og))
            m_new = mt[L - 1:L, :]
            b_last = b_col[L - 1:L, :]
            wk = jnp.exp(b_last - b_col + ig_col - m_new)
            dec = jnp.exp(b_last + m_h - m_new)
            cn_ref[bi, h] = dec * cn + mm_tn(k, wk * vext)
            new_m = jnp.where(lane == h, m_new, new_m)
        m_ref[bi] = new_m
        h_ref[bi] = jnp.concatenate(outs, axis=-1)


def _mlstm(z_ml, z_g, bias_row, norm_row, cn0, m0, B, T, L, precise):
    nc = T // L
    bb = min(B, 4)
    tok = lambda b, c: (b, c, 0)
    st4 = lambda b, c: (b, 0, 0, 0)
    st3 = lambda b, c: (b, 0, 0)
    return pl.pallas_call(
        functools.partial(_mlstm_kernel, L=L, BB=bb, precise=precise),
        grid=(B // bb, nc),
        in_specs=[pl.BlockSpec((bb, L, ML_W), tok),
                  pl.BlockSpec((bb, L, LANES), tok),
                  pl.BlockSpec((1, LANES), lambda b, c: (0, 0)),
                  pl.BlockSpec((1, ML_HEADS * ML_DV), lambda b, c: (0, 0)),
                  pl.BlockSpec((bb, ML_HEADS, ML_DK, LANES), st4),
                  pl.BlockSpec((bb, 1, LANES), st3)],
        out_specs=[pl.BlockSpec((bb, L, ML_HEADS * ML_DV), tok),
                   pl.BlockSpec((bb, ML_HEADS, ML_DK, LANES), st4),
                   pl.BlockSpec((bb, 1, LANES), st3)],
        out_shape=[SDS((B, T, ML_HEADS * ML_DV), f32),
                   SDS((B, ML_HEADS, ML_DK, LANES), f32),
                   SDS((B, 1, LANES), f32)],
        compiler_params=_cparams("parallel", "arbitrary"),
        name="mlstm",
    )(z_ml.reshape(B, T, ML_W), z_g.reshape(B, T, LANES), bias_row, norm_row, cn0, m0)


ML_PAIRS = ML_HEADS // 2
ML_REP_QUANTS = 3


def _pair_select_matrix():
    sel = np.zeros((LANES, ML_REP_QUANTS * ML_PAIRS * LANES), np.float32)
    for qn in range(ML_REP_QUANTS):
        for pr in range(ML_PAIRS):
            for half in range(2):
                lo = (qn * ML_PAIRS + pr) * LANES + half * ML_DV
                sel[ML_HEADS * qn + 2 * pr + half, lo:lo + ML_DV] = 1.0
    return jnp.asarray(sel, bf16)


def _exact_select(x, sel):
    hi = x.astype(bf16)
    r1 = x - hi.astype(f32)
    mid = r1.astype(bf16)
    lo = (r1 - mid.astype(f32)).astype(bf16)
    mm = lambda t: jnp.dot(t, sel, preferred_element_type=f32)
    return (mm(hi) + mm(mid)) + mm(lo)


def _mlstm_pair_kernel(z_ref, g_ref, bias_ref, nw_ref, sel_ref, cbd0_ref, nbd0_ref, m0_ref,
                       h_ref, cbd_ref, nbd_ref, m_ref, *, L, BB):
    @pl.when(pl.program_id(1) == 0)
    def _():
        cbd_ref[...] = cbd0_ref[...]
        nbd_ref[...] = nbd0_ref[...]
        m_ref[...] = m0_ref[...]

    tri = (lax.broadcasted_iota(jnp.int32, (L, L), 0) >= lax.broadcasted_iota(jnp.int32, (L, L), 1)).astype(f32)
    row_t = lax.broadcasted_iota(jnp.int32, (L, LANES), 0)
    lane_t = lax.broadcasted_iota(jnp.int32, (L, LANES), 1)
    first_half = lane_t < ML_DV
    causal2 = row_t >= (lane_t % ML_DV)
    rr = lax.broadcasted_iota(jnp.int32, (LANES, LANES), 0)
    cc = lax.broadcasted_iota(jnp.int32, (LANES, LANES), 1)
    same_block = (rr < ML_DV) == (cc < ML_DV)
    ones_bd = same_block.astype(bf16)
    lane_1 = lax.broadcasted_iota(jnp.int32, (1, LANES), 1)
    sel = sel_ref[...]
    n_tiles = ML_REP_QUANTS * ML_PAIRS

    gs = [g_ref[bi] + bias_ref[...] for bi in range(BB)]
    lfs = [jnp.minimum(g, 0.0) - jnp.log(1.0 + jnp.exp(-jnp.abs(g))) for g in gs]
    bc_all = _dot(tri, jnp.concatenate(lfs, axis=1))
    g_ts = [g.T for g in gs]
    b_ts = [bc_all[:, bi * LANES:(bi + 1) * LANES].T for bi in range(BB)]
    r_rows = [g_ts[bi][0:ML_HEADS] - b_ts[bi][ML_HEADS:2 * ML_HEADS] for bi in range(BB)]
    cm = jnp.concatenate([jnp.concatenate(r_rows, axis=0), jnp.full((BB * ML_HEADS, LANES - L), -jnp.inf, f32)], axis=1)
    shift = 1
    while shift < L:
        cm = jnp.maximum(cm, pltpu.roll(cm, shift, 1))
        shift *= 2
    cols = [jnp.concatenate([g_ts[bi][0:ML_HEADS], b_ts[bi][ML_HEADS:2 * ML_HEADS],
                             cm[bi * ML_HEADS:(bi + 1) * ML_HEADS, :L],
                             jnp.zeros((LANES - 3 * ML_HEADS, L), f32)], axis=0).T for bi in range(BB)]
    rep_all = _exact_select(jnp.concatenate(cols, axis=0), sel)
    rep = [rep_all[bi * L:(bi + 1) * L] for bi in range(BB)]
    m_all = jnp.concatenate([m_ref[bi] for bi in range(BB)] + [jnp.zeros((8 - BB, LANES), f32)], axis=0)
    m_rep_all = _exact_select(m_all, sel[:, :ML_PAIRS * LANES])
    m_rep = [m_rep_all[bi:bi + 1] for bi in range(BB)]

    units = [(bi, pr) for bi in range(BB) for pr in range(ML_PAIRS)]
    st = {}
    for u in units:
        bi, pr = u
        tile = lambda qn: rep[bi][:, (qn * ML_PAIRS + pr) * LANES:(qn * ML_PAIRS + pr + 1) * LANES]
        ig_rep, b_rep, cm_rep = tile(0), tile(1), tile(2)
        m_pair = m_rep[bi][:, pr * LANES:(pr + 1) * LANES]
        inter = b_rep + m_pair
        mt = jnp.maximum(inter, b_rep + cm_rep)
        r_row = jnp.concatenate([r_rows[bi][2 * pr:2 * pr + 1], r_rows[bi][2 * pr + 1:2 * pr + 2]], axis=1)
        e = jnp.exp(jnp.where(causal2, (b_rep - mt) + r_row, -jnp.inf))
        m_new = mt[L - 1:L]
        b_last = b_rep[L - 1:L]
        lo = pr * LANES
        q = z_ref[bi, :, lo:lo + LANES].astype(bf16)
        k = z_ref[bi, :, ML_HEADS * ML_DK + lo:ML_HEADS * ML_DK + lo + LANES] * (ML_DK ** -0.5)
        v = z_ref[bi, :, 2 * ML_HEADS * ML_DK + lo:2 * ML_HEADS * ML_DK + lo + LANES]
        kbd = jnp.concatenate([jnp.where(first_half, k, 0.0), jnp.where(first_half, 0.0, k)], axis=0).astype(bf16)
        vbd = jnp.concatenate([jnp.where(first_half, v, 0.0), jnp.where(first_half, 0.0, v)], axis=0).astype(bf16)
        wk = jnp.exp(b_last - b_rep + ig_rep - m_new)
        st[u] = dict(a=jnp.exp(inter - mt), em=jnp.exp(-mt), e=e, m_new=m_new, dec=jnp.exp(b_last + m_pair - m_new),
                     q=q, kbd=kbd, vbd=vbd, k_t=k.T.astype(bf16), wkv=(wk * v).astype(bf16), wk=wk.astype(bf16),
                     cbd=cbd_ref[bi, pr], nbd=nbd_ref[bi, pr])
    for u in units:
        d = st[u]
        d["qk"] = lax.dot_general(d["q"], d["kbd"], (((1,), (1,)), ((), ())), preferred_element_type=f32)
        d["qc"] = jnp.dot(d["q"], d["cbd"].astype(bf16), preferred_element_type=f32)
        d["qn"] = jnp.dot(d["q"], d["nbd"].astype(bf16), preferred_element_type=f32)
    for u in units:
        d = st[u]
        s = (d["qk"] * d["e"]).astype(bf16)
        num = d["a"] * d["qc"] + jnp.dot(s, d["vbd"], preferred_element_type=f32)
        den = d["a"] * d["qn"] + jnp.dot(s, ones_bd, preferred_element_type=f32)
        d["hh"] = num / jnp.maximum(jnp.abs(den), d["em"])
    for u in units:
        bi, pr = u
        d = st[u]
        sq = d["hh"] * d["hh"]
        sq_hi = sq.astype(bf16)
        sq_lo = (sq - sq_hi.astype(f32)).astype(bf16)
        ms = (jnp.dot(sq_hi, ones_bd, preferred_element_type=f32)
              + jnp.dot(sq_lo, ones_bd, preferred_element_type=f32)) * (1.0 / ML_DV)
        lo = pr * LANES
        og = z_ref[bi, :, 3 * ML_HEADS * ML_DK + lo:3 * ML_HEADS * ML_DK + lo + LANES]
        h_ref[bi, :, lo:lo + LANES] = d["hh"] * lax.rsqrt(ms + RMS_EPS) * nw_ref[:, lo:lo + LANES] * _sigmoid(og)
    for u in units:
        bi, pr = u
        d = st[u]
        cbd_ref[bi, pr] = d["dec"] * d["cbd"] + jnp.where(
            same_block, jnp.dot(d["k_t"], d["wkv"], preferred_element_type=f32), 0.0)
        nbd_ref[bi, pr] = d["dec"] * d["nbd"] + jnp.where(
            same_block, jnp.dot(d["k_t"], d["wk"], preferred_element_type=f32), 0.0)
    for bi in range(BB):
        new_m = m_ref[bi]
        for pr in range(ML_PAIRS):
            m_new = st[(bi, pr)]["m_new"]
            new_m = jnp.where(lane_1 == 2 * pr, m_new[:, 0:1], new_m)
            new_m = jnp.where(lane_1 == 2 * pr + 1, m_new[:, ML_DV:ML_DV + 1], new_m)
        m_ref[bi] = new_m


def _mlstm_pairs(z_ml, z_g, bias_row, norm_row, B, T, L):
    assert 2 * L == LANES and ML_DK == ML_DV == L
    nc = T // L
    bb = min(B, 4)
    tok = lambda b, c: (b, c, 0)
    st4 = lambda b, c: (b, 0, 0, 0)
    st3 = lambda b, c: (b, 0, 0)
    const = lambda b, c: (0, 0)
    sel = _pair_select_matrix()
    zeros_bd = jnp.zeros((B, ML_PAIRS, LANES, LANES), f32)
    h, cbd, nbd, m = pl.pallas_call(
        functools.partial(_mlstm_pair_kernel, L=L, BB=bb),
        grid=(B // bb, nc),
        in_specs=[pl.BlockSpec((bb, L, ML_W), tok),
                  pl.BlockSpec((bb, L, LANES), tok),
                  pl.BlockSpec((1, LANES), const),
                  pl.BlockSpec((1, ML_HEADS * ML_DV), const),
                  pl.BlockSpec(sel.shape, const),
                  pl.BlockSpec((bb, ML_PAIRS, LANES, LANES), st4),
                  pl.BlockSpec((bb, ML_PAIRS, LANES, LANES), st4),
                  pl.BlockSpec((bb, 1, LANES), st3)],
        out_specs=[pl.BlockSpec((bb, L, ML_HEADS * ML_DV), tok),
                   pl.BlockSpec((bb, ML_PAIRS, LANES, LANES), st4),
                   pl.BlockSpec((bb, ML_PAIRS, LANES, LANES), st4),
                   pl.BlockSpec((bb, 1, LANES), st3)],
        out_shape=[SDS((B, T, ML_HEADS * ML_DV), f32),
                   SDS((B, ML_PAIRS, LANES, LANES), f32),
                   SDS((B, ML_PAIRS, LANES, LANES), f32),
                   SDS((B, 1, LANES), f32)],
        compiler_params=_cparams("parallel", "arbitrary"),
        name="mlstm_pairs",
    )(z_ml.reshape(B, T, ML_W), z_g.reshape(B, T, LANES), bias_row, norm_row, sel, zeros_bd, zeros_bd,
      jnp.zeros((B, 1, LANES), f32))
    c_out = jnp.stack([cbd[:, :, :ML_DK, :ML_DV], cbd[:, :, ML_DK:, ML_DV:]], axis=2).reshape(B, ML_HEADS, ML_DK, ML_DV)
    n_out = jnp.stack([nbd[:, :, :ML_DK, 0], nbd[:, :, ML_DK:, ML_DV]], axis=2).reshape(B, ML_HEADS, ML_DK)
    return h, c_out, n_out, m[:, 0, :ML_HEADS]


def _rope(x, cos, sin_signed):
    w = x.shape[-1]
    lane = lax.broadcasted_iota(jnp.int32, x.shape, 1)
    swapped = jnp.where((lane % SW_HD) < SW_HD // 2, pltpu.roll(x, w - SW_HD // 2, 1), pltpu.roll(x, SW_HD // 2, 1))
    return x * cos + swapped * sin_signed


def _swa_attend(jobs, sinks_ref, L, precise=False):
    mm, mm_nt, _ = _dots(precise)
    units = [(j, g) for j in range(len(jobs)) for g in range(SW_KV_HEADS)]
    sinks = [jnp.concatenate(
        [jnp.broadcast_to(sinks_ref[:, g * SW_GROUP + i:g * SW_GROUP + i + 1], (L, 1)) for i in range(SW_GROUP)],
        axis=0) for g in range(SW_KV_HEADS)]
    s, p = {}, {}
    for j, g in units:
        qr, keys, _, _ = jobs[j]
        q4 = jnp.concatenate([qr[:, (g * SW_GROUP + i) * SW_HD:(g * SW_GROUP + i + 1) * SW_HD]
                              for i in range(SW_GROUP)], axis=0)
        s[j, g] = mm_nt(q4, keys[:, g * SW_HD:(g + 1) * SW_HD]) * (SW_HD ** -0.5)
    for j, g in units:
        first_valid = jobs[j][3]
        sc = s[j, g]
        if first_valid is not None:
            kcol = lax.broadcasted_iota(jnp.int32, (1, sc.shape[1]), 1)
            sc = jnp.where(kcol >= first_valid, sc, -jnp.inf)
        mx = jnp.maximum(jnp.max(sc, axis=-1, keepdims=True), sinks[g])
        e = jnp.exp(sc - mx)
        p[j, g] = e / (jnp.sum(e, axis=-1, keepdims=True) + jnp.exp(sinks[g] - mx))
    o = {u: mm(p[u], jobs[u[0]][2][:, u[1] * SW_HD:(u[1] + 1) * SW_HD]) for u in units}
    return [jnp.concatenate([o[j, g][i * L:(i + 1) * L, :] for g in range(SW_KV_HEADS) for i in range(SW_GROUP)],
                            axis=-1) for j in range(len(jobs))]


def _swa_prompt_kernel(q_ref, kp_ref, kc_ref, vp_ref, vc_ref, cp_ref, cc_ref, sp_ref, sc_ref, sinks_ref,
                       o_ref, kr_ref, *, L, CB):
    i = pl.program_id(1)
    rows = CB * L
    back = 2 * L
    cos_q = jnp.concatenate([cc_ref[...]] * (SW_HEADS // SW_KV_HEADS), axis=-1)
    sin_q = jnp.concatenate([sc_ref[...]] * (SW_HEADS // SW_KV_HEADS), axis=-1)
    qr = _rope(q_ref[...], cos_q, sin_q)
    k_cur = _rope(kc_ref[...], cc_ref[...], sc_ref[...])
    kr_ref[...] = k_cur
    k_prev = _rope(kp_ref[rows - back:rows, :], cp_ref[rows - back:rows, :], sp_ref[rows - back:rows, :])
    keys = jnp.concatenate([k_prev, k_cur], axis=0)
    vals = jnp.concatenate([vp_ref[rows - back:rows, :], vc_ref[...]], axis=0)
    jobs = []
    for u in range(CB):
        first_valid = jnp.where(i == 0, back - u * L, 0) if u * L < back else None
        jobs.append((qr[u * L:(u + 1) * L], keys[u * L:(u + 3) * L], vals[u * L:(u + 3) * L], first_valid))
    o_ref[...] = jnp.concatenate(_swa_attend(jobs, sinks_ref, L), axis=0)


def _swa_prompt(z_sw, cos_t, sin_t, sinks_row, B, T, L):
    cb = 4
    rows = cb * L
    nb = T // rows
    n = B * T
    kcol = SW_HEADS * SW_HD // LANES
    vcol = kcol + 1
    cur = lambda b, i: (b * nb + i, 0)
    prev = lambda col: (lambda b, i: (b * nb + jnp.maximum(i - 1, 0), col))
    curc = lambda col: (lambda b, i: (b * nb + i, col))
    tab_cur = lambda b, i: (i, 0)
    tab_prev = lambda b, i: (jnp.maximum(i - 1, 0), 0)
    return pl.pallas_call(
        functools.partial(_swa_prompt_kernel, L=L, CB=cb),
        grid=(B, nb),
        in_specs=[pl.BlockSpec((rows, SW_HEADS * SW_HD), cur),
                  pl.BlockSpec((rows, LANES), prev(kcol)), pl.BlockSpec((rows, LANES), curc(kcol)),
                  pl.BlockSpec((rows, LANES), prev(vcol)), pl.BlockSpec((rows, LANES), curc(vcol)),
                  pl.BlockSpec((rows, LANES), tab_prev), pl.BlockSpec((rows, LANES), tab_cur),
                  pl.BlockSpec((rows, LANES), tab_prev), pl.BlockSpec((rows, LANES), tab_cur),
                  pl.BlockSpec((1, LANES), lambda b, i: (0, 0))],
        out_specs=[pl.BlockSpec((rows, SW_HEADS * SW_HD), cur),
                   pl.BlockSpec((rows, LANES), cur)],
        out_shape=[SDS((n, SW_HEADS * SW_HD), f32), SDS((n, LANES), f32)],
        compiler_params=_cparams("parallel", "parallel"),
        name="swa_prompt",
    )(z_sw, z_sw, z_sw, z_sw, z_sw, cos_t, cos_t, sin_t, sin_t, sinks_row)


def _swa_sample_kernel(q_ref, k_ref, v_ref, ck_ref, cv_ref, cos_ref, sin_ref, sinks_ref, o_ref, kr_ref, *, L):
    cos_q = jnp.concatenate([cos_ref[...]] * (SW_HEADS // SW_KV_HEADS), axis=-1)
    sin_q = jnp.concatenate([sin_ref[...]] * (SW_HEADS // SW_KV_HEADS), axis=-1)
    qr = _rope(q_ref[...], cos_q, sin_q)
    kr = _rope(k_ref[...], cos_ref[...], sin_ref[...])
    kr_ref[...] = kr
    keys = jnp.concatenate([ck_ref[0], kr], axis=0)
    vals = jnp.concatenate([cv_ref[0], v_ref[...]], axis=0)
    o_ref[...] = _swa_attend([(qr, keys, vals, None)], sinks_ref, L, precise=True)[0]


def _swa_sample(z_sw, cache_k, cache_v, cos_t, sin_t, sinks_row, B, T):
    n = B * T
    kcol = SW_HEADS * SW_HD // LANES
    return pl.pallas_call(
        functools.partial(_swa_sample_kernel, L=T),
        grid=(B,),
        in_specs=[pl.BlockSpec((T, SW_HEADS * SW_HD), lambda b: (b, 0)),
                  pl.BlockSpec((T, LANES), lambda b: (b, kcol)),
                  pl.BlockSpec((T, LANES), lambda b: (b, kcol + 1)),
                  pl.BlockSpec((1, WINDOW, LANES), lambda b: (b, 0, 0)),
                  pl.BlockSpec((1, WINDOW, LANES), lambda b: (b, 0, 0)),
                  pl.BlockSpec((T, LANES), lambda b: (0, 0)),
                  pl.BlockSpec((T, LANES), lambda b: (0, 0)),
                  pl.BlockSpec((1, LANES), lambda b: (0, 0))],
        out_specs=[pl.BlockSpec((T, SW_HEADS * SW_HD), lambda b: (b, 0)),
                   pl.BlockSpec((T, LANES), lambda b: (b, 0))],
        out_shape=[SDS((n, SW_HEADS * SW_HD), f32), SDS((n, LANES), f32)],
        compiler_params=_cparams("parallel"),
        name="swa_sample",
    )(z_sw, z_sw, z_sw, cache_k, cache_v, cos_t, sin_t, sinks_row)


def _out_ln_kernel(*refs, n_in):
    x_ref = refs[0]
    a_refs = refs[1:1 + n_in]
    w_refs = refs[1 + n_in:1 + 2 * n_in]
    g_ref, b_ref, rw_ref, rb_ref, o_ref, gates_ref, route_ref = refs[1 + 2 * n_in:]
    mm = _dots(w_refs[0].dtype == f32)[0]
    y = mm(a_refs[0][...], w_refs[0][...])
    for a_ref, w_ref in zip(a_refs[1:], w_refs[1:]):
        y = y + mm(a_ref[...], w_ref[...])
    x_new = _layer_norm(DN_ALPHA * x_ref[...] + y, g_ref[...], b_ref[...])
    o_ref[...] = x_new
    gates_ref[...], route_ref[...] = _route(x_new, rw_ref, rb_ref)


def _out_ln(x2, acts, ws, g_row, b_row, rw_t, rb_col, tm):
    n = x2.shape[0]
    row = lambda i: (i, 0)
    const = lambda i: (0, 0)
    col = lambda i: (0, i)
    return pl.pallas_call(
        functools.partial(_out_ln_kernel, n_in=len(acts)),
        grid=(n // tm,),
        in_specs=[pl.BlockSpec((tm, D_MODEL), row)]
        + [pl.BlockSpec((tm, a.shape[1]), row) for a in acts]
        + [pl.BlockSpec(w.shape, const) for w in ws]
        + [pl.BlockSpec((1, D_MODEL), const), pl.BlockSpec((1, D_MODEL), const),
           pl.BlockSpec((N_EXPERTS, D_MODEL), const), pl.BlockSpec((N_EXPERTS, 1), const)],
        out_specs=[pl.BlockSpec((tm, D_MODEL), row), pl.BlockSpec((N_EXPERTS, tm), col), pl.BlockSpec((8, tm), col)],
        out_shape=[SDS((n, D_MODEL), f32), SDS((N_EXPERTS, n), f32), SDS((8, n), f32)],
        compiler_params=_cparams("parallel"),
        name="out_proj_ln",
    )(x2, *acts, *ws, g_row, b_row, rw_t, rb_col)


def _logistic(x):
    return 1.0 / (1.0 + jnp.exp(-x))


def _route(x, rw_ref, rb_ref):
    logits = _dot_nt(rw_ref[...], x)
    aff = _logistic(logits)
    sc = aff + rb_ref[...]
    s = [sc[e:e + 1, :] for e in range(N_EXPERTS)]
    a = [aff[e:e + 1, :] for e in range(N_EXPERTS)]
    scores = []
    for gi in range(N_GROUPS):
        w, x, y, z = s[4 * gi:4 * gi + 4]
        p, q = jnp.maximum(w, x), jnp.minimum(w, x)
        r, t = jnp.maximum(y, z), jnp.minimum(y, z)
        scores.append(jnp.maximum(p, r) + jnp.maximum(jnp.minimum(p, r), jnp.maximum(q, t)))
    best = scores[0]
    gsel = jnp.zeros_like(best, dtype=jnp.int32)
    for gi in range(1, N_GROUPS):
        better = scores[gi] > best
        best = jnp.where(better, scores[gi], best)
        gsel = jnp.where(better, gi, gsel)
    sel = []
    for e in range(N_EXPERTS):
        gi, i = divmod(e, EXP_PER_GROUP)
        beaten = jnp.zeros_like(gsel)
        for j in range(EXP_PER_GROUP):
            if j == i:
                continue
            o = s[4 * gi + j]
            wins = (o >= s[e]) if j < i else (o > s[e])
            beaten = beaten + wins.astype(jnp.int32)
        sel.append((gsel == gi) & (beaten < 2))
    den = jnp.zeros_like(best)
    for e in range(N_EXPERTS):
        den = den + jnp.where(sel[e], a[e], 0.0)
    gate = [jnp.where(sel[e], a[e] / den, 0.0) for e in range(N_EXPERTS)]
    taken = jnp.zeros_like(gsel)
    ea = eb = wa = wb = jnp.zeros_like(best)
    for e in range(N_EXPERTS):
        first = sel[e] & (taken == 0)
        second = sel[e] & (taken == 1)
        ea = jnp.where(first, float(e), ea)
        wa = jnp.where(first, gate[e], wa)
        eb = jnp.where(second, float(e), eb)
        wb = jnp.where(second, gate[e], wb)
        taken = taken + sel[e].astype(jnp.int32)
    route = jnp.concatenate([ea, eb, wa, wb, jnp.zeros((4, ea.shape[1]), f32)], axis=0)
    return jnp.concatenate(gate, axis=0), route


def _moe_kernel(x_ref, gates_ref, wg_ref, wu_ref, wd_ref, g_ref, b_ref, o_ref, xb_ref, acc_ref):
    e = pl.program_id(1)

    @pl.when(e == 0)
    def _():
        xb_ref[...] = x_ref[...].astype(bf16)
        acc_ref[...] = jnp.zeros_like(acc_ref)

    xb = xb_ref[...]
    lane = lax.broadcasted_iota(jnp.int32, gates_ref.shape, 1)
    gcol = jnp.sum(jnp.where(lane == e, gates_ref[...], 0.0), axis=-1, keepdims=True)
    h = _silu(_bdot(xb, wg_ref[0])) * _bdot(xb, wu_ref[0])
    acc_ref[...] += _bdot(gcol * h, wd_ref[0])

    @pl.when(e == N_EXPERTS - 1)
    def _():
        o_ref[...] = _layer_norm(DN_ALPHA * x_ref[...] + acc_ref[...], g_ref[...], b_ref[...])


def _moe_ln(x2, gates, wg, wu, wd, base, g_row, b_row, tm):
    n = x2.shape[0]
    row = lambda i, e: (i, 0)
    const = lambda i, e: (0, 0)
    expert = lambda i, e: (base + e, 0, 0)
    return pl.pallas_call(
        _moe_kernel,
        grid=(n // tm, N_EXPERTS),
        in_specs=[pl.BlockSpec((tm, D_MODEL), row),
                  pl.BlockSpec((tm, N_EXPERTS), row),
                  pl.BlockSpec((1, D_MODEL, D_EXPERT), expert),
                  pl.BlockSpec((1, D_MODEL, D_EXPERT), expert),
                  pl.BlockSpec((1, D_EXPERT, D_MODEL), expert),
                  pl.BlockSpec((1, D_MODEL), const), pl.BlockSpec((1, D_MODEL), const)],
        out_specs=pl.BlockSpec((tm, D_MODEL), row),
        out_shape=SDS((n, D_MODEL), f32),
        scratch_shapes=[pltpu.VMEM((tm, D_MODEL), bf16), pltpu.VMEM((tm, D_MODEL), f32)],
        compiler_params=_cparams("parallel", "arbitrary"),
        name="moe_ln",
    )(x2, gates, wg, wu, wd, g_row, b_row)


N_PAIRS = N_GROUPS * (EXP_PER_GROUP * (EXP_PER_GROUP - 1) // 2)
MOE_TM = 256
MOE_DMA_ROWS = 512
_PAIR_A = [g * EXP_PER_GROUP + a for g in range(N_GROUPS) for a in range(EXP_PER_GROUP) for b in range(a + 1, EXP_PER_GROUP)]
_PAIR_B = [g * EXP_PER_GROUP + b for g in range(N_GROUPS) for a in range(EXP_PER_GROUP) for b in range(a + 1, EXP_PER_GROUP)]


def _gather_rows_kernel(idx_ref, src_ref, o_ref, sem, *, rows):
    base = pl.program_id(0) * rows

    def row_copy(j):
        return pltpu.make_async_copy(src_ref.at[pl.ds(idx_ref[base + j], 1)], o_ref.at[pl.ds(j, 1)], sem)

    def issue(j, carry):
        row_copy(j).start()
        return carry

    def drain(j, carry):
        row_copy(j).wait()
        return carry

    lax.fori_loop(0, rows, issue, 0, unroll=8)
    lax.fori_loop(0, rows, drain, 0, unroll=8)


def _gather_rows(src, idx, rows):
    n_out = idx.shape[0]
    d = src.shape[1]
    return pl.pallas_call(
        functools.partial(_gather_rows_kernel, rows=rows),
        grid_spec=pltpu.PrefetchScalarGridSpec(
            num_scalar_prefetch=1,
            grid=(n_out // rows,),
            in_specs=[pl.BlockSpec(memory_space=pl.ANY)],
            out_specs=pl.BlockSpec((rows, d), lambda i, idx_ref: (i, 0)),
            scratch_shapes=[pltpu.SemaphoreType.DMA(())]),
        out_shape=SDS((n_out, d), f32),
        compiler_params=_cparams("arbitrary"),
        name="gather_rows",
    )(idx, src)


def _scatter_rows_kernel(idx_ref, src_ref, init_ref, o_ref, sem, *, rows):
    del init_ref
    base = pl.program_id(0) * rows

    def row_copy(j):
        return pltpu.make_async_copy(src_ref.at[pl.ds(j, 1)], o_ref.at[pl.ds(idx_ref[base + j], 1)], sem)

    def issue(j, carry):
        row_copy(j).start()
        return carry

    def drain(j, carry):
        row_copy(j).wait()
        return carry

    lax.fori_loop(0, rows, issue, 0, unroll=8)
    lax.fori_loop(0, rows, drain, 0, unroll=8)


def _scatter_rows(src, idx, n_out, rows):
    n_src, d = src.shape
    return pl.pallas_call(
        functools.partial(_scatter_rows_kernel, rows=rows),
        grid_spec=pltpu.PrefetchScalarGridSpec(
            num_scalar_prefetch=1,
            grid=(n_src // rows,),
            in_specs=[pl.BlockSpec((rows, d), lambda i, idx_ref: (i, 0)), pl.BlockSpec(memory_space=pl.ANY)],
            out_specs=pl.BlockSpec(memory_space=pl.ANY),
            scratch_shapes=[pltpu.SemaphoreType.DMA(())]),
        out_shape=SDS((n_out, d), f32),
        input_output_aliases={2: 0},
        compiler_params=_cparams("arbitrary"),
        name="scatter_rows",
    )(idx, src, jnp.zeros((n_out, d), f32))


def _pair_plan(route, n, tm):
    ea = route[0].astype(jnp.int32)
    eb = route[1].astype(jnp.int32)
    a = ea % EXP_PER_GROUP
    b = eb % EXP_PER_GROUP
    pidx = jnp.where(a == 0, b - 1, jnp.where(a == 1, b + 1, 5))
    pair = (ea // EXP_PER_GROUP) * (N_PAIRS // N_GROUPS) + pidx
    onehot = (pair[:, None] == jnp.arange(N_PAIRS, dtype=jnp.int32)[None, :]).astype(jnp.int32)
    csum = jnp.cumsum(onehot, axis=0)
    counts = csum[-1]
    ntiles = (counts + tm - 1) // tm
    tile_end = jnp.cumsum(ntiles)
    tile_start = tile_end - ntiles
    row_of_token = jnp.sum(onehot * (csum - 1 + (tile_start * tm)[None, :]), axis=1)
    nt = n // tm + N_PAIRS
    tile_id = jnp.arange(nt, dtype=jnp.int32)
    tile_valid = tile_id < tile_end[-1]
    tile_pair = jnp.sum((tile_end[None, :] <= jnp.minimum(tile_id, tile_end[-1] - 1)[:, None]).astype(jnp.int32), axis=1)
    tile_pair = jnp.minimum(tile_pair, N_PAIRS - 1)
    pick = (tile_pair[:, None] == jnp.arange(N_PAIRS, dtype=jnp.int32)[None, :]).astype(jnp.int32)
    tile_a = jnp.sum(pick * jnp.asarray(_PAIR_A, jnp.int32)[None, :], axis=1)
    tile_b = jnp.sum(pick * jnp.asarray(_PAIR_B, jnp.int32)[None, :], axis=1)
    return row_of_token, tile_a, tile_b, tile_valid.astype(jnp.int32)


def _pair_expert_kernel(ta_ref, tb_ref, tv_ref, x_ref, rwt_ref, wga_ref, wua_ref, wda_ref, wgb_ref, wub_ref, wdb_ref,
                        g_ref, b_ref, o_ref, *w_bf16):
    i = pl.program_id(0)
    valid = tv_ref[i] == 1
    prev = jnp.maximum(i - 1, 0)
    new_pair = (i == 0) | (ta_ref[i] != ta_ref[prev]) | (tb_ref[i] != tb_ref[prev])

    @pl.when(new_pair)
    def _():
        for dst, src in zip(w_bf16, (wga_ref, wua_ref, wda_ref, wgb_ref, wub_ref, wdb_ref)):
            dst[...] = src[0].astype(bf16)

    @pl.when(valid)
    def _():
        wga, wua, wda, wgb, wub, wdb = w_bf16
        x = x_ref[...]
        xb = x.astype(bf16)
        aff_a = _logistic(jnp.sum(x * rwt_ref[pl.ds(ta_ref[i], 1), :], axis=-1, keepdims=True))
        aff_b = _logistic(jnp.sum(x * rwt_ref[pl.ds(tb_ref[i], 1), :], axis=-1, keepdims=True))
        den = aff_a + aff_b
        mm = lambda a, w_ref: jnp.dot(a, w_ref[...], preferred_element_type=f32)
        ga, gb, ua, ub = mm(xb, wga), mm(xb, wgb), mm(xb, wua), mm(xb, wub)
        ha = ((aff_a / den) * (_silu(ga) * ua)).astype(bf16)
        hb = ((aff_b / den) * (_silu(gb) * ub)).astype(bf16)
        acc = mm(ha, wda) + mm(hb, wdb)
        o_ref[...] = _layer_norm(DN_ALPHA * x + acc, g_ref[...], b_ref[...])

    @pl.when(jnp.logical_not(valid))
    def _():
        o_ref[...] = jnp.zeros_like(o_ref)


def _pair_experts(xs, rw_t, tile_a, tile_b, tile_valid, wg, wu, wd, base, g_row, b_row, tm):
    rows = xs.shape[0]
    row = lambda i, ta, tb, tv: (i, 0)
    const = lambda i, ta, tb, tv: (0, 0)
    ex_a = lambda i, ta, tb, tv: (base + ta[i], 0, 0)
    ex_b = lambda i, ta, tb, tv: (base + tb[i], 0, 0)
    up = pl.BlockSpec((1, D_MODEL, D_EXPERT), ex_a), pl.BlockSpec((1, D_MODEL, D_EXPERT), ex_b)
    down = pl.BlockSpec((1, D_EXPERT, D_MODEL), ex_a), pl.BlockSpec((1, D_EXPERT, D_MODEL), ex_b)
    return pl.pallas_call(
        _pair_expert_kernel,
        grid_spec=pltpu.PrefetchScalarGridSpec(
            num_scalar_prefetch=3,
            grid=(rows // tm,),
            in_specs=[pl.BlockSpec((tm, D_MODEL), row), pl.BlockSpec((N_EXPERTS, D_MODEL), const),
                      up[0], up[0], down[0], up[1], up[1], down[1],
                      pl.BlockSpec((1, D_MODEL), const), pl.BlockSpec((1, D_MODEL), const)],
            out_specs=pl.BlockSpec((tm, D_MODEL), row),
            scratch_shapes=[pltpu.VMEM((D_MODEL, D_EXPERT), bf16), pltpu.VMEM((D_MODEL, D_EXPERT), bf16),
                            pltpu.VMEM((D_EXPERT, D_MODEL), bf16)] * 2),
        out_shape=SDS((rows, D_MODEL), f32),
        compiler_params=_cparams("arbitrary"),
        name="pair_experts",
    )(tile_a, tile_b, tile_valid, xs, rw_t, wg, wu, wd, wg, wu, wd, g_row, b_row)


def _unit_lower_inverses(mats, L):
    row = lax.broadcasted_iota(jnp.int32, (L, L), 0)
    col = lax.broadcasted_iota(jnp.int32, (L, L), 1)
    eye = (row == col).astype(f32)
    ps = [eye - a for a in mats]
    pws = [a.astype(bf16) for a in mats]
    span = 2
    while span < L:
        pws = [jnp.dot(pw, pw, preferred_element_type=f32).astype(bf16) for pw in pws]
        ps = [p + jnp.dot(pw, p.astype(bf16), preferred_element_type=f32) for p, pw in zip(ps, pws)]
        span *= 2
    return ps


def _gdn_kernel(x_ref, zg_ref, ba_ref, cw_ref, alog_ref, dt_ref, nw_ref, s0_ref, cb_ref,
                o_ref, s_ref, prev_ref, *, L, BB):
    @pl.when(pl.program_id(1) == 0)
    def _():
        s_ref[...] = s0_ref[...]
        prev_ref[...] = cb_ref[...]

    def conv_silu(bi, lo, width):
        cur = x_ref[bi, :, lo:lo + width]
        cat =jnp.concatenate([prev_ref[bi, :, lo:lo + width], cur], axis=0)
        acc = cat[5:5 + L] * cw_ref[0:1, lo:lo + width]
        acc = acc + cat[6:6 + L] * cw_ref[1:2, lo:lo + width]
        acc = acc + cat[7:7 + L] * cw_ref[2:3, lo:lo + width]
        acc = acc + cur * cw_ref[3:4, lo:lo + width]
        return _silu(acc)

    def l2n(v, scale):
        return v * lax.rsqrt(jnp.sum(v * v, axis=-1, keepdims=True) + 1e-6) * scale

    row = lax.broadcasted_iota(jnp.int32, (L, L), 0)
    col = lax.broadcasted_iota(jnp.int32, (L, L), 1)
    incl = row >= col
    strict = row > col
    rep = GD_V_HEADS // GD_QK_HEADS
    bas = [ba_ref[bi] for bi in range(BB)]
    betas = [_sigmoid(ba) for ba in bas]
    gls = [-jnp.exp(alog_ref[...]) * _softplus(ba + dt_ref[...]) for ba in bas]
    gcum_all = _dot(incl.astype(f32), jnp.concatenate(gls, axis=1))
    gcums = [gcum_all[:, bi * LANES:(bi + 1) * LANES] for bi in range(BB)]
    gcum_ts = [g.T for g in gcums]
    units = [(bi, hv) for bi in range(BB) for hv in range(GD_V_HEADS)]
    qs, ks, amats, qkds, egs, g_cols, rhss = {}, {}, [], {}, {}, {}, []
    for bi in range(BB):
        for j in range(GD_QK_HEADS):
            q = l2n(conv_silu(bi, j * GD_HD, GD_HD), GD_HD ** -0.5)
            k = l2n(conv_silu(bi, GD_QK_W + j * GD_HD, GD_HD), 1.0)
            qb, kb = q.astype(bf16), k.astype(bf16)
            kk = lax.dot_general(kb, kb, (((1,), (1,)), ((), ())), preferred_element_type=f32)
            qk = lax.dot_general(qb, kb, (((1,), (1,)), ((), ())), preferred_element_type=f32)
            qs[bi, j] = qb
            ks[bi, j] = k
            for r in range(rep):
                hv = j * rep + r
                v = conv_silu(bi, 2 * GD_QK_W + hv * GD_HD, GD_HD)
                g_col = gcums[bi][:, GD_V_HEADS + hv:GD_V_HEADS + hv + 1]
                g_row = gcum_ts[bi][GD_V_HEADS + hv:GD_V_HEADS + hv + 1, :]
                b_col = betas[bi][:, hv:hv + 1]
                decay = jnp.exp(jnp.where(incl, g_col - g_row, -jnp.inf))
                eg = jnp.exp(g_col)
                amats.append(jnp.where(strict, b_col * kk * decay, 0.0))
                qkds[bi, hv] = (qk * decay).astype(bf16)
                egs[bi, hv] = eg
                g_cols[bi, hv] = g_col
                rhss.append(jnp.concatenate([b_col * v, (b_col * eg) * k], axis=-1))
    tinvs = _unit_lower_inverses(amats, L)
    sols = dict(zip(units, [_bdot(t, r) for t, r in zip(tinvs, rhss)]))
    sts, wks, qss, kts = {}, {}, {}, {}
    for u in units:
        bi, hv = u
        j = hv // rep
        sts[u] = s_ref[bi, hv]
        stb = sts[u].astype(bf16)
        wks[u] = jnp.dot(sols[u][:, GD_HD:].astype(bf16), stb, preferred_element_type=f32)
        qss[u] = jnp.dot(qs[bi, j], stb, preferred_element_type=f32)
        g_col = g_cols[u]
        kts[u] = (jnp.exp(g_col[L - 1:L, :] - g_col) * ks[bi, j]).T.astype(bf16)
    outs = {}
    for u in units:
        bi, hv = u
        wnb = (sols[u][:, :GD_HD] - wks[u]).astype(bf16)
        outs[u] = egs[u] * qss[u] + jnp.dot(qkds[u], wnb, preferred_element_type=f32)
        s_ref[bi, hv] = jnp.exp(g_cols[u][L - 1:L, :]) * sts[u] + jnp.dot(kts[u], wnb, preferred_element_type=f32)
    for u in units:
        bi, hv = u
        o = outs[u]
        o = o * lax.rsqrt(jnp.mean(o * o, axis=-1, keepdims=True) + RMS_EPS) * nw_ref[...]
        o_ref[bi, :, hv * GD_HD:(hv + 1) * GD_HD] = o * _silu(zg_ref[bi, :, hv * GD_HD:(hv + 1) * GD_HD])
    for bi in range(BB):
        prev_ref[bi] = x_ref[bi, L - 8:L, :]


def _gdn(qkv, zg, ba, conv_w, alog_row, dt_row, norm_row, s0, conv8, B, T, L):
    nc = T // L
    n = B * T
    bb = 2
    tok = lambda b, c: (b, c, 0)
    const = lambda b, c: (0, 0)
    st4 = lambda b, c: (b, 0, 0, 0)
    o, s_out = pl.pallas_call(
        functools.partial(_gdn_kernel, L=L, BB=bb),
        grid=(B // bb, nc),
        in_specs=[pl.BlockSpec((bb, L, GD_CONV_CH), tok),
                  pl.BlockSpec((bb, L, GD_V_W), tok),
                  pl.BlockSpec((bb, L, LANES), tok),
                  pl.BlockSpec((GD_CONV, GD_CONV_CH), const),
                  pl.BlockSpec((1, LANES), const),
                  pl.BlockSpec((1, LANES), const),
                  pl.BlockSpec((1, GD_HD), const),
                  pl.BlockSpec((bb, GD_V_HEADS, GD_HD, GD_HD), st4),
                  pl.BlockSpec((bb, 8, GD_CONV_CH), lambda b, c: (b, 0, 0))],
        out_specs=[pl.BlockSpec((bb, L, GD_V_W), tok),
                   pl.BlockSpec((bb, GD_V_HEADS, GD_HD, GD_HD), st4)],
        out_shape=[SDS((B, T, GD_V_W), f32), SDS((B, GD_V_HEADS, GD_HD, GD_HD), f32)],
        scratch_shapes=[pltpu.VMEM((bb, 8, GD_CONV_CH), f32)],
        compiler_params=_cparams("parallel", "arbitrary"),
        name="gdn",
    )(qkv.reshape(B, T, GD_CONV_CH), zg.reshape(B, T, GD_V_W), ba.reshape(B, T, LANES), conv_w, alog_row, dt_row,
      norm_row, s0, conv8)
    return o.reshape(n, GD_V_W), s_out


def _pad_lanes(row, offset=0):
    return jnp.zeros((1, LANES), f32).at[0, offset:offset + row.shape[0]].set(row.astype(f32))


def _rope_tables(pos):
    half = SW_HD // 2
    inv = ROPE_THETA ** (-jnp.arange(half, dtype=f32) / half)
    ang = pos.astype(f32)[:, None] * inv[None, :]
    cos, sin = jnp.cos(ang), jnp.sin(ang)
    cos_t = jnp.concatenate([cos, cos] * SW_KV_HEADS, axis=-1)
    sin_t = jnp.concatenate([-sin, sin] * SW_KV_HEADS, axis=-1)
    return cos_t, sin_t


def _tile(n, pref):
    return pref if n % pref == 0 else n


def _trunk(x, pos, L, state, p):
    B, T, _ = x.shape
    n = B * T
    x2 = x.reshape(n, D_MODEL)
    tm = _tile(n, 512)
    precise = state is not None
    ab_w, wo_h, wo_a = (p["ab_w32"], p["ab_wo_h32"], p["ab_wo_a32"]) if precise else (p["ab_w"], p["ab_wo_h"], p["ab_wo_a"])

    z_ml, z_sw, z_g = _proj(x2, ab_w, ((0, ML_W), (ML_W, SW_W), (ML_W + SW_W, LANES)), tm)
    if state is None:
        h_ml, ml_c, ml_n, ml_m = _mlstm_pairs(z_ml, z_g, p["ab_bias"], p["ab_norm"], B, T, L)
    else:
        cn0 = jnp.concatenate([state["ml_C"], state["ml_n"][..., None],
                               jnp.zeros((B, ML_HEADS, ML_DK, LANES - ML_DV - 1), f32)], axis=-1)
        m0 = jnp.zeros((B, 1, LANES), f32).at[:, 0, :ML_HEADS].set(state["ml_m"])
        h_ml, cn, m_out = _mlstm(z_ml, z_g, p["ab_bias"], p["ab_norm"], cn0, m0, B, T, L, precise)
        ml_c, ml_n, ml_m = cn[..., :ML_DV], cn[..., ML_DV], m_out[:, 0, :ML_HEADS]
    h_ml = h_ml.reshape(n, ML_HEADS * ML_DV)
    cos_t, sin_t = _rope_tables(pos)
    if state is None:
        a_sw, k_rot = _swa_prompt(z_sw, cos_t, sin_t, p["ab_sinks"], B, T, L)
    else:
        a_sw, k_rot = _swa_sample(z_sw, state["sw_k"].reshape(B, WINDOW, LANES),
                                  state["sw_v"].reshape(B, WINDOW, LANES), cos_t, sin_t, p["ab_sinks"], B, T)
    keep = min(T, WINDOW)
    new_k = k_rot.reshape(B, T, LANES)[:, T - keep:].reshape(B, keep, SW_KV_HEADS, SW_HD)
    new_v = z_sw.reshape(B, T, SW_W)[:, T - keep:, SW_HEADS * SW_HD + LANES:].reshape(B, keep, SW_KV_HEADS, SW_HD)
    x2, gates_t, route = _out_ln(x2, [h_ml, a_sw], [wo_h, wo_a], p["ln_g"][0][0], p["ln_b"][0][0],
                                 p["router_wt"], p["router_b"], tm)
    x2 = _moe_block(x2, gates_t, route, p, 0)

    tm1 = _tile(n, 256)
    if state is None:
        s0 = jnp.zeros((B, GD_V_HEADS, GD_HD, GD_HD), f32)
        conv8 = jnp.zeros((B, 8, GD_CONV_CH), f32)
    else:
        s0 = state["gd_S"]
        conv8 = jnp.concatenate([jnp.zeros((B, 8 - (GD_CONV - 1), GD_CONV_CH), f32), state["gd_conv"]], axis=1)
    qkv, zg, ba = _proj(x2, p["c_w"], ((0, GD_CONV_CH), (GD_CONV_CH, GD_V_W), (GD_CONV_CH + GD_V_W, LANES)), tm1)
    new_conv = qkv.reshape(B, T, GD_CONV_CH)[:, T - (GD_CONV - 1):]
    o_gd, s_out = _gdn(qkv, zg, ba, p["c_conv_w"], p["c_alog"], p["c_dt"], p["c_norm"], s0, conv8, B, T, L)
    x2, gates_t, route = _out_ln(x2, [o_gd], [p["c_wo"]], p["ln_g"][1][0], p["ln_b"][1][0],
                                 p["router_wt"], p["router_b"], tm)
    x2 = _moe_block(x2, gates_t, route, p, 1)

    outs = (new_k[None], new_v[None], ml_c[None], ml_n[None], ml_m[None],
            s_out[None], new_conv[None])
    return x2.reshape(B, T, D_MODEL), outs


def _moe_block(x2, gates_t, route, p, layer):
    n = x2.shape[0]
    wg, wu, wd = p["ex_gate"], p["ex_up"], p["ex_down"]
    base = layer * N_EXPERTS
    g_row, b_row = p["ln_g"][layer][1], p["ln_b"][layer][1]
    if n < N_PAIRS * MOE_TM:
        return _moe_ln(x2, gates_t.T, wg, wu, wd, base, g_row, b_row, _tile(n, 1024))
    row_of_token, tile_a, tile_b, tile_valid = _pair_plan(route, n, MOE_TM)
    xs = _scatter_rows(x2, row_of_token, n + N_PAIRS * MOE_TM, MOE_DMA_ROWS)
    ys = _pair_experts(xs, p["router_wt"], tile_a, tile_b, tile_valid, wg, wu, wd, base, g_row, b_row, MOE_TM)
    return _gather_rows(ys, row_of_token, MOE_DMA_ROWS)


def kernel(x_prompt, x_sample, cache_swa_k, cache_swa_v, state_mlstm_C, state_mlstm_n, state_mlstm_m, state_gdn_S, state_gdn_conv, ab_w_in, ab_b_i, ab_b_f, ab_norm, ab_sinks, ab_w_out, c_w_in, c_conv_w, c_a_log, c_dt_bias, c_norm, c_w_out, ln_g, ln_b, router_w, router_b, ex_gate, ex_up, ex_down):
    gate_lo = ML_W
    sw_lo = ML_W + 2 * ML_HEADS
    w0 = ab_w_in[0]
    ab_w32 = jnp.concatenate([w0[:, :gate_lo], w0[:, sw_lo:], w0[:, gate_lo:sw_lo],
                              jnp.zeros((D_MODEL, LANES - 2 * ML_HEADS), f32)], axis=1)
    w1 = c_w_in[0]
    c_w = jnp.concatenate([w1, jnp.zeros((D_MODEL, LANES - 2 * GD_V_HEADS), f32)], axis=1).astype(bf16)
    wo32 = ab_w_out[0]
    wo = wo32.astype(bf16)
    p = {
        "ab_w": ab_w32.astype(bf16),
        "ab_w32": ab_w32,
        "ab_wo_h32": wo32[:ML_HEADS * ML_DV],
        "ab_wo_a32": wo32[ML_HEADS * ML_DV:],
        "ab_bias": _pad_lanes(jnp.concatenate([ab_b_i[0], ab_b_f[0]])),
        "ab_norm": ab_norm[0].reshape(1, ML_HEADS * ML_DV),
        "ab_sinks": _pad_lanes(ab_sinks[0]),
        "ab_wo_h": wo[:ML_HEADS * ML_DV],
        "ab_wo_a": wo[ML_HEADS * ML_DV:],
        "c_w": c_w,
        "c_conv_w": c_conv_w[0],
        "c_alog": _pad_lanes(c_a_log[0], GD_V_HEADS),
        "c_dt": _pad_lanes(c_dt_bias[0], GD_V_HEADS),
        "c_norm": c_norm[0].reshape(1, GD_HD),
        "c_wo": c_w_out[0].astype(bf16),
        "ln_g": [[ln_g[i, j].reshape(1, D_MODEL) for j in range(2)] for i in range(DEPTH)],
        "ln_b": [[ln_b[i, j].reshape(1, D_MODEL) for j in range(2)] for i in range(DEPTH)],
        "router_wt": router_w.T,
        "router_b": router_b.reshape(N_EXPERTS, 1),
        "ex_gate": ex_gate.reshape(DEPTH * N_EXPERTS, D_MODEL, D_EXPERT),
        "ex_up": ex_up.reshape(DEPTH * N_EXPERTS, D_MODEL, D_EXPERT),
        "ex_down": ex_down.reshape(DEPTH * N_EXPERTS, D_EXPERT, D_MODEL),
    }
    t_p = x_prompt.shape[1]
    y_p, st_p = _trunk(x_prompt, jnp.arange(t_p, dtype=jnp.int32), CHUNK, None, p)
    t_s = x_sample.shape[1]
    state = {"sw_k": cache_swa_k[0], "sw_v": cache_swa_v[0], "ml_C": state_mlstm_C[0], "ml_n": state_mlstm_n[0],
             "ml_m": state_mlstm_m[0], "gd_S": state_gdn_S[0], "gd_conv": state_gdn_conv[0]}
    y_s, st_s = _trunk(x_sample, PAST_LEN + jnp.arange(t_s, dtype=jnp.int32), t_s, state, p)
    return (y_p, y_s) + st_p + st_s
```

```python
import functools
import math

import jax
import jax.numpy as jnp
import numpy as np
from jax import lax
from jax.experimental import pallas as pl
from jax.experimental.pallas import tpu as pltpu

f32 = jnp.float32
bf16 = jnp.bfloat16
HIGHEST = lax.Precision.HIGHEST

D_MODEL = 1024
DEPTH = 2
CHUNK = 64
PAST_LEN = 2048
ML_HEADS = 8
ML_DK = 64
ML_DV = 64
SW_HEADS = 8
SW_KV_HEADS = 2
SW_HD = 64
SW_GROUP = SW_HEADS // SW_KV_HEADS
WINDOW = 128
ROPE_THETA = 10000.0
GD_QK_HEADS = 8
GD_V_HEADS = 16
GD_HD = 128
GD_CONV = 4
GD_QK_W = GD_QK_HEADS * GD_HD
GD_V_W = GD_V_HEADS * GD_HD
GD_CONV_CH = 2 * GD_QK_W + GD_V_W
N_EXPERTS = 16
N_GROUPS = 4
EXP_PER_GROUP = 4
D_EXPERT = 512
DN_ALPHA = (2 * DEPTH) ** 0.25
LN_EPS = 1e-5
RMS_EPS = 1e-6

LANES = 128
ML_W = 4 * ML_HEADS * ML_DK
SW_W = SW_HEADS * SW_HD + 2 * SW_KV_HEADS * SW_HD
VMEM_LIMIT = 56 * 1024 * 1024

SDS = jax.ShapeDtypeStruct


def _cparams(*sem):
    return pltpu.CompilerParams(dimension_semantics=sem, vmem_limit_bytes=VMEM_LIMIT)


def _dot(a, b):
    return jnp.dot(a, b, preferred_element_type=f32, precision=HIGHEST)


def _dot_nt(a, b):
    return lax.dot_general(a, b, (((1,), (1,)), ((), ())), preferred_element_type=f32, precision=HIGHEST)


def _bdot(a, b):
    return jnp.dot(a.astype(bf16), b.astype(bf16), preferred_element_type=f32)


def _bdot_nt(a, b):
    return lax.dot_general(a.astype(bf16), b.astype(bf16), (((1,), (1,)), ((), ())), preferred_element_type=f32)


def _bdot_tn(a, b):
    return jnp.dot(a.T.astype(bf16), b.astype(bf16), preferred_element_type=f32)


def _dots(precise):
    if precise:
        return _dot, _dot_nt, lambda a, b: _dot(a.T, b)
    return _bdot, _bdot_nt, _bdot_tn


def _sigmoid(x):
    return 0.5 + 0.5 * jnp.tanh(0.5 * x)


def _silu(x):
    hx = 0.5 * x
    return hx + hx * jnp.tanh(hx)


def _softplus(x):
    return jnp.maximum(x, 0.0) + jnp.log(1.0 + jnp.exp(-jnp.abs(x)))


def _layer_norm(v, g, b):
    mu = jnp.mean(v, axis=-1, keepdims=True)
    d = v - mu
    var = jnp.mean(d * d, axis=-1, keepdims=True)
    return d * lax.rsqrt(var + LN_EPS) * g + b


def _proj_kernel(x_ref, w_ref, *o_refs, splits, col_chunk, precise):
    mm = _dots(precise)[0]
    xb = x_ref[...] if precise else x_ref[...].astype(bf16)
    for o_ref, (start, width) in zip(o_refs, splits):
        for c in range(0, width, col_chunk):
            cw = min(col_chunk, width - c)
            o_ref[:, c:c + cw] = mm(xb, w_ref[:, start + c:start + c + cw])


def _proj(x2, w, splits, tm):
    n, k = x2.shape
    return pl.pallas_call(
        functools.partial(_proj_kernel, splits=splits, col_chunk=512, precise=w.dtype == f32),
        grid=(n // tm,),
        in_specs=[pl.BlockSpec((tm, k), lambda i: (i, 0)),
                  pl.BlockSpec(w.shape, lambda i: (0, 0), pipeline_mode=pl.Buffered(1))],
        out_specs=[pl.BlockSpec((tm, wd), lambda i: (i, 0)) for _, wd in splits],
        out_shape=[SDS((n, wd), f32) for _, wd in splits],
        compiler_params=_cparams("parallel"),
        name="in_proj",
    )(x2, w)


def _mlstm_kernel(z_ref, g_ref, bias_ref, nw_ref, cn0_ref, m0_ref, h_ref, cn_ref, m_ref, *, L, BB, precise):
    @pl.when(pl.program_id(1) == 0)
    def _():
        cn_ref[...] = cn0_ref[...]
        m_ref[...] = m0_ref[...]

    mm, mm_nt, mm_tn = _dots(precise)
    row = lax.broadcasted_iota(jnp.int32, (L, L), 0)
    col = lax.broadcasted_iota(jnp.int32, (L, L), 1)
    causal = row >= col
    tri = causal.astype(f32)
    lane = lax.broadcasted_iota(jnp.int32, (1, LANES), 1)
    lane_l = lax.broadcasted_iota(jnp.int32, (L, ML_DV), 1)
    one_hot0 = (lane_l == 0).astype(f32)
    for bi in range(BB):
        g = g_ref[bi] + bias_ref[...]
        lf = jnp.minimum(g, 0.0) - jnp.log(1.0 + jnp.exp(-jnp.abs(g)))
        bcum = _dot(tri, lf)
        b_t = bcum.T
        g_t = g.T
        m_row = m_ref[bi]
        new_m = m_row
        outs = []
        for h in range(ML_HEADS):
            b_col = bcum[:, ML_HEADS + h:ML_HEADS + h + 1]
            b_row = b_t[ML_HEADS + h:ML_HEADS + h + 1, :]
            ig_row = g_t[h:h + 1, :]
            ig_col = g[:, h:h + 1]
            m_h = m_row[:, h:h + 1]
            dmat = jnp.where(causal, b_col - b_row + ig_row, -jnp.inf)
            inter = b_col + m_h
            mt = jnp.maximum(inter, jnp.max(dmat, axis=-1, keepdims=True))
            a = jnp.exp(inter - mt)
            q = z_ref[bi, :, h * ML_DK:(h + 1) * ML_DK]
            k = z_ref[bi, :, ML_HEADS * ML_DK + h * ML_DK:ML_HEADS * ML_DK + (h + 1) * ML_DK] * (ML_DK ** -0.5)
            v = z_ref[bi, :, 2 * ML_HEADS * ML_DK + h * ML_DV:2 * ML_HEADS * ML_DK + (h + 1) * ML_DV]
            og = z_ref[bi, :, 3 * ML_HEADS * ML_DK + h * ML_DV:3 * ML_HEADS * ML_DK + (h + 1) * ML_DV]
            s = mm_nt(q, k) * jnp.exp(dmat - mt)
            vext = jnp.concatenate([v, one_hot0], axis=-1)
            cn = cn_ref[bi, h]
            tot = a * mm(q, cn) + mm(s, vext)
            num = tot[:, :ML_DV]
            den = tot[:, ML_DV:ML_DV + 1]
            hh = num / jnp.maximum(jnp.abs(den), jnp.exp(-mt))
            hh = hh * lax.rsqrt(jnp.mean(hh * hh, axis=-1, keepdims=True) + RMS_EPS) * nw_ref[:, h * ML_DV:(h + 1) * ML_DV]
            outs.append(hh * _sigmoid(og))
            m_new = mt[L - 1:L, :]
            b_last = b_col[L - 1:L, :]
            wk = jnp.exp(b_last - b_col + ig_col - m_new)
            dec = jnp.exp(b_last + m_h - m_new)
            cn_ref[bi, h] = dec * cn + mm_tn(k, wk * vext)
            new_m = jnp.where(lane == h, m_new, new_m)
        m_ref[bi] = new_m
        h_ref[bi] = jnp.concatenate(outs, axis=-1)


def _mlstm(z_ml, z_g, bias_row, norm_row, cn0, m0, B, T, L, precise):
    nc = T // L
    bb = min(B, 4)
    tok = lambda b, c: (b, c, 0)
    st4 = lambda b, c: (b, 0, 0, 0)
    st3 = lambda b, c: (b, 0, 0)
    return pl.pallas_call(
        functools.partial(_mlstm_kernel, L=L, BB=bb, precise=precise),
        grid=(B // bb, nc),
        in_specs=[pl.BlockSpec((bb, L, ML_W), tok),
                  pl.BlockSpec((bb, L, LANES), tok),
                  pl.BlockSpec((1, LANES), lambda b, c: (0, 0)),
                  pl.BlockSpec((1, ML_HEADS * ML_DV), lambda b, c: (0, 0)),
                  pl.BlockSpec((bb, ML_HEADS, ML_DK, LANES), st4),
                  pl.BlockSpec((bb, 1, LANES), st3)],
        out_specs=[pl.BlockSpec((bb, L, ML_HEADS * ML_DV), tok),
                   pl.BlockSpec((bb, ML_HEADS, ML_DK, LANES), st4),
                   pl.BlockSpec((bb, 1, LANES), st3)],
        out_shape=[SDS((B, T, ML_HEADS * ML_DV), f32),
                   SDS((B, ML_HEADS, ML_DK, LANES), f32),
                   SDS((B, 1, LANES), f32)],
        compiler_params=_cparams("parallel", "arbitrary"),
        name="mlstm",
    )(z_ml.reshape(B, T, ML_W), z_g.reshape(B, T, LANES), bias_row, norm_row, cn0, m0)


ML_PAIRS = ML_HEADS // 2
ML_REP_QUANTS = 3


def _pair_select_matrix():
    sel = np.zeros((LANES, ML_REP_QUANTS * ML_PAIRS * LANES), np.float32)
    for qn in range(ML_REP_QUANTS):
        for pr in range(ML_PAIRS):
            for half in range(2):
                lo = (qn * ML_PAIRS + pr) * LANES + half * ML_DV
                sel[ML_HEADS * qn + 2 * pr + half, lo:lo + ML_DV] = 1.0
    return jnp.asarray(sel, bf16)


def _exact_select(x, sel):
    hi = x.astype(bf16)
    r1 = x - hi.astype(f32)
    mid = r1.astype(bf16)
    lo = (r1 - mid.astype(f32)).astype(bf16)
    mm = lambda t: jnp.dot(t, sel, preferred_element_type=f32)
    return (mm(hi) + mm(mid)) + mm(lo)


def _mlstm_pair_kernel(z_ref, g_ref, bias_ref, nw_ref, sel_ref, cbd0_ref, nbd0_ref, m0_ref,
                       h_ref, cbd_ref, nbd_ref, m_ref, *, L, BB):
    @pl.when(pl.program_id(1) == 0)
    def _():
        cbd_ref[...] = cbd0_ref[...]
        nbd_ref[...] = nbd0_ref[...]
        m_ref[...] = m0_ref[...]

    tri = (lax.broadcasted_iota(jnp.int32, (L, L), 0) >= lax.broadcasted_iota(jnp.int32, (L, L), 1)).astype(f32)
    row_t = lax.broadcasted_iota(jnp.int32, (L, LANES), 0)
    lane_t = lax.broadcasted_iota(jnp.int32, (L, LANES), 1)
    first_half = lane_t < ML_DV
    causal2 = row_t >= (lane_t % ML_DV)
    rr = lax.broadcasted_iota(jnp.int32, (LANES, LANES), 0)
    cc = lax.broadcasted_iota(jnp.int32, (LANES, LANES), 1)
    same_block = (rr < ML_DV) == (cc < ML_DV)
    ones_bd = same_block.astype(bf16)
    lane_1 = lax.broadcasted_iota(jnp.int32, (1, LANES), 1)
    sel = sel_ref[...]
    n_tiles = ML_REP_QUANTS * ML_PAIRS

    gs = [g_ref[bi] + bias_ref[...] for bi in range(BB)]
    lfs = [jnp.minimum(g, 0.0) - jnp.log(1.0 + jnp.exp(-jnp.abs(g))) for g in gs]
    bc_all = _dot(tri, jnp.concatenate(lfs, axis=1))
    g_ts = [g.T for g in gs]
    b_ts = [bc_all[:, bi * LANES:(bi + 1) * LANES].T for bi in range(BB)]
    r_rows = [g_ts[bi][0:ML_HEADS] - b_ts[bi][ML_HEADS:2 * ML_HEADS] for bi in range(BB)]
    cm = jnp.concatenate([jnp.concatenate(r_rows, axis=0), jnp.full((BB * ML_HEADS, LANES - L), -jnp.inf, f32)], axis=1)
    shift = 1
    while shift < L:
        cm = jnp.maximum(cm, pltpu.roll(cm, shift, 1))
        shift *= 2
    cols = [jnp.concatenate([g_ts[bi][0:ML_HEADS], b_ts[bi][ML_HEADS:2 * ML_HEADS],
                             cm[bi * ML_HEADS:(bi + 1) * ML_HEADS, :L],
                             jnp.zeros((LANES - 3 * ML_HEADS, L), f32)], axis=0).T for bi in range(BB)]
    rep_all = _exact_select(jnp.concatenate(cols, axis=0), sel)
    rep = [rep_all[bi * L:(bi + 1) * L] for bi in range(BB)]
    m_all = jnp.concatenate([m_ref[bi] for bi in range(BB)] + [jnp.zeros((8 - BB, LANES), f32)], axis=0)
    m_rep_all = _exact_select(m_all, sel[:, :ML_PAIRS * LANES])
    m_rep = [m_rep_all[bi:bi + 1] for bi in range(BB)]

    units = [(bi, pr) for bi in range(BB) for pr in range(ML_PAIRS)]
    st = {}
    for u in units:
        bi, pr = u
        tile = lambda qn: rep[bi][:, (qn * ML_PAIRS + pr) * LANES:(qn * ML_PAIRS + pr + 1) * LANES]
        ig_rep, b_rep, cm_rep = tile(0), tile(1), tile(2)
        m_pair = m_rep[bi][:, pr * LANES:(pr + 1) * LANES]
        inter = b_rep + m_pair
        mt = jnp.maximum(inter, b_rep + cm_rep)
        r_row = jnp.concatenate([r_rows[bi][2 * pr:2 * pr + 1], r_rows[bi][2 * pr + 1:2 * pr + 2]], axis=1)
        e = jnp.exp(jnp.where(causal2, (b_rep - mt) + r_row, -jnp.inf))
        m_new = mt[L - 1:L]
        b_last = b_rep[L - 1:L]
        lo = pr * LANES
        q = z_ref[bi, :, lo:lo + LANES].astype(bf16)
        k = z_ref[bi, :, ML_HEADS * ML_DK + lo:ML_HEADS * ML_DK + lo + LANES] * (ML_DK ** -0.5)
        v = z_ref[bi, :, 2 * ML_HEADS * ML_DK + lo:2 * ML_HEADS * ML_DK + lo + LANES]
        kbd = jnp.concatenate([jnp.where(first_half, k, 0.0), jnp.where(first_half, 0.0, k)], axis=0).astype(bf16)
        vbd = jnp.concatenate([jnp.where(first_half, v, 0.0), jnp.where(first_half, 0.0, v)], axis=0).astype(bf16)
        wk = jnp.exp(b_last - b_rep + ig_rep - m_new)
        st[u] = dict(a=jnp.exp(inter - mt), em=jnp.exp(-mt), e=e, m_new=m_new, dec=jnp.exp(b_last + m_pair - m_new),
                     q=q, kbd=kbd, vbd=vbd, k_t=k.T.astype(bf16), wkv=(wk * v).astype(bf16), wk=wk.astype(bf16),
                     cbd=cbd_ref[bi, pr], nbd=nbd_ref[bi, pr])
    for u in units:
        d = st[u]
        d["qk"] = lax.dot_general(d["q"], d["kbd"], (((1,), (1,)), ((), ())), preferred_element_type=f32)
        d["qc"] = jnp.dot(d["q"], d["cbd"].astype(bf16), preferred_element_type=f32)
        d["qn"] = jnp.dot(d["q"], d["nbd"].astype(bf16), preferred_element_type=f32)
    for u in units:
        d = st[u]
        s = (d["qk"] * d["e"]).astype(bf16)
        num = d["a"] * d["qc"] + jnp.dot(s, d["vbd"], preferred_element_type=f32)
        den = d["a"] * d["qn"] + jnp.dot(s, ones_bd, preferred_element_type=f32)
        d["hh"] = num / jnp.maximum(jnp.abs(den), d["em"])
    for u in units:
        bi, pr = u
        d = st[u]
        sq = d["hh"] * d["hh"]
        sq_hi = sq.astype(bf16)
        sq_lo = (sq - sq_hi.astype(f32)).astype(bf16)
        ms = (jnp.dot(sq_hi, ones_bd, preferred_element_type=f32)
              + jnp.dot(sq_lo, ones_bd, preferred_element_type=f32)) * (1.0 / ML_DV)
        lo = pr * LANES
        og = z_ref[bi, :, 3 * ML_HEADS * ML_DK + lo:3 * ML_HEADS * ML_DK + lo + LANES]
        h_ref[bi, :, lo:lo + LANES] = d["hh"] * lax.rsqrt(ms + RMS_EPS) * nw_ref[:, lo:lo + LANES] * _sigmoid(og)
    for u in units:
        bi, pr = u
        d = st[u]
        cbd_ref[bi, pr] = d["dec"] * d["cbd"] + jnp.where(
            same_block, jnp.dot(d["k_t"], d["wkv"], preferred_element_type=f32), 0.0)
        nbd_ref[bi, pr] = d["dec"] * d["nbd"] + jnp.where(
            same_block, jnp.dot(d["k_t"], d["wk"], preferred_element_type=f32), 0.0)
    for bi in range(BB):
        new_m = m_ref[bi]
        for pr in range(ML_PAIRS):
            m_new = st[(bi, pr)]["m_new"]
            new_m = jnp.where(lane_1 == 2 * pr, m_new[:, 0:1], new_m)
            new_m = jnp.where(lane_1 == 2 * pr + 1, m_new[:, ML_DV:ML_DV + 1], new_m)
        m_ref[bi] = new_m


def _mlstm_pairs(z_ml, z_g, bias_row, norm_row, B, T, L):
    assert 2 * L == LANES and ML_DK == ML_DV == L
    nc = T // L
    bb = min(B, 4)
    tok = lambda b, c: (b, c, 0)
    st4 = lambda b, c: (b, 0, 0, 0)
    st3 = lambda b, c: (b, 0, 0)
    const = lambda b, c: (0, 0)
    sel = _pair_select_matrix()
    zeros_bd = jnp.zeros((B, ML_PAIRS, LANES, LANES), f32)
    h, cbd, nbd, m = pl.pallas_call(
        functools.partial(_mlstm_pair_kernel, L=L, BB=bb),
        grid=(B // bb, nc),
        in_specs=[pl.BlockSpec((bb, L, ML_W), tok),
                  pl.BlockSpec((bb, L, LANES), tok),
                  pl.BlockSpec((1, LANES), const),
                  pl.BlockSpec((1, ML_HEADS * ML_DV), const),
                  pl.BlockSpec(sel.shape, const),
                  pl.BlockSpec((bb, ML_PAIRS, LANES, LANES), st4),
                  pl.BlockSpec((bb, ML_PAIRS, LANES, LANES), st4),
                  pl.BlockSpec((bb, 1, LANES), st3)],
        out_specs=[pl.BlockSpec((bb, L, ML_HEADS * ML_DV), tok),
                   pl.BlockSpec((bb, ML_PAIRS, LANES, LANES), st4),
                   pl.BlockSpec((bb, ML_PAIRS, LANES, LANES), st4),
                   pl.BlockSpec((bb, 1, LANES), st3)],
        out_shape=[SDS((B, T, ML_HEADS * ML_DV), f32),
                   SDS((B, ML_PAIRS, LANES, LANES), f32),
                   SDS((B, ML_PAIRS, LANES, LANES), f32),
                   SDS((B, 1, LANES), f32)],
        compiler_params=_cparams("parallel", "arbitrary"),
        name="mlstm_pairs",
    )(z_ml.reshape(B, T, ML_W), z_g.reshape(B, T, LANES), bias_row, norm_row, sel, zeros_bd, zeros_bd,
      jnp.zeros((B, 1, LANES), f32))
    c_out = jnp.stack([cbd[:, :, :ML_DK, :ML_DV], cbd[:, :, ML_DK:, ML_DV:]], axis=2).reshape(B, ML_HEADS, ML_DK, ML_DV)
    n_out = jnp.stack([nbd[:, :, :ML_DK, 0], nbd[:, :, ML_DK:, ML_DV]], axis=2).reshape(B, ML_HEADS, ML_DK)
    return h, c_out, n_out, m[:, 0, :ML_HEADS]


def _rope(x, cos, sin_signed):
    w = x.shape[-1]
    lane = lax.broadcasted_iota(jnp.int32, x.shape, 1)
    swapped = jnp.where((lane % SW_HD) < SW_HD // 2, pltpu.roll(x, w - SW_HD // 2, 1), pltpu.roll(x, SW_HD // 2, 1))
    return x * cos + swapped * sin_signed


def _swa_attend(jobs, sinks_ref, L, precise=False):
    mm, mm_nt, _ = _dots(precise)
    units = [(j, g) for j in range(len(jobs)) for g in range(SW_KV_HEADS)]
    sinks = [jnp.concatenate(
        [jnp.broadcast_to(sinks_ref[:, g * SW_GROUP + i:g * SW_GROUP + i + 1], (L, 1)) for i in range(SW_GROUP)],
        axis=0) for g in range(SW_KV_HEADS)]
    s, p, sink_e = {}, {}, {}
    for j, g in units:
        qr, keys, _, _ = jobs[j]
        q4 = jnp.concatenate([qr[:, (g * SW_GROUP + i) * SW_HD:(g * SW_GROUP + i + 1) * SW_HD]
                              for i in range(SW_GROUP)], axis=0)
        s[j, g] = mm_nt(q4, keys[:, g * SW_HD:(g + 1) * SW_HD]) * (SW_HD ** -0.5)
    for j, g in units:
        first_valid = jobs[j][3]
        sc = s[j, g]
        if first_valid is not None:
            kcol = lax.broadcasted_iota(jnp.int32, (1, sc.shape[1]), 1)
            sc = jnp.where(kcol >= first_valid, sc, -jnp.inf)
        mx = jnp.maximum(jnp.max(sc, axis=-1, keepdims=True), sinks[g])
        p[j, g] = jnp.exp(sc - mx)
        sink_e[j, g] = jnp.exp(sinks[g] - mx)
    ones = jnp.ones((jobs[0][1].shape[0], SW_HD), f32)
    o = {u: mm(p[u], jobs[u[0]][2][:, u[1] * SW_HD:(u[1] + 1) * SW_HD]) / (mm(p[u], ones) + sink_e[u]) for u in units}
    return [jnp.concatenate([o[j, g][i * L:(i + 1) * L, :] for g in range(SW_KV_HEADS) for i in range(SW_GROUP)],
                            axis=-1) for j in range(len(jobs))]


def _swa_prompt_kernel(q_ref, kp_ref, kc_ref, vp_ref, vc_ref, cp_ref, cc_ref, sp_ref, sc_ref, sinks_ref,
                       o_ref, kr_ref, *, L, CB):
    i = pl.program_id(1)
    rows = CB * L
    back = 2 * L
    cos_q = jnp.concatenate([cc_ref[...]] * (SW_HEADS // SW_KV_HEADS), axis=-1)
    sin_q = jnp.concatenate([sc_ref[...]] * (SW_HEADS // SW_KV_HEADS), axis=-1)
    qr = _rope(q_ref[...], cos_q, sin_q)
    k_cur = _rope(kc_ref[...], cc_ref[...], sc_ref[...])
    kr_ref[...] = k_cur
    k_prev = _rope(kp_ref[rows - back:rows, :], cp_ref[rows - back:rows, :], sp_ref[rows - back:rows, :])
    keys = jnp.concatenate([k_prev, k_cur], axis=0)
    vals = jnp.concatenate([vp_ref[rows - back:rows, :], vc_ref[...]], axis=0)
    jobs = []
    for u in range(CB):
        first_valid = jnp.where(i == 0, back - u * L, 0) if u * L < back else None
        jobs.append((qr[u * L:(u + 1) * L], keys[u * L:(u + 3) * L], vals[u * L:(u + 3) * L], first_valid))
    o_ref[...] = jnp.concatenate(_swa_attend(jobs, sinks_ref, L), axis=0)


def _swa_prompt(z_sw, cos_t, sin_t, sinks_row, B, T, L):
    cb = 4
    rows = cb * L
    nb = T // rows
    n = B * T
    kcol = SW_HEADS * SW_HD // LANES
    vcol = kcol + 1
    cur = lambda b, i: (b * nb + i, 0)
    prev = lambda col: (lambda b, i: (b * nb + jnp.maximum(i - 1, 0), col))
    curc = lambda col: (lambda b, i: (b * nb + i, col))
    tab_cur = lambda b, i: (i, 0)
    tab_prev = lambda b, i: (jnp.maximum(i - 1, 0), 0)
    return pl.pallas_call(
        functools.partial(_swa_prompt_kernel, L=L, CB=cb),
        grid=(B, nb),
        in_specs=[pl.BlockSpec((rows, SW_HEADS * SW_HD), cur),
                  pl.BlockSpec((rows, LANES), prev(kcol)), pl.BlockSpec((rows, LANES), curc(kcol)),
                  pl.BlockSpec((rows, LANES), prev(vcol)), pl.BlockSpec((rows, LANES), curc(vcol)),
                  pl.BlockSpec((rows, LANES), tab_prev), pl.BlockSpec((rows, LANES), tab_cur),
                  pl.BlockSpec((rows, LANES), tab_prev), pl.BlockSpec((rows, LANES), tab_cur),
                  pl.BlockSpec((1, LANES), lambda b, i: (0, 0))],
        out_specs=[pl.BlockSpec((rows, SW_HEADS * SW_HD), cur),
                   pl.BlockSpec((rows, LANES), cur)],
        out_shape=[SDS((n, SW_HEADS * SW_HD), f32), SDS((n, LANES), f32)],
        compiler_params=_cparams("parallel", "parallel"),
        name="swa_prompt",
    )(z_sw, z_sw, z_sw, z_sw, z_sw, cos_t, cos_t, sin_t, sin_t, sinks_row)


def _swa_sample_kernel(q_ref, k_ref, v_ref, ck_ref, cv_ref, cos_ref, sin_ref, sinks_ref, o_ref, kr_ref, *, L):
    cos_q = jnp.concatenate([cos_ref[...]] * (SW_HEADS // SW_KV_HEADS), axis=-1)
    sin_q = jnp.concatenate([sin_ref[...]] * (SW_HEADS // SW_KV_HEADS), axis=-1)
    qr = _rope(q_ref[...], cos_q, sin_q)
    kr = _rope(k_ref[...], cos_ref[...], sin_ref[...])
    kr_ref[...] = kr
    keys = jnp.concatenate([ck_ref[0], kr], axis=0)
    vals = jnp.concatenate([cv_ref[0], v_ref[...]], axis=0)
    o_ref[...] = _swa_attend([(qr, keys, vals, None)], sinks_ref, L, precise=True)[0]


def _swa_sample(z_sw, cache_k, cache_v, cos_t, sin_t, sinks_row, B, T):
    n = B * T
    kcol = SW_HEADS * SW_HD // LANES
    return pl.pallas_call(
        functools.partial(_swa_sample_kernel, L=T),
        grid=(B,),
        in_specs=[pl.BlockSpec((T, SW_HEADS * SW_HD), lambda b: (b, 0)),
                  pl.BlockSpec((T, LANES), lambda b: (b, kcol)),
                  pl.BlockSpec((T, LANES), lambda b: (b, kcol + 1)),
                  pl.BlockSpec((1, WINDOW, LANES), lambda b: (b, 0, 0)),
                  pl.BlockSpec((1, WINDOW, LANES), lambda b: (b, 0, 0)),
                  pl.BlockSpec((T, LANES), lambda b: (0, 0)),
                  pl.BlockSpec((T, LANES), lambda b: (0, 0)),
                  pl.BlockSpec((1, LANES), lambda b: (0, 0))],
        out_specs=[pl.BlockSpec((T, SW_HEADS * SW_HD), lambda b: (b, 0)),
                   pl.BlockSpec((T, LANES), lambda b: (b, 0))],
        out_shape=[SDS((n, SW_HEADS * SW_HD), f32), SDS((n, LANES), f32)],
        compiler_params=_cparams("parallel"),
        name="swa_sample",
    )(z_sw, z_sw, z_sw, cache_k, cache_v, cos_t, sin_t, sinks_row)


def _out_ln_kernel(*refs, n_in):
    x_ref = refs[0]
    a_refs = refs[1:1 + n_in]
    w_refs = refs[1 + n_in:1 + 2 * n_in]
    g_ref, b_ref, rw_ref, rb_ref, o_ref, gates_ref, route_ref = refs[1 + 2 * n_in:]
    mm = _dots(w_refs[0].dtype == f32)[0]
    y = mm(a_refs[0][...], w_refs[0][...])
    for a_ref, w_ref in zip(a_refs[1:], w_refs[1:]):
        y = y + mm(a_ref[...], w_ref[...])
    x_new = _layer_norm(DN_ALPHA * x_ref[...] + y, g_ref[...], b_ref[...])
    o_ref[...] = x_new
    gates_ref[...], route_ref[...] = _route(x_new, rw_ref, rb_ref)


def _out_ln(x2, acts, ws, g_row, b_row, rw_t, rb_col, tm):
    n = x2.shape[0]
    row = lambda i: (i, 0)
    const = lambda i: (0, 0)
    col = lambda i: (0, i)
    return pl.pallas_call(
        functools.partial(_out_ln_kernel, n_in=len(acts)),
        grid=(n // tm,),
        in_specs=[pl.BlockSpec((tm, D_MODEL), row)]
        + [pl.BlockSpec((tm, a.shape[1]), row) for a in acts]
        + [pl.BlockSpec(w.shape, const) for w in ws]
        + [pl.BlockSpec((1, D_MODEL), const), pl.BlockSpec((1, D_MODEL), const),
           pl.BlockSpec((N_EXPERTS, D_MODEL), const), pl.BlockSpec((N_EXPERTS, 1), const)],
        out_specs=[pl.BlockSpec((tm, D_MODEL), row), pl.BlockSpec((N_EXPERTS, tm), col), pl.BlockSpec((8, tm), col)],
        out_shape=[SDS((n, D_MODEL), f32), SDS((N_EXPERTS, n), f32), SDS((8, n), f32)],
        compiler_params=_cparams("parallel"),
        name="out_proj_ln",
    )(x2, *acts, *ws, g_row, b_row, rw_t, rb_col)


def _logistic(x):
    return 1.0 / (1.0 + jnp.exp(-x))


def _route(x, rw_ref, rb_ref):
    logits = _dot_nt(rw_ref[...], x)
    aff = _logistic(logits)
    sc = aff + rb_ref[...]
    s = [sc[e:e + 1, :] for e in range(N_EXPERTS)]
    a = [aff[e:e + 1, :] for e in range(N_EXPERTS)]
    scores = []
    for gi in range(N_GROUPS):
        w, x, y, z = s[4 * gi:4 * gi + 4]
        p, q = jnp.maximum(w, x), jnp.minimum(w, x)
        r, t = jnp.maximum(y, z), jnp.minimum(y, z)
        scores.append(jnp.maximum(p, r) + jnp.maximum(jnp.minimum(p, r), jnp.maximum(q, t)))
    best = scores[0]
    gsel = jnp.zeros_like(best, dtype=jnp.int32)
    for gi in range(1, N_GROUPS):
        better = scores[gi] > best
        best = jnp.where(better, scores[gi], best)
        gsel = jnp.where(better, gi, gsel)
    sel = []
    for e in range(N_EXPERTS):
        gi, i = divmod(e, EXP_PER_GROUP)
        beaten = jnp.zeros_like(gsel)
        for j in range(EXP_PER_GROUP):
            if j == i:
                continue
            o = s[4 * gi + j]
            wins = (o >= s[e]) if j < i else (o > s[e])
            beaten = beaten + wins.astype(jnp.int32)
        sel.append((gsel == gi) & (beaten < 2))
    den = jnp.zeros_like(best)
    for e in range(N_EXPERTS):
        den = den + jnp.where(sel[e], a[e], 0.0)
    gate = [jnp.where(sel[e], a[e] / den, 0.0) for e in range(N_EXPERTS)]
    taken = jnp.zeros_like(gsel)
    ea = eb = wa = wb = jnp.zeros_like(best)
    for e in range(N_EXPERTS):
        first = sel[e] & (taken == 0)
        second = sel[e] & (taken == 1)
        ea = jnp.where(first, float(e), ea)
        wa = jnp.where(first, gate[e], wa)
        eb = jnp.where(second, float(e), eb)
        wb = jnp.where(second, gate[e], wb)
        taken = taken + sel[e].astype(jnp.int32)
    route = jnp.concatenate([ea, eb, wa, wb, jnp.zeros((4, ea.shape[1]), f32)], axis=0)
    return jnp.concatenate(gate, axis=0), route


def _moe_kernel(x_ref, gates_ref, wg_ref, wu_ref, wd_ref, g_ref, b_ref, o_ref, xb_ref, acc_ref):
    e = pl.program_id(1)

    @pl.when(e == 0)
    def _():
        xb_ref[...] = x_ref[...].astype(bf16)
        acc_ref[...] = jnp.zeros_like(acc_ref)

    xb = xb_ref[...]
    lane = lax.broadcasted_iota(jnp.int32, gates_ref.shape, 1)
    gcol = jnp.sum(jnp.where(lane == e, gates_ref[...], 0.0), axis=-1, keepdims=True)
    h = _silu(_bdot(xb, wg_ref[0])) * _bdot(xb, wu_ref[0])
    acc_ref[...] += _bdot(gcol * h, wd_ref[0])

    @pl.when(e == N_EXPERTS - 1)
    def _():
        o_ref[...] = _layer_norm(DN_ALPHA * x_ref[...] + acc_ref[...], g_ref[...], b_ref[...])


def _moe_ln(x2, gates, wg, wu, wd, base, g_row, b_row, tm):
    n = x2.shape[0]
    row = lambda i, e: (i, 0)
    const = lambda i, e: (0, 0)
    expert = lambda i, e: (base + e, 0, 0)
    return pl.pallas_call(
        _moe_kernel,
        grid=(n // tm, N_EXPERTS),
        in_specs=[pl.BlockSpec((tm, D_MODEL), row),
                  pl.BlockSpec((tm, N_EXPERTS), row),
                  pl.BlockSpec((1, D_MODEL, D_EXPERT), expert),
                  pl.BlockSpec((1, D_MODEL, D_EXPERT), expert),
                  pl.BlockSpec((1, D_EXPERT, D_MODEL), expert),
                  pl.BlockSpec((1, D_MODEL), const), pl.BlockSpec((1, D_MODEL), const)],
        out_specs=pl.BlockSpec((tm, D_MODEL), row),
        out_shape=SDS((n, D_MODEL), f32),
        scratch_shapes=[pltpu.VMEM((tm, D_MODEL), bf16), pltpu.VMEM((tm, D_MODEL), f32)],
        compiler_params=_cparams("parallel", "arbitrary"),
        name="moe_ln",
    )(x2, gates, wg, wu, wd, g_row, b_row)


N_PAIRS = N_GROUPS * (EXP_PER_GROUP * (EXP_PER_GROUP - 1) // 2)
MOE_TM = 256
MOE_DMA_ROWS = 512
_PAIR_A = [g * EXP_PER_GROUP + a for g in range(N_GROUPS) for a in range(EXP_PER_GROUP) for b in range(a + 1, EXP_PER_GROUP)]
_PAIR_B = [g * EXP_PER_GROUP + b for g in range(N_GROUPS) for a in range(EXP_PER_GROUP) for b in range(a + 1, EXP_PER_GROUP)]


def _gather_rows_kernel(idx_ref, src_ref, o_ref, sem, *, rows):
    base = pl.program_id(0) * rows

    def row_copy(j):
        return pltpu.make_async_copy(src_ref.at[pl.ds(idx_ref[base + j], 1)], o_ref.at[pl.ds(j, 1)], sem)

    def issue(j, carry):
        row_copy(j).start()
        return carry

    def drain(j, carry):
        row_copy(j).wait()
        return carry

    lax.fori_loop(0, rows, issue, 0, unroll=8)
    lax.fori_loop(0, rows, drain, 0, unroll=8)


def _gather_rows(src, idx, rows):
    n_out = idx.shape[0]
    d = src.shape[1]
    return pl.pallas_call(
        functools.partial(_gather_rows_kernel, rows=rows),
        grid_spec=pltpu.PrefetchScalarGridSpec(
            num_scalar_prefetch=1,
            grid=(n_out // rows,),
            in_specs=[pl.BlockSpec(memory_space=pl.ANY)],
            out_specs=pl.BlockSpec((rows, d), lambda i, idx_ref: (i, 0)),
            scratch_shapes=[pltpu.SemaphoreType.DMA(())]),
        out_shape=SDS((n_out, d), f32),
        compiler_params=_cparams("arbitrary"),
        name="gather_rows",
    )(idx, src)


def _scatter_rows_kernel(idx_ref, src_ref, init_ref, o_ref, sem, *, rows):
    del init_ref
    base = pl.program_id(0) * rows

    def row_copy(j):
        return pltpu.make_async_copy(src_ref.at[pl.ds(j, 1)], o_ref.at[pl.ds(idx_ref[base + j], 1)], sem)

    def issue(j, carry):
        row_copy(j).start()
        return carry

    def drain(j, carry):
        row_copy(j).wait()
        return carry

    lax.fori_loop(0, rows, issue, 0, unroll=8)
    lax.fori_loop(0, rows, drain, 0, unroll=8)


def _scatter_rows(src, idx, n_out, rows, init=None):
    n_src, d = src.shape
    if init is None:
        init = jnp.zeros((n_out, d), f32)
    return pl.pallas_call(
        functools.partial(_scatter_rows_kernel, rows=rows),
        grid_spec=pltpu.PrefetchScalarGridSpec(
            num_scalar_prefetch=1,
            grid=(n_src // rows,),
            in_specs=[pl.BlockSpec((rows, d), lambda i, idx_ref: (i, 0)), pl.BlockSpec(memory_space=pl.ANY)],
            out_specs=pl.BlockSpec(memory_space=pl.ANY),
            scratch_shapes=[pltpu.SemaphoreType.DMA(())]),
        out_shape=SDS((n_out, d), f32),
        input_output_aliases={2: 0},
        compiler_params=_cparams("arbitrary"),
        name="scatter_rows",
    )(idx, src, init)


def _pair_plan(route, n, tm):
    ea = route[0].astype(jnp.int32)
    eb = route[1].astype(jnp.int32)
    a = ea % EXP_PER_GROUP
    b = eb % EXP_PER_GROUP
    pidx = jnp.where(a == 0, b - 1, jnp.where(a == 1, b + 1, 5))
    pair = (ea // EXP_PER_GROUP) * (N_PAIRS // N_GROUPS) + pidx
    onehot = (pair[:, None] == jnp.arange(N_PAIRS, dtype=jnp.int32)[None, :]).astype(jnp.int32)
    csum = jnp.cumsum(onehot, axis=0)
    counts = csum[-1]
    ntiles = (counts + tm - 1) // tm
    tile_end = jnp.cumsum(ntiles)
    tile_start = tile_end - ntiles
    row_of_token = jnp.sum(onehot * (csum - 1 + (tile_start * tm)[None, :]), axis=1)
    nt = n // tm + N_PAIRS
    tile_id = jnp.arange(nt, dtype=jnp.int32)
    tile_valid = tile_id < tile_end[-1]
    tile_pair = jnp.sum((tile_end[None, :] <= jnp.minimum(tile_id, tile_end[-1] - 1)[:, None]).astype(jnp.int32), axis=1)
    tile_pair = jnp.minimum(tile_pair, N_PAIRS - 1)
    pick = (tile_pair[:, None] == jnp.arange(N_PAIRS, dtype=jnp.int32)[None, :]).astype(jnp.int32)
    tile_a = jnp.sum(pick * jnp.asarray(_PAIR_A, jnp.int32)[None, :], axis=1)
    tile_b = jnp.sum(pick * jnp.asarray(_PAIR_B, jnp.int32)[None, :], axis=1)
    return row_of_token, tile_a, tile_b, tile_valid.astype(jnp.int32)


def _pair_expert_kernel(ta_ref, tb_ref, tv_ref, x_ref, rwt_ref, wga_ref, wua_ref, wda_ref, wgb_ref, wub_ref, wdb_ref,
                        g_ref, b_ref, o_ref):
    i = pl.program_id(0)
    valid = tv_ref[i] == 1

    @pl.when(valid)
    def _():
        x = x_ref[...]
        xb = x.astype(bf16)
        aff_a = _logistic(jnp.sum(x * rwt_ref[pl.ds(ta_ref[i], 1), :], axis=-1, keepdims=True))
        aff_b = _logistic(jnp.sum(x * rwt_ref[pl.ds(tb_ref[i], 1), :], axis=-1, keepdims=True))
        den = aff_a + aff_b
        acc = None
        for w, (wg, wu, wd) in ((aff_a / den, (wga_ref, wua_ref, wda_ref)), (aff_b / den, (wgb_ref, wub_ref, wdb_ref))):
            h = _silu(_bdot(xb, wg[0])) * _bdot(xb, wu[0])
            y = _bdot(w * h, wd[0])
            acc = y if acc is None else acc + y
        o_ref[...] = _layer_norm(DN_ALPHA * x + acc, g_ref[...], b_ref[...])

    @pl.when(jnp.logical_not(valid))
    def _():
        o_ref[...] = jnp.zeros_like(o_ref)


def _pair_experts(xs, rw_t, tile_a, tile_b, tile_valid, wg, wu, wd, base, g_row, b_row, tm):
    rows = xs.shape[0]
    row = lambda i, ta, tb, tv: (i, 0)
    const = lambda i, ta, tb, tv: (0, 0)
    ex_a = lambda i, ta, tb, tv: (base + ta[i], 0, 0)
    ex_b = lambda i, ta, tb, tv: (base + tb[i], 0, 0)
    up = pl.BlockSpec((1, D_MODEL, D_EXPERT), ex_a), pl.BlockSpec((1, D_MODEL, D_EXPERT), ex_b)
    down = pl.BlockSpec((1, D_EXPERT, D_MODEL), ex_a), pl.BlockSpec((1, D_EXPERT, D_MODEL), ex_b)
    return pl.pallas_call(
        _pair_expert_kernel,
        grid_spec=pltpu.PrefetchScalarGridSpec(
            num_scalar_prefetch=3,
            grid=(rows // tm,),
            in_specs=[pl.BlockSpec((tm, D_MODEL), row), pl.BlockSpec((N_EXPERTS, D_MODEL), const),
                      up[0], up[0], down[0], up[1], up[1], down[1],
                      pl.BlockSpec((1, D_MODEL), const), pl.BlockSpec((1, D_MODEL), const)],
            out_specs=pl.BlockSpec((tm, D_MODEL), row)),
        out_shape=SDS((rows, D_MODEL), f32),
        compiler_params=_cparams("arbitrary"),
        name="pair_experts",
    )(tile_a, tile_b, tile_valid, xs, rw_t, wg, wu, wd, wg, wu, wd, g_row, b_row)


def _unit_lower_inverses(mats, L):
    row = lax.broadcasted_iota(jnp.int32, (L, L), 0)
    col = lax.broadcasted_iota(jnp.int32, (L, L), 1)
    eye = (row == col).astype(f32)
    ps = [eye - a for a in mats]
    pws = [a.astype(bf16) for a in mats]
    span = 2
    while span < L:
        pws = [jnp.dot(pw, pw, preferred_element_type=f32).astype(bf16) for pw in pws]
        ps = [p + jnp.dot(pw, p.astype(bf16), preferred_element_type=f32) for p, pw in zip(ps, pws)]
        span *= 2
    return ps


def _gdn_kernel(x_ref, zg_ref, ba_ref, cw_ref, alog_ref, dt_ref, nw_ref, s0_ref, cb_ref,
                o_ref, s_ref, prev_ref, *, L, BB):
    @pl.when(pl.program_id(1) == 0)
    def _():
        s_ref[...] = s0_ref[...]
        prev_ref[...] = cb_ref[...]

    def conv_silu(bi, lo, width):
        cur = x_ref[bi, :, lo:lo + width]
        cat =jnp.concatenate([prev_ref[bi, :, lo:lo + width], cur], axis=0)
        acc = cat[5:5 + L] * cw_ref[0:1, lo:lo + width]
        acc = acc + cat[6:6 + L] * cw_ref[1:2, lo:lo + width]
        acc = acc + cat[7:7 + L] * cw_ref[2:3, lo:lo + width]
        acc = acc + cur * cw_ref[3:4, lo:lo + width]
        return _silu(acc)

    def l2n(v, scale):
        return v * lax.rsqrt(jnp.sum(v * v, axis=-1, keepdims=True) + 1e-6) * scale

    row = lax.broadcasted_iota(jnp.int32, (L, L), 0)
    col = lax.broadcasted_iota(jnp.int32, (L, L), 1)
    incl = row >= col
    strict = row > col
    rep = GD_V_HEADS // GD_QK_HEADS
    bas = [ba_ref[bi] for bi in range(BB)]
    betas = [_sigmoid(ba) for ba in bas]
    gls = [-jnp.exp(alog_ref[...]) * _softplus(ba + dt_ref[...]) for ba in bas]
    gcum_all = _dot(incl.astype(f32), jnp.concatenate(gls, axis=1))
    gcums = [gcum_all[:, bi * LANES:(bi + 1) * LANES] for bi in range(BB)]
    gcum_ts = [g.T for g in gcums]
    units = [(bi, hv) for bi in range(BB) for hv in range(GD_V_HEADS)]
    qs, ks, amats, qkds, egs, g_cols, rhss = {}, {}, [], {}, {}, {}, []
    for bi in range(BB):
        for j in range(GD_QK_HEADS):
            q = l2n(conv_silu(bi, j * GD_HD, GD_HD), GD_HD ** -0.5)
            k = l2n(conv_silu(bi, GD_QK_W + j * GD_HD, GD_HD), 1.0)
            qb, kb = q.astype(bf16), k.astype(bf16)
            kk = lax.dot_general(kb, kb, (((1,), (1,)), ((), ())), preferred_element_type=f32)
            qk = lax.dot_general(qb, kb, (((1,), (1,)), ((), ())), preferred_element_type=f32)
            qs[bi, j] = qb
            ks[bi, j] = k
            for r in range(rep):
                hv = j * rep + r
                v = conv_silu(bi, 2 * GD_QK_W + hv * GD_HD, GD_HD)
                g_col = gcums[bi][:, GD_V_HEADS + hv:GD_V_HEADS + hv + 1]
                g_row = gcum_ts[bi][GD_V_HEADS + hv:GD_V_HEADS + hv + 1, :]
                b_col = betas[bi][:, hv:hv + 1]
                decay = jnp.exp(jnp.where(incl, g_col - g_row, -jnp.inf))
                eg = jnp.exp(g_col)
                amats.append(jnp.where(strict, b_col * kk * decay, 0.0))
                qkds[bi, hv] = (qk * decay).astype(bf16)
                egs[bi, hv] = eg
                g_cols[bi, hv] = g_col
                rhss.append(((b_col * v).astype(bf16), ((b_col * eg) * k).astype(bf16)))
    tinvs = _unit_lower_inverses(amats, L)
    tinvs = [t.astype(bf16) for t in tinvs]
    sol_v = dict(zip(units, [jnp.dot(t, r[0], preferred_element_type=f32) for t, r in zip(tinvs, rhss)]))
    sol_k = dict(zip(units, [jnp.dot(t, r[1], preferred_element_type=f32) for t, r in zip(tinvs, rhss)]))
    sts, wks, qss, kts = {}, {}, {}, {}
    for u in units:
        bi, hv = u
        j = hv // rep
        sts[u] = s_ref[bi, hv]
        stb = sts[u].astype(bf16)
        wks[u] = jnp.dot(sol_k[u].astype(bf16), stb, preferred_element_type=f32)
        qss[u] = jnp.dot(qs[bi, j], stb, preferred_element_type=f32)
        g_col = g_cols[u]
        kts[u] = (jnp.exp(g_col[L - 1:L, :] - g_col) * ks[bi, j]).T.astype(bf16)
    outs = {}
    for u in units:
        bi, hv = u
        wnb = (sol_v[u] - wks[u]).astype(bf16)
        outs[u] = egs[u] * qss[u] + jnp.dot(qkds[u], wnb, preferred_element_type=f32)
        s_ref[bi, hv] = jnp.exp(g_cols[u][L - 1:L, :]) * sts[u] + jnp.dot(kts[u], wnb, preferred_element_type=f32)
    for u in units:
        bi, hv = u
        o = outs[u]
        o = o * lax.rsqrt(jnp.mean(o * o, axis=-1, keepdims=True) + RMS_EPS) * nw_ref[...]
        o_ref[bi, :, hv * GD_HD:(hv + 1) * GD_HD] = o * _silu(zg_ref[bi, :, hv * GD_HD:(hv + 1) * GD_HD])
    for bi in range(BB):
        prev_ref[bi] = x_ref[bi, L - 8:L, :]


def _gdn(qkv, zg, ba, conv_w, alog_row, dt_row, norm_row, s0, conv8, B, T, L):
    nc = T // L
    n = B * T
    bb = 2
    tok = lambda b, c: (b, c, 0)
    const = lambda b, c: (0, 0)
    st4 = lambda b, c: (b, 0, 0, 0)
    o, s_out = pl.pallas_call(
        functools.partial(_gdn_kernel, L=L, BB=bb),
        grid=(B // bb, nc),
        in_specs=[pl.BlockSpec((bb, L, GD_CONV_CH), tok),
                  pl.BlockSpec((bb, L, GD_V_W), tok),
                  pl.BlockSpec((bb, L, LANES), tok),
                  pl.BlockSpec((GD_CONV, GD_CONV_CH), const),
                  pl.BlockSpec((1, LANES), const),
                  pl.BlockSpec((1, LANES), const),
                  pl.BlockSpec((1, GD_HD), const),
                  pl.BlockSpec((bb, GD_V_HEADS, GD_HD, GD_HD), st4),
                  pl.BlockSpec((bb, 8, GD_CONV_CH), lambda b, c: (b, 0, 0))],
        out_specs=[pl.BlockSpec((bb, L, GD_V_W), tok),
                   pl.BlockSpec((bb, GD_V_HEADS, GD_HD, GD_HD), st4)],
        out_shape=[SDS((B, T, GD_V_W), f32), SDS((B, GD_V_HEADS, GD_HD, GD_HD), f32)],
        scratch_shapes=[pltpu.VMEM((bb, 8, GD_CONV_CH), f32)],
        compiler_params=_cparams("parallel", "arbitrary"),
        name="gdn",
    )(qkv.reshape(B, T, GD_CONV_CH), zg.reshape(B, T, GD_V_W), ba.reshape(B, T, LANES), conv_w, alog_row, dt_row,
      norm_row, s0, conv8)
    return o.reshape(n, GD_V_W), s_out


def _pad_lanes(row, offset=0):
    return jnp.zeros((1, LANES), f32).at[0, offset:offset + row.shape[0]].set(row.astype(f32))


def _rope_tables(pos):
    half = SW_HD // 2
    inv = ROPE_THETA ** (-jnp.arange(half, dtype=f32) / half)
    ang = pos.astype(f32)[:, None] * inv[None, :]
    cos, sin = jnp.cos(ang), jnp.sin(ang)
    cos_t = jnp.concatenate([cos, cos] * SW_KV_HEADS, axis=-1)
    sin_t = jnp.concatenate([-sin, sin] * SW_KV_HEADS, axis=-1)
    return cos_t, sin_t


def _tile(n, pref):
    return pref if n % pref == 0 else n


def _trunk(x, pos, L, state, p):
    B, T, _ = x.shape
    n = B * T
    x2 = x.reshape(n, D_MODEL)
    tm = _tile(n, 512)
    precise = state is not None
    ab_w, wo_h, wo_a = (p["ab_w32"], p["ab_wo_h32"], p["ab_wo_a32"]) if precise else (p["ab_w"], p["ab_wo_h"], p["ab_wo_a"])

    z_ml, z_sw, z_g = _proj(x2, ab_w, ((0, ML_W), (ML_W, SW_W), (ML_W + SW_W, LANES)), tm)
    if state is None:
        h_ml, ml_c, ml_n, ml_m = _mlstm_pairs(z_ml, z_g, p["ab_bias"], p["ab_norm"], B, T, L)
    else:
        cn0 = jnp.concatenate([state["ml_C"], state["ml_n"][..., None],
                               jnp.zeros((B, ML_HEADS, ML_DK, LANES - ML_DV - 1), f32)], axis=-1)
        m0 = jnp.zeros((B, 1, LANES), f32).at[:, 0, :ML_HEADS].set(state["ml_m"])
        h_ml, cn, m_out = _mlstm(z_ml, z_g, p["ab_bias"], p["ab_norm"], cn0, m0, B, T, L, precise)
        ml_c, ml_n, ml_m = cn[..., :ML_DV], cn[..., ML_DV], m_out[:, 0, :ML_HEADS]
    h_ml = h_ml.reshape(n, ML_HEADS * ML_DV)
    cos_t, sin_t = _rope_tables(pos)
    if state is None:
        a_sw, k_rot = _swa_prompt(z_sw, cos_t, sin_t, p["ab_sinks"], B, T, L)
    else:
        a_sw, k_rot = _swa_sample(z_sw, state["sw_k"].reshape(B, WINDOW, LANES),
                                  state["sw_v"].reshape(B, WINDOW, LANES), cos_t, sin_t, p["ab_sinks"], B, T)
    keep = min(T, WINDOW)
    new_k = k_rot.reshape(B, T, LANES)[:, T - keep:].reshape(B, keep, SW_KV_HEADS, SW_HD)
    new_v = z_sw.reshape(B, T, SW_W)[:, T - keep:, SW_HEADS * SW_HD + LANES:].reshape(B, keep, SW_KV_HEADS, SW_HD)
    x2, gates_t, route = _out_ln(x2, [h_ml, a_sw], [wo_h, wo_a], p["ln_g"][0][0], p["ln_b"][0][0],
                                 p["router_wt"], p["router_b"], tm)
    x2, spare = _moe_block(x2, gates_t, route, p, 0)

    tm1 = _tile(n, 256)
    if state is None:
        s0 = jnp.zeros((B, GD_V_HEADS, GD_HD, GD_HD), f32)
        conv8 = jnp.zeros((B, 8, GD_CONV_CH), f32)
    else:
        s0 = state["gd_S"]
        conv8 = jnp.concatenate([jnp.zeros((B, 8 - (GD_CONV - 1), GD_CONV_CH), f32), state["gd_conv"]], axis=1)
    qkv, zg, ba = _proj(x2, p["c_w"], ((0, GD_CONV_CH), (GD_CONV_CH, GD_V_W), (GD_CONV_CH + GD_V_W, LANES)), tm1)
    new_conv = qkv.reshape(B, T, GD_CONV_CH)[:, T - (GD_CONV - 1):]
    o_gd, s_out = _gdn(qkv, zg, ba, p["c_conv_w"], p["c_alog"], p["c_dt"], p["c_norm"], s0, conv8, B, T, L)
    x2, gates_t, route = _out_ln(x2, [o_gd], [p["c_wo"]], p["ln_g"][1][0], p["ln_b"][1][0],
                                 p["router_wt"], p["router_b"], tm)
    x2, _ = _moe_block(x2, gates_t, route, p, 1, spare)

    outs = (new_k[None], new_v[None], ml_c[None], ml_n[None], ml_m[None],
            s_out[None], new_conv[None])
    return x2.reshape(B, T, D_MODEL), outs


def _moe_block(x2, gates_t, route, p, layer, spare=None):
    n = x2.shape[0]
    wg, wu, wd = p["ex_gate"], p["ex_up"], p["ex_down"]
    base = layer * N_EXPERTS
    g_row, b_row = p["ln_g"][layer][1], p["ln_b"][layer][1]
    if n < N_PAIRS * MOE_TM:
        return _moe_ln(x2, gates_t.T, wg, wu, wd, base, g_row, b_row, _tile(n, 1024)), None
    row_of_token, tile_a, tile_b, tile_valid = _pair_plan(route, n, MOE_TM)
    xs = _scatter_rows(x2, row_of_token, n + N_PAIRS * MOE_TM, MOE_DMA_ROWS, spare)
    ys = _pair_experts(xs, p["router_wt"], tile_a, tile_b, tile_valid, wg, wu, wd, base, g_row, b_row, MOE_TM)
    return _gather_rows(ys, row_of_token, MOE_DMA_ROWS), ys


def kernel(x_prompt, x_sample, cache_swa_k, cache_swa_v, state_mlstm_C, state_mlstm_n, state_mlstm_m, state_gdn_S, state_gdn_conv, ab_w_in, ab_b_i, ab_b_f, ab_norm, ab_sinks, ab_w_out, c_w_in, c_conv_w, c_a_log, c_dt_bias, c_norm, c_w_out, ln_g, ln_b, router_w, router_b, ex_gate, ex_up, ex_down):
    gate_lo = ML_W
    sw_lo = ML_W + 2 * ML_HEADS
    w0 = ab_w_in[0]
    ab_w32 = jnp.concatenate([w0[:, :gate_lo], w0[:, sw_lo:], w0[:, gate_lo:sw_lo],
                              jnp.zeros((D_MODEL, LANES - 2 * ML_HEADS), f32)], axis=1)
    w1 = c_w_in[0]
    c_w = jnp.concatenate([w1, jnp.zeros((D_MODEL, LANES - 2 * GD_V_HEADS), f32)], axis=1).astype(bf16)
    wo32 = ab_w_out[0]
    wo = wo32.astype(bf16)
    p = {
        "ab_w": ab_w32.astype(bf16),
        "ab_w32": ab_w32,
        "ab_wo_h32": wo32[:ML_HEADS * ML_DV],
        "ab_wo_a32": wo32[ML_HEADS * ML_DV:],
        "ab_bias": _pad_lanes(jnp.concatenate([ab_b_i[0], ab_b_f[0]])),
        "ab_norm": ab_norm[0].reshape(1, ML_HEADS * ML_DV),
        "ab_sinks": _pad_lanes(ab_sinks[0]),
        "ab_wo_h": wo[:ML_HEADS * ML_DV],
        "ab_wo_a": wo[ML_HEADS * ML_DV:],
        "c_w": c_w,
        "c_conv_w": c_conv_w[0],
        "c_alog": _pad_lanes(c_a_log[0], GD_V_HEADS),
        "c_dt": _pad_lanes(c_dt_bias[0], GD_V_HEADS),
        "c_norm": c_norm[0].reshape(1, GD_HD),
        "c_wo": c_w_out[0].astype(bf16),
        "ln_g": [[ln_g[i, j].reshape(1, D_MODEL) for j in range(2)] for i in range(DEPTH)],
        "ln_b": [[ln_b[i, j].reshape(1, D_MODEL) for j in range(2)] for i in range(DEPTH)],
        "router_wt": router_w.T,
        "router_b": router_b.reshape(N_EXPERTS, 1),
        "ex_gate": ex_gate.reshape(DEPTH * N_EXPERTS, D_MODEL, D_EXPERT),
        "ex_up": ex_up.reshape(DEPTH * N_EXPERTS, D_MODEL, D_EXPERT),
        "ex_down": ex_down.reshape(DEPTH * N_EXPERTS, D_EXPERT, D_MODEL),
    }
    t_p = x_prompt.shape[1]
    y_p, st_p = _trunk(x_prompt, jnp.arange(t_p, dtype=jnp.int32), CHUNK, None, p)
    t_s = x_sample.shape[1]
    state = {"sw_k": cache_swa_k[0], "sw_v": cache_swa_v[0], "ml_C": state_mlstm_C[0], "ml_n": state_mlstm_n[0],
             "ml_m": state_mlstm_m[0], "gd_S": state_gdn_S[0], "gd_conv": state_gdn_conv[0]}
    y_s, st_s = _trunk(x_sample, PAST_LEN + jnp.arange(t_s, dtype=jnp.int32), t_s, state, p)
    return (y_p, y_s) + st_p + st_s
```

```python
import functools
import math

import jax
import jax.numpy as jnp
import numpy as np
from jax import lax
from jax.experimental import pallas as pl
from jax.experimental.pallas import tpu as pltpu

f32 = jnp.float32
bf16 = jnp.bfloat16
HIGHEST = lax.Precision.HIGHEST

D_MODEL = 1024
DEPTH = 2
CHUNK = 64
PAST_LEN = 2048
ML_HEADS = 8
ML_DK = 64
ML_DV = 64
SW_HEADS = 8
SW_KV_HEADS = 2
SW_HD = 64
SW_GROUP = SW_HEADS // SW_KV_HEADS
WINDOW = 128
ROPE_THETA = 10000.0
GD_QK_HEADS = 8
GD_V_HEADS = 16
GD_HD = 128
GD_CONV = 4
GD_QK_W = GD_QK_HEADS * GD_HD
GD_V_W = GD_V_HEADS * GD_HD
GD_CONV_CH = 2 * GD_QK_W + GD_V_W
N_EXPERTS = 16
N_GROUPS = 4
EXP_PER_GROUP = 4
D_EXPERT = 512
DN_ALPHA = (2 * DEPTH) ** 0.25
LN_EPS = 1e-5
RMS_EPS = 1e-6

LANES = 128
ML_W = 4 * ML_HEADS * ML_DK
SW_W = SW_HEADS * SW_HD + 2 * SW_KV_HEADS * SW_HD
VMEM_LIMIT = 56 * 1024 * 1024

SDS = jax.ShapeDtypeStruct


def _cparams(*sem):
    return pltpu.CompilerParams(dimension_semantics=sem, vmem_limit_bytes=VMEM_LIMIT)


def _dot(a, b):
    return jnp.dot(a, b, preferred_element_type=f32, precision=HIGHEST)


def _dot_nt(a, b):
    return lax.dot_general(a, b, (((1,), (1,)), ((), ())), preferred_element_type=f32, precision=HIGHEST)


def _bdot(a, b):
    return jnp.dot(a.astype(bf16), b.astype(bf16), preferred_element_type=f32)


def _bdot_nt(a, b):
    return lax.dot_general(a.astype(bf16), b.astype(bf16), (((1,), (1,)), ((), ())), preferred_element_type=f32)


def _bdot_tn(a, b):
    return jnp.dot(a.T.astype(bf16), b.astype(bf16), preferred_element_type=f32)


def _dots(precise):
    if precise:
        return _dot, _dot_nt, lambda a, b: _dot(a.T, b)
    return _bdot, _bdot_nt, _bdot_tn


def _sigmoid(x):
    return 0.5 + 0.5 * jnp.tanh(0.5 * x)


def _silu(x):
    hx = 0.5 * x
    return hx + hx * jnp.tanh(hx)


def _softplus(x):
    return jnp.maximum(x, 0.0) + jnp.log(1.0 + jnp.exp(-jnp.abs(x)))


def _layer_norm(v, g, b):
    mu = jnp.mean(v, axis=-1, keepdims=True)
    d = v - mu
    var = jnp.mean(d * d, axis=-1, keepdims=True)
    return d * lax.rsqrt(var + LN_EPS) * g + b


def _proj_kernel(x_ref, w_ref, *o_refs, splits, col_chunk, precise):
    mm = _dots(precise)[0]
    xb = x_ref[...] if precise else x_ref[...].astype(bf16)
    for o_ref, (start, width) in zip(o_refs, splits):
        for c in range(0, width, col_chunk):
            cw = min(col_chunk, width - c)
            o_ref[:, c:c + cw] = mm(xb, w_ref[:, start + c:start + c + cw])


def _proj(x2, w, splits, tm):
    n, k = x2.shape
    return pl.pallas_call(
        functools.partial(_proj_kernel, splits=splits, col_chunk=512, precise=w.dtype == f32),
        grid=(n // tm,),
        in_specs=[pl.BlockSpec((tm, k), lambda i: (i, 0)),
                  pl.BlockSpec(w.shape, lambda i: (0, 0), pipeline_mode=pl.Buffered(1))],
        out_specs=[pl.BlockSpec((tm, wd), lambda i: (i, 0)) for _, wd in splits],
        out_shape=[SDS((n, wd), f32) for _, wd in splits],
        compiler_params=_cparams("parallel"),
        name="in_proj",
    )(x2, w)


def _mlstm_kernel(z_ref, g_ref, bias_ref, nw_ref, cn0_ref, m0_ref, h_ref, cn_ref, m_ref, *, L, BB, precise):
    @pl.when(pl.program_id(1) == 0)
    def _():
        cn_ref[...] = cn0_ref[...]
        m_ref[...] = m0_ref[...]

    mm, mm_nt, mm_tn = _dots(precise)
    row = lax.broadcasted_iota(jnp.int32, (L, L), 0)
    col = lax.broadcasted_iota(jnp.int32, (L, L), 1)
    causal = row >= col
    tri = causal.astype(f32)
    lane = lax.broadcasted_iota(jnp.int32, (1, LANES), 1)
    lane_l = lax.broadcasted_iota(jnp.int32, (L, ML_DV), 1)
    one_hot0 = (lane_l == 0).astype(f32)
    for bi in range(BB):
        g = g_ref[bi] + bias_ref[...]
        lf = jnp.minimum(g, 0.0) - jnp.log(1.0 + jnp.exp(-jnp.abs(g)))
        bcum = _dot(tri, lf)
        b_t = bcum.T
        g_t = g.T
        m_row = m_ref[bi]
        new_m = m_row
        outs = []
        for h in range(ML_HEADS):
            b_col = bcum[:, ML_HEADS + h:ML_HEADS + h + 1]
            b_row = b_t[ML_HEADS + h:ML_HEADS + h + 1, :]
            ig_row = g_t[h:h + 1, :]
            ig_col = g[:, h:h + 1]
            m_h = m_row[:, h:h + 1]
            dmat = jnp.where(causal, b_col - b_row + ig_row, -jnp.inf)
            inter = b_col + m_h
            mt = jnp.maximum(inter, jnp.max(dmat, axis=-1, keepdims=True))
            a = jnp.exp(inter - mt)
            q = z_ref[bi, :, h * ML_DK:(h + 1) * ML_DK]
            k = z_ref[bi, :, ML_HEADS * ML_DK + h * ML_DK:ML_HEADS * ML_DK + (h + 1) * ML_DK] * (ML_DK ** -0.5)
            v = z_ref[bi, :, 2 * ML_HEADS * ML_DK + h * ML_DV:2 * ML_HEADS * ML_DK + (h + 1) * ML_DV]
            og = z_ref[bi, :, 3 * ML_HEADS * ML_DK + h * ML_DV:3 * ML_HEADS * ML_DK + (h + 1) * ML_DV]
            s = mm_nt(q, k) * jnp.exp(dmat - mt)
            vext = jnp.concatenate([v, one_hot0], axis=-1)
            cn = cn_ref[bi, h]
            tot = a * mm(q, cn) + mm(s, vext)
            num = tot[:, :ML_DV]
            den = tot[:, ML_DV:ML_DV + 1]
            hh = num / jnp.maximum(jnp.abs(den), jnp.exp(-mt))
            hh = hh * lax.rsqrt(jnp.mean(hh * hh, axis=-1, keepdims=True) + RMS_EPS) * nw_ref[:, h * ML_DV:(h + 1) * ML_DV]
            outs.append(hh * _sigmoid(og))
            m_new = mt[L - 1:L, :]
            b_last = b_col[L - 1:L, :]
            wk = jnp.exp(b_last - b_col + ig_col - m_new)
            dec = jnp.exp(b_last + m_h - m_new)
            cn_ref[bi, h] = dec * cn + mm_tn(k, wk * vext)
            new_m = jnp.where(lane == h, m_new, new_m)
        m_ref[bi] = new_m
        h_ref[bi] = jnp.concatenate(outs, axis=-1)


def _mlstm(z_ml, z_g, bias_row, norm_row, cn0, m0, B, T, L, precise):
    nc = T // L
    bb = min(B, 4)
    tok = lambda b, c: (b, c, 0)
    st4 = lambda b, c: (b, 0, 0, 0)
    st3 = lambda b, c: (b, 0, 0)
    return pl.pallas_call(
        functools.partial(_mlstm_kernel, L=L, BB=bb, precise=precise),
        grid=(B // bb, nc),
        in_specs=[pl.BlockSpec((bb, L, ML_W), tok),
                  pl.BlockSpec((bb, L, LANES), tok),
                  pl.BlockSpec((1, LANES), lambda b, c: (0, 0)),
                  pl.BlockSpec((1, ML_HEADS * ML_DV), lambda b, c: (0, 0)),
                  pl.BlockSpec((bb, ML_HEADS, ML_DK, LANES), st4),
                  pl.BlockSpec((bb, 1, LANES), st3)],
        out_specs=[pl.BlockSpec((bb, L, ML_HEADS * ML_DV), tok),
                   pl.BlockSpec((bb, ML_HEADS, ML_DK, LANES), st4),
                   pl.BlockSpec((bb, 1, LANES), st3)],
        out_shape=[SDS((B, T, ML_HEADS * ML_DV), f32),
                   SDS((B, ML_HEADS, ML_DK, LANES), f32),
                   SDS((B, 1, LANES), f32)],
        compiler_params=_cparams("parallel", "arbitrary"),
        name="mlstm",
    )(z_ml.reshape(B, T, ML_W), z_g.reshape(B, T, LANES), bias_row, norm_row, cn0, m0)


ML_PAIRS = ML_HEADS // 2
ML_REP_QUANTS = 3


def _pair_select_matrix():
    sel = np.zeros((LANES, ML_REP_QUANTS * ML_PAIRS * LANES), np.float32)
    for qn in range(ML_REP_QUANTS):
        for pr in range(ML_PAIRS):
            for half in range(2):
                lo = (qn * ML_PAIRS + pr) * LANES + half * ML_DV
                sel[ML_HEADS * qn + 2 * pr + half, lo:lo + ML_DV] = 1.0
    return jnp.asarray(sel, bf16)


def _exact_select(x, sel):
    hi = x.astype(bf16)
    r1 = x - hi.astype(f32)
    mid = r1.astype(bf16)
    lo = (r1 - mid.astype(f32)).astype(bf16)
    mm = lambda t: jnp.dot(t, sel, preferred_element_type=f32)
    return (mm(hi) + mm(mid)) + mm(lo)


def _mlstm_pair_kernel(z_ref, g_ref, bias_ref, nw_ref, sel_ref, cbd0_ref, nbd0_ref, m0_ref,
                       h_ref, cbd_ref, nbd_ref, m_ref, *, L, BB):
    @pl.when(pl.program_id(1) == 0)
    def _():
        cbd_ref[...] = cbd0_ref[...]
        nbd_ref[...] = nbd0_ref[...]
        m_ref[...] = m0_ref[...]

    tri = (lax.broadcasted_iota(jnp.int32, (L, L), 0) >= lax.broadcasted_iota(jnp.int32, (L, L), 1)).astype(f32)
    row_t = lax.broadcasted_iota(jnp.int32, (L, LANES), 0)
    lane_t = lax.broadcasted_iota(jnp.int32, (L, LANES), 1)
    first_half = lane_t < ML_DV
    causal2 = row_t >= (lane_t % ML_DV)
    rr = lax.broadcasted_iota(jnp.int32, (LANES, LANES), 0)
    cc = lax.broadcasted_iota(jnp.int32, (LANES, LANES), 1)
    same_block = (rr < ML_DV) == (cc < ML_DV)
    ones_bd = same_block.astype(bf16)
    lane_1 = lax.broadcasted_iota(jnp.int32, (1, LANES), 1)
    sel = sel_ref[...]
    n_tiles = ML_REP_QUANTS * ML_PAIRS

    gs = [g_ref[bi] + bias_ref[...] for bi in range(BB)]
    lfs = [jnp.minimum(g, 0.0) - jnp.log(1.0 + jnp.exp(-jnp.abs(g))) for g in gs]
    bc_all = _dot(tri, jnp.concatenate(lfs, axis=1))
    g_ts = [g.T for g in gs]
    b_ts = [bc_all[:, bi * LANES:(bi + 1) * LANES].T for bi in range(BB)]
    r_rows = [g_ts[bi][0:ML_HEADS] - b_ts[bi][ML_HEADS:2 * ML_HEADS] for bi in range(BB)]
    cm = jnp.concatenate([jnp.concatenate(r_rows, axis=0), jnp.full((BB * ML_HEADS, LANES - L), -jnp.inf, f32)], axis=1)
    shift = 1
    while shift < L:
        cm = jnp.maximum(cm, pltpu.roll(cm, shift, 1))
        shift *= 2
    cols = [jnp.concatenate([g_ts[bi][0:ML_HEADS], b_ts[bi][ML_HEADS:2 * ML_HEADS],
                             cm[bi * ML_HEADS:(bi + 1) * ML_HEADS, :L],
                             jnp.zeros((LANES - 3 * ML_HEADS, L), f32)], axis=0).T for bi in range(BB)]
    rep_all = _exact_select(jnp.concatenate(cols, axis=0), sel)
    rep = [rep_all[bi * L:(bi + 1) * L] for bi in range(BB)]
    m_all = jnp.concatenate([m_ref[bi] for bi in range(BB)] + [jnp.zeros((8 - BB, LANES), f32)], axis=0)
    m_rep_all = _exact_select(m_all, sel[:, :ML_PAIRS * LANES])
    m_rep = [m_rep_all[bi:bi + 1] for bi in range(BB)]

    units = [(bi, pr) for bi in range(BB) for pr in range(ML_PAIRS)]
    st = {}
    for u in units:
        bi, pr = u
        tile = lambda qn: rep[bi][:, (qn * ML_PAIRS + pr) * LANES:(qn * ML_PAIRS + pr + 1) * LANES]
        ig_rep, b_rep, cm_rep = tile(0), tile(1), tile(2)
        m_pair = m_rep[bi][:, pr * LANES:(pr + 1) * LANES]
        inter = b_rep + m_pair
        mt = jnp.maximum(inter, b_rep + cm_rep)
        r_row = jnp.concatenate([r_rows[bi][2 * pr:2 * pr + 1], r_rows[bi][2 * pr + 1:2 * pr + 2]], axis=1)
        e = jnp.exp(jnp.where(causal2, (b_rep - mt) + r_row, -jnp.inf))
        m_new = mt[L - 1:L]
        b_last = b_rep[L - 1:L]
        lo = pr * LANES
        q = z_ref[bi, :, lo:lo + LANES].astype(bf16)
        k = z_ref[bi, :, ML_HEADS * ML_DK + lo:ML_HEADS * ML_DK + lo + LANES] * (ML_DK ** -0.5)
        v = z_ref[bi, :, 2 * ML_HEADS * ML_DK + lo:2 * ML_HEADS * ML_DK + lo + LANES]
        kbd = jnp.concatenate([jnp.where(first_half, k, 0.0), jnp.where(first_half, 0.0, k)], axis=0).astype(bf16)
        vbd = jnp.concatenate([jnp.where(first_half, v, 0.0), jnp.where(first_half, 0.0, v)], axis=0).astype(bf16)
        wk = jnp.exp(b_last - b_rep + ig_rep - m_new)
        st[u] = dict(a=jnp.exp(inter - mt), em=jnp.exp(-mt), e=e, m_new=m_new, dec=jnp.exp(b_last + m_pair - m_new),
                     q=q, kbd=kbd, vbd=vbd, k_t=k.T.astype(bf16), wkv=(wk * v).astype(bf16), wk=wk.astype(bf16),
                     cbd=cbd_ref[bi, pr], nbd=nbd_ref[bi, pr])
    for u in units:
        d = st[u]
        d["qk"] = lax.dot_general(d["q"], d["kbd"], (((1,), (1,)), ((), ())), preferred_element_type=f32)
        d["qc"] = jnp.dot(d["q"], d["cbd"].astype(bf16), preferred_element_type=f32)
        d["qn"] = jnp.dot(d["q"], d["nbd"].astype(bf16), preferred_element_type=f32)
    for u in units:
        d = st[u]
        s = (d["qk"] * d["e"]).astype(bf16)
        num = d["a"] * d["qc"] + jnp.dot(s, d["vbd"], preferred_element_type=f32)
        den = d["a"] * d["qn"] + jnp.dot(s, ones_bd, preferred_element_type=f32)
        d["hh"] = num / jnp.maximum(jnp.abs(den), d["em"])
    for u in units:
        bi, pr = u
        d = st[u]
        sq = d["hh"] * d["hh"]
        sq_hi = sq.astype(bf16)
        sq_lo = (sq - sq_hi.astype(f32)).astype(bf16)
        ms = (jnp.dot(sq_hi, ones_bd, preferred_element_type=f32)
              + jnp.dot(sq_lo, ones_bd, preferred_element_type=f32)) * (1.0 / ML_DV)
        lo = pr * LANES
        og = z_ref[bi, :, 3 * ML_HEADS * ML_DK + lo:3 * ML_HEADS * ML_DK + lo + LANES]
        h_ref[bi, :, lo:lo + LANES] = d["hh"] * lax.rsqrt(ms + RMS_EPS) * nw_ref[:, lo:lo + LANES] * _sigmoid(og)
    for u in units:
        bi, pr = u
        d = st[u]
        cbd_ref[bi, pr] = d["dec"] * d["cbd"] + jnp.where(
            same_block, jnp.dot(d["k_t"], d["wkv"], preferred_element_type=f32), 0.0)
        nbd_ref[bi, pr] = d["dec"] * d["nbd"] + jnp.where(
            same_block, jnp.dot(d["k_t"], d["wk"], preferred_element_type=f32), 0.0)
    for bi in range(BB):
        new_m = m_ref[bi]
        for pr in range(ML_PAIRS):
            m_new = st[(bi, pr)]["m_new"]
            new_m = jnp.where(lane_1 == 2 * pr, m_new[:, 0:1], new_m)
            new_m = jnp.where(lane_1 == 2 * pr + 1, m_new[:, ML_DV:ML_DV + 1], new_m)
        m_ref[bi] = new_m


def _mlstm_pairs(z_ml, z_g, bias_row, norm_row, B, T, L):
    assert 2 * L == LANES and ML_DK == ML_DV == L
    nc = T // L
    bb = min(B, 4)
    tok = lambda b, c: (b, c, 0)
    st4 = lambda b, c: (b, 0, 0, 0)
    st3 = lambda b, c: (b, 0, 0)
    const = lambda b, c: (0, 0)
    sel = _pair_select_matrix()
    zeros_bd = jnp.zeros((B, ML_PAIRS, LANES, LANES), f32)
    h, cbd, nbd, m = pl.pallas_call(
        functools.partial(_mlstm_pair_kernel, L=L, BB=bb),
        grid=(B // bb, nc),
        in_specs=[pl.BlockSpec((bb, L, ML_W), tok),
                  pl.BlockSpec((bb, L, LANES), tok),
                  pl.BlockSpec((1, LANES), const),
                  pl.BlockSpec((1, ML_HEADS * ML_DV), const),
                  pl.BlockSpec(sel.shape, const),
                  pl.BlockSpec((bb, ML_PAIRS, LANES, LANES), st4),
                  pl.BlockSpec((bb, ML_PAIRS, LANES, LANES), st4),
                  pl.BlockSpec((bb, 1, LANES), st3)],
        out_specs=[pl.BlockSpec((bb, L, ML_HEADS * ML_DV), tok),
                   pl.BlockSpec((bb, ML_PAIRS, LANES, LANES), st4),
                   pl.BlockSpec((bb, ML_PAIRS, LANES, LANES), st4),
                   pl.BlockSpec((bb, 1, LANES), st3)],
        out_shape=[SDS((B, T, ML_HEADS * ML_DV), f32),
                   SDS((B, ML_PAIRS, LANES, LANES), f32),
                   SDS((B, ML_PAIRS, LANES, LANES), f32),
                   SDS((B, 1, LANES), f32)],
        compiler_params=_cparams("parallel", "arbitrary"),
        name="mlstm_pairs",
    )(z_ml.reshape(B, T, ML_W), z_g.reshape(B, T, LANES), bias_row, norm_row, sel, zeros_bd, zeros_bd,
      jnp.zeros((B, 1, LANES), f32))
    c_out = jnp.stack([cbd[:, :, :ML_DK, :ML_DV], cbd[:, :, ML_DK:, ML_DV:]], axis=2).reshape(B, ML_HEADS, ML_DK, ML_DV)
    n_out = jnp.stack([nbd[:, :, :ML_DK, 0], nbd[:, :, ML_DK:, ML_DV]], axis=2).reshape(B, ML_HEADS, ML_DK)
    return h, c_out, n_out, m[:, 0, :ML_HEADS]


def _rope(x, cos, sin_signed):
    w = x.shape[-1]
    lane = lax.broadcasted_iota(jnp.int32, x.shape, 1)
    swapped = jnp.where((lane % SW_HD) < SW_HD // 2, pltpu.roll(x, w - SW_HD // 2, 1), pltpu.roll(x, SW_HD // 2, 1))
    return x * cos + swapped * sin_signed


def _swa_attend(jobs, sinks_ref, L, precise=False):
    mm, mm_nt, _ = _dots(precise)
    units = [(j, g) for j in range(len(jobs)) for g in range(SW_KV_HEADS)]
    sinks = [jnp.concatenate(
        [jnp.broadcast_to(sinks_ref[:, g * SW_GROUP + i:g * SW_GROUP + i + 1], (L, 1)) for i in range(SW_GROUP)],
        axis=0) for g in range(SW_KV_HEADS)]
    s, p, sink_e = {}, {}, {}
    for j, g in units:
        qr, keys, _, _ = jobs[j]
        q4 = jnp.concatenate([qr[:, (g * SW_GROUP + i) * SW_HD:(g * SW_GROUP + i + 1) * SW_HD]
                              for i in range(SW_GROUP)], axis=0)
        s[j, g] = mm_nt(q4, keys[:, g * SW_HD:(g + 1) * SW_HD]) * (SW_HD ** -0.5)
    for j, g in units:
        first_valid = jobs[j][3]
        sc = s[j, g]
        if first_valid is not None:
            kcol = lax.broadcasted_iota(jnp.int32, (1, sc.shape[1]), 1)
            sc = jnp.where(kcol >= first_valid, sc, -jnp.inf)
        mx = jnp.maximum(jnp.max(sc, axis=-1, keepdims=True), sinks[g])
        p[j, g] = jnp.exp(sc - mx)
        sink_e[j, g] = jnp.exp(sinks[g] - mx)
    ones = jnp.ones((jobs[0][1].shape[0], SW_HD), f32)
    o = {u: mm(p[u], jobs[u[0]][2][:, u[1] * SW_HD:(u[1] + 1) * SW_HD]) / (mm(p[u], ones) + sink_e[u]) for u in units}
    return [jnp.concatenate([o[j, g][i * L:(i + 1) * L, :] for g in range(SW_KV_HEADS) for i in range(SW_GROUP)],
                            axis=-1) for j in range(len(jobs))]


def _swa_prompt_kernel(q_ref, kp_ref, kc_ref, vp_ref, vc_ref, cp_ref, cc_ref, sp_ref, sc_ref, sinks_ref,
                       o_ref, kr_ref, *, L, CB):
    i = pl.program_id(1)
    rows = CB * L
    back = 2 * L
    cos_q = jnp.concatenate([cc_ref[...]] * (SW_HEADS // SW_KV_HEADS), axis=-1)
    sin_q = jnp.concatenate([sc_ref[...]] * (SW_HEADS // SW_KV_HEADS), axis=-1)
    qr = _rope(q_ref[...], cos_q, sin_q)
    k_cur = _rope(kc_ref[...], cc_ref[...], sc_ref[...])
    kr_ref[...] = k_cur
    k_prev = _rope(kp_ref[rows - back:rows, :], cp_ref[rows - back:rows, :], sp_ref[rows - back:rows, :])
    keys = jnp.concatenate([k_prev, k_cur], axis=0)
    vals = jnp.concatenate([vp_ref[rows - back:rows, :], vc_ref[...]], axis=0)
    jobs = []
    for u in range(CB):
        first_valid = jnp.where(i == 0, back - u * L, 0) if u * L < back else None
        jobs.append((qr[u * L:(u + 1) * L], keys[u * L:(u + 3) * L], vals[u * L:(u + 3) * L], first_valid))
    o_ref[...] = jnp.concatenate(_swa_attend(jobs, sinks_ref, L), axis=0)


def _swa_prompt(z_sw, cos_t, sin_t, sinks_row, B, T, L):
    cb = 4
    rows = cb * L
    nb = T // rows
    n = B * T
    kcol = SW_HEADS * SW_HD // LANES
    vcol = kcol + 1
    cur = lambda b, i: (b * nb + i, 0)
    prev = lambda col: (lambda b, i: (b * nb + jnp.maximum(i - 1, 0), col))
    curc = lambda col: (lambda b, i: (b * nb + i, col))
    tab_cur = lambda b, i: (i, 0)
    tab_prev = lambda b, i: (jnp.maximum(i - 1, 0), 0)
    return pl.pallas_call(
        functools.partial(_swa_prompt_kernel, L=L, CB=cb),
        grid=(B, nb),
        in_specs=[pl.BlockSpec((rows, SW_HEADS * SW_HD), cur),
                  pl.BlockSpec((rows, LANES), prev(kcol)), pl.BlockSpec((rows, LANES), curc(kcol)),
                  pl.BlockSpec((rows, LANES), prev(vcol)), pl.BlockSpec((rows, LANES), curc(vcol)),
                  pl.BlockSpec((rows, LANES), tab_prev), pl.BlockSpec((rows, LANES), tab_cur),
                  pl.BlockSpec((rows, LANES), tab_prev), pl.BlockSpec((rows, LANES), tab_cur),
                  pl.BlockSpec((1, LANES), lambda b, i: (0, 0))],
        out_specs=[pl.BlockSpec((rows, SW_HEADS * SW_HD), cur),
                   pl.BlockSpec((rows, LANES), cur)],
        out_shape=[SDS((n, SW_HEADS * SW_HD), f32), SDS((n, LANES), f32)],
        compiler_params=_cparams("parallel", "parallel"),
        name="swa_prompt",
    )(z_sw, z_sw, z_sw, z_sw, z_sw, cos_t, cos_t, sin_t, sin_t, sinks_row)


def _swa_sample_kernel(q_ref, k_ref, v_ref, ck_ref, cv_ref, cos_ref, sin_ref, sinks_ref, o_ref, kr_ref, *, L):
    cos_q = jnp.concatenate([cos_ref[...]] * (SW_HEADS // SW_KV_HEADS), axis=-1)
    sin_q = jnp.concatenate([sin_ref[...]] * (SW_HEADS // SW_KV_HEADS), axis=-1)
    qr = _rope(q_ref[...], cos_q, sin_q)
    kr = _rope(k_ref[...], cos_ref[...], sin_ref[...])
    kr_ref[...] = kr
    keys = jnp.concatenate([ck_ref[0], kr], axis=0)
    vals = jnp.concatenate([cv_ref[0], v_ref[...]], axis=0)
    o_ref[...] = _swa_attend([(qr, keys, vals, None)], sinks_ref, L, precise=True)[0]


def _swa_sample(z_sw, cache_k, cache_v, cos_t, sin_t, sinks_row, B, T):
    n = B * T
    kcol = SW_HEADS * SW_HD // LANES
    return pl.pallas_call(
        functools.partial(_swa_sample_kernel, L=T),
        grid=(B,),
        in_specs=[pl.BlockSpec((T, SW_HEADS * SW_HD), lambda b: (b, 0)),
                  pl.BlockSpec((T, LANES), lambda b: (b, kcol)),
                  pl.BlockSpec((T, LANES), lambda b: (b, kcol + 1)),
                  pl.BlockSpec((1, WINDOW, LANES), lambda b: (b, 0, 0)),
                  pl.BlockSpec((1, WINDOW, LANES), lambda b: (b, 0, 0)),
                  pl.BlockSpec((T, LANES), lambda b: (0, 0)),
                  pl.BlockSpec((T, LANES), lambda b: (0, 0)),
                  pl.BlockSpec((1, LANES), lambda b: (0, 0))],
        out_specs=[pl.BlockSpec((T, SW_HEADS * SW_HD), lambda b: (b, 0)),
                   pl.BlockSpec((T, LANES), lambda b: (b, 0))],
        out_shape=[SDS((n, SW_HEADS * SW_HD), f32), SDS((n, LANES), f32)],
        compiler_params=_cparams("parallel"),
        name="swa_sample",
    )(z_sw, z_sw, z_sw, cache_k, cache_v, cos_t, sin_t, sinks_row)


def _out_ln_kernel(*refs, n_in):
    x_ref = refs[0]
    a_refs = refs[1:1 + n_in]
    w_refs = refs[1 + n_in:1 + 2 * n_in]
    g_ref, b_ref, rw_ref, rb_ref, o_ref, gates_ref, route_ref = refs[1 + 2 * n_in:]
    mm = _dots(w_refs[0].dtype == f32)[0]
    y = mm(a_refs[0][...], w_refs[0][...])
    for a_ref, w_ref in zip(a_refs[1:], w_refs[1:]):
        y = y + mm(a_ref[...], w_ref[...])
    x_new = _layer_norm(DN_ALPHA * x_ref[...] + y, g_ref[...], b_ref[...])
    o_ref[...] = x_new
    gates_ref[...], route_ref[...] = _route(x_new, rw_ref, rb_ref)


def _out_ln(x2, acts, ws, g_row, b_row, rw_t, rb_col, tm):
    n = x2.shape[0]
    row = lambda i: (i, 0)
    const = lambda i: (0, 0)
    col = lambda i: (0, i)
    return pl.pallas_call(
        functools.partial(_out_ln_kernel, n_in=len(acts)),
        grid=(n // tm,),
        in_specs=[pl.BlockSpec((tm, D_MODEL), row)]
        + [pl.BlockSpec((tm, a.shape[1]), row) for a in acts]
        + [pl.BlockSpec(w.shape, const) for w in ws]
        + [pl.BlockSpec((1, D_MODEL), const), pl.BlockSpec((1, D_MODEL), const),
           pl.BlockSpec((N_EXPERTS, D_MODEL), const), pl.BlockSpec((N_EXPERTS, 1), const)],
        out_specs=[pl.BlockSpec((tm, D_MODEL), row), pl.BlockSpec((N_EXPERTS, tm), col), pl.BlockSpec((8, tm), col)],
        out_shape=[SDS((n, D_MODEL), f32), SDS((N_EXPERTS, n), f32), SDS((8, n), f32)],
        compiler_params=_cparams("parallel"),
        name="out_proj_ln",
    )(x2, *acts, *ws, g_row, b_row, rw_t, rb_col)


def _logistic(x):
    return 1.0 / (1.0 + jnp.exp(-x))


def _route(x, rw_ref, rb_ref):
    logits = _dot_nt(rw_ref[...], x)
    aff = _logistic(logits)
    sc = aff + rb_ref[...]
    s = [sc[e:e + 1, :] for e in range(N_EXPERTS)]
    a = [aff[e:e + 1, :] for e in range(N_EXPERTS)]
    scores = []
    for gi in range(N_GROUPS):
        w, x, y, z = s[4 * gi:4 * gi + 4]
        p, q = jnp.maximum(w, x), jnp.minimum(w, x)
        r, t = jnp.maximum(y, z), jnp.minimum(y, z)
        scores.append(jnp.maximum(p, r) + jnp.maximum(jnp.minimum(p, r), jnp.maximum(q, t)))
    best = scores[0]
    gsel = jnp.zeros_like(best, dtype=jnp.int32)
    for gi in range(1, N_GROUPS):
        better = scores[gi] > best
        best = jnp.where(better, scores[gi], best)
        gsel = jnp.where(better, gi, gsel)
    sel = []
    for e in range(N_EXPERTS):
        gi, i = divmod(e, EXP_PER_GROUP)
        beaten = jnp.zeros_like(gsel)
        for j in range(EXP_PER_GROUP):
            if j == i:
                continue
            o = s[4 * gi + j]
            wins = (o >= s[e]) if j < i else (o > s[e])
            beaten = beaten + wins.astype(jnp.int32)
        sel.append((gsel == gi) & (beaten < 2))
    den = jnp.zeros_like(best)
    for e in range(N_EXPERTS):
        den = den + jnp.where(sel[e], a[e], 0.0)
    gate = [jnp.where(sel[e], a[e] / den, 0.0) for e in range(N_EXPERTS)]
    taken = jnp.zeros_like(gsel)
    ea = eb = wa = wb = jnp.zeros_like(best)
    for e in range(N_EXPERTS):
        first = sel[e] & (taken == 0)
        second = sel[e] & (taken == 1)
        ea = jnp.where(first, float(e), ea)
        wa = jnp.where(first, gate[e], wa)
        eb = jnp.where(second, float(e), eb)
        wb = jnp.where(second, gate[e], wb)
        taken = taken + sel[e].astype(jnp.int32)
    route = jnp.concatenate([ea, eb, wa, wb, jnp.zeros((4, ea.shape[1]), f32)], axis=0)
    return jnp.concatenate(gate, axis=0), route


def _moe_kernel(x_ref, gates_ref, wg_ref, wu_ref, wd_ref, g_ref, b_ref, o_ref, xb_ref, acc_ref):
    e = pl.program_id(1)

    @pl.when(e == 0)
    def _():
        xb_ref[...] = x_ref[...].astype(bf16)
        acc_ref[...] = jnp.zeros_like(acc_ref)

    xb = xb_ref[...]
    lane = lax.broadcasted_iota(jnp.int32, gates_ref.shape, 1)
    gcol = jnp.sum(jnp.where(lane == e, gates_ref[...], 0.0), axis=-1, keepdims=True)
    h = _silu(_bdot(xb, wg_ref[0])) * _bdot(xb, wu_ref[0])
    acc_ref[...] += _bdot(gcol * h, wd_ref[0])

    @pl.when(e == N_EXPERTS - 1)
    def _():
        o_ref[...] = _layer_norm(DN_ALPHA * x_ref[...] + acc_ref[...], g_ref[...], b_ref[...])


def _moe_ln(x2, gates, wg, wu, wd, base, g_row, b_row, tm):
    n = x2.shape[0]
    row = lambda i, e: (i, 0)
    const = lambda i, e: (0, 0)
    expert = lambda i, e: (base + e, 0, 0)
    return pl.pallas_call(
        _moe_kernel,
        grid=(n // tm, N_EXPERTS),
        in_specs=[pl.BlockSpec((tm, D_MODEL), row),
                  pl.BlockSpec((tm, N_EXPERTS), row),
                  pl.BlockSpec((1, D_MODEL, D_EXPERT), expert),
                  pl.BlockSpec((1, D_MODEL, D_EXPERT), expert),
                  pl.BlockSpec((1, D_EXPERT, D_MODEL), expert),
                  pl.BlockSpec((1, D_MODEL), const), pl.BlockSpec((1, D_MODEL), const)],
        out_specs=pl.BlockSpec((tm, D_MODEL), row),
        out_shape=SDS((n, D_MODEL), f32),
        scratch_shapes=[pltpu.VMEM((tm, D_MODEL), bf16), pltpu.VMEM((tm, D_MODEL), f32)],
        compiler_params=_cparams("parallel", "arbitrary"),
        name="moe_ln",
    )(x2, gates, wg, wu, wd, g_row, b_row)


N_PAIRS = N_GROUPS * (EXP_PER_GROUP * (EXP_PER_GROUP - 1) // 2)
MOE_TM = 256
MOE_DMA_ROWS = 512
_PAIR_A = [g * EXP_PER_GROUP + a for g in range(N_GROUPS) for a in range(EXP_PER_GROUP) for b in range(a + 1, EXP_PER_GROUP)]
_PAIR_B = [g * EXP_PER_GROUP + b for g in range(N_GROUPS) for a in range(EXP_PER_GROUP) for b in range(a + 1, EXP_PER_GROUP)]


def _gather_rows_kernel(idx_ref, src_ref, o_ref, sem, *, rows):
    base = pl.program_id(0) * rows

    def row_copy(j):
        return pltpu.make_async_copy(src_ref.at[pl.ds(idx_ref[base + j], 1)], o_ref.at[pl.ds(j, 1)], sem)

    def issue(j, carry):
        row_copy(j).start()
        return carry

    lax.fori_loop(0, rows, issue, 0, unroll=8)
    pltpu.make_async_copy(src_ref.at[pl.ds(0, rows)], o_ref, sem).wait()


def _gather_rows(src, idx, rows):
    n_out = idx.shape[0]
    d = src.shape[1]
    return pl.pallas_call(
        functools.partial(_gather_rows_kernel, rows=rows),
        grid_spec=pltpu.PrefetchScalarGridSpec(
            num_scalar_prefetch=1,
            grid=(n_out // rows,),
            in_specs=[pl.BlockSpec(memory_space=pl.ANY)],
            out_specs=pl.BlockSpec((rows, d), lambda i, idx_ref: (i, 0)),
            scratch_shapes=[pltpu.SemaphoreType.DMA(())]),
        out_shape=SDS((n_out, d), f32),
        compiler_params=_cparams("arbitrary"),
        name="gather_rows",
    )(idx, src)


def _scatter_rows_kernel(idx_ref, src_ref, init_ref, o_ref, sem, *, rows):
    del init_ref
    base = pl.program_id(0) * rows

    def row_copy(j):
        return pltpu.make_async_copy(src_ref.at[pl.ds(j, 1)], o_ref.at[pl.ds(idx_ref[base + j], 1)], sem)

    def issue(j, carry):
        row_copy(j).start()
        return carry

    lax.fori_loop(0, rows, issue, 0, unroll=8)
    pltpu.make_async_copy(src_ref, o_ref.at[pl.ds(0, rows)], sem).wait()


def _scatter_rows(src, idx, n_out, rows, init=None):
    n_src, d = src.shape
    if init is None:
        init = jnp.zeros((n_out, d), f32)
    return pl.pallas_call(
        functools.partial(_scatter_rows_kernel, rows=rows),
        grid_spec=pltpu.PrefetchScalarGridSpec(
            num_scalar_prefetch=1,
            grid=(n_src // rows,),
            in_specs=[pl.BlockSpec((rows, d), lambda i, idx_ref: (i, 0)), pl.BlockSpec(memory_space=pl.ANY)],
            out_specs=pl.BlockSpec(memory_space=pl.ANY),
            scratch_shapes=[pltpu.SemaphoreType.DMA(())]),
        out_shape=SDS((n_out, d), f32),
        input_output_aliases={2: 0},
        compiler_params=_cparams("arbitrary"),
        name="scatter_rows",
    )(idx, src, init)


def _pair_plan(route, n, tm):
    ea = route[0].astype(jnp.int32)
    eb = route[1].astype(jnp.int32)
    a = ea % EXP_PER_GROUP
    b = eb % EXP_PER_GROUP
    pidx = jnp.where(a == 0, b - 1, jnp.where(a == 1, b + 1, 5))
    pair = (ea // EXP_PER_GROUP) * (N_PAIRS // N_GROUPS) + pidx
    onehot = (pair[:, None] == jnp.arange(N_PAIRS, dtype=jnp.int32)[None, :]).astype(jnp.int32)
    csum = jnp.cumsum(onehot, axis=0)
    counts = csum[-1]
    ntiles = (counts + tm - 1) // tm
    tile_end = jnp.cumsum(ntiles)
    tile_start = tile_end - ntiles
    row_of_token = jnp.sum(onehot * (csum - 1 + (tile_start * tm)[None, :]), axis=1)
    nt = n // tm + N_PAIRS
    tile_id = jnp.arange(nt, dtype=jnp.int32)
    tile_valid = tile_id < tile_end[-1]
    tile_pair = jnp.sum((tile_end[None, :] <= jnp.minimum(tile_id, tile_end[-1] - 1)[:, None]).astype(jnp.int32), axis=1)
    tile_pair = jnp.minimum(tile_pair, N_PAIRS - 1)
    pick = (tile_pair[:, None] == jnp.arange(N_PAIRS, dtype=jnp.int32)[None, :]).astype(jnp.int32)
    tile_a = jnp.sum(pick * jnp.asarray(_PAIR_A, jnp.int32)[None, :], axis=1)
    tile_b = jnp.sum(pick * jnp.asarray(_PAIR_B, jnp.int32)[None, :], axis=1)
    return row_of_token, tile_a, tile_b, tile_valid.astype(jnp.int32)


def _pair_expert_kernel(ta_ref, tb_ref, tv_ref, x_ref, rwt_ref, wga_ref, wua_ref, wda_ref, wgb_ref, wub_ref, wdb_ref,
                        g_ref, b_ref, o_ref):
    i = pl.program_id(0)
    valid = tv_ref[i] == 1

    @pl.when(valid)
    def _():
        x = x_ref[...]
        xb = x.astype(bf16)
        aff_a = _logistic(jnp.sum(x * rwt_ref[pl.ds(ta_ref[i], 1), :], axis=-1, keepdims=True))
        aff_b = _logistic(jnp.sum(x * rwt_ref[pl.ds(tb_ref[i], 1), :], axis=-1, keepdims=True))
        den = aff_a + aff_b
        acc = None
        for w, (wg, wu, wd) in ((aff_a / den, (wga_ref, wua_ref, wda_ref)), (aff_b / den, (wgb_ref, wub_ref, wdb_ref))):
            h = _silu(_bdot(xb, wg[0])) * _bdot(xb, wu[0])
            y = _bdot(w * h, wd[0])
            acc = y if acc is None else acc + y
        o_ref[...] = _layer_norm(DN_ALPHA * x + acc, g_ref[...], b_ref[...])

    @pl.when(jnp.logical_not(valid))
    def _():
        o_ref[...] = jnp.zeros_like(o_ref)


def _pair_experts(xs, rw_t, tile_a, tile_b, tile_valid, wg, wu, wd, base, g_row, b_row, tm):
    rows = xs.shape[0]
    row = lambda i, ta, tb, tv: (i, 0)
    const = lambda i, ta, tb, tv: (0, 0)
    ex_a = lambda i, ta, tb, tv: (base + ta[i], 0, 0)
    ex_b = lambda i, ta, tb, tv: (base + tb[i], 0, 0)
    up = pl.BlockSpec((1, D_MODEL, D_EXPERT), ex_a), pl.BlockSpec((1, D_MODEL, D_EXPERT), ex_b)
    down = pl.BlockSpec((1, D_EXPERT, D_MODEL), ex_a), pl.BlockSpec((1, D_EXPERT, D_MODEL), ex_b)
    return pl.pallas_call(
        _pair_expert_kernel,
        grid_spec=pltpu.PrefetchScalarGridSpec(
            num_scalar_prefetch=3,
            grid=(rows // tm,),
            in_specs=[pl.BlockSpec((tm, D_MODEL), row), pl.BlockSpec((N_EXPERTS, D_MODEL), const),
                      up[0], up[0], down[0], up[1], up[1], down[1],
                      pl.BlockSpec((1, D_MODEL), const), pl.BlockSpec((1, D_MODEL), const)],
            out_specs=pl.BlockSpec((tm, D_MODEL), row)),
        out_shape=SDS((rows, D_MODEL), f32),
        compiler_params=_cparams("arbitrary"),
        name="pair_experts",
    )(tile_a, tile_b, tile_valid, xs, rw_t, wg, wu, wd, wg, wu, wd, g_row, b_row)


def _unit_lower_inverses(mats, L):
    row = lax.broadcasted_iota(jnp.int32, (L, L), 0)
    col = lax.broadcasted_iota(jnp.int32, (L, L), 1)
    eye = (row == col).astype(f32)
    ps = [eye - a for a in mats]
    pws = [a.astype(bf16) for a in mats]
    span = 2
    while span < L:
        pws = [jnp.dot(pw, pw, preferred_element_type=f32).astype(bf16) for pw in pws]
        ps = [p + jnp.dot(pw, p.astype(bf16), preferred_element_type=f32) for p, pw in zip(ps, pws)]
        span *= 2
    return ps


def _gdn_kernel(x_ref, zg_ref, ba_ref, cw_ref, alog_ref, dt_ref, nw_ref, s0_ref, cb_ref,
                o_ref, s_ref, prev_ref, *, L, BB):
    @pl.when(pl.program_id(1) == 0)
    def _():
        s_ref[...] = s0_ref[...]
        prev_ref[...] = cb_ref[...]

    def conv_silu(bi, lo, width):
        cur = x_ref[bi, :, lo:lo + width]
        cat =jnp.concatenate([prev_ref[bi, :, lo:lo + width], cur], axis=0)
        acc = cat[5:5 + L] * cw_ref[0:1, lo:lo + width]
        acc = acc + cat[6:6 + L] * cw_ref[1:2, lo:lo + width]
        acc = acc + cat[7:7 + L] * cw_ref[2:3, lo:lo + width]
        acc = acc + cur * cw_ref[3:4, lo:lo + width]
        return _silu(acc)

    def l2n(v, scale):
        return v * lax.rsqrt(jnp.sum(v * v, axis=-1, keepdims=True) + 1e-6) * scale

    row = lax.broadcasted_iota(jnp.int32, (L, L), 0)
    col = lax.broadcasted_iota(jnp.int32, (L, L), 1)
    incl = row >= col
    strict = row > col
    rep = GD_V_HEADS // GD_QK_HEADS
    bas = [ba_ref[bi] for bi in range(BB)]
    betas = [_sigmoid(ba) for ba in bas]
    gls = [-jnp.exp(alog_ref[...]) * _softplus(ba + dt_ref[...]) for ba in bas]
    gcum_all = _dot(incl.astype(f32), jnp.concatenate(gls, axis=1))
    gcums = [gcum_all[:, bi * LANES:(bi + 1) * LANES] for bi in range(BB)]
    gcum_ts = [g.T for g in gcums]
    units = [(bi, hv) for bi in range(BB) for hv in range(GD_V_HEADS)]
    qs, ks, amats, qkds, egs, g_cols, rhss = {}, {}, [], {}, {}, {}, []
    for bi in range(BB):
        for j in range(GD_QK_HEADS):
            q = l2n(conv_silu(bi, j * GD_HD, GD_HD), GD_HD ** -0.5)
            k = l2n(conv_silu(bi, GD_QK_W + j * GD_HD, GD_HD), 1.0)
            qb, kb = q.astype(bf16), k.astype(bf16)
            kk = lax.dot_general(kb, kb, (((1,), (1,)), ((), ())), preferred_element_type=f32)
            qk = lax.dot_general(qb, kb, (((1,), (1,)), ((), ())), preferred_element_type=f32)
            qs[bi, j] = qb
            ks[bi, j] = k
            for r in range(rep):
                hv = j * rep + r
                v = conv_silu(bi, 2 * GD_QK_W + hv * GD_HD, GD_HD)
                g_col = gcums[bi][:, GD_V_HEADS + hv:GD_V_HEADS + hv + 1]
                g_row = gcum_ts[bi][GD_V_HEADS + hv:GD_V_HEADS + hv + 1, :]
                b_col = betas[bi][:, hv:hv + 1]
                decay = jnp.exp(jnp.where(incl, g_col - g_row, -jnp.inf))
                eg = jnp.exp(g_col)
                amats.append(jnp.where(strict, b_col * kk * decay, 0.0))
                qkds[bi, hv] = (qk * decay).astype(bf16)
                egs[bi, hv] = eg
                g_cols[bi, hv] = g_col
                rhss.append(((b_col * v).astype(bf16), ((b_col * eg) * k).astype(bf16)))
    tinvs = _unit_lower_inverses(amats, L)
    tinvs = [t.astype(bf16) for t in tinvs]
    sol_v = dict(zip(units, [jnp.dot(t, r[0], preferred_element_type=f32) for t, r in zip(tinvs, rhss)]))
    sol_k = dict(zip(units, [jnp.dot(t, r[1], preferred_element_type=f32) for t, r in zip(tinvs, rhss)]))
    sts, wks, qss, kts = {}, {}, {}, {}
    for u in units:
        bi, hv = u
        j = hv // rep
        sts[u] = s_ref[bi, hv]
        stb = sts[u].astype(bf16)
        wks[u] = jnp.dot(sol_k[u].astype(bf16), stb, preferred_element_type=f32)
        qss[u] = jnp.dot(qs[bi, j], stb, preferred_element_type=f32)
        g_col = g_cols[u]
        kts[u] = (jnp.exp(g_col[L - 1:L, :] - g_col) * ks[bi, j]).T.astype(bf16)
    outs = {}
    for u in units:
        bi, hv = u
        wnb = (sol_v[u] - wks[u]).astype(bf16)
        outs[u] = egs[u] * qss[u] + jnp.dot(qkds[u], wnb, preferred_element_type=f32)
        s_ref[bi, hv] = jnp.exp(g_cols[u][L - 1:L, :]) * sts[u] + jnp.dot(kts[u], wnb, preferred_element_type=f32)
    for u in units:
        bi, hv = u
        o = outs[u]
        o = o * lax.rsqrt(jnp.mean(o * o, axis=-1, keepdims=True) + RMS_EPS) * nw_ref[...]
        o_ref[bi, :, hv * GD_HD:(hv + 1) * GD_HD] = o * _silu(zg_ref[bi, :, hv * GD_HD:(hv + 1) * GD_HD])
    for bi in range(BB):
        prev_ref[bi] = x_ref[bi, L - 8:L, :]


def _gdn(qkv, zg, ba, conv_w, alog_row, dt_row, norm_row, s0, conv8, B, T, L):
    nc = T // L
    n = B * T
    bb = 2
    tok = lambda b, c: (b, c, 0)
    const = lambda b, c: (0, 0)
    st4 = lambda b, c: (b, 0, 0, 0)
    o, s_out = pl.pallas_call(
        functools.partial(_gdn_kernel, L=L, BB=bb),
        grid=(B // bb, nc),
        in_specs=[pl.BlockSpec((bb, L, GD_CONV_CH), tok),
                  pl.BlockSpec((bb, L, GD_V_W), tok),
                  pl.BlockSpec((bb, L, LANES), tok),
                  pl.BlockSpec((GD_CONV, GD_CONV_CH), const),
                  pl.BlockSpec((1, LANES), const),
                  pl.BlockSpec((1, LANES), const),
                  pl.BlockSpec((1, GD_HD), const),
                  pl.BlockSpec((bb, GD_V_HEADS, GD_HD, GD_HD), st4),
                  pl.BlockSpec((bb, 8, GD_CONV_CH), lambda b, c: (b, 0, 0))],
        out_specs=[pl.BlockSpec((bb, L, GD_V_W), tok),
                   pl.BlockSpec((bb, GD_V_HEADS, GD_HD, GD_HD), st4)],
        out_shape=[SDS((B, T, GD_V_W), f32), SDS((B, GD_V_HEADS, GD_HD, GD_HD), f32)],
        scratch_shapes=[pltpu.VMEM((bb, 8, GD_CONV_CH), f32)],
        compiler_params=_cparams("parallel", "arbitrary"),
        name="gdn",
    )(qkv.reshape(B, T, GD_CONV_CH), zg.reshape(B, T, GD_V_W), ba.reshape(B, T, LANES), conv_w, alog_row, dt_row,
      norm_row, s0, conv8)
    return o.reshape(n, GD_V_W), s_out


def _pad_lanes(row, offset=0):
    return jnp.zeros((1, LANES), f32).at[0, offset:offset + row.shape[0]].set(row.astype(f32))


def _rope_tables(pos):
    half = SW_HD // 2
    inv = ROPE_THETA ** (-jnp.arange(half, dtype=f32) / half)
    ang = pos.astype(f32)[:, None] * inv[None, :]
    cos, sin = jnp.cos(ang), jnp.sin(ang)
    cos_t = jnp.concatenate([cos, cos] * SW_KV_HEADS, axis=-1)
    sin_t = jnp.concatenate([-sin, sin] * SW_KV_HEADS, axis=-1)
    return cos_t, sin_t


def _tile(n, pref):
    return pref if n % pref == 0 else n


def _trunk(x, pos, L, state, p):
    B, T, _ = x.shape
    n = B * T
    x2 = x.reshape(n, D_MODEL)
    tm = _tile(n, 512)
    precise = state is not None
    ab_w, wo_h, wo_a = (p["ab_w32"], p["ab_wo_h32"], p["ab_wo_a32"]) if precise else (p["ab_w"], p["ab_wo_h"], p["ab_wo_a"])

    z_ml, z_sw, z_g = _proj(x2, ab_w, ((0, ML_W), (ML_W, SW_W), (ML_W + SW_W, LANES)), tm)
    if state is None:
        h_ml, ml_c, ml_n, ml_m = _mlstm_pairs(z_ml, z_g, p["ab_bias"], p["ab_norm"], B, T, L)
    else:
        cn0 = jnp.concatenate([state["ml_C"], state["ml_n"][..., None],
                               jnp.zeros((B, ML_HEADS, ML_DK, LANES - ML_DV - 1), f32)], axis=-1)
        m0 = jnp.zeros((B, 1, LANES), f32).at[:, 0, :ML_HEADS].set(state["ml_m"])
        h_ml, cn, m_out = _mlstm(z_ml, z_g, p["ab_bias"], p["ab_norm"], cn0, m0, B, T, L, precise)
        ml_c, ml_n, ml_m = cn[..., :ML_DV], cn[..., ML_DV], m_out[:, 0, :ML_HEADS]
    h_ml = h_ml.reshape(n, ML_HEADS * ML_DV)
    cos_t, sin_t = _rope_tables(pos)
    if state is None:
        a_sw, k_rot = _swa_prompt(z_sw, cos_t, sin_t, p["ab_sinks"], B, T, L)
    else:
        a_sw, k_rot = _swa_sample(z_sw, state["sw_k"].reshape(B, WINDOW, LANES),
                                  state["sw_v"].reshape(B, WINDOW, LANES), cos_t, sin_t, p["ab_sinks"], B, T)
    keep = min(T, WINDOW)
    new_k = k_rot.reshape(B, T, LANES)[:, T - keep:].reshape(B, keep, SW_KV_HEADS, SW_HD)
    new_v = z_sw.reshape(B, T, SW_W)[:, T - keep:, SW_HEADS * SW_HD + LANES:].reshape(B, keep, SW_KV_HEADS, SW_HD)
    x2, gates_t, route = _out_ln(x2, [h_ml, a_sw], [wo_h, wo_a], p["ln_g"][0][0], p["ln_b"][0][0],
                                 p["router_wt"], p["router_b"], tm)
    x2, spare = _moe_block(x2, gates_t, route, p, 0)

    tm1 = _tile(n, 256)
    if state is None:
        s0 = jnp.zeros((B, GD_V_HEADS, GD_HD, GD_HD), f32)
        conv8 = jnp.zeros((B, 8, GD_CONV_CH), f32)
    else:
        s0 = state["gd_S"]
        conv8 = jnp.concatenate([jnp.zeros((B, 8 - (GD_CONV - 1), GD_CONV_CH), f32), state["gd_conv"]], axis=1)
    qkv, zg, ba = _proj(x2, p["c_w"], ((0, GD_CONV_CH), (GD_CONV_CH, GD_V_W), (GD_CONV_CH + GD_V_W, LANES)), tm1)
    new_conv = qkv.reshape(B, T, GD_CONV_CH)[:, T - (GD_CONV - 1):]
    o_gd, s_out = _gdn(qkv, zg, ba, p["c_conv_w"], p["c_alog"], p["c_dt"], p["c_norm"], s0, conv8, B, T, L)
    x2, gates_t, route = _out_ln(x2, [o_gd], [p["c_wo"]], p["ln_g"][1][0], p["ln_b"][1][0],
                                 p["router_wt"], p["router_b"], tm)
    x2, _ = _moe_block(x2, gates_t, route, p, 1, spare)

    outs = (new_k[None], new_v[None], ml_c[None], ml_n[None], ml_m[None],
            s_out[None], new_conv[None])
    return x2.reshape(B, T, D_MODEL), outs


def _moe_block(x2, gates_t, route, p, layer, spare=None):
    n = x2.shape[0]
    wg, wu, wd = p["ex_gate"], p["ex_up"], p["ex_down"]
    base = layer * N_EXPERTS
    g_row, b_row = p["ln_g"][layer][1], p["ln_b"][layer][1]
    if n < N_PAIRS * MOE_TM:
        return _moe_ln(x2, gates_t.T, wg, wu, wd, base, g_row, b_row, _tile(n, 1024)), None
    row_of_token, tile_a, tile_b, tile_valid = _pair_plan(route, n, MOE_TM)
    xs = _scatter_rows(x2, row_of_token, n + N_PAIRS * MOE_TM, MOE_DMA_ROWS, spare)
    ys = _pair_experts(xs, p["router_wt"], tile_a, tile_b, tile_valid, wg, wu, wd, base, g_row, b_row, MOE_TM)
    return _gather_rows(ys, row_of_token, MOE_DMA_ROWS), ys


def kernel(x_prompt, x_sample, cache_swa_k, cache_swa_v, state_mlstm_C, state_mlstm_n, state_mlstm_m, state_gdn_S, state_gdn_conv, ab_w_in, ab_b_i, ab_b_f, ab_norm, ab_sinks, ab_w_out, c_w_in, c_conv_w, c_a_log, c_dt_bias, c_norm, c_w_out, ln_g, ln_b, router_w, router_b, ex_gate, ex_up, ex_down):
    gate_lo = ML_W
    sw_lo = ML_W + 2 * ML_HEADS
    w0 = ab_w_in[0]
    ab_w32 = jnp.concatenate([w0[:, :gate_lo], w0[:, sw_lo:], w0[:, gate_lo:sw_lo],
                              jnp.zeros((D_MODEL, LANES - 2 * ML_HEADS), f32)], axis=1)
    w1 = c_w_in[0]
    c_w = jnp.concatenate([w1, jnp.zeros((D_MODEL, LANES - 2 * GD_V_HEADS), f32)], axis=1).astype(bf16)
    wo32 = ab_w_out[0]
    wo = wo32.astype(bf16)
    p = {
        "ab_w": ab_w32.astype(bf16),
        "ab_w32": ab_w32,
        "ab_wo_h32": wo32[:ML_HEADS * ML_DV],
        "ab_wo_a32": wo32[ML_HEADS * ML_DV:],
        "ab_bias": _pad_lanes(jnp.concatenate([ab_b_i[0], ab_b_f[0]])),
        "ab_norm": ab_norm[0].reshape(1, ML_HEADS * ML_DV),
        "ab_sinks": _pad_lanes(ab_sinks[0]),
        "ab_wo_h": wo[:ML_HEADS * ML_DV],
        "ab_wo_a": wo[ML_HEADS * ML_DV:],
        "c_w": c_w,
        "c_conv_w": c_conv_w[0],
        "c_alog": _pad_lanes(c_a_log[0], GD_V_HEADS),
        "c_dt": _pad_lanes(c_dt_bias[0], GD_V_HEADS),
        "c_norm": c_norm[0].reshape(1, GD_HD),
        "c_wo": c_w_out[0].astype(bf16),
        "ln_g": [[ln_g[i, j].reshape(1, D_MODEL) for j in range(2)] for i in range(DEPTH)],
        "ln_b": [[ln_b[i, j].reshape(1, D_MODEL) for j in range(2)] for i in range(DEPTH)],
        "router_wt": router_w.T,
        "router_b": router_b.reshape(N_EXPERTS, 1),
        "ex_gate": ex_gate.reshape(DEPTH * N_EXPERTS, D_MODEL, D_EXPERT),
        "ex_up": ex_up.reshape(DEPTH * N_EXPERTS, D_MODEL, D_EXPERT),
        "ex_down": ex_down.reshape(DEPTH * N_EXPERTS, D_EXPERT, D_MODEL),
    }
    t_p = x_prompt.shape[1]
    y_p, st_p = _trunk(x_prompt, jnp.arange(t_p, dtype=jnp.int32), CHUNK, None, p)
    t_s = x_sample.shape[1]
    state = {"sw_k": cache_swa_k[0], "sw_v": cache_swa_v[0], "ml_C": state_mlstm_C[0], "ml_n": state_mlstm_n[0],
             "ml_m": state_mlstm_m[0], "gd_S": state_gdn_S[0], "gd_conv": state_gdn_conv[0]}
    y_s, st_s = _trunk(x_sample, PAST_LEN + jnp.arange(t_s, dtype=jnp.int32), t_s, state, p)
    return (y_p, y_s) + st_p + st_s
```

```python
import functools
import math

import jax
import jax.numpy as jnp
import numpy as np
from jax import lax
from jax.experimental import pallas as pl
from jax.experimental.pallas import tpu as pltpu

f32 = jnp.float32
bf16 = jnp.bfloat16
HIGHEST = lax.Precision.HIGHEST

D_MODEL = 1024
DEPTH = 2
CHUNK = 64
PAST_LEN = 2048
ML_HEADS = 8
ML_DK = 64
ML_DV = 64
SW_HEADS = 8
SW_KV_HEADS = 2
SW_HD = 64
SW_GROUP = SW_HEADS // SW_KV_HEADS
WINDOW = 128
ROPE_THETA = 10000.0
GD_QK_HEADS = 8
GD_V_HEADS = 16
GD_HD = 128
GD_CONV = 4
GD_QK_W = GD_QK_HEADS * GD_HD
GD_V_W = GD_V_HEADS * GD_HD
GD_CONV_CH = 2 * GD_QK_W + GD_V_W
N_EXPERTS = 16
N_GROUPS = 4
EXP_PER_GROUP = 4
D_EXPERT = 512
DN_ALPHA = (2 * DEPTH) ** 0.25
LN_EPS = 1e-5
RMS_EPS = 1e-6

LANES = 128
ML_W = 4 * ML_HEADS * ML_DK
SW_W = SW_HEADS * SW_HD + 2 * SW_KV_HEADS * SW_HD
VMEM_LIMIT = 56 * 1024 * 1024

SDS = jax.ShapeDtypeStruct


def _cparams(*sem):
    return pltpu.CompilerParams(dimension_semantics=sem, vmem_limit_bytes=VMEM_LIMIT)


def _dot(a, b):
    return jnp.dot(a, b, preferred_element_type=f32, precision=HIGHEST)


def _dot_nt(a, b):
    return lax.dot_general(a, b, (((1,), (1,)), ((), ())), preferred_element_type=f32, precision=HIGHEST)


def _bdot(a, b):
    return jnp.dot(a.astype(bf16), b.astype(bf16), preferred_element_type=f32)


def _bdot_nt(a, b):
    return lax.dot_general(a.astype(bf16), b.astype(bf16), (((1,), (1,)), ((), ())), preferred_element_type=f32)


def _bdot_tn(a, b):
    return jnp.dot(a.T.astype(bf16), b.astype(bf16), preferred_element_type=f32)


def _dots(precise):
    if precise:
        return _dot, _dot_nt, lambda a, b: _dot(a.T, b)
    return _bdot, _bdot_nt, _bdot_tn


def _sigmoid(x):
    return 0.5 + 0.5 * jnp.tanh(0.5 * x)


def _silu(x):
    hx = 0.5 * x
    return hx + hx * jnp.tanh(hx)


def _softplus(x):
    return jnp.maximum(x, 0.0) + jnp.log(1.0 + jnp.exp(-jnp.abs(x)))


def _layer_norm(v, g, b):
    mu = jnp.mean(v, axis=-1, keepdims=True)
    d = v - mu
    var = jnp.mean(d * d, axis=-1, keepdims=True)
    return d * lax.rsqrt(var + LN_EPS) * g + b


def _proj_kernel(x_ref, w_ref, *o_refs, splits, col_chunk, precise):
    mm = _dots(precise)[0]
    xb = x_ref[...] if precise else x_ref[...].astype(bf16)
    for o_ref, (start, width) in zip(o_refs, splits):
        for c in range(0, width, col_chunk):
            cw = min(col_chunk, width - c)
            o_ref[:, c:c + cw] = mm(xb, w_ref[:, start + c:start + c + cw])


def _proj(x2, w, splits, tm):
    n, k = x2.shape
    return pl.pallas_call(
        functools.partial(_proj_kernel, splits=splits, col_chunk=512, precise=w.dtype == f32),
        grid=(n // tm,),
        in_specs=[pl.BlockSpec((tm, k), lambda i: (i, 0)),
                  pl.BlockSpec(w.shape, lambda i: (0, 0), pipeline_mode=pl.Buffered(1))],
        out_specs=[pl.BlockSpec((tm, wd), lambda i: (i, 0)) for _, wd in splits],
        out_shape=[SDS((n, wd), f32) for _, wd in splits],
        compiler_params=_cparams("parallel"),
        name="in_proj",
    )(x2, w)


def _mlstm_kernel(z_ref, g_ref, bias_ref, nw_ref, cn0_ref, m0_ref, h_ref, cn_ref, m_ref, *, L, BB, precise):
    @pl.when(pl.program_id(1) == 0)
    def _():
        cn_ref[...] = cn0_ref[...]
        m_ref[...] = m0_ref[...]

    mm, mm_nt, mm_tn = _dots(precise)
    row = lax.broadcasted_iota(jnp.int32, (L, L), 0)
    col = lax.broadcasted_iota(jnp.int32, (L, L), 1)
    causal = row >= col
    tri = causal.astype(f32)
    lane = lax.broadcasted_iota(jnp.int32, (1, LANES), 1)
    lane_l = lax.broadcasted_iota(jnp.int32, (L, ML_DV), 1)
    one_hot0 = (lane_l == 0).astype(f32)
    for bi in range(BB):
        g = g_ref[bi] + bias_ref[...]
        lf = jnp.minimum(g, 0.0) - jnp.log(1.0 + jnp.exp(-jnp.abs(g)))
        bcum = _dot(tri, lf)
        b_t = bcum.T
        g_t = g.T
        m_row = m_ref[bi]
        new_m = m_row
        outs = []
        for h in range(ML_HEADS):
            b_col = bcum[:, ML_HEADS + h:ML_HEADS + h + 1]
            b_row = b_t[ML_HEADS + h:ML_HEADS + h + 1, :]
            ig_row = g_t[h:h + 1, :]
            ig_col = g[:, h:h + 1]
            m_h = m_row[:, h:h + 1]
            dmat = jnp.where(causal, b_col - b_row + ig_row, -jnp.inf)
            inter = b_col + m_h
            mt = jnp.maximum(inter, jnp.max(dmat, axis=-1, keepdims=True))
            a = jnp.exp(inter - mt)
            q = z_ref[bi, :, h * ML_DK:(h + 1) * ML_DK]
            k = z_ref[bi, :, ML_HEADS * ML_DK + h * ML_DK:ML_HEADS * ML_DK + (h + 1) * ML_DK] * (ML_DK ** -0.5)
            v = z_ref[bi, :, 2 * ML_HEADS * ML_DK + h * ML_DV:2 * ML_HEADS * ML_DK + (h + 1) * ML_DV]
            og = z_ref[bi, :, 3 * ML_HEADS * ML_DK + h * ML_DV:3 * ML_HEADS * ML_DK + (h + 1) * ML_DV]
            s = mm_nt(q, k) * jnp.exp(dmat - mt)
            vext = jnp.concatenate([v, one_hot0], axis=-1)
            cn = cn_ref[bi, h]
            tot = a * mm(q, cn) + mm(s, vext)
            num = tot[:, :ML_DV]
            den = tot[:, ML_DV:ML_DV + 1]
            hh = num / jnp.maximum(jnp.abs(den), jnp.exp(-mt))
            hh = hh * lax.rsqrt(jnp.mean(hh * hh, axis=-1, keepdims=True) + RMS_EPS) * nw_ref[:, h * ML_DV:(h + 1) * ML_DV]
            outs.append(hh * _sigmoid(og))
            m_new = mt[L - 1:L, :]
            b_last = b_col[L - 1:L, :]
            wk = jnp.exp(b_last - b_col + ig_col - m_new)
            dec = jnp.exp(b_last + m_h - m_new)
            cn_ref[bi, h] = dec * cn + mm_tn(k, wk * vext)
            new_m = jnp.where(lane == h, m_new, new_m)
        m_ref[bi] = new_m
        h_ref[bi] = jnp.concatenate(outs, axis=-1)


def _mlstm(z_ml, z_g, bias_row, norm_row, cn0, m0, B, T, L, precise):
    nc = T // L
    bb = min(B, 4)
    tok = lambda b, c: (b, c, 0)
    st4 = lambda b, c: (b, 0, 0, 0)
    st3 = lambda b, c: (b, 0, 0)
    return pl.pallas_call(
        functools.partial(_mlstm_kernel, L=L, BB=bb, precise=precise),
        grid=(B // bb, nc),
        in_specs=[pl.BlockSpec((bb, L, ML_W), tok),
                  pl.BlockSpec((bb, L, LANES), tok),
                  pl.BlockSpec((1, LANES), lambda b, c: (0, 0)),
                  pl.BlockSpec((1, ML_HEADS * ML_DV), lambda b, c: (0, 0)),
                  pl.BlockSpec((bb, ML_HEADS, ML_DK, LANES), st4),
                  pl.BlockSpec((bb, 1, LANES), st3)],
        out_specs=[pl.BlockSpec((bb, L, ML_HEADS * ML_DV), tok),
                   pl.BlockSpec((bb, ML_HEADS, ML_DK, LANES), st4),
                   pl.BlockSpec((bb, 1, LANES), st3)],
        out_shape=[SDS((B, T, ML_HEADS * ML_DV), f32),
                   SDS((B, ML_HEADS, ML_DK, LANES), f32),
                   SDS((B, 1, LANES), f32)],
        compiler_params=_cparams("parallel", "arbitrary"),
        name="mlstm",
    )(z_ml.reshape(B, T, ML_W), z_g.reshape(B, T, LANES), bias_row, norm_row, cn0, m0)


ML_PAIRS = ML_HEADS // 2
ML_REP_QUANTS = 3


def _pair_select_matrix():
    sel = np.zeros((LANES, ML_REP_QUANTS * ML_PAIRS * LANES), np.float32)
    for qn in range(ML_REP_QUANTS):
        for pr in range(ML_PAIRS):
            for half in range(2):
                lo = (qn * ML_PAIRS + pr) * LANES + half * ML_DV
                sel[ML_HEADS * qn + 2 * pr + half, lo:lo + ML_DV] = 1.0
    return jnp.asarray(sel, bf16)


def _exact_select(x, sel):
    hi = x.astype(bf16)
    r1 = x - hi.astype(f32)
    mid = r1.astype(bf16)
    lo = (r1 - mid.astype(f32)).astype(bf16)
    mm = lambda t: jnp.dot(t, sel, preferred_element_type=f32)
    return (mm(hi) + mm(mid)) + mm(lo)


def _mlstm_pair_kernel(z_ref, g_ref, bias_ref, nw_ref, sel_ref, cbd0_ref, nbd0_ref, m0_ref,
                       h_ref, cbd_ref, nbd_ref, m_ref, *, L, BB):
    @pl.when(pl.program_id(1) == 0)
    def _():
        cbd_ref[...] = cbd0_ref[...]
        nbd_ref[...] = nbd0_ref[...]
        m_ref[...] = m0_ref[...]

    tri = (lax.broadcasted_iota(jnp.int32, (L, L), 0) >= lax.broadcasted_iota(jnp.int32, (L, L), 1)).astype(f32)
    row_t = lax.broadcasted_iota(jnp.int32, (L, LANES), 0)
    lane_t = lax.broadcasted_iota(jnp.int32, (L, LANES), 1)
    first_half = lane_t < ML_DV
    causal2 = row_t >= (lane_t % ML_DV)
    rr = lax.broadcasted_iota(jnp.int32, (LANES, LANES), 0)
    cc = lax.broadcasted_iota(jnp.int32, (LANES, LANES), 1)
    same_block = (rr < ML_DV) == (cc < ML_DV)
    ones_bd = same_block.astype(bf16)
    lane_1 = lax.broadcasted_iota(jnp.int32, (1, LANES), 1)
    sel = sel_ref[...]
    n_tiles = ML_REP_QUANTS * ML_PAIRS

    gs = [g_ref[bi] + bias_ref[...] for bi in range(BB)]
    lfs = [jnp.minimum(g, 0.0) - jnp.log(1.0 + jnp.exp(-jnp.abs(g))) for g in gs]
    bc_all = _dot(tri, jnp.concatenate(lfs, axis=1))
    g_ts = [g.T for g in gs]
    b_ts = [bc_all[:, bi * LANES:(bi + 1) * LANES].T for bi in range(BB)]
    r_rows = [g_ts[bi][0:ML_HEADS] - b_ts[bi][ML_HEADS:2 * ML_HEADS] for bi in range(BB)]
    cm = jnp.concatenate([jnp.concatenate(r_rows, axis=0), jnp.full((BB * ML_HEADS, LANES - L), -jnp.inf, f32)], axis=1)
    shift = 1
    while shift < L:
        cm = jnp.maximum(cm, pltpu.roll(cm, shift, 1))
        shift *= 2
    cols = [jnp.concatenate([g_ts[bi][0:ML_HEADS], b_ts[bi][ML_HEADS:2 * ML_HEADS],
                             cm[bi * ML_HEADS:(bi + 1) * ML_HEADS, :L],
                             jnp.zeros((LANES - 3 * ML_HEADS, L), f32)], axis=0).T for bi in range(BB)]
    rep_all = _exact_select(jnp.concatenate(cols, axis=0), sel)
    rep = [rep_all[bi * L:(bi + 1) * L] for bi in range(BB)]
    m_all = jnp.concatenate([m_ref[bi] for bi in range(BB)] + [jnp.zeros((8 - BB, LANES), f32)], axis=0)
    m_rep_all = _exact_select(m_all, sel[:, :ML_PAIRS * LANES])
    m_rep = [m_rep_all[bi:bi + 1] for bi in range(BB)]

    units = [(bi, pr) for bi in range(BB) for pr in range(ML_PAIRS)]
    st = {}
    for u in units:
        bi, pr = u
        tile = lambda qn: rep[bi][:, (qn * ML_PAIRS + pr) * LANES:(qn * ML_PAIRS + pr + 1) * LANES]
        ig_rep, b_rep, cm_rep = tile(0), tile(1), tile(2)
        m_pair = m_rep[bi][:, pr * LANES:(pr + 1) * LANES]
        inter = b_rep + m_pair
        mt = jnp.maximum(inter, b_rep + cm_rep)
        r_row = jnp.concatenate([r_rows[bi][2 * pr:2 * pr + 1], r_rows[bi][2 * pr + 1:2 * pr + 2]], axis=1)
        e = jnp.exp(jnp.where(causal2, (b_rep - mt) + r_row, -jnp.inf))
        m_new = mt[L - 1:L]
        b_last = b_rep[L - 1:L]
        lo = pr * LANES
        q = z_ref[bi, :, lo:lo + LANES].astype(bf16)
        k = z_ref[bi, :, ML_HEADS * ML_DK + lo:ML_HEADS * ML_DK + lo + LANES] * (ML_DK ** -0.5)
        v = z_ref[bi, :, 2 * ML_HEADS * ML_DK + lo:2 * ML_HEADS * ML_DK + lo + LANES]
        kbd = jnp.concatenate([jnp.where(first_half, k, 0.0), jnp.where(first_half, 0.0, k)], axis=0).astype(bf16)
        vbd = jnp.concatenate([jnp.where(first_half, v, 0.0), jnp.where(first_half, 0.0, v)], axis=0).astype(bf16)
        wk = jnp.exp(b_last - b_rep + ig_rep - m_new)
        st[u] = dict(a=jnp.exp(inter - mt), em=jnp.exp(-mt), e=e, m_new=m_new, dec=jnp.exp(b_last + m_pair - m_new),
                     q=q, kbd=kbd, vbd=vbd, k_t=k.T.astype(bf16), wkv=(wk * v).astype(bf16), wk=wk.astype(bf16),
                     cbd=cbd_ref[bi, pr], nbd=nbd_ref[bi, pr])
    for u in units:
        d = st[u]
        d["qk"] = lax.dot_general(d["q"], d["kbd"], (((1,), (1,)), ((), ())), preferred_element_type=f32)
        d["qc"] = jnp.dot(d["q"], d["cbd"].astype(bf16), preferred_element_type=f32)
        d["qn"] = jnp.dot(d["q"], d["nbd"].astype(bf16), preferred_element_type=f32)
    for u in units:
        d = st[u]
        s = (d["qk"] * d["e"]).astype(bf16)
        num = d["a"] * d["qc"] + jnp.dot(s, d["vbd"], preferred_element_type=f32)
        den = d["a"] * d["qn"] + jnp.dot(s, ones_bd, preferred_element_type=f32)
        d["hh"] = num / jnp.maximum(jnp.abs(den), d["em"])
    for u in units:
        bi, pr = u
        d = st[u]
        sq = d["hh"] * d["hh"]
        sq_hi = sq.astype(bf16)
        sq_lo = (sq - sq_hi.astype(f32)).astype(bf16)
        ms = (jnp.dot(sq_hi, ones_bd, preferred_element_type=f32)
              + jnp.dot(sq_lo, ones_bd, preferred_element_type=f32)) * (1.0 / ML_DV)
        lo = pr * LANES
        og = z_ref[bi, :, 3 * ML_HEADS * ML_DK + lo:3 * ML_HEADS * ML_DK + lo + LANES]
        h_ref[bi, :, lo:lo + LANES] = d["hh"] * lax.rsqrt(ms + RMS_EPS) * nw_ref[:, lo:lo + LANES] * _sigmoid(og)
    for u in units:
        bi, pr = u
        d = st[u]
        cbd_ref[bi, pr] = d["dec"] * d["cbd"] + jnp.where(
            same_block, jnp.dot(d["k_t"], d["wkv"], preferred_element_type=f32), 0.0)
        nbd_ref[bi, pr] = d["dec"] * d["nbd"] + jnp.where(
            same_block, jnp.dot(d["k_t"], d["wk"], preferred_element_type=f32), 0.0)
    for bi in range(BB):
        new_m = m_ref[bi]
        for pr in range(ML_PAIRS):
            m_new = st[(bi, pr)]["m_new"]
            new_m = jnp.where(lane_1 == 2 * pr, m_new[:, 0:1], new_m)
            new_m = jnp.where(lane_1 == 2 * pr + 1, m_new[:, ML_DV:ML_DV + 1], new_m)
        m_ref[bi] = new_m


def _mlstm_pairs(z_ml, z_g, bias_row, norm_row, B, T, L):
    assert 2 * L == LANES and ML_DK == ML_DV == L
    nc = T // L
    bb = min(B, 4)
    tok = lambda b, c: (b, c, 0)
    st4 = lambda b, c: (b, 0, 0, 0)
    st3 = lambda b, c: (b, 0, 0)
    const = lambda b, c: (0, 0)
    sel = _pair_select_matrix()
    zeros_bd = jnp.zeros((B, ML_PAIRS, LANES, LANES), f32)
    h, cbd, nbd, m = pl.pallas_call(
        functools.partial(_mlstm_pair_kernel, L=L, BB=bb),
        grid=(B // bb, nc),
        in_specs=[pl.BlockSpec((bb, L, ML_W), tok),
                  pl.BlockSpec((bb, L, LANES), tok),
                  pl.BlockSpec((1, LANES), const),
                  pl.BlockSpec((1, ML_HEADS * ML_DV), const),
                  pl.BlockSpec(sel.shape, const),
                  pl.BlockSpec((bb, ML_PAIRS, LANES, LANES), st4),
                  pl.BlockSpec((bb, ML_PAIRS, LANES, LANES), st4),
                  pl.BlockSpec((bb, 1, LANES), st3)],
        out_specs=[pl.BlockSpec((bb, L, ML_HEADS * ML_DV), tok),
                   pl.BlockSpec((bb, ML_PAIRS, LANES, LANES), st4),
                   pl.BlockSpec((bb, ML_PAIRS, LANES, LANES), st4),
                   pl.BlockSpec((bb, 1, LANES), st3)],
        out_shape=[SDS((B, T, ML_HEADS * ML_DV), f32),
                   SDS((B, ML_PAIRS, LANES, LANES), f32),
                   SDS((B, ML_PAIRS, LANES, LANES), f32),
                   SDS((B, 1, LANES), f32)],
        compiler_params=_cparams("parallel", "arbitrary"),
        name="mlstm_pairs",
    )(z_ml.reshape(B, T, ML_W), z_g.reshape(B, T, LANES), bias_row, norm_row, sel, zeros_bd, zeros_bd,
      jnp.zeros((B, 1, LANES), f32))
    c_out = jnp.stack([cbd[:, :, :ML_DK, :ML_DV], cbd[:, :, ML_DK:, ML_DV:]], axis=2).reshape(B, ML_HEADS, ML_DK, ML_DV)
    n_out = jnp.stack([nbd[:, :, :ML_DK, 0], nbd[:, :, ML_DK:, ML_DV]], axis=2).reshape(B, ML_HEADS, ML_DK)
    return h, c_out, n_out, m[:, 0, :ML_HEADS]


def _rope(x, cos, sin_signed):
    w = x.shape[-1]
    lane = lax.broadcasted_iota(jnp.int32, x.shape, 1)
    swapped = jnp.where((lane % SW_HD) < SW_HD // 2, pltpu.roll(x, w - SW_HD // 2, 1), pltpu.roll(x, SW_HD // 2, 1))
    return x * cos + swapped * sin_signed


def _swa_attend(jobs, sinks_ref, L, precise=False):
    mm, mm_nt, _ = _dots(precise)
    units = [(j, g) for j in range(len(jobs)) for g in range(SW_KV_HEADS)]
    sinks = [jnp.concatenate(
        [jnp.broadcast_to(sinks_ref[:, g * SW_GROUP + i:g * SW_GROUP + i + 1], (L, 1)) for i in range(SW_GROUP)],
        axis=0) for g in range(SW_KV_HEADS)]
    s, p, sink_e = {}, {}, {}
    for j, g in units:
        qr, keys, _, _ = jobs[j]
        q4 = jnp.concatenate([qr[:, (g * SW_GROUP + i) * SW_HD:(g * SW_GROUP + i + 1) * SW_HD]
                              for i in range(SW_GROUP)], axis=0)
        s[j, g] = mm_nt(q4, keys[:, g * SW_HD:(g + 1) * SW_HD]) * (SW_HD ** -0.5)
    for j, g in units:
        first_valid = jobs[j][3]
        sc = s[j, g]
        if first_valid is not None:
            kcol = lax.broadcasted_iota(jnp.int32, (1, sc.shape[1]), 1)
            sc = jnp.where(kcol >= first_valid, sc, -jnp.inf)
        mx = jnp.maximum(jnp.max(sc, axis=-1, keepdims=True), sinks[g])
        p[j, g] = jnp.exp(sc - mx)
        sink_e[j, g] = jnp.exp(sinks[g] - mx)
    ones = jnp.ones((jobs[0][1].shape[0], SW_HD), f32)
    o = {u: mm(p[u], jobs[u[0]][2][:, u[1] * SW_HD:(u[1] + 1) * SW_HD]) / (mm(p[u], ones) + sink_e[u]) for u in units}
    return [jnp.concatenate([o[j, g][i * L:(i + 1) * L, :] for g in range(SW_KV_HEADS) for i in range(SW_GROUP)],
                            axis=-1) for j in range(len(jobs))]


def _swa_prompt_kernel(q_ref, kp_ref, kc_ref, vp_ref, vc_ref, cp_ref, cc_ref, sp_ref, sc_ref, sinks_ref,
                       o_ref, kr_ref, *, L, CB):
    i = pl.program_id(1)
    rows = CB * L
    back = 2 * L
    cos_q = jnp.concatenate([cc_ref[...]] * (SW_HEADS // SW_KV_HEADS), axis=-1)
    sin_q = jnp.concatenate([sc_ref[...]] * (SW_HEADS // SW_KV_HEADS), axis=-1)
    qr = _rope(q_ref[...], cos_q, sin_q)
    k_cur = _rope(kc_ref[...], cc_ref[...], sc_ref[...])
    kr_ref[...] = k_cur
    k_prev = _rope(kp_ref[rows - back:rows, :], cp_ref[rows - back:rows, :], sp_ref[rows - back:rows, :])
    keys = jnp.concatenate([k_prev, k_cur], axis=0)
    vals = jnp.concatenate([vp_ref[rows - back:rows, :], vc_ref[...]], axis=0)
    jobs = []
    for u in range(CB):
        first_valid = jnp.where(i == 0, back - u * L, 0) if u * L < back else None
        jobs.append((qr[u * L:(u + 1) * L], keys[u * L:(u + 3) * L], vals[u * L:(u + 3) * L], first_valid))
    o_ref[...] = jnp.concatenate(_swa_attend(jobs, sinks_ref, L), axis=0)


def _swa_prompt(z_sw, cos_t, sin_t, sinks_row, B, T, L):
    cb = 8
    rows = cb * L
    nb = T // rows
    n = B * T
    kcol = SW_HEADS * SW_HD // LANES
    vcol = kcol + 1
    cur = lambda b, i: (b * nb + i, 0)
    prev = lambda col: (lambda b, i: (b * nb + jnp.maximum(i - 1, 0), col))
    curc = lambda col: (lambda b, i: (b * nb + i, col))
    tab_cur = lambda b, i: (i, 0)
    tab_prev = lambda b, i: (jnp.maximum(i - 1, 0), 0)
    return pl.pallas_call(
        functools.partial(_swa_prompt_kernel, L=L, CB=cb),
        grid=(B, nb),
        in_specs=[pl.BlockSpec((rows, SW_HEADS * SW_HD), cur),
                  pl.BlockSpec((rows, LANES), prev(kcol)), pl.BlockSpec((rows, LANES), curc(kcol)),
                  pl.BlockSpec((rows, LANES), prev(vcol)), pl.BlockSpec((rows, LANES), curc(vcol)),
                  pl.BlockSpec((rows, LANES), tab_prev), pl.BlockSpec((rows, LANES), tab_cur),
                  pl.BlockSpec((rows, LANES), tab_prev), pl.BlockSpec((rows, LANES), tab_cur),
                  pl.BlockSpec((1, LANES), lambda b, i: (0, 0))],
        out_specs=[pl.BlockSpec((rows, SW_HEADS * SW_HD), cur),
                   pl.BlockSpec((rows, LANES), cur)],
        out_shape=[SDS((n, SW_HEADS * SW_HD), f32), SDS((n, LANES), f32)],
        compiler_params=_cparams("parallel", "parallel"),
        name="swa_prompt",
    )(z_sw, z_sw, z_sw, z_sw, z_sw, cos_t, cos_t, sin_t, sin_t, sinks_row)


def _swa_sample_kernel(q_ref, k_ref, v_ref, ck_ref, cv_ref, cos_ref, sin_ref, sinks_ref, o_ref, kr_ref, *, L):
    cos_q = jnp.concatenate([cos_ref[...]] * (SW_HEADS // SW_KV_HEADS), axis=-1)
    sin_q = jnp.concatenate([sin_ref[...]] * (SW_HEADS // SW_KV_HEADS), axis=-1)
    qr = _rope(q_ref[...], cos_q, sin_q)
    kr = _rope(k_ref[...], cos_ref[...], sin_ref[...])
    kr_ref[...] = kr
    keys = jnp.concatenate([ck_ref[0], kr], axis=0)
    vals = jnp.concatenate([cv_ref[0], v_ref[...]], axis=0)
    o_ref[...] = _swa_attend([(qr, keys, vals, None)], sinks_ref, L, precise=True)[0]


def _swa_sample(z_sw, cache_k, cache_v, cos_t, sin_t, sinks_row, B, T):
    n = B * T
    kcol = SW_HEADS * SW_HD // LANES
    return pl.pallas_call(
        functools.partial(_swa_sample_kernel, L=T),
        grid=(B,),
        in_specs=[pl.BlockSpec((T, SW_HEADS * SW_HD), lambda b: (b, 0)),
                  pl.BlockSpec((T, LANES), lambda b: (b, kcol)),
                  pl.BlockSpec((T, LANES), lambda b: (b, kcol + 1)),
                  pl.BlockSpec((1, WINDOW, LANES), lambda b: (b, 0, 0)),
                  pl.BlockSpec((1, WINDOW, LANES), lambda b: (b, 0, 0)),
                  pl.BlockSpec((T, LANES), lambda b: (0, 0)),
                  pl.BlockSpec((T, LANES), lambda b: (0, 0)),
                  pl.BlockSpec((1, LANES), lambda b: (0, 0))],
        out_specs=[pl.BlockSpec((T, SW_HEADS * SW_HD), lambda b: (b, 0)),
                   pl.BlockSpec((T, LANES), lambda b: (b, 0))],
        out_shape=[SDS((n, SW_HEADS * SW_HD), f32), SDS((n, LANES), f32)],
        compiler_params=_cparams("parallel"),
        name="swa_sample",
    )(z_sw, z_sw, z_sw, cache_k, cache_v, cos_t, sin_t, sinks_row)


def _out_ln_kernel(*refs, n_in):
    x_ref = refs[0]
    a_refs = refs[1:1 + n_in]
    w_refs = refs[1 + n_in:1 + 2 * n_in]
    g_ref, b_ref, rw_ref, rb_ref, o_ref, gates_ref, route_ref = refs[1 + 2 * n_in:]
    mm = _dots(w_refs[0].dtype == f32)[0]
    y = mm(a_refs[0][...], w_refs[0][...])
    for a_ref, w_ref in zip(a_refs[1:], w_refs[1:]):
        y = y + mm(a_ref[...], w_ref[...])
    x_new = _layer_norm(DN_ALPHA * x_ref[...] + y, g_ref[...], b_ref[...])
    o_ref[...] = x_new
    gates_ref[...], route_ref[...] = _route(x_new, rw_ref, rb_ref)


def _out_ln(x2, acts, ws, g_row, b_row, rw_t, rb_col, tm):
    n = x2.shape[0]
    row = lambda i: (i, 0)
    const = lambda i: (0, 0)
    col = lambda i: (0, i)
    return pl.pallas_call(
        functools.partial(_out_ln_kernel, n_in=len(acts)),
        grid=(n // tm,),
        in_specs=[pl.BlockSpec((tm, D_MODEL), row)]
        + [pl.BlockSpec((tm, a.shape[1]), row) for a in acts]
        + [pl.BlockSpec(w.shape, const) for w in ws]
        + [pl.BlockSpec((1, D_MODEL), const), pl.BlockSpec((1, D_MODEL), const),
           pl.BlockSpec((N_EXPERTS, D_MODEL), const), pl.BlockSpec((N_EXPERTS, 1), const)],
        out_specs=[pl.BlockSpec((tm, D_MODEL), row), pl.BlockSpec((N_EXPERTS, tm), col), pl.BlockSpec((8, tm), col)],
        out_shape=[SDS((n, D_MODEL), f32), SDS((N_EXPERTS, n), f32), SDS((8, n), f32)],
        compiler_params=_cparams("parallel"),
        name="out_proj_ln",
    )(x2, *acts, *ws, g_row, b_row, rw_t, rb_col)


def _logistic(x):
    return 1.0 / (1.0 + jnp.exp(-x))


def _route(x, rw_ref, rb_ref):
    logits = _dot_nt(rw_ref[...], x)
    aff = _logistic(logits)
    sc = aff + rb_ref[...]
    s = [sc[e:e + 1, :] for e in range(N_EXPERTS)]
    a = [aff[e:e + 1, :] for e in range(N_EXPERTS)]
    scores = []
    for gi in range(N_GROUPS):
        w, x, y, z = s[4 * gi:4 * gi + 4]
        p, q = jnp.maximum(w, x), jnp.minimum(w, x)
        r, t = jnp.maximum(y, z), jnp.minimum(y, z)
        scores.append(jnp.maximum(p, r) + jnp.maximum(jnp.minimum(p, r), jnp.maximum(q, t)))
    best = scores[0]
    gsel = jnp.zeros_like(best, dtype=jnp.int32)
    for gi in range(1, N_GROUPS):
        better = scores[gi] > best
        best = jnp.where(better, scores[gi], best)
        gsel = jnp.where(better, gi, gsel)
    sel = []
    for e in range(N_EXPERTS):
        gi, i = divmod(e, EXP_PER_GROUP)
        beaten = jnp.zeros_like(gsel)
        for j in range(EXP_PER_GROUP):
            if j == i:
                continue
            o = s[4 * gi + j]
            wins = (o >= s[e]) if j < i else (o > s[e])
            beaten = beaten + wins.astype(jnp.int32)
        sel.append((gsel == gi) & (beaten < 2))
    den = jnp.zeros_like(best)
    for e in range(N_EXPERTS):
        den = den + jnp.where(sel[e], a[e], 0.0)
    gate = [jnp.where(sel[e], a[e] / den, 0.0) for e in range(N_EXPERTS)]
    taken = jnp.zeros_like(gsel)
    ea = eb = wa = wb = jnp.zeros_like(best)
    for e in range(N_EXPERTS):
        first = sel[e] & (taken == 0)
        second = sel[e] & (taken == 1)
        ea = jnp.where(first, float(e), ea)
        wa = jnp.where(first, gate[e], wa)
        eb = jnp.where(second, float(e), eb)
        wb = jnp.where(second, gate[e], wb)
        taken = taken + sel[e].astype(jnp.int32)
    route = jnp.concatenate([ea, eb, wa, wb, jnp.zeros((4, ea.shape[1]), f32)], axis=0)
    return jnp.concatenate(gate, axis=0), route


def _moe_kernel(x_ref, gates_ref, wg_ref, wu_ref, wd_ref, g_ref, b_ref, o_ref, xb_ref, acc_ref):
    e = pl.program_id(1)

    @pl.when(e == 0)
    def _():
        xb_ref[...] = x_ref[...].astype(bf16)
        acc_ref[...] = jnp.zeros_like(acc_ref)

    xb = xb_ref[...]
    lane = lax.broadcasted_iota(jnp.int32, gates_ref.shape, 1)
    gcol = jnp.sum(jnp.where(lane == e, gates_ref[...], 0.0), axis=-1, keepdims=True)
    h = _silu(_bdot(xb, wg_ref[0])) * _bdot(xb, wu_ref[0])
    acc_ref[...] += _bdot(gcol * h, wd_ref[0])

    @pl.when(e == N_EXPERTS - 1)
    def _():
        o_ref[...] = _layer_norm(DN_ALPHA * x_ref[...] + acc_ref[...], g_ref[...], b_ref[...])


def _moe_ln(x2, gates, wg, wu, wd, base, g_row, b_row, tm):
    n = x2.shape[0]
    row = lambda i, e: (i, 0)
    const = lambda i, e: (0, 0)
    expert = lambda i, e: (base + e, 0, 0)
    return pl.pallas_call(
        _moe_kernel,
        grid=(n // tm, N_EXPERTS),
        in_specs=[pl.BlockSpec((tm, D_MODEL), row),
                  pl.BlockSpec((tm, N_EXPERTS), row),
                  pl.BlockSpec((1, D_MODEL, D_EXPERT), expert),
                  pl.BlockSpec((1, D_MODEL, D_EXPERT), expert),
                  pl.BlockSpec((1, D_EXPERT, D_MODEL), expert),
                  pl.BlockSpec((1, D_MODEL), const), pl.BlockSpec((1, D_MODEL), const)],
        out_specs=pl.BlockSpec((tm, D_MODEL), row),
        out_shape=SDS((n, D_MODEL), f32),
        scratch_shapes=[pltpu.VMEM((tm, D_MODEL), bf16), pltpu.VMEM((tm, D_MODEL), f32)],
        compiler_params=_cparams("parallel", "arbitrary"),
        name="moe_ln",
    )(x2, gates, wg, wu, wd, g_row, b_row)


N_PAIRS = N_GROUPS * (EXP_PER_GROUP * (EXP_PER_GROUP - 1) // 2)
MOE_TM = 256
MOE_DMA_ROWS = 512
_PAIR_A = [g * EXP_PER_GROUP + a for g in range(N_GROUPS) for a in range(EXP_PER_GROUP) for b in range(a + 1, EXP_PER_GROUP)]
_PAIR_B = [g * EXP_PER_GROUP + b for g in range(N_GROUPS) for a in range(EXP_PER_GROUP) for b in range(a + 1, EXP_PER_GROUP)]


def _gather_rows_kernel(idx_ref, src_ref, o_ref, sem, *, rows):
    base = pl.program_id(0) * rows

    def row_copy(j):
        return pltpu.make_async_copy(src_ref.at[pl.ds(idx_ref[base + j], 1)], o_ref.at[pl.ds(j, 1)], sem)

    def issue(j, carry):
        row_copy(j).start()
        return carry

    def drain(j, carry):
        row_copy(j).wait()
        return carry

    lax.fori_loop(0, rows, issue, 0, unroll=8)
    lax.fori_loop(0, rows, drain, 0, unroll=8)


def _gather_rows(src, idx, rows):
    n_out = idx.shape[0]
    d = src.shape[1]
    return pl.pallas_call(
        functools.partial(_gather_rows_kernel, rows=rows),
        grid_spec=pltpu.PrefetchScalarGridSpec(
            num_scalar_prefetch=1,
            grid=(n_out // rows,),
            in_specs=[pl.BlockSpec(memory_space=pl.ANY)],
            out_specs=pl.BlockSpec((rows, d), lambda i, idx_ref: (i, 0)),
            scratch_shapes=[pltpu.SemaphoreType.DMA(())]),
        out_shape=SDS((n_out, d), f32),
        compiler_params=_cparams("arbitrary"),
        name="gather_rows",
    )(idx, src)


def _scatter_rows_kernel(idx_ref, src_ref, init_ref, o_ref, sem, *, rows):
    del init_ref
    base = pl.program_id(0) * rows

    def row_copy(j):
        return pltpu.make_async_copy(src_ref.at[pl.ds(j, 1)], o_ref.at[pl.ds(idx_ref[base + j], 1)], sem)

    def issue(j, carry):
        row_copy(j).start()
        return carry

    def drain(j, carry):
        row_copy(j).wait()
        return carry

    lax.fori_loop(0, rows, issue, 0, unroll=8)
    lax.fori_loop(0, rows, drain, 0, unroll=8)


def _scatter_rows(src, idx, n_out, rows, init=None):
    n_src, d = src.shape
    if init is None:
        init = jnp.zeros((n_out, d), f32)
    return pl.pallas_call(
        functools.partial(_scatter_rows_kernel, rows=rows),
        grid_spec=pltpu.PrefetchScalarGridSpec(
            num_scalar_prefetch=1,
            grid=(n_src // rows,),
            in_specs=[pl.BlockSpec((rows, d), lambda i, idx_ref: (i, 0)), pl.BlockSpec(memory_space=pl.ANY)],
            out_specs=pl.BlockSpec(memory_space=pl.ANY),
            scratch_shapes=[pltpu.SemaphoreType.DMA(())]),
        out_shape=SDS((n_out, d), f32),
        input_output_aliases={2: 0},
        compiler_params=_cparams("arbitrary"),
        name="scatter_rows",
    )(idx, src, init)


def _pair_plan(route, n, tm):
    ea = route[0].astype(jnp.int32)
    eb = route[1].astype(jnp.int32)
    a = ea % EXP_PER_GROUP
    b = eb % EXP_PER_GROUP
    pidx = jnp.where(a == 0, b - 1, jnp.where(a == 1, b + 1, 5))
    pair = (ea // EXP_PER_GROUP) * (N_PAIRS // N_GROUPS) + pidx
    onehot = (pair[:, None] == jnp.arange(N_PAIRS, dtype=jnp.int32)[None, :]).astype(jnp.int32)
    csum = jnp.cumsum(onehot, axis=0)
    counts = csum[-1]
    ntiles = (counts + tm - 1) // tm
    tile_end = jnp.cumsum(ntiles)
    tile_start = tile_end - ntiles
    row_of_token = jnp.sum(onehot * (csum - 1 + (tile_start * tm)[None, :]), axis=1)
    nt = n // tm + N_PAIRS
    tile_id = jnp.arange(nt, dtype=jnp.int32)
    tile_valid = tile_id < tile_end[-1]
    tile_pair = jnp.sum((tile_end[None, :] <= jnp.minimum(tile_id, tile_end[-1] - 1)[:, None]).astype(jnp.int32), axis=1)
    tile_pair = jnp.minimum(tile_pair, N_PAIRS - 1)
    pick = (tile_pair[:, None] == jnp.arange(N_PAIRS, dtype=jnp.int32)[None, :]).astype(jnp.int32)
    tile_a = jnp.sum(pick * jnp.asarray(_PAIR_A, jnp.int32)[None, :], axis=1)
    tile_b = jnp.sum(pick * jnp.asarray(_PAIR_B, jnp.int32)[None, :], axis=1)
    return row_of_token, tile_a, tile_b, tile_valid.astype(jnp.int32)


def _pair_expert_kernel(ta_ref, tb_ref, tv_ref, x_ref, rwt_ref, wga_ref, wua_ref, wda_ref, wgb_ref, wub_ref, wdb_ref,
                        g_ref, b_ref, o_ref):
    i = pl.program_id(0)
    valid = tv_ref[i] == 1

    @pl.when(valid)
    def _():
        x = x_ref[...]
        xb = x.astype(bf16)
        aff_a = _logistic(jnp.sum(x * rwt_ref[pl.ds(ta_ref[i], 1), :], axis=-1, keepdims=True))
        aff_b = _logistic(jnp.sum(x * rwt_ref[pl.ds(tb_ref[i], 1), :], axis=-1, keepdims=True))
        den = aff_a + aff_b
        acc = None
        for w, (wg, wu, wd) in ((aff_a / den, (wga_ref, wua_ref, wda_ref)), (aff_b / den, (wgb_ref, wub_ref, wdb_ref))):
            h = _silu(_bdot(xb, wg[0])) * _bdot(xb, wu[0])
            y = _bdot(w * h, wd[0])
            acc = y if acc is None else acc + y
        o_ref[...] = _layer_norm(DN_ALPHA * x + acc, g_ref[...], b_ref[...])

    @pl.when(jnp.logical_not(valid))
    def _():
        o_ref[...] = jnp.zeros_like(o_ref)


def _pair_experts(xs, rw_t, tile_a, tile_b, tile_valid, wg, wu, wd, base, g_row, b_row, tm):
    rows = xs.shape[0]
    row = lambda i, ta, tb, tv: (i, 0)
    const = lambda i, ta, tb, tv: (0, 0)
    ex_a = lambda i, ta, tb, tv: (base + ta[i], 0, 0)
    ex_b = lambda i, ta, tb, tv: (base + tb[i], 0, 0)
    up = pl.BlockSpec((1, D_MODEL, D_EXPERT), ex_a), pl.BlockSpec((1, D_MODEL, D_EXPERT), ex_b)
    down = pl.BlockSpec((1, D_EXPERT, D_MODEL), ex_a), pl.BlockSpec((1, D_EXPERT, D_MODEL), ex_b)
    return pl.pallas_call(
        _pair_expert_kernel,
        grid_spec=pltpu.PrefetchScalarGridSpec(
            num_scalar_prefetch=3,
            grid=(rows // tm,),
            in_specs=[pl.BlockSpec((tm, D_MODEL), row), pl.BlockSpec((N_EXPERTS, D_MODEL), const),
                      up[0], up[0], down[0], up[1], up[1], down[1],
                      pl.BlockSpec((1, D_MODEL), const), pl.BlockSpec((1, D_MODEL), const)],
            out_specs=pl.BlockSpec((tm, D_MODEL), row)),
        out_shape=SDS((rows, D_MODEL), f32),
        compiler_params=_cparams("arbitrary"),
        name="pair_experts",
    )(tile_a, tile_b, tile_valid, xs, rw_t, wg, wu, wd, wg, wu, wd, g_row, b_row)


def _unit_lower_inverses(mats, L):
    row = lax.broadcasted_iota(jnp.int32, (L, L), 0)
    col = lax.broadcasted_iota(jnp.int32, (L, L), 1)
    eye = (row == col).astype(f32)
    ps = [eye - a for a in mats]
    pws = [a.astype(bf16) for a in mats]
    span = 2
    while span < L:
        pws = [jnp.dot(pw, pw, preferred_element_type=f32).astype(bf16) for pw in pws]
        ps = [p + jnp.dot(pw, p.astype(bf16), preferred_element_type=f32) for p, pw in zip(ps, pws)]
        span *= 2
    return ps


def _gdn_kernel(x_ref, zg_ref, ba_ref, cw_ref, alog_ref, dt_ref, nw_ref, s0_ref, cb_ref,
                o_ref, s_ref, prev_ref, *, L, BB):
    @pl.when(pl.program_id(1) == 0)
    def _():
        s_ref[...] = s0_ref[...]
        prev_ref[...] = cb_ref[...]

    def conv_silu(bi, lo, width):
        cur = x_ref[bi, :, lo:lo + width]
        cat =jnp.concatenate([prev_ref[bi, :, lo:lo + width], cur], axis=0)
        acc = cat[5:5 + L] * cw_ref[0:1, lo:lo + width]
        acc = acc + cat[6:6 + L] * cw_ref[1:2, lo:lo + width]
        acc = acc + cat[7:7 + L] * cw_ref[2:3, lo:lo + width]
        acc = acc + cur * cw_ref[3:4, lo:lo + width]
        return _silu(acc)

    def l2n(v, scale):
        return v * lax.rsqrt(jnp.sum(v * v, axis=-1, keepdims=True) + 1e-6) * scale

    row = lax.broadcasted_iota(jnp.int32, (L, L), 0)
    col = lax.broadcasted_iota(jnp.int32, (L, L), 1)
    incl = row >= col
    strict = row > col
    rep = GD_V_HEADS // GD_QK_HEADS
    bas = [ba_ref[bi] for bi in range(BB)]
    betas = [_sigmoid(ba) for ba in bas]
    gls = [-jnp.exp(alog_ref[...]) * _softplus(ba + dt_ref[...]) for ba in bas]
    gcum_all = _dot(incl.astype(f32), jnp.concatenate(gls, axis=1))
    gcums = [gcum_all[:, bi * LANES:(bi + 1) * LANES] for bi in range(BB)]
    gcum_ts = [g.T for g in gcums]
    units = [(bi, hv) for bi in range(BB) for hv in range(GD_V_HEADS)]
    qs, ks, amats, qkds, egs, g_cols, rhss = {}, {}, [], {}, {}, {}, []
    for bi in range(BB):
        for j in range(GD_QK_HEADS):
            q = l2n(conv_silu(bi, j * GD_HD, GD_HD), GD_HD ** -0.5)
            k = l2n(conv_silu(bi, GD_QK_W + j * GD_HD, GD_HD), 1.0)
            qb, kb = q.astype(bf16), k.astype(bf16)
            kk = lax.dot_general(kb, kb, (((1,), (1,)), ((), ())), preferred_element_type=f32)
            qk = lax.dot_general(qb, kb, (((1,), (1,)), ((), ())), preferred_element_type=f32)
            qs[bi, j] = qb
            ks[bi, j] = k
            for r in range(rep):
                hv = j * rep + r
                v = conv_silu(bi, 2 * GD_QK_W + hv * GD_HD, GD_HD)
                g_col = gcums[bi][:, GD_V_HEADS + hv:GD_V_HEADS + hv + 1]
                g_row = gcum_ts[bi][GD_V_HEADS + hv:GD_V_HEADS + hv + 1, :]
                b_col = betas[bi][:, hv:hv + 1]
                decay = jnp.exp(jnp.where(incl, g_col - g_row, -jnp.inf))
                eg = jnp.exp(g_col)
                amats.append(jnp.where(strict, b_col * kk * decay, 0.0))
                qkds[bi, hv] = (qk * decay).astype(bf16)
                egs[bi, hv] = eg
                g_cols[bi, hv] = g_col
                rhss.append(((b_col * v).astype(bf16), ((b_col * eg) * k).astype(bf16)))
    tinvs = _unit_lower_inverses(amats, L)
    tinvs = [t.astype(bf16) for t in tinvs]
    sol_v = dict(zip(units, [jnp.dot(t, r[0], preferred_element_type=f32) for t, r in zip(tinvs, rhss)]))
    sol_k = dict(zip(units, [jnp.dot(t, r[1], preferred_element_type=f32) for t, r in zip(tinvs, rhss)]))
    sts, wks, qss, kts = {}, {}, {}, {}
    for u in units:
        bi, hv = u
        j = hv // rep
        sts[u] = s_ref[bi, hv]
        stb = sts[u].astype(bf16)
        wks[u] = jnp.dot(sol_k[u].astype(bf16), stb, preferred_element_type=f32)
        qss[u] = jnp.dot(qs[bi, j], stb, preferred_element_type=f32)
        g_col = g_cols[u]
        kts[u] = (jnp.exp(g_col[L - 1:L, :] - g_col) * ks[bi, j]).T.astype(bf16)
    outs = {}
    for u in units:
        bi, hv = u
        wnb = (sol_v[u] - wks[u]).astype(bf16)
        outs[u] = egs[u] * qss[u] + jnp.dot(qkds[u], wnb, preferred_element_type=f32)
        s_ref[bi, hv] = jnp.exp(g_cols[u][L - 1:L, :]) * sts[u] + jnp.dot(kts[u], wnb, preferred_element_type=f32)
    for u in units:
        bi, hv = u
        o = outs[u]
        o = o * lax.rsqrt(jnp.mean(o * o, axis=-1, keepdims=True) + RMS_EPS) * nw_ref[...]
        o_ref[bi, :, hv * GD_HD:(hv + 1) * GD_HD] = o * _silu(zg_ref[bi, :, hv * GD_HD:(hv + 1) * GD_HD])
    for bi in range(BB):
        prev_ref[bi] = x_ref[bi, L - 8:L, :]


def _gdn(qkv, zg, ba, conv_w, alog_row, dt_row, norm_row, s0, conv8, B, T, L):
    nc = T // L
    n = B * T
    bb = 2
    tok = lambda b, c: (b, c, 0)
    const = lambda b, c: (0, 0)
    st4 = lambda b, c: (b, 0, 0, 0)
    o, s_out = pl.pallas_call(
        functools.partial(_gdn_kernel, L=L, BB=bb),
        grid=(B // bb, nc),
        in_specs=[pl.BlockSpec((bb, L, GD_CONV_CH), tok),
                  pl.BlockSpec((bb, L, GD_V_W), tok),
                  pl.BlockSpec((bb, L, LANES), tok),
                  pl.BlockSpec((GD_CONV, GD_CONV_CH), const),
                  pl.BlockSpec((1, LANES), const),
                  pl.BlockSpec((1, LANES), const),
                  pl.BlockSpec((1, GD_HD), const),
                  pl.BlockSpec((bb, GD_V_HEADS, GD_HD, GD_HD), st4),
                  pl.BlockSpec((bb, 8, GD_CONV_CH), lambda b, c: (b, 0, 0))],
        out_specs=[pl.BlockSpec((bb, L, GD_V_W), tok),
                   pl.BlockSpec((bb, GD_V_HEADS, GD_HD, GD_HD), st4)],
        out_shape=[SDS((B, T, GD_V_W), f32), SDS((B, GD_V_HEADS, GD_HD, GD_HD), f32)],
        scratch_shapes=[pltpu.VMEM((bb, 8, GD_CONV_CH), f32)],
        compiler_params=_cparams("parallel", "arbitrary"),
        name="gdn",
    )(qkv.reshape(B, T, GD_CONV_CH), zg.reshape(B, T, GD_V_W), ba.reshape(B, T, LANES), conv_w, alog_row, dt_row,
      norm_row, s0, conv8)
    return o.reshape(n, GD_V_W), s_out


def _pad_lanes(row, offset=0):
    return jnp.zeros((1, LANES), f32).at[0, offset:offset + row.shape[0]].set(row.astype(f32))


def _rope_tables(pos):
    half = SW_HD // 2
    inv = ROPE_THETA ** (-jnp.arange(half, dtype=f32) / half)
    ang = pos.astype(f32)[:, None] * inv[None, :]
    cos, sin = jnp.cos(ang), jnp.sin(ang)
    cos_t = jnp.concatenate([cos, cos] * SW_KV_HEADS, axis=-1)
    sin_t = jnp.concatenate([-sin, sin] * SW_KV_HEADS, axis=-1)
    return cos_t, sin_t


def _tile(n, pref):
    return pref if n % pref == 0 else n


def _trunk(x, pos, L, state, p):
    B, T, _ = x.shape
    n = B * T
    x2 = x.reshape(n, D_MODEL)
    tm = _tile(n, 512)
    precise = state is not None
    ab_w, wo_h, wo_a = (p["ab_w32"], p["ab_wo_h32"], p["ab_wo_a32"]) if precise else (p["ab_w"], p["ab_wo_h"], p["ab_wo_a"])

    z_ml, z_sw, z_g = _proj(x2, ab_w, ((0, ML_W), (ML_W, SW_W), (ML_W + SW_W, LANES)), tm)
    if state is None:
        h_ml, ml_c, ml_n, ml_m = _mlstm_pairs(z_ml, z_g, p["ab_bias"], p["ab_norm"], B, T, L)
    else:
        cn0 = jnp.concatenate([state["ml_C"], state["ml_n"][..., None],
                               jnp.zeros((B, ML_HEADS, ML_DK, LANES - ML_DV - 1), f32)], axis=-1)
        m0 = jnp.zeros((B, 1, LANES), f32).at[:, 0, :ML_HEADS].set(state["ml_m"])
        h_ml, cn, m_out = _mlstm(z_ml, z_g, p["ab_bias"], p["ab_norm"], cn0, m0, B, T, L, precise)
        ml_c, ml_n, ml_m = cn[..., :ML_DV], cn[..., ML_DV], m_out[:, 0, :ML_HEADS]
    h_ml = h_ml.reshape(n, ML_HEADS * ML_DV)
    cos_t, sin_t = _rope_tables(pos)
    if state is None:
        a_sw, k_rot = _swa_prompt(z_sw, cos_t, sin_t, p["ab_sinks"], B, T, L)
    else:
        a_sw, k_rot = _swa_sample(z_sw, state["sw_k"].reshape(B, WINDOW, LANES),
                                  state["sw_v"].reshape(B, WINDOW, LANES), cos_t, sin_t, p["ab_sinks"], B, T)
    keep = min(T, WINDOW)
    new_k = k_rot.reshape(B, T, LANES)[:, T - keep:].reshape(B, keep, SW_KV_HEADS, SW_HD)
    new_v = z_sw.reshape(B, T, SW_W)[:, T - keep:, SW_HEADS * SW_HD + LANES:].reshape(B, keep, SW_KV_HEADS, SW_HD)
    x2, gates_t, route = _out_ln(x2, [h_ml, a_sw], [wo_h, wo_a], p["ln_g"][0][0], p["ln_b"][0][0],
                                 p["router_wt"], p["router_b"], tm)
    x2, spare = _moe_block(x2, gates_t, route, p, 0)

    tm1 = _tile(n, 256)
    if state is None:
        s0 = jnp.zeros((B, GD_V_HEADS, GD_HD, GD_HD), f32)
        conv8 = jnp.zeros((B, 8, GD_CONV_CH), f32)
    else:
        s0 = state["gd_S"]
        conv8 = jnp.concatenate([jnp.zeros((B, 8 - (GD_CONV - 1), GD_CONV_CH), f32), state["gd_conv"]], axis=1)
    qkv, zg, ba = _proj(x2, p["c_w"], ((0, GD_CONV_CH), (GD_CONV_CH, GD_V_W), (GD_CONV_CH + GD_V_W, LANES)), tm1)
    new_conv = qkv.reshape(B, T, GD_CONV_CH)[:, T - (GD_CONV - 1):]
    o_gd, s_out = _gdn(qkv, zg, ba, p["c_conv_w"], p["c_alog"], p["c_dt"], p["c_norm"], s0, conv8, B, T, L)
    x2, gates_t, route = _out_ln(x2, [o_gd], [p["c_wo"]], p["ln_g"][1][0], p["ln_b"][1][0],
                                 p["router_wt"], p["router_b"], tm)
    x2, _ = _moe_block(x2, gates_t, route, p, 1, spare)

    outs = (new_k[None], new_v[None], ml_c[None], ml_n[None], ml_m[None],
            s_out[None], new_conv[None])
    return x2.reshape(B, T, D_MODEL), outs


def _moe_block(x2, gates_t, route, p, layer, spare=None):
    n = x2.shape[0]
    wg, wu, wd = p["ex_gate"], p["ex_up"], p["ex_down"]
    base = layer * N_EXPERTS
    g_row, b_row = p["ln_g"][layer][1], p["ln_b"][layer][1]
    if n < N_PAIRS * MOE_TM:
        return _moe_ln(x2, gates_t.T, wg, wu, wd, base, g_row, b_row, _tile(n, 1024)), None
    row_of_token, tile_a, tile_b, tile_valid = _pair_plan(route, n, MOE_TM)
    xs = _scatter_rows(x2, row_of_token, n + N_PAIRS * MOE_TM, MOE_DMA_ROWS, spare)
    ys = _pair_experts(xs, p["router_wt"], tile_a, tile_b, tile_valid, wg, wu, wd, base, g_row, b_row, MOE_TM)
    return _gather_rows(ys, row_of_token, MOE_DMA_ROWS), ys


def kernel(x_prompt, x_sample, cache_swa_k, cache_swa_v, state_mlstm_C, state_mlstm_n, state_mlstm_m, state_gdn_S, state_gdn_conv, ab_w_in, ab_b_i, ab_b_f, ab_norm, ab_sinks, ab_w_out, c_w_in, c_conv_w, c_a_log, c_dt_bias, c_norm, c_w_out, ln_g, ln_b, router_w, router_b, ex_gate, ex_up, ex_down):
    gate_lo = ML_W
    sw_lo = ML_W + 2 * ML_HEADS
    w0 = ab_w_in[0]
    ab_w32 = jnp.concatenate([w0[:, :gate_lo], w0[:, sw_lo:], w0[:, gate_lo:sw_lo],
                              jnp.zeros((D_MODEL, LANES - 2 * ML_HEADS), f32)], axis=1)
    w1 = c_w_in[0]
    c_w = jnp.concatenate([w1, jnp.zeros((D_MODEL, LANES - 2 * GD_V_HEADS), f32)], axis=1).astype(bf16)
    wo32 = ab_w_out[0]
    wo = wo32.astype(bf16)
    p = {
        "ab_w": ab_w32.astype(bf16),
        "ab_w32": ab_w32,
        "ab_wo_h32": wo32[:ML_HEADS * ML_DV],
        "ab_wo_a32": wo32[ML_HEADS * ML_DV:],
        "ab_bias": _pad_lanes(jnp.concatenate([ab_b_i[0], ab_b_f[0]])),
        "ab_norm": ab_norm[0].reshape(1, ML_HEADS * ML_DV),
        "ab_sinks": _pad_lanes(ab_sinks[0]),
        "ab_wo_h": wo[:ML_HEADS * ML_DV],
        "ab_wo_a": wo[ML_HEADS * ML_DV:],
        "c_w": c_w,
        "c_conv_w": c_conv_w[0],
        "c_alog": _pad_lanes(c_a_log[0], GD_V_HEADS),
        "c_dt": _pad_lanes(c_dt_bias[0], GD_V_HEADS),
        "c_norm": c_norm[0].reshape(1, GD_HD),
        "c_wo": c_w_out[0].astype(bf16),
        "ln_g": [[ln_g[i, j].reshape(1, D_MODEL) for j in range(2)] for i in range(DEPTH)],
        "ln_b": [[ln_b[i, j].reshape(1, D_MODEL) for j in range(2)] for i in range(DEPTH)],
        "router_wt": router_w.T,
        "router_b": router_b.reshape(N_EXPERTS, 1),
        "ex_gate": ex_gate.reshape(DEPTH * N_EXPERTS, D_MODEL, D_EXPERT),
        "ex_up": ex_up.reshape(DEPTH * N_EXPERTS, D_MODEL, D_EXPERT),
        "ex_down": ex_down.reshape(DEPTH * N_EXPERTS, D_EXPERT, D_MODEL),
    }
    t_p = x_prompt.shape[1]
    y_p, st_p = _trunk(x_prompt, jnp.arange(t_p, dtype=jnp.int32), CHUNK, None, p)
    t_s = x_sample.shape[1]
    state = {"sw_k": cache_swa_k[0], "sw_v": cache_swa_v[0], "ml_C": state_mlstm_C[0], "ml_n": state_mlstm_n[0],
             "ml_m": state_mlstm_m[0], "gd_S": state_gdn_S[0], "gd_conv": state_gdn_conv[0]}
    y_s, st_s = _trunk(x_sample, PAST_LEN + jnp.arange(t_s, dtype=jnp.int32), t_s, state, p)
    return (y_p, y_s) + st_p + st_s
```

```python
import functools
import math

import jax
import jax.numpy as jnp
import numpy as np
from jax import lax
from jax.experimental import pallas as pl
from jax.experimental.pallas import tpu as pltpu

f32 = jnp.float32
bf16 = jnp.bfloat16
HIGHEST = lax.Precision.HIGHEST

D_MODEL = 1024
DEPTH = 2
CHUNK = 64
PAST_LEN = 2048
ML_HEADS = 8
ML_DK = 64
ML_DV = 64
SW_HEADS = 8
SW_KV_HEADS = 2
SW_HD = 64
SW_GROUP = SW_HEADS // SW_KV_HEADS
WINDOW = 128
ROPE_THETA = 10000.0
GD_QK_HEADS = 8
GD_V_HEADS = 16
GD_HD = 128
GD_CONV = 4
GD_QK_W = GD_QK_HEADS * GD_HD
GD_V_W = GD_V_HEADS * GD_HD
GD_CONV_CH = 2 * GD_QK_W + GD_V_W
N_EXPERTS = 16
N_GROUPS = 4
EXP_PER_GROUP = 4
D_EXPERT = 512
DN_ALPHA = (2 * DEPTH) ** 0.25
LN_EPS = 1e-5
RMS_EPS = 1e-6

LANES = 128
ML_W = 4 * ML_HEADS * ML_DK
SW_W = SW_HEADS * SW_HD + 2 * SW_KV_HEADS * SW_HD
VMEM_LIMIT = 56 * 1024 * 1024

SDS = jax.ShapeDtypeStruct


def _cparams(*sem):
    return pltpu.CompilerParams(dimension_semantics=sem, vmem_limit_bytes=VMEM_LIMIT)


def _dot(a, b):
    return jnp.dot(a, b, preferred_element_type=f32, precision=HIGHEST)


def _dot_nt(a, b):
    return lax.dot_general(a, b, (((1,), (1,)), ((), ())), preferred_element_type=f32, precision=HIGHEST)


def _bdot(a, b):
    return jnp.dot(a.astype(bf16), b.astype(bf16), preferred_element_type=f32)


def _bdot_nt(a, b):
    return lax.dot_general(a.astype(bf16), b.astype(bf16), (((1,), (1,)), ((), ())), preferred_element_type=f32)


def _bdot_tn(a, b):
    return jnp.dot(a.T.astype(bf16), b.astype(bf16), preferred_element_type=f32)


def _dots(precise):
    if precise:
        return _dot, _dot_nt, lambda a, b: _dot(a.T, b)
    return _bdot, _bdot_nt, _bdot_tn


def _sigmoid(x):
    return 0.5 + 0.5 * jnp.tanh(0.5 * x)


def _silu(x):
    hx = 0.5 * x
    return hx + hx * jnp.tanh(hx)


def _softplus(x):
    return jnp.maximum(x, 0.0) + jnp.log(1.0 + jnp.exp(-jnp.abs(x)))


def _layer_norm(v, g, b):
    mu = jnp.mean(v, axis=-1, keepdims=True)
    d = v - mu
    var = jnp.mean(d * d, axis=-1, keepdims=True)
    return d * lax.rsqrt(var + LN_EPS) * g + b


def _proj_kernel(x_ref, w_ref, *o_refs, splits, col_chunk, precise):
    mm = _dots(precise)[0]
    xb = x_ref[...] if precise else x_ref[...].astype(bf16)
    for o_ref, (start, width) in zip(o_refs, splits):
        for c in range(0, width, col_chunk):
            cw = min(col_chunk, width - c)
            o_ref[:, c:c + cw] = mm(xb, w_ref[:, start + c:start + c + cw])


def _proj(x2, w, splits, tm):
    n, k = x2.shape
    return pl.pallas_call(
        functools.partial(_proj_kernel, splits=splits, col_chunk=512, precise=w.dtype == f32),
        grid=(n // tm,),
        in_specs=[pl.BlockSpec((tm, k), lambda i: (i, 0)),
                  pl.BlockSpec(w.shape, lambda i: (0, 0), pipeline_mode=pl.Buffered(1))],
        out_specs=[pl.BlockSpec((tm, wd), lambda i: (i, 0)) for _, wd in splits],
        out_shape=[SDS((n, wd), f32) for _, wd in splits],
        compiler_params=_cparams("parallel"),
        name="in_proj",
    )(x2, w)


def _mlstm_kernel(z_ref, g_ref, bias_ref, nw_ref, cn0_ref, m0_ref, h_ref, cn_ref, m_ref, *, L, BB, precise):
    @pl.when(pl.program_id(1) == 0)
    def _():
        cn_ref[...] = cn0_ref[...]
        m_ref[...] = m0_ref[...]

    mm, mm_nt, mm_tn = _dots(precise)
    row = lax.broadcasted_iota(jnp.int32, (L, L), 0)
    col = lax.broadcasted_iota(jnp.int32, (L, L), 1)
    causal = row >= col
    tri = causal.astype(f32)
    lane = lax.broadcasted_iota(jnp.int32, (1, LANES), 1)
    lane_l = lax.broadcasted_iota(jnp.int32, (L, ML_DV), 1)
    one_hot0 = (lane_l == 0).astype(f32)
    for bi in range(BB):
        g = g_ref[bi] + bias_ref[...]
        lf = jnp.minimum(g, 0.0) - jnp.log(1.0 + jnp.exp(-jnp.abs(g)))
        bcum = _dot(tri, lf)
        b_t = bcum.T
        g_t = g.T
        m_row = m_ref[bi]
        new_m = m_row
        outs = []
        for h in range(ML_HEADS):
            b_col = bcum[:, ML_HEADS + h:ML_HEADS + h + 1]
            b_row = b_t[ML_HEADS + h:ML_HEADS + h + 1, :]
            ig_row = g_t[h:h + 1, :]
            ig_col = g[:, h:h + 1]
            m_h = m_row[:, h:h + 1]
            dmat = jnp.where(causal, b_col - b_row + ig_row, -jnp.inf)
            inter = b_col + m_h
            mt = jnp.maximum(inter, jnp.max(dmat, axis=-1, keepdims=True))
            a = jnp.exp(inter - mt)
            q = z_ref[bi, :, h * ML_DK:(h + 1) * ML_DK]
            k = z_ref[bi, :, ML_HEADS * ML_DK + h * ML_DK:ML_HEADS * ML_DK + (h + 1) * ML_DK] * (ML_DK ** -0.5)
            v = z_ref[bi, :, 2 * ML_HEADS * ML_DK + h * ML_DV:2 * ML_HEADS * ML_DK + (h + 1) * ML_DV]
            og = z_ref[bi, :, 3 * ML_HEADS * ML_DK + h * ML_DV:3 * ML_HEADS * ML_DK + (h + 1) * ML_DV]
            s = mm_nt(q, k) * jnp.exp(dmat - mt)
            vext = jnp.concatenate([v, one_hot0], axis=-1)
            cn = cn_ref[bi, h]
            tot = a * mm(q, cn) + mm(s, vext)
            num = tot[:, :ML_DV]
            den = tot[:, ML_DV:ML_DV + 1]
            hh = num / jnp.maximum(jnp.abs(den), jnp.exp(-mt))
            hh = hh * lax.rsqrt(jnp.mean(hh * hh, axis=-1, keepdims=True) + RMS_EPS) * nw_ref[:, h * ML_DV:(h + 1) * ML_DV]
            outs.append(hh * _sigmoid(og))
            m_new = mt[L - 1:L, :]
            b_last = b_col[L - 1:L, :]
            wk = jnp.exp(b_last - b_col + ig_col - m_new)
            dec = jnp.exp(b_last + m_h - m_new)
            cn_ref[bi, h] = dec * cn + mm_tn(k, wk * vext)
            new_m = jnp.where(lane == h, m_new, new_m)
        m_ref[bi] = new_m
        h_ref[bi] = jnp.concatenate(outs, axis=-1)


def _mlstm(z_ml, z_g, bias_row, norm_row, cn0, m0, B, T, L, precise):
    nc = T // L
    bb = min(B, 4)
    tok = lambda b, c: (b, c, 0)
    st4 = lambda b, c: (b, 0, 0, 0)
    st3 = lambda b, c: (b, 0, 0)
    return pl.pallas_call(
        functools.partial(_mlstm_kernel, L=L, BB=bb, precise=precise),
        grid=(B // bb, nc),
        in_specs=[pl.BlockSpec((bb, L, ML_W), tok),
                  pl.BlockSpec((bb, L, LANES), tok),
                  pl.BlockSpec((1, LANES), lambda b, c: (0, 0)),
                  pl.BlockSpec((1, ML_HEADS * ML_DV), lambda b, c: (0, 0)),
                  pl.BlockSpec((bb, ML_HEADS, ML_DK, LANES), st4),
                  pl.BlockSpec((bb, 1, LANES), st3)],
        out_specs=[pl.BlockSpec((bb, L, ML_HEADS * ML_DV), tok),
                   pl.BlockSpec((bb, ML_HEADS, ML_DK, LANES), st4),
                   pl.BlockSpec((bb, 1, LANES), st3)],
        out_shape=[SDS((B, T, ML_HEADS * ML_DV), f32),
                   SDS((B, ML_HEADS, ML_DK, LANES), f32),
                   SDS((B, 1, LANES), f32)],
        compiler_params=_cparams("parallel", "arbitrary"),
        name="mlstm",
    )(z_ml.reshape(B, T, ML_W), z_g.reshape(B, T, LANES), bias_row, norm_row, cn0, m0)


ML_PAIRS = ML_HEADS // 2
ML_REP_QUANTS = 3


def _pair_select_matrix():
    sel = np.zeros((LANES, ML_REP_QUANTS * ML_PAIRS * LANES), np.float32)
    for qn in range(ML_REP_QUANTS):
        for pr in range(ML_PAIRS):
            for half in range(2):
                lo = (qn * ML_PAIRS + pr) * LANES + half * ML_DV
                sel[ML_HEADS * qn + 2 * pr + half, lo:lo + ML_DV] = 1.0
    return jnp.asarray(sel, bf16)


def _exact_select(x, sel):
    hi = x.astype(bf16)
    r1 = x - hi.astype(f32)
    mid = r1.astype(bf16)
    lo = (r1 - mid.astype(f32)).astype(bf16)
    mm = lambda t: jnp.dot(t, sel, preferred_element_type=f32)
    return (mm(hi) + mm(mid)) + mm(lo)


def _mlstm_pair_kernel(z_ref, g_ref, bias_ref, nw_ref, sel_ref, cbd0_ref, nbd0_ref, m0_ref,
                       h_ref, cbd_ref, nbd_ref, m_ref, *, L, BB):
    @pl.when(pl.program_id(1) == 0)
    def _():
        cbd_ref[...] = cbd0_ref[...]
        nbd_ref[...] = nbd0_ref[...]
        m_ref[...] = m0_ref[...]

    tri = (lax.broadcasted_iota(jnp.int32, (L, L), 0) >= lax.broadcasted_iota(jnp.int32, (L, L), 1)).astype(f32)
    row_t = lax.broadcasted_iota(jnp.int32, (L, LANES), 0)
    lane_t = lax.broadcasted_iota(jnp.int32, (L, LANES), 1)
    first_half = lane_t < ML_DV
    causal2 = row_t >= (lane_t % ML_DV)
    rr = lax.broadcasted_iota(jnp.int32, (LANES, LANES), 0)
    cc = lax.broadcasted_iota(jnp.int32, (LANES, LANES), 1)
    same_block = (rr < ML_DV) == (cc < ML_DV)
    ones_bd = same_block.astype(bf16)
    lane_1 = lax.broadcasted_iota(jnp.int32, (1, LANES), 1)
    sel = sel_ref[...]
    n_tiles = ML_REP_QUANTS * ML_PAIRS

    gs = [g_ref[bi] + bias_ref[...] for bi in range(BB)]
    lfs = [jnp.minimum(g, 0.0) - jnp.log(1.0 + jnp.exp(-jnp.abs(g))) for g in gs]
    bc_all = _dot(tri, jnp.concatenate(lfs, axis=1))
    g_ts = [g.T for g in gs]
    b_ts = [bc_all[:, bi * LANES:(bi + 1) * LANES].T for bi in range(BB)]
    r_rows = [g_ts[bi][0:ML_HEADS] - b_ts[bi][ML_HEADS:2 * ML_HEADS] for bi in range(BB)]
    cm = jnp.concatenate([jnp.concatenate(r_rows, axis=0), jnp.full((BB * ML_HEADS, LANES - L), -jnp.inf, f32)], axis=1)
    shift = 1
    while shift < L:
        cm = jnp.maximum(cm, pltpu.roll(cm, shift, 1))
        shift *= 2
    cols = [jnp.concatenate([g_ts[bi][0:ML_HEADS], b_ts[bi][ML_HEADS:2 * ML_HEADS],
                             cm[bi * ML_HEADS:(bi + 1) * ML_HEADS, :L],
                             jnp.zeros((LANES - 3 * ML_HEADS, L), f32)], axis=0).T for bi in range(BB)]
    rep_all = _exact_select(jnp.concatenate(cols, axis=0), sel)
    rep = [rep_all[bi * L:(bi + 1) * L] for bi in range(BB)]
    m_all = jnp.concatenate([m_ref[bi] for bi in range(BB)] + [jnp.zeros((8 - BB, LANES), f32)], axis=0)
    m_rep_all = _exact_select(m_all, sel[:, :ML_PAIRS * LANES])
    m_rep = [m_rep_all[bi:bi + 1] for bi in range(BB)]

    units = [(bi, pr) for bi in range(BB) for pr in range(ML_PAIRS)]
    st = {}
    for u in units:
        bi, pr = u
        tile = lambda qn: rep[bi][:, (qn * ML_PAIRS + pr) * LANES:(qn * ML_PAIRS + pr + 1) * LANES]
        ig_rep, b_rep, cm_rep = tile(0), tile(1), tile(2)
        m_pair = m_rep[bi][:, pr * LANES:(pr + 1) * LANES]
        inter = b_rep + m_pair
        mt = jnp.maximum(inter, b_rep + cm_rep)
        r_row = jnp.concatenate([r_rows[bi][2 * pr:2 * pr + 1], r_rows[bi][2 * pr + 1:2 * pr + 2]], axis=1)
        e = jnp.exp(jnp.where(causal2, (b_rep - mt) + r_row, -jnp.inf))
        m_new = mt[L - 1:L]
        b_last = b_rep[L - 1:L]
        lo = pr * LANES
        q = z_ref[bi, :, lo:lo + LANES].astype(bf16)
        k = z_ref[bi, :, ML_HEADS * ML_DK + lo:ML_HEADS * ML_DK + lo + LANES] * (ML_DK ** -0.5)
        v = z_ref[bi, :, 2 * ML_HEADS * ML_DK + lo:2 * ML_HEADS * ML_DK + lo + LANES]
        kbd = jnp.concatenate([jnp.where(first_half, k, 0.0), jnp.where(first_half, 0.0, k)], axis=0).astype(bf16)
        vbd = jnp.concatenate([jnp.where(first_half, v, 0.0), jnp.where(first_half, 0.0, v)], axis=0).astype(bf16)
        wk = jnp.exp(b_last - b_rep + ig_rep - m_new)
        st[u] = dict(a=jnp.exp(inter - mt), em=jnp.exp(-mt), e=e, m_new=m_new, dec=jnp.exp(b_last + m_pair - m_new),
                     q=q, kbd=kbd, vbd=vbd, k_t=k.T.astype(bf16), wkv=(wk * v).astype(bf16), wk=wk.astype(bf16),
                     cbd=cbd_ref[bi, pr], nbd=nbd_ref[bi, pr])
    for u in units:
        d = st[u]
        d["qk"] = lax.dot_general(d["q"], d["kbd"], (((1,), (1,)), ((), ())), preferred_element_type=f32)
        d["qc"] = jnp.dot(d["q"], d["cbd"].astype(bf16), preferred_element_type=f32)
        d["qn"] = jnp.dot(d["q"], d["nbd"].astype(bf16), preferred_element_type=f32)
    for u in units:
        d = st[u]
        s = (d["qk"] * d["e"]).astype(bf16)
        num = d["a"] * d["qc"] + jnp.dot(s, d["vbd"], preferred_element_type=f32)
        den = d["a"] * d["qn"] + jnp.dot(s, ones_bd, preferred_element_type=f32)
        d["hh"] = num / jnp.maximum(jnp.abs(den), d["em"])
    for u in units:
        bi, pr = u
        d = st[u]
        sq = d["hh"] * d["hh"]
        sq_hi = sq.astype(bf16)
        sq_lo = (sq - sq_hi.astype(f32)).astype(bf16)
        ms = (jnp.dot(sq_hi, ones_bd, preferred_element_type=f32)
              + jnp.dot(sq_lo, ones_bd, preferred_element_type=f32)) * (1.0 / ML_DV)
        lo = pr * LANES
        og = z_ref[bi, :, 3 * ML_HEADS * ML_DK + lo:3 * ML_HEADS * ML_DK + lo + LANES]
        h_ref[bi, :, lo:lo + LANES] = d["hh"] * lax.rsqrt(ms + RMS_EPS) * nw_ref[:, lo:lo + LANES] * _sigmoid(og)
    for u in units:
        bi, pr = u
        d = st[u]
        cbd_ref[bi, pr] = d["dec"] * d["cbd"] + jnp.where(
            same_block, jnp.dot(d["k_t"], d["wkv"], preferred_element_type=f32), 0.0)
        nbd_ref[bi, pr] = d["dec"] * d["nbd"] + jnp.where(
            same_block, jnp.dot(d["k_t"], d["wk"], preferred_element_type=f32), 0.0)
    for bi in range(BB):
        new_m = m_ref[bi]
        for pr in range(ML_PAIRS):
            m_new = st[(bi, pr)]["m_new"]
            new_m = jnp.where(lane_1 == 2 * pr, m_new[:, 0:1], new_m)
            new_m = jnp.where(lane_1 == 2 * pr + 1, m_new[:, ML_DV:ML_DV + 1], new_m)
        m_ref[bi] = new_m


def _mlstm_pairs(z_ml, z_g, bias_row, norm_row, B, T, L):
    assert 2 * L == LANES and ML_DK == ML_DV == L
    nc = T // L
    bb = min(B, 4)
    tok = lambda b, c: (b, c, 0)
    st4 = lambda b, c: (b, 0, 0, 0)
    st3 = lambda b, c: (b, 0, 0)
    const = lambda b, c: (0, 0)
    sel = _pair_select_matrix()
    zeros_bd = jnp.zeros((B, ML_PAIRS, LANES, LANES), f32)
    h, cbd, nbd, m = pl.pallas_call(
        functools.partial(_mlstm_pair_kernel, L=L, BB=bb),
        grid=(B // bb, nc),
        in_specs=[pl.BlockSpec((bb, L, ML_W), tok),
                  pl.BlockSpec((bb, L, LANES), tok),
                  pl.BlockSpec((1, LANES), const),
                  pl.BlockSpec((1, ML_HEADS * ML_DV), const),
                  pl.BlockSpec(sel.shape, const),
                  pl.BlockSpec((bb, ML_PAIRS, LANES, LANES), st4),
                  pl.BlockSpec((bb, ML_PAIRS, LANES, LANES), st4),
                  pl.BlockSpec((bb, 1, LANES), st3)],
        out_specs=[pl.BlockSpec((bb, L, ML_HEADS * ML_DV), tok),
                   pl.BlockSpec((bb, ML_PAIRS, LANES, LANES), st4),
                   pl.BlockSpec((bb, ML_PAIRS, LANES, LANES), st4),
                   pl.BlockSpec((bb, 1, LANES), st3)],
        out_shape=[SDS((B, T, ML_HEADS * ML_DV), f32),
                   SDS((B, ML_PAIRS, LANES, LANES), f32),
                   SDS((B, ML_PAIRS, LANES, LANES), f32),
                   SDS((B, 1, LANES), f32)],
        compiler_params=_cparams("parallel", "arbitrary"),
        name="mlstm_pairs",
    )(z_ml.reshape(B, T, ML_W), z_g.reshape(B, T, LANES), bias_row, norm_row, sel, zeros_bd, zeros_bd,
      jnp.zeros((B, 1, LANES), f32))
    c_out = jnp.stack([cbd[:, :, :ML_DK, :ML_DV], cbd[:, :, ML_DK:, ML_DV:]], axis=2).reshape(B, ML_HEADS, ML_DK, ML_DV)
    n_out = jnp.stack([nbd[:, :, :ML_DK, 0], nbd[:, :, ML_DK:, ML_DV]], axis=2).reshape(B, ML_HEADS, ML_DK)
    return h, c_out, n_out, m[:, 0, :ML_HEADS]


def _rope(x, cos, sin_signed):
    w = x.shape[-1]
    lane = lax.broadcasted_iota(jnp.int32, x.shape, 1)
    swapped = jnp.where((lane % SW_HD) < SW_HD // 2, pltpu.roll(x, w - SW_HD // 2, 1), pltpu.roll(x, SW_HD // 2, 1))
    return x * cos + swapped * sin_signed


def _swa_attend(jobs, sinks_ref, L, precise=False):
    mm, mm_nt, _ = _dots(precise)
    units = [(j, g) for j in range(len(jobs)) for g in range(SW_KV_HEADS)]
    sinks = [jnp.concatenate(
        [jnp.broadcast_to(sinks_ref[:, g * SW_GROUP + i:g * SW_GROUP + i + 1], (L, 1)) for i in range(SW_GROUP)],
        axis=0) for g in range(SW_KV_HEADS)]
    s, p, sink_e = {}, {}, {}
    for j, g in units:
        qr, keys, _, _ = jobs[j]
        q4 = jnp.concatenate([qr[:, (g * SW_GROUP + i) * SW_HD:(g * SW_GROUP + i + 1) * SW_HD]
                              for i in range(SW_GROUP)], axis=0)
        s[j, g] = mm_nt(q4, keys[:, g * SW_HD:(g + 1) * SW_HD]) * (SW_HD ** -0.5)
    for j, g in units:
        first_valid = jobs[j][3]
        sc = s[j, g]
        if first_valid is not None:
            kcol = lax.broadcasted_iota(jnp.int32, (1, sc.shape[1]), 1)
            sc = jnp.where(kcol >= first_valid, sc, -jnp.inf)
        mx = jnp.maximum(jnp.max(sc, axis=-1, keepdims=True), sinks[g])
        p[j, g] = jnp.exp(sc - mx)
        sink_e[j, g] = jnp.exp(sinks[g] - mx)
    ones = jnp.ones((jobs[0][1].shape[0], SW_HD), f32)
    o = {u: mm(p[u], jobs[u[0]][2][:, u[1] * SW_HD:(u[1] + 1) * SW_HD]) / (mm(p[u], ones) + sink_e[u]) for u in units}
    return [jnp.concatenate([o[j, g][i * L:(i + 1) * L, :] for g in range(SW_KV_HEADS) for i in range(SW_GROUP)],
                            axis=-1) for j in range(len(jobs))]


def _swa_prompt_kernel(q_ref, kp_ref, kc_ref, vp_ref, vc_ref, cp_ref, cc_ref, sp_ref, sc_ref, sinks_ref,
                       o_ref, kr_ref, *, L, CB):
    i = pl.program_id(1)
    rows = CB * L
    back = 2 * L
    cos_q = jnp.concatenate([cc_ref[...]] * (SW_HEADS // SW_KV_HEADS), axis=-1)
    sin_q = jnp.concatenate([sc_ref[...]] * (SW_HEADS // SW_KV_HEADS), axis=-1)
    qr = _rope(q_ref[...], cos_q, sin_q)
    k_cur = _rope(kc_ref[...], cc_ref[...], sc_ref[...])
    kr_ref[...] = k_cur
    k_prev = _rope(kp_ref[rows - back:rows, :], cp_ref[rows - back:rows, :], sp_ref[rows - back:rows, :])
    keys = jnp.concatenate([k_prev, k_cur], axis=0)
    vals = jnp.concatenate([vp_ref[rows - back:rows, :], vc_ref[...]], axis=0)
    jobs = []
    for u in range(CB):
        first_valid = jnp.where(i == 0, back - u * L, 0) if u * L < back else None
        jobs.append((qr[u * L:(u + 1) * L], keys[u * L:(u + 3) * L], vals[u * L:(u + 3) * L], first_valid))
    o_ref[...] = jnp.concatenate(_swa_attend(jobs, sinks_ref, L), axis=0)


def _swa_prompt(z_sw, cos_t, sin_t, sinks_row, B, T, L):
    cb = 8
    rows = cb * L
    nb = T // rows
    n = B * T
    kcol = SW_HEADS * SW_HD // LANES
    vcol = kcol + 1
    cur = lambda b, i: (b * nb + i, 0)
    prev = lambda col: (lambda b, i: (b * nb + jnp.maximum(i - 1, 0), col))
    curc = lambda col: (lambda b, i: (b * nb + i, col))
    tab_cur = lambda b, i: (i, 0)
    tab_prev = lambda b, i: (jnp.maximum(i - 1, 0), 0)
    return pl.pallas_call(
        functools.partial(_swa_prompt_kernel, L=L, CB=cb),
        grid=(B, nb),
        in_specs=[pl.BlockSpec((rows, SW_HEADS * SW_HD), cur),
                  pl.BlockSpec((rows, LANES), prev(kcol)), pl.BlockSpec((rows, LANES), curc(kcol)),
                  pl.BlockSpec((rows, LANES), prev(vcol)), pl.BlockSpec((rows, LANES), curc(vcol)),
                  pl.BlockSpec((rows, LANES), tab_prev), pl.BlockSpec((rows, LANES), tab_cur),
                  pl.BlockSpec((rows, LANES), tab_prev), pl.BlockSpec((rows, LANES), tab_cur),
                  pl.BlockSpec((1, LANES), lambda b, i: (0, 0))],
        out_specs=[pl.BlockSpec((rows, SW_HEADS * SW_HD), cur),
                   pl.BlockSpec((rows, LANES), cur)],
        out_shape=[SDS((n, SW_HEADS * SW_HD), f32), SDS((n, LANES), f32)],
        compiler_params=_cparams("parallel", "parallel"),
        name="swa_prompt",
    )(z_sw, z_sw, z_sw, z_sw, z_sw, cos_t, cos_t, sin_t, sin_t, sinks_row)


def _swa_sample_kernel(q_ref, k_ref, v_ref, ck_ref, cv_ref, cos_ref, sin_ref, sinks_ref, o_ref, kr_ref, *, L):
    cos_q = jnp.concatenate([cos_ref[...]] * (SW_HEADS // SW_KV_HEADS), axis=-1)
    sin_q = jnp.concatenate([sin_ref[...]] * (SW_HEADS // SW_KV_HEADS), axis=-1)
    qr = _rope(q_ref[...], cos_q, sin_q)
    kr = _rope(k_ref[...], cos_ref[...], sin_ref[...])
    kr_ref[...] = kr
    keys = jnp.concatenate([ck_ref[0], kr], axis=0)
    vals = jnp.concatenate([cv_ref[0], v_ref[...]], axis=0)
    o_ref[...] = _swa_attend([(qr, keys, vals, None)], sinks_ref, L, precise=True)[0]


def _swa_sample(z_sw, cache_k, cache_v, cos_t, sin_t, sinks_row, B, T):
    n = B * T
    kcol = SW_HEADS * SW_HD // LANES
    return pl.pallas_call(
        functools.partial(_swa_sample_kernel, L=T),
        grid=(B,),
        in_specs=[pl.BlockSpec((T, SW_HEADS * SW_HD), lambda b: (b, 0)),
                  pl.BlockSpec((T, LANES), lambda b: (b, kcol)),
                  pl.BlockSpec((T, LANES), lambda b: (b, kcol + 1)),
                  pl.BlockSpec((1, WINDOW, LANES), lambda b: (b, 0, 0)),
                  pl.BlockSpec((1, WINDOW, LANES), lambda b: (b, 0, 0)),
                  pl.BlockSpec((T, LANES), lambda b: (0, 0)),
                  pl.BlockSpec((T, LANES), lambda b: (0, 0)),
                  pl.BlockSpec((1, LANES), lambda b: (0, 0))],
        out_specs=[pl.BlockSpec((T, SW_HEADS * SW_HD), lambda b: (b, 0)),
                   pl.BlockSpec((T, LANES), lambda b: (b, 0))],
        out_shape=[SDS((n, SW_HEADS * SW_HD), f32), SDS((n, LANES), f32)],
        compiler_params=_cparams("parallel"),
        name="swa_sample",
    )(z_sw, z_sw, z_sw, cache_k, cache_v, cos_t, sin_t, sinks_row)


def _out_ln_kernel(*refs, n_in):
    x_ref = refs[0]
    a_refs = refs[1:1 + n_in]
    w_refs = refs[1 + n_in:1 + 2 * n_in]
    g_ref, b_ref, rw_ref, rb_ref, o_ref, gates_ref, route_ref = refs[1 + 2 * n_in:]
    mm = _dots(w_refs[0].dtype == f32)[0]
    y = mm(a_refs[0][...], w_refs[0][...])
    for a_ref, w_ref in zip(a_refs[1:], w_refs[1:]):
        y = y + mm(a_ref[...], w_ref[...])
    x_new = _layer_norm(DN_ALPHA * x_ref[...] + y, g_ref[...], b_ref[...])
    o_ref[...] = x_new
    gates_ref[...], route_ref[...] = _route(x_new, rw_ref, rb_ref)


def _out_ln(x2, acts, ws, g_row, b_row, rw_t, rb_col, tm):
    n = x2.shape[0]
    row = lambda i: (i, 0)
    const = lambda i: (0, 0)
    col = lambda i: (0, i)
    return pl.pallas_call(
        functools.partial(_out_ln_kernel, n_in=len(acts)),
        grid=(n // tm,),
        in_specs=[pl.BlockSpec((tm, D_MODEL), row)]
        + [pl.BlockSpec((tm, a.shape[1]), row) for a in acts]
        + [pl.BlockSpec(w.shape, const) for w in ws]
        + [pl.BlockSpec((1, D_MODEL), const), pl.BlockSpec((1, D_MODEL), const),
           pl.BlockSpec((N_EXPERTS, D_MODEL), const), pl.BlockSpec((N_EXPERTS, 1), const)],
        out_specs=[pl.BlockSpec((tm, D_MODEL), row), pl.BlockSpec((N_EXPERTS, tm), col), pl.BlockSpec((8, tm), col)],
        out_shape=[SDS((n, D_MODEL), f32), SDS((N_EXPERTS, n), f32), SDS((8, n), f32)],
        compiler_params=_cparams("parallel"),
        name="out_proj_ln",
    )(x2, *acts, *ws, g_row, b_row, rw_t, rb_col)


def _logistic(x):
    return 1.0 / (1.0 + jnp.exp(-x))


def _route(x, rw_ref, rb_ref):
    logits = _dot_nt(rw_ref[...], x)
    aff = _logistic(logits)
    sc = aff + rb_ref[...]
    s = [sc[e:e + 1, :] for e in range(N_EXPERTS)]
    a = [aff[e:e + 1, :] for e in range(N_EXPERTS)]
    scores = []
    for gi in range(N_GROUPS):
        w, x, y, z = s[4 * gi:4 * gi + 4]
        p, q = jnp.maximum(w, x), jnp.minimum(w, x)
        r, t = jnp.maximum(y, z), jnp.minimum(y, z)
        scores.append(jnp.maximum(p, r) + jnp.maximum(jnp.minimum(p, r), jnp.maximum(q, t)))
    best = scores[0]
    gsel = jnp.zeros_like(best, dtype=jnp.int32)
    for gi in range(1, N_GROUPS):
        better = scores[gi] > best
        best = jnp.where(better, scores[gi], best)
        gsel = jnp.where(better, gi, gsel)
    sel = []
    for e in range(N_EXPERTS):
        gi, i = divmod(e, EXP_PER_GROUP)
        beaten = jnp.zeros_like(gsel)
        for j in range(EXP_PER_GROUP):
            if j == i:
                continue
            o = s[4 * gi + j]
            wins = (o >= s[e]) if j < i else (o > s[e])
            beaten = beaten + wins.astype(jnp.int32)
        sel.append((gsel == gi) & (beaten < 2))
    den = jnp.zeros_like(best)
    for e in range(N_EXPERTS):
        den = den + jnp.where(sel[e], a[e], 0.0)
    gate = [jnp.where(sel[e], a[e] / den, 0.0) for e in range(N_EXPERTS)]
    taken = jnp.zeros_like(gsel)
    ea = eb = wa = wb = jnp.zeros_like(best)
    for e in range(N_EXPERTS):
        first = sel[e] & (taken == 0)
        second = sel[e] & (taken == 1)
        ea = jnp.where(first, float(e), ea)
        wa = jnp.where(first, gate[e], wa)
        eb = jnp.where(second, float(e), eb)
        wb = jnp.where(second, gate[e], wb)
        taken = taken + sel[e].astype(jnp.int32)
    route = jnp.concatenate([ea, eb, wa, wb, jnp.zeros((4, ea.shape[1]), f32)], axis=0)
    return jnp.concatenate(gate, axis=0), route


def _moe_kernel(x_ref, gates_ref, wg_ref, wu_ref, wd_ref, g_ref, b_ref, o_ref, xb_ref, acc_ref):
    e = pl.program_id(1)

    @pl.when(e == 0)
    def _():
        xb_ref[...] = x_ref[...].astype(bf16)
        acc_ref[...] = jnp.zeros_like(acc_ref)

    xb = xb_ref[...]
    lane = lax.broadcasted_iota(jnp.int32, gates_ref.shape, 1)
    gcol = jnp.sum(jnp.where(lane == e, gates_ref[...], 0.0), axis=-1, keepdims=True)
    h = _silu(_bdot(xb, wg_ref[0])) * _bdot(xb, wu_ref[0])
    acc_ref[...] += _bdot(gcol * h, wd_ref[0])

    @pl.when(e == N_EXPERTS - 1)
    def _():
        o_ref[...] = _layer_norm(DN_ALPHA * x_ref[...] + acc_ref[...], g_ref[...], b_ref[...])


def _moe_ln(x2, gates, wg, wu, wd, base, g_row, b_row, tm):
    n = x2.shape[0]
    row = lambda i, e: (i, 0)
    const = lambda i, e: (0, 0)
    expert = lambda i, e: (base + e, 0, 0)
    return pl.pallas_call(
        _moe_kernel,
        grid=(n // tm, N_EXPERTS),
        in_specs=[pl.BlockSpec((tm, D_MODEL), row),
                  pl.BlockSpec((tm, N_EXPERTS), row),
                  pl.BlockSpec((1, D_MODEL, D_EXPERT), expert),
                  pl.BlockSpec((1, D_MODEL, D_EXPERT), expert),
                  pl.BlockSpec((1, D_EXPERT, D_MODEL), expert),
                  pl.BlockSpec((1, D_MODEL), const), pl.BlockSpec((1, D_MODEL), const)],
        out_specs=pl.BlockSpec((tm, D_MODEL), row),
        out_shape=SDS((n, D_MODEL), f32),
        scratch_shapes=[pltpu.VMEM((tm, D_MODEL), bf16), pltpu.VMEM((tm, D_MODEL), f32)],
        compiler_params=_cparams("parallel", "arbitrary"),
        name="moe_ln",
    )(x2, gates, wg, wu, wd, g_row, b_row)


N_PAIRS = N_GROUPS * (EXP_PER_GROUP * (EXP_PER_GROUP - 1) // 2)
MOE_TM = 256
MOE_DMA_ROWS = 512
_PAIR_A = [g * EXP_PER_GROUP + a for g in range(N_GROUPS) for a in range(EXP_PER_GROUP) for b in range(a + 1, EXP_PER_GROUP)]
_PAIR_B = [g * EXP_PER_GROUP + b for g in range(N_GROUPS) for a in range(EXP_PER_GROUP) for b in range(a + 1, EXP_PER_GROUP)]


def _gather_rows_kernel(idx_ref, src_ref, o_ref, sem, *, rows):
    base = pl.program_id(0) * rows

    def row_copy(j):
        return pltpu.make_async_copy(src_ref.at[pl.ds(idx_ref[base + j], 1)], o_ref.at[pl.ds(j, 1)], sem)

    def issue(j, carry):
        row_copy(j).start()
        return carry

    def drain(j, carry):
        row_copy(j).wait()
        return carry

    lax.fori_loop(0, rows, issue, 0, unroll=8)
    lax.fori_loop(0, rows, drain, 0, unroll=8)


def _gather_rows(src, idx, rows):
    n_out = idx.shape[0]
    d = src.shape[1]
    return pl.pallas_call(
        functools.partial(_gather_rows_kernel, rows=rows),
        grid_spec=pltpu.PrefetchScalarGridSpec(
            num_scalar_prefetch=1,
            grid=(n_out // rows,),
            in_specs=[pl.BlockSpec(memory_space=pl.ANY)],
            out_specs=pl.BlockSpec((rows, d), lambda i, idx_ref: (i, 0)),
            scratch_shapes=[pltpu.SemaphoreType.DMA(())]),
        out_shape=SDS((n_out, d), f32),
        compiler_params=_cparams("arbitrary"),
        name="gather_rows",
    )(idx, src)


def _scatter_rows_kernel(idx_ref, src_ref, init_ref, o_ref, sem, *, rows):
    del init_ref
    base = pl.program_id(0) * rows

    def row_copy(j):
        return pltpu.make_async_copy(src_ref.at[pl.ds(j, 1)], o_ref.at[pl.ds(idx_ref[base + j], 1)], sem)

    def issue(j, carry):
        row_copy(j).start()
        return carry

    def drain(j, carry):
        row_copy(j).wait()
        return carry

    lax.fori_loop(0, rows, issue, 0, unroll=8)
    lax.fori_loop(0, rows, drain, 0, unroll=8)


def _scatter_rows(src, idx, n_out, rows, init=None):
    n_src, d = src.shape
    if init is None:
        init = jnp.zeros((n_out, d), f32)
    return pl.pallas_call(
        functools.partial(_scatter_rows_kernel, rows=rows),
        grid_spec=pltpu.PrefetchScalarGridSpec(
            num_scalar_prefetch=1,
            grid=(n_src // rows,),
            in_specs=[pl.BlockSpec((rows, d), lambda i, idx_ref: (i, 0)), pl.BlockSpec(memory_space=pl.ANY)],
            out_specs=pl.BlockSpec(memory_space=pl.ANY),
            scratch_shapes=[pltpu.SemaphoreType.DMA(())]),
        out_shape=SDS((n_out, d), f32),
        input_output_aliases={2: 0},
        compiler_params=_cparams("arbitrary"),
        name="scatter_rows",
    )(idx, src, init)


def _pair_plan(route, n, tm):
    ea = route[0].astype(jnp.int32)
    eb = route[1].astype(jnp.int32)
    a = ea % EXP_PER_GROUP
    b = eb % EXP_PER_GROUP
    pidx = jnp.where(a == 0, b - 1, jnp.where(a == 1, b + 1, 5))
    pair = (ea // EXP_PER_GROUP) * (N_PAIRS // N_GROUPS) + pidx
    onehot = (pair[:, None] == jnp.arange(N_PAIRS, dtype=jnp.int32)[None, :]).astype(jnp.int32)
    csum = jnp.cumsum(onehot, axis=0)
    counts = csum[-1]
    ntiles = (counts + tm - 1) // tm
    tile_end = jnp.cumsum(ntiles)
    tile_start = tile_end - ntiles
    row_of_token = jnp.sum(onehot * (csum - 1 + (tile_start * tm)[None, :]), axis=1)
    nt = n // tm + N_PAIRS
    tile_id = jnp.arange(nt, dtype=jnp.int32)
    tile_valid = tile_id < tile_end[-1]
    tile_pair = jnp.sum((tile_end[None, :] <= jnp.minimum(tile_id, tile_end[-1] - 1)[:, None]).astype(jnp.int32), axis=1)
    tile_pair = jnp.minimum(tile_pair, N_PAIRS - 1)
    pick = (tile_pair[:, None] == jnp.arange(N_PAIRS, dtype=jnp.int32)[None, :]).astype(jnp.int32)
    tile_a = jnp.sum(pick * jnp.asarray(_PAIR_A, jnp.int32)[None, :], axis=1)
    tile_b = jnp.sum(pick * jnp.asarray(_PAIR_B, jnp.int32)[None, :], axis=1)
    return row_of_token, tile_a, tile_b, tile_valid.astype(jnp.int32)


def _pair_expert_kernel(ta_ref, tb_ref, tv_ref, x_ref, rwt_ref, wga_ref, wua_ref, wda_ref, wgb_ref, wub_ref, wdb_ref,
                        g_ref, b_ref, o_ref):
    i = pl.program_id(0)
    valid = tv_ref[i] == 1

    @pl.when(valid)
    def _():
        x = x_ref[...]
        xb = x.astype(bf16)
        aff_a = _logistic(jnp.sum(x * rwt_ref[pl.ds(ta_ref[i], 1), :], axis=-1, keepdims=True))
        aff_b = _logistic(jnp.sum(x * rwt_ref[pl.ds(tb_ref[i], 1), :], axis=-1, keepdims=True))
        den = aff_a + aff_b
        acc = None
        for w, (wg, wu, wd) in ((aff_a / den, (wga_ref, wua_ref, wda_ref)), (aff_b / den, (wgb_ref, wub_ref, wdb_ref))):
            h = _silu(_bdot(xb, wg[0])) * _bdot(xb, wu[0])
            y = _bdot(w * h, wd[0])
            acc = y if acc is None else acc + y
        o_ref[...] = _layer_norm(DN_ALPHA * x + acc, g_ref[...], b_ref[...])

    @pl.when(jnp.logical_not(valid))
    def _():
        o_ref[...] = jnp.zeros_like(o_ref)


def _pair_experts(xs, rw_t, tile_a, tile_b, tile_valid, wg, wu, wd, base, g_row, b_row, tm):
    rows = xs.shape[0]
    row = lambda i, ta, tb, tv: (i, 0)
    const = lambda i, ta, tb, tv: (0, 0)
    ex_a = lambda i, ta, tb, tv: (base + ta[i], 0, 0)
    ex_b = lambda i, ta, tb, tv: (base + tb[i], 0, 0)
    up = pl.BlockSpec((1, D_MODEL, D_EXPERT), ex_a), pl.BlockSpec((1, D_MODEL, D_EXPERT), ex_b)
    down = pl.BlockSpec((1, D_EXPERT, D_MODEL), ex_a), pl.BlockSpec((1, D_EXPERT, D_MODEL), ex_b)
    return pl.pallas_call(
        _pair_expert_kernel,
        grid_spec=pltpu.PrefetchScalarGridSpec(
            num_scalar_prefetch=3,
            grid=(rows // tm,),
            in_specs=[pl.BlockSpec((tm, D_MODEL), row), pl.BlockSpec((N_EXPERTS, D_MODEL), const),
                      up[0], up[0], down[0], up[1], up[1], down[1],
                      pl.BlockSpec((1, D_MODEL), const), pl.BlockSpec((1, D_MODEL), const)],
            out_specs=pl.BlockSpec((tm, D_MODEL), row)),
        out_shape=SDS((rows, D_MODEL), f32),
        compiler_params=_cparams("arbitrary"),
        name="pair_experts",
    )(tile_a, tile_b, tile_valid, xs, rw_t, wg, wu, wd, wg, wu, wd, g_row, b_row)


def _unit_lower_inverses(mats, L):
    row = lax.broadcasted_iota(jnp.int32, (L, L), 0)
    col = lax.broadcasted_iota(jnp.int32, (L, L), 1)
    eye = (row == col).astype(f32)
    ps = [eye - a for a in mats]
    pws = [a.astype(bf16) for a in mats]
    span = 2
    while span < L:
        pws = [jnp.dot(pw, pw, preferred_element_type=f32).astype(bf16) for pw in pws]
        ps = [p + jnp.dot(pw, p.astype(bf16), preferred_element_type=f32) for p, pw in zip(ps, pws)]
        span *= 2
    return ps


def _gdn_kernel(x_ref, zg_ref, ba_ref, cw_ref, alog_ref, dt_ref, nw_ref, s0_ref, cb_ref,
                o_ref, s_ref, prev_ref, *, L, BB):
    @pl.when(pl.program_id(1) == 0)
    def _():
        s_ref[...] = s0_ref[...]
        prev_ref[...] = cb_ref[...]

    def conv_silu(bi, lo, width):
        cur = x_ref[bi, :, lo:lo + width]
        cat =jnp.concatenate([prev_ref[bi, :, lo:lo + width], cur], axis=0)
        acc = cat[5:5 + L] * cw_ref[0:1, lo:lo + width]
        acc = acc + cat[6:6 + L] * cw_ref[1:2, lo:lo + width]
        acc = acc + cat[7:7 + L] * cw_ref[2:3, lo:lo + width]
        acc = acc + cur * cw_ref[3:4, lo:lo + width]
        return _silu(acc)

    def l2n(v, scale):
        return v * lax.rsqrt(jnp.sum(v * v, axis=-1, keepdims=True) + 1e-6) * scale

    row = lax.broadcasted_iota(jnp.int32, (L, L), 0)
    col = lax.broadcasted_iota(jnp.int32, (L, L), 1)
    incl = row >= col
    strict = row > col
    rep = GD_V_HEADS // GD_QK_HEADS
    bas = [ba_ref[bi] for bi in range(BB)]
    betas = [_sigmoid(ba) for ba in bas]
    gls = [-jnp.exp(alog_ref[...]) * _softplus(ba + dt_ref[...]) for ba in bas]
    gcum_all = _dot(incl.astype(f32), jnp.concatenate(gls, axis=1))
    gcums = [gcum_all[:, bi * LANES:(bi + 1) * LANES] for bi in range(BB)]
    gcum_ts = [g.T for g in gcums]
    units = [(bi, hv) for bi in range(BB) for hv in range(GD_V_HEADS)]
    qs, ks, amats, qkds, egs, g_cols, rhss = {}, {}, [], {}, {}, {}, []
    for bi in range(BB):
        for j in range(GD_QK_HEADS):
            q = l2n(conv_silu(bi, j * GD_HD, GD_HD), GD_HD ** -0.5)
            k = l2n(conv_silu(bi, GD_QK_W + j * GD_HD, GD_HD), 1.0)
            qb, kb = q.astype(bf16), k.astype(bf16)
            kk = lax.dot_general(kb, kb, (((1,), (1,)), ((), ())), preferred_element_type=f32)
            qk = lax.dot_general(qb, kb, (((1,), (1,)), ((), ())), preferred_element_type=f32)
            qs[bi, j] = qb
            ks[bi, j] = k
            for r in range(rep):
                hv = j * rep + r
                v = conv_silu(bi, 2 * GD_QK_W + hv * GD_HD, GD_HD)
                g_col = gcums[bi][:, GD_V_HEADS + hv:GD_V_HEADS + hv + 1]
                g_row = gcum_ts[bi][GD_V_HEADS + hv:GD_V_HEADS + hv + 1, :]
                b_col = betas[bi][:, hv:hv + 1]
                decay = jnp.exp(jnp.where(incl, g_col - g_row, -jnp.inf))
                eg = jnp.exp(g_col)
                amats.append(jnp.where(strict, b_col * kk * decay, 0.0))
                qkds[bi, hv] = (qk * decay).astype(bf16)
                egs[bi, hv] = eg
                g_cols[bi, hv] = g_col
                rhss.append(((b_col * v).astype(bf16), ((b_col * eg) * k).astype(bf16)))
    tinvs = _unit_lower_inverses(amats, L)
    tinvs = [t.astype(bf16) for t in tinvs]
    sol_v = dict(zip(units, [jnp.dot(t, r[0], preferred_element_type=f32) for t, r in zip(tinvs, rhss)]))
    sol_k = dict(zip(units, [jnp.dot(t, r[1], preferred_element_type=f32) for t, r in zip(tinvs, rhss)]))
    sts, wks, qss, kts = {}, {}, {}, {}
    for u in units:
        bi, hv = u
        j = hv // rep
        sts[u] = s_ref[bi, hv]
        stb = sts[u].astype(bf16)
        wks[u] = jnp.dot(sol_k[u].astype(bf16), stb, preferred_element_type=f32)
        qss[u] = jnp.dot(qs[bi, j], stb, preferred_element_type=f32)
        g_col = g_cols[u]
        kts[u] = (jnp.exp(g_col[L - 1:L, :] - g_col) * ks[bi, j]).T.astype(bf16)
    outs = {}
    for u in units:
        bi, hv = u
        wnb = (sol_v[u] - wks[u]).astype(bf16)
        outs[u] = egs[u] * qss[u] + jnp.dot(qkds[u], wnb, preferred_element_type=f32)
        s_ref[bi, hv] = jnp.exp(g_cols[u][L - 1:L, :]) * sts[u] + jnp.dot(kts[u], wnb, preferred_element_type=f32)
    for u in units:
        bi, hv = u
        o = outs[u]
        o = o * lax.rsqrt(jnp.mean(o * o, axis=-1, keepdims=True) + RMS_EPS) * nw_ref[...]
        o_ref[bi, :, hv * GD_HD:(hv + 1) * GD_HD] = o * _silu(zg_ref[bi, :, hv * GD_HD:(hv + 1) * GD_HD])
    for bi in range(BB):
        prev_ref[bi] = x_ref[bi, L - 8:L, :]


def _gdn(qkv, zg, ba, conv_w, alog_row, dt_row, norm_row, s0, conv8, B, T, L):
    nc = T // L
    n = B * T
    bb = 2
    tok = lambda b, c: (b, c, 0)
    const = lambda b, c: (0, 0)
    st4 = lambda b, c: (b, 0, 0, 0)
    o, s_out = pl.pallas_call(
        functools.partial(_gdn_kernel, L=L, BB=bb),
        grid=(B // bb, nc),
        in_specs=[pl.BlockSpec((bb, L, GD_CONV_CH), tok),
                  pl.BlockSpec((bb, L, GD_V_W), tok),
                  pl.BlockSpec((bb, L, LANES), tok),
                  pl.BlockSpec((GD_CONV, GD_CONV_CH), const),
                  pl.BlockSpec((1, LANES), const),
                  pl.BlockSpec((1, LANES), const),
                  pl.BlockSpec((1, GD_HD), const),
                  pl.BlockSpec((bb, GD_V_HEADS, GD_HD, GD_HD), st4),
                  pl.BlockSpec((bb, 8, GD_CONV_CH), lambda b, c: (b, 0, 0))],
        out_specs=[pl.BlockSpec((bb, L, GD_V_W), tok),
                   pl.BlockSpec((bb, GD_V_HEADS, GD_HD, GD_HD), st4)],
        out_shape=[SDS((B, T, GD_V_W), f32), SDS((B, GD_V_HEADS, GD_HD, GD_HD), f32)],
        scratch_shapes=[pltpu.VMEM((bb, 8, GD_CONV_CH), f32)],
        compiler_params=_cparams("parallel", "arbitrary"),
        name="gdn",
    )(qkv.reshape(B, T, GD_CONV_CH), zg.reshape(B, T, GD_V_W), ba.reshape(B, T, LANES), conv_w, alog_row, dt_row,
      norm_row, s0, conv8)
    return o.reshape(n, GD_V_W), s_out


def _pad_lanes(row, offset=0):
    return jnp.zeros((1, LANES), f32).at[0, offset:offset + row.shape[0]].set(row.astype(f32))


def _rope_tables(pos):
    half = SW_HD // 2
    inv = ROPE_THETA ** (-jnp.arange(half, dtype=f32) / half)
    ang = pos.astype(f32)[:, None] * inv[None, :]
    cos, sin = jnp.cos(ang), jnp.sin(ang)
    cos_t = jnp.concatenate([cos, cos] * SW_KV_HEADS, axis=-1)
    sin_t = jnp.concatenate([-sin, sin] * SW_KV_HEADS, axis=-1)
    return cos_t, sin_t


def _tile(n, pref):
    return pref if n % pref == 0 else n


def _trunk(x, pos, L, state, p):
    B, T, _ = x.shape
    n = B * T
    x2 = x.reshape(n, D_MODEL)
    tm = _tile(n, 1024)
    precise = state is not None
    ab_w, wo_h, wo_a = (p["ab_w32"], p["ab_wo_h32"], p["ab_wo_a32"]) if precise else (p["ab_w"], p["ab_wo_h"], p["ab_wo_a"])

    z_ml, z_sw, z_g = _proj(x2, ab_w, ((0, ML_W), (ML_W, SW_W), (ML_W + SW_W, LANES)), tm)
    if state is None:
        h_ml, ml_c, ml_n, ml_m = _mlstm_pairs(z_ml, z_g, p["ab_bias"], p["ab_norm"], B, T, L)
    else:
        cn0 = jnp.concatenate([state["ml_C"], state["ml_n"][..., None],
                               jnp.zeros((B, ML_HEADS, ML_DK, LANES - ML_DV - 1), f32)], axis=-1)
        m0 = jnp.zeros((B, 1, LANES), f32).at[:, 0, :ML_HEADS].set(state["ml_m"])
        h_ml, cn, m_out = _mlstm(z_ml, z_g, p["ab_bias"], p["ab_norm"], cn0, m0, B, T, L, precise)
        ml_c, ml_n, ml_m = cn[..., :ML_DV], cn[..., ML_DV], m_out[:, 0, :ML_HEADS]
    h_ml = h_ml.reshape(n, ML_HEADS * ML_DV)
    cos_t, sin_t = _rope_tables(pos)
    if state is None:
        a_sw, k_rot = _swa_prompt(z_sw, cos_t, sin_t, p["ab_sinks"], B, T, L)
    else:
        a_sw, k_rot = _swa_sample(z_sw, state["sw_k"].reshape(B, WINDOW, LANES),
                                  state["sw_v"].reshape(B, WINDOW, LANES), cos_t, sin_t, p["ab_sinks"], B, T)
    keep = min(T, WINDOW)
    new_k = k_rot.reshape(B, T, LANES)[:, T - keep:].reshape(B, keep, SW_KV_HEADS, SW_HD)
    new_v = z_sw.reshape(B, T, SW_W)[:, T - keep:, SW_HEADS * SW_HD + LANES:].reshape(B, keep, SW_KV_HEADS, SW_HD)
    x2, gates_t, route = _out_ln(x2, [h_ml, a_sw], [wo_h, wo_a], p["ln_g"][0][0], p["ln_b"][0][0],
                                 p["router_wt"], p["router_b"], tm)
    x2, spare = _moe_block(x2, gates_t, route, p, 0)

    tm1 = _tile(n, 256)
    if state is None:
        s0 = jnp.zeros((B, GD_V_HEADS, GD_HD, GD_HD), f32)
        conv8 = jnp.zeros((B, 8, GD_CONV_CH), f32)
    else:
        s0 = state["gd_S"]
        conv8 = jnp.concatenate([jnp.zeros((B, 8 - (GD_CONV - 1), GD_CONV_CH), f32), state["gd_conv"]], axis=1)
    qkv, zg, ba = _proj(x2, p["c_w"], ((0, GD_CONV_CH), (GD_CONV_CH, GD_V_W), (GD_CONV_CH + GD_V_W, LANES)), tm1)
    new_conv = qkv.reshape(B, T, GD_CONV_CH)[:, T - (GD_CONV - 1):]
    o_gd, s_out = _gdn(qkv, zg, ba, p["c_conv_w"], p["c_alog"], p["c_dt"], p["c_norm"], s0, conv8, B, T, L)
    x2, gates_t, route = _out_ln(x2, [o_gd], [p["c_wo"]], p["ln_g"][1][0], p["ln_b"][1][0],
                                 p["router_wt"], p["router_b"], tm)
    x2, _ = _moe_block(x2, gates_t, route, p, 1, spare)

    outs = (new_k[None], new_v[None], ml_c[None], ml_n[None], ml_m[None],
            s_out[None], new_conv[None])
    return x2.reshape(B, T, D_MODEL), outs


def _moe_block(x2, gates_t, route, p, layer, spare=None):
    n = x2.shape[0]
    wg, wu, wd = p["ex_gate"], p["ex_up"], p["ex_down"]
    base = layer * N_EXPERTS
    g_row, b_row = p["ln_g"][layer][1], p["ln_b"][layer][1]
    if n < N_PAIRS * MOE_TM:
        return _moe_ln(x2, gates_t.T, wg, wu, wd, base, g_row, b_row, _tile(n, 1024)), None
    row_of_token, tile_a, tile_b, tile_valid = _pair_plan(route, n, MOE_TM)
    xs = _scatter_rows(x2, row_of_token, n + N_PAIRS * MOE_TM, MOE_DMA_ROWS, spare)
    ys = _pair_experts(xs, p["router_wt"], tile_a, tile_b, tile_valid, wg, wu, wd, base, g_row, b_row, MOE_TM)
    return _gather_rows(ys, row_of_token, MOE_DMA_ROWS), ys


def kernel(x_prompt, x_sample, cache_swa_k, cache_swa_v, state_mlstm_C, state_mlstm_n, state_mlstm_m, state_gdn_S, state_gdn_conv, ab_w_in, ab_b_i, ab_b_f, ab_norm, ab_sinks, ab_w_out, c_w_in, c_conv_w, c_a_log, c_dt_bias, c_norm, c_w_out, ln_g, ln_b, router_w, router_b, ex_gate, ex_up, ex_down):
    gate_lo = ML_W
    sw_lo = ML_W + 2 * ML_HEADS
    w0 = ab_w_in[0]
    ab_w32 = jnp.concatenate([w0[:, :gate_lo], w0[:, sw_lo:], w0[:, gate_lo:sw_lo],
                              jnp.zeros((D_MODEL, LANES - 2 * ML_HEADS), f32)], axis=1)
    w1 = c_w_in[0]
    c_w = jnp.concatenate([w1, jnp.zeros((D_MODEL, LANES - 2 * GD_V_HEADS), f32)], axis=1).astype(bf16)
    wo32 = ab_w_out[0]
    wo = wo32.astype(bf16)
    p = {
        "ab_w": ab_w32.astype(bf16),
        "ab_w32": ab_w32,
        "ab_wo_h32": wo32[:ML_HEADS * ML_DV],
        "ab_wo_a32": wo32[ML_HEADS * ML_DV:],
        "ab_bias": _pad_lanes(jnp.concatenate([ab_b_i[0], ab_b_f[0]])),
        "ab_norm": ab_norm[0].reshape(1, ML_HEADS * ML_DV),
        "ab_sinks": _pad_lanes(ab_sinks[0]),
        "ab_wo_h": wo[:ML_HEADS * ML_DV],
        "ab_wo_a": wo[ML_HEADS * ML_DV:],
        "c_w": c_w,
        "c_conv_w": c_conv_w[0],
        "c_alog": _pad_lanes(c_a_log[0], GD_V_HEADS),
        "c_dt": _pad_lanes(c_dt_bias[0], GD_V_HEADS),
        "c_norm": c_norm[0].reshape(1, GD_HD),
        "c_wo": c_w_out[0].astype(bf16),
        "ln_g": [[ln_g[i, j].reshape(1, D_MODEL) for j in range(2)] for i in range(DEPTH)],
        "ln_b": [[ln_b[i, j].reshape(1, D_MODEL) for j in range(2)] for i in range(DEPTH)],
        "router_wt": router_w.T,
        "router_b": router_b.reshape(N_EXPERTS, 1),
        "ex_gate": ex_gate.reshape(DEPTH * N_EXPERTS, D_MODEL, D_EXPERT),
        "ex_up": ex_up.reshape(DEPTH * N_EXPERTS, D_MODEL, D_EXPERT),
        "ex_down": ex_down.reshape(DEPTH * N_EXPERTS, D_EXPERT, D_MODEL),
    }
    t_p = x_prompt.shape[1]
    y_p, st_p = _trunk(x_prompt, jnp.arange(t_p, dtype=jnp.int32), CHUNK, None, p)
    t_s = x_sample.shape[1]
    state = {"sw_k": cache_swa_k[0], "sw_v": cache_swa_v[0], "ml_C": state_mlstm_C[0], "ml_n": state_mlstm_n[0],
             "ml_m": state_mlstm_m[0], "gd_S": state_gdn_S[0], "gd_conv": state_gdn_conv[0]}
    y_s, st_s = _trunk(x_sample, PAST_LEN + jnp.arange(t_s, dtype=jnp.int32), t_s, state, p)
    return (y_p, y_s) + st_p + st_s
```

```python
import functools
import math

import jax
import jax.numpy as jnp
import numpy as np
from jax import lax
from jax.experimental import pallas as pl
from jax.experimental.pallas import tpu as pltpu

f32 = jnp.float32
bf16 = jnp.bfloat16
HIGHEST = lax.Precision.HIGHEST

D_MODEL = 1024
DEPTH = 2
CHUNK = 64
PAST_LEN = 2048
ML_HEADS = 8
ML_DK = 64
ML_DV = 64
SW_HEADS = 8
SW_KV_HEADS = 2
SW_HD = 64
SW_GROUP = SW_HEADS // SW_KV_HEADS
WINDOW = 128
ROPE_THETA = 10000.0
GD_QK_HEADS = 8
GD_V_HEADS = 16
GD_HD = 128
GD_CONV = 4
GD_QK_W = GD_QK_HEADS * GD_HD
GD_V_W = GD_V_HEADS * GD_HD
GD_CONV_CH = 2 * GD_QK_W + GD_V_W
N_EXPERTS = 16
N_GROUPS = 4
EXP_PER_GROUP = 4
D_EXPERT = 512
DN_ALPHA = (2 * DEPTH) ** 0.25
LN_EPS = 1e-5
RMS_EPS = 1e-6

LANES = 128
ML_W = 4 * ML_HEADS * ML_DK
SW_W = SW_HEADS * SW_HD + 2 * SW_KV_HEADS * SW_HD
VMEM_LIMIT = 56 * 1024 * 1024

SDS = jax.ShapeDtypeStruct


def _cparams(*sem):
    return pltpu.CompilerParams(dimension_semantics=sem, vmem_limit_bytes=VMEM_LIMIT)


def _dot(a, b):
    return jnp.dot(a, b, preferred_element_type=f32, precision=HIGHEST)


def _dot_nt(a, b):
    return lax.dot_general(a, b, (((1,), (1,)), ((), ())), preferred_element_type=f32, precision=HIGHEST)


def _bdot(a, b):
    return jnp.dot(a.astype(bf16), b.astype(bf16), preferred_element_type=f32)


def _bdot_nt(a, b):
    return lax.dot_general(a.astype(bf16), b.astype(bf16), (((1,), (1,)), ((), ())), preferred_element_type=f32)


def _bdot_tn(a, b):
    return jnp.dot(a.T.astype(bf16), b.astype(bf16), preferred_element_type=f32)


def _dots(precise):
    if precise:
        return _dot, _dot_nt, lambda a, b: _dot(a.T, b)
    return _bdot, _bdot_nt, _bdot_tn


def _sigmoid(x):
    return 0.5 + 0.5 * jnp.tanh(0.5 * x)


def _silu(x):
    hx = 0.5 * x
    return hx + hx * jnp.tanh(hx)


def _softplus(x):
    return jnp.maximum(x, 0.0) + jnp.log(1.0 + jnp.exp(-jnp.abs(x)))


def _layer_norm(v, g, b):
    mu = jnp.mean(v, axis=-1, keepdims=True)
    d = v - mu
    var = jnp.mean(d * d, axis=-1, keepdims=True)
    return d * lax.rsqrt(var + LN_EPS) * g + b


def _proj_kernel(x_ref, w_ref, *o_refs, splits, col_chunk, precise):
    mm = _dots(precise)[0]
    xb = x_ref[...] if precise else x_ref[...].astype(bf16)
    for o_ref, (start, width) in zip(o_refs, splits):
        for c in range(0, width, col_chunk):
            cw = min(col_chunk, width - c)
            o_ref[:, c:c + cw] = mm(xb, w_ref[:, start + c:start + c + cw])


def _proj(x2, w, splits, tm):
    n, k = x2.shape
    return pl.pallas_call(
        functools.partial(_proj_kernel, splits=splits, col_chunk=512, precise=w.dtype == f32),
        grid=(n // tm,),
        in_specs=[pl.BlockSpec((tm, k), lambda i: (i, 0)),
                  pl.BlockSpec(w.shape, lambda i: (0, 0), pipeline_mode=pl.Buffered(1))],
        out_specs=[pl.BlockSpec((tm, wd), lambda i: (i, 0)) for _, wd in splits],
        out_shape=[SDS((n, wd), f32) for _, wd in splits],
        compiler_params=_cparams("parallel"),
        name="in_proj",
    )(x2, w)


def _mlstm_kernel(z_ref, g_ref, bias_ref, nw_ref, cn0_ref, m0_ref, h_ref, cn_ref, m_ref, *, L, BB, precise):
    @pl.when(pl.program_id(1) == 0)
    def _():
        cn_ref[...] = cn0_ref[...]
        m_ref[...] = m0_ref[...]

    mm, mm_nt, mm_tn = _dots(precise)
    row = lax.broadcasted_iota(jnp.int32, (L, L), 0)
    col = lax.broadcasted_iota(jnp.int32, (L, L), 1)
    causal = row >= col
    tri = causal.astype(f32)
    lane = lax.broadcasted_iota(jnp.int32, (1, LANES), 1)
    lane_l = lax.broadcasted_iota(jnp.int32, (L, ML_DV), 1)
    one_hot0 = (lane_l == 0).astype(f32)
    for bi in range(BB):
        g = g_ref[bi] + bias_ref[...]
        lf = jnp.minimum(g, 0.0) - jnp.log(1.0 + jnp.exp(-jnp.abs(g)))
        bcum = _dot(tri, lf)
        b_t = bcum.T
        g_t = g.T
        m_row = m_ref[bi]
        new_m = m_row
        outs = []
        for h in range(ML_HEADS):
            b_col = bcum[:, ML_HEADS + h:ML_HEADS + h + 1]
            b_row = b_t[ML_HEADS + h:ML_HEADS + h + 1, :]
            ig_row = g_t[h:h + 1, :]
            ig_col = g[:, h:h + 1]
            m_h = m_row[:, h:h + 1]
            dmat = jnp.where(causal, b_col - b_row + ig_row, -jnp.inf)
            inter = b_col + m_h
            mt = jnp.maximum(inter, jnp.max(dmat, axis=-1, keepdims=True))
            a = jnp.exp(inter - mt)
            q = z_ref[bi, :, h * ML_DK:(h + 1) * ML_DK]
            k = z_ref[bi, :, ML_HEADS * ML_DK + h * ML_DK:ML_HEADS * ML_DK + (h + 1) * ML_DK] * (ML_DK ** -0.5)
            v = z_ref[bi, :, 2 * ML_HEADS * ML_DK + h * ML_DV:2 * ML_HEADS * ML_DK + (h + 1) * ML_DV]
            og = z_ref[bi, :, 3 * ML_HEADS * ML_DK + h * ML_DV:3 * ML_HEADS * ML_DK + (h + 1) * ML_DV]
            s = mm_nt(q, k) * jnp.exp(dmat - mt)
            vext = jnp.concatenate([v, one_hot0], axis=-1)
            cn = cn_ref[bi, h]
            tot = a * mm(q, cn) + mm(s, vext)
            num = tot[:, :ML_DV]
            den = tot[:, ML_DV:ML_DV + 1]
            hh = num / jnp.maximum(jnp.abs(den), jnp.exp(-mt))
            hh = hh * lax.rsqrt(jnp.mean(hh * hh, axis=-1, keepdims=True) + RMS_EPS) * nw_ref[:, h * ML_DV:(h + 1) * ML_DV]
            outs.append(hh * _sigmoid(og))
            m_new = mt[L - 1:L, :]
            b_last = b_col[L - 1:L, :]
            wk = jnp.exp(b_last - b_col + ig_col - m_new)
            dec = jnp.exp(b_last + m_h - m_new)
            cn_ref[bi, h] = dec * cn + mm_tn(k, wk * vext)
            new_m = jnp.where(lane == h, m_new, new_m)
        m_ref[bi] = new_m
        h_ref[bi] = jnp.concatenate(outs, axis=-1)


def _mlstm(z_ml, z_g, bias_row, norm_row, cn0, m0, B, T, L, precise):
    nc = T // L
    bb = min(B, 4)
    tok = lambda b, c: (b, c, 0)
    st4 = lambda b, c: (b, 0, 0, 0)
    st3 = lambda b, c: (b, 0, 0)
    return pl.pallas_call(
        functools.partial(_mlstm_kernel, L=L, BB=bb, precise=precise),
        grid=(B // bb, nc),
        in_specs=[pl.BlockSpec((bb, L, ML_W), tok),
                  pl.BlockSpec((bb, L, LANES), tok),
                  pl.BlockSpec((1, LANES), lambda b, c: (0, 0)),
                  pl.BlockSpec((1, ML_HEADS * ML_DV), lambda b, c: (0, 0)),
                  pl.BlockSpec((bb, ML_HEADS, ML_DK, LANES), st4),
                  pl.BlockSpec((bb, 1, LANES), st3)],
        out_specs=[pl.BlockSpec((bb, L, ML_HEADS * ML_DV), tok),
                   pl.BlockSpec((bb, ML_HEADS, ML_DK, LANES), st4),
                   pl.BlockSpec((bb, 1, LANES), st3)],
        out_shape=[SDS((B, T, ML_HEADS * ML_DV), f32),
                   SDS((B, ML_HEADS, ML_DK, LANES), f32),
                   SDS((B, 1, LANES), f32)],
        compiler_params=_cparams("parallel", "arbitrary"),
        name="mlstm",
    )(z_ml.reshape(B, T, ML_W), z_g.reshape(B, T, LANES), bias_row, norm_row, cn0, m0)


ML_PAIRS = ML_HEADS // 2
ML_REP_QUANTS = 3


def _pair_select_matrix():
    sel = np.zeros((LANES, ML_REP_QUANTS * ML_PAIRS * LANES), np.float32)
    for qn in range(ML_REP_QUANTS):
        for pr in range(ML_PAIRS):
            for half in range(2):
                lo = (qn * ML_PAIRS + pr) * LANES + half * ML_DV
                sel[ML_HEADS * qn + 2 * pr + half, lo:lo + ML_DV] = 1.0
    return jnp.asarray(sel, bf16)


def _exact_select(x, sel):
    hi = x.astype(bf16)
    r1 = x - hi.astype(f32)
    mid = r1.astype(bf16)
    lo = (r1 - mid.astype(f32)).astype(bf16)
    mm = lambda t: jnp.dot(t, sel, preferred_element_type=f32)
    return (mm(hi) + mm(mid)) + mm(lo)


def _mlstm_pair_kernel(z_ref, g_ref, bias_ref, nw_ref, sel_ref, cbd0_ref, nbd0_ref, m0_ref,
                       h_ref, cbd_ref, nbd_ref, m_ref, *, L, BB):
    @pl.when(pl.program_id(1) == 0)
    def _():
        cbd_ref[...] = cbd0_ref[...]
        nbd_ref[...] = nbd0_ref[...]
        m_ref[...] = m0_ref[...]

    tri = (lax.broadcasted_iota(jnp.int32, (L, L), 0) >= lax.broadcasted_iota(jnp.int32, (L, L), 1)).astype(f32)
    row_t = lax.broadcasted_iota(jnp.int32, (L, LANES), 0)
    lane_t = lax.broadcasted_iota(jnp.int32, (L, LANES), 1)
    first_half = lane_t < ML_DV
    causal2 = row_t >= (lane_t % ML_DV)
    rr = lax.broadcasted_iota(jnp.int32, (LANES, LANES), 0)
    cc = lax.broadcasted_iota(jnp.int32, (LANES, LANES), 1)
    same_block = (rr < ML_DV) == (cc < ML_DV)
    ones_bd = same_block.astype(bf16)
    lane_1 = lax.broadcasted_iota(jnp.int32, (1, LANES), 1)
    sel = sel_ref[...]
    n_tiles = ML_REP_QUANTS * ML_PAIRS

    gs = [g_ref[bi] + bias_ref[...] for bi in range(BB)]
    lfs = [jnp.minimum(g, 0.0) - jnp.log(1.0 + jnp.exp(-jnp.abs(g))) for g in gs]
    bc_all = _dot(tri, jnp.concatenate(lfs, axis=1))
    g_ts = [g.T for g in gs]
    b_ts = [bc_all[:, bi * LANES:(bi + 1) * LANES].T for bi in range(BB)]
    r_rows = [g_ts[bi][0:ML_HEADS] - b_ts[bi][ML_HEADS:2 * ML_HEADS] for bi in range(BB)]
    cm = jnp.concatenate([jnp.concatenate(r_rows, axis=0), jnp.full((BB * ML_HEADS, LANES - L), -jnp.inf, f32)], axis=1)
    shift = 1
    while shift < L:
        cm = jnp.maximum(cm, pltpu.roll(cm, shift, 1))
        shift *= 2
    cols = [jnp.concatenate([g_ts[bi][0:ML_HEADS], b_ts[bi][ML_HEADS:2 * ML_HEADS],
                             cm[bi * ML_HEADS:(bi + 1) * ML_HEADS, :L],
                             jnp.zeros((LANES - 3 * ML_HEADS, L), f32)], axis=0).T for bi in range(BB)]
    rep_all = _exact_select(jnp.concatenate(cols, axis=0), sel)
    rep = [rep_all[bi * L:(bi + 1) * L] for bi in range(BB)]
    m_all = jnp.concatenate([m_ref[bi] for bi in range(BB)] + [jnp.zeros((8 - BB, LANES), f32)], axis=0)
    m_rep_all = _exact_select(m_all, sel[:, :ML_PAIRS * LANES])
    m_rep = [m_rep_all[bi:bi + 1] for bi in range(BB)]

    units = [(bi, pr) for bi in range(BB) for pr in range(ML_PAIRS)]
    st = {}
    for u in units:
        bi, pr = u
        tile = lambda qn: rep[bi][:, (qn * ML_PAIRS + pr) * LANES:(qn * ML_PAIRS + pr + 1) * LANES]
        ig_rep, b_rep, cm_rep = tile(0), tile(1), tile(2)
        m_pair = m_rep[bi][:, pr * LANES:(pr + 1) * LANES]
        inter = b_rep + m_pair
        mt = jnp.maximum(inter, b_rep + cm_rep)
        r_row = jnp.concatenate([r_rows[bi][2 * pr:2 * pr + 1], r_rows[bi][2 * pr + 1:2 * pr + 2]], axis=1)
        e = jnp.exp(jnp.where(causal2, (b_rep - mt) + r_row, -jnp.inf))
        m_new = mt[L - 1:L]
        b_last = b_rep[L - 1:L]
        lo = pr * LANES
        q = z_ref[bi, :, lo:lo + LANES].astype(bf16)
        k = z_ref[bi, :, ML_HEADS * ML_DK + lo:ML_HEADS * ML_DK + lo + LANES] * (ML_DK ** -0.5)
        v = z_ref[bi, :, 2 * ML_HEADS * ML_DK + lo:2 * ML_HEADS * ML_DK + lo + LANES]
        kbd = jnp.concatenate([jnp.where(first_half, k, 0.0), jnp.where(first_half, 0.0, k)], axis=0).astype(bf16)
        vbd = jnp.concatenate([jnp.where(first_half, v, 0.0), jnp.where(first_half, 0.0, v)], axis=0).astype(bf16)
        wk = jnp.exp(b_last - b_rep + ig_rep - m_new)
        st[u] = dict(a=jnp.exp(inter - mt), em=jnp.exp(-mt), e=e, m_new=m_new, dec=jnp.exp(b_last + m_pair - m_new),
                     q=q, kbd=kbd, vbd=vbd, k_t=k.T.astype(bf16), wkv=(wk * v).astype(bf16), wk=wk.astype(bf16),
                     cbd=cbd_ref[bi, pr], nbd=nbd_ref[bi, pr])
    for u in units:
        d = st[u]
        d["qk"] = lax.dot_general(d["q"], d["kbd"], (((1,), (1,)), ((), ())), preferred_element_type=f32)
        d["qc"] = jnp.dot(d["q"], d["cbd"].astype(bf16), preferred_element_type=f32)
        d["qn"] = jnp.dot(d["q"], d["nbd"].astype(bf16), preferred_element_type=f32)
    for u in units:
        d = st[u]
        s = (d["qk"] * d["e"]).astype(bf16)
        num = d["a"] * d["qc"] + jnp.dot(s, d["vbd"], preferred_element_type=f32)
        den = d["a"] * d["qn"] + jnp.dot(s, ones_bd, preferred_element_type=f32)
        d["hh"] = num / jnp.maximum(jnp.abs(den), d["em"])
    for u in units:
        bi, pr = u
        d = st[u]
        sq = d["hh"] * d["hh"]
        sq_hi = sq.astype(bf16)
        sq_lo = (sq - sq_hi.astype(f32)).astype(bf16)
        ms = (jnp.dot(sq_hi, ones_bd, preferred_element_type=f32)
              + jnp.dot(sq_lo, ones_bd, preferred_element_type=f32)) * (1.0 / ML_DV)
        lo = pr * LANES
        og = z_ref[bi, :, 3 * ML_HEADS * ML_DK + lo:3 * ML_HEADS * ML_DK + lo + LANES]
        h_ref[bi, :, lo:lo + LANES] = d["hh"] * lax.rsqrt(ms + RMS_EPS) * nw_ref[:, lo:lo + LANES] * _sigmoid(og)
    for u in units:
        bi, pr = u
        d = st[u]
        cbd_ref[bi, pr] = d["dec"] * d["cbd"] + jnp.where(
            same_block, jnp.dot(d["k_t"], d["wkv"], preferred_element_type=f32), 0.0)
        nbd_ref[bi, pr] = d["dec"] * d["nbd"] + jnp.where(
            same_block, jnp.dot(d["k_t"], d["wk"], preferred_element_type=f32), 0.0)
    for bi in range(BB):
        new_m = m_ref[bi]
        for pr in range(ML_PAIRS):
            m_new = st[(bi, pr)]["m_new"]
            new_m = jnp.where(lane_1 == 2 * pr, m_new[:, 0:1], new_m)
            new_m = jnp.where(lane_1 == 2 * pr + 1, m_new[:, ML_DV:ML_DV + 1], new_m)
        m_ref[bi] = new_m


def _mlstm_pairs(z_ml, z_g, bias_row, norm_row, B, T, L):
    assert 2 * L == LANES and ML_DK == ML_DV == L
    nc = T // L
    bb = min(B, 4)
    tok = lambda b, c: (b, c, 0)
    st4 = lambda b, c: (b, 0, 0, 0)
    st3 = lambda b, c: (b, 0, 0)
    const = lambda b, c: (0, 0)
    sel = _pair_select_matrix()
    zeros_bd = jnp.zeros((B, ML_PAIRS, LANES, LANES), f32)
    h, cbd, nbd, m = pl.pallas_call(
        functools.partial(_mlstm_pair_kernel, L=L, BB=bb),
        grid=(B // bb, nc),
        in_specs=[pl.BlockSpec((bb, L, ML_W), tok),
                  pl.BlockSpec((bb, L, LANES), tok),
                  pl.BlockSpec((1, LANES), const),
                  pl.BlockSpec((1, ML_HEADS * ML_DV), const),
                  pl.BlockSpec(sel.shape, const),
                  pl.BlockSpec((bb, ML_PAIRS, LANES, LANES), st4),
                  pl.BlockSpec((bb, ML_PAIRS, LANES, LANES), st4),
                  pl.BlockSpec((bb, 1, LANES), st3)],
        out_specs=[pl.BlockSpec((bb, L, ML_HEADS * ML_DV), tok),
                   pl.BlockSpec((bb, ML_PAIRS, LANES, LANES), st4),
                   pl.BlockSpec((bb, ML_PAIRS, LANES, LANES), st4),
                   pl.BlockSpec((bb, 1, LANES), st3)],
        out_shape=[SDS((B, T, ML_HEADS * ML_DV), f32),
                   SDS((B, ML_PAIRS, LANES, LANES), f32),
                   SDS((B, ML_PAIRS, LANES, LANES), f32),
                   SDS((B, 1, LANES), f32)],
        compiler_params=_cparams("parallel", "arbitrary"),
        name="mlstm_pairs",
    )(z_ml.reshape(B, T, ML_W), z_g.reshape(B, T, LANES), bias_row, norm_row, sel, zeros_bd, zeros_bd,
      jnp.zeros((B, 1, LANES), f32))
    c_out = jnp.stack([cbd[:, :, :ML_DK, :ML_DV], cbd[:, :, ML_DK:, ML_DV:]], axis=2).reshape(B, ML_HEADS, ML_DK, ML_DV)
    n_out = jnp.stack([nbd[:, :, :ML_DK, 0], nbd[:, :, ML_DK:, ML_DV]], axis=2).reshape(B, ML_HEADS, ML_DK)
    return h, c_out, n_out, m[:, 0, :ML_HEADS]


def _rope(x, cos, sin_signed):
    w = x.shape[-1]
    lane = lax.broadcasted_iota(jnp.int32, x.shape, 1)
    swapped = jnp.where((lane % SW_HD) < SW_HD // 2, pltpu.roll(x, w - SW_HD // 2, 1), pltpu.roll(x, SW_HD // 2, 1))
    return x * cos + swapped * sin_signed


def _swa_attend(jobs, sinks_ref, L, precise=False):
    mm, mm_nt, _ = _dots(precise)
    units = [(j, g) for j in range(len(jobs)) for g in range(SW_KV_HEADS)]
    sinks = [jnp.concatenate(
        [jnp.broadcast_to(sinks_ref[:, g * SW_GROUP + i:g * SW_GROUP + i + 1], (L, 1)) for i in range(SW_GROUP)],
        axis=0) for g in range(SW_KV_HEADS)]
    s, p, sink_e = {}, {}, {}
    for j, g in units:
        qr, keys, _, _ = jobs[j]
        q4 = jnp.concatenate([qr[:, (g * SW_GROUP + i) * SW_HD:(g * SW_GROUP + i + 1) * SW_HD]
                              for i in range(SW_GROUP)], axis=0)
        s[j, g] = mm_nt(q4, keys[:, g * SW_HD:(g + 1) * SW_HD]) * (SW_HD ** -0.5)
    for j, g in units:
        first_valid = jobs[j][3]
        sc = s[j, g]
        if first_valid is not None:
            kcol = lax.broadcasted_iota(jnp.int32, (1, sc.shape[1]), 1)
            sc = jnp.where(kcol >= first_valid, sc, -jnp.inf)
        mx = jnp.maximum(jnp.max(sc, axis=-1, keepdims=True), sinks[g])
        p[j, g] = jnp.exp(sc - mx)
        sink_e[j, g] = jnp.exp(sinks[g] - mx)
    ones = jnp.ones((jobs[0][1].shape[0], SW_HD), f32)
    o = {u: mm(p[u], jobs[u[0]][2][:, u[1] * SW_HD:(u[1] + 1) * SW_HD]) / (mm(p[u], ones) + sink_e[u]) for u in units}
    return [jnp.concatenate([o[j, g][i * L:(i + 1) * L, :] for g in range(SW_KV_HEADS) for i in range(SW_GROUP)],
                            axis=-1) for j in range(len(jobs))]


def _swa_prompt_kernel(q_ref, kp_ref, kc_ref, vp_ref, vc_ref, cp_ref, cc_ref, sp_ref, sc_ref, sinks_ref,
                       o_ref, kr_ref, *, L, CB):
    i = pl.program_id(1)
    rows = CB * L
    back = 2 * L
    cos_q = jnp.concatenate([cc_ref[...]] * (SW_HEADS // SW_KV_HEADS), axis=-1)
    sin_q = jnp.concatenate([sc_ref[...]] * (SW_HEADS // SW_KV_HEADS), axis=-1)
    qr = _rope(q_ref[...], cos_q, sin_q)
    k_cur = _rope(kc_ref[...], cc_ref[...], sc_ref[...])
    kr_ref[...] = k_cur
    k_prev = _rope(kp_ref[rows - back:rows, :], cp_ref[rows - back:rows, :], sp_ref[rows - back:rows, :])
    keys = jnp.concatenate([k_prev, k_cur], axis=0)
    vals = jnp.concatenate([vp_ref[rows - back:rows, :], vc_ref[...]], axis=0)
    jobs = []
    for u in range(CB):
        first_valid = jnp.where(i == 0, back - u * L, 0) if u * L < back else None
        jobs.append((qr[u * L:(u + 1) * L], keys[u * L:(u + 3) * L], vals[u * L:(u + 3) * L], first_valid))
    o_ref[...] = jnp.concatenate(_swa_attend(jobs, sinks_ref, L), axis=0)


def _swa_prompt(z_sw, cos_t, sin_t, sinks_row, B, T, L):
    cb = 8
    rows = cb * L
    nb = T // rows
    n = B * T
    kcol = SW_HEADS * SW_HD // LANES
    vcol = kcol + 1
    cur = lambda b, i: (b * nb + i, 0)
    prev = lambda col: (lambda b, i: (b * nb + jnp.maximum(i - 1, 0), col))
    curc = lambda col: (lambda b, i: (b * nb + i, col))
    tab_cur = lambda b, i: (i, 0)
    tab_prev = lambda b, i: (jnp.maximum(i - 1, 0), 0)
    return pl.pallas_call(
        functools.partial(_swa_prompt_kernel, L=L, CB=cb),
        grid=(B, nb),
        in_specs=[pl.BlockSpec((rows, SW_HEADS * SW_HD), cur),
                  pl.BlockSpec((rows, LANES), prev(kcol)), pl.BlockSpec((rows, LANES), curc(kcol)),
                  pl.BlockSpec((rows, LANES), prev(vcol)), pl.BlockSpec((rows, LANES), curc(vcol)),
                  pl.BlockSpec((rows, LANES), tab_prev), pl.BlockSpec((rows, LANES), tab_cur),
                  pl.BlockSpec((rows, LANES), tab_prev), pl.BlockSpec((rows, LANES), tab_cur),
                  pl.BlockSpec((1, LANES), lambda b, i: (0, 0))],
        out_specs=[pl.BlockSpec((rows, SW_HEADS * SW_HD), cur),
                   pl.BlockSpec((rows, LANES), cur)],
        out_shape=[SDS((n, SW_HEADS * SW_HD), f32), SDS((n, LANES), f32)],
        compiler_params=_cparams("parallel", "parallel"),
        name="swa_prompt",
    )(z_sw, z_sw, z_sw, z_sw, z_sw, cos_t, cos_t, sin_t, sin_t, sinks_row)


def _swa_sample_kernel(q_ref, k_ref, v_ref, ck_ref, cv_ref, cos_ref, sin_ref, sinks_ref, o_ref, kr_ref, *, L):
    cos_q = jnp.concatenate([cos_ref[...]] * (SW_HEADS // SW_KV_HEADS), axis=-1)
    sin_q = jnp.concatenate([sin_ref[...]] * (SW_HEADS // SW_KV_HEADS), axis=-1)
    qr = _rope(q_ref[...], cos_q, sin_q)
    kr = _rope(k_ref[...], cos_ref[...], sin_ref[...])
    kr_ref[...] = kr
    keys = jnp.concatenate([ck_ref[0], kr], axis=0)
    vals = jnp.concatenate([cv_ref[0], v_ref[...]], axis=0)
    o_ref[...] = _swa_attend([(qr, keys, vals, None)], sinks_ref, L, precise=True)[0]


def _swa_sample(z_sw, cache_k, cache_v, cos_t, sin_t, sinks_row, B, T):
    n = B * T
    kcol = SW_HEADS * SW_HD // LANES
    return pl.pallas_call(
        functools.partial(_swa_sample_kernel, L=T),
        grid=(B,),
        in_specs=[pl.BlockSpec((T, SW_HEADS * SW_HD), lambda b: (b, 0)),
                  pl.BlockSpec((T, LANES), lambda b: (b, kcol)),
                  pl.BlockSpec((T, LANES), lambda b: (b, kcol + 1)),
                  pl.BlockSpec((1, WINDOW, LANES), lambda b: (b, 0, 0)),
                  pl.BlockSpec((1, WINDOW, LANES), lambda b: (b, 0, 0)),
                  pl.BlockSpec((T, LANES), lambda b: (0, 0)),
                  pl.BlockSpec((T, LANES), lambda b: (0, 0)),
                  pl.BlockSpec((1, LANES), lambda b: (0, 0))],
        out_specs=[pl.BlockSpec((T, SW_HEADS * SW_HD), lambda b: (b, 0)),
                   pl.BlockSpec((T, LANES), lambda b: (b, 0))],
        out_shape=[SDS((n, SW_HEADS * SW_HD), f32), SDS((n, LANES), f32)],
        compiler_params=_cparams("parallel"),
        name="swa_sample",
    )(z_sw, z_sw, z_sw, cache_k, cache_v, cos_t, sin_t, sinks_row)


def _out_ln_kernel(*refs, n_in):
    x_ref = refs[0]
    a_refs = refs[1:1 + n_in]
    w_refs = refs[1 + n_in:1 + 2 * n_in]
    g_ref, b_ref, rw_ref, rb_ref, o_ref, gates_ref, route_ref = refs[1 + 2 * n_in:]
    mm = _dots(w_refs[0].dtype == f32)[0]
    y = mm(a_refs[0][...], w_refs[0][...])
    for a_ref, w_ref in zip(a_refs[1:], w_refs[1:]):
        y = y + mm(a_ref[...], w_ref[...])
    x_new = _layer_norm(DN_ALPHA * x_ref[...] + y, g_ref[...], b_ref[...])
    o_ref[...] = x_new
    gates_ref[...], route_ref[...] = _route(x_new, rw_ref, rb_ref)


def _out_ln(x2, acts, ws, g_row, b_row, rw_t, rb_col, tm):
    n = x2.shape[0]
    row = lambda i: (i, 0)
    const = lambda i: (0, 0)
    col = lambda i: (0, i)
    return pl.pallas_call(
        functools.partial(_out_ln_kernel, n_in=len(acts)),
        grid=(n // tm,),
        in_specs=[pl.BlockSpec((tm, D_MODEL), row)]
        + [pl.BlockSpec((tm, a.shape[1]), row) for a in acts]
        + [pl.BlockSpec(w.shape, const) for w in ws]
        + [pl.BlockSpec((1, D_MODEL), const), pl.BlockSpec((1, D_MODEL), const),
           pl.BlockSpec((N_EXPERTS, D_MODEL), const), pl.BlockSpec((N_EXPERTS, 1), const)],
        out_specs=[pl.BlockSpec((tm, D_MODEL), row), pl.BlockSpec((N_EXPERTS, tm), col), pl.BlockSpec((8, tm), col)],
        out_shape=[SDS((n, D_MODEL), f32), SDS((N_EXPERTS, n), f32), SDS((8, n), f32)],
        compiler_params=_cparams("parallel"),
        name="out_proj_ln",
    )(x2, *acts, *ws, g_row, b_row, rw_t, rb_col)


def _logistic(x):
    return 1.0 / (1.0 + jnp.exp(-x))


def _route(x, rw_ref, rb_ref):
    logits = _dot_nt(rw_ref[...], x)
    aff = _logistic(logits)
    sc = aff + rb_ref[...]
    s = [sc[e:e + 1, :] for e in range(N_EXPERTS)]
    a = [aff[e:e + 1, :] for e in range(N_EXPERTS)]
    scores = []
    for gi in range(N_GROUPS):
        w, x, y, z = s[4 * gi:4 * gi + 4]
        p, q = jnp.maximum(w, x), jnp.minimum(w, x)
        r, t = jnp.maximum(y, z), jnp.minimum(y, z)
        scores.append(jnp.maximum(p, r) + jnp.maximum(jnp.minimum(p, r), jnp.maximum(q, t)))
    best = scores[0]
    gsel = jnp.zeros_like(best, dtype=jnp.int32)
    for gi in range(1, N_GROUPS):
        better = scores[gi] > best
        best = jnp.where(better, scores[gi], best)
        gsel = jnp.where(better, gi, gsel)
    sel = []
    for e in range(N_EXPERTS):
        gi, i = divmod(e, EXP_PER_GROUP)
        beaten = jnp.zeros_like(gsel)
        for j in range(EXP_PER_GROUP):
            if j == i:
                continue
            o = s[4 * gi + j]
            wins = (o >= s[e]) if j < i else (o > s[e])
            beaten = beaten + wins.astype(jnp.int32)
        sel.append((gsel == gi) & (beaten < 2))
    den = jnp.zeros_like(best)
    for e in range(N_EXPERTS):
        den = den + jnp.where(sel[e], a[e], 0.0)
    gate = [jnp.where(sel[e], a[e] / den, 0.0) for e in range(N_EXPERTS)]
    taken = jnp.zeros_like(gsel)
    ea = eb = wa = wb = jnp.zeros_like(best)
    for e in range(N_EXPERTS):
        first = sel[e] & (taken == 0)
        second = sel[e] & (taken == 1)
        ea = jnp.where(first, float(e), ea)
        wa = jnp.where(first, gate[e], wa)
        eb = jnp.where(second, float(e), eb)
        wb = jnp.where(second, gate[e], wb)
        taken = taken + sel[e].astype(jnp.int32)
    route = jnp.concatenate([ea, eb, wa, wb, jnp.zeros((4, ea.shape[1]), f32)], axis=0)
    return jnp.concatenate(gate, axis=0), route


def _moe_kernel(x_ref, gates_ref, wg_ref, wu_ref, wd_ref, g_ref, b_ref, o_ref, xb_ref, acc_ref):
    e = pl.program_id(1)

    @pl.when(e == 0)
    def _():
        xb_ref[...] = x_ref[...].astype(bf16)
        acc_ref[...] = jnp.zeros_like(acc_ref)

    xb = xb_ref[...]
    lane = lax.broadcasted_iota(jnp.int32, gates_ref.shape, 1)
    gcol = jnp.sum(jnp.where(lane == e, gates_ref[...], 0.0), axis=-1, keepdims=True)
    h = _silu(_bdot(xb, wg_ref[0])) * _bdot(xb, wu_ref[0])
    acc_ref[...] += _bdot(gcol * h, wd_ref[0])

    @pl.when(e == N_EXPERTS - 1)
    def _():
        o_ref[...] = _layer_norm(DN_ALPHA * x_ref[...] + acc_ref[...], g_ref[...], b_ref[...])


def _moe_ln(x2, gates, wg, wu, wd, base, g_row, b_row, tm):
    n = x2.shape[0]
    row = lambda i, e: (i, 0)
    const = lambda i, e: (0, 0)
    expert = lambda i, e: (base + e, 0, 0)
    return pl.pallas_call(
        _moe_kernel,
        grid=(n // tm, N_EXPERTS),
        in_specs=[pl.BlockSpec((tm, D_MODEL), row),
                  pl.BlockSpec((tm, N_EXPERTS), row),
                  pl.BlockSpec((1, D_MODEL, D_EXPERT), expert),
                  pl.BlockSpec((1, D_MODEL, D_EXPERT), expert),
                  pl.BlockSpec((1, D_EXPERT, D_MODEL), expert),
                  pl.BlockSpec((1, D_MODEL), const), pl.BlockSpec((1, D_MODEL), const)],
        out_specs=pl.BlockSpec((tm, D_MODEL), row),
        out_shape=SDS((n, D_MODEL), f32),
        scratch_shapes=[pltpu.VMEM((tm, D_MODEL), bf16), pltpu.VMEM((tm, D_MODEL), f32)],
        compiler_params=_cparams("parallel", "arbitrary"),
        name="moe_ln",
    )(x2, gates, wg, wu, wd, g_row, b_row)


N_PAIRS = N_GROUPS * (EXP_PER_GROUP * (EXP_PER_GROUP - 1) // 2)
MOE_TM = 256
MOE_DMA_ROWS = 512
_PAIR_A = [g * EXP_PER_GROUP + a for g in range(N_GROUPS) for a in range(EXP_PER_GROUP) for b in range(a + 1, EXP_PER_GROUP)]
_PAIR_B = [g * EXP_PER_GROUP + b for g in range(N_GROUPS) for a in range(EXP_PER_GROUP) for b in range(a + 1, EXP_PER_GROUP)]


def _gather_rows_kernel(idx_ref, src_ref, o_ref, sem, *, rows):
    base = pl.program_id(0) * rows

    def row_copy(j):
        return pltpu.make_async_copy(src_ref.at[pl.ds(idx_ref[base + j], 1)], o_ref.at[pl.ds(j, 1)], sem)

    def issue(j, carry):
        row_copy(2 * j).start(priority=0)
        row_copy(2 * j + 1).start(priority=1)
        return carry

    def drain(j, carry):
        row_copy(j).wait()
        return carry

    lax.fori_loop(0, rows // 2, issue, 0, unroll=4)
    lax.fori_loop(0, rows, drain, 0, unroll=8)


def _gather_rows(src, idx, rows):
    n_out = idx.shape[0]
    d = src.shape[1]
    return pl.pallas_call(
        functools.partial(_gather_rows_kernel, rows=rows),
        grid_spec=pltpu.PrefetchScalarGridSpec(
            num_scalar_prefetch=1,
            grid=(n_out // rows,),
            in_specs=[pl.BlockSpec(memory_space=pl.ANY)],
            out_specs=pl.BlockSpec((rows, d), lambda i, idx_ref: (i, 0)),
            scratch_shapes=[pltpu.SemaphoreType.DMA(())]),
        out_shape=SDS((n_out, d), f32),
        compiler_params=_cparams("arbitrary"),
        name="gather_rows",
    )(idx, src)


def _scatter_rows_kernel(idx_ref, src_ref, init_ref, o_ref, sem, *, rows):
    del init_ref
    base = pl.program_id(0) * rows

    def row_copy(j):
        return pltpu.make_async_copy(src_ref.at[pl.ds(j, 1)], o_ref.at[pl.ds(idx_ref[base + j], 1)], sem)

    def issue(j, carry):
        row_copy(2 * j).start(priority=0)
        row_copy(2 * j + 1).start(priority=1)
        return carry

    def drain(j, carry):
        row_copy(j).wait()
        return carry

    lax.fori_loop(0, rows // 2, issue, 0, unroll=4)
    lax.fori_loop(0, rows, drain, 0, unroll=8)


def _scatter_rows(src, idx, n_out, rows, init=None):
    n_src, d = src.shape
    if init is None:
        init = jnp.zeros((n_out, d), f32)
    return pl.pallas_call(
        functools.partial(_scatter_rows_kernel, rows=rows),
        grid_spec=pltpu.PrefetchScalarGridSpec(
            num_scalar_prefetch=1,
            grid=(n_src // rows,),
            in_specs=[pl.BlockSpec((rows, d), lambda i, idx_ref: (i, 0)), pl.BlockSpec(memory_space=pl.ANY)],
            out_specs=pl.BlockSpec(memory_space=pl.ANY),
            scratch_shapes=[pltpu.SemaphoreType.DMA(())]),
        out_shape=SDS((n_out, d), f32),
        input_output_aliases={2: 0},
        compiler_params=_cparams("arbitrary"),
        name="scatter_rows",
    )(idx, src, init)


def _pair_plan(route, n, tm):
    ea = route[0].astype(jnp.int32)
    eb = route[1].astype(jnp.int32)
    a = ea % EXP_PER_GROUP
    b = eb % EXP_PER_GROUP
    pidx = jnp.where(a == 0, b - 1, jnp.where(a == 1, b + 1, 5))
    pair = (ea // EXP_PER_GROUP) * (N_PAIRS // N_GROUPS) + pidx
    onehot = (pair[:, None] == jnp.arange(N_PAIRS, dtype=jnp.int32)[None, :]).astype(jnp.int32)
    csum = jnp.cumsum(onehot, axis=0)
    counts = csum[-1]
    ntiles = (counts + tm - 1) // tm
    tile_end = jnp.cumsum(ntiles)
    tile_start = tile_end - ntiles
    row_of_token = jnp.sum(onehot * (csum - 1 + (tile_start * tm)[None, :]), axis=1)
    nt = n // tm + N_PAIRS
    tile_id = jnp.arange(nt, dtype=jnp.int32)
    tile_valid = tile_id < tile_end[-1]
    tile_pair = jnp.sum((tile_end[None, :] <= jnp.minimum(tile_id, tile_end[-1] - 1)[:, None]).astype(jnp.int32), axis=1)
    tile_pair = jnp.minimum(tile_pair, N_PAIRS - 1)
    pick = (tile_pair[:, None] == jnp.arange(N_PAIRS, dtype=jnp.int32)[None, :]).astype(jnp.int32)
    tile_a = jnp.sum(pick * jnp.asarray(_PAIR_A, jnp.int32)[None, :], axis=1)
    tile_b = jnp.sum(pick * jnp.asarray(_PAIR_B, jnp.int32)[None, :], axis=1)
    return row_of_token, tile_a, tile_b, tile_valid.astype(jnp.int32)


def _pair_expert_kernel(ta_ref, tb_ref, tv_ref, x_ref, rwt_ref, wga_ref, wua_ref, wda_ref, wgb_ref, wub_ref, wdb_ref,
                        g_ref, b_ref, o_ref):
    i = pl.program_id(0)
    valid = tv_ref[i] == 1

    @pl.when(valid)
    def _():
        x = x_ref[...]
        xb = x.astype(bf16)
        aff_a = _logistic(jnp.sum(x * rwt_ref[pl.ds(ta_ref[i], 1), :], axis=-1, keepdims=True))
        aff_b = _logistic(jnp.sum(x * rwt_ref[pl.ds(tb_ref[i], 1), :], axis=-1, keepdims=True))
        den = aff_a + aff_b
        acc = None
        for w, (wg, wu, wd) in ((aff_a / den, (wga_ref, wua_ref, wda_ref)), (aff_b / den, (wgb_ref, wub_ref, wdb_ref))):
            h = _silu(_bdot(xb, wg[0])) * _bdot(xb, wu[0])
            y = _bdot(w * h, wd[0])
            acc = y if acc is None else acc + y
        o_ref[...] = _layer_norm(DN_ALPHA * x + acc, g_ref[...], b_ref[...])

    @pl.when(jnp.logical_not(valid))
    def _():
        o_ref[...] = jnp.zeros_like(o_ref)


def _pair_experts(xs, rw_t, tile_a, tile_b, tile_valid, wg, wu, wd, base, g_row, b_row, tm):
    rows = xs.shape[0]
    row = lambda i, ta, tb, tv: (i, 0)
    const = lambda i, ta, tb, tv: (0, 0)
    ex_a = lambda i, ta, tb, tv: (base + ta[i], 0, 0)
    ex_b = lambda i, ta, tb, tv: (base + tb[i], 0, 0)
    up = pl.BlockSpec((1, D_MODEL, D_EXPERT), ex_a), pl.BlockSpec((1, D_MODEL, D_EXPERT), ex_b)
    down = pl.BlockSpec((1, D_EXPERT, D_MODEL), ex_a), pl.BlockSpec((1, D_EXPERT, D_MODEL), ex_b)
    return pl.pallas_call(
        _pair_expert_kernel,
        grid_spec=pltpu.PrefetchScalarGridSpec(
            num_scalar_prefetch=3,
            grid=(rows // tm,),
            in_specs=[pl.BlockSpec((tm, D_MODEL), row), pl.BlockSpec((N_EXPERTS, D_MODEL), const),
                      up[0], up[0], down[0], up[1], up[1], down[1],
                      pl.BlockSpec((1, D_MODEL), const), pl.BlockSpec((1, D_MODEL), const)],
            out_specs=pl.BlockSpec((tm, D_MODEL), row)),
        out_shape=SDS((rows, D_MODEL), f32),
        compiler_params=_cparams("arbitrary"),
        name="pair_experts",
    )(tile_a, tile_b, tile_valid, xs, rw_t, wg, wu, wd, wg, wu, wd, g_row, b_row)


def _unit_lower_inverses(mats, L):
    row = lax.broadcasted_iota(jnp.int32, (L, L), 0)
    col = lax.broadcasted_iota(jnp.int32, (L, L), 1)
    eye = (row == col).astype(f32)
    ps = [eye - a for a in mats]
    pws = [a.astype(bf16) for a in mats]
    span = 2
    while span < L:
        pws = [jnp.dot(pw, pw, preferred_element_type=f32).astype(bf16) for pw in pws]
        ps = [p + jnp.dot(pw, p.astype(bf16), preferred_element_type=f32) for p, pw in zip(ps, pws)]
        span *= 2
    return ps


def _gdn_kernel(x_ref, zg_ref, ba_ref, cw_ref, alog_ref, dt_ref, nw_ref, s0_ref, cb_ref,
                o_ref, s_ref, prev_ref, *, L, BB):
    @pl.when(pl.program_id(1) == 0)
    def _():
        s_ref[...] = s0_ref[...]
        prev_ref[...] = cb_ref[...]

    def conv_silu(bi, lo, width):
        cur = x_ref[bi, :, lo:lo + width]
        cat =jnp.concatenate([prev_ref[bi, :, lo:lo + width], cur], axis=0)
        acc = cat[5:5 + L] * cw_ref[0:1, lo:lo + width]
        acc = acc + cat[6:6 + L] * cw_ref[1:2, lo:lo + width]
        acc = acc + cat[7:7 + L] * cw_ref[2:3, lo:lo + width]
        acc = acc + cur * cw_ref[3:4, lo:lo + width]
        return _silu(acc)

    def l2n(v, scale):
        return v * lax.rsqrt(jnp.sum(v * v, axis=-1, keepdims=True) + 1e-6) * scale

    row = lax.broadcasted_iota(jnp.int32, (L, L), 0)
    col = lax.broadcasted_iota(jnp.int32, (L, L), 1)
    incl = row >= col
    strict = row > col
    rep = GD_V_HEADS // GD_QK_HEADS
    bas = [ba_ref[bi] for bi in range(BB)]
    betas = [_sigmoid(ba) for ba in bas]
    gls = [-jnp.exp(alog_ref[...]) * _softplus(ba + dt_ref[...]) for ba in bas]
    gcum_all = _dot(incl.astype(f32), jnp.concatenate(gls, axis=1))
    gcums = [gcum_all[:, bi * LANES:(bi + 1) * LANES] for bi in range(BB)]
    gcum_ts = [g.T for g in gcums]
    units = [(bi, hv) for bi in range(BB) for hv in range(GD_V_HEADS)]
    qs, ks, amats, qkds, egs, g_cols, rhss = {}, {}, [], {}, {}, {}, []
    for bi in range(BB):
        for j in range(GD_QK_HEADS):
            q = l2n(conv_silu(bi, j * GD_HD, GD_HD), GD_HD ** -0.5)
            k = l2n(conv_silu(bi, GD_QK_W + j * GD_HD, GD_HD), 1.0)
            qb, kb = q.astype(bf16), k.astype(bf16)
            kk = lax.dot_general(kb, kb, (((1,), (1,)), ((), ())), preferred_element_type=f32)
            qk = lax.dot_general(qb, kb, (((1,), (1,)), ((), ())), preferred_element_type=f32)
            qs[bi, j] = qb
            ks[bi, j] = k
            for r in range(rep):
                hv = j * rep + r
                v = conv_silu(bi, 2 * GD_QK_W + hv * GD_HD, GD_HD)
                g_col = gcums[bi][:, GD_V_HEADS + hv:GD_V_HEADS + hv + 1]
                g_row = gcum_ts[bi][GD_V_HEADS + hv:GD_V_HEADS + hv + 1, :]
                b_col = betas[bi][:, hv:hv + 1]
                decay = jnp.exp(jnp.where(incl, g_col - g_row, -jnp.inf))
                eg = jnp.exp(g_col)
                amats.append(jnp.where(strict, b_col * kk * decay, 0.0))
                qkds[bi, hv] = (qk * decay).astype(bf16)
                egs[bi, hv] = eg
                g_cols[bi, hv] = g_col
                rhss.append(((b_col * v).astype(bf16), ((b_col * eg) * k).astype(bf16)))
    tinvs = _unit_lower_inverses(amats, L)
    tinvs = [t.astype(bf16) for t in tinvs]
    sol_v = dict(zip(units, [jnp.dot(t, r[0], preferred_element_type=f32) for t, r in zip(tinvs, rhss)]))
    sol_k = dict(zip(units, [jnp.dot(t, r[1], preferred_element_type=f32) for t, r in zip(tinvs, rhss)]))
    sts, wks, qss, kts = {}, {}, {}, {}
    for u in units:
        bi, hv = u
        j = hv // rep
        sts[u] = s_ref[bi, hv]
        stb = sts[u].astype(bf16)
        wks[u] = jnp.dot(sol_k[u].astype(bf16), stb, preferred_element_type=f32)
        qss[u] = jnp.dot(qs[bi, j], stb, preferred_element_type=f32)
        g_col = g_cols[u]
        kts[u] = (jnp.exp(g_col[L - 1:L, :] - g_col) * ks[bi, j]).T.astype(bf16)
    outs = {}
    for u in units:
        bi, hv = u
        wnb = (sol_v[u] - wks[u]).astype(bf16)
        outs[u] = egs[u] * qss[u] + jnp.dot(qkds[u], wnb, preferred_element_type=f32)
        s_ref[bi, hv] = jnp.exp(g_cols[u][L - 1:L, :]) * sts[u] + jnp.dot(kts[u], wnb, preferred_element_type=f32)
    for u in units:
        bi, hv = u
        o = outs[u]
        o = o * lax.rsqrt(jnp.mean(o * o, axis=-1, keepdims=True) + RMS_EPS) * nw_ref[...]
        o_ref[bi, :, hv * GD_HD:(hv + 1) * GD_HD] = o * _silu(zg_ref[bi, :, hv * GD_HD:(hv + 1) * GD_HD])
    for bi in range(BB):
        prev_ref[bi] = x_ref[bi, L - 8:L, :]


def _gdn(qkv, zg, ba, conv_w, alog_row, dt_row, norm_row, s0, conv8, B, T, L):
    nc = T // L
    n = B * T
    bb = 2
    tok = lambda b, c: (b, c, 0)
    const = lambda b, c: (0, 0)
    st4 = lambda b, c: (b, 0, 0, 0)
    o, s_out = pl.pallas_call(
        functools.partial(_gdn_kernel, L=L, BB=bb),
        grid=(B // bb, nc),
        in_specs=[pl.BlockSpec((bb, L, GD_CONV_CH), tok),
                  pl.BlockSpec((bb, L, GD_V_W), tok),
                  pl.BlockSpec((bb, L, LANES), tok),
                  pl.BlockSpec((GD_CONV, GD_CONV_CH), const),
                  pl.BlockSpec((1, LANES), const),
                  pl.BlockSpec((1, LANES), const),
                  pl.BlockSpec((1, GD_HD), const),
                  pl.BlockSpec((bb, GD_V_HEADS, GD_HD, GD_HD), st4),
                  pl.BlockSpec((bb, 8, GD_CONV_CH), lambda b, c: (b, 0, 0))],
        out_specs=[pl.BlockSpec((bb, L, GD_V_W), tok),
                   pl.BlockSpec((bb, GD_V_HEADS, GD_HD, GD_HD), st4)],
        out_shape=[SDS((B, T, GD_V_W), f32), SDS((B, GD_V_HEADS, GD_HD, GD_HD), f32)],
        scratch_shapes=[pltpu.VMEM((bb, 8, GD_CONV_CH), f32)],
        compiler_params=_cparams("parallel", "arbitrary"),
        name="gdn",
    )(qkv.reshape(B, T, GD_CONV_CH), zg.reshape(B, T, GD_V_W), ba.reshape(B, T, LANES), conv_w, alog_row, dt_row,
      norm_row, s0, conv8)
    return o.reshape(n, GD_V_W), s_out


def _pad_lanes(row, offset=0):
    return jnp.zeros((1, LANES), f32).at[0, offset:offset + row.shape[0]].set(row.astype(f32))


def _rope_tables(pos):
    half = SW_HD // 2
    inv = ROPE_THETA ** (-jnp.arange(half, dtype=f32) / half)
    ang = pos.astype(f32)[:, None] * inv[None, :]
    cos, sin = jnp.cos(ang), jnp.sin(ang)
    cos_t = jnp.concatenate([cos, cos] * SW_KV_HEADS, axis=-1)
    sin_t = jnp.concatenate([-sin, sin] * SW_KV_HEADS, axis=-1)
    return cos_t, sin_t


def _tile(n, pref):
    return pref if n % pref == 0 else n


def _trunk(x, pos, L, state, p):
    B, T, _ = x.shape
    n = B * T
    x2 = x.reshape(n, D_MODEL)
    tm = _tile(n, 1024)
    precise = state is not None
    ab_w, wo_h, wo_a = (p["ab_w32"], p["ab_wo_h32"], p["ab_wo_a32"]) if precise else (p["ab_w"], p["ab_wo_h"], p["ab_wo_a"])

    z_ml, z_sw, z_g = _proj(x2, ab_w, ((0, ML_W), (ML_W, SW_W), (ML_W + SW_W, LANES)), tm)
    if state is None:
        h_ml, ml_c, ml_n, ml_m = _mlstm_pairs(z_ml, z_g, p["ab_bias"], p["ab_norm"], B, T, L)
    else:
        cn0 = jnp.concatenate([state["ml_C"], state["ml_n"][..., None],
                               jnp.zeros((B, ML_HEADS, ML_DK, LANES - ML_DV - 1), f32)], axis=-1)
        m0 = jnp.zeros((B, 1, LANES), f32).at[:, 0, :ML_HEADS].set(state["ml_m"])
        h_ml, cn, m_out = _mlstm(z_ml, z_g, p["ab_bias"], p["ab_norm"], cn0, m0, B, T, L, precise)
        ml_c, ml_n, ml_m = cn[..., :ML_DV], cn[..., ML_DV], m_out[:, 0, :ML_HEADS]
    h_ml = h_ml.reshape(n, ML_HEADS * ML_DV)
    cos_t, sin_t = _rope_tables(pos)
    if state is None:
        a_sw, k_rot = _swa_prompt(z_sw, cos_t, sin_t, p["ab_sinks"], B, T, L)
    else:
        a_sw, k_rot = _swa_sample(z_sw, state["sw_k"].reshape(B, WINDOW, LANES),
                                  state["sw_v"].reshape(B, WINDOW, LANES), cos_t, sin_t, p["ab_sinks"], B, T)
    keep = min(T, WINDOW)
    new_k = k_rot.reshape(B, T, LANES)[:, T - keep:].reshape(B, keep, SW_KV_HEADS, SW_HD)
    new_v = z_sw.reshape(B, T, SW_W)[:, T - keep:, SW_HEADS * SW_HD + LANES:].reshape(B, keep, SW_KV_HEADS, SW_HD)
    x2, gates_t, route = _out_ln(x2, [h_ml, a_sw], [wo_h, wo_a], p["ln_g"][0][0], p["ln_b"][0][0],
                                 p["router_wt"], p["router_b"], tm)
    x2, spare = _moe_block(x2, gates_t, route, p, 0)

    tm1 = _tile(n, 256)
    if state is None:
        s0 = jnp.zeros((B, GD_V_HEADS, GD_HD, GD_HD), f32)
        conv8 = jnp.zeros((B, 8, GD_CONV_CH), f32)
    else:
        s0 = state["gd_S"]
        conv8 = jnp.concatenate([jnp.zeros((B, 8 - (GD_CONV - 1), GD_CONV_CH), f32), state["gd_conv"]], axis=1)
    qkv, zg, ba = _proj(x2, p["c_w"], ((0, GD_CONV_CH), (GD_CONV_CH, GD_V_W), (GD_CONV_CH + GD_V_W, LANES)), tm1)
    new_conv = qkv.reshape(B, T, GD_CONV_CH)[:, T - (GD_CONV - 1):]
    o_gd, s_out = _gdn(qkv, zg, ba, p["c_conv_w"], p["c_alog"], p["c_dt"], p["c_norm"], s0, conv8, B, T, L)
    x2, gates_t, route = _out_ln(x2, [o_gd], [p["c_wo"]], p["ln_g"][1][0], p["ln_b"][1][0],
                                 p["router_wt"], p["router_b"], tm)
    x2, _ = _moe_block(x2, gates_t, route, p, 1, spare)

    outs = (new_k[None], new_v[None], ml_c[None], ml_n[None], ml_m[None],
            s_out[None], new_conv[None])
    return x2.reshape(B, T, D_MODEL), outs


def _moe_block(x2, gates_t, route, p, layer, spare=None):
    n = x2.shape[0]
    wg, wu, wd = p["ex_gate"], p["ex_up"], p["ex_down"]
    base = layer * N_EXPERTS
    g_row, b_row = p["ln_g"][layer][1], p["ln_b"][layer][1]
    if n < N_PAIRS * MOE_TM:
        return _moe_ln(x2, gates_t.T, wg, wu, wd, base, g_row, b_row, _tile(n, 1024)), None
    row_of_token, tile_a, tile_b, tile_valid = _pair_plan(route, n, MOE_TM)
    xs = _scatter_rows(x2, row_of_token, n + N_PAIRS * MOE_TM, MOE_DMA_ROWS, spare)
    ys = _pair_experts(xs, p["router_wt"], tile_a, tile_b, tile_valid, wg, wu, wd, base, g_row, b_row, MOE_TM)
    return _gather_rows(ys, row_of_token, MOE_DMA_ROWS), ys


def kernel(x_prompt, x_sample, cache_swa_k, cache_swa_v, state_mlstm_C, state_mlstm_n, state_mlstm_m, state_gdn_S, state_gdn_conv, ab_w_in, ab_b_i, ab_b_f, ab_norm, ab_sinks, ab_w_out, c_w_in, c_conv_w, c_a_log, c_dt_bias, c_norm, c_w_out, ln_g, ln_b, router_w, router_b, ex_gate, ex_up, ex_down):
    gate_lo = ML_W
    sw_lo = ML_W + 2 * ML_HEADS
    w0 = ab_w_in[0]
    ab_w32 = jnp.concatenate([w0[:, :gate_lo], w0[:, sw_lo:], w0[:, gate_lo:sw_lo],
                              jnp.zeros((D_MODEL, LANES - 2 * ML_HEADS), f32)], axis=1)
    w1 = c_w_in[0]
    c_w = jnp.concatenate([w1, jnp.zeros((D_MODEL, LANES - 2 * GD_V_HEADS), f32)], axis=1).astype(bf16)
    wo32 = ab_w_out[0]
    wo = wo32.astype(bf16)
    p = {
        "ab_w": ab_w32.astype(bf16),
        "ab_w32": ab_w32,
        "ab_wo_h32": wo32[:ML_HEADS * ML_DV],
        "ab_wo_a32": wo32[ML_HEADS * ML_DV:],
        "ab_bias": _pad_lanes(jnp.concatenate([ab_b_i[0], ab_b_f[0]])),
        "ab_norm": ab_norm[0].reshape(1, ML_HEADS * ML_DV),
        "ab_sinks": _pad_lanes(ab_sinks[0]),
        "ab_wo_h": wo[:ML_HEADS * ML_DV],
        "ab_wo_a": wo[ML_HEADS * ML_DV:],
        "c_w": c_w,
        "c_conv_w": c_conv_w[0],
        "c_alog": _pad_lanes(c_a_log[0], GD_V_HEADS),
        "c_dt": _pad_lanes(c_dt_bias[0], GD_V_HEADS),
        "c_norm": c_norm[0].reshape(1, GD_HD),
        "c_wo": c_w_out[0].astype(bf16),
        "ln_g": [[ln_g[i, j].reshape(1, D_MODEL) for j in range(2)] for i in range(DEPTH)],
        "ln_b": [[ln_b[i, j].reshape(1, D_MODEL) for j in range(2)] for i in range(DEPTH)],
        "router_wt": router_w.T,
        "router_b": router_b.reshape(N_EXPERTS, 1),
        "ex_gate": ex_gate.reshape(DEPTH * N_EXPERTS, D_MODEL, D_EXPERT),
        "ex_up": ex_up.reshape(DEPTH * N_EXPERTS, D_MODEL, D_EXPERT),
        "ex_down": ex_down.reshape(DEPTH * N_EXPERTS, D_EXPERT, D_MODEL),
    }
    t_p = x_prompt.shape[1]
    y_p, st_p = _trunk(x_prompt, jnp.arange(t_p, dtype=jnp.int32), CHUNK, None, p)
    t_s = x_sample.shape[1]
    state = {"sw_k": cache_swa_k[0], "sw_v": cache_swa_v[0], "ml_C": state_mlstm_C[0], "ml_n": state_mlstm_n[0],
             "ml_m": state_mlstm_m[0], "gd_S": state_gdn_S[0], "gd_conv": state_gdn_conv[0]}
    y_s, st_s = _trunk(x_sample, PAST_LEN + jnp.arange(t_s, dtype=jnp.int32), t_s, state, p)
    return (y_p, y_s) + st_p + st_s
```
